```python
import jax, jax.numpy as jnp
from jax import lax
import numpy as np

D_MODEL = 1024
BATCH = 32
SEQ = 2048
DEPTH = 2

HEAD_DIM = D_MODEL // 16
ATTN_HEADS = 6
ATTN_WIDTH = ATTN_HEADS * HEAD_DIM
ATTN_PATTERNS = ((128, 1), (512, 4), (2048, 16))
HEADS_PER_PATTERN = ATTN_HEADS // len(ATTN_PATTERNS)
CONV_GROUPS = 4
CONV_WIDTH = CONV_GROUPS * HEAD_DIM
CONV_TAPS = 31
LRU_HEADS = 6
LRU_WIDTH = LRU_HEADS * HEAD_DIM
LRU_CONV_TAPS = 4
LRU_C = 8.0
MIX_WIDTH = ATTN_WIDTH + CONV_WIDTH + LRU_WIDTH
D_FF = 4 * D_MODEL
RMS_EPS = 1e-6
LN_EPS = 1e-5

Q0 = 0
K0 = Q0 + ATTN_WIDTH
V0 = K0 + ATTN_WIDTH
CA0 = V0 + ATTN_WIDTH
CG0 = CA0 + CONV_WIDTH
LG0 = CG0 + CONV_WIDTH
LX0 = LG0 + LRU_WIDTH
IN_COLS = LX0 + LRU_WIDTH

kernel_name = "hymba_style_conv_dilattn_rglru_block"


def rmsnorm(x, g):
    xf = x.astype(jnp.float32)
    y = xf * lax.rsqrt(jnp.mean(xf * xf, axis=-1, keepdims=True) + RMS_EPS)
    return y.astype(x.dtype) * g


def layernorm(x, g, b):
    xf = x.astype(jnp.float32)
    mu = jnp.mean(xf, axis=-1, keepdims=True)
    var = jnp.mean(jnp.square(xf - mu), axis=-1, keepdims=True)
    y = (xf - mu) * lax.rsqrt(var + LN_EPS)
    return y.astype(x.dtype) * g + b


def alibi_slopes(n):
    return jnp.asarray(2.0 ** (-8.0 * np.arange(1, n + 1) / n), dtype=jnp.float32)


def causal_depthwise_conv(x, w, b):
    K, C = w.shape
    y = lax.conv_general_dilated(x, w[:, None, :], window_strides=(1,), padding=[(K - 1, 0)],
                                 dimension_numbers=("NWC", "WIO", "NWC"), feature_group_count=C)
    return y + b


def dilated_window_attention(q, k, v, slopes, window, dilation):
    B, S, H, Dh = q.shape
    W = window // dilation
    L = S // dilation
    nb = -(-L // W)
    Lp = nb * W

    def to_blocks(t):
        t = t.reshape(B, L, dilation, H, Dh).transpose(0, 2, 3, 1, 4)
        t = jnp.pad(t, ((0, 0), (0, 0), (0, 0), (0, Lp - L), (0, 0)))
        return t.reshape(B, dilation, H, nb, W, Dh)

    def with_prev(t):
        prev = jnp.pad(t, ((0, 0), (0, 0), (0, 0), (1, 0), (0, 0), (0, 0)))[:, :, :, :-1]
        return jnp.concatenate([prev, t], axis=4)

    qb = to_blocks(q)
    kb = with_prev(to_blocks(k))
    vb = with_prev(to_blocks(v))
    s = jnp.einsum("brhnqc,brhnkc->brhnqk", qb, kb).astype(jnp.float32) * (Dh ** -0.5)
    qi = jnp.arange(W)[:, None]
    kj = jnp.arange(2 * W)[None, :]
    dist = qi + W - kj
    band = (dist >= 0) & (dist <= W)
    has_prev = (jnp.arange(nb)[:, None, None] > 0) | (kj[None] >= W)
    valid = band[None] & has_prev
    bias = -slopes[:, None, None] * (dilation * dist).astype(jnp.float32)[None]
    s = jnp.where(valid[None, None, None], s + bias[None, None, :, None], -jnp.inf)
    m = jnp.max(s, axis=-1, keepdims=True)
    p = jnp.exp(s - m)
    l = jnp.sum(p, axis=-1, keepdims=True)
    o = jnp.einsum("brhnqk,brhnkc->brhnqc", p, vb.astype(jnp.float32)) / l
    lse = (m + jnp.log(l))[..., 0]
    o = o.reshape(B, dilation, H, Lp, Dh)[:, :, :, :L].transpose(0, 3, 1, 2, 4).reshape(B, S, H, Dh)
    lse = lse.reshape(B, dilation, H, Lp)[..., :L].transpose(0, 3, 1, 2).reshape(B, S, H)
    return o, lse


def attention_mixer(q, k, v):
    B, S = q.shape[:2]
    slopes = alibi_slopes(ATTN_HEADS)
    outs, lses = [], []
    for g, (window, dilation) in enumerate(ATTN_PATTERNS):
        sl = slice(g * HEADS_PER_PATTERN, (g + 1) * HEADS_PER_PATTERN)
        o, lse = dilated_window_attention(q[:, :, sl], k[:, :, sl], v[:, :, sl], slopes[sl], window, dilation)
        outs.append(o)
        lses.append(lse)
    o = jnp.stack(outs, axis=2)
    lse = jnp.stack(lses, axis=2)
    alpha = jax.nn.softmax(lse, axis=2)
    return (o * alpha[..., None]).reshape(B, S, ATTN_WIDTH).astype(q.dtype)


def conv_module(a, gate, dw_w, dw_b, ln_g, ln_b):
    u = a * jax.nn.sigmoid(gate)
    u = causal_depthwise_conv(u, dw_w, dw_b)
    u = layernorm(u, ln_g, ln_b)
    return jax.nn.silu(u)


def rg_lru(x, w_a, b_a, w_x, b_x, lam):
    B, S, C = x.shape
    xh = x.reshape(B, S, LRU_HEADS, C // LRU_HEADS)
    r = jax.nn.sigmoid(jnp.einsum("bshi,hio->bsho", xh, w_a).reshape(B, S, C) + b_a)
    i = jax.nn.sigmoid(jnp.einsum("bshi,hio->bsho", xh, w_x).reshape(B, S, C) + b_x)
    log_a = -LRU_C * r.astype(jnp.float32) * jax.nn.softplus(-lam.astype(jnp.float32))
    a = jnp.exp(log_a)
    b = jnp.sqrt(-jnp.expm1(2.0 * log_a)) * (i * x).astype(jnp.float32)

    def combine(left, right):
        a1, b1 = left
        a2, b2 = right
        return a1 * a2, a2 * b1 + b2

    _, h = lax.associative_scan(combine, (a, b), axis=1)
    return h.astype(x.dtype)


def recurrent_mixer(gate, xr, cw, cb, w_a, b_a, w_x, b_x, lam):
    u = causal_depthwise_conv(xr, cw, cb)
    return jax.nn.gelu(gate) * rg_lru(u, w_a, b_a, w_x, b_x, lam)


def hybrid_layer(x, norm1_g, w_in, conv_dw_w, conv_dw_b, conv_ln_g, conv_ln_b,
                 lru_conv_w, lru_conv_b, lru_wa, lru_ba, lru_wx, lru_bx, lru_lambda,
                 w_out, norm2_g, w_up, w_down):
    B, S, _ = x.shape
    h = rmsnorm(x, norm1_g)
    z = jnp.einsum("bsd,dc->bsc", h, w_in)
    q = z[..., Q0:K0].reshape(B, S, ATTN_HEADS, HEAD_DIM)
    k = z[..., K0:V0].reshape(B, S, ATTN_HEADS, HEAD_DIM)
    v = z[..., V0:CA0].reshape(B, S, ATTN_HEADS, HEAD_DIM)
    y_attn = attention_mixer(q, k, v)
    y_conv = conv_module(z[..., CA0:CG0], z[..., CG0:LG0], conv_dw_w, conv_dw_b, conv_ln_g, conv_ln_b)
    y_lru = recurrent_mixer(z[..., LG0:LX0], z[..., LX0:IN_COLS], lru_conv_w, lru_conv_b,
                            lru_wa, lru_ba, lru_wx, lru_bx, lru_lambda)
    mix = jnp.concatenate([y_attn, y_conv, y_lru], axis=-1)
    x = x + jnp.einsum("bsc,cd->bsd", mix, w_out)
    h2 = rmsnorm(x, norm2_g)
    ff = jnp.square(jax.nn.relu(jnp.einsum("bsd,df->bsf", h2, w_up)))
    return x + jnp.einsum("bsf,fd->bsd", ff, w_down)


def _fwd_setup_inputs(seed: int = 0) -> dict:
    key = jax.random.key(seed)
    ks = jax.random.split(key, 20)
    n = jax.random.normal
    f32 = jnp.float32
    x = n(ks[0], (BATCH, SEQ, D_MODEL), f32)
    norm1_g = 1.0 + 0.02 * n(ks[1], (DEPTH, D_MODEL), f32)
    w_in = n(ks[2], (DEPTH, D_MODEL, IN_COLS), f32) * D_MODEL ** -0.5
    conv_dw_w = n(ks[3], (DEPTH, CONV_TAPS, CONV_WIDTH), f32) * CONV_TAPS ** -0.5
    conv_dw_b = 0.02 * n(ks[4], (DEPTH, CONV_WIDTH), f32)
    conv_ln_g = 1.0 + 0.02 * n(ks[5], (DEPTH, CONV_WIDTH), f32)
    conv_ln_b = 0.02 * n(ks[6], (DEPTH, CONV_WIDTH), f32)
    lru_conv_w = n(ks[7], (DEPTH, LRU_CONV_TAPS, LRU_WIDTH), f32) * LRU_CONV_TAPS ** -0.5
    lru_conv_b = 0.02 * n(ks[8], (DEPTH, LRU_WIDTH), f32)
    blk = LRU_WIDTH // LRU_HEADS
    lru_wa = n(ks[9], (DEPTH, LRU_HEADS, blk, blk), f32) * blk ** -0.5
    lru_ba = 0.02 * n(ks[10], (DEPTH, LRU_WIDTH), f32)
    lru_wx = n(ks[11], (DEPTH, LRU_HEADS, blk, blk), f32) * blk ** -0.5
    lru_bx = 0.02 * n(ks[12], (DEPTH, LRU_WIDTH), f32)
    a_c = jax.random.uniform(ks[13], (DEPTH, LRU_WIDTH), f32, 0.9, 0.999)
    a0 = a_c ** (1.0 / LRU_C)
    lru_lambda = jnp.log(a0) - jnp.log1p(-a0)
    w_out = n(ks[14], (DEPTH, MIX_WIDTH, D_MODEL), f32) * MIX_WIDTH ** -0.5
    norm2_g = 1.0 + 0.02 * n(ks[15], (DEPTH, D_MODEL), f32)
    w_up = n(ks[16], (DEPTH, D_MODEL, D_FF), f32) * D_MODEL ** -0.5
    w_down = n(ks[17], (DEPTH, D_FF, D_MODEL), f32) * D_FF ** -0.5
    final_g = 1.0 + 0.02 * n(ks[18], (D_MODEL,), f32)
    return {"x": x, "norm1_g": norm1_g, "w_in": w_in, "conv_dw_w": conv_dw_w, "conv_dw_b": conv_dw_b,
            "conv_ln_g": conv_ln_g, "conv_ln_b": conv_ln_b, "lru_conv_w": lru_conv_w, "lru_conv_b": lru_conv_b,
            "lru_wa": lru_wa, "lru_ba": lru_ba, "lru_wx": lru_wx, "lru_bx": lru_bx, "lru_lambda": lru_lambda,
            "w_out": w_out, "norm2_g": norm2_g, "w_up": w_up, "w_down": w_down, "final_g": final_g}


def _fwd_reference(x, norm1_g, w_in, conv_dw_w, conv_dw_b, conv_ln_g, conv_ln_b, lru_conv_w, lru_conv_b,
              lru_wa, lru_ba, lru_wx, lru_bx, lru_lambda, w_out, norm2_g, w_up, w_down, final_g):
    for l in range(DEPTH):
        x = hybrid_layer(x, norm1_g[l], w_in[l], conv_dw_w[l], conv_dw_b[l], conv_ln_g[l], conv_ln_b[l],
                         lru_conv_w[l], lru_conv_b[l], lru_wa[l], lru_ba[l], lru_wx[l], lru_bx[l], lru_lambda[l],
                         w_out[l], norm2_g[l], w_up[l], w_down[l])
    return rmsnorm(x, final_g)


import jax as _jax
import jax.numpy as _jnp

TWIN_FORMAT = 'train_step'
FWD_PARAMS = ['x', 'norm1_g', 'w_in', 'conv_dw_w', 'conv_dw_b', 'conv_ln_g', 'conv_ln_b', 'lru_conv_w', 'lru_conv_b', 'lru_wa', 'lru_ba', 'lru_wx', 'lru_bx', 'lru_lambda', 'w_out', 'norm2_g', 'w_up', 'w_down', 'final_g']
TWIN_WEIGHTS = ['norm1_g', 'w_in', 'conv_dw_w', 'conv_dw_b', 'conv_ln_g', 'conv_ln_b', 'lru_conv_w', 'lru_conv_b', 'lru_wa', 'lru_ba', 'lru_wx', 'lru_bx', 'lru_lambda', 'w_out', 'norm2_g', 'w_up', 'w_down', 'final_g']
TWIN_DIFF_INPUT = 'x'
TWIN_INPUTS = ['x', 'norm1_g', 'w_in', 'conv_dw_w', 'conv_dw_b', 'conv_ln_g', 'conv_ln_b', 'lru_conv_w', 'lru_conv_b', 'lru_wa', 'lru_ba', 'lru_wx', 'lru_bx', 'lru_lambda', 'w_out', 'norm2_g', 'w_up', 'w_down', 'final_g', 'loss_target', 'm_norm1_g', 'm_w_in', 'm_conv_dw_w', 'm_conv_dw_b', 'm_conv_ln_g', 'm_conv_ln_b', 'm_lru_conv_w', 'm_lru_conv_b', 'm_lru_wa', 'm_lru_ba', 'm_lru_wx', 'm_lru_bx', 'm_lru_lambda', 'm_w_out', 'm_norm2_g', 'm_w_up', 'm_w_down', 'm_final_g', 'v_norm1_g', 'v_w_in', 'v_conv_dw_w', 'v_conv_dw_b', 'v_conv_ln_g', 'v_conv_ln_b', 'v_lru_conv_w', 'v_lru_conv_b', 'v_lru_wa', 'v_lru_ba', 'v_lru_wx', 'v_lru_bx', 'v_lru_lambda', 'v_w_out', 'v_norm2_g', 'v_w_up', 'v_w_down', 'v_final_g']
TWIN_OUTPUTS = ['loss', 'grad_x', 'grad_norm1_g', 'grad_w_in', 'grad_conv_dw_w', 'grad_conv_dw_b', 'grad_conv_ln_g', 'grad_conv_ln_b', 'grad_lru_conv_w', 'grad_lru_conv_b', 'grad_lru_wa', 'grad_lru_ba', 'grad_lru_wx', 'grad_lru_bx', 'grad_lru_lambda', 'grad_w_out', 'grad_norm2_g', 'grad_w_up', 'grad_w_down', 'grad_final_g', 'delta_norm1_g', 'delta_w_in', 'delta_conv_dw_w', 'delta_conv_dw_b', 'delta_conv_ln_g', 'delta_conv_ln_b', 'delta_lru_conv_w', 'delta_lru_conv_b', 'delta_lru_wa', 'delta_lru_ba', 'delta_lru_wx', 'delta_lru_bx', 'delta_lru_lambda', 'delta_w_out', 'delta_norm2_g', 'delta_w_up', 'delta_w_down', 'delta_final_g', 'new_m_norm1_g', 'new_m_w_in', 'new_m_conv_dw_w', 'new_m_conv_dw_b', 'new_m_conv_ln_g', 'new_m_conv_ln_b', 'new_m_lru_conv_w', 'new_m_lru_conv_b', 'new_m_lru_wa', 'new_m_lru_ba', 'new_m_lru_wx', 'new_m_lru_bx', 'new_m_lru_lambda', 'new_m_w_out', 'new_m_norm2_g', 'new_m_w_up', 'new_m_w_down', 'new_m_final_g', 'new_v_norm1_g', 'new_v_w_in', 'new_v_conv_dw_w', 'new_v_conv_dw_b', 'new_v_conv_ln_g', 'new_v_conv_ln_b', 'new_v_lru_conv_w', 'new_v_lru_conv_b', 'new_v_lru_wa', 'new_v_lru_ba', 'new_v_lru_wx', 'new_v_lru_bx', 'new_v_lru_lambda', 'new_v_w_out', 'new_v_norm2_g', 'new_v_w_up', 'new_v_w_down', 'new_v_final_g']
TWIN_LEAF_KINDS = {'loss': 'loss', 'grad_x': 'grad_x', 'grad_norm1_g': 'grad_w', 'grad_w_in': 'grad_w', 'grad_conv_dw_w': 'grad_w', 'grad_conv_dw_b': 'grad_w', 'grad_conv_ln_g': 'grad_w', 'grad_conv_ln_b': 'grad_w', 'grad_lru_conv_w': 'grad_w', 'grad_lru_conv_b': 'grad_w', 'grad_lru_wa': 'grad_w', 'grad_lru_ba': 'grad_w', 'grad_lru_wx': 'grad_w', 'grad_lru_bx': 'grad_w', 'grad_lru_lambda': 'grad_w', 'grad_w_out': 'grad_w', 'grad_norm2_g': 'grad_w', 'grad_w_up': 'grad_w', 'grad_w_down': 'grad_w', 'grad_final_g': 'grad_w', 'delta_norm1_g': 'delta_w', 'delta_w_in': 'delta_w', 'delta_conv_dw_w': 'delta_w', 'delta_conv_dw_b': 'delta_w', 'delta_conv_ln_g': 'delta_w', 'delta_conv_ln_b': 'delta_w', 'delta_lru_conv_w': 'delta_w', 'delta_lru_conv_b': 'delta_w', 'delta_lru_wa': 'delta_w', 'delta_lru_ba': 'delta_w', 'delta_lru_wx': 'delta_w', 'delta_lru_bx': 'delta_w', 'delta_lru_lambda': 'delta_w', 'delta_w_out': 'delta_w', 'delta_norm2_g': 'delta_w', 'delta_w_up': 'delta_w', 'delta_w_down': 'delta_w', 'delta_final_g': 'delta_w', 'new_m_norm1_g': 'new_m', 'new_m_w_in': 'new_m', 'new_m_conv_dw_w': 'new_m', 'new_m_conv_dw_b': 'new_m', 'new_m_conv_ln_g': 'new_m', 'new_m_conv_ln_b': 'new_m', 'new_m_lru_conv_w': 'new_m', 'new_m_lru_conv_b': 'new_m', 'new_m_lru_wa': 'new_m', 'new_m_lru_ba': 'new_m', 'new_m_lru_wx': 'new_m', 'new_m_lru_bx': 'new_m', 'new_m_lru_lambda': 'new_m', 'new_m_w_out': 'new_m', 'new_m_norm2_g': 'new_m', 'new_m_w_up': 'new_m', 'new_m_w_down': 'new_m', 'new_m_final_g': 'new_m', 'new_v_norm1_g': 'new_v', 'new_v_w_in': 'new_v', 'new_v_conv_dw_w': 'new_v', 'new_v_conv_dw_b': 'new_v', 'new_v_conv_ln_g': 'new_v', 'new_v_conv_ln_b': 'new_v', 'new_v_lru_conv_w': 'new_v', 'new_v_lru_conv_b': 'new_v', 'new_v_lru_wa': 'new_v', 'new_v_lru_ba': 'new_v', 'new_v_lru_wx': 'new_v', 'new_v_lru_bx': 'new_v', 'new_v_lru_lambda': 'new_v', 'new_v_w_out': 'new_v', 'new_v_norm2_g': 'new_v', 'new_v_w_up': 'new_v', 'new_v_w_down': 'new_v', 'new_v_final_g': 'new_v'}


def _forward(args):
    return _fwd_reference(*[args[k] for k in FWD_PARAMS])


def _output_shape():
    out = _jax.eval_shape(lambda: _forward(_fwd_setup_inputs(0)))
    return out.shape, out.dtype

N_MICROBATCH = 1
ADAM_LR = 0.001
ADAM_B1 = 0.9
ADAM_B2 = 0.999
ADAM_EPS = 1e-08
ADAM_WD = 0.01
ADAM_STEP = 10
PER_EXAMPLE_BATCH_AXIS = {'x': 0, 'loss_target': 0}
SHARED_INPUTS = []
_WEIGHT_DTYPES = {'norm1_g': _jnp.float32, 'w_in': _jnp.float32, 'conv_dw_w': _jnp.float32, 'conv_dw_b': _jnp.float32, 'conv_ln_g': _jnp.float32, 'conv_ln_b': _jnp.float32, 'lru_conv_w': _jnp.float32, 'lru_conv_b': _jnp.float32, 'lru_wa': _jnp.float32, 'lru_ba': _jnp.float32, 'lru_wx': _jnp.float32, 'lru_bx': _jnp.float32, 'lru_lambda': _jnp.float32, 'w_out': _jnp.float32, 'norm2_g': _jnp.float32, 'w_up': _jnp.float32, 'w_down': _jnp.float32, 'final_g': _jnp.float32}
MOMENT_SCALE = {'norm1_g': 1.627271e-01, 'w_in': 1.146546e-01, 'conv_dw_w': 1.367984e-01, 'conv_dw_b': 3.287415e-01, 'conv_ln_g': 1.879486e-01, 'conv_ln_b': 2.059292e-01, 'lru_conv_w': 3.067214e-01, 'lru_conv_b': 1.266892e+00, 'lru_wa': 3.768573e-02, 'lru_ba': 4.283707e-02, 'lru_wx': 7.279694e-02, 'lru_bx': 9.703860e-02, 'lru_lambda': 1.030228e-01, 'w_out': 1.539320e-01, 'norm2_g': 2.089273e-01, 'w_up': 1.010945e-01, 'w_down': 2.153690e-01, 'final_g': 6.563636e+01}


def _to_microbatches(a, axis):
    t = _jnp.moveaxis(a, axis, 0)
    t = t.reshape((N_MICROBATCH, t.shape[0] // N_MICROBATCH) + t.shape[1:])
    return _jnp.moveaxis(t, 1, axis + 1)


def setup_inputs(seed: int = 0) -> dict:
    inp = _fwd_setup_inputs(seed)
    key = _jax.random.fold_in(_jax.random.key(seed), 7919)
    shape, _ = _output_shape()
    out = dict(inp)
    out["loss_target"] = _jax.random.normal(_jax.random.fold_in(key, 0), shape, _jnp.float32)
    for i, name in enumerate(TWIN_WEIGHTS):
        w = inp[name].astype(_jnp.float32)
        if MOMENT_SCALE is None:
            s = _jnp.sqrt(_jnp.mean(_jnp.square(w)) + 1e-30)
        else:
            s = MOMENT_SCALE[name]
        km, kv = _jax.random.split(_jax.random.fold_in(key, i + 1))
        out[name] = w
        out["m_" + name] = s * _jax.random.normal(km, w.shape, _jnp.float32)
        out["v_" + name] = (s * s) * _jax.random.uniform(kv, w.shape, _jnp.float32, 0.5, 1.5)
    if N_MICROBATCH > 1:
        for name, axis in PER_EXAMPLE_BATCH_AXIS.items():
            out[name] = _to_microbatches(out[name], axis)
    return {'x': out['x'], 'norm1_g': out['norm1_g'], 'w_in': out['w_in'], 'conv_dw_w': out['conv_dw_w'], 'conv_dw_b': out['conv_dw_b'], 'conv_ln_g': out['conv_ln_g'], 'conv_ln_b': out['conv_ln_b'], 'lru_conv_w': out['lru_conv_w'], 'lru_conv_b': out['lru_conv_b'], 'lru_wa': out['lru_wa'], 'lru_ba': out['lru_ba'], 'lru_wx': out['lru_wx'], 'lru_bx': out['lru_bx'], 'lru_lambda': out['lru_lambda'], 'w_out': out['w_out'], 'norm2_g': out['norm2_g'], 'w_up': out['w_up'], 'w_down': out['w_down'], 'final_g': out['final_g'], 'loss_target': out['loss_target'], 'm_norm1_g': out['m_norm1_g'], 'm_w_in': out['m_w_in'], 'm_conv_dw_w': out['m_conv_dw_w'], 'm_conv_dw_b': out['m_conv_dw_b'], 'm_conv_ln_g': out['m_conv_ln_g'], 'm_conv_ln_b': out['m_conv_ln_b'], 'm_lru_conv_w': out['m_lru_conv_w'], 'm_lru_conv_b': out['m_lru_conv_b'], 'm_lru_wa': out['m_lru_wa'], 'm_lru_ba': out['m_lru_ba'], 'm_lru_wx': out['m_lru_wx'], 'm_lru_bx': out['m_lru_bx'], 'm_lru_lambda': out['m_lru_lambda'], 'm_w_out': out['m_w_out'], 'm_norm2_g': out['m_norm2_g'], 'm_w_up': out['m_w_up'], 'm_w_down': out['m_w_down'], 'm_final_g': out['m_final_g'], 'v_norm1_g': out['v_norm1_g'], 'v_w_in': out['v_w_in'], 'v_conv_dw_w': out['v_conv_dw_w'], 'v_conv_dw_b': out['v_conv_dw_b'], 'v_conv_ln_g': out['v_conv_ln_g'], 'v_conv_ln_b': out['v_conv_ln_b'], 'v_lru_conv_w': out['v_lru_conv_w'], 'v_lru_conv_b': out['v_lru_conv_b'], 'v_lru_wa': out['v_lru_wa'], 'v_lru_ba': out['v_lru_ba'], 'v_lru_wx': out['v_lru_wx'], 'v_lru_bx': out['v_lru_bx'], 'v_lru_lambda': out['v_lru_lambda'], 'v_w_out': out['v_w_out'], 'v_norm2_g': out['v_norm2_g'], 'v_w_up': out['v_w_up'], 'v_w_down': out['v_w_down'], 'v_final_g': out['v_final_g']}


def _loss(weights, diff, rest, loss_target):
    with _jax.named_scope("forward"):
        args = {**rest, TWIN_DIFF_INPUT: diff, **{k: w.astype(_WEIGHT_DTYPES[k]) for k, w in weights.items()}}
        y = _forward(args)
    with _jax.named_scope("loss_head"):
        err = _jnp.square(y.astype(_jnp.float32) - loss_target)
        return 0.5 * _jnp.sum(_jnp.mean(err, axis=-1)) if err.ndim else 0.5 * err


def _adamw(w, g, m, v):
    m = ADAM_B1 * m + (1.0 - ADAM_B1) * g
    v = ADAM_B2 * v + (1.0 - ADAM_B2) * _jnp.square(g)
    m_hat = m / (1.0 - ADAM_B1 ** ADAM_STEP)
    v_hat = v / (1.0 - ADAM_B2 ** ADAM_STEP)
    delta = -ADAM_LR * (m_hat / (_jnp.sqrt(v_hat) + ADAM_EPS) + ADAM_WD * w)
    return delta, m, v


def reference(x, norm1_g, w_in, conv_dw_w, conv_dw_b, conv_ln_g, conv_ln_b, lru_conv_w, lru_conv_b, lru_wa, lru_ba, lru_wx, lru_bx, lru_lambda, w_out, norm2_g, w_up, w_down, final_g, loss_target, m_norm1_g, m_w_in, m_conv_dw_w, m_conv_dw_b, m_conv_ln_g, m_conv_ln_b, m_lru_conv_w, m_lru_conv_b, m_lru_wa, m_lru_ba, m_lru_wx, m_lru_bx, m_lru_lambda, m_w_out, m_norm2_g, m_w_up, m_w_down, m_final_g, v_norm1_g, v_w_in, v_conv_dw_w, v_conv_dw_b, v_conv_ln_g, v_conv_ln_b, v_lru_conv_w, v_lru_conv_b, v_lru_wa, v_lru_ba, v_lru_wx, v_lru_bx, v_lru_lambda, v_w_out, v_norm2_g, v_w_up, v_w_down, v_final_g):
    given = dict(x=x, norm1_g=norm1_g, w_in=w_in, conv_dw_w=conv_dw_w, conv_dw_b=conv_dw_b, conv_ln_g=conv_ln_g, conv_ln_b=conv_ln_b, lru_conv_w=lru_conv_w, lru_conv_b=lru_conv_b, lru_wa=lru_wa, lru_ba=lru_ba, lru_wx=lru_wx, lru_bx=lru_bx, lru_lambda=lru_lambda, w_out=w_out, norm2_g=norm2_g, w_up=w_up, w_down=w_down, final_g=final_g, loss_target=loss_target, m_norm1_g=m_norm1_g, m_w_in=m_w_in, m_conv_dw_w=m_conv_dw_w, m_conv_dw_b=m_conv_dw_b, m_conv_ln_g=m_conv_ln_g, m_conv_ln_b=m_conv_ln_b, m_lru_conv_w=m_lru_conv_w, m_lru_conv_b=m_lru_conv_b, m_lru_wa=m_lru_wa, m_lru_ba=m_lru_ba, m_lru_wx=m_lru_wx, m_lru_bx=m_lru_bx, m_lru_lambda=m_lru_lambda, m_w_out=m_w_out, m_norm2_g=m_norm2_g, m_w_up=m_w_up, m_w_down=m_w_down, m_final_g=m_final_g, v_norm1_g=v_norm1_g, v_w_in=v_w_in, v_conv_dw_w=v_conv_dw_w, v_conv_dw_b=v_conv_dw_b, v_conv_ln_g=v_conv_ln_g, v_conv_ln_b=v_conv_ln_b, v_lru_conv_w=v_lru_conv_w, v_lru_conv_b=v_lru_conv_b, v_lru_wa=v_lru_wa, v_lru_ba=v_lru_ba, v_lru_wx=v_lru_wx, v_lru_bx=v_lru_bx, v_lru_lambda=v_lru_lambda, v_w_out=v_w_out, v_norm2_g=v_norm2_g, v_w_up=v_w_up, v_w_down=v_w_down, v_final_g=v_final_g)
    weights = {n: given[n] for n in TWIN_WEIGHTS}
    shared = {n: given[n] for n in SHARED_INPUTS}
    per_example = {n: given[n] for n in ['x']}
    grad_fn = _jax.value_and_grad(_loss, argnums=(0, 1))

    def one_microbatch(ex, loss_target):
        ex = dict(ex)
        diff = ex.pop(TWIN_DIFF_INPUT)
        return grad_fn(weights, diff, {**shared, **ex}, loss_target)

    if N_MICROBATCH == 1:
        loss, (grad_w, grad_x) = one_microbatch(per_example, given["loss_target"])
    else:
        def body(carry, xs):
            loss_sum, grad_sum = carry
            l_k, (gw_k, gx_k) = one_microbatch(xs[0], xs[1])
            with _jax.named_scope("update"):
                return (loss_sum + l_k, _jax.tree.map(_jnp.add, grad_sum, gw_k)), gx_k

        init = (_jnp.zeros((), _jnp.float32), _jax.tree.map(_jnp.zeros_like, weights))
        (loss, grad_w), grad_x = _jax.lax.scan(body, init, (per_example, given["loss_target"]))
    with _jax.named_scope("update"):
        delta_w, new_m, new_v = {}, {}, {}
        for n in TWIN_WEIGHTS:
            delta_w[n], new_m[n], new_v[n] = _adamw(weights[n], grad_w[n], given["m_" + n], given["v_" + n])
    return (loss, grad_x, *[grad_w[n] for n in TWIN_WEIGHTS], *[delta_w[n] for n in TWIN_WEIGHTS],
            *[new_m[n] for n in TWIN_WEIGHTS], *[new_v[n] for n in TWIN_WEIGHTS])
```

```python
import functools
import math

import numpy as np
import jax
import jax.numpy as jnp
from jax import lax
from jax.experimental import pallas as pl
from jax.experimental.pallas import tpu as pltpu

F32 = jnp.float32
BF16 = jnp.bfloat16

D_MODEL = 1024
HEAD_DIM = 64
ATTN_W = 384
CONV_W = 256
CONV_TAPS = 31
LRU_W = 384
LRU_TAPS = 4
LRU_HEADS = 6
LRU_C = 8.0
QKV_W = 3 * ATTN_W
CONV_IN_W = 2 * CONV_W
LRU_IN_W = 2 * LRU_W
IN_COLS = QKV_W + CONV_IN_W + LRU_IN_W
D_FF = 4096
DEPTH = 2
N_DEV = 8
RMS_EPS = 1e-6
LN_EPS = 1e-5
ATTN_BLOCK = 128
ATTN_DILATIONS = (1, 4, 16)
N_UNITS = 16
NEG_BIG = -1e30

ADAM_LR = 0.001
ADAM_B1 = 0.9
ADAM_B2 = 0.999
ADAM_EPS = 1e-08
ADAM_WD = 0.01
ADAM_STEP = 10

VMEM_LIMIT = 56 * 1024 * 1024
ROW_TILE = 512
SEQ_CHUNK = 128


def _params(*sem):
    return pltpu.CompilerParams(dimension_semantics=sem if sem else None, vmem_limit_bytes=VMEM_LIMIT)


def _dot(a, b):
    return jnp.dot(a, b, preferred_element_type=F32)


def _dot_nt(a, b):
    return lax.dot_general(a, b, (((1,), (1,)), ((), ())), preferred_element_type=F32)


def _dot_tn(a, b):
    return lax.dot_general(a, b, (((0,), (0,)), ((), ())), preferred_element_type=F32)


def _rms_fwd(x, g):
    rstd = lax.rsqrt(jnp.mean(x * x, axis=-1, keepdims=True) + RMS_EPS)
    xhat = x * rstd
    return xhat * g, xhat, rstd


def _rms_bwd(dh, xhat, rstd, g):
    dxh = dh * g
    dx = rstd * (dxh - xhat * jnp.mean(dxh * xhat, axis=-1, keepdims=True))
    dg = jnp.sum(dh * xhat, axis=0, keepdims=True)
    return dx, dg


def _sigmoid(x):
    return 1.0 / (1.0 + jnp.exp(-x))


def _expm1(x):
    small = x * (1.0 + x * (0.5 + x * (1.0 / 6.0 + x * (1.0 / 24.0 + x * (1.0 / 120.0)))))
    return jnp.where(jnp.abs(x) < 0.05, small, jnp.exp(x) - 1.0)


def _log1p(z):
    w = 1.0 + z
    return jnp.where(w == 1.0, z, z * jnp.log(w) / jnp.where(w == 1.0, 1.0, w - 1.0))


def _softplus(x):
    return jnp.maximum(x, 0.0) + _log1p(jnp.exp(-jnp.abs(x)))


GELU_K = math.sqrt(2.0 / math.pi)


def _gelu(x):
    t = jnp.tanh(GELU_K * (x + 0.044715 * x * x * x))
    return 0.5 * x * (1.0 + t), t


def _gelu_grad(x, t):
    return 0.5 * (1.0 + t) + 0.5 * x * (1.0 - t * t) * GELU_K * (1.0 + 3.0 * 0.044715 * x * x)


def _inproj(x2d, g, w):
    T = x2d.shape[0]
    tm = ROW_TILE

    def body(x_ref, g_ref, w_ref, h_ref, qkv_ref, ci_ref, li_ref):
        h, _, _ = _rms_fwd(x_ref[...], g_ref[...])
        hb = h.astype(BF16)
        h_ref[...] = hb
        qkv_ref[...] = _dot(hb, w_ref[:, 0:QKV_W]).astype(BF16)
        ci_ref[...] = _dot(hb, w_ref[:, QKV_W:QKV_W + CONV_IN_W])
        li_ref[...] = _dot(hb, w_ref[:, QKV_W + CONV_IN_W:IN_COLS])

    return pl.pallas_call(
        body, name="inproj", grid=(T // tm,),
        in_specs=[pl.BlockSpec((tm, D_MODEL), lambda i: (i, 0)),
                  pl.BlockSpec((1, D_MODEL), lambda i: (0, 0)),
                  pl.BlockSpec((D_MODEL, IN_COLS), lambda i: (0, 0))],
        out_specs=[pl.BlockSpec((tm, D_MODEL), lambda i: (i, 0)),
                   pl.BlockSpec((tm, QKV_W), lambda i: (i, 0)),
                   pl.BlockSpec((tm, CONV_IN_W), lambda i: (i, 0)),
                   pl.BlockSpec((tm, LRU_IN_W), lambda i: (i, 0))],
        out_shape=[jax.ShapeDtypeStruct((T, D_MODEL), BF16), jax.ShapeDtypeStruct((T, QKV_W), BF16),
                   jax.ShapeDtypeStruct((T, CONV_IN_W), F32), jax.ShapeDtypeStruct((T, LRU_IN_W), F32)],
        compiler_params=_params("parallel"),
    )(x2d, g, w)


def _attn_alpha(lse):
    l0, l1, l2 = lse[:, 0:128], lse[:, 128:256], lse[:, 256:384]
    m = jnp.maximum(jnp.maximum(l0, l1), l2)
    e0, e1, e2 = jnp.exp(l0 - m), jnp.exp(l1 - m), jnp.exp(l2 - m)
    inv = 1.0 / (e0 + e1 + e2)
    return e0 * inv, e1 * inv, e2 * inv


def _outproj(x2d, o, lse, yc, yl, w):
    T = x2d.shape[0]
    tm = ROW_TILE

    def body(x_ref, o_ref, lse_ref, yc_ref, yl_ref, w_ref, x1_ref, mix_ref):
        al = _attn_alpha(lse_ref[...])
        for p in range(3):
            mix_ref[:, p * 128:(p + 1) * 128] = (o_ref[:, p * 128:(p + 1) * 128] * al[p]).astype(BF16)
        mix_ref[:, ATTN_W:ATTN_W + CONV_W] = yc_ref[...]
        mix_ref[:, ATTN_W + CONV_W:D_MODEL] = yl_ref[...]
        x1_ref[...] = x_ref[...] + _dot(mix_ref[...], w_ref[...])

    return pl.pallas_call(
        body, name="outproj", grid=(T // tm,),
        in_specs=[pl.BlockSpec((tm, D_MODEL), lambda i: (i, 0)),
                  pl.BlockSpec((tm, ATTN_W), lambda i: (i, 0)),
                  pl.BlockSpec((tm, ATTN_W), lambda i: (i, 0)),
                  pl.BlockSpec((tm, CONV_W), lambda i: (i, 0)),
                  pl.BlockSpec((tm, LRU_W), lambda i: (i, 0)),
                  pl.BlockSpec((D_MODEL, D_MODEL), lambda i: (0, 0))],
        out_specs=[pl.BlockSpec((tm, D_MODEL), lambda i: (i, 0)),
                   pl.BlockSpec((tm, D_MODEL), lambda i: (i, 0))],
        out_shape=[jax.ShapeDtypeStruct((T, D_MODEL), F32), jax.ShapeDtypeStruct((T, D_MODEL), BF16)],
        compiler_params=_params("parallel"),
    )(x2d, o, lse, yc, yl, w)


def _up(x1, g, w):
    T = x1.shape[0]
    tm, tn = ROW_TILE, 1024

    def body(x_ref, g_ref, w_ref, h_ref, pre_ref):
        @pl.when(pl.program_id(1) == 0)
        def _():
            h, _, _ = _rms_fwd(x_ref[...], g_ref[...])
            h_ref[...] = h.astype(BF16)

        pre_ref[...] = _dot(h_ref[...], w_ref[...])

    return pl.pallas_call(
        body, name="up", grid=(T // tm, D_FF // tn),
        in_specs=[pl.BlockSpec((tm, D_MODEL), lambda i, j: (i, 0)),
                  pl.BlockSpec((1, D_MODEL), lambda i, j: (0, 0)),
                  pl.BlockSpec((D_MODEL, tn), lambda i, j: (0, j))],
        out_specs=[pl.BlockSpec((tm, D_MODEL), lambda i, j: (i, 0)),
                   pl.BlockSpec((tm, tn), lambda i, j: (i, j))],
        out_shape=[jax.ShapeDtypeStruct((T, D_MODEL), BF16), jax.ShapeDtypeStruct((T, D_FF), F32)],
        compiler_params=_params("parallel", "arbitrary"),
    )(x1, g, w)


def _relu2(pre):
    r = jnp.maximum(pre, 0.0)
    return r * r


def _down(x1, pre, w):
    T = x1.shape[0]
    tm, tk = ROW_TILE, 1024

    def body(x_ref, pre_ref, w_ref, o_ref):
        part = _dot(_relu2(pre_ref[...]).astype(BF16), w_ref[...])

        @pl.when(pl.program_id(1) == 0)
        def _():
            o_ref[...] = x_ref[...] + part

        @pl.when(pl.program_id(1) != 0)
        def _():
            o_ref[...] += part

    return pl.pallas_call(
        body, name="down", grid=(T // tm, D_FF // tk),
        in_specs=[pl.BlockSpec((tm, D_MODEL), lambda i, k: (i, 0)),
                  pl.BlockSpec((tm, tk), lambda i, k: (i, k)),
                  pl.BlockSpec((tk, D_MODEL), lambda i, k: (k, 0))],
        out_specs=pl.BlockSpec((tm, D_MODEL), lambda i, k: (i, 0)),
        out_shape=jax.ShapeDtypeStruct((T, D_MODEL), F32),
        compiler_params=_params("parallel", "arbitrary"),
    )(x1, pre, w)


def _loss_head(x2, target, g):
    T = x2.shape[0]
    tm = ROW_TILE

    def body(x_ref, t_ref, g_ref, dx_ref, loss_ref, dg_ref):
        @pl.when(pl.program_id(0) == 0)
        def _():
            loss_ref[...] = jnp.zeros_like(loss_ref)
            dg_ref[...] = jnp.zeros_like(dg_ref)

        gv = g_ref[...]
        y, xhat, rstd = _rms_fwd(x_ref[...], gv)
        err = y - t_ref[...]
        loss_ref[...] += 0.5 * jnp.sum(jnp.mean(err * err, axis=-1, keepdims=True))
        dy = err * (1.0 / D_MODEL)
        dx, dg = _rms_bwd(dy, xhat, rstd, gv)
        dx_ref[...] = dx
        dg_ref[...] += dg

    return pl.pallas_call(
        body, name="loss_head", grid=(T // tm,),
        in_specs=[pl.BlockSpec((tm, D_MODEL), lambda i: (i, 0)),
                  pl.BlockSpec((tm, D_MODEL), lambda i: (i, 0)),
                  pl.BlockSpec((1, D_MODEL), lambda i: (0, 0))],
        out_specs=[pl.BlockSpec((tm, D_MODEL), lambda i: (i, 0)),
                   pl.BlockSpec((8, 128), lambda i: (0, 0)),
                   pl.BlockSpec((1, D_MODEL), lambda i: (0, 0))],
        out_shape=[jax.ShapeDtypeStruct((T, D_MODEL), F32), jax.ShapeDtypeStruct((8, 128), F32),
                   jax.ShapeDtypeStruct((1, D_MODEL), F32)],
        compiler_params=_params("arbitrary"),
    )(x2, target, g)


def _down_bwd_act(dx2, pre, w):
    T = dx2.shape[0]
    tm, tn = ROW_TILE, 1024

    def body(dx_ref, pre_ref, w_ref, o_ref):
        dff = _dot_nt(dx_ref[...].astype(BF16), w_ref[...])
        o_ref[...] = (dff * (2.0 * jnp.maximum(pre_ref[...], 0.0))).astype(BF16)

    return pl.pallas_call(
        body, name="down_bwd_act", grid=(T // tm, D_FF // tn),
        in_specs=[pl.BlockSpec((tm, D_MODEL), lambda i, j: (i, 0)),
                  pl.BlockSpec((tm, tn), lambda i, j: (i, j)),
                  pl.BlockSpec((tn, D_MODEL), lambda i, j: (j, 0))],
        out_specs=pl.BlockSpec((tm, tn), lambda i, j: (i, j)),
        out_shape=jax.ShapeDtypeStruct((T, D_FF), BF16),
        compiler_params=_params("parallel", "parallel"),
    )(dx2, pre, w)


def _down_bwd_w(pre, dx2):
    T = dx2.shape[0]
    tm, tk = 1024, ROW_TILE
    nk = T // tk

    def body(pre_ref, dx_ref, o_ref, acc_ref):
        part = _dot_tn(_relu2(pre_ref[...]).astype(BF16), dx_ref[...].astype(BF16))

        @pl.when(pl.program_id(1) == 0)
        def _():
            acc_ref[...] = part

        @pl.when(pl.program_id(1) != 0)
        def _():
            acc_ref[...] += part

        @pl.when(pl.program_id(1) == nk - 1)
        def _():
            o_ref[...] = acc_ref[...].astype(BF16)

    return pl.pallas_call(
        body, name="down_bwd_w", grid=(D_FF // tm, nk),
        in_specs=[pl.BlockSpec((tk, tm), lambda i, k: (k, i)),
                  pl.BlockSpec((tk, D_MODEL), lambda i, k: (k, 0))],
        out_specs=pl.BlockSpec((tm, D_MODEL), lambda i, k: (i, 0)),
        out_shape=jax.ShapeDtypeStruct((D_FF, D_MODEL), BF16),
        scratch_shapes=[pltpu.VMEM((tm, D_MODEL), F32)],
        compiler_params=_params("parallel", "arbitrary"),
    )(pre, dx2)


def _up_bwd_act(dpre, w, x1, g, dx2):
    T = dx2.shape[0]
    tm, tk = ROW_TILE, 1024
    nk = D_FF // tk

    def body(dp_ref, w_ref, x_ref, g_ref, dx2_ref, dx1_ref, dg_ref, acc_ref):
        i, k = pl.program_id(0), pl.program_id(1)
        part = _dot_nt(dp_ref[...], w_ref[...])

        @pl.when(k == 0)
        def _():
            acc_ref[...] = part

        @pl.when(k != 0)
        def _():
            acc_ref[...] += part

        @pl.when((i == 0) & (k == 0))
        def _():
            dg_ref[...] = jnp.zeros_like(dg_ref)

        @pl.when(k == nk - 1)
        def _():
            gv = g_ref[...]
            _, xhat, rstd = _rms_fwd(x_ref[...], gv)
            dx, dg = _rms_bwd(acc_ref[...], xhat, rstd, gv)
            dx1_ref[...] = dx2_ref[...] + dx
            dg_ref[...] += dg

    return pl.pallas_call(
        body, name="up_bwd_act", grid=(T // tm, nk),
        in_specs=[pl.BlockSpec((tm, tk), lambda i, k: (i, k)),
                  pl.BlockSpec((D_MODEL, tk), lambda i, k: (0, k)),
                  pl.BlockSpec((tm, D_MODEL), lambda i, k: (i, 0)),
                  pl.BlockSpec((1, D_MODEL), lambda i, k: (0, 0)),
                  pl.BlockSpec((tm, D_MODEL), lambda i, k: (i, 0))],
        out_specs=[pl.BlockSpec((tm, D_MODEL), lambda i, k: (i, 0)),
                   pl.BlockSpec((1, D_MODEL), lambda i, k: (0, 0))],
        out_shape=[jax.ShapeDtypeStruct((T, D_MODEL), F32), jax.ShapeDtypeStruct((1, D_MODEL), F32)],
        scratch_shapes=[pltpu.VMEM((tm, D_MODEL), F32)],
        compiler_params=_params("arbitrary", "arbitrary"),
    )(dpre, w, x1, g, dx2)


def _up_bwd_w(h2, dpre):
    T = h2.shape[0]
    tn, tk = 1024, ROW_TILE
    nk = T // tk

    def body(h_ref, dp_ref, o_ref, acc_ref):
        part = _dot_tn(h_ref[...], dp_ref[...])

        @pl.when(pl.program_id(1) == 0)
        def _():
            acc_ref[...] = part

        @pl.when(pl.program_id(1) != 0)
        def _():
            acc_ref[...] += part

        @pl.when(pl.program_id(1) == nk - 1)
        def _():
            o_ref[...] = acc_ref[...].astype(BF16)

    return pl.pallas_call(
        body, name="up_bwd_w", grid=(D_FF // tn, nk),
        in_specs=[pl.BlockSpec((tk, D_MODEL), lambda j, k: (k, 0)),
                  pl.BlockSpec((tk, tn), lambda j, k: (k, j))],
        out_specs=pl.BlockSpec((D_MODEL, tn), lambda j, k: (0, j)),
        out_shape=jax.ShapeDtypeStruct((D_MODEL, D_FF), BF16),
        scratch_shapes=[pltpu.VMEM((D_MODEL, tn), F32)],
        compiler_params=_params("parallel", "arbitrary"),
    )(h2, dpre)


def _outproj_bwd(dx1, mix, w):
    T = dx1.shape[0]
    tm = ROW_TILE
    nk = T // tm

    def body(dx_ref, mix_ref, w_ref, da_ref, dc_ref, dl_ref, dw_ref, acc_ref):
        i = pl.program_id(0)
        dxb = dx_ref[...].astype(BF16)
        dmix = _dot_nt(dxb, w_ref[...])
        da_ref[...] = dmix[:, 0:ATTN_W]
        dc_ref[...] = dmix[:, ATTN_W:ATTN_W + CONV_W]
        dl_ref[...] = dmix[:, ATTN_W + CONV_W:D_MODEL]
        part = _dot_tn(mix_ref[...], dxb)

        @pl.when(i == 0)
        def _():
            acc_ref[...] = part

        @pl.when(i != 0)
        def _():
            acc_ref[...] += part

        @pl.when(i == nk - 1)
        def _():
            dw_ref[...] = acc_ref[...].astype(BF16)

    return pl.pallas_call(
        body, name="outproj_bwd", grid=(nk,),
        in_specs=[pl.BlockSpec((tm, D_MODEL), lambda i: (i, 0)),
                  pl.BlockSpec((tm, D_MODEL), lambda i: (i, 0)),
                  pl.BlockSpec((D_MODEL, D_MODEL), lambda i: (0, 0))],
        out_specs=[pl.BlockSpec((tm, ATTN_W), lambda i: (i, 0)),
                   pl.BlockSpec((tm, CONV_W), lambda i: (i, 0)),
                   pl.BlockSpec((tm, LRU_W), lambda i: (i, 0)),
                   pl.BlockSpec((D_MODEL, D_MODEL), lambda i: (0, 0))],
        out_shape=[jax.ShapeDtypeStruct((T, ATTN_W), F32), jax.ShapeDtypeStruct((T, CONV_W), F32),
                   jax.ShapeDtypeStruct((T, LRU_W), F32), jax.ShapeDtypeStruct((D_MODEL, D_MODEL), BF16)],
        scratch_shapes=[pltpu.VMEM((D_MODEL, D_MODEL), F32)],
        compiler_params=_params("arbitrary"),
    )(dx1, mix, w)


def _inproj_bwd(dqkv, dci, dli, h, w, x2d, g, dx1):
    T = x2d.shape[0]
    tm = ROW_TILE
    nk = T // tm
    c1, c2 = QKV_W, QKV_W + CONV_IN_W

    def body(dq_ref, dc_ref, dl_ref, h_ref, w_ref, x_ref, g_ref, dx1_ref, dx_ref, dg_ref, dw_ref, acc_ref):
        i = pl.program_id(0)
        dq, dc, dl = dq_ref[...], dc_ref[...], dl_ref[...]
        dh = _dot_nt(dq, w_ref[:, 0:c1]) + _dot_nt(dc, w_ref[:, c1:c2]) + _dot_nt(dl, w_ref[:, c2:IN_COLS])
        gv = g_ref[...]
        _, xhat, rstd = _rms_fwd(x_ref[...], gv)
        dx, dg = _rms_bwd(dh, xhat, rstd, gv)
        dx_ref[...] = dx1_ref[...] + dx
        hb = h_ref[...]

        @pl.when(i == 0)
        def _():
            dg_ref[...] = dg
            acc_ref[:, 0:c1] = _dot_tn(hb, dq)
            acc_ref[:, c1:c2] = _dot_tn(hb, dc)
            acc_ref[:, c2:IN_COLS] = _dot_tn(hb, dl)

        @pl.when(i != 0)
        def _():
            dg_ref[...] += dg
            acc_ref[:, 0:c1] += _dot_tn(hb, dq)
            acc_ref[:, c1:c2] += _dot_tn(hb, dc)
            acc_ref[:, c2:IN_COLS] += _dot_tn(hb, dl)

        @pl.when(i == nk - 1)
        def _():
            dw_ref[...] = acc_ref[...].astype(BF16)

    return pl.pallas_call(
        body, name="inproj_bwd", grid=(nk,),
        in_specs=[pl.BlockSpec((tm, QKV_W), lambda i: (i, 0)),
                  pl.BlockSpec((tm, CONV_IN_W), lambda i: (i, 0)),
                  pl.BlockSpec((tm, LRU_IN_W), lambda i: (i, 0)),
                  pl.BlockSpec((tm, D_MODEL), lambda i: (i, 0)),
                  pl.BlockSpec((D_MODEL, IN_COLS), lambda i: (0, 0)),
                  pl.BlockSpec((tm, D_MODEL), lambda i: (i, 0)),
                  pl.BlockSpec((1, D_MODEL), lambda i: (0, 0)),
                  pl.BlockSpec((tm, D_MODEL), lambda i: (i, 0))],
        out_specs=[pl.BlockSpec((tm, D_MODEL), lambda i: (i, 0)),
                   pl.BlockSpec((1, D_MODEL), lambda i: (0, 0)),
                   pl.BlockSpec((D_MODEL, IN_COLS), lambda i: (0, 0))],
        out_shape=[jax.ShapeDtypeStruct((T, D_MODEL), F32), jax.ShapeDtypeStruct((1, D_MODEL), F32),
                   jax.ShapeDtypeStruct((D_MODEL, IN_COLS), BF16)],
        scratch_shapes=[pltpu.VMEM((D_MODEL, IN_COLS), F32)],
        compiler_params=_params("arbitrary"),
    )(dqkv, dci, dli, h, w, x2d, g, dx1)


def _to_units(a):
    B = a.shape[0]
    outs = []
    for p, d in enumerate(ATTN_DILATIONS):
        nb = N_UNITS // d
        t = a[:, :, p * 128:(p + 1) * 128].reshape(B, nb, ATTN_BLOCK, d, 128)
        outs.append(t.transpose(0, 3, 1, 2, 4).reshape(B, N_UNITS, ATTN_BLOCK, 128))
    return jnp.stack(outs, axis=1)


def _from_units(u):
    B = u.shape[0]
    outs = []
    for p, d in enumerate(ATTN_DILATIONS):
        nb = N_UNITS // d
        t = u[:, p].reshape(B, d, nb, ATTN_BLOCK, 128).transpose(0, 2, 3, 1, 4)
        outs.append(t.reshape(B, nb * ATTN_BLOCK * d, 128))
    return jnp.concatenate(outs, axis=-1)


def _alibi_coef():
    slopes = 2.0 ** (-8.0 * np.arange(1, 7) / 6)
    return jnp.asarray((slopes.reshape(3, 2) * np.asarray(ATTN_DILATIONS)[:, None]).astype(np.float32))


def _blocks_per_residue(p):
    return jnp.right_shift(N_UNITS, 2 * p)


def _unit_own(b, p, u):
    return (b, p, u, 0, 0)


def _unit_prev(b, p, u):
    has_prev = jnp.bitwise_and(u, _blocks_per_residue(p) - 1) != 0
    return (b, p, jnp.where(has_prev, u - 1, u), 0, 0)


def _unit_next(b, p, u):
    has_next = jnp.bitwise_and(u + 1, _blocks_per_residue(p) - 1) != 0
    return (b, p, jnp.where(has_next, u + 1, u), 0, 0)


UNIT_BLOCK = (None, None, None, ATTN_BLOCK, 128)


def _attn_masks():
    qi = lax.broadcasted_iota(jnp.int32, (ATTN_BLOCK, ATTN_BLOCK), 0)
    kj = lax.broadcasted_iota(jnp.int32, (ATTN_BLOCK, ATTN_BLOCK), 1)
    d_own = qi - kj
    return d_own, d_own >= 0, d_own <= 0


def _attn_fwd(qu, ku, vu):
    B = qu.shape[0]

    def body(coef_ref, q_ref, k_ref, kp_ref, v_ref, vp_ref, o_ref, lse_ref):
        p, u = pl.program_id(1), pl.program_id(2)
        has_prev = jnp.bitwise_and(u, _blocks_per_residue(p) - 1) != 0
        d_own, ok_own, in_band = _attn_masks()
        ok_prev = in_band & has_prev
        d_own_f = d_own.astype(F32)
        for j in range(2):
            sl = slice(HEAD_DIM * j, HEAD_DIM * (j + 1))
            c = coef_ref[p, j]
            q = q_ref[:, sl]
            s_o = jnp.where(ok_own, _dot_nt(q, k_ref[:, sl]) * 0.125 - c * d_own_f, NEG_BIG)
            s_p = jnp.where(ok_prev, _dot_nt(q, kp_ref[:, sl]) * 0.125 - c * (d_own_f + ATTN_BLOCK), NEG_BIG)
            m = jnp.maximum(jnp.max(s_o, axis=-1, keepdims=True), jnp.max(s_p, axis=-1, keepdims=True))
            e_o, e_p = jnp.exp(s_o - m), jnp.exp(s_p - m)
            l = jnp.sum(e_o, axis=-1, keepdims=True) + jnp.sum(e_p, axis=-1, keepdims=True)
            acc = _dot(e_o.astype(BF16), v_ref[:, sl]) + _dot(e_p.astype(BF16), vp_ref[:, sl])
            o_ref[:, sl] = acc / l
            lse_ref[:, sl] = jnp.broadcast_to(m + jnp.log(l), (ATTN_BLOCK, HEAD_DIM))

    unit = lambda im: pl.BlockSpec(UNIT_BLOCK, im)
    shape = jax.ShapeDtypeStruct(qu.shape, F32)
    return pl.pallas_call(
        body, name="attn_fwd", grid=(B, 3, N_UNITS),
        in_specs=[pl.BlockSpec(memory_space=pltpu.SMEM), unit(_unit_own), unit(_unit_own), unit(_unit_prev),
                  unit(_unit_own), unit(_unit_prev)],
        out_specs=[unit(_unit_own), unit(_unit_own)],
        out_shape=[shape, shape],
        compiler_params=_params("parallel", "parallel", "parallel"),
    )(_alibi_coef(), qu, ku, ku, vu, vu)


def _attn_mix_bwd(dya, o, lse):
    T = dya.shape[0]
    tm = ROW_TILE

    def body(dy_ref, o_ref, lse_ref, do_ref, dd_ref):
        al = _attn_alpha(lse_ref[...])
        lane = lax.broadcasted_iota(jnp.int32, (tm, 128), 1)
        first = lane < HEAD_DIM
        tot = jnp.zeros((tm, 128), F32)
        for p in range(3):
            sl = slice(p * 128, (p + 1) * 128)
            dy = dy_ref[:, sl]
            do_ref[:, sl] = (dy * al[p]).astype(BF16)
            prod = dy * o_ref[:, sl]
            s0 = jnp.sum(jnp.where(first, prod, 0.0), axis=-1, keepdims=True)
            s1 = jnp.sum(jnp.where(first, 0.0, prod), axis=-1, keepdims=True)
            tot = tot + al[p] * jnp.where(first, s0, s1)
        for p in range(3):
            dd_ref[:, p * 128:(p + 1) * 128] = -al[p] * tot

    spec = pl.BlockSpec((tm, ATTN_W), lambda i: (i, 0))
    return pl.pallas_call(
        body, name="attn_mix_bwd", grid=(T // tm,),
        in_specs=[spec, spec, spec], out_specs=[spec, spec],
        out_shape=[jax.ShapeDtypeStruct((T, ATTN_W), BF16), jax.ShapeDtypeStruct((T, ATTN_W), F32)],
        compiler_params=_params("parallel"),
    )(dya, o, lse)


def _attn_bwd(qu, ku, vu, dou, lseu, ddu):
    B = qu.shape[0]

    def body(coef_ref, q_ref, qn_ref, k_ref, kp_ref, v_ref, vp_ref, do_ref, don_ref, lse_ref, lsen_ref, dd_ref, ddn_ref,
             dq_ref, dk_ref, dv_ref):
        p, u = pl.program_id(1), pl.program_id(2)
        mask = _blocks_per_residue(p) - 1
        has_prev = jnp.bitwise_and(u, mask) != 0
        has_next = jnp.bitwise_and(u + 1, mask) != 0
        d_own, ok_own, in_band = _attn_masks()
        ok_prev = in_band & has_prev
        ok_next = in_band & has_next
        d_own_f = d_own.astype(F32)

        def probs(q, k, ok, dist, lse, c):
            s = _dot_nt(q, k) * 0.125 - c * dist
            return jnp.where(ok, jnp.exp(jnp.where(ok, s, NEG_BIG) - lse), 0.0)

        for j in range(2):
            sl = slice(HEAD_DIM * j, HEAD_DIM * (j + 1))
            c = coef_ref[p, j]
            q, qn, k, kp, v, vp = q_ref[:, sl], qn_ref[:, sl], k_ref[:, sl], kp_ref[:, sl], v_ref[:, sl], vp_ref[:, sl]
            do, don = do_ref[:, sl], don_ref[:, sl]
            lse, lsen = lse_ref[:, sl][:, 0:1], lsen_ref[:, sl][:, 0:1]
            dd, ddn = dd_ref[:, sl][:, 0:1], ddn_ref[:, sl][:, 0:1]
            p_o = probs(q, k, ok_own, d_own_f, lse, c)
            ds_o = (p_o * (_dot_nt(do, v) + dd)).astype(BF16)
            p_p = probs(q, kp, ok_prev, d_own_f + ATTN_BLOCK, lse, c)
            ds_p = (p_p * (_dot_nt(do, vp) + dd)).astype(BF16)
            p_n = probs(qn, k, ok_next, d_own_f + ATTN_BLOCK, lsen, c)
            ds_n = (p_n * (_dot_nt(don, v) + ddn)).astype(BF16)
            dq_ref[:, sl] = ((_dot(ds_o, k) + _dot(ds_p, kp)) * 0.125).astype(BF16)
            dk_ref[:, sl] = ((_dot_tn(ds_o, q) + _dot_tn(ds_n, qn)) * 0.125).astype(BF16)
            dv_ref[:, sl] = (_dot_tn(p_o.astype(BF16), do) + _dot_tn(p_n.astype(BF16), don)).astype(BF16)

    unit = lambda im: pl.BlockSpec(UNIT_BLOCK, im)
    shape = jax.ShapeDtypeStruct(qu.shape, BF16)
    return pl.pallas_call(
        body, name="attn_bwd", grid=(B, 3, N_UNITS),
        in_specs=[pl.BlockSpec(memory_space=pltpu.SMEM), unit(_unit_own), unit(_unit_next), unit(_unit_own), unit(_unit_prev),
                  unit(_unit_own), unit(_unit_prev), unit(_unit_own), unit(_unit_next), unit(_unit_own), unit(_unit_next),
                  unit(_unit_own), unit(_unit_next)],
        out_specs=[unit(_unit_own), unit(_unit_own), unit(_unit_own)],
        out_shape=[shape, shape, shape],
        compiler_params=_params("parallel", "parallel", "parallel"),
    )(_alibi_coef(), qu, qu, ku, ku, vu, vu, dou, dou, lseu, lseu, ddu, ddu)


def _for_chunks(n_rows, fn, chunk=SEQ_CHUNK):
    def step(c, carry):
        fn(pl.multiple_of(c * chunk, chunk))
        return carry

    lax.fori_loop(0, n_rows // chunk, step, 0)


def _shift_down(win, s, rows):
    lead = win.shape[0] - rows
    if s == 0:
        return win[lead:]
    if s % 8 == 0:
        return win[lead - s:lead - s + rows]
    q, r = divmod(s, 8)
    rolled = pltpu.roll(win, r, 0)
    return rolled[lead - 8 * q:lead - 8 * q + rows]


def _shift_up(win, s, rows):
    if s % 8 == 0:
        return win[s:s + rows]
    q, r = divmod(s, 8)
    rolled = pltpu.roll(win, win.shape[0] - r, 0)
    return rolled[8 * q:8 * q + rows]


CONV_PAD = 32


def _ln_silu(c, lg, lb):
    mu = jnp.mean(c, axis=-1, keepdims=True)
    cc = c - mu
    rstd = lax.rsqrt(jnp.mean(cc * cc, axis=-1, keepdims=True) + LN_EPS)
    nrm = cc * rstd
    v = nrm * lg + lb
    sg = _sigmoid(v)
    return v * sg, nrm, rstd, v, sg


def _conv_fwd(ci, w, b, lg, lb):
    B, S, _ = ci.shape
    CH = SEQ_CHUNK

    def body(ci_ref, w_ref, b_ref, lg_ref, lb_ref, y_ref, c_ref, pad_ref):
        pad_ref[0:CONV_PAD, :] = jnp.zeros((CONV_PAD, CONV_W), F32)

        def glu(base):
            blk = ci_ref[pl.ds(base, CH), :]
            pad_ref[pl.ds(CONV_PAD + base, CH), :] = blk[:, 0:CONV_W] * _sigmoid(blk[:, CONV_W:])

        _for_chunks(S, glu)

        def conv(base):
            win = pad_ref[pl.ds(base, CH + CONV_PAD), :]
            acc = jnp.broadcast_to(b_ref[...], (CH, CONV_W))
            for k in range(CONV_TAPS):
                acc = acc + w_ref[k:k + 1, :] * _shift_down(win, CONV_TAPS - 1 - k, CH)
            c_ref[pl.ds(base, CH), :] = acc
            y, _, _, _, _ = _ln_silu(acc, lg_ref[...], lb_ref[...])
            y_ref[pl.ds(base, CH), :] = y.astype(BF16)

        _for_chunks(S, conv)

    vec = pl.BlockSpec((1, CONV_W), lambda i: (0, 0))
    return pl.pallas_call(
        body, name="conv_fwd", grid=(B,),
        in_specs=[pl.BlockSpec((None, S, CONV_IN_W), lambda i: (i, 0, 0)),
                  pl.BlockSpec((CONV_TAPS, CONV_W), lambda i: (0, 0)), vec, vec, vec],
        out_specs=[pl.BlockSpec((None, S, CONV_W), lambda i: (i, 0, 0)),
                   pl.BlockSpec((None, S, CONV_W), lambda i: (i, 0, 0))],
        out_shape=[jax.ShapeDtypeStruct((B, S, CONV_W), BF16), jax.ShapeDtypeStruct((B, S, CONV_W), F32)],
        scratch_shapes=[pltpu.VMEM((S + CONV_PAD, CONV_W), F32)],
        compiler_params=_params("parallel"),
    )(ci, w, b, lg, lb)


def _conv_bwd(ci, cpre, dy, w, lg, lb):
    B, S, _ = ci.shape
    CH = SEQ_CHUNK

    def body(ci_ref, c_ref, dy_ref, w_ref, lg_ref, lb_ref, dci_ref, dw_ref, db_ref, dlg_ref, dlb_ref, upad_ref, dcpad_ref,
             dwacc_ref):
        @pl.when(pl.program_id(0) == 0)
        def _():
            dw_ref[...] = jnp.zeros_like(dw_ref)
            db_ref[...] = jnp.zeros_like(db_ref)
            dlg_ref[...] = jnp.zeros_like(dlg_ref)
            dlb_ref[...] = jnp.zeros_like(dlb_ref)

        upad_ref[0:CONV_PAD, :] = jnp.zeros((CONV_PAD, CONV_W), F32)
        dcpad_ref[S:S + CONV_PAD, :] = jnp.zeros((CONV_PAD, CONV_W), F32)
        dwacc_ref[...] = jnp.zeros_like(dwacc_ref)

        def norm_bwd(base):
            blk = ci_ref[pl.ds(base, CH), :]
            upad_ref[pl.ds(CONV_PAD + base, CH), :] = blk[:, 0:CONV_W] * _sigmoid(blk[:, CONV_W:])
            lgv = lg_ref[...]
            _, nrm, rstd, v, sg = _ln_silu(c_ref[pl.ds(base, CH), :], lgv, lb_ref[...])
            dv = dy_ref[pl.ds(base, CH), :] * (sg * (1.0 + v * (1.0 - sg)))
            dlg_ref[...] += jnp.sum(dv * nrm, axis=0, keepdims=True)
            dlb_ref[...] += jnp.sum(dv, axis=0, keepdims=True)
            dn = dv * lgv
            dc = rstd * (dn - jnp.mean(dn, axis=-1, keepdims=True) - nrm * jnp.mean(dn * nrm, axis=-1, keepdims=True))
            dcpad_ref[pl.ds(base, CH), :] = dc
            db_ref[...] += jnp.sum(dc, axis=0, keepdims=True)

        _for_chunks(S, norm_bwd)

        def conv_bwd(base):
            dwin = dcpad_ref[pl.ds(base, CH + CONV_PAD), :]
            uwin = upad_ref[pl.ds(base, CH + CONV_PAD), :]
            dc = dwin[0:CH]
            du = jnp.zeros((CH, CONV_W), F32)
            for k in range(CONV_TAPS):
                du = du + w_ref[k:k + 1, :] * _shift_up(dwin, CONV_TAPS - 1 - k, CH)
                prod = dc * _shift_down(uwin, CONV_TAPS - 1 - k, CH)
                dwacc_ref[8 * k:8 * k + 8, :] += jnp.sum(prod.reshape(CH // 8, 8, CONV_W), axis=0)
            blk = ci_ref[pl.ds(base, CH), :]
            a, sg = blk[:, 0:CONV_W], _sigmoid(blk[:, CONV_W:])
            dci_ref[pl.ds(base, CH), 0:CONV_W] = (du * sg).astype(BF16)
            dci_ref[pl.ds(base, CH), CONV_W:] = (du * a * sg * (1.0 - sg)).astype(BF16)

        _for_chunks(S, conv_bwd)
        for k in range(CONV_TAPS):
            dw_ref[k:k + 1, :] += jnp.sum(dwacc_ref[8 * k:8 * k + 8, :], axis=0, keepdims=True)

    vec = pl.BlockSpec((1, CONV_W), lambda i: (0, 0))
    mat = pl.BlockSpec((CONV_TAPS, CONV_W), lambda i: (0, 0))
    seq = lambda width: pl.BlockSpec((None, S, width), lambda i: (i, 0, 0))
    return pl.pallas_call(
        body, name="conv_bwd", grid=(B,),
        in_specs=[seq(CONV_IN_W), seq(CONV_W), seq(CONV_W), mat, vec, vec],
        out_specs=[seq(CONV_IN_W), mat, vec, vec, vec],
        out_shape=[jax.ShapeDtypeStruct((B, S, CONV_IN_W), BF16), jax.ShapeDtypeStruct((CONV_TAPS, CONV_W), F32),
                   jax.ShapeDtypeStruct((1, CONV_W), F32), jax.ShapeDtypeStruct((1, CONV_W), F32),
                   jax.ShapeDtypeStruct((1, CONV_W), F32)],
        scratch_shapes=[pltpu.VMEM((S + CONV_PAD, CONV_W), F32), pltpu.VMEM((S + CONV_PAD, CONV_W), F32),
                        pltpu.VMEM((8 * CONV_TAPS, CONV_W), F32)],
        compiler_params=_params("arbitrary"),
    )(ci, cpre, dy, w, lg, lb)


SCAN_SHIFTS = tuple(1 << e for e in range(11))


def _prev8(ref, base, cols, fill):
    start = pl.multiple_of(jnp.maximum(base - 8, 0), 8)
    return jnp.where(base > 0, ref[pl.ds(start, 8), cols], fill)


def _next8(ref, base, rows, total, cols, fill):
    start = pl.multiple_of(jnp.minimum(base + rows, total - 8), 8)
    return jnp.where(base + rows < total, ref[pl.ds(start, 8), cols], fill)


ALL = slice(None)
LRU_X = slice(LRU_W, LRU_IN_W)
LRU_GATE = slice(0, LRU_W)


def _lru_conv(li_ref, base, rows, cw_ref, cb_ref):
    win = jnp.concatenate([_prev8(li_ref, base, LRU_X, 0.0), li_ref[pl.ds(base, rows), LRU_X]], axis=0)
    u = jnp.broadcast_to(cb_ref[...], (rows, LRU_W))
    for k in range(LRU_TAPS):
        u = u + cw_ref[k:k + 1, :] * _shift_down(win, LRU_TAPS - 1 - k, rows)
    return u, win


def _lru_gates(u, wa_ref, ba_ref, wx_ref, bx_ref, sp):
    ub = u.astype(BF16)
    r = _sigmoid(_dot(ub, wa_ref[...]) + ba_ref[...])
    i = _sigmoid(_dot(ub, wx_ref[...]) + bx_ref[...])
    la = (-LRU_C) * r * sp
    a = jnp.exp(la)
    mult = jnp.sqrt(-_expm1(2.0 * la))
    return ub, r, i, a, mult


def _scan_forward(bufs, S, CH):
    for n, s in enumerate(SCAN_SHIFTS):
        (sa, sb), (da, db) = bufs[n % 2], bufs[(n + 1) % 2]

        def step(base, s=s, sa=sa, sb=sb, da=da, db=db):
            a, b = sa[pl.ds(base, CH), :], sb[pl.ds(base, CH), :]
            if s < 8:
                a_s = _shift_down(jnp.concatenate([_prev8(sa, base, ALL, 1.0), a], axis=0), s, CH)
                b_s = _shift_down(jnp.concatenate([_prev8(sb, base, ALL, 0.0), b], axis=0), s, CH)
            elif s < CH:
                start = pl.multiple_of(jnp.maximum(base - s, 0), 8)
                a_s = jnp.concatenate([jnp.where(base > 0, sa[pl.ds(start, s), :], 1.0), a[0:CH - s]], axis=0)
                b_s = jnp.concatenate([jnp.where(base > 0, sb[pl.ds(start, s), :], 0.0), b[0:CH - s]], axis=0)
            else:
                start = pl.multiple_of(jnp.maximum(base - s, 0), 8)
                a_s = jnp.where(base < s, 1.0, sa[pl.ds(start, CH), :])
                b_s = jnp.where(base < s, 0.0, sb[pl.ds(start, CH), :])
            db[pl.ds(base, CH), :] = a * b_s + b
            da[pl.ds(base, CH), :] = a * a_s

        _for_chunks(S, step, CH)
    return len(SCAN_SHIFTS) % 2


def _scan_backward(bufs, S, CH):
    for n, s in enumerate(SCAN_SHIFTS):
        (sa, sb), (da, db) = bufs[n % 2], bufs[(n + 1) % 2]

        def step(base, s=s, sa=sa, sb=sb, da=da, db=db):
            a, b = sa[pl.ds(base, CH), :], sb[pl.ds(base, CH), :]
            if s < 8:
                a_s = _shift_up(jnp.concatenate([a, _next8(sa, base, CH, S, ALL, 1.0)], axis=0), s, CH)
                b_s = _shift_up(jnp.concatenate([b, _next8(sb, base, CH, S, ALL, 0.0)], axis=0), s, CH)
            elif s < CH:
                start = pl.multiple_of(jnp.minimum(base + CH, S - s), 8)
                more = base + CH < S
                a_s = jnp.concatenate([a[s:CH], jnp.where(more, sa[pl.ds(start, s), :], 1.0)], axis=0)
                b_s = jnp.concatenate([b[s:CH], jnp.where(more, sb[pl.ds(start, s), :], 0.0)], axis=0)
            else:
                start = pl.multiple_of(jnp.minimum(base + s, S - CH), 8)
                a_s = jnp.where(base + s >= S, 1.0, sa[pl.ds(start, CH), :])
                b_s = jnp.where(base + s >= S, 0.0, sb[pl.ds(start, CH), :])
            db[pl.ds(base, CH), :] = a * b_s + b
            da[pl.ds(base, CH), :] = a * a_s

        _for_chunks(S, step, CH)
    return len(SCAN_SHIFTS) % 2


def _lru_fwd(li, cw, cb, wa, ba, wx, bx, lam):
    B, S, _ = li.shape
    CH = SEQ_CHUNK

    def body(li_ref, cw_ref, cb_ref, wa_ref, ba_ref, wx_ref, bx_ref, lam_ref, y_ref, h_ref, a0, b0, a1, b1):
        sp = _softplus(-lam_ref[...])

        def gates(base):
            u, _ = _lru_conv(li_ref, base, CH, cw_ref, cb_ref)
            _, _, i, a, mult = _lru_gates(u, wa_ref, ba_ref, wx_ref, bx_ref, sp)
            a0[pl.ds(base, CH), :] = a
            b0[pl.ds(base, CH), :] = mult * (i * u)

        _for_chunks(S, gates)
        bufs = ((a0, b0), (a1, b1))
        hb = bufs[_scan_forward(bufs, S, CH)][1]

        def out(base):
            h = hb[pl.ds(base, CH), :]
            h_ref[pl.ds(base, CH), :] = h
            gl, _ = _gelu(li_ref[pl.ds(base, CH), LRU_GATE])
            y_ref[pl.ds(base, CH), :] = (gl * h).astype(BF16)

        _for_chunks(S, out)

    vec = pl.BlockSpec((1, LRU_W), lambda i: (0, 0))
    mat = pl.BlockSpec((LRU_W, LRU_W), lambda i: (0, 0))
    seq = lambda width: pl.BlockSpec((None, S, width), lambda i: (i, 0, 0))
    return pl.pallas_call(
        body, name="lru_fwd", grid=(B,),
        in_specs=[seq(LRU_IN_W), pl.BlockSpec((LRU_TAPS, LRU_W), lambda i: (0, 0)), vec, mat, vec, mat, vec, vec],
        out_specs=[seq(LRU_W), seq(LRU_W)],
        out_shape=[jax.ShapeDtypeStruct((B, S, LRU_W), BF16), jax.ShapeDtypeStruct((B, S, LRU_W), F32)],
        scratch_shapes=[pltpu.VMEM((S, LRU_W), F32)] * 4,
        compiler_params=_params("parallel"),
    )(li, cw, cb, wa, ba, wx, bx, lam)


def _lru_bwd(li, hs, dy, cw, cb, wa, ba, wx, bx, lam):
    B, S, _ = li.shape
    CH = SEQ_CHUNK

    def body(li_ref, hs_ref, dy_ref, cw_ref, cb_ref, wa_ref, ba_ref, wx_ref, bx_ref, lam_ref,
             dli_ref, dcw_ref, dcb_ref, dwa_ref, dba_ref, dwx_ref, dbx_ref, dlam_ref, a0, b0, a1, b1, u_s, du_s):
        @pl.when(pl.program_id(0) == 0)
        def _():
            for ref in (dcw_ref, dcb_ref, dwa_ref, dba_ref, dwx_ref, dbx_ref, dlam_ref):
                ref[...] = jnp.zeros_like(ref)

        lam_v = lam_ref[...]
        sp = _softplus(-lam_v)
        dsp_dlam = -_sigmoid(-lam_v)

        def gates(base):
            u, _ = _lru_conv(li_ref, base, CH, cw_ref, cb_ref)
            _, _, _, a, _ = _lru_gates(u, wa_ref, ba_ref, wx_ref, bx_ref, sp)
            gl, _ = _gelu(li_ref[pl.ds(base, CH), LRU_GATE])
            u_s[pl.ds(base, CH), :] = u
            a0[pl.ds(base, CH), :] = a
            b0[pl.ds(base, CH), :] = a * (dy_ref[pl.ds(base, CH), :] * gl)

        _for_chunks(S, gates)
        bufs = ((a0, b0), (a1, b1))
        eb = bufs[_scan_backward(bufs, S, CH)][1]

        def grads(base):
            e = eb[pl.ds(base, CH), :]
            e_next = _shift_up(jnp.concatenate([e, _next8(eb, base, CH, S, ALL, 0.0)], axis=0), 1, CH)
            gate = li_ref[pl.ds(base, CH), LRU_GATE]
            gl, th = _gelu(gate)
            dy = dy_ref[pl.ds(base, CH), :]
            g = dy * gl + e_next
            h = hs_ref[pl.ds(base, CH), :]
            h_prev = _shift_down(jnp.concatenate([_prev8(hs_ref, base, ALL, 0.0), h], axis=0), 1, CH)
            dli_ref[pl.ds(base, CH), LRU_GATE] = (dy * h * _gelu_grad(gate, th)).astype(BF16)
            u = u_s[pl.ds(base, CH), :]
            ub, r, i, a, mult = _lru_gates(u, wa_ref, ba_ref, wx_ref, bx_ref, sp)
            da = g * h_prev
            dmult = g * (i * u)
            di = g * mult * u
            dla = da * a - dmult * (a * a) / mult
            dlam_ref[...] += dsp_dlam * jnp.sum(dla * ((-LRU_C) * r), axis=0, keepdims=True)
            dpa = (dla * ((-LRU_C) * sp)) * r * (1.0 - r)
            dpx = di * i * (1.0 - i)
            dpab, dpxb = dpa.astype(BF16), dpx.astype(BF16)
            dwa_ref[...] += _dot_tn(ub, dpab)
            dwx_ref[...] += _dot_tn(ub, dpxb)
            dba_ref[...] += jnp.sum(dpa, axis=0, keepdims=True)
            dbx_ref[...] += jnp.sum(dpx, axis=0, keepdims=True)
            du = g * mult * i + _dot_nt(dpab, wa_ref[...]) + _dot_nt(dpxb, wx_ref[...])
            du_s[pl.ds(base, CH), :] = du
            dcb_ref[...] += jnp.sum(du, axis=0, keepdims=True)

        _for_chunks(S, grads)

        def conv_bwd(base):
            du = du_s[pl.ds(base, CH), :]
            dwin = jnp.concatenate([du, _next8(du_s, base, CH, S, ALL, 0.0)], axis=0)
            xwin = jnp.concatenate([_prev8(li_ref, base, LRU_X, 0.0), li_ref[pl.ds(base, CH), LRU_X]], axis=0)
            dx = jnp.zeros((CH, LRU_W), F32)
            for k in range(LRU_TAPS):
                dx = dx + cw_ref[k:k + 1, :] * _shift_up(dwin, LRU_TAPS - 1 - k, CH)
                dcw_ref[k:k + 1, :] += jnp.sum(du * _shift_down(xwin, LRU_TAPS - 1 - k, CH), axis=0, keepdims=True)
            dli_ref[pl.ds(base, CH), LRU_X] = dx.astype(BF16)

        _for_chunks(S, conv_bwd)

    vec = pl.BlockSpec((1, LRU_W), lambda i: (0, 0))
    mat = pl.BlockSpec((LRU_W, LRU_W), lambda i: (0, 0))
    taps = pl.BlockSpec((LRU_TAPS, LRU_W), lambda i: (0, 0))
    seq = lambda width: pl.BlockSpec((None, S, width), lambda i: (i, 0, 0))
    vec_shape = jax.ShapeDtypeStruct((1, LRU_W), F32)
    mat_shape = jax.ShapeDtypeStruct((LRU_W, LRU_W), F32)
    return pl.pallas_call(
        body, name="lru_bwd", grid=(B,),
        in_specs=[seq(LRU_IN_W), seq(LRU_W), seq(LRU_W), taps, vec, mat, vec, mat, vec, vec],
        out_specs=[seq(LRU_IN_W), taps, vec, mat, vec, mat, vec, vec],
        out_shape=[jax.ShapeDtypeStruct((B, S, LRU_IN_W), BF16), jax.ShapeDtypeStruct((LRU_TAPS, LRU_W), F32),
                   vec_shape, mat_shape, vec_shape, mat_shape, vec_shape, vec_shape],
        scratch_shapes=[pltpu.VMEM((S, LRU_W), F32)] * 6,
        compiler_params=_params("arbitrary"),
    )(li, hs, dy, cw, cb, wa, ba, wx, bx, lam)


MESH = pl.DeviceIdType.MESH
HBM_SPEC = pl.BlockSpec(memory_space=pltpu.HBM)


def _all_gather(block, name):
    def body(x_ref, out_ref, send_sems, recv_sems, local_sem):
        x, y, c = lax.axis_index("x"), lax.axis_index("y"), lax.axis_index("c")
        me, sibling = (x, y, c), (x, y, 1 - c)
        chips = [(1 - x, y), (x, 1 - y), (1 - x, 1 - y)]

        def slot(px, py, pc):
            return out_ref.at[4 * px + 2 * py + pc]

        def copy(k, blk, to, src=None):
            return pltpu.make_async_remote_copy(
                src_ref=slot(*blk) if src is None else src, dst_ref=slot(*blk),
                send_sem=send_sems.at[k], recv_sem=recv_sems.at[k], device_id=to, device_id_type=MESH)

        mine = pltpu.make_async_copy(x_ref, slot(*me), local_sem)
        mine.start()
        first = [copy(0, me, sibling, src=x_ref)]
        first += [copy(1 + j, me, (*chip, c), src=x_ref) for j, chip in enumerate(chips)]
        for cp in first:
            cp.start()
        passed = [copy(4 + j, (*chip, c), sibling) for j, chip in enumerate(chips)]
        for j, chip in enumerate(chips):
            copy(1 + j, (*chip, c), me).wait_recv()
            passed[j].start()
        copy(0, sibling, me).wait_recv()
        for j, chip in enumerate(chips):
            copy(4 + j, (*chip, 1 - c), me).wait_recv()
        for cp in first + passed:
            cp.wait_send()
        mine.wait()

    return pl.pallas_call(
        body, name=name,
        out_shape=jax.ShapeDtypeStruct((N_DEV,) + block.shape, block.dtype),
        in_specs=[HBM_SPEC], out_specs=HBM_SPEC,
        scratch_shapes=[pltpu.SemaphoreType.DMA((7,)), pltpu.SemaphoreType.DMA((7,)), pltpu.SemaphoreType.DMA],
    )(block)


def _all_to_all(pieces, name):
    def body(src_ref, dst_ref, send_sems, recv_sems, local_sem):
        x, y, c = lax.axis_index("x"), lax.axis_index("y"), lax.axis_index("c")
        me = 4 * x + 2 * y + c
        mine = pltpu.make_async_copy(src_ref.at[me], dst_ref.at[me], local_sem)
        mine.start()
        copies = []
        for k in range(1, N_DEV):
            px, py, pc = x ^ ((k >> 2) & 1), y ^ ((k >> 1) & 1), c ^ (k & 1)
            peer = 4 * px + 2 * py + pc
            copies.append(pltpu.make_async_remote_copy(
                src_ref=src_ref.at[peer], dst_ref=dst_ref.at[me], send_sem=send_sems.at[k - 1], recv_sem=recv_sems.at[k - 1],
                device_id=(px, py, pc), device_id_type=MESH))
        for cp in copies:
            cp.start()
        for cp in copies:
            cp.wait_recv()
        for cp in copies:
            cp.wait_send()
        mine.wait()

    return pl.pallas_call(
        body, name=name,
        out_shape=jax.ShapeDtypeStruct(pieces.shape, pieces.dtype),
        in_specs=[HBM_SPEC], out_specs=HBM_SPEC,
        scratch_shapes=[pltpu.SemaphoreType.DMA((7,)), pltpu.SemaphoreType.DMA((7,)), pltpu.SemaphoreType.DMA],
    )(pieces)


def _sum_slots(parts, rows_per_step, name):
    _, R, C = parts.shape

    def body(p_ref, o_ref):
        acc = p_ref[0].astype(F32)
        for q in range(1, N_DEV):
            acc = acc + p_ref[q].astype(F32)
        o_ref[...] = acc

    return pl.pallas_call(
        body, name=name, grid=(R // rows_per_step,),
        in_specs=[pl.BlockSpec((N_DEV, rows_per_step, C), lambda i: (0, i, 0))],
        out_specs=pl.BlockSpec((rows_per_step, C), lambda i: (i, 0)),
        out_shape=jax.ShapeDtypeStruct((R, C), F32),
        compiler_params=_params("parallel"),
    )(parts)


def _adamw(w, g, m, v, rows_per_step, name):
    R, C = w.shape
    c1 = 1.0 - ADAM_B1 ** ADAM_STEP
    c2 = 1.0 - ADAM_B2 ** ADAM_STEP

    def body(w_ref, g_ref, m_ref, v_ref, d_ref, nm_ref, nv_ref):
        gv = g_ref[...]
        nm = ADAM_B1 * m_ref[...] + (1.0 - ADAM_B1) * gv
        nv = ADAM_B2 * v_ref[...] + (1.0 - ADAM_B2) * (gv * gv)
        nm_ref[...] = nm
        nv_ref[...] = nv
        d_ref[...] = (-ADAM_LR) * ((nm / c1) / (jnp.sqrt(nv / c2) + ADAM_EPS) + ADAM_WD * w_ref[...])

    spec = pl.BlockSpec((rows_per_step, C), lambda i: (i, 0))
    shape = jax.ShapeDtypeStruct((R, C), F32)
    return pl.pallas_call(
        body, name=name, grid=(R // rows_per_step,),
        in_specs=[spec] * 4, out_specs=[spec] * 3, out_shape=[shape] * 3,
        compiler_params=_params("parallel"),
    )(w, g, m, v)


WEIGHT_ORDER = ("norm1_g", "w_in", "conv_dw_w", "conv_dw_b", "conv_ln_g", "conv_ln_b", "lru_conv_w", "lru_conv_b", "lru_wa",
                "lru_ba", "lru_wx", "lru_bx", "lru_lambda", "w_out", "norm2_g", "w_up", "w_down", "final_g")
BIG = ("w_in", "w_out", "w_up", "w_down")
BIG_COL_SHARDED = {"w_in": True, "w_out": False, "w_up": True, "w_down": False}
BIG_FULL = {"w_in": (D_MODEL, IN_COLS), "w_out": (D_MODEL, D_MODEL), "w_up": (D_MODEL, D_FF), "w_down": (D_FF, D_MODEL)}
SMALL_SHARDED = {"conv_dw_w": (DEPTH, CONV_TAPS, CONV_W), "lru_conv_w": (DEPTH, LRU_TAPS, LRU_W)}
SMALL_FULL = {
    "norm1_g": (DEPTH, D_MODEL), "conv_dw_w": (DEPTH, CONV_TAPS, CONV_W), "conv_dw_b": (DEPTH, CONV_W),
    "conv_ln_g": (DEPTH, CONV_W), "conv_ln_b": (DEPTH, CONV_W), "lru_conv_w": (DEPTH, LRU_TAPS, LRU_W),
    "lru_conv_b": (DEPTH, LRU_W), "lru_wa": (DEPTH, LRU_HEADS, HEAD_DIM, HEAD_DIM), "lru_ba": (DEPTH, LRU_W),
    "lru_wx": (DEPTH, LRU_HEADS, HEAD_DIM, HEAD_DIM), "lru_bx": (DEPTH, LRU_W), "lru_lambda": (DEPTH, LRU_W),
    "norm2_g": (DEPTH, D_MODEL), "final_g": (D_MODEL,),
}
PACK_COLS = 1024
SMALL_COLS = 128
SMALL_ROWS = 1000


def _big_shard_shape(name):
    r, c = BIG_FULL[name]
    return (DEPTH, r, c // N_DEV) if BIG_COL_SHARDED[name] else (DEPTH, r // N_DEV, c)


def _pack_rows(flat_parts, cols, rows):
    flat = jnp.concatenate(flat_parts)
    return jnp.pad(flat, (0, rows * cols - flat.shape[0])).reshape(rows, cols)


def _as_bf16_triples(a):
    a = a.reshape(-1)
    hi = a.astype(BF16)
    r1 = a - hi.astype(F32)
    mid = r1.astype(BF16)
    lo = (r1 - mid.astype(F32)).astype(BF16)
    return jnp.concatenate([hi, mid, lo])


def _from_bf16_triples(a, shape):
    t = a.astype(F32).reshape(a.shape[0], 3, -1)
    return ((t[:, 0] + t[:, 1]) + t[:, 2]).reshape(shape)


WEIGHT_PACK_ROWS = 2928


def _pack_weight_shards(shards, conv_dw_w, lru_conv_w):
    parts = [shards[n].astype(BF16).reshape(-1) for n in BIG]
    parts += [_as_bf16_triples(conv_dw_w), _as_bf16_triples(lru_conv_w)]
    return _pack_rows(parts, PACK_COLS, WEIGHT_PACK_ROWS)


def _full_from_slots(slots, name):
    if BIG_COL_SHARDED[name]:
        return slots.transpose(1, 2, 0, 3).reshape((DEPTH,) + BIG_FULL[name])
    return slots.transpose(1, 0, 2, 3).reshape((DEPTH,) + BIG_FULL[name])


def _slots_from_full(full, name):
    r, c = BIG_FULL[name]
    if BIG_COL_SHARDED[name]:
        return full.reshape(DEPTH, r, N_DEV, c // N_DEV).transpose(2, 0, 1, 3).reshape(N_DEV, -1)
    return full.reshape(DEPTH, N_DEV, r // N_DEV, c).transpose(1, 0, 2, 3).reshape(N_DEV, -1)


def _unpack_gathered(g):
    flat = g.reshape(N_DEV, -1)
    out, off = {}, 0
    for n in BIG:
        shp = _big_shard_shape(n)
        size = int(np.prod(shp))
        out[n] = _full_from_slots(flat[:, off:off + size].reshape((N_DEV,) + shp), n)
        off += size
    for n, full in SMALL_SHARDED.items():
        shp = full[:-1] + (full[-1] // N_DEV,)
        size = 3 * int(np.prod(shp))
        slots = _from_bf16_triples(flat[:, off:off + size], (N_DEV,) + shp)
        out[n] = jnp.moveaxis(slots, 0, -2).reshape(full)
        off += size
    return out


def _block_diag(w):
    out = jnp.zeros((LRU_W, LRU_W), w.dtype)
    for i in range(LRU_HEADS):
        out = lax.dynamic_update_slice(out, w[i], (HEAD_DIM * i, HEAD_DIM * i))
    return out


def _diag_blocks(m):
    return jnp.stack([m[HEAD_DIM * i:HEAD_DIM * (i + 1), HEAD_DIM * i:HEAD_DIM * (i + 1)] for i in range(LRU_HEADS)])


def _pack_small(values, loss_row):
    parts = [values[n].reshape(-1) for n in SMALL_FULL]
    parts.append(loss_row.reshape(-1))
    return _pack_rows(parts, SMALL_COLS, SMALL_ROWS)


def _unpack_small(packed):
    flat = packed.reshape(-1)
    out, off = {}, 0
    for n, shp in SMALL_FULL.items():
        size = int(np.prod(shp))
        out[n] = flat[off:off + size].reshape(shp)
        off += size
    return out, flat[off]


def kernel(x, norm1_g, w_in, conv_dw_w, conv_dw_b, conv_ln_g, conv_ln_b, lru_conv_w, lru_conv_b, lru_wa, lru_ba, lru_wx, lru_bx, lru_lambda, w_out, norm2_g, w_up, w_down, final_g, loss_target, m_norm1_g, m_w_in, m_conv_dw_w, m_conv_dw_b, m_conv_ln_g, m_conv_ln_b, m_lru_conv_w, m_lru_conv_b, m_lru_wa, m_lru_ba, m_lru_wx, m_lru_bx, m_lru_lambda, m_w_out, m_norm2_g, m_w_up, m_w_down, m_final_g, v_norm1_g, v_w_in, v_conv_dw_w, v_conv_dw_b, v_conv_ln_g, v_conv_ln_b, v_lru_conv_w, v_lru_conv_b, v_lru_wa, v_lru_ba, v_lru_wx, v_lru_bx, v_lru_lambda, v_w_out, v_norm2_g, v_w_up, v_w_down, v_final_g):
    local = dict(zip(WEIGHT_ORDER, (norm1_g, w_in, conv_dw_w, conv_dw_b, conv_ln_g, conv_ln_b, lru_conv_w, lru_conv_b, lru_wa,
                                    lru_ba, lru_wx, lru_bx, lru_lambda, w_out, norm2_g, w_up, w_down, final_g)))
    mom1 = dict(zip(WEIGHT_ORDER, (m_norm1_g, m_w_in, m_conv_dw_w, m_conv_dw_b, m_conv_ln_g, m_conv_ln_b, m_lru_conv_w,
                                   m_lru_conv_b, m_lru_wa, m_lru_ba, m_lru_wx, m_lru_bx, m_lru_lambda, m_w_out, m_norm2_g,
                                   m_w_up, m_w_down, m_final_g)))
    mom2 = dict(zip(WEIGHT_ORDER, (v_norm1_g, v_w_in, v_conv_dw_w, v_conv_dw_b, v_conv_ln_g, v_conv_ln_b, v_lru_conv_w,
                                   v_lru_conv_b, v_lru_wa, v_lru_ba, v_lru_wx, v_lru_bx, v_lru_lambda, v_w_out, v_norm2_g,
                                   v_w_up, v_w_down, v_final_g)))
    B, S, _ = x.shape
    T = B * S
    my_slot = 4 * lax.axis_index("x") + 2 * lax.axis_index("y") + lax.axis_index("c")
    row = lambda a: a.reshape(1, -1)

    gathered = _all_gather(_pack_weight_shards(local, conv_dw_w, lru_conv_w), "gather_weights")
    full = _unpack_gathered(gathered)

    saved = []
    cur = x.reshape(T, D_MODEL)
    for l in range(DEPTH):
        h, qkv, ci, li = _inproj(cur, row(norm1_g[l]), full["w_in"][l])
        qu, ku, vu = (_to_units(qkv[:, i * ATTN_W:(i + 1) * ATTN_W].reshape(B, S, ATTN_W)) for i in range(3))
        ou, lseu = _attn_fwd(qu, ku, vu)
        o = _from_units(ou).reshape(T, ATTN_W)
        lse = _from_units(lseu).reshape(T, ATTN_W)
        ci = ci.reshape(B, S, CONV_IN_W)
        li = li.reshape(B, S, LRU_IN_W)
        conv_p = (full["conv_dw_w"][l], row(conv_dw_b[l]), row(conv_ln_g[l]), row(conv_ln_b[l]))
        lru_p = (full["lru_conv_w"][l], row(lru_conv_b[l]), _block_diag(lru_wa[l]).astype(BF16), row(lru_ba[l]),
                 _block_diag(lru_wx[l]).astype(BF16), row(lru_bx[l]), row(lru_lambda[l]))
        yc, cpre = _conv_fwd(ci, *conv_p)
        yl, hs = _lru_fwd(li, *lru_p)
        x1, mix = _outproj(cur, o, lse, yc.reshape(T, CONV_W), yl.reshape(T, LRU_W), full["w_out"][l])
        h2, pre = _up(x1, row(norm2_g[l]), full["w_up"][l])
        x2 = _down(x1, pre, full["w_down"][l])
        saved.append(dict(x=cur, h=h, qu=qu, ku=ku, vu=vu, lseu=lseu, o=o, lse=lse, ci=ci, li=li, cpre=cpre, hs=hs, x1=x1,
                          mix=mix, h2=h2, pre=pre, conv_p=conv_p, lru_p=lru_p))
        cur = x2

    dx, loss_part, dgf = _loss_head(cur, loss_target.reshape(T, D_MODEL), row(final_g))

    big_grads = {n: [None] * DEPTH for n in BIG}
    small_grads = {n: [None] * DEPTH for n in SMALL_FULL if n != "final_g"}
    for l in reversed(range(DEPTH)):
        sv = saved[l]
        dpre = _down_bwd_act(dx, sv["pre"], full["w_down"][l])
        big_grads["w_down"][l] = _down_bwd_w(sv["pre"], dx)
        dx1, dg2 = _up_bwd_act(dpre, full["w_up"][l], sv["x1"], row(norm2_g[l]), dx)
        big_grads["w_up"][l] = _up_bwd_w(sv["h2"], dpre)
        dya, dyc, dyl, big_grads["w_out"][l] = _outproj_bwd(dx1, sv["mix"], full["w_out"][l])
        do, dd = _attn_mix_bwd(dya, sv["o"], sv["lse"])
        dqu, dku, dvu = _attn_bwd(sv["qu"], sv["ku"], sv["vu"], _to_units(do.reshape(B, S, ATTN_W)), sv["lseu"],
                                  _to_units(dd.reshape(B, S, ATTN_W)))
        dqkv = jnp.concatenate([_from_units(t) for t in (dqu, dku, dvu)], axis=-1).reshape(T, QKV_W)
        dci, dcw, dcb, dlg, dlb = _conv_bwd(sv["ci"], sv["cpre"], dyc.reshape(B, S, CONV_W), sv["conv_p"][0], sv["conv_p"][2],
                                            sv["conv_p"][3])
        dli, dlcw, dlcb, dwa, dba, dwx, dbx, dlam = _lru_bwd(sv["li"], sv["hs"], dyl.reshape(B, S, LRU_W), *sv["lru_p"])
        dx, dg1, big_grads["w_in"][l] = _inproj_bwd(dqkv, dci.reshape(T, CONV_IN_W), dli.reshape(T, LRU_IN_W), sv["h"],
                                                    full["w_in"][l], sv["x"], row(norm1_g[l]), dx1)
        for n, g in (("norm1_g", dg1), ("conv_dw_w", dcw), ("conv_dw_b", dcb), ("conv_ln_g", dlg), ("conv_ln_b", dlb),
                     ("lru_conv_w", dlcw), ("lru_conv_b", dlcb), ("lru_wa", _diag_blocks(dwa)), ("lru_ba", dba),
                     ("lru_wx", _diag_blocks(dwx)), ("lru_bx", dbx), ("lru_lambda", dlam), ("norm2_g", dg2)):
            small_grads[n][l] = g.reshape(SMALL_FULL[n][1:])
    grad_x = dx.reshape(B, S, D_MODEL)

    pieces = jnp.concatenate([_slots_from_full(jnp.stack(big_grads[n]), n) for n in BIG], axis=1)
    pieces = pieces.reshape(N_DEV, -1, PACK_COLS)
    big_sum = _sum_slots(_all_to_all(pieces, "exchange_grads"), 208, "sum_big_grads").reshape(-1)

    small_local = {n: jnp.stack(g) for n, g in small_grads.items()}
    small_local["final_g"] = dgf.reshape(D_MODEL)
    packet = _pack_small(small_local, loss_part[0:1, :])
    small_sum = _sum_slots(_all_gather(packet, "gather_small_grads"), SMALL_ROWS, "sum_small_grads")
    small_g, loss = _unpack_small(small_sum)

    grads, delta, new_m, new_v = {}, {}, {}, {}
    off = 0
    for n in BIG:
        shp = _big_shard_shape(n)
        size = int(np.prod(shp))
        two_d = (shp[0] * shp[1], shp[2])
        g = big_sum[off:off + size].reshape(two_d)
        off += size
        d, nm, nv = _adamw(local[n].reshape(two_d), g, mom1[n].reshape(two_d), mom2[n].reshape(two_d), 256, "adamw_" + n)
        grads[n], delta[n], new_m[n], new_v[n] = (t.reshape(shp) for t in (g, d, nm, nv))
    for n, fullshape in SMALL_SHARDED.items():
        width = fullshape[-1] // N_DEV
        g = lax.dynamic_slice_in_dim(small_g[n], my_slot * width, width, axis=2)
        two_d = (fullshape[0] * fullshape[1], width)
        d, nm, nv = _adamw(local[n].reshape(two_d), g.reshape(two_d), mom1[n].reshape(two_d), mom2[n].reshape(two_d),
                           two_d[0], "adamw_" + n)
        grads[n], delta[n], new_m[n], new_v[n] = (t.reshape(g.shape) for t in (g.reshape(two_d), d, nm, nv))
    replicated = [n for n in SMALL_FULL if n not in SMALL_SHARDED]
    zero_row = jnp.zeros((1, SMALL_COLS), F32)
    packed_state = []
    for src in (local, mom1, mom2):
        vals = {n: (src[n] if n in replicated else jnp.zeros(SMALL_FULL[n], F32)) for n in SMALL_FULL}
        packed_state.append(_pack_small(vals, zero_row))
    d, nm, nv = _adamw(packed_state[0], small_sum, packed_state[1], packed_state[2], SMALL_ROWS, "adamw_small")
    d, nm, nv = _unpack_small(d)[0], _unpack_small(nm)[0], _unpack_small(nv)[0]
    for n in replicated:
        grads[n], delta[n], new_m[n], new_v[n] = small_g[n], d[n], nm[n], nv[n]

    return (loss, grad_x, *[grads[n] for n in WEIGHT_ORDER], *[delta[n] for n in WEIGHT_ORDER],
            *[new_m[n] for n in WEIGHT_ORDER], *[new_v[n] for n in WEIGHT_ORDER])
```

```python
import functools
import math

import numpy as np
import jax
import jax.numpy as jnp
from jax import lax
from jax.experimental import pallas as pl
from jax.experimental.pallas import tpu as pltpu

F32 = jnp.float32
BF16 = jnp.bfloat16

D_MODEL = 1024
SEQ_LEN = 2048
HEAD_DIM = 64
ATTN_W = 384
CONV_W = 256
CONV_TAPS = 31
LRU_W = 384
LRU_TAPS = 4
LRU_HEADS = 6
LRU_C = 8.0
QKV_W = 3 * ATTN_W
CONV_IN_W = 2 * CONV_W
LRU_IN_W = 2 * LRU_W
IN_COLS = QKV_W + CONV_IN_W + LRU_IN_W
D_FF = 4096
DEPTH = 2
N_DEV = 8
RMS_EPS = 1e-6
LN_EPS = 1e-5
ATTN_BLOCK = 128
ATTN_DILATIONS = (1, 4, 16)
N_UNITS = 16
NEG_BIG = -1e30

ADAM_LR = 0.001
ADAM_B1 = 0.9
ADAM_B2 = 0.999
ADAM_EPS = 1e-08
ADAM_WD = 0.01
ADAM_STEP = 10

VMEM_LIMIT = 56 * 1024 * 1024
ROW_TILE = 512
SEQ_CHUNK = 128


def _params(*sem):
    return pltpu.CompilerParams(dimension_semantics=sem if sem else None, vmem_limit_bytes=VMEM_LIMIT)


def _resident(shape):
    return pl.BlockSpec(shape, lambda *_: (0,) * len(shape), pipeline_mode=pl.Buffered(1))


def _dot(a, b):
    return jnp.dot(a, b, preferred_element_type=F32)


def _dot_nt(a, b):
    return lax.dot_general(a, b, (((1,), (1,)), ((), ())), preferred_element_type=F32)


def _dot_tn(a, b):
    return lax.dot_general(a, b, (((0,), (0,)), ((), ())), preferred_element_type=F32)


def _rms_fwd(x, g):
    rstd = lax.rsqrt(jnp.mean(x * x, axis=-1, keepdims=True) + RMS_EPS)
    xhat = x * rstd
    return xhat * g, xhat, rstd


def _rms_bwd(dh, xhat, rstd, g):
    dxh = dh * g
    dx = rstd * (dxh - xhat * jnp.mean(dxh * xhat, axis=-1, keepdims=True))
    dg = jnp.sum(dh * xhat, axis=0, keepdims=True)
    return dx, dg


def _sigmoid(x):
    return 1.0 / (1.0 + jnp.exp(-x))


def _expm1(x):
    small = x * (1.0 + x * (0.5 + x * (1.0 / 6.0 + x * (1.0 / 24.0 + x * (1.0 / 120.0)))))
    return jnp.where(jnp.abs(x) < 0.05, small, jnp.exp(x) - 1.0)


def _log1p(z):
    w = 1.0 + z
    return jnp.where(w == 1.0, z, z * jnp.log(w) / jnp.where(w == 1.0, 1.0, w - 1.0))


def _softplus(x):
    return jnp.maximum(x, 0.0) + _log1p(jnp.exp(-jnp.abs(x)))


GELU_K = math.sqrt(2.0 / math.pi)


def _gelu(x):
    t = jnp.tanh(GELU_K * (x + 0.044715 * x * x * x))
    return 0.5 * x * (1.0 + t), t


def _gelu_grad(x, t):
    return 0.5 * (1.0 + t) + 0.5 * x * (1.0 - t * t) * GELU_K * (1.0 + 3.0 * 0.044715 * x * x)


def _inproj(x2d, g, w):
    T = x2d.shape[0]
    tm = ROW_TILE

    def body(x_ref, g_ref, w_ref, h_ref, qkv_ref, ci_ref, li_ref):
        h, _, _ = _rms_fwd(x_ref[...], g_ref[...])
        hb = h.astype(BF16)
        h_ref[...] = hb
        qkv_ref[...] = _dot(hb, w_ref[:, 0:QKV_W])
        ci_ref[...] = _dot(hb, w_ref[:, QKV_W:QKV_W + CONV_IN_W])
        li_ref[...] = _dot(hb, w_ref[:, QKV_W + CONV_IN_W:IN_COLS])

    return pl.pallas_call(
        body, name="inproj", grid=(T // tm,),
        in_specs=[pl.BlockSpec((tm, D_MODEL), lambda i: (i, 0)),
                  pl.BlockSpec((1, D_MODEL), lambda i: (0, 0)),
                  _resident((D_MODEL, IN_COLS))],
        out_specs=[pl.BlockSpec((tm, D_MODEL), lambda i: (i, 0)),
                   pl.BlockSpec((tm, QKV_W), lambda i: (i, 0)),
                   pl.BlockSpec((tm, CONV_IN_W), lambda i: (i, 0)),
                   pl.BlockSpec((tm, LRU_IN_W), lambda i: (i, 0))],
        out_shape=[jax.ShapeDtypeStruct((T, D_MODEL), BF16), jax.ShapeDtypeStruct((T, QKV_W), F32),
                   jax.ShapeDtypeStruct((T, CONV_IN_W), F32), jax.ShapeDtypeStruct((T, LRU_IN_W), F32)],
        compiler_params=_params("parallel"),
    )(x2d, g, w)


def _attn_alpha(lse):
    l0, l1, l2 = lse[:, 0:128], lse[:, 128:256], lse[:, 256:384]
    m = jnp.maximum(jnp.maximum(l0, l1), l2)
    e0, e1, e2 = jnp.exp(l0 - m), jnp.exp(l1 - m), jnp.exp(l2 - m)
    inv = 1.0 / (e0 + e1 + e2)
    return e0 * inv, e1 * inv, e2 * inv


def _outproj(x2d, o, lse, yc, yl, w):
    T = x2d.shape[0]
    tm = ROW_TILE

    def body(x_ref, o_ref, lse_ref, yc_ref, yl_ref, w_ref, x1_ref, mix_ref):
        al = _attn_alpha(lse_ref[...])
        for p in range(3):
            mix_ref[:, p * 128:(p + 1) * 128] = (o_ref[:, p * 128:(p + 1) * 128] * al[p]).astype(BF16)
        mix_ref[:, ATTN_W:ATTN_W + CONV_W] = yc_ref[...]
        mix_ref[:, ATTN_W + CONV_W:D_MODEL] = yl_ref[...]
        x1_ref[...] = x_ref[...] + _dot(mix_ref[...], w_ref[...])

    return pl.pallas_call(
        body, name="outproj", grid=(T // tm,),
        in_specs=[pl.BlockSpec((tm, D_MODEL), lambda i: (i, 0)),
                  pl.BlockSpec((tm, ATTN_W), lambda i: (i, 0)),
                  pl.BlockSpec((tm, ATTN_W), lambda i: (i, 0)),
                  pl.BlockSpec((tm, CONV_W), lambda i: (i, 0)),
                  pl.BlockSpec((tm, LRU_W), lambda i: (i, 0)),
                  _resident((D_MODEL, D_MODEL))],
        out_specs=[pl.BlockSpec((tm, D_MODEL), lambda i: (i, 0)),
                   pl.BlockSpec((tm, D_MODEL), lambda i: (i, 0))],
        out_shape=[jax.ShapeDtypeStruct((T, D_MODEL), F32), jax.ShapeDtypeStruct((T, D_MODEL), BF16)],
        compiler_params=_params("parallel"),
    )(x2d, o, lse, yc, yl, w)


FF_CHUNK = 1024


def _up(x1, g, w):
    T = x1.shape[0]
    tm = ROW_TILE

    def body(x_ref, g_ref, w_ref, h_ref, r_ref):
        h, _, _ = _rms_fwd(x_ref[...], g_ref[...])
        hb = h.astype(BF16)
        h_ref[...] = hb
        for c in range(0, D_FF, FF_CHUNK):
            r_ref[:, c:c + FF_CHUNK] = jnp.maximum(_dot(hb, w_ref[:, c:c + FF_CHUNK]), 0.0).astype(BF16)

    return pl.pallas_call(
        body, name="up", grid=(T // tm,),
        in_specs=[pl.BlockSpec((tm, D_MODEL), lambda i: (i, 0)),
                  pl.BlockSpec((1, D_MODEL), lambda i: (0, 0)),
                  _resident((D_MODEL, D_FF))],
        out_specs=[pl.BlockSpec((tm, D_MODEL), lambda i: (i, 0)),
                   pl.BlockSpec((tm, D_FF), lambda i: (i, 0))],
        out_shape=[jax.ShapeDtypeStruct((T, D_MODEL), BF16), jax.ShapeDtypeStruct((T, D_FF), BF16)],
        compiler_params=_params("parallel"),
    )(x1, g, w)


def _square_bf16(r):
    rf = r.astype(F32)
    return (rf * rf).astype(BF16)


def _down(x1, r, w):
    T = x1.shape[0]
    tm = ROW_TILE

    def body(x_ref, r_ref, w_ref, o_ref):
        acc = x_ref[...]
        for c in range(0, D_FF, FF_CHUNK):
            acc = acc + _dot(_square_bf16(r_ref[:, c:c + FF_CHUNK]), w_ref[c:c + FF_CHUNK, :])
        o_ref[...] = acc

    return pl.pallas_call(
        body, name="down", grid=(T // tm,),
        in_specs=[pl.BlockSpec((tm, D_MODEL), lambda i: (i, 0)),
                  pl.BlockSpec((tm, D_FF), lambda i: (i, 0)),
                  _resident((D_FF, D_MODEL))],
        out_specs=pl.BlockSpec((tm, D_MODEL), lambda i: (i, 0)),
        out_shape=jax.ShapeDtypeStruct((T, D_MODEL), F32),
        compiler_params=_params("parallel"),
    )(x1, r, w)


def _loss_head(x2, target, g):
    T = x2.shape[0]
    tm = ROW_TILE

    def body(x_ref, t_ref, g_ref, dx_ref, loss_ref, dg_ref):
        @pl.when(pl.program_id(0) == 0)
        def _():
            loss_ref[...] = jnp.zeros_like(loss_ref)
            dg_ref[...] = jnp.zeros_like(dg_ref)

        gv = g_ref[...]
        y, xhat, rstd = _rms_fwd(x_ref[...], gv)
        err = y - t_ref[...]
        loss_ref[...] += 0.5 * jnp.sum(jnp.mean(err * err, axis=-1, keepdims=True))
        dy = err * (1.0 / D_MODEL)
        dx, dg = _rms_bwd(dy, xhat, rstd, gv)
        dx_ref[...] = dx
        dg_ref[...] += dg

    return pl.pallas_call(
        body, name="loss_head", grid=(T // tm,),
        in_specs=[pl.BlockSpec((tm, D_MODEL), lambda i: (i, 0)),
                  pl.BlockSpec((tm, D_MODEL), lambda i: (i, 0)),
                  pl.BlockSpec((1, D_MODEL), lambda i: (0, 0))],
        out_specs=[pl.BlockSpec((tm, D_MODEL), lambda i: (i, 0)),
                   pl.BlockSpec((8, 128), lambda i: (0, 0)),
                   pl.BlockSpec((1, D_MODEL), lambda i: (0, 0))],
        out_shape=[jax.ShapeDtypeStruct((T, D_MODEL), F32), jax.ShapeDtypeStruct((8, 128), F32),
                   jax.ShapeDtypeStruct((1, D_MODEL), F32)],
        compiler_params=_params("arbitrary"),
    )(x2, target, g)


def _down_bwd_act(dx2, r, w):
    T = dx2.shape[0]
    tm = ROW_TILE

    def body(dx_ref, r_ref, w_ref, o_ref):
        dxb = dx_ref[...].astype(BF16)
        for c in range(0, D_FF, FF_CHUNK):
            dff = _dot_nt(dxb, w_ref[c:c + FF_CHUNK, :])
            o_ref[:, c:c + FF_CHUNK] = (dff * (2.0 * r_ref[:, c:c + FF_CHUNK].astype(F32))).astype(BF16)

    return pl.pallas_call(
        body, name="down_bwd_act", grid=(T // tm,),
        in_specs=[pl.BlockSpec((tm, D_MODEL), lambda i: (i, 0)),
                  pl.BlockSpec((tm, D_FF), lambda i: (i, 0)),
                  _resident((D_FF, D_MODEL))],
        out_specs=pl.BlockSpec((tm, D_FF), lambda i: (i, 0)),
        out_shape=jax.ShapeDtypeStruct((T, D_FF), BF16),
        compiler_params=_params("parallel"),
    )(dx2, r, w)


def _down_bwd_w(r, dx2):
    T = dx2.shape[0]
    tk = ROW_TILE
    nk = T // tk

    def body(r_ref, dx_ref, o_ref, acc_ref):
        k = pl.program_id(0)
        dxb = dx_ref[...].astype(BF16)

        @pl.when(k == 0)
        def _():
            acc_ref[...] = jnp.zeros_like(acc_ref)

        for c in range(0, D_FF, FF_CHUNK):
            acc_ref[:, c:c + FF_CHUNK] += _dot_tn(dxb, _square_bf16(r_ref[:, c:c + FF_CHUNK]))

        @pl.when(k == nk - 1)
        def _():
            o_ref[...] = acc_ref[...].astype(BF16)

    return pl.pallas_call(
        body, name="down_bwd_w", grid=(nk,),
        in_specs=[pl.BlockSpec((tk, D_FF), lambda k: (k, 0)),
                  pl.BlockSpec((tk, D_MODEL), lambda k: (k, 0))],
        out_specs=_resident((D_MODEL, D_FF)),
        out_shape=jax.ShapeDtypeStruct((D_MODEL, D_FF), BF16),
        scratch_shapes=[pltpu.VMEM((D_MODEL, D_FF), F32)],
        compiler_params=_params("arbitrary"),
    )(r, dx2)


def _up_bwd_act(dpre, w, x1, g, dx2):
    T = dx2.shape[0]
    tm = ROW_TILE

    def body(dp_ref, w_ref, x_ref, g_ref, dx2_ref, dx1_ref, dg_ref):
        dh = _dot_nt(dp_ref[:, 0:FF_CHUNK], w_ref[:, 0:FF_CHUNK])
        for c in range(FF_CHUNK, D_FF, FF_CHUNK):
            dh = dh + _dot_nt(dp_ref[:, c:c + FF_CHUNK], w_ref[:, c:c + FF_CHUNK])
        gv = g_ref[...]
        _, xhat, rstd = _rms_fwd(x_ref[...], gv)
        dx, dg = _rms_bwd(dh, xhat, rstd, gv)
        dx1_ref[...] = dx2_ref[...] + dx

        @pl.when(pl.program_id(0) == 0)
        def _():
            dg_ref[...] = dg

        @pl.when(pl.program_id(0) != 0)
        def _():
            dg_ref[...] += dg

    return pl.pallas_call(
        body, name="up_bwd_act", grid=(T // tm,),
        in_specs=[pl.BlockSpec((tm, D_FF), lambda i: (i, 0)),
                  _resident((D_MODEL, D_FF)),
                  pl.BlockSpec((tm, D_MODEL), lambda i: (i, 0)),
                  pl.BlockSpec((1, D_MODEL), lambda i: (0, 0)),
                  pl.BlockSpec((tm, D_MODEL), lambda i: (i, 0))],
        out_specs=[pl.BlockSpec((tm, D_MODEL), lambda i: (i, 0)),
                   pl.BlockSpec((1, D_MODEL), lambda i: (0, 0))],
        out_shape=[jax.ShapeDtypeStruct((T, D_MODEL), F32), jax.ShapeDtypeStruct((1, D_MODEL), F32)],
        compiler_params=_params("arbitrary"),
    )(dpre, w, x1, g, dx2)


def _up_bwd_w(h2, dpre):
    T = h2.shape[0]
    tk = ROW_TILE
    nk = T // tk

    def body(h_ref, dp_ref, o_ref, acc_ref):
        k = pl.program_id(0)
        hb = h_ref[...]

        @pl.when(k == 0)
        def _():
            acc_ref[...] = jnp.zeros_like(acc_ref)

        for c in range(0, D_FF, FF_CHUNK):
            acc_ref[:, c:c + FF_CHUNK] += _dot_tn(hb, dp_ref[:, c:c + FF_CHUNK])

        @pl.when(k == nk - 1)
        def _():
            o_ref[...] = acc_ref[...].astype(BF16)

    return pl.pallas_call(
        body, name="up_bwd_w", grid=(nk,),
        in_specs=[pl.BlockSpec((tk, D_MODEL), lambda k: (k, 0)),
                  pl.BlockSpec((tk, D_FF), lambda k: (k, 0))],
        out_specs=_resident((D_MODEL, D_FF)),
        out_shape=jax.ShapeDtypeStruct((D_MODEL, D_FF), BF16),
        scratch_shapes=[pltpu.VMEM((D_MODEL, D_FF), F32)],
        compiler_params=_params("arbitrary"),
    )(h2, dpre)


def _outproj_bwd(dx1, mix, w, o, lse):
    T = dx1.shape[0]
    tm = ROW_TILE
    nk = T // tm

    def body(dx_ref, mix_ref, w_ref, o_ref, lse_ref, do_ref, dd_ref, dc_ref, dl_ref, dw_ref, acc_ref):
        i = pl.program_id(0)
        dxb = dx_ref[...].astype(BF16)
        dmix = _dot_nt(dxb, w_ref[...])
        al = _attn_alpha(lse_ref[...])
        first = lax.broadcasted_iota(jnp.int32, (tm, 128), 1) < HEAD_DIM
        tot = jnp.zeros((tm, 128), F32)
        for p in range(3):
            sl = slice(p * 128, (p + 1) * 128)
            dy = dmix[:, sl]
            do_ref[:, sl] = dy * al[p]
            prod = dy * o_ref[:, sl]
            s0 = jnp.sum(jnp.where(first, prod, 0.0), axis=-1, keepdims=True)
            s1 = jnp.sum(jnp.where(first, 0.0, prod), axis=-1, keepdims=True)
            tot = tot + al[p] * jnp.where(first, s0, s1)
        for p in range(3):
            dd_ref[:, p * 128:(p + 1) * 128] = -al[p] * tot
        dc_ref[...] = dmix[:, ATTN_W:ATTN_W + CONV_W]
        dl_ref[...] = dmix[:, ATTN_W + CONV_W:D_MODEL]
        part = _dot_tn(mix_ref[...], dxb)

        @pl.when(i == 0)
        def _():
            acc_ref[...] = part

        @pl.when(i != 0)
        def _():
            acc_ref[...] += part

        @pl.when(i == nk - 1)
        def _():
            dw_ref[...] = acc_ref[...].astype(BF16)

    return pl.pallas_call(
        body, name="outproj_bwd", grid=(nk,),
        in_specs=[pl.BlockSpec((tm, D_MODEL), lambda i: (i, 0)),
                  pl.BlockSpec((tm, D_MODEL), lambda i: (i, 0)),
                  _resident((D_MODEL, D_MODEL)),
                  pl.BlockSpec((tm, ATTN_W), lambda i: (i, 0)),
                  pl.BlockSpec((tm, ATTN_W), lambda i: (i, 0))],
        out_specs=[pl.BlockSpec((tm, ATTN_W), lambda i: (i, 0)),
                   pl.BlockSpec((tm, ATTN_W), lambda i: (i, 0)),
                   pl.BlockSpec((tm, CONV_W), lambda i: (i, 0)),
                   pl.BlockSpec((tm, LRU_W), lambda i: (i, 0)),
                   _resident((D_MODEL, D_MODEL))],
        out_shape=[jax.ShapeDtypeStruct((T, ATTN_W), F32), jax.ShapeDtypeStruct((T, ATTN_W), F32),
                   jax.ShapeDtypeStruct((T, CONV_W), F32), jax.ShapeDtypeStruct((T, LRU_W), F32),
                   jax.ShapeDtypeStruct((D_MODEL, D_MODEL), BF16)],
        scratch_shapes=[pltpu.VMEM((D_MODEL, D_MODEL), F32)],
        compiler_params=_params("arbitrary"),
    )(dx1, mix, w, o, lse)


DZ_COLS = ((0, ATTN_W), (ATTN_W, 2 * ATTN_W), (2 * ATTN_W, QKV_W), (QKV_W, QKV_W + CONV_IN_W), (QKV_W + CONV_IN_W, IN_COLS))


def _inproj_bwd(dz_parts, h, w, x2d, g, dx1):
    T = x2d.shape[0]
    tm = ROW_TILE
    nk = T // tm
    n_parts = len(DZ_COLS)

    def body(*refs):
        dz_refs = refs[:n_parts]
        h_ref, w_ref, x_ref, g_ref, dx1_ref, dx_ref, dg_ref, dw_ref, acc_ref = refs[n_parts:]
        i = pl.program_id(0)
        dz = [r[...].astype(BF16) for r in dz_refs]
        dh = _dot_nt(dz[0], w_ref[:, DZ_COLS[0][0]:DZ_COLS[0][1]])
        for part, (lo, hi) in zip(dz[1:], DZ_COLS[1:]):
            dh = dh + _dot_nt(part, w_ref[:, lo:hi])
        gv = g_ref[...]
        _, xhat, rstd = _rms_fwd(x_ref[...], gv)
        dx, dg = _rms_bwd(dh, xhat, rstd, gv)
        dx_ref[...] = dx1_ref[...] + dx
        hb = h_ref[...]

        @pl.when(i == 0)
        def _():
            dg_ref[...] = jnp.zeros_like(dg_ref)
            acc_ref[...] = jnp.zeros_like(acc_ref)

        dg_ref[...] += dg
        for part, (lo, hi) in zip(dz, DZ_COLS):
            acc_ref[:, lo:hi] += _dot_tn(hb, part)

        @pl.when(i == nk - 1)
        def _():
            dw_ref[...] = acc_ref[...].astype(BF16)

    rows = lambda width: pl.BlockSpec((tm, width), lambda i: (i, 0))
    return pl.pallas_call(
        body, name="inproj_bwd", grid=(nk,),
        in_specs=[rows(hi - lo) for lo, hi in DZ_COLS] + [
            rows(D_MODEL), _resident((D_MODEL, IN_COLS)), rows(D_MODEL), pl.BlockSpec((1, D_MODEL), lambda i: (0, 0)),
            rows(D_MODEL)],
        out_specs=[rows(D_MODEL), pl.BlockSpec((1, D_MODEL), lambda i: (0, 0)), _resident((D_MODEL, IN_COLS))],
        out_shape=[jax.ShapeDtypeStruct((T, D_MODEL), F32), jax.ShapeDtypeStruct((1, D_MODEL), F32),
                   jax.ShapeDtypeStruct((D_MODEL, IN_COLS), BF16)],
        scratch_shapes=[pltpu.VMEM((D_MODEL, IN_COLS), F32)],
        compiler_params=_params("arbitrary"),
    )(*dz_parts, h, w, x2d, g, dx1)


def _alibi_coef():
    slopes = 2.0 ** (-8.0 * np.arange(1, 7) / 6)
    return jnp.asarray((slopes.reshape(3, 2) * np.asarray(ATTN_DILATIONS)[:, None]).astype(np.float32))


def _unit_rows(u, d):
    nb = N_UNITS // d
    r, n = u // nb, u % nb
    span = ATTN_BLOCK * d

    def rows(block):
        start = block * span + r
        return pl.ds(pl.multiple_of(start, ATTN_BLOCK), ATTN_BLOCK) if d == 1 else pl.ds(start, ATTN_BLOCK, stride=d)

    return rows(n), rows(jnp.maximum(n - 1, 0)), rows(jnp.minimum(n + 1, nb - 1)), n > 0, n + 1 < nb


def _per_pattern(fn):
    for p, d in enumerate(ATTN_DILATIONS):
        pl.when(pl.program_id(1) == p)(functools.partial(fn, p, d))


def _attn_col(offset):
    return pl.BlockSpec((None, SEQ_LEN, 128), lambda b, p: (b, 0, p + offset))


def _attn_masks():
    qi = lax.broadcasted_iota(jnp.int32, (ATTN_BLOCK, ATTN_BLOCK), 0)
    kj = lax.broadcasted_iota(jnp.int32, (ATTN_BLOCK, ATTN_BLOCK), 1)
    d_own = qi - kj
    return d_own, d_own >= 0, d_own <= 0


def _attn_fwd(qkv):
    B = qkv.shape[0]

    def body(coef_ref, q_ref, k_ref, v_ref, o_ref, lse_ref, o_tmp, lse_tmp):
        d_own, ok_own, in_band = _attn_masks()
        d_own_f = d_own.astype(F32)

        def pattern(p, d):
            def unit(u, carry):
                own, prev, _, has_prev, _ = _unit_rows(u, d)
                ok_prev = in_band & has_prev
                q, k, kp = (ref[rows, :].astype(BF16) for ref, rows in ((q_ref, own), (k_ref, own), (k_ref, prev)))
                v, vp = v_ref[own, :].astype(BF16), v_ref[prev, :].astype(BF16)
                for j in range(2):
                    sl = slice(HEAD_DIM * j, HEAD_DIM * (j + 1))
                    c = coef_ref[p, j]
                    s_o = jnp.where(ok_own, _dot_nt(q[:, sl], k[:, sl]) * 0.125 - c * d_own_f, NEG_BIG)
                    s_p = jnp.where(ok_prev, _dot_nt(q[:, sl], kp[:, sl]) * 0.125 - c * (d_own_f + ATTN_BLOCK), NEG_BIG)
                    m = jnp.maximum(jnp.max(s_o, axis=-1, keepdims=True), jnp.max(s_p, axis=-1, keepdims=True))
                    e_o, e_p = jnp.exp(s_o - m), jnp.exp(s_p - m)
                    l = jnp.sum(e_o, axis=-1, keepdims=True) + jnp.sum(e_p, axis=-1, keepdims=True)
                    acc = _dot(e_o.astype(BF16), v[:, sl]) + _dot(e_p.astype(BF16), vp[:, sl])
                    o_tmp[:, sl] = acc / l
                    lse_tmp[:, sl] = jnp.broadcast_to(m + jnp.log(l), (ATTN_BLOCK, HEAD_DIM))
                o_ref[own, :] = o_tmp[...]
                lse_ref[own, :] = lse_tmp[...]
                return carry

            lax.fori_loop(0, N_UNITS, unit, 0)

        _per_pattern(pattern)

    shape = jax.ShapeDtypeStruct((B, SEQ_LEN, ATTN_W), F32)
    return pl.pallas_call(
        body, name="attn_fwd", grid=(B, 3),
        in_specs=[pl.BlockSpec(memory_space=pltpu.SMEM), _attn_col(0), _attn_col(3), _attn_col(6)],
        out_specs=[_attn_col(0), _attn_col(0)],
        out_shape=[shape, shape],
        scratch_shapes=[pltpu.VMEM((ATTN_BLOCK, 128), F32)] * 2,
        compiler_params=_params("parallel", "parallel"),
    )(_alibi_coef(), qkv, qkv, qkv)


def _attn_bwd(qkv, do, lse, dd):
    B = qkv.shape[0]

    def body(coef_ref, q_ref, k_ref, v_ref, do_ref, lse_ref, dd_ref, dq_ref, dk_ref, dv_ref, dq_tmp, dk_tmp, dv_tmp):
        d_own, ok_own, in_band = _attn_masks()
        d_own_f = d_own.astype(F32)

        def probs(q, k, ok, dist, lse_col, c):
            s = _dot_nt(q, k) * 0.125 - c * dist
            return jnp.where(ok, jnp.exp(jnp.where(ok, s, NEG_BIG) - lse_col), 0.0)

        def pattern(p, d):
            def unit(u, carry):
                own, prev, nxt, has_prev, has_next = _unit_rows(u, d)
                ok_prev, ok_next = in_band & has_prev, in_band & has_next
                bf = lambda ref, rows: ref[rows, :].astype(BF16)
                q_a, qn_a, k_a, kp_a = bf(q_ref, own), bf(q_ref, nxt), bf(k_ref, own), bf(k_ref, prev)
                v_a, vp_a, do_a, don_a = bf(v_ref, own), bf(v_ref, prev), bf(do_ref, own), bf(do_ref, nxt)
                lse_a, lsen_a, dd_a, ddn_a = lse_ref[own, :], lse_ref[nxt, :], dd_ref[own, :], dd_ref[nxt, :]
                for j in range(2):
                    sl = slice(HEAD_DIM * j, HEAD_DIM * (j + 1))
                    col = slice(HEAD_DIM * j, HEAD_DIM * j + 1)
                    c = coef_ref[p, j]
                    q, qn, k, kp, v, vp = q_a[:, sl], qn_a[:, sl], k_a[:, sl], kp_a[:, sl], v_a[:, sl], vp_a[:, sl]
                    do_h, don_h = do_a[:, sl], don_a[:, sl]
                    p_o = probs(q, k, ok_own, d_own_f, lse_a[:, col], c)
                    ds_o = (p_o * (_dot_nt(do_h, v) + dd_a[:, col])).astype(BF16)
                    p_p = probs(q, kp, ok_prev, d_own_f + ATTN_BLOCK, lse_a[:, col], c)
                    ds_p = (p_p * (_dot_nt(do_h, vp) + dd_a[:, col])).astype(BF16)
                    p_n = probs(qn, k, ok_next, d_own_f + ATTN_BLOCK, lsen_a[:, col], c)
                    ds_n = (p_n * (_dot_nt(don_h, v) + ddn_a[:, col])).astype(BF16)
                    dq_tmp[:, sl] = (_dot(ds_o, k) + _dot(ds_p, kp)) * 0.125
                    dk_tmp[:, sl] = (_dot_tn(ds_o, q) + _dot_tn(ds_n, qn)) * 0.125
                    dv_tmp[:, sl] = _dot_tn(p_o.astype(BF16), do_h) + _dot_tn(p_n.astype(BF16), don_h)
                dq_ref[own, :] = dq_tmp[...]
                dk_ref[own, :] = dk_tmp[...]
                dv_ref[own, :] = dv_tmp[...]
                return carry

            lax.fori_loop(0, N_UNITS, unit, 0)

        _per_pattern(pattern)

    shape = jax.ShapeDtypeStruct((B, SEQ_LEN, ATTN_W), F32)
    return pl.pallas_call(
        body, name="attn_bwd", grid=(B, 3),
        in_specs=[pl.BlockSpec(memory_space=pltpu.SMEM), _attn_col(0), _attn_col(3), _attn_col(6), _attn_col(0), _attn_col(0),
                  _attn_col(0)],
        out_specs=[_attn_col(0)] * 3,
        out_shape=[shape] * 3,
        scratch_shapes=[pltpu.VMEM((ATTN_BLOCK, 128), F32)] * 3,
        compiler_params=_params("parallel", "parallel"),
    )(_alibi_coef(), qkv, qkv, qkv, do, lse, dd)


def _for_chunks(n_rows, fn, chunk=SEQ_CHUNK):
    def step(c, carry):
        fn(pl.multiple_of(c * chunk, chunk))
        return carry

    lax.fori_loop(0, n_rows // chunk, step, 0)


def _shift_down(win, s, rows):
    lead = win.shape[0] - rows
    if s == 0:
        return win[lead:]
    if s % 8 == 0:
        return win[lead - s:lead - s + rows]
    q, r = divmod(s, 8)
    rolled = pltpu.roll(win, r, 0)
    return rolled[lead - 8 * q:lead - 8 * q + rows]


def _shift_up(win, s, rows):
    if s % 8 == 0:
        return win[s:s + rows]
    q, r = divmod(s, 8)
    rolled = pltpu.roll(win, win.shape[0] - r, 0)
    return rolled[8 * q:8 * q + rows]


CONV_PAD = 32


def _ln_silu(c, lg, lb):
    mu = jnp.mean(c, axis=-1, keepdims=True)
    cc = c - mu
    rstd = lax.rsqrt(jnp.mean(cc * cc, axis=-1, keepdims=True) + LN_EPS)
    nrm = cc * rstd
    v = nrm * lg + lb
    sg = _sigmoid(v)
    return v * sg, nrm, rstd, v, sg


def _conv_fwd(ci, w, b, lg, lb):
    B, S, _ = ci.shape
    CH = SEQ_CHUNK

    def body(ci_ref, w_ref, b_ref, lg_ref, lb_ref, y_ref, c_ref, pad_ref):
        pad_ref[0:CONV_PAD, :] = jnp.zeros((CONV_PAD, CONV_W), F32)

        def glu(base):
            blk = ci_ref[pl.ds(base, CH), :]
            pad_ref[pl.ds(CONV_PAD + base, CH), :] = blk[:, 0:CONV_W] * _sigmoid(blk[:, CONV_W:])

        _for_chunks(S, glu)

        def conv(base):
            win = pad_ref[pl.ds(base, CH + CONV_PAD), :]
            acc = jnp.broadcast_to(b_ref[...], (CH, CONV_W))
            for k in range(CONV_TAPS):
                acc = acc + w_ref[k:k + 1, :] * _shift_down(win, CONV_TAPS - 1 - k, CH)
            c_ref[pl.ds(base, CH), :] = acc
            y, _, _, _, _ = _ln_silu(acc, lg_ref[...], lb_ref[...])
            y_ref[pl.ds(base, CH), :] = y.astype(BF16)

        _for_chunks(S, conv)

    vec = pl.BlockSpec((1, CONV_W), lambda i: (0, 0))
    return pl.pallas_call(
        body, name="conv_fwd", grid=(B,),
        in_specs=[pl.BlockSpec((None, S, CONV_IN_W), lambda i: (i, 0, 0)),
                  pl.BlockSpec((CONV_TAPS, CONV_W), lambda i: (0, 0)), vec, vec, vec],
        out_specs=[pl.BlockSpec((None, S, CONV_W), lambda i: (i, 0, 0)),
                   pl.BlockSpec((None, S, CONV_W), lambda i: (i, 0, 0))],
        out_shape=[jax.ShapeDtypeStruct((B, S, CONV_W), BF16), jax.ShapeDtypeStruct((B, S, CONV_W), F32)],
        scratch_shapes=[pltpu.VMEM((S + CONV_PAD, CONV_W), F32)],
        compiler_params=_params("parallel"),
    )(ci, w, b, lg, lb)


def _conv_bwd(ci, cpre, dy, w, lg, lb):
    B, S, _ = ci.shape
    CH = SEQ_CHUNK

    def body(ci_ref, c_ref, dy_ref, w_ref, lg_ref, lb_ref, dci_ref, dw_ref, db_ref, dlg_ref, dlb_ref, upad_ref, dcpad_ref,
             dwacc_ref):
        @pl.when(pl.program_id(0) == 0)
        def _():
            dw_ref[...] = jnp.zeros_like(dw_ref)
            db_ref[...] = jnp.zeros_like(db_ref)
            dlg_ref[...] = jnp.zeros_like(dlg_ref)
            dlb_ref[...] = jnp.zeros_like(dlb_ref)

        upad_ref[0:CONV_PAD, :] = jnp.zeros((CONV_PAD, CONV_W), F32)
        dcpad_ref[S:S + CONV_PAD, :] = jnp.zeros((CONV_PAD, CONV_W), F32)
        dwacc_ref[...] = jnp.zeros_like(dwacc_ref)

        def norm_bwd(base):
            blk = ci_ref[pl.ds(base, CH), :]
            upad_ref[pl.ds(CONV_PAD + base, CH), :] = blk[:, 0:CONV_W] * _sigmoid(blk[:, CONV_W:])
            lgv = lg_ref[...]
            _, nrm, rstd, v, sg = _ln_silu(c_ref[pl.ds(base, CH), :], lgv, lb_ref[...])
            dv = dy_ref[pl.ds(base, CH), :] * (sg * (1.0 + v * (1.0 - sg)))
            dlg_ref[...] += jnp.sum(dv * nrm, axis=0, keepdims=True)
            dlb_ref[...] += jnp.sum(dv, axis=0, keepdims=True)
            dn = dv * lgv
            dc = rstd * (dn - jnp.mean(dn, axis=-1, keepdims=True) - nrm * jnp.mean(dn * nrm, axis=-1, keepdims=True))
            dcpad_ref[pl.ds(base, CH), :] = dc
            db_ref[...] += jnp.sum(dc, axis=0, keepdims=True)

        _for_chunks(S, norm_bwd)

        def conv_bwd(base):
            dwin = dcpad_ref[pl.ds(base, CH + CONV_PAD), :]
            uwin = upad_ref[pl.ds(base, CH + CONV_PAD), :]
            dc = dwin[0:CH]
            du = jnp.zeros((CH, CONV_W), F32)
            for k in range(CONV_TAPS):
                du = du + w_ref[k:k + 1, :] * _shift_up(dwin, CONV_TAPS - 1 - k, CH)
                prod = dc * _shift_down(uwin, CONV_TAPS - 1 - k, CH)
                dwacc_ref[8 * k:8 * k + 8, :] += jnp.sum(prod.reshape(CH // 8, 8, CONV_W), axis=0)
            blk = ci_ref[pl.ds(base, CH), :]
            a, sg = blk[:, 0:CONV_W], _sigmoid(blk[:, CONV_W:])
            dci_ref[pl.ds(base, CH), 0:CONV_W] = (du * sg).astype(BF16)
            dci_ref[pl.ds(base, CH), CONV_W:] = (du * a * sg * (1.0 - sg)).astype(BF16)

        _for_chunks(S, conv_bwd)
        for k in range(CONV_TAPS):
            dw_ref[k:k + 1, :] += jnp.sum(dwacc_ref[8 * k:8 * k + 8, :], axis=0, keepdims=True)

    vec = pl.BlockSpec((1, CONV_W), lambda i: (0, 0))
    mat = pl.BlockSpec((CONV_TAPS, CONV_W), lambda i: (0, 0))
    seq = lambda width: pl.BlockSpec((None, S, width), lambda i: (i, 0, 0))
    return pl.pallas_call(
        body, name="conv_bwd", grid=(B,),
        in_specs=[seq(CONV_IN_W), seq(CONV_W), seq(CONV_W), mat, vec, vec],
        out_specs=[seq(CONV_IN_W), mat, vec, vec, vec],
        out_shape=[jax.ShapeDtypeStruct((B, S, CONV_IN_W), BF16), jax.ShapeDtypeStruct((CONV_TAPS, CONV_W), F32),
                   jax.ShapeDtypeStruct((1, CONV_W), F32), jax.ShapeDtypeStruct((1, CONV_W), F32),
                   jax.ShapeDtypeStruct((1, CONV_W), F32)],
        scratch_shapes=[pltpu.VMEM((S + CONV_PAD, CONV_W), F32), pltpu.VMEM((S + CONV_PAD, CONV_W), F32),
                        pltpu.VMEM((8 * CONV_TAPS, CONV_W), F32)],
        compiler_params=_params("arbitrary"),
    )(ci, cpre, dy, w, lg, lb)


SCAN_SHIFTS = tuple(1 << e for e in range(11))


def _prev8(ref, base, cols, fill):
    start = pl.multiple_of(jnp.maximum(base - 8, 0), 8)
    return jnp.where(base > 0, ref[pl.ds(start, 8), cols], fill)


def _next8(ref, base, rows, total, cols, fill):
    start = pl.multiple_of(jnp.minimum(base + rows, total - 8), 8)
    return jnp.where(base + rows < total, ref[pl.ds(start, 8), cols], fill)


ALL = slice(None)
LRU_X = slice(LRU_W, LRU_IN_W)
LRU_GATE = slice(0, LRU_W)


def _lru_conv(li_ref, base, rows, cw_ref, cb_ref):
    win = jnp.concatenate([_prev8(li_ref, base, LRU_X, 0.0), li_ref[pl.ds(base, rows), LRU_X]], axis=0)
    u = jnp.broadcast_to(cb_ref[...], (rows, LRU_W))
    for k in range(LRU_TAPS):
        u = u + cw_ref[k:k + 1, :] * _shift_down(win, LRU_TAPS - 1 - k, rows)
    return u, win


def _lru_gates(u, wa_ref, ba_ref, wx_ref, bx_ref, sp):
    ub = u.astype(BF16)
    r = _sigmoid(_dot(ub, wa_ref[...]) + ba_ref[...])
    i = _sigmoid(_dot(ub, wx_ref[...]) + bx_ref[...])
    la = (-LRU_C) * r * sp
    a = jnp.exp(la)
    mult = jnp.sqrt(-_expm1(2.0 * la))
    return ub, r, i, a, mult


def _scan_forward(bufs, S, CH):
    for n, s in enumerate(SCAN_SHIFTS):
        (sa, sb), (da, db) = bufs[n % 2], bufs[(n + 1) % 2]

        def step(base, s=s, sa=sa, sb=sb, da=da, db=db):
            a, b = sa[pl.ds(base, CH), :], sb[pl.ds(base, CH), :]
            if s < 8:
                a_s = _shift_down(jnp.concatenate([_prev8(sa, base, ALL, 1.0), a], axis=0), s, CH)
                b_s = _shift_down(jnp.concatenate([_prev8(sb, base, ALL, 0.0), b], axis=0), s, CH)
            elif s < CH:
                start = pl.multiple_of(jnp.maximum(base - s, 0), 8)
                a_s = jnp.concatenate([jnp.where(base > 0, sa[pl.ds(start, s), :], 1.0), a[0:CH - s]], axis=0)
                b_s = jnp.concatenate([jnp.where(base > 0, sb[pl.ds(start, s), :], 0.0), b[0:CH - s]], axis=0)
            else:
                start = pl.multiple_of(jnp.maximum(base - s, 0), 8)
                a_s = jnp.where(base < s, 1.0, sa[pl.ds(start, CH), :])
                b_s = jnp.where(base < s, 0.0, sb[pl.ds(start, CH), :])
            db[pl.ds(base, CH), :] = a * b_s + b
            da[pl.ds(base, CH), :] = a * a_s

        _for_chunks(S, step, CH)
    return len(SCAN_SHIFTS) % 2


def _scan_backward(bufs, S, CH):
    for n, s in enumerate(SCAN_SHIFTS):
        (sa, sb), (da, db) = bufs[n % 2], bufs[(n + 1) % 2]

        def step(base, s=s, sa=sa, sb=sb, da=da, db=db):
            a, b = sa[pl.ds(base, CH), :], sb[pl.ds(base, CH), :]
            if s < 8:
                a_s = _shift_up(jnp.concatenate([a, _next8(sa, base, CH, S, ALL, 1.0)], axis=0), s, CH)
                b_s = _shift_up(jnp.concatenate([b, _next8(sb, base, CH, S, ALL, 0.0)], axis=0), s, CH)
            elif s < CH:
                start = pl.multiple_of(jnp.minimum(base + CH, S - s), 8)
                more = base + CH < S
                a_s = jnp.concatenate([a[s:CH], jnp.where(more, sa[pl.ds(start, s), :], 1.0)], axis=0)
                b_s = jnp.concatenate([b[s:CH], jnp.where(more, sb[pl.ds(start, s), :], 0.0)], axis=0)
            else:
                start = pl.multiple_of(jnp.minimum(base + s, S - CH), 8)
                a_s = jnp.where(base + s >= S, 1.0, sa[pl.ds(start, CH), :])
                b_s = jnp.where(base + s >= S, 0.0, sb[pl.ds(start, CH), :])
            db[pl.ds(base, CH), :] = a * b_s + b
            da[pl.ds(base, CH), :] = a * a_s

        _for_chunks(S, step, CH)
    return len(SCAN_SHIFTS) % 2


def _lru_fwd(li, cw, cb, wa, ba, wx, bx, lam):
    B, S, _ = li.shape
    CH = SEQ_CHUNK

    def body(li_ref, cw_ref, cb_ref, wa_ref, ba_ref, wx_ref, bx_ref, lam_ref, y_ref, h_ref, a0, b0, a1, b1):
        sp = _softplus(-lam_ref[...])

        def gates(base):
            u, _ = _lru_conv(li_ref, base, CH, cw_ref, cb_ref)
            _, _, i, a, mult = _lru_gates(u, wa_ref, ba_ref, wx_ref, bx_ref, sp)
            a0[pl.ds(base, CH), :] = a
            b0[pl.ds(base, CH), :] = mult * (i * u)

        _for_chunks(S, gates)
        bufs = ((a0, b0), (a1, b1))
        hb = bufs[_scan_forward(bufs, S, CH)][1]

        def out(base):
            h = hb[pl.ds(base, CH), :]
            h_ref[pl.ds(base, CH), :] = h
            gl, _ = _gelu(li_ref[pl.ds(base, CH), LRU_GATE])
            y_ref[pl.ds(base, CH), :] = (gl * h).astype(BF16)

        _for_chunks(S, out)

    vec = pl.BlockSpec((1, LRU_W), lambda i: (0, 0))
    mat = pl.BlockSpec((LRU_W, LRU_W), lambda i: (0, 0))
    seq = lambda width: pl.BlockSpec((None, S, width), lambda i: (i, 0, 0))
    return pl.pallas_call(
        body, name="lru_fwd", grid=(B,),
        in_specs=[seq(LRU_IN_W), pl.BlockSpec((LRU_TAPS, LRU_W), lambda i: (0, 0)), vec, mat, vec, mat, vec, vec],
        out_specs=[seq(LRU_W), seq(LRU_W)],
        out_shape=[jax.ShapeDtypeStruct((B, S, LRU_W), BF16), jax.ShapeDtypeStruct((B, S, LRU_W), F32)],
        scratch_shapes=[pltpu.VMEM((S, LRU_W), F32)] * 4,
        compiler_params=_params("parallel"),
    )(li, cw, cb, wa, ba, wx, bx, lam)


def _lru_bwd(li, hs, dy, cw, cb, wa, ba, wx, bx, lam):
    B, S, _ = li.shape
    CH = SEQ_CHUNK

    def body(li_ref, hs_ref, dy_ref, cw_ref, cb_ref, wa_ref, ba_ref, wx_ref, bx_ref, lam_ref,
             dli_ref, dcw_ref, dcb_ref, dwa_ref, dba_ref, dwx_ref, dbx_ref, dlam_ref, a0, b0, a1, b1, u_s, du_s):
        @pl.when(pl.program_id(0) == 0)
        def _():
            for ref in (dcw_ref, dcb_ref, dwa_ref, dba_ref, dwx_ref, dbx_ref, dlam_ref):
                ref[...] = jnp.zeros_like(ref)

        lam_v = lam_ref[...]
        sp = _softplus(-lam_v)
        dsp_dlam = -_sigmoid(-lam_v)

        def gates(base):
            u, _ = _lru_conv(li_ref, base, CH, cw_ref, cb_ref)
            _, _, _, a, _ = _lru_gates(u, wa_ref, ba_ref, wx_ref, bx_ref, sp)
            gl, _ = _gelu(li_ref[pl.ds(base, CH), LRU_GATE])
            u_s[pl.ds(base, CH), :] = u
            a0[pl.ds(base, CH), :] = a
            b0[pl.ds(base, CH), :] = a * (dy_ref[pl.ds(base, CH), :] * gl)

        _for_chunks(S, gates)
        bufs = ((a0, b0), (a1, b1))
        eb = bufs[_scan_backward(bufs, S, CH)][1]

        def grads(base):
            e = eb[pl.ds(base, CH), :]
            e_next = _shift_up(jnp.concatenate([e, _next8(eb, base, CH, S, ALL, 0.0)], axis=0), 1, CH)
            gate = li_ref[pl.ds(base, CH), LRU_GATE]
            gl, th = _gelu(gate)
            dy = dy_ref[pl.ds(base, CH), :]
            g = dy * gl + e_next
            h = hs_ref[pl.ds(base, CH), :]
            h_prev = _shift_down(jnp.concatenate([_prev8(hs_ref, base, ALL, 0.0), h], axis=0), 1, CH)
            dli_ref[pl.ds(base, CH), LRU_GATE] = (dy * h * _gelu_grad(gate, th)).astype(BF16)
            u = u_s[pl.ds(base, CH), :]
            ub, r, i, a, mult = _lru_gates(u, wa_ref, ba_ref, wx_ref, bx_ref, sp)
            da = g * h_prev
            dmult = g * (i * u)
            di = g * mult * u
            dla = da * a - dmult * (a * a) / mult
            dlam_ref[...] += dsp_dlam * jnp.sum(dla * ((-LRU_C) * r), axis=0, keepdims=True)
            dpa = (dla * ((-LRU_C) * sp)) * r * (1.0 - r)
            dpx = di * i * (1.0 - i)
            dpab, dpxb = dpa.astype(BF16), dpx.astype(BF16)
            dwa_ref[...] += _dot_tn(ub, dpab)
            dwx_ref[...] += _dot_tn(ub, dpxb)
            dba_ref[...] += jnp.sum(dpa, axis=0, keepdims=True)
            dbx_ref[...] += jnp.sum(dpx, axis=0, keepdims=True)
            du = g * mult * i + _dot_nt(dpab, wa_ref[...]) + _dot_nt(dpxb, wx_ref[...])
            du_s[pl.ds(base, CH), :] = du
            dcb_ref[...] += jnp.sum(du, axis=0, keepdims=True)

        _for_chunks(S, grads)

        def conv_bwd(base):
            du = du_s[pl.ds(base, CH), :]
            dwin = jnp.concatenate([du, _next8(du_s, base, CH, S, ALL, 0.0)], axis=0)
            xwin = jnp.concatenate([_prev8(li_ref, base, LRU_X, 0.0), li_ref[pl.ds(base, CH), LRU_X]], axis=0)
            dx = jnp.zeros((CH, LRU_W), F32)
            for k in range(LRU_TAPS):
                dx = dx + cw_ref[k:k + 1, :] * _shift_up(dwin, LRU_TAPS - 1 - k, CH)
                dcw_ref[k:k + 1, :] += jnp.sum(du * _shift_down(xwin, LRU_TAPS - 1 - k, CH), axis=0, keepdims=True)
            dli_ref[pl.ds(base, CH), LRU_X] = dx.astype(BF16)

        _for_chunks(S, conv_bwd)

    vec = pl.BlockSpec((1, LRU_W), lambda i: (0, 0))
    mat = pl.BlockSpec((LRU_W, LRU_W), lambda i: (0, 0))
    taps = pl.BlockSpec((LRU_TAPS, LRU_W), lambda i: (0, 0))
    seq = lambda width: pl.BlockSpec((None, S, width), lambda i: (i, 0, 0))
    vec_shape = jax.ShapeDtypeStruct((1, LRU_W), F32)
    mat_shape = jax.ShapeDtypeStruct((LRU_W, LRU_W), F32)
    return pl.pallas_call(
        body, name="lru_bwd", grid=(B,),
        in_specs=[seq(LRU_IN_W), seq(LRU_W), seq(LRU_W), taps, vec, mat, vec, mat, vec, vec],
        out_specs=[seq(LRU_IN_W), taps, vec, mat, vec, mat, vec, vec],
        out_shape=[jax.ShapeDtypeStruct((B, S, LRU_IN_W), BF16), jax.ShapeDtypeStruct((LRU_TAPS, LRU_W), F32),
                   vec_shape, mat_shape, vec_shape, mat_shape, vec_shape, vec_shape],
        scratch_shapes=[pltpu.VMEM((S, LRU_W), F32)] * 6,
        compiler_params=_params("arbitrary"),
    )(li, hs, dy, cw, cb, wa, ba, wx, bx, lam)


MESH = pl.DeviceIdType.MESH
HBM_SPEC = pl.BlockSpec(memory_space=pltpu.HBM)


def _slot(ref, p):
    return ref.at[p]


def _row_block(rows):
    return lambda ref, p: ref.at[pl.ds(p * rows, rows), :]


def _col_block(cols):
    return lambda ref, p: ref.at[:, pl.ds(p * cols, cols)]


def _all_gather(blocks, out_shapes, places, name):
    n = len(blocks)

    def body(*refs):
        x_refs, out_refs = refs[:n], refs[n:2 * n]
        send_sems, recv_sems, local_sems = refs[2 * n:]
        x, y, c = lax.axis_index("x"), lax.axis_index("y"), lax.axis_index("c")
        me, sibling = (x, y, c), (x, y, 1 - c)
        chips = [(1 - x, y), (x, 1 - y), (1 - x, 1 - y)]

        def place(a, dev):
            return places[a](out_refs[a], 4 * dev[0] + 2 * dev[1] + dev[2])

        def copy(a, k, blk, to, src=None):
            return pltpu.make_async_remote_copy(
                src_ref=place(a, blk) if src is None else src, dst_ref=place(a, blk),
                send_sem=send_sems.at[a, k], recv_sem=recv_sems.at[a, k], device_id=to, device_id_type=MESH)

        mine = [pltpu.make_async_copy(x_refs[a], place(a, me), local_sems.at[a]) for a in range(n)]
        first = [copy(a, 0, me, sibling, src=x_refs[a]) for a in range(n)]
        first += [copy(a, 1 + j, me, (*chip, c), src=x_refs[a]) for j, chip in enumerate(chips) for a in range(n)]
        for cp in mine + first:
            cp.start()
        passed = []
        for j, chip in enumerate(chips):
            for a in range(n):
                copy(a, 1 + j, (*chip, c), me).wait_recv()
                passed.append(copy(a, 4 + j, (*chip, c), sibling))
                passed[-1].start()
        for a in range(n):
            copy(a, 0, sibling, me).wait_recv()
        for j, chip in enumerate(chips):
            for a in range(n):
                copy(a, 4 + j, (*chip, 1 - c), me).wait_recv()
        for cp in first + passed:
            cp.wait_send()
        for cp in mine:
            cp.wait()

    return pl.pallas_call(
        body, name=name, out_shape=out_shapes,
        in_specs=[HBM_SPEC] * n, out_specs=[HBM_SPEC] * n,
        scratch_shapes=[pltpu.SemaphoreType.DMA((n, 7)), pltpu.SemaphoreType.DMA((n, 7)), pltpu.SemaphoreType.DMA((n,))],
    )(*blocks)


def _all_to_all(sources, takes, puts, out_shapes, name):
    n, n_out = len(sources), len(out_shapes)

    def body(*refs):
        src_refs, out_refs = refs[:n], refs[n:n + n_out]
        send_sems, recv_sems, local_sems = refs[n + n_out:]
        x, y, c = lax.axis_index("x"), lax.axis_index("y"), lax.axis_index("c")
        me = 4 * x + 2 * y + c

        def landing(i, q):
            out_idx, fn = puts[i]
            return fn(out_refs[out_idx], q)

        mine = [pltpu.make_async_copy(takes[i](src_refs[i], me), landing(i, me), local_sems.at[i]) for i in range(n)]
        copies = []
        for k in range(1, N_DEV):
            px, py, pc = x ^ ((k >> 2) & 1), y ^ ((k >> 1) & 1), c ^ (k & 1)
            peer = 4 * px + 2 * py + pc
            for i in range(n):
                copies.append(pltpu.make_async_remote_copy(
                    src_ref=takes[i](src_refs[i], peer), dst_ref=landing(i, me), send_sem=send_sems.at[i, k - 1],
                    recv_sem=recv_sems.at[i, k - 1], device_id=(px, py, pc), device_id_type=MESH))
        for cp in mine + copies:
            cp.start()
        for cp in copies:
            cp.wait_recv()
        for cp in copies:
            cp.wait_send()
        for cp in mine:
            cp.wait()

    return pl.pallas_call(
        body, name=name, out_shape=out_shapes,
        in_specs=[HBM_SPEC] * n, out_specs=[HBM_SPEC] * n_out,
        scratch_shapes=[pltpu.SemaphoreType.DMA((n, 7)), pltpu.SemaphoreType.DMA((n, 7)), pltpu.SemaphoreType.DMA((n,))],
    )(*sources)


def _sum_slots(parts, rows_per_step, name):
    _, R, C = parts.shape

    def body(p_ref, o_ref):
        acc = p_ref[0].astype(F32)
        for q in range(1, N_DEV):
            acc = acc + p_ref[q].astype(F32)
        o_ref[...] = acc

    return pl.pallas_call(
        body, name=name, grid=(R // rows_per_step,),
        in_specs=[pl.BlockSpec((N_DEV, rows_per_step, C), lambda i: (0, i, 0))],
        out_specs=pl.BlockSpec((rows_per_step, C), lambda i: (i, 0)),
        out_shape=jax.ShapeDtypeStruct((R, C), F32),
        compiler_params=_params("parallel"),
    )(parts)


def _adamw(w, g, m, v, rows_per_step, name):
    R, C = w.shape
    c1 = 1.0 - ADAM_B1 ** ADAM_STEP
    c2 = 1.0 - ADAM_B2 ** ADAM_STEP

    def body(w_ref, g_ref, m_ref, v_ref, d_ref, nm_ref, nv_ref):
        gv = g_ref[...]
        nm = ADAM_B1 * m_ref[...] + (1.0 - ADAM_B1) * gv
        nv = ADAM_B2 * v_ref[...] + (1.0 - ADAM_B2) * (gv * gv)
        nm_ref[...] = nm
        nv_ref[...] = nv
        d_ref[...] = (-ADAM_LR) * ((nm / c1) / (jnp.sqrt(nv / c2) + ADAM_EPS) + ADAM_WD * w_ref[...])

    spec = pl.BlockSpec((rows_per_step, C), lambda i: (i, 0))
    shape = jax.ShapeDtypeStruct((R, C), F32)
    return pl.pallas_call(
        body, name=name, grid=(R // rows_per_step,),
        in_specs=[spec] * 4, out_specs=[spec] * 3, out_shape=[shape] * 3,
        compiler_params=_params("parallel"),
    )(w, g, m, v)


WEIGHT_ORDER = ("norm1_g", "w_in", "conv_dw_w", "conv_dw_b", "conv_ln_g", "conv_ln_b", "lru_conv_w", "lru_conv_b", "lru_wa",
                "lru_ba", "lru_wx", "lru_bx", "lru_lambda", "w_out", "norm2_g", "w_up", "w_down", "final_g")
BIG = ("w_in", "w_out", "w_up", "w_down")
SMALL_SHARDED = {"conv_dw_w": (DEPTH, CONV_TAPS, CONV_W), "lru_conv_w": (DEPTH, LRU_TAPS, LRU_W)}
SMALL_FULL = {
    "norm1_g": (DEPTH, D_MODEL), "conv_dw_w": (DEPTH, CONV_TAPS, CONV_W), "conv_dw_b": (DEPTH, CONV_W),
    "conv_ln_g": (DEPTH, CONV_W), "conv_ln_b": (DEPTH, CONV_W), "lru_conv_w": (DEPTH, LRU_TAPS, LRU_W),
    "lru_conv_b": (DEPTH, LRU_W), "lru_wa": (DEPTH, LRU_HEADS, HEAD_DIM, HEAD_DIM), "lru_ba": (DEPTH, LRU_W),
    "lru_wx": (DEPTH, LRU_HEADS, HEAD_DIM, HEAD_DIM), "lru_bx": (DEPTH, LRU_W), "lru_lambda": (DEPTH, LRU_W),
    "norm2_g": (DEPTH, D_MODEL), "final_g": (D_MODEL,),
}
SMALL_COLS = 128
SMALL_ROWS = 1000
FILTER_ROWS = 24
W_IN_SHARD = IN_COLS // N_DEV
W_OUT_SHARD = D_MODEL // N_DEV
FF_SHARD = D_FF // N_DEV


def _pack_rows(flat_parts, cols, rows):
    flat = jnp.concatenate(flat_parts)
    return jnp.pad(flat, (0, rows * cols - flat.shape[0])).reshape(rows, cols)


def _gather_weights(local):
    blocks, shapes, places = [], [], []
    for l in range(DEPTH):
        blocks += [local["w_in"][l].astype(BF16), local["w_out"][l].astype(BF16), local["w_up"][l].astype(BF16),
                   local["w_down"][l].astype(BF16)]
        shapes += [jax.ShapeDtypeStruct((N_DEV, D_MODEL, W_IN_SHARD), BF16), jax.ShapeDtypeStruct((D_MODEL, D_MODEL), BF16),
                   jax.ShapeDtypeStruct((D_MODEL, D_FF), BF16), jax.ShapeDtypeStruct((D_FF, D_MODEL), BF16)]
        places += [_slot, _row_block(W_OUT_SHARD), _col_block(FF_SHARD), _row_block(FF_SHARD)]
    blocks.append(_pack_rows([local[n].reshape(-1) for n in SMALL_SHARDED], SMALL_COLS, FILTER_ROWS))
    shapes.append(jax.ShapeDtypeStruct((N_DEV, FILTER_ROWS, SMALL_COLS), F32))
    places.append(_slot)
    out = _all_gather(blocks, shapes, places, "gather_weights")
    full = {n: [None] * DEPTH for n in BIG}
    for l in range(DEPTH):
        w_in_slots, full["w_out"][l], full["w_up"][l], full["w_down"][l] = out[4 * l:4 * l + 4]
        full["w_in"][l] = w_in_slots.transpose(1, 0, 2).reshape(D_MODEL, IN_COLS)
    flat, off = out[-1].reshape(N_DEV, -1), 0
    for n, shp in SMALL_SHARDED.items():
        shard = shp[:-1] + (shp[-1] // N_DEV,)
        size = int(np.prod(shard))
        full[n] = jnp.moveaxis(flat[:, off:off + size].reshape((N_DEV,) + shard), 0, -2).reshape(shp)
        off += size
    return full


def _exchange_big_grads(dw):
    sources, takes, puts = [], [], []
    for l in range(DEPTH):
        sources += [dw["w_in"][l].reshape(D_MODEL, N_DEV, W_IN_SHARD).transpose(1, 0, 2), dw["w_out"][l], dw["w_up"][l],
                    dw["w_down"][l]]
        takes += [_slot, _row_block(W_OUT_SHARD), _col_block(FF_SHARD), _col_block(FF_SHARD)]
        puts += [(a, functools.partial(lambda ref, q, l: ref.at[q, l], l=l)) for a in range(4)]
    shapes = [jax.ShapeDtypeStruct((N_DEV, DEPTH, D_MODEL, W_IN_SHARD), BF16),
              jax.ShapeDtypeStruct((N_DEV, DEPTH, W_OUT_SHARD, D_MODEL), BF16),
              jax.ShapeDtypeStruct((N_DEV, DEPTH, D_MODEL, FF_SHARD), BF16),
              jax.ShapeDtypeStruct((N_DEV, DEPTH, D_MODEL, FF_SHARD), BF16)]
    return dict(zip(BIG, _all_to_all(sources, takes, puts, shapes, "exchange_grads")))


def _sum_adamw(parts, w, m, v, rows_per_step, transposed, name):
    _, _, R, C = parts.shape
    tr = rows_per_step
    c1 = 1.0 - ADAM_B1 ** ADAM_STEP
    c2 = 1.0 - ADAM_B2 ** ADAM_STEP

    def body(p_ref, w_ref, m_ref, v_ref, g_ref, d_ref, nm_ref, nv_ref):
        gv = p_ref[0].astype(F32)
        for q in range(1, N_DEV):
            gv = gv + p_ref[q].astype(F32)
        if transposed:
            gv = gv.T
        nm = ADAM_B1 * m_ref[...] + (1.0 - ADAM_B1) * gv
        nv = ADAM_B2 * v_ref[...] + (1.0 - ADAM_B2) * (gv * gv)
        g_ref[...] = gv
        nm_ref[...] = nm
        nv_ref[...] = nv
        d_ref[...] = (-ADAM_LR) * ((nm / c1) / (jnp.sqrt(nv / c2) + ADAM_EPS) + ADAM_WD * w_ref[...])

    if transposed:
        spec = pl.BlockSpec((None, C, tr), lambda l, i: (l, 0, i))
    else:
        spec = pl.BlockSpec((None, tr, C), lambda l, i: (l, i, 0))
    shape = jax.ShapeDtypeStruct(w.shape, F32)
    return pl.pallas_call(
        body, name=name, grid=(DEPTH, R // tr),
        in_specs=[pl.BlockSpec((N_DEV, None, tr, C), lambda l, i: (0, l, i, 0)), spec, spec, spec],
        out_specs=[spec] * 4, out_shape=[shape] * 4,
        compiler_params=_params("parallel", "parallel"),
    )(parts, w, m, v)


def _block_diag(w):
    out = jnp.zeros((LRU_W, LRU_W), w.dtype)
    for i in range(LRU_HEADS):
        out = lax.dynamic_update_slice(out, w[i], (HEAD_DIM * i, HEAD_DIM * i))
    return out


def _diag_blocks(m):
    return jnp.stack([m[HEAD_DIM * i:HEAD_DIM * (i + 1), HEAD_DIM * i:HEAD_DIM * (i + 1)] for i in range(LRU_HEADS)])


def _pack_small(values, loss_row):
    parts = [values[n].reshape(-1) for n in SMALL_FULL]
    parts.append(loss_row.reshape(-1))
    return _pack_rows(parts, SMALL_COLS, SMALL_ROWS)


def _unpack_small(packed):
    flat = packed.reshape(-1)
    out, off = {}, 0
    for n, shp in SMALL_FULL.items():
        size = int(np.prod(shp))
        out[n] = flat[off:off + size].reshape(shp)
        off += size
    return out, flat[off]


def kernel(x, norm1_g, w_in, conv_dw_w, conv_dw_b, conv_ln_g, conv_ln_b, lru_conv_w, lru_conv_b, lru_wa, lru_ba, lru_wx, lru_bx, lru_lambda, w_out, norm2_g, w_up, w_down, final_g, loss_target, m_norm1_g, m_w_in, m_conv_dw_w, m_conv_dw_b, m_conv_ln_g, m_conv_ln_b, m_lru_conv_w, m_lru_conv_b, m_lru_wa, m_lru_ba, m_lru_wx, m_lru_bx, m_lru_lambda, m_w_out, m_norm2_g, m_w_up, m_w_down, m_final_g, v_norm1_g, v_w_in, v_conv_dw_w, v_conv_dw_b, v_conv_ln_g, v_conv_ln_b, v_lru_conv_w, v_lru_conv_b, v_lru_wa, v_lru_ba, v_lru_wx, v_lru_bx, v_lru_lambda, v_w_out, v_norm2_g, v_w_up, v_w_down, v_final_g):
    local = dict(zip(WEIGHT_ORDER, (norm1_g, w_in, conv_dw_w, conv_dw_b, conv_ln_g, conv_ln_b, lru_conv_w, lru_conv_b, lru_wa,
                                    lru_ba, lru_wx, lru_bx, lru_lambda, w_out, norm2_g, w_up, w_down, final_g)))
    mom1 = dict(zip(WEIGHT_ORDER, (m_norm1_g, m_w_in, m_conv_dw_w, m_conv_dw_b, m_conv_ln_g, m_conv_ln_b, m_lru_conv_w,
                                   m_lru_conv_b, m_lru_wa, m_lru_ba, m_lru_wx, m_lru_bx, m_lru_lambda, m_w_out, m_norm2_g,
                                   m_w_up, m_w_down, m_final_g)))
    mom2 = dict(zip(WEIGHT_ORDER, (v_norm1_g, v_w_in, v_conv_dw_w, v_conv_dw_b, v_conv_ln_g, v_conv_ln_b, v_lru_conv_w,
                                   v_lru_conv_b, v_lru_wa, v_lru_ba, v_lru_wx, v_lru_bx, v_lru_lambda, v_w_out, v_norm2_g,
                                   v_w_up, v_w_down, v_final_g)))
    B, S, _ = x.shape
    T = B * S
    my_slot = 4 * lax.axis_index("x") + 2 * lax.axis_index("y") + lax.axis_index("c")
    row = lambda a: a.reshape(1, -1)

    full = _gather_weights(local)

    saved = []
    cur = x.reshape(T, D_MODEL)
    for l in range(DEPTH):
        h, qkv, ci, li = _inproj(cur, row(norm1_g[l]), full["w_in"][l])
        qkv = qkv.reshape(B, S, QKV_W)
        o, lse = _attn_fwd(qkv)
        o, lse = o.reshape(T, ATTN_W), lse.reshape(T, ATTN_W)
        ci = ci.reshape(B, S, CONV_IN_W)
        li = li.reshape(B, S, LRU_IN_W)
        conv_p = (full["conv_dw_w"][l], row(conv_dw_b[l]), row(conv_ln_g[l]), row(conv_ln_b[l]))
        lru_p = (full["lru_conv_w"][l], row(lru_conv_b[l]), _block_diag(lru_wa[l]).astype(BF16), row(lru_ba[l]),
                 _block_diag(lru_wx[l]).astype(BF16), row(lru_bx[l]), row(lru_lambda[l]))
        yc, cpre = _conv_fwd(ci, *conv_p)
        yl, hs = _lru_fwd(li, *lru_p)
        x1, mix = _outproj(cur, o, lse, yc.reshape(T, CONV_W), yl.reshape(T, LRU_W), full["w_out"][l])
        h2, r = _up(x1, row(norm2_g[l]), full["w_up"][l])
        x2 = _down(x1, r, full["w_down"][l])
        saved.append(dict(x=cur, h=h, qkv=qkv, o=o, lse=lse, ci=ci, li=li, cpre=cpre, hs=hs, x1=x1, mix=mix, h2=h2, r=r,
                          conv_p=conv_p, lru_p=lru_p))
        cur = x2

    dx, loss_part, dgf = _loss_head(cur, loss_target.reshape(T, D_MODEL), row(final_g))

    big_grads = {n: [None] * DEPTH for n in BIG}
    small_grads = {n: [None] * DEPTH for n in SMALL_FULL if n != "final_g"}
    for l in reversed(range(DEPTH)):
        sv = saved[l]
        dpre = _down_bwd_act(dx, sv["r"], full["w_down"][l])
        big_grads["w_down"][l] = _down_bwd_w(sv["r"], dx)
        dx1, dg2 = _up_bwd_act(dpre, full["w_up"][l], sv["x1"], row(norm2_g[l]), dx)
        big_grads["w_up"][l] = _up_bwd_w(sv["h2"], dpre)
        do, dd, dyc, dyl, big_grads["w_out"][l] = _outproj_bwd(dx1, sv["mix"], full["w_out"][l], sv["o"], sv["lse"])
        seq = lambda a: a.reshape(B, S, ATTN_W)
        dq, dk, dv = _attn_bwd(sv["qkv"], seq(do), seq(sv["lse"]), seq(dd))
        dci, dcw, dcb, dlg, dlb = _conv_bwd(sv["ci"], sv["cpre"], dyc.reshape(B, S, CONV_W), sv["conv_p"][0], sv["conv_p"][2],
                                            sv["conv_p"][3])
        dli, dlcw, dlcb, dwa, dba, dwx, dbx, dlam = _lru_bwd(sv["li"], sv["hs"], dyl.reshape(B, S, LRU_W), *sv["lru_p"])
        dz = tuple(t.reshape(T, -1) for t in (dq, dk, dv, dci, dli))
        dx, dg1, big_grads["w_in"][l] = _inproj_bwd(dz, sv["h"], full["w_in"][l], sv["x"], row(norm1_g[l]), dx1)
        for n, g in (("norm1_g", dg1), ("conv_dw_w", dcw), ("conv_dw_b", dcb), ("conv_ln_g", dlg), ("conv_ln_b", dlb),
                     ("lru_conv_w", dlcw), ("lru_conv_b", dlcb), ("lru_wa", _diag_blocks(dwa)), ("lru_ba", dba),
                     ("lru_wx", _diag_blocks(dwx)), ("lru_bx", dbx), ("lru_lambda", dlam), ("norm2_g", dg2)):
            small_grads[n][l] = g.reshape(SMALL_FULL[n][1:])
    grad_x = dx.reshape(B, S, D_MODEL)

    grads, delta, new_m, new_v = {}, {}, {}, {}
    parts = _exchange_big_grads(big_grads)
    for n, rows_per_step in (("w_in", 256), ("w_out", W_OUT_SHARD), ("w_up", 256), ("w_down", 256)):
        grads[n], delta[n], new_m[n], new_v[n] = _sum_adamw(parts[n], local[n], mom1[n], mom2[n], rows_per_step,
                                                            n == "w_down", "sum_adamw_" + n)

    small_local = {n: jnp.stack(g) for n, g in small_grads.items()}
    small_local["final_g"] = dgf.reshape(D_MODEL)
    packet = _pack_small(small_local, loss_part[0:1, :])
    gathered = _all_gather([packet], [jax.ShapeDtypeStruct((N_DEV, SMALL_ROWS, SMALL_COLS), F32)], [_slot],
                           "gather_small_grads")[0]
    small_sum = _sum_slots(gathered, SMALL_ROWS, "sum_small_grads")
    small_g, loss = _unpack_small(small_sum)

    for n, fullshape in SMALL_SHARDED.items():
        width = fullshape[-1] // N_DEV
        g = lax.dynamic_slice_in_dim(small_g[n], my_slot * width, width, axis=2)
        two_d = (fullshape[0] * fullshape[1], width)
        d, nm, nv = _adamw(local[n].reshape(two_d), g.reshape(two_d), mom1[n].reshape(two_d), mom2[n].reshape(two_d),
                           two_d[0], "adamw_" + n)
        grads[n], delta[n], new_m[n], new_v[n] = (t.reshape(g.shape) for t in (g.reshape(two_d), d, nm, nv))
    replicated = [n for n in SMALL_FULL if n not in SMALL_SHARDED]
    zero_row = jnp.zeros((1, SMALL_COLS), F32)
    packed_state = []
    for src in (local, mom1, mom2):
        vals = {n: (src[n] if n in replicated else jnp.zeros(SMALL_FULL[n], F32)) for n in SMALL_FULL}
        packed_state.append(_pack_small(vals, zero_row))
    d, nm, nv = _adamw(packed_state[0], small_sum, packed_state[1], packed_state[2], SMALL_ROWS, "adamw_small")
    d, nm, nv = _unpack_small(d)[0], _unpack_small(nm)[0], _unpack_small(nv)[0]
    for n in replicated:
        grads[n], delta[n], new_m[n], new_v[n] = small_g[n], d[n], nm[n], nv[n]

    return (loss, grad_x, *[grads[n] for n in WEIGHT_ORDER], *[delta[n] for n in WEIGHT_ORDER],
            *[new_m[n] for n in WEIGHT_ORDER], *[new_v[n] for n in WEIGHT_ORDER])
```

```python
import functools
import math

import numpy as np
import jax
import jax.numpy as jnp
from jax import lax
from jax.experimental import pallas as pl
from jax.experimental.pallas import tpu as pltpu

F32 = jnp.float32
BF16 = jnp.bfloat16

D_MODEL = 1024
SEQ_LEN = 2048
HEAD_DIM = 64
ATTN_W = 384
CONV_W = 256
CONV_TAPS = 31
LRU_W = 384
LRU_TAPS = 4
LRU_HEADS = 6
LRU_C = 8.0
QKV_W = 3 * ATTN_W
CONV_IN_W = 2 * CONV_W
LRU_IN_W = 2 * LRU_W
IN_COLS = QKV_W + CONV_IN_W + LRU_IN_W
D_FF = 4096
DEPTH = 2
N_DEV = 8
RMS_EPS = 1e-6
LN_EPS = 1e-5
ATTN_BLOCK = 128
ATTN_DILATIONS = (1, 4, 16)
N_UNITS = 16
UNIT_UNROLL = 4
NEG_BIG = -1e30

ADAM_LR = 0.001
ADAM_B1 = 0.9
ADAM_B2 = 0.999
ADAM_EPS = 1e-08
ADAM_WD = 0.01
ADAM_STEP = 10

VMEM_LIMIT = 56 * 1024 * 1024
ROW_TILE = 512
SEQ_CHUNK = 128


def _params(*sem):
    return pltpu.CompilerParams(dimension_semantics=sem if sem else None, vmem_limit_bytes=VMEM_LIMIT)


def _resident(shape):
    return pl.BlockSpec(shape, lambda *_: (0,) * len(shape), pipeline_mode=pl.Buffered(1))


def _dot(a, b):
    return jnp.dot(a, b, preferred_element_type=F32)


def _dot_nt(a, b):
    return lax.dot_general(a, b, (((1,), (1,)), ((), ())), preferred_element_type=F32)


def _dot_tn(a, b):
    return lax.dot_general(a, b, (((0,), (0,)), ((), ())), preferred_element_type=F32)


def _rms_fwd(x, g):
    rstd = lax.rsqrt(jnp.mean(x * x, axis=-1, keepdims=True) + RMS_EPS)
    xhat = x * rstd
    return xhat * g, xhat, rstd


def _rms_bwd(dh, xhat, rstd, g):
    dxh = dh * g
    dx = rstd * (dxh - xhat * jnp.mean(dxh * xhat, axis=-1, keepdims=True))
    dg = jnp.sum(dh * xhat, axis=0, keepdims=True)
    return dx, dg


def _sigmoid(x):
    return 0.5 * jnp.tanh(0.5 * x) + 0.5


def _one_minus_exp(x, exp_x):
    small = -x * (1.0 + x * (0.5 + x * (1.0 / 6.0)))
    return jnp.where(x > -0.01, small, 1.0 - exp_x)


def _log1p(z):
    w = 1.0 + z
    return jnp.where(w == 1.0, z, z * jnp.log(w) / jnp.where(w == 1.0, 1.0, w - 1.0))


def _softplus(x):
    return jnp.maximum(x, 0.0) + _log1p(jnp.exp(-jnp.abs(x)))


GELU_K = math.sqrt(2.0 / math.pi)


def _gelu(x):
    t = jnp.tanh(GELU_K * (x + 0.044715 * x * x * x))
    return 0.5 * x * (1.0 + t), t


def _gelu_grad(x, t):
    return 0.5 * (1.0 + t) + 0.5 * x * (1.0 - t * t) * GELU_K * (1.0 + 3.0 * 0.044715 * x * x)


def _inproj(x2d, g, w):
    T = x2d.shape[0]
    tm = ROW_TILE

    def body(x_ref, g_ref, w_ref, h_ref, qkv_ref, ci_ref, li_ref):
        h, _, _ = _rms_fwd(x_ref[...], g_ref[...])
        hb = h.astype(BF16)
        h_ref[...] = hb
        qkv_ref[...] = _dot(hb, w_ref[:, 0:QKV_W])
        ci_ref[...] = _dot(hb, w_ref[:, QKV_W:QKV_W + CONV_IN_W])
        li_ref[...] = _dot(hb, w_ref[:, QKV_W + CONV_IN_W:IN_COLS])

    return pl.pallas_call(
        body, name="inproj", grid=(T // tm,),
        in_specs=[pl.BlockSpec((tm, D_MODEL), lambda i: (i, 0)),
                  pl.BlockSpec((1, D_MODEL), lambda i: (0, 0)),
                  _resident((D_MODEL, IN_COLS))],
        out_specs=[pl.BlockSpec((tm, D_MODEL), lambda i: (i, 0)),
                   pl.BlockSpec((tm, QKV_W), lambda i: (i, 0)),
                   pl.BlockSpec((tm, CONV_IN_W), lambda i: (i, 0)),
                   pl.BlockSpec((tm, LRU_IN_W), lambda i: (i, 0))],
        out_shape=[jax.ShapeDtypeStruct((T, D_MODEL), BF16), jax.ShapeDtypeStruct((T, QKV_W), F32),
                   jax.ShapeDtypeStruct((T, CONV_IN_W), F32), jax.ShapeDtypeStruct((T, LRU_IN_W), F32)],
        compiler_params=_params("parallel"),
    )(x2d, g, w)


def _attn_alpha(lse):
    l0, l1, l2 = lse[:, 0:128], lse[:, 128:256], lse[:, 256:384]
    m = jnp.maximum(jnp.maximum(l0, l1), l2)
    e0, e1, e2 = jnp.exp(l0 - m), jnp.exp(l1 - m), jnp.exp(l2 - m)
    inv = 1.0 / (e0 + e1 + e2)
    return e0 * inv, e1 * inv, e2 * inv


def _outproj(x2d, o, lse, yc, yl, w):
    T = x2d.shape[0]
    tm = ROW_TILE

    def body(x_ref, o_ref, lse_ref, yc_ref, yl_ref, w_ref, x1_ref, mix_ref):
        al = _attn_alpha(lse_ref[...])
        for p in range(3):
            mix_ref[:, p * 128:(p + 1) * 128] = (o_ref[:, p * 128:(p + 1) * 128] * al[p]).astype(BF16)
        mix_ref[:, ATTN_W:ATTN_W + CONV_W] = yc_ref[...]
        mix_ref[:, ATTN_W + CONV_W:D_MODEL] = yl_ref[...]
        x1_ref[...] = x_ref[...] + _dot(mix_ref[...], w_ref[...])

    return pl.pallas_call(
        body, name="outproj", grid=(T // tm,),
        in_specs=[pl.BlockSpec((tm, D_MODEL), lambda i: (i, 0)),
                  pl.BlockSpec((tm, ATTN_W), lambda i: (i, 0)),
                  pl.BlockSpec((tm, ATTN_W), lambda i: (i, 0)),
                  pl.BlockSpec((tm, CONV_W), lambda i: (i, 0)),
                  pl.BlockSpec((tm, LRU_W), lambda i: (i, 0)),
                  _resident((D_MODEL, D_MODEL))],
        out_specs=[pl.BlockSpec((tm, D_MODEL), lambda i: (i, 0)),
                   pl.BlockSpec((tm, D_MODEL), lambda i: (i, 0))],
        out_shape=[jax.ShapeDtypeStruct((T, D_MODEL), F32), jax.ShapeDtypeStruct((T, D_MODEL), BF16)],
        compiler_params=_params("parallel"),
    )(x2d, o, lse, yc, yl, w)


FF_CHUNK = 1024


def _up(x1, g, w):
    T = x1.shape[0]
    tm = ROW_TILE

    def body(x_ref, g_ref, w_ref, h_ref, r_ref):
        h, _, _ = _rms_fwd(x_ref[...], g_ref[...])
        hb = h.astype(BF16)
        h_ref[...] = hb
        for c in range(0, D_FF, FF_CHUNK):
            r_ref[:, c:c + FF_CHUNK] = jnp.maximum(_dot(hb, w_ref[:, c:c + FF_CHUNK]), 0.0).astype(BF16)

    return pl.pallas_call(
        body, name="up", grid=(T // tm,),
        in_specs=[pl.BlockSpec((tm, D_MODEL), lambda i: (i, 0)),
                  pl.BlockSpec((1, D_MODEL), lambda i: (0, 0)),
                  _resident((D_MODEL, D_FF))],
        out_specs=[pl.BlockSpec((tm, D_MODEL), lambda i: (i, 0)),
                   pl.BlockSpec((tm, D_FF), lambda i: (i, 0))],
        out_shape=[jax.ShapeDtypeStruct((T, D_MODEL), BF16), jax.ShapeDtypeStruct((T, D_FF), BF16)],
        compiler_params=_params("parallel"),
    )(x1, g, w)


def _square_bf16(r):
    rf = r.astype(F32)
    return (rf * rf).astype(BF16)


def _down(x1, r, w):
    T = x1.shape[0]
    tm = ROW_TILE

    def body(x_ref, r_ref, w_ref, o_ref):
        acc = x_ref[...]
        for c in range(0, D_FF, FF_CHUNK):
            acc = acc + _dot(_square_bf16(r_ref[:, c:c + FF_CHUNK]), w_ref[c:c + FF_CHUNK, :])
        o_ref[...] = acc

    return pl.pallas_call(
        body, name="down", grid=(T // tm,),
        in_specs=[pl.BlockSpec((tm, D_MODEL), lambda i: (i, 0)),
                  pl.BlockSpec((tm, D_FF), lambda i: (i, 0)),
                  _resident((D_FF, D_MODEL))],
        out_specs=pl.BlockSpec((tm, D_MODEL), lambda i: (i, 0)),
        out_shape=jax.ShapeDtypeStruct((T, D_MODEL), F32),
        compiler_params=_params("parallel"),
    )(x1, r, w)


def _loss_head(x2, target, g):
    T = x2.shape[0]
    tm = ROW_TILE

    def body(x_ref, t_ref, g_ref, dx_ref, loss_ref, dg_ref):
        @pl.when(pl.program_id(0) == 0)
        def _():
            loss_ref[...] = jnp.zeros_like(loss_ref)
            dg_ref[...] = jnp.zeros_like(dg_ref)

        gv = g_ref[...]
        y, xhat, rstd = _rms_fwd(x_ref[...], gv)
        err = y - t_ref[...]
        loss_ref[...] += 0.5 * jnp.sum(jnp.mean(err * err, axis=-1, keepdims=True))
        dy = err * (1.0 / D_MODEL)
        dx, dg = _rms_bwd(dy, xhat, rstd, gv)
        dx_ref[...] = dx
        dg_ref[...] += dg

    return pl.pallas_call(
        body, name="loss_head", grid=(T // tm,),
        in_specs=[pl.BlockSpec((tm, D_MODEL), lambda i: (i, 0)),
                  pl.BlockSpec((tm, D_MODEL), lambda i: (i, 0)),
                  pl.BlockSpec((1, D_MODEL), lambda i: (0, 0))],
        out_specs=[pl.BlockSpec((tm, D_MODEL), lambda i: (i, 0)),
                   pl.BlockSpec((8, 128), lambda i: (0, 0)),
                   pl.BlockSpec((1, D_MODEL), lambda i: (0, 0))],
        out_shape=[jax.ShapeDtypeStruct((T, D_MODEL), F32), jax.ShapeDtypeStruct((8, 128), F32),
                   jax.ShapeDtypeStruct((1, D_MODEL), F32)],
        compiler_params=_params("arbitrary"),
    )(x2, target, g)


def _down_bwd_act(dx2, r, w, ex=None):
    T = dx2.shape[0]
    tm = ROW_TILE

    def body(dx_ref, r_ref, w_ref, o_ref):
        dxb = dx_ref[...].astype(BF16)
        for c in range(0, D_FF, FF_CHUNK):
            dff = _dot_nt(dxb, w_ref[c:c + FF_CHUNK, :])
            o_ref[:, c:c + FF_CHUNK] = (dff * (2.0 * r_ref[:, c:c + FF_CHUNK].astype(F32))).astype(BF16)

    return _pallas_hosting(
        body, ex, name="down_bwd_act", grid=(T // tm,),
        in_specs=[pl.BlockSpec((tm, D_MODEL), lambda i: (i, 0)),
                  pl.BlockSpec((tm, D_FF), lambda i: (i, 0)),
                  _resident((D_FF, D_MODEL))],
        out_specs=[pl.BlockSpec((tm, D_FF), lambda i: (i, 0))],
        out_shape=[jax.ShapeDtypeStruct((T, D_FF), BF16)],
        scratch_shapes=[], semantics=("parallel",), operands=(dx2, r, w))


def _down_bwd_w(r, dx2):
    T = dx2.shape[0]
    tk = ROW_TILE
    nk = T // tk

    def body(r_ref, dx_ref, o_ref, acc_ref):
        k = pl.program_id(0)
        dxb = dx_ref[...].astype(BF16)

        @pl.when(k == 0)
        def _():
            acc_ref[...] = jnp.zeros_like(acc_ref)

        for c in range(0, D_FF, FF_CHUNK):
            acc_ref[:, c:c + FF_CHUNK] += _dot_tn(dxb, _square_bf16(r_ref[:, c:c + FF_CHUNK]))

        @pl.when(k == nk - 1)
        def _():
            o_ref[...] = acc_ref[...].astype(BF16)

    return pl.pallas_call(
        body, name="down_bwd_w", grid=(nk,),
        in_specs=[pl.BlockSpec((tk, D_FF), lambda k: (k, 0)),
                  pl.BlockSpec((tk, D_MODEL), lambda k: (k, 0))],
        out_specs=_resident((D_MODEL, D_FF)),
        out_shape=jax.ShapeDtypeStruct((D_MODEL, D_FF), BF16),
        scratch_shapes=[pltpu.VMEM((D_MODEL, D_FF), F32)],
        compiler_params=_params("arbitrary"),
    )(r, dx2)


def _up_bwd_act(dpre, w, x1, g, dx2, ex=None):
    T = dx2.shape[0]
    tm = ROW_TILE

    def body(dp_ref, w_ref, x_ref, g_ref, dx2_ref, dx1_ref, dg_ref):
        dh = _dot_nt(dp_ref[:, 0:FF_CHUNK], w_ref[:, 0:FF_CHUNK])
        for c in range(FF_CHUNK, D_FF, FF_CHUNK):
            dh = dh + _dot_nt(dp_ref[:, c:c + FF_CHUNK], w_ref[:, c:c + FF_CHUNK])
        gv = g_ref[...]
        _, xhat, rstd = _rms_fwd(x_ref[...], gv)
        dx, dg = _rms_bwd(dh, xhat, rstd, gv)
        dx1_ref[...] = dx2_ref[...] + dx

        @pl.when(pl.program_id(0) == 0)
        def _():
            dg_ref[...] = dg

        @pl.when(pl.program_id(0) != 0)
        def _():
            dg_ref[...] += dg

    return _pallas_hosting(
        body, ex, name="up_bwd_act", grid=(T // tm,),
        in_specs=[pl.BlockSpec((tm, D_FF), lambda i: (i, 0)),
                  _resident((D_MODEL, D_FF)),
                  pl.BlockSpec((tm, D_MODEL), lambda i: (i, 0)),
                  pl.BlockSpec((1, D_MODEL), lambda i: (0, 0)),
                  pl.BlockSpec((tm, D_MODEL), lambda i: (i, 0))],
        out_specs=[pl.BlockSpec((tm, D_MODEL), lambda i: (i, 0)),
                   pl.BlockSpec((1, D_MODEL), lambda i: (0, 0))],
        out_shape=[jax.ShapeDtypeStruct((T, D_MODEL), F32), jax.ShapeDtypeStruct((1, D_MODEL), F32)],
        scratch_shapes=[], semantics=("arbitrary",), operands=(dpre, w, x1, g, dx2))


def _up_bwd_w(h2, dpre):
    T = h2.shape[0]
    tk = ROW_TILE
    nk = T // tk

    def body(h_ref, dp_ref, o_ref, acc_ref):
        k = pl.program_id(0)
        hb = h_ref[...]

        @pl.when(k == 0)
        def _():
            acc_ref[...] = jnp.zeros_like(acc_ref)

        for c in range(0, D_FF, FF_CHUNK):
            acc_ref[:, c:c + FF_CHUNK] += _dot_tn(hb, dp_ref[:, c:c + FF_CHUNK])

        @pl.when(k == nk - 1)
        def _():
            o_ref[...] = acc_ref[...].astype(BF16)

    return pl.pallas_call(
        body, name="up_bwd_w", grid=(nk,),
        in_specs=[pl.BlockSpec((tk, D_MODEL), lambda k: (k, 0)),
                  pl.BlockSpec((tk, D_FF), lambda k: (k, 0))],
        out_specs=_resident((D_MODEL, D_FF)),
        out_shape=jax.ShapeDtypeStruct((D_MODEL, D_FF), BF16),
        scratch_shapes=[pltpu.VMEM((D_MODEL, D_FF), F32)],
        compiler_params=_params("arbitrary"),
    )(h2, dpre)


def _outproj_bwd(dx1, mix, w, o, lse):
    T = dx1.shape[0]
    tm = ROW_TILE
    nk = T // tm

    def body(dx_ref, mix_ref, w_ref, o_ref, lse_ref, do_ref, dd_ref, dc_ref, dl_ref, dw_ref, acc_ref):
        i = pl.program_id(0)
        dxb = dx_ref[...].astype(BF16)
        dmix = _dot_nt(dxb, w_ref[...])
        al = _attn_alpha(lse_ref[...])
        first = lax.broadcasted_iota(jnp.int32, (tm, 128), 1) < HEAD_DIM
        tot = jnp.zeros((tm, 128), F32)
        for p in range(3):
            sl = slice(p * 128, (p + 1) * 128)
            dy = dmix[:, sl]
            do_ref[:, sl] = dy * al[p]
            prod = dy * o_ref[:, sl]
            s0 = jnp.sum(jnp.where(first, prod, 0.0), axis=-1, keepdims=True)
            s1 = jnp.sum(jnp.where(first, 0.0, prod), axis=-1, keepdims=True)
            tot = tot + al[p] * jnp.where(first, s0, s1)
        for p in range(3):
            dd_ref[:, p * 128:(p + 1) * 128] = -al[p] * tot
        dc_ref[...] = dmix[:, ATTN_W:ATTN_W + CONV_W]
        dl_ref[...] = dmix[:, ATTN_W + CONV_W:D_MODEL]
        part = _dot_tn(mix_ref[...], dxb)

        @pl.when(i == 0)
        def _():
            acc_ref[...] = part

        @pl.when(i != 0)
        def _():
            acc_ref[...] += part

        @pl.when(i == nk - 1)
        def _():
            dw_ref[...] = acc_ref[...].astype(BF16)

    return pl.pallas_call(
        body, name="outproj_bwd", grid=(nk,),
        in_specs=[pl.BlockSpec((tm, D_MODEL), lambda i: (i, 0)),
                  pl.BlockSpec((tm, D_MODEL), lambda i: (i, 0)),
                  _resident((D_MODEL, D_MODEL)),
                  pl.BlockSpec((tm, ATTN_W), lambda i: (i, 0)),
                  pl.BlockSpec((tm, ATTN_W), lambda i: (i, 0))],
        out_specs=[pl.BlockSpec((tm, ATTN_W), lambda i: (i, 0)),
                   pl.BlockSpec((tm, ATTN_W), lambda i: (i, 0)),
                   pl.BlockSpec((tm, CONV_W), lambda i: (i, 0)),
                   pl.BlockSpec((tm, LRU_W), lambda i: (i, 0)),
                   _resident((D_MODEL, D_MODEL))],
        out_shape=[jax.ShapeDtypeStruct((T, ATTN_W), F32), jax.ShapeDtypeStruct((T, ATTN_W), F32),
                   jax.ShapeDtypeStruct((T, CONV_W), F32), jax.ShapeDtypeStruct((T, LRU_W), F32),
                   jax.ShapeDtypeStruct((D_MODEL, D_MODEL), BF16)],
        scratch_shapes=[pltpu.VMEM((D_MODEL, D_MODEL), F32)],
        compiler_params=_params("arbitrary"),
    )(dx1, mix, w, o, lse)


DZ_COLS = ((0, ATTN_W), (ATTN_W, 2 * ATTN_W), (2 * ATTN_W, QKV_W), (QKV_W, QKV_W + CONV_IN_W), (QKV_W + CONV_IN_W, IN_COLS))


def _inproj_bwd(dz_parts, h, w, x2d, g, dx1):
    T = x2d.shape[0]
    tm = ROW_TILE
    nk = T // tm
    n_parts = len(DZ_COLS)

    def body(*refs):
        dz_refs = refs[:n_parts]
        h_ref, w_ref, x_ref, g_ref, dx1_ref, dx_ref, dg_ref, dw_ref, acc_ref = refs[n_parts:]
        i = pl.program_id(0)
        dz = [r[...].astype(BF16) for r in dz_refs]
        dh = _dot_nt(dz[0], w_ref[:, DZ_COLS[0][0]:DZ_COLS[0][1]])
        for part, (lo, hi) in zip(dz[1:], DZ_COLS[1:]):
            dh = dh + _dot_nt(part, w_ref[:, lo:hi])
        gv = g_ref[...]
        _, xhat, rstd = _rms_fwd(x_ref[...], gv)
        dx, dg = _rms_bwd(dh, xhat, rstd, gv)
        dx_ref[...] = dx1_ref[...] + dx
        hb = h_ref[...]

        @pl.when(i == 0)
        def _():
            dg_ref[...] = jnp.zeros_like(dg_ref)
            acc_ref[...] = jnp.zeros_like(acc_ref)

        dg_ref[...] += dg
        for part, (lo, hi) in zip(dz, DZ_COLS):
            acc_ref[:, lo:hi] += _dot_tn(hb, part)

        @pl.when(i == nk - 1)
        def _():
            dw_ref[...] = acc_ref[...].astype(BF16)

    rows = lambda width: pl.BlockSpec((tm, width), lambda i: (i, 0))
    return pl.pallas_call(
        body, name="inproj_bwd", grid=(nk,),
        in_specs=[rows(hi - lo) for lo, hi in DZ_COLS] + [
            rows(D_MODEL), _resident((D_MODEL, IN_COLS)), rows(D_MODEL), pl.BlockSpec((1, D_MODEL), lambda i: (0, 0)),
            rows(D_MODEL)],
        out_specs=[rows(D_MODEL), pl.BlockSpec((1, D_MODEL), lambda i: (0, 0)), _resident((D_MODEL, IN_COLS))],
        out_shape=[jax.ShapeDtypeStruct((T, D_MODEL), F32), jax.ShapeDtypeStruct((1, D_MODEL), F32),
                   jax.ShapeDtypeStruct((D_MODEL, IN_COLS), BF16)],
        scratch_shapes=[pltpu.VMEM((D_MODEL, IN_COLS), F32)],
        compiler_params=_params("arbitrary"),
    )(*dz_parts, h, w, x2d, g, dx1)


def _alibi_coef():
    slopes = 2.0 ** (-8.0 * np.arange(1, 7) / 6)
    return jnp.asarray((slopes.reshape(3, 2) * np.asarray(ATTN_DILATIONS)[:, None]).astype(np.float32))


def _unit_rows(u, d):
    nb = N_UNITS // d
    r, n = u // nb, u % nb
    span = ATTN_BLOCK * d

    def rows(block):
        start = block * span + r
        return pl.ds(pl.multiple_of(start, ATTN_BLOCK), ATTN_BLOCK) if d == 1 else pl.ds(start, ATTN_BLOCK, stride=d)

    return rows(n), rows(jnp.maximum(n - 1, 0)), rows(jnp.minimum(n + 1, nb - 1)), n > 0, n + 1 < nb


def _per_pattern(fn):
    for p, d in enumerate(ATTN_DILATIONS):
        pl.when(pl.program_id(1) == p)(functools.partial(fn, p, d))


def _attn_col(offset):
    return pl.BlockSpec((None, SEQ_LEN, 128), lambda b, p: (b, 0, p + offset))


def _attn_masks():
    qi = lax.broadcasted_iota(jnp.int32, (ATTN_BLOCK, 2 * ATTN_BLOCK), 0)
    kj = lax.broadcasted_iota(jnp.int32, (ATTN_BLOCK, 2 * ATTN_BLOCK), 1)
    dist = qi + ATTN_BLOCK - kj
    first = lax.broadcasted_iota(jnp.int32, (ATTN_BLOCK, 128), 1) < HEAD_DIM
    return dist.astype(F32), (dist >= 0) & (dist <= ATTN_BLOCK), kj >= ATTN_BLOCK, first


def _head_lanes(a, first, j):
    return jnp.where(first if j == 0 else jnp.logical_not(first), a, jnp.zeros_like(a))


def _load_kv(ref, prev, own):
    return jnp.concatenate([ref[prev, :], ref[own, :]], axis=0).astype(BF16)


def _attn_fwd(qkv):
    B = qkv.shape[0]

    def body(coef_ref, q_ref, k_ref, v_ref, o_ref, lse_ref):
        dist, band, own_half, first = _attn_masks()

        def pattern(p, d):
            def unit(u, carry):
                own, prev, _, has_prev, _ = _unit_rows(u, d)
                ok = band & jnp.logical_or(own_half, has_prev)
                q = q_ref[own, :].astype(BF16)
                kcat, vcat = _load_kv(k_ref, prev, own), _load_kv(v_ref, prev, own)
                outs, lses = [], []
                for j in range(2):
                    s = jnp.where(ok, _dot_nt(_head_lanes(q, first, j), kcat) * 0.125 - coef_ref[p, j] * dist, NEG_BIG)
                    m = jnp.max(s, axis=-1, keepdims=True)
                    e = jnp.exp(s - m)
                    l = jnp.sum(e, axis=-1, keepdims=True)
                    outs.append(_dot(e.astype(BF16), vcat) * (1.0 / l))
                    lses.append(m + jnp.log(l))
                o_ref[own, :] = jnp.where(first, outs[0], outs[1])
                lse_ref[own, :] = jnp.where(first, lses[0], lses[1])
                return carry

            lax.fori_loop(0, N_UNITS, unit, 0, unroll=UNIT_UNROLL)

        _per_pattern(pattern)

    shape = jax.ShapeDtypeStruct((B, SEQ_LEN, ATTN_W), F32)
    return pl.pallas_call(
        body, name="attn_fwd", grid=(B, 3),
        in_specs=[pl.BlockSpec(memory_space=pltpu.SMEM), _attn_col(0), _attn_col(3), _attn_col(6)],
        out_specs=[_attn_col(0), _attn_col(0)],
        out_shape=[shape, shape],
        compiler_params=_params("parallel", "parallel"),
    )(_alibi_coef(), qkv, qkv, qkv)


def _attn_bwd(qkv, do, lse, dd):
    B = qkv.shape[0]

    def body(coef_ref, q_ref, k_ref, v_ref, do_ref, lse_ref, dd_ref, dq_ref, dk_ref, dv_ref):
        dist, band, own_half, first = _attn_masks()

        def pattern(p, d):
            def unit(u, carry):
                own, prev, _, has_prev, _ = _unit_rows(u, d)
                ok = band & jnp.logical_or(own_half, has_prev)
                q, do = q_ref[own, :].astype(BF16), do_ref[own, :].astype(BF16)
                kcat, vcat = _load_kv(k_ref, prev, own), _load_kv(v_ref, prev, own)
                lse_a, dd_a = lse_ref[own, :], dd_ref[own, :]
                dqs, dks, dvs = [], [], []
                for j in range(2):
                    col = slice(HEAD_DIM * j, HEAD_DIM * j + 1)
                    s = _dot_nt(_head_lanes(q, first, j), kcat) * 0.125 - coef_ref[p, j] * dist
                    pr = jnp.where(ok, jnp.exp(jnp.where(ok, s, NEG_BIG) - lse_a[:, col]), 0.0)
                    ds = (pr * (_dot_nt(_head_lanes(do, first, j), vcat) + dd_a[:, col])).astype(BF16)
                    dqs.append(_dot(ds, kcat))
                    dks.append(_dot_tn(ds, q))
                    dvs.append(_dot_tn(pr.astype(BF16), do))
                both = lambda pair: jnp.where(jnp.concatenate([first] * (pair[0].shape[0] // ATTN_BLOCK), axis=0), *pair)
                dq_ref[own, :] = both(dqs) * 0.125
                dk, dv = both(dks) * 0.125, both(dvs)
                dk_ref[own, :] = dk[ATTN_BLOCK:]
                dv_ref[own, :] = dv[ATTN_BLOCK:]
                dk_ref[prev, :] += dk[:ATTN_BLOCK]
                dv_ref[prev, :] += dv[:ATTN_BLOCK]
                return carry

            lax.fori_loop(0, N_UNITS, unit, 0, unroll=UNIT_UNROLL)

        _per_pattern(pattern)

    shape = jax.ShapeDtypeStruct((B, SEQ_LEN, ATTN_W), F32)
    return pl.pallas_call(
        body, name="attn_bwd", grid=(B, 3),
        in_specs=[pl.BlockSpec(memory_space=pltpu.SMEM), _attn_col(0), _attn_col(3), _attn_col(6), _attn_col(0), _attn_col(0),
                  _attn_col(0)],
        out_specs=[_attn_col(0)] * 3,
        out_shape=[shape] * 3,
        compiler_params=_params("parallel", "parallel"),
    )(_alibi_coef(), qkv, qkv, qkv, do, lse, dd)


def _for_chunks(n_rows, fn, chunk=SEQ_CHUNK):
    def step(c, carry):
        fn(pl.multiple_of(c * chunk, chunk))
        return carry

    lax.fori_loop(0, n_rows // chunk, step, 0)


def _shift_down(win, s, rows):
    lead = win.shape[0] - rows
    if s == 0:
        return win[lead:]
    if s % 8 == 0:
        return win[lead - s:lead - s + rows]
    q, r = divmod(s, 8)
    rolled = pltpu.roll(win, r, 0)
    return rolled[lead - 8 * q:lead - 8 * q + rows]


def _shift_up(win, s, rows):
    if s % 8 == 0:
        return win[s:s + rows]
    q, r = divmod(s, 8)
    rolled = pltpu.roll(win, win.shape[0] - r, 0)
    return rolled[8 * q:8 * q + rows]


CONV_PAD = 32


def _ln_silu(c, lg, lb):
    mu = jnp.mean(c, axis=-1, keepdims=True)
    cc = c - mu
    rstd = lax.rsqrt(jnp.mean(cc * cc, axis=-1, keepdims=True) + LN_EPS)
    nrm = cc * rstd
    v = nrm * lg + lb
    sg = _sigmoid(v)
    return v * sg, nrm, rstd, v, sg


def _conv_fwd(ci, w, b, lg, lb):
    B, S, _ = ci.shape
    CH = SEQ_CHUNK

    def body(ci_ref, w_ref, b_ref, lg_ref, lb_ref, y_ref, c_ref, pad_ref):
        pad_ref[0:CONV_PAD, :] = jnp.zeros((CONV_PAD, CONV_W), F32)

        def glu(base):
            blk = ci_ref[pl.ds(base, CH), :]
            pad_ref[pl.ds(CONV_PAD + base, CH), :] = blk[:, 0:CONV_W] * _sigmoid(blk[:, CONV_W:])

        _for_chunks(S, glu)

        def conv(base):
            win = pad_ref[pl.ds(base, CH + CONV_PAD), :]
            acc = jnp.broadcast_to(b_ref[...], (CH, CONV_W))
            for k in range(CONV_TAPS):
                acc = acc + w_ref[k:k + 1, :] * _shift_down(win, CONV_TAPS - 1 - k, CH)
            c_ref[pl.ds(base, CH), :] = acc
            y, _, _, _, _ = _ln_silu(acc, lg_ref[...], lb_ref[...])
            y_ref[pl.ds(base, CH), :] = y.astype(BF16)

        _for_chunks(S, conv)

    vec = pl.BlockSpec((1, CONV_W), lambda i: (0, 0))
    return pl.pallas_call(
        body, name="conv_fwd", grid=(B,),
        in_specs=[pl.BlockSpec((None, S, CONV_IN_W), lambda i: (i, 0, 0)),
                  pl.BlockSpec((CONV_TAPS, CONV_W), lambda i: (0, 0)), vec, vec, vec],
        out_specs=[pl.BlockSpec((None, S, CONV_W), lambda i: (i, 0, 0)),
                   pl.BlockSpec((None, S, CONV_W), lambda i: (i, 0, 0))],
        out_shape=[jax.ShapeDtypeStruct((B, S, CONV_W), BF16), jax.ShapeDtypeStruct((B, S, CONV_W), F32)],
        scratch_shapes=[pltpu.VMEM((S + CONV_PAD, CONV_W), F32)],
        compiler_params=_params("parallel"),
    )(ci, w, b, lg, lb)


def _conv_bwd(ci, cpre, dy, w, lg, lb, ex=None):
    B, S, _ = ci.shape
    CH = SEQ_CHUNK

    def body(ci_ref, c_ref, dy_ref, w_ref, lg_ref, lb_ref, dci_ref, dw_ref, db_ref, dlg_ref, dlb_ref, upad_ref, dcpad_ref,
             dwacc_ref):
        @pl.when(pl.program_id(0) == 0)
        def _():
            dw_ref[...] = jnp.zeros_like(dw_ref)
            db_ref[...] = jnp.zeros_like(db_ref)
            dlg_ref[...] = jnp.zeros_like(dlg_ref)
            dlb_ref[...] = jnp.zeros_like(dlb_ref)

        upad_ref[0:CONV_PAD, :] = jnp.zeros((CONV_PAD, CONV_W), F32)
        dcpad_ref[S:S + CONV_PAD, :] = jnp.zeros((CONV_PAD, CONV_W), F32)
        dwacc_ref[...] = jnp.zeros_like(dwacc_ref)

        def norm_bwd(base):
            blk = ci_ref[pl.ds(base, CH), :]
            upad_ref[pl.ds(CONV_PAD + base, CH), :] = blk[:, 0:CONV_W] * _sigmoid(blk[:, CONV_W:])
            lgv = lg_ref[...]
            _, nrm, rstd, v, sg = _ln_silu(c_ref[pl.ds(base, CH), :], lgv, lb_ref[...])
            dv = dy_ref[pl.ds(base, CH), :] * (sg * (1.0 + v * (1.0 - sg)))
            dlg_ref[...] += jnp.sum(dv * nrm, axis=0, keepdims=True)
            dlb_ref[...] += jnp.sum(dv, axis=0, keepdims=True)
            dn = dv * lgv
            dc = rstd * (dn - jnp.mean(dn, axis=-1, keepdims=True) - nrm * jnp.mean(dn * nrm, axis=-1, keepdims=True))
            dcpad_ref[pl.ds(base, CH), :] = dc
            db_ref[...] += jnp.sum(dc, axis=0, keepdims=True)

        _for_chunks(S, norm_bwd)

        def conv_bwd(base):
            dwin = dcpad_ref[pl.ds(base, CH + CONV_PAD), :]
            uwin = upad_ref[pl.ds(base, CH + CONV_PAD), :]
            dc = dwin[0:CH]
            du = jnp.zeros((CH, CONV_W), F32)
            for k in range(CONV_TAPS):
                du = du + w_ref[k:k + 1, :] * _shift_up(dwin, CONV_TAPS - 1 - k, CH)
                prod = dc * _shift_down(uwin, CONV_TAPS - 1 - k, CH)
                dwacc_ref[8 * k:8 * k + 8, :] += jnp.sum(prod.reshape(CH // 8, 8, CONV_W), axis=0)
            blk = ci_ref[pl.ds(base, CH), :]
            a, sg = blk[:, 0:CONV_W], _sigmoid(blk[:, CONV_W:])
            dci_ref[pl.ds(base, CH), 0:CONV_W] = (du * sg).astype(BF16)
            dci_ref[pl.ds(base, CH), CONV_W:] = (du * a * sg * (1.0 - sg)).astype(BF16)

        _for_chunks(S, conv_bwd)
        for k in range(CONV_TAPS):
            dw_ref[k:k + 1, :] += jnp.sum(dwacc_ref[8 * k:8 * k + 8, :], axis=0, keepdims=True)

    vec = pl.BlockSpec((1, CONV_W), lambda i: (0, 0))
    mat = pl.BlockSpec((CONV_TAPS, CONV_W), lambda i: (0, 0))
    seq = lambda width: pl.BlockSpec((None, S, width), lambda i: (i, 0, 0))
    return _pallas_hosting(
        body, ex, name="conv_bwd", grid=(B,),
        in_specs=[seq(CONV_IN_W), seq(CONV_W), seq(CONV_W), mat, vec, vec],
        out_specs=[seq(CONV_IN_W), mat, vec, vec, vec],
        out_shape=[jax.ShapeDtypeStruct((B, S, CONV_IN_W), BF16), jax.ShapeDtypeStruct((CONV_TAPS, CONV_W), F32),
                   jax.ShapeDtypeStruct((1, CONV_W), F32), jax.ShapeDtypeStruct((1, CONV_W), F32),
                   jax.ShapeDtypeStruct((1, CONV_W), F32)],
        scratch_shapes=[pltpu.VMEM((S + CONV_PAD, CONV_W), F32), pltpu.VMEM((S + CONV_PAD, CONV_W), F32),
                        pltpu.VMEM((8 * CONV_TAPS, CONV_W), F32)],
        semantics=("arbitrary",), operands=(ci, cpre, dy, w, lg, lb))


SCAN_SHIFTS = tuple(1 << e for e in range(11))


def _prev8(ref, base, cols, fill):
    start = pl.multiple_of(jnp.maximum(base - 8, 0), 8)
    return jnp.where(base > 0, ref[pl.ds(start, 8), cols], fill)


def _next8(ref, base, rows, total, cols, fill):
    start = pl.multiple_of(jnp.minimum(base + rows, total - 8), 8)
    return jnp.where(base + rows < total, ref[pl.ds(start, 8), cols], fill)


ALL = slice(None)
LRU_X = slice(LRU_W, LRU_IN_W)
LRU_GATE = slice(0, LRU_W)


def _lru_conv(li_ref, base, rows, cw_ref, cb_ref):
    win = jnp.concatenate([_prev8(li_ref, base, LRU_X, 0.0), li_ref[pl.ds(base, rows), LRU_X]], axis=0)
    u = jnp.broadcast_to(cb_ref[...], (rows, LRU_W))
    for k in range(LRU_TAPS):
        u = u + cw_ref[k:k + 1, :] * _shift_down(win, LRU_TAPS - 1 - k, rows)
    return u, win


def _lru_gates(u, wa_ref, ba_ref, wx_ref, bx_ref, sp):
    ub = u.astype(BF16)
    r = _sigmoid(_dot(ub, wa_ref[...]) + ba_ref[...])
    i = _sigmoid(_dot(ub, wx_ref[...]) + bx_ref[...])
    la = (-LRU_C) * r * sp
    a = jnp.exp(la)
    return ub, r, i, a, _one_minus_exp(2.0 * la, a * a)


def _scan_forward(bufs, S, CH):
    for n, s in enumerate(SCAN_SHIFTS):
        (sa, sb), (da, db) = bufs[n % 2], bufs[(n + 1) % 2]

        def step(base, s=s, sa=sa, sb=sb, da=da, db=db):
            a, b = sa[pl.ds(base, CH), :], sb[pl.ds(base, CH), :]
            if s < 8:
                a_s = _shift_down(jnp.concatenate([_prev8(sa, base, ALL, 1.0), a], axis=0), s, CH)
                b_s = _shift_down(jnp.concatenate([_prev8(sb, base, ALL, 0.0), b], axis=0), s, CH)
            elif s < CH:
                start = pl.multiple_of(jnp.maximum(base - s, 0), 8)
                a_s = jnp.concatenate([jnp.where(base > 0, sa[pl.ds(start, s), :], 1.0), a[0:CH - s]], axis=0)
                b_s = jnp.concatenate([jnp.where(base > 0, sb[pl.ds(start, s), :], 0.0), b[0:CH - s]], axis=0)
            else:
                start = pl.multiple_of(jnp.maximum(base - s, 0), 8)
                a_s = jnp.where(base < s, 1.0, sa[pl.ds(start, CH), :])
                b_s = jnp.where(base < s, 0.0, sb[pl.ds(start, CH), :])
            db[pl.ds(base, CH), :] = a * b_s + b
            da[pl.ds(base, CH), :] = a * a_s

        _for_chunks(S, step, CH)
    return len(SCAN_SHIFTS) % 2


def _scan_backward(bufs, S, CH):
    for n, s in enumerate(SCAN_SHIFTS):
        (sa, sb), (da, db) = bufs[n % 2], bufs[(n + 1) % 2]

        def step(base, s=s, sa=sa, sb=sb, da=da, db=db):
            a, b = sa[pl.ds(base, CH), :], sb[pl.ds(base, CH), :]
            if s < 8:
                a_s = _shift_up(jnp.concatenate([a, _next8(sa, base, CH, S, ALL, 1.0)], axis=0), s, CH)
                b_s = _shift_up(jnp.concatenate([b, _next8(sb, base, CH, S, ALL, 0.0)], axis=0), s, CH)
            elif s < CH:
                start = pl.multiple_of(jnp.minimum(base + CH, S - s), 8)
                more = base + CH < S
                a_s = jnp.concatenate([a[s:CH], jnp.where(more, sa[pl.ds(start, s), :], 1.0)], axis=0)
                b_s = jnp.concatenate([b[s:CH], jnp.where(more, sb[pl.ds(start, s), :], 0.0)], axis=0)
            else:
                start = pl.multiple_of(jnp.minimum(base + s, S - CH), 8)
                a_s = jnp.where(base + s >= S, 1.0, sa[pl.ds(start, CH), :])
                b_s = jnp.where(base + s >= S, 0.0, sb[pl.ds(start, CH), :])
            db[pl.ds(base, CH), :] = a * b_s + b
            da[pl.ds(base, CH), :] = a * a_s

        _for_chunks(S, step, CH)
    return len(SCAN_SHIFTS) % 2


def _lru_fwd(li, cw, cb, wa, ba, wx, bx, lam):
    B, S, _ = li.shape
    CH = SEQ_CHUNK

    def body(li_ref, cw_ref, cb_ref, wa_ref, ba_ref, wx_ref, bx_ref, lam_ref, y_ref, h_ref, a0, b0, a1, b1):
        sp = _softplus(-lam_ref[...])

        def gates(base):
            u, _ = _lru_conv(li_ref, base, CH, cw_ref, cb_ref)
            _, _, i, a, em = _lru_gates(u, wa_ref, ba_ref, wx_ref, bx_ref, sp)
            a0[pl.ds(base, CH), :] = a
            b0[pl.ds(base, CH), :] = jnp.sqrt(em) * (i * u)

        _for_chunks(S, gates)
        bufs = ((a0, b0), (a1, b1))
        hb = bufs[_scan_forward(bufs, S, CH)][1]

        def out(base):
            h = hb[pl.ds(base, CH), :]
            h_ref[pl.ds(base, CH), :] = h
            gl, _ = _gelu(li_ref[pl.ds(base, CH), LRU_GATE])
            y_ref[pl.ds(base, CH), :] = (gl * h).astype(BF16)

        _for_chunks(S, out)

    vec = pl.BlockSpec((1, LRU_W), lambda i: (0, 0))
    mat = pl.BlockSpec((LRU_W, LRU_W), lambda i: (0, 0))
    seq = lambda width: pl.BlockSpec((None, S, width), lambda i: (i, 0, 0))
    return pl.pallas_call(
        body, name="lru_fwd", grid=(B,),
        in_specs=[seq(LRU_IN_W), pl.BlockSpec((LRU_TAPS, LRU_W), lambda i: (0, 0)), vec, mat, vec, mat, vec, vec],
        out_specs=[seq(LRU_W), seq(LRU_W)],
        out_shape=[jax.ShapeDtypeStruct((B, S, LRU_W), BF16), jax.ShapeDtypeStruct((B, S, LRU_W), F32)],
        scratch_shapes=[pltpu.VMEM((S, LRU_W), F32)] * 4,
        compiler_params=_params("parallel"),
    )(li, cw, cb, wa, ba, wx, bx, lam)


def _lru_bwd(li, hs, dy, cw, cb, wa, ba, wx, bx, lam, ex=None):
    B, S, _ = li.shape
    CH = SEQ_CHUNK

    def body(li_ref, hs_ref, dy_ref, cw_ref, cb_ref, wa_ref, ba_ref, wx_ref, bx_ref, lam_ref,
             dli_ref, dcw_ref, dcb_ref, dwa_ref, dba_ref, dwx_ref, dbx_ref, dlam_ref, a0, b0, a1, b1, u_s, du_s):
        @pl.when(pl.program_id(0) == 0)
        def _():
            for ref in (dcw_ref, dcb_ref, dwa_ref, dba_ref, dwx_ref, dbx_ref, dlam_ref):
                ref[...] = jnp.zeros_like(ref)

        lam_v = lam_ref[...]
        sp = _softplus(-lam_v)
        dsp_dlam = -_sigmoid(-lam_v)

        def gates(base):
            u, _ = _lru_conv(li_ref, base, CH, cw_ref, cb_ref)
            _, _, _, a, _ = _lru_gates(u, wa_ref, ba_ref, wx_ref, bx_ref, sp)
            gl, _ = _gelu(li_ref[pl.ds(base, CH), LRU_GATE])
            u_s[pl.ds(base, CH), :] = u
            a0[pl.ds(base, CH), :] = a
            b0[pl.ds(base, CH), :] = a * (dy_ref[pl.ds(base, CH), :] * gl)

        _for_chunks(S, gates)
        bufs = ((a0, b0), (a1, b1))
        eb = bufs[_scan_backward(bufs, S, CH)][1]

        def grads(base):
            e = eb[pl.ds(base, CH), :]
            e_next = _shift_up(jnp.concatenate([e, _next8(eb, base, CH, S, ALL, 0.0)], axis=0), 1, CH)
            gate = li_ref[pl.ds(base, CH), LRU_GATE]
            gl, th = _gelu(gate)
            dy = dy_ref[pl.ds(base, CH), :]
            g = dy * gl + e_next
            h = hs_ref[pl.ds(base, CH), :]
            h_prev = _shift_down(jnp.concatenate([_prev8(hs_ref, base, ALL, 0.0), h], axis=0), 1, CH)
            dli_ref[pl.ds(base, CH), LRU_GATE] = (dy * h * _gelu_grad(gate, th)).astype(BF16)
            u = u_s[pl.ds(base, CH), :]
            ub, r, i, a, em = _lru_gates(u, wa_ref, ba_ref, wx_ref, bx_ref, sp)
            mult = jnp.sqrt(em)
            da = g * h_prev
            dmult = g * (i * u)
            di = g * mult * u
            dla = da * a - dmult * (a * a) * lax.rsqrt(jnp.maximum(em, 1e-30))
            dlam_ref[...] += dsp_dlam * jnp.sum(dla * ((-LRU_C) * r), axis=0, keepdims=True)
            dpa = (dla * ((-LRU_C) * sp)) * r * (1.0 - r)
            dpx = di * i * (1.0 - i)
            dpab, dpxb = dpa.astype(BF16), dpx.astype(BF16)
            dwa_ref[...] += _dot_tn(ub, dpab)
            dwx_ref[...] += _dot_tn(ub, dpxb)
            dba_ref[...] += jnp.sum(dpa, axis=0, keepdims=True)
            dbx_ref[...] += jnp.sum(dpx, axis=0, keepdims=True)
            du = g * mult * i + _dot_nt(dpab, wa_ref[...]) + _dot_nt(dpxb, wx_ref[...])
            du_s[pl.ds(base, CH), :] = du
            dcb_ref[...] += jnp.sum(du, axis=0, keepdims=True)

        _for_chunks(S, grads)

        def conv_bwd(base):
            du = du_s[pl.ds(base, CH), :]
            dwin = jnp.concatenate([du, _next8(du_s, base, CH, S, ALL, 0.0)], axis=0)
            xwin = jnp.concatenate([_prev8(li_ref, base, LRU_X, 0.0), li_ref[pl.ds(base, CH), LRU_X]], axis=0)
            dx = jnp.zeros((CH, LRU_W), F32)
            for k in range(LRU_TAPS):
                dx = dx + cw_ref[k:k + 1, :] * _shift_up(dwin, LRU_TAPS - 1 - k, CH)
                dcw_ref[k:k + 1, :] += jnp.sum(du * _shift_down(xwin, LRU_TAPS - 1 - k, CH), axis=0, keepdims=True)
            dli_ref[pl.ds(base, CH), LRU_X] = dx.astype(BF16)

        _for_chunks(S, conv_bwd)

    vec = pl.BlockSpec((1, LRU_W), lambda i: (0, 0))
    mat = pl.BlockSpec((LRU_W, LRU_W), lambda i: (0, 0))
    taps = pl.BlockSpec((LRU_TAPS, LRU_W), lambda i: (0, 0))
    seq = lambda width: pl.BlockSpec((None, S, width), lambda i: (i, 0, 0))
    vec_shape = jax.ShapeDtypeStruct((1, LRU_W), F32)
    mat_shape = jax.ShapeDtypeStruct((LRU_W, LRU_W), F32)
    return _pallas_hosting(
        body, ex, name="lru_bwd", grid=(B,),
        in_specs=[seq(LRU_IN_W), seq(LRU_W), seq(LRU_W), taps, vec, mat, vec, mat, vec, vec],
        out_specs=[seq(LRU_IN_W), taps, vec, mat, vec, mat, vec, vec],
        out_shape=[jax.ShapeDtypeStruct((B, S, LRU_IN_W), BF16), jax.ShapeDtypeStruct((LRU_TAPS, LRU_W), F32),
                   vec_shape, mat_shape, vec_shape, mat_shape, vec_shape, vec_shape],
        scratch_shapes=[pltpu.VMEM((S, LRU_W), F32)] * 6,
        semantics=("arbitrary",), operands=(li, hs, dy, cw, cb, wa, ba, wx, bx, lam))


MESH = pl.DeviceIdType.MESH
HBM_SPEC = pl.BlockSpec(memory_space=pltpu.HBM)


def _slot(ref, p):
    return ref.at[p]


def _row_block(rows):
    return lambda ref, p: ref.at[pl.ds(p * rows, rows), :]


def _col_block(cols):
    return lambda ref, p: ref.at[:, pl.ds(p * cols, cols)]


def _all_gather(blocks, out_shapes, places, name):
    n = len(blocks)

    def body(*refs):
        x_refs, out_refs = refs[:n], refs[n:2 * n]
        send_sems, recv_sems, local_sems = refs[2 * n:]
        x, y, c = lax.axis_index("x"), lax.axis_index("y"), lax.axis_index("c")
        me, sibling = (x, y, c), (x, y, 1 - c)
        chips = [(1 - x, y), (x, 1 - y), (1 - x, 1 - y)]

        def place(a, dev):
            return places[a](out_refs[a], 4 * dev[0] + 2 * dev[1] + dev[2])

        def copy(a, k, blk, to, src=None):
            return pltpu.make_async_remote_copy(
                src_ref=place(a, blk) if src is None else src, dst_ref=place(a, blk),
                send_sem=send_sems.at[a, k], recv_sem=recv_sems.at[a, k], device_id=to, device_id_type=MESH)

        mine = [pltpu.make_async_copy(x_refs[a], place(a, me), local_sems.at[a]) for a in range(n)]
        first = [copy(a, 0, me, sibling, src=x_refs[a]) for a in range(n)]
        first += [copy(a, 1 + j, me, (*chip, c), src=x_refs[a]) for j, chip in enumerate(chips) for a in range(n)]
        for cp in mine + first:
            cp.start()
        passed = []
        for j, chip in enumerate(chips):
            for a in range(n):
                copy(a, 1 + j, (*chip, c), me).wait_recv()
                passed.append(copy(a, 4 + j, (*chip, c), sibling))
                passed[-1].start()
        for a in range(n):
            copy(a, 0, sibling, me).wait_recv()
        for j, chip in enumerate(chips):
            for a in range(n):
                copy(a, 4 + j, (*chip, 1 - c), me).wait_recv()
        for cp in first + passed:
            cp.wait_send()
        for cp in mine:
            cp.wait()

    return pl.pallas_call(
        body, name=name, out_shape=out_shapes,
        in_specs=[HBM_SPEC] * n, out_specs=[HBM_SPEC] * n,
        scratch_shapes=[pltpu.SemaphoreType.DMA((n, 7)), pltpu.SemaphoreType.DMA((n, 7)), pltpu.SemaphoreType.DMA((n,))],
    )(*blocks)


class _GradExchange:
    def __init__(self, sources, takes, piece_shapes):
        self.sources, self.takes, self.n = list(sources), list(takes), len(sources)
        self.out_shapes = [jax.ShapeDtypeStruct((N_DEV,) + tuple(s), BF16) for s in piece_shapes]

    def scratch(self):
        return [pltpu.SemaphoreType.DMA((self.n, 7)), pltpu.SemaphoreType.DMA((self.n, 7)), pltpu.SemaphoreType.DMA((self.n,))]

    def _copies(self, src_refs, out_refs, send_sems, recv_sems, local_sems):
        x, y, c = lax.axis_index("x"), lax.axis_index("y"), lax.axis_index("c")
        me = 4 * x + 2 * y + c
        mine = [pltpu.make_async_copy(self.takes[i](src_refs[i], me), out_refs[i].at[me], local_sems.at[i]) for i in range(self.n)]
        remote = []
        for k in range(1, N_DEV):
            px, py, pc = x ^ ((k >> 2) & 1), y ^ ((k >> 1) & 1), c ^ (k & 1)
            peer = 4 * px + 2 * py + pc
            for i in range(self.n):
                remote.append(pltpu.make_async_remote_copy(
                    src_ref=self.takes[i](src_refs[i], peer), dst_ref=out_refs[i].at[me], send_sem=send_sems.at[i, k - 1],
                    recv_sem=recv_sems.at[i, k - 1], device_id=(px, py, pc), device_id_type=MESH))
        return mine, remote

    def start(self, *refs):
        mine, remote = self._copies(*refs)
        for cp in mine + remote:
            cp.start()

    def finish(self, *refs):
        mine, remote = self._copies(*refs)
        for cp in remote:
            cp.wait_recv()
        for cp in remote:
            cp.wait_send()
        for cp in mine:
            cp.wait()


def _run_exchange(ex, name):
    def body(*refs):
        src_refs, out_refs, sems = refs[:ex.n], refs[ex.n:2 * ex.n], refs[2 * ex.n:]
        ex.start(src_refs, out_refs, *sems)
        ex.finish(src_refs, out_refs, *sems)

    return pl.pallas_call(
        body, name=name, out_shape=ex.out_shapes, in_specs=[HBM_SPEC] * ex.n, out_specs=[HBM_SPEC] * ex.n,
        scratch_shapes=ex.scratch(),
    )(*ex.sources)


def _pallas_hosting(body, ex, *, name, grid, in_specs, out_specs, out_shape, scratch_shapes, semantics, operands):
    if ex is None:
        outs = pl.pallas_call(body, name=name, grid=grid, in_specs=in_specs, out_specs=out_specs, out_shape=out_shape,
                              scratch_shapes=scratch_shapes, compiler_params=_params(*semantics))(*operands)
        return outs, None
    n_in, n_out, n_scr, n = len(in_specs), len(out_specs), len(scratch_shapes), ex.n

    def at_step(pick):
        conds = [pl.program_id(k) == pick(size) for k, size in enumerate(grid)]
        return functools.reduce(jnp.logical_and, conds)

    def hosting(*refs):
        ins, ex_ins = refs[:n_in], refs[n_in:n_in + n]
        outs, ex_outs = refs[n_in + n:n_in + n + n_out], refs[n_in + n + n_out:n_in + 2 * n + n_out]
        scratch, sems = refs[n_in + 2 * n + n_out:n_in + 2 * n + n_out + n_scr], refs[n_in + 2 * n + n_out + n_scr:]
        pl.when(at_step(lambda size: 0))(lambda: ex.start(ex_ins, ex_outs, *sems))
        body(*ins, *outs, *scratch)
        pl.when(at_step(lambda size: size - 1))(lambda: ex.finish(ex_ins, ex_outs, *sems))

    res = pl.pallas_call(
        hosting, name=name, grid=grid, in_specs=list(in_specs) + [HBM_SPEC] * n, out_specs=list(out_specs) + [HBM_SPEC] * n,
        out_shape=list(out_shape) + ex.out_shapes, scratch_shapes=list(scratch_shapes) + ex.scratch(),
        compiler_params=_params(*(["arbitrary"] * len(grid))),
    )(*operands, *ex.sources)
    return res[:n_out], res[n_out:]


def _sum_slots(parts, rows_per_step, name):
    _, R, C = parts.shape

    def body(p_ref, o_ref):
        acc = p_ref[0].astype(F32)
        for q in range(1, N_DEV):
            acc = acc + p_ref[q].astype(F32)
        o_ref[...] = acc

    return pl.pallas_call(
        body, name=name, grid=(R // rows_per_step,),
        in_specs=[pl.BlockSpec((N_DEV, rows_per_step, C), lambda i: (0, i, 0))],
        out_specs=pl.BlockSpec((rows_per_step, C), lambda i: (i, 0)),
        out_shape=jax.ShapeDtypeStruct((R, C), F32),
        compiler_params=_params("parallel"),
    )(parts)


def _adamw(w, g, m, v, rows_per_step, name):
    R, C = w.shape
    c1 = 1.0 - ADAM_B1 ** ADAM_STEP
    c2 = 1.0 - ADAM_B2 ** ADAM_STEP

    def body(w_ref, g_ref, m_ref, v_ref, d_ref, nm_ref, nv_ref):
        gv = g_ref[...]
        nm = ADAM_B1 * m_ref[...] + (1.0 - ADAM_B1) * gv
        nv = ADAM_B2 * v_ref[...] + (1.0 - ADAM_B2) * (gv * gv)
        nm_ref[...] = nm
        nv_ref[...] = nv
        d_ref[...] = (-ADAM_LR) * ((nm / c1) / (jnp.sqrt(nv / c2) + ADAM_EPS) + ADAM_WD * w_ref[...])

    spec = pl.BlockSpec((rows_per_step, C), lambda i: (i, 0))
    shape = jax.ShapeDtypeStruct((R, C), F32)
    return pl.pallas_call(
        body, name=name, grid=(R // rows_per_step,),
        in_specs=[spec] * 4, out_specs=[spec] * 3, out_shape=[shape] * 3,
        compiler_params=_params("parallel"),
    )(w, g, m, v)


WEIGHT_ORDER = ("norm1_g", "w_in", "conv_dw_w", "conv_dw_b", "conv_ln_g", "conv_ln_b", "lru_conv_w", "lru_conv_b", "lru_wa",
                "lru_ba", "lru_wx", "lru_bx", "lru_lambda", "w_out", "norm2_g", "w_up", "w_down", "final_g")
BIG = ("w_in", "w_out", "w_up", "w_down")
SMALL_SHARDED = {"conv_dw_w": (DEPTH, CONV_TAPS, CONV_W), "lru_conv_w": (DEPTH, LRU_TAPS, LRU_W)}
SMALL_FULL = {
    "norm1_g": (DEPTH, D_MODEL), "conv_dw_w": (DEPTH, CONV_TAPS, CONV_W), "conv_dw_b": (DEPTH, CONV_W),
    "conv_ln_g": (DEPTH, CONV_W), "conv_ln_b": (DEPTH, CONV_W), "lru_conv_w": (DEPTH, LRU_TAPS, LRU_W),
    "lru_conv_b": (DEPTH, LRU_W), "lru_wa": (DEPTH, LRU_HEADS, HEAD_DIM, HEAD_DIM), "lru_ba": (DEPTH, LRU_W),
    "lru_wx": (DEPTH, LRU_HEADS, HEAD_DIM, HEAD_DIM), "lru_bx": (DEPTH, LRU_W), "lru_lambda": (DEPTH, LRU_W),
    "norm2_g": (DEPTH, D_MODEL), "final_g": (D_MODEL,),
}
SMALL_COLS = 128
SMALL_ROWS = 1000
FILTER_ROWS = 24
W_IN_SHARD = IN_COLS // N_DEV
W_OUT_SHARD = D_MODEL // N_DEV
FF_SHARD = D_FF // N_DEV


def _pack_rows(flat_parts, cols, rows):
    flat = jnp.concatenate(flat_parts)
    return jnp.pad(flat, (0, rows * cols - flat.shape[0])).reshape(rows, cols)


def _gather_weights(local):
    blocks, shapes, places = [], [], []
    for l in range(DEPTH):
        blocks += [local["w_in"][l].astype(BF16), local["w_out"][l].astype(BF16), local["w_up"][l].astype(BF16),
                   local["w_down"][l].astype(BF16)]
        shapes += [jax.ShapeDtypeStruct((N_DEV, D_MODEL, W_IN_SHARD), BF16), jax.ShapeDtypeStruct((D_MODEL, D_MODEL), BF16),
                   jax.ShapeDtypeStruct((D_MODEL, D_FF), BF16), jax.ShapeDtypeStruct((D_FF, D_MODEL), BF16)]
        places += [_slot, _row_block(W_OUT_SHARD), _col_block(FF_SHARD), _row_block(FF_SHARD)]
    blocks.append(_pack_rows([local[n].reshape(-1) for n in SMALL_SHARDED], SMALL_COLS, FILTER_ROWS))
    shapes.append(jax.ShapeDtypeStruct((N_DEV, FILTER_ROWS, SMALL_COLS), F32))
    places.append(_slot)
    out = _all_gather(blocks, shapes, places, "gather_weights")
    full = {n: [None] * DEPTH for n in BIG}
    for l in range(DEPTH):
        w_in_slots, full["w_out"][l], full["w_up"][l], full["w_down"][l] = out[4 * l:4 * l + 4]
        full["w_in"][l] = w_in_slots.transpose(1, 0, 2).reshape(D_MODEL, IN_COLS)
    flat, off = out[-1].reshape(N_DEV, -1), 0
    for n, shp in SMALL_SHARDED.items():
        shard = shp[:-1] + (shp[-1] // N_DEV,)
        size = int(np.prod(shard))
        full[n] = jnp.moveaxis(flat[:, off:off + size].reshape((N_DEV,) + shard), 0, -2).reshape(shp)
        off += size
    return full


def _grad_exchange(name, dw):
    if name == "w_in":
        return _GradExchange([dw.reshape(D_MODEL, N_DEV, W_IN_SHARD).transpose(1, 0, 2)], [_slot], [(D_MODEL, W_IN_SHARD)])
    if name == "w_out":
        return _GradExchange([dw], [_row_block(W_OUT_SHARD)], [(W_OUT_SHARD, D_MODEL)])
    return _GradExchange([dw], [_col_block(FF_SHARD)], [(D_MODEL, FF_SHARD)])


def _sum_adamw(parts, w, m, v, rows_per_step, transposed, name):
    _, R, C = parts[0].shape
    tr = rows_per_step
    steps = R // tr
    c1 = 1.0 - ADAM_B1 ** ADAM_STEP
    c2 = 1.0 - ADAM_B2 ** ADAM_STEP

    def slot_sum(p_ref):
        acc = p_ref[0].astype(F32)
        for q in range(1, N_DEV):
            acc = acc + p_ref[q].astype(F32)
        return acc

    def body(p0_ref, p1_ref, w_ref, m_ref, v_ref, g_ref, d_ref, nm_ref, nv_ref):
        gv = jnp.where(pl.program_id(0) == 0, slot_sum(p0_ref), slot_sum(p1_ref))
        if transposed:
            gv = gv.T
        nm = ADAM_B1 * m_ref[...] + (1.0 - ADAM_B1) * gv
        nv = ADAM_B2 * v_ref[...] + (1.0 - ADAM_B2) * (gv * gv)
        g_ref[...] = gv
        nm_ref[...] = nm
        nv_ref[...] = nv
        d_ref[...] = (-ADAM_LR) * ((nm / c1) / (jnp.sqrt(nv / c2) + ADAM_EPS) + ADAM_WD * w_ref[...])

    if transposed:
        spec = pl.BlockSpec((None, C, tr), lambda l, i: (l, 0, i))
    else:
        spec = pl.BlockSpec((None, tr, C), lambda l, i: (l, i, 0))
    shape = jax.ShapeDtypeStruct(w.shape, F32)
    part0 = pl.BlockSpec((N_DEV, tr, C), lambda l, i: (0, jnp.where(l == 0, i, steps - 1), 0))
    part1 = pl.BlockSpec((N_DEV, tr, C), lambda l, i: (0, jnp.where(l == 1, i, 0), 0))
    return pl.pallas_call(
        body, name=name, grid=(DEPTH, steps),
        in_specs=[part0, part1, spec, spec, spec],
        out_specs=[spec] * 4, out_shape=[shape] * 4,
        compiler_params=_params("arbitrary", "arbitrary"),
    )(parts[0], parts[1], w, m, v)


def _block_diag(w):
    out = jnp.zeros((LRU_W, LRU_W), w.dtype)
    for i in range(LRU_HEADS):
        out = lax.dynamic_update_slice(out, w[i], (HEAD_DIM * i, HEAD_DIM * i))
    return out


def _diag_blocks(m):
    return jnp.stack([m[HEAD_DIM * i:HEAD_DIM * (i + 1), HEAD_DIM * i:HEAD_DIM * (i + 1)] for i in range(LRU_HEADS)])


def _pack_small(values, loss_row):
    parts = [values[n].reshape(-1) for n in SMALL_FULL]
    parts.append(loss_row.reshape(-1))
    return _pack_rows(parts, SMALL_COLS, SMALL_ROWS)


def _unpack_small(packed):
    flat = packed.reshape(-1)
    out, off = {}, 0
    for n, shp in SMALL_FULL.items():
        size = int(np.prod(shp))
        out[n] = flat[off:off + size].reshape(shp)
        off += size
    return out, flat[off]


def kernel(x, norm1_g, w_in, conv_dw_w, conv_dw_b, conv_ln_g, conv_ln_b, lru_conv_w, lru_conv_b, lru_wa, lru_ba, lru_wx, lru_bx, lru_lambda, w_out, norm2_g, w_up, w_down, final_g, loss_target, m_norm1_g, m_w_in, m_conv_dw_w, m_conv_dw_b, m_conv_ln_g, m_conv_ln_b, m_lru_conv_w, m_lru_conv_b, m_lru_wa, m_lru_ba, m_lru_wx, m_lru_bx, m_lru_lambda, m_w_out, m_norm2_g, m_w_up, m_w_down, m_final_g, v_norm1_g, v_w_in, v_conv_dw_w, v_conv_dw_b, v_conv_ln_g, v_conv_ln_b, v_lru_conv_w, v_lru_conv_b, v_lru_wa, v_lru_ba, v_lru_wx, v_lru_bx, v_lru_lambda, v_w_out, v_norm2_g, v_w_up, v_w_down, v_final_g):
    local = dict(zip(WEIGHT_ORDER, (norm1_g, w_in, conv_dw_w, conv_dw_b, conv_ln_g, conv_ln_b, lru_conv_w, lru_conv_b, lru_wa,
                                    lru_ba, lru_wx, lru_bx, lru_lambda, w_out, norm2_g, w_up, w_down, final_g)))
    mom1 = dict(zip(WEIGHT_ORDER, (m_norm1_g, m_w_in, m_conv_dw_w, m_conv_dw_b, m_conv_ln_g, m_conv_ln_b, m_lru_conv_w,
                                   m_lru_conv_b, m_lru_wa, m_lru_ba, m_lru_wx, m_lru_bx, m_lru_lambda, m_w_out, m_norm2_g,
                                   m_w_up, m_w_down, m_final_g)))
    mom2 = dict(zip(WEIGHT_ORDER, (v_norm1_g, v_w_in, v_conv_dw_w, v_conv_dw_b, v_conv_ln_g, v_conv_ln_b, v_lru_conv_w,
                                   v_lru_conv_b, v_lru_wa, v_lru_ba, v_lru_wx, v_lru_bx, v_lru_lambda, v_w_out, v_norm2_g,
                                   v_w_up, v_w_down, v_final_g)))
    B, S, _ = x.shape
    T = B * S
    my_slot = 4 * lax.axis_index("x") + 2 * lax.axis_index("y") + lax.axis_index("c")
    row = lambda a: a.reshape(1, -1)

    full = _gather_weights(local)

    saved = []
    cur = x.reshape(T, D_MODEL)
    for l in range(DEPTH):
        h, qkv, ci, li = _inproj(cur, row(norm1_g[l]), full["w_in"][l])
        qkv = qkv.reshape(B, S, QKV_W)
        o, lse = _attn_fwd(qkv)
        o, lse = o.reshape(T, ATTN_W), lse.reshape(T, ATTN_W)
        ci = ci.reshape(B, S, CONV_IN_W)
        li = li.reshape(B, S, LRU_IN_W)
        conv_p = (full["conv_dw_w"][l], row(conv_dw_b[l]), row(conv_ln_g[l]), row(conv_ln_b[l]))
        lru_p = (full["lru_conv_w"][l], row(lru_conv_b[l]), _block_diag(lru_wa[l]).astype(BF16), row(lru_ba[l]),
                 _block_diag(lru_wx[l]).astype(BF16), row(lru_bx[l]), row(lru_lambda[l]))
        yc, cpre = _conv_fwd(ci, *conv_p)
        yl, hs = _lru_fwd(li, *lru_p)
        x1, mix = _outproj(cur, o, lse, yc.reshape(T, CONV_W), yl.reshape(T, LRU_W), full["w_out"][l])
        h2, r = _up(x1, row(norm2_g[l]), full["w_up"][l])
        x2 = _down(x1, r, full["w_down"][l])
        saved.append(dict(x=cur, h=h, qkv=qkv, o=o, lse=lse, ci=ci, li=li, cpre=cpre, hs=hs, x1=x1, mix=mix, h2=h2, r=r,
                          conv_p=conv_p, lru_p=lru_p))
        cur = x2

    dx, loss_part, dgf = _loss_head(cur, loss_target.reshape(T, D_MODEL), row(final_g))

    received = {n: [None] * DEPTH for n in BIG}
    small_grads = {n: [None] * DEPTH for n in SMALL_FULL if n != "final_g"}
    w_in_exchange = None
    for l in reversed(range(DEPTH)):
        sv = saved[l]
        (dpre,), landed = _down_bwd_act(dx, sv["r"], full["w_down"][l], ex=w_in_exchange)
        if landed is not None:
            received["w_in"][l + 1] = landed[0]
        dw_down = _down_bwd_w(sv["r"], dx)
        (dx1, dg2), (received["w_down"][l],) = _up_bwd_act(dpre, full["w_up"][l], sv["x1"], row(norm2_g[l]), dx,
                                                           ex=_grad_exchange("w_down", dw_down))
        dw_up = _up_bwd_w(sv["h2"], dpre)
        do, dd, dyc, dyl, dw_out = _outproj_bwd(dx1, sv["mix"], full["w_out"][l], sv["o"], sv["lse"])
        seq = lambda a: a.reshape(B, S, ATTN_W)
        dq, dk, dv = _attn_bwd(sv["qkv"], seq(do), seq(sv["lse"]), seq(dd))
        (dci, dcw, dcb, dlg, dlb), (received["w_out"][l],) = _conv_bwd(
            sv["ci"], sv["cpre"], dyc.reshape(B, S, CONV_W), sv["conv_p"][0], sv["conv_p"][2], sv["conv_p"][3],
            ex=_grad_exchange("w_out", dw_out))
        (dli, dlcw, dlcb, dwa, dba, dwx, dbx, dlam), (received["w_up"][l],) = _lru_bwd(
            sv["li"], sv["hs"], dyl.reshape(B, S, LRU_W), *sv["lru_p"], ex=_grad_exchange("w_up", dw_up))
        dz = tuple(t.reshape(T, -1) for t in (dq, dk, dv, dci, dli))
        dx, dg1, dw_in = _inproj_bwd(dz, sv["h"], full["w_in"][l], sv["x"], row(norm1_g[l]), dx1)
        w_in_exchange = _grad_exchange("w_in", dw_in)
        for n, g in (("norm1_g", dg1), ("conv_dw_w", dcw), ("conv_dw_b", dcb), ("conv_ln_g", dlg), ("conv_ln_b", dlb),
                     ("lru_conv_w", dlcw), ("lru_conv_b", dlcb), ("lru_wa", _diag_blocks(dwa)), ("lru_ba", dba),
                     ("lru_wx", _diag_blocks(dwx)), ("lru_bx", dbx), ("lru_lambda", dlam), ("norm2_g", dg2)):
            small_grads[n][l] = g.reshape(SMALL_FULL[n][1:])
    grad_x = dx.reshape(B, S, D_MODEL)

    grads, delta, new_m, new_v = {}, {}, {}, {}
    received["w_in"][0] = _run_exchange(w_in_exchange, "exchange_w_in_grad")[0]
    for n, rows_per_step in (("w_in", 256), ("w_out", W_OUT_SHARD), ("w_up", 256), ("w_down", 256)):
        grads[n], delta[n], new_m[n], new_v[n] = _sum_adamw(received[n], local[n], mom1[n], mom2[n], rows_per_step,
                                                            n == "w_down", "sum_adamw_" + n)

    small_local = {n: jnp.stack(g) for n, g in small_grads.items()}
    small_local["final_g"] = dgf.reshape(D_MODEL)
    packet = _pack_small(small_local, loss_part[0:1, :])
    gathered = _all_gather([packet], [jax.ShapeDtypeStruct((N_DEV, SMALL_ROWS, SMALL_COLS), F32)], [_slot],
                           "gather_small_grads")[0]
    small_sum = _sum_slots(gathered, SMALL_ROWS, "sum_small_grads")
    small_g, loss = _unpack_small(small_sum)

    for n, fullshape in SMALL_SHARDED.items():
        width = fullshape[-1] // N_DEV
        g = lax.dynamic_slice_in_dim(small_g[n], my_slot * width, width, axis=2)
        two_d = (fullshape[0] * fullshape[1], width)
        d, nm, nv = _adamw(local[n].reshape(two_d), g.reshape(two_d), mom1[n].reshape(two_d), mom2[n].reshape(two_d),
                           two_d[0], "adamw_" + n)
        grads[n], delta[n], new_m[n], new_v[n] = (t.reshape(g.shape) for t in (g.reshape(two_d), d, nm, nv))
    replicated = [n for n in SMALL_FULL if n not in SMALL_SHARDED]
    zero_row = jnp.zeros((1, SMALL_COLS), F32)
    packed_state = []
    for src in (local, mom1, mom2):
        vals = {n: (src[n] if n in replicated else jnp.zeros(SMALL_FULL[n], F32)) for n in SMALL_FULL}
        packed_state.append(_pack_small(vals, zero_row))
    d, nm, nv = _adamw(packed_state[0], small_sum, packed_state[1], packed_state[2], SMALL_ROWS, "adamw_small")
    d, nm, nv = _unpack_small(d)[0], _unpack_small(nm)[0], _unpack_small(nv)[0]
    for n in replicated:
        grads[n], delta[n], new_m[n], new_v[n] = small_g[n], d[n], nm[n], nv[n]

    return (loss, grad_x, *[grads[n] for n in WEIGHT_ORDER], *[delta[n] for n in WEIGHT_ORDER],
            *[new_m[n] for n in WEIGHT_ORDER], *[new_v[n] for n in WEIGHT_ORDER])
```

```python
import functools
import math

import numpy as np
import jax
import jax.numpy as jnp
from jax import lax
from jax.experimental import pallas as pl
from jax.experimental.pallas import tpu as pltpu

F32 = jnp.float32
BF16 = jnp.bfloat16

D_MODEL = 1024
SEQ_LEN = 2048
HEAD_DIM = 64
ATTN_W = 384
CONV_W = 256
CONV_TAPS = 31
LRU_W = 384
LRU_TAPS = 4
LRU_HEADS = 6
LRU_C = 8.0
QKV_W = 3 * ATTN_W
CONV_IN_W = 2 * CONV_W
LRU_IN_W = 2 * LRU_W
IN_COLS = QKV_W + CONV_IN_W + LRU_IN_W
D_FF = 4096
DEPTH = 2
N_DEV = 8
RMS_EPS = 1e-6
LN_EPS = 1e-5
ATTN_BLOCK = 128
ATTN_DILATIONS = (1, 4, 16)
N_UNITS = 16
UNIT_UNROLL = 4
NEG_BIG = -1e30

ADAM_LR = 0.001
ADAM_B1 = 0.9
ADAM_B2 = 0.999
ADAM_EPS = 1e-08
ADAM_WD = 0.01
ADAM_STEP = 10

VMEM_LIMIT = 56 * 1024 * 1024
ROW_TILE = 512
CONV_FWD_CHUNK = 128
CONV_BWD_CHUNK = 128
LRU_FWD_CHUNK = 128
LRU_BWD_CHUNK = 128


def _params(*sem):
    return pltpu.CompilerParams(dimension_semantics=sem if sem else None, vmem_limit_bytes=VMEM_LIMIT)


def _resident(shape):
    return pl.BlockSpec(shape, lambda *_: (0,) * len(shape), pipeline_mode=pl.Buffered(1))


def _dot(a, b):
    return jnp.dot(a, b, preferred_element_type=F32)


def _dot_nt(a, b):
    return lax.dot_general(a, b, (((1,), (1,)), ((), ())), preferred_element_type=F32)


def _dot_tn(a, b):
    return lax.dot_general(a, b, (((0,), (0,)), ((), ())), preferred_element_type=F32)


def _rms_fwd(x, g):
    rstd = lax.rsqrt(jnp.mean(x * x, axis=-1, keepdims=True) + RMS_EPS)
    xhat = x * rstd
    return xhat * g, xhat, rstd


def _rms_bwd(dh, xhat, rstd, g):
    dxh = dh * g
    dx = rstd * (dxh - xhat * jnp.mean(dxh * xhat, axis=-1, keepdims=True))
    dg = jnp.sum(dh * xhat, axis=0, keepdims=True)
    return dx, dg


def _sigmoid(x):
    return 0.5 * jnp.tanh(0.5 * x) + 0.5


def _one_minus_exp(x, exp_x):
    small = -x * (1.0 + x * (0.5 + x * (1.0 / 6.0)))
    return jnp.where(x > -0.01, small, 1.0 - exp_x)


def _log1p(z):
    w = 1.0 + z
    return jnp.where(w == 1.0, z, z * jnp.log(w) / jnp.where(w == 1.0, 1.0, w - 1.0))


def _softplus(x):
    return jnp.maximum(x, 0.0) + _log1p(jnp.exp(-jnp.abs(x)))


GELU_K = math.sqrt(2.0 / math.pi)


def _gelu(x):
    t = jnp.tanh(GELU_K * (x + 0.044715 * x * x * x))
    return 0.5 * x * (1.0 + t), t


def _gelu_grad(x, t):
    return 0.5 * (1.0 + t) + 0.5 * x * (1.0 - t * t) * GELU_K * (1.0 + 3.0 * 0.044715 * x * x)


def _inproj(x2d, g, w, ex=None):
    T = x2d.shape[0]
    tm = ROW_TILE

    def body(x_ref, g_ref, w_ref, h_ref, qkv_ref, ci_ref, li_ref):
        h, _, _ = _rms_fwd(x_ref[...], g_ref[...])
        hb = h.astype(BF16)
        h_ref[...] = hb
        qkv_ref[...] = _dot(hb, w_ref[:, 0:QKV_W])
        ci_ref[...] = _dot(hb, w_ref[:, QKV_W:QKV_W + CONV_IN_W])
        li_ref[...] = _dot(hb, w_ref[:, QKV_W + CONV_IN_W:IN_COLS])

    return _pallas_hosting(
        body, ex, name="inproj", grid=(T // tm,),
        in_specs=[pl.BlockSpec((tm, D_MODEL), lambda i: (i, 0)),
                  pl.BlockSpec((1, D_MODEL), lambda i: (0, 0)),
                  _resident((D_MODEL, IN_COLS))],
        out_specs=[pl.BlockSpec((tm, D_MODEL), lambda i: (i, 0)),
                   pl.BlockSpec((tm, QKV_W), lambda i: (i, 0)),
                   pl.BlockSpec((tm, CONV_IN_W), lambda i: (i, 0)),
                   pl.BlockSpec((tm, LRU_IN_W), lambda i: (i, 0))],
        out_shape=[jax.ShapeDtypeStruct((T, D_MODEL), BF16), jax.ShapeDtypeStruct((T, QKV_W), F32),
                   jax.ShapeDtypeStruct((T, CONV_IN_W), F32), jax.ShapeDtypeStruct((T, LRU_IN_W), F32)],
        scratch_shapes=[], semantics=("parallel",), operands=(x2d, g, w))


def _attn_alpha(lse):
    l0, l1, l2 = lse[:, 0:128], lse[:, 128:256], lse[:, 256:384]
    m = jnp.maximum(jnp.maximum(l0, l1), l2)
    e0, e1, e2 = jnp.exp(l0 - m), jnp.exp(l1 - m), jnp.exp(l2 - m)
    inv = 1.0 / (e0 + e1 + e2)
    return e0 * inv, e1 * inv, e2 * inv


def _outproj(x2d, o, lse, yc, yl, w, ex=None):
    T = x2d.shape[0]
    tm = ROW_TILE

    def body(x_ref, o_ref, lse_ref, yc_ref, yl_ref, w_ref, x1_ref, mix_ref):
        al = _attn_alpha(lse_ref[...])
        for p in range(3):
            mix_ref[:, p * 128:(p + 1) * 128] = (o_ref[:, p * 128:(p + 1) * 128] * al[p]).astype(BF16)
        mix_ref[:, ATTN_W:ATTN_W + CONV_W] = yc_ref[...]
        mix_ref[:, ATTN_W + CONV_W:D_MODEL] = yl_ref[...]
        x1_ref[...] = x_ref[...] + _dot(mix_ref[...], w_ref[...])

    return _pallas_hosting(
        body, ex, name="outproj", grid=(T // tm,),
        in_specs=[pl.BlockSpec((tm, D_MODEL), lambda i: (i, 0)),
                  pl.BlockSpec((tm, ATTN_W), lambda i: (i, 0)),
                  pl.BlockSpec((tm, ATTN_W), lambda i: (i, 0)),
                  pl.BlockSpec((tm, CONV_W), lambda i: (i, 0)),
                  pl.BlockSpec((tm, LRU_W), lambda i: (i, 0)),
                  _resident((D_MODEL, D_MODEL))],
        out_specs=[pl.BlockSpec((tm, D_MODEL), lambda i: (i, 0)),
                   pl.BlockSpec((tm, D_MODEL), lambda i: (i, 0))],
        out_shape=[jax.ShapeDtypeStruct((T, D_MODEL), F32), jax.ShapeDtypeStruct((T, D_MODEL), BF16)],
        scratch_shapes=[], semantics=("parallel",), operands=(x2d, o, lse, yc, yl, w))


FF_CHUNK = 1024


def _up(x1, g, w, ex=None):
    T = x1.shape[0]
    tm = ROW_TILE

    def body(x_ref, g_ref, w_ref, h_ref, r_ref):
        h, _, _ = _rms_fwd(x_ref[...], g_ref[...])
        hb = h.astype(BF16)
        h_ref[...] = hb
        for c in range(0, D_FF, FF_CHUNK):
            r_ref[:, c:c + FF_CHUNK] = jnp.maximum(_dot(hb, w_ref[:, c:c + FF_CHUNK]), 0.0).astype(BF16)

    return _pallas_hosting(
        body, ex, name="up", grid=(T // tm,),
        in_specs=[pl.BlockSpec((tm, D_MODEL), lambda i: (i, 0)),
                  pl.BlockSpec((1, D_MODEL), lambda i: (0, 0)),
                  _resident((D_MODEL, D_FF))],
        out_specs=[pl.BlockSpec((tm, D_MODEL), lambda i: (i, 0)),
                   pl.BlockSpec((tm, D_FF), lambda i: (i, 0))],
        out_shape=[jax.ShapeDtypeStruct((T, D_MODEL), BF16), jax.ShapeDtypeStruct((T, D_FF), BF16)],
        scratch_shapes=[], semantics=("parallel",), operands=(x1, g, w))


def _square_bf16(r):
    rf = r.astype(F32)
    return (rf * rf).astype(BF16)


def _down(x1, r, w, ex=None):
    T = x1.shape[0]
    tm = ROW_TILE

    def body(x_ref, r_ref, w_ref, o_ref):
        acc = x_ref[...]
        for c in range(0, D_FF, FF_CHUNK):
            acc = acc + _dot(_square_bf16(r_ref[:, c:c + FF_CHUNK]), w_ref[c:c + FF_CHUNK, :])
        o_ref[...] = acc

    return _pallas_hosting(
        body, ex, name="down", grid=(T // tm,),
        in_specs=[pl.BlockSpec((tm, D_MODEL), lambda i: (i, 0)),
                  pl.BlockSpec((tm, D_FF), lambda i: (i, 0)),
                  _resident((D_FF, D_MODEL))],
        out_specs=[pl.BlockSpec((tm, D_MODEL), lambda i: (i, 0))],
        out_shape=[jax.ShapeDtypeStruct((T, D_MODEL), F32)],
        scratch_shapes=[], semantics=("parallel",), operands=(x1, r, w))


def _loss_head(x2, target, g):
    T = x2.shape[0]
    tm = ROW_TILE

    def body(x_ref, t_ref, g_ref, dx_ref, loss_ref, dg_ref):
        @pl.when(pl.program_id(0) == 0)
        def _():
            loss_ref[...] = jnp.zeros_like(loss_ref)
            dg_ref[...] = jnp.zeros_like(dg_ref)

        gv = g_ref[...]
        y, xhat, rstd = _rms_fwd(x_ref[...], gv)
        err = y - t_ref[...]
        loss_ref[...] += 0.5 * jnp.sum(jnp.mean(err * err, axis=-1, keepdims=True))
        dy = err * (1.0 / D_MODEL)
        dx, dg = _rms_bwd(dy, xhat, rstd, gv)
        dx_ref[...] = dx
        dg_ref[...] += dg

    return pl.pallas_call(
        body, name="loss_head", grid=(T // tm,),
        in_specs=[pl.BlockSpec((tm, D_MODEL), lambda i: (i, 0)),
                  pl.BlockSpec((tm, D_MODEL), lambda i: (i, 0)),
                  pl.BlockSpec((1, D_MODEL), lambda i: (0, 0))],
        out_specs=[pl.BlockSpec((tm, D_MODEL), lambda i: (i, 0)),
                   pl.BlockSpec((8, 128), lambda i: (0, 0)),
                   pl.BlockSpec((1, D_MODEL), lambda i: (0, 0))],
        out_shape=[jax.ShapeDtypeStruct((T, D_MODEL), F32), jax.ShapeDtypeStruct((8, 128), F32),
                   jax.ShapeDtypeStruct((1, D_MODEL), F32)],
        compiler_params=_params("arbitrary"),
    )(x2, target, g)


def _down_bwd_act(dx2, r, w, ex=None):
    T = dx2.shape[0]
    tm = ROW_TILE

    def body(dx_ref, r_ref, w_ref, o_ref):
        dxb = dx_ref[...].astype(BF16)
        for c in range(0, D_FF, FF_CHUNK):
            dff = _dot_nt(dxb, w_ref[c:c + FF_CHUNK, :])
            o_ref[:, c:c + FF_CHUNK] = (dff * (2.0 * r_ref[:, c:c + FF_CHUNK].astype(F32))).astype(BF16)

    return _pallas_hosting(
        body, ex, name="down_bwd_act", grid=(T // tm,),
        in_specs=[pl.BlockSpec((tm, D_MODEL), lambda i: (i, 0)),
                  pl.BlockSpec((tm, D_FF), lambda i: (i, 0)),
                  _resident((D_FF, D_MODEL))],
        out_specs=[pl.BlockSpec((tm, D_FF), lambda i: (i, 0))],
        out_shape=[jax.ShapeDtypeStruct((T, D_FF), BF16)],
        scratch_shapes=[], semantics=("parallel",), operands=(dx2, r, w))


def _down_bwd_w(r, dx2):
    T = dx2.shape[0]
    tk = ROW_TILE
    nk = T // tk

    def body(r_ref, dx_ref, o_ref, acc_ref):
        k = pl.program_id(0)
        dxb = dx_ref[...].astype(BF16)

        @pl.when(k == 0)
        def _():
            acc_ref[...] = jnp.zeros_like(acc_ref)

        for c in range(0, D_FF, FF_CHUNK):
            acc_ref[:, c:c + FF_CHUNK] += _dot_tn(dxb, _square_bf16(r_ref[:, c:c + FF_CHUNK]))

        @pl.when(k == nk - 1)
        def _():
            o_ref[...] = acc_ref[...].astype(BF16)

    return pl.pallas_call(
        body, name="down_bwd_w", grid=(nk,),
        in_specs=[pl.BlockSpec((tk, D_FF), lambda k: (k, 0)),
                  pl.BlockSpec((tk, D_MODEL), lambda k: (k, 0))],
        out_specs=_resident((D_MODEL, D_FF)),
        out_shape=jax.ShapeDtypeStruct((D_MODEL, D_FF), BF16),
        scratch_shapes=[pltpu.VMEM((D_MODEL, D_FF), F32)],
        compiler_params=_params("arbitrary"),
    )(r, dx2)


def _up_bwd_act(dpre, w, x1, g, dx2, ex=None):
    T = dx2.shape[0]
    tm = ROW_TILE

    def body(dp_ref, w_ref, x_ref, g_ref, dx2_ref, dx1_ref, dg_ref):
        dh = _dot_nt(dp_ref[:, 0:FF_CHUNK], w_ref[:, 0:FF_CHUNK])
        for c in range(FF_CHUNK, D_FF, FF_CHUNK):
            dh = dh + _dot_nt(dp_ref[:, c:c + FF_CHUNK], w_ref[:, c:c + FF_CHUNK])
        gv = g_ref[...]
        _, xhat, rstd = _rms_fwd(x_ref[...], gv)
        dx, dg = _rms_bwd(dh, xhat, rstd, gv)
        dx1_ref[...] = dx2_ref[...] + dx

        @pl.when(pl.program_id(0) == 0)
        def _():
            dg_ref[...] = dg

        @pl.when(pl.program_id(0) != 0)
        def _():
            dg_ref[...] += dg

    return _pallas_hosting(
        body, ex, name="up_bwd_act", grid=(T // tm,),
        in_specs=[pl.BlockSpec((tm, D_FF), lambda i: (i, 0)),
                  _resident((D_MODEL, D_FF)),
                  pl.BlockSpec((tm, D_MODEL), lambda i: (i, 0)),
                  pl.BlockSpec((1, D_MODEL), lambda i: (0, 0)),
                  pl.BlockSpec((tm, D_MODEL), lambda i: (i, 0))],
        out_specs=[pl.BlockSpec((tm, D_MODEL), lambda i: (i, 0)),
                   pl.BlockSpec((1, D_MODEL), lambda i: (0, 0))],
        out_shape=[jax.ShapeDtypeStruct((T, D_MODEL), F32), jax.ShapeDtypeStruct((1, D_MODEL), F32)],
        scratch_shapes=[], semantics=("arbitrary",), operands=(dpre, w, x1, g, dx2))


def _up_bwd_w(h2, dpre):
    T = h2.shape[0]
    tk = ROW_TILE
    nk = T // tk

    def body(h_ref, dp_ref, o_ref, acc_ref):
        k = pl.program_id(0)
        hb = h_ref[...]

        @pl.when(k == 0)
        def _():
            acc_ref[...] = jnp.zeros_like(acc_ref)

        for c in range(0, D_FF, FF_CHUNK):
            acc_ref[:, c:c + FF_CHUNK] += _dot_tn(hb, dp_ref[:, c:c + FF_CHUNK])

        @pl.when(k == nk - 1)
        def _():
            o_ref[...] = acc_ref[...].astype(BF16)

    return pl.pallas_call(
        body, name="up_bwd_w", grid=(nk,),
        in_specs=[pl.BlockSpec((tk, D_MODEL), lambda k: (k, 0)),
                  pl.BlockSpec((tk, D_FF), lambda k: (k, 0))],
        out_specs=_resident((D_MODEL, D_FF)),
        out_shape=jax.ShapeDtypeStruct((D_MODEL, D_FF), BF16),
        scratch_shapes=[pltpu.VMEM((D_MODEL, D_FF), F32)],
        compiler_params=_params("arbitrary"),
    )(h2, dpre)


def _outproj_bwd(dx1, mix, w, o, lse):
    T = dx1.shape[0]
    tm = ROW_TILE
    nk = T // tm

    def body(dx_ref, mix_ref, w_ref, o_ref, lse_ref, do_ref, dd_ref, dc_ref, dl_ref, dw_ref, acc_ref):
        i = pl.program_id(0)
        dxb = dx_ref[...].astype(BF16)
        dmix = _dot_nt(dxb, w_ref[...])
        al = _attn_alpha(lse_ref[...])
        first = lax.broadcasted_iota(jnp.int32, (tm, 128), 1) < HEAD_DIM
        tot = jnp.zeros((tm, 128), F32)
        for p in range(3):
            sl = slice(p * 128, (p + 1) * 128)
            dy = dmix[:, sl]
            do_ref[:, sl] = dy * al[p]
            prod = dy * o_ref[:, sl]
            s0 = jnp.sum(jnp.where(first, prod, 0.0), axis=-1, keepdims=True)
            s1 = jnp.sum(jnp.where(first, 0.0, prod), axis=-1, keepdims=True)
            tot = tot + al[p] * jnp.where(first, s0, s1)
        for p in range(3):
            dd_ref[:, p * 128:(p + 1) * 128] = -al[p] * tot
        dc_ref[...] = dmix[:, ATTN_W:ATTN_W + CONV_W]
        dl_ref[...] = dmix[:, ATTN_W + CONV_W:D_MODEL]
        part = _dot_tn(mix_ref[...], dxb)

        @pl.when(i == 0)
        def _():
            acc_ref[...] = part

        @pl.when(i != 0)
        def _():
            acc_ref[...] += part

        @pl.when(i == nk - 1)
        def _():
            dw_ref[...] = acc_ref[...].astype(BF16)

    return pl.pallas_call(
        body, name="outproj_bwd", grid=(nk,),
        in_specs=[pl.BlockSpec((tm, D_MODEL), lambda i: (i, 0)),
                  pl.BlockSpec((tm, D_MODEL), lambda i: (i, 0)),
                  _resident((D_MODEL, D_MODEL)),
                  pl.BlockSpec((tm, ATTN_W), lambda i: (i, 0)),
                  pl.BlockSpec((tm, ATTN_W), lambda i: (i, 0))],
        out_specs=[pl.BlockSpec((tm, ATTN_W), lambda i: (i, 0)),
                   pl.BlockSpec((tm, ATTN_W), lambda i: (i, 0)),
                   pl.BlockSpec((tm, CONV_W), lambda i: (i, 0)),
                   pl.BlockSpec((tm, LRU_W), lambda i: (i, 0)),
                   _resident((D_MODEL, D_MODEL))],
        out_shape=[jax.ShapeDtypeStruct((T, ATTN_W), F32), jax.ShapeDtypeStruct((T, ATTN_W), F32),
                   jax.ShapeDtypeStruct((T, CONV_W), F32), jax.ShapeDtypeStruct((T, LRU_W), F32),
                   jax.ShapeDtypeStruct((D_MODEL, D_MODEL), BF16)],
        scratch_shapes=[pltpu.VMEM((D_MODEL, D_MODEL), F32)],
        compiler_params=_params("arbitrary"),
    )(dx1, mix, w, o, lse)


DZ_COLS = ((0, ATTN_W), (ATTN_W, 2 * ATTN_W), (2 * ATTN_W, QKV_W), (QKV_W, QKV_W + CONV_IN_W), (QKV_W + CONV_IN_W, IN_COLS))


def _inproj_bwd(dz_parts, h, w, x2d, g, dx1):
    T = x2d.shape[0]
    tm = ROW_TILE
    nk = T // tm
    n_parts = len(DZ_COLS)

    def body(*refs):
        dz_refs = refs[:n_parts]
        h_ref, w_ref, x_ref, g_ref, dx1_ref, dx_ref, dg_ref, dw_ref, acc_ref = refs[n_parts:]
        i = pl.program_id(0)
        dz = [r[...].astype(BF16) for r in dz_refs]
        dh = _dot_nt(dz[0], w_ref[:, DZ_COLS[0][0]:DZ_COLS[0][1]])
        for part, (lo, hi) in zip(dz[1:], DZ_COLS[1:]):
            dh = dh + _dot_nt(part, w_ref[:, lo:hi])
        gv = g_ref[...]
        _, xhat, rstd = _rms_fwd(x_ref[...], gv)
        dx, dg = _rms_bwd(dh, xhat, rstd, gv)
        dx_ref[...] = dx1_ref[...] + dx
        hb = h_ref[...]

        @pl.when(i == 0)
        def _():
            dg_ref[...] = jnp.zeros_like(dg_ref)
            acc_ref[...] = jnp.zeros_like(acc_ref)

        dg_ref[...] += dg
        for part, (lo, hi) in zip(dz, DZ_COLS):
            acc_ref[:, lo:hi] += _dot_tn(hb, part)

        @pl.when(i == nk - 1)
        def _():
            dw_ref[...] = acc_ref[...].astype(BF16)

    rows = lambda width: pl.BlockSpec((tm, width), lambda i: (i, 0))
    return pl.pallas_call(
        body, name="inproj_bwd", grid=(nk,),
        in_specs=[rows(hi - lo) for lo, hi in DZ_COLS] + [
            rows(D_MODEL), _resident((D_MODEL, IN_COLS)), rows(D_MODEL), pl.BlockSpec((1, D_MODEL), lambda i: (0, 0)),
            rows(D_MODEL)],
        out_specs=[rows(D_MODEL), pl.BlockSpec((1, D_MODEL), lambda i: (0, 0)), _resident((D_MODEL, IN_COLS))],
        out_shape=[jax.ShapeDtypeStruct((T, D_MODEL), F32), jax.ShapeDtypeStruct((1, D_MODEL), F32),
                   jax.ShapeDtypeStruct((D_MODEL, IN_COLS), BF16)],
        scratch_shapes=[pltpu.VMEM((D_MODEL, IN_COLS), F32)],
        compiler_params=_params("arbitrary"),
    )(*dz_parts, h, w, x2d, g, dx1)


def _alibi_coef():
    slopes = 2.0 ** (-8.0 * np.arange(1, 7) / 6)
    return jnp.asarray((slopes.reshape(3, 2) * np.asarray(ATTN_DILATIONS)[:, None]).astype(np.float32))


def _unit_rows(u, d):
    nb = N_UNITS // d
    r, n = u // nb, u % nb
    span = ATTN_BLOCK * d

    def rows(block):
        start = block * span + r
        return pl.ds(pl.multiple_of(start, ATTN_BLOCK), ATTN_BLOCK) if d == 1 else pl.ds(start, ATTN_BLOCK, stride=d)

    return rows(n), rows(jnp.maximum(n - 1, 0)), rows(jnp.minimum(n + 1, nb - 1)), n > 0, n + 1 < nb


def _per_pattern(fn):
    for p, d in enumerate(ATTN_DILATIONS):
        pl.when(pl.program_id(1) == p)(functools.partial(fn, p, d))


def _attn_col(offset):
    return pl.BlockSpec((None, SEQ_LEN, 128), lambda b, p: (b, 0, p + offset))


def _attn_masks():
    qi = lax.broadcasted_iota(jnp.int32, (ATTN_BLOCK, 2 * ATTN_BLOCK), 0)
    kj = lax.broadcasted_iota(jnp.int32, (ATTN_BLOCK, 2 * ATTN_BLOCK), 1)
    dist = qi + ATTN_BLOCK - kj
    first = lax.broadcasted_iota(jnp.int32, (ATTN_BLOCK, 128), 1) < HEAD_DIM
    return dist.astype(F32), (dist >= 0) & (dist <= ATTN_BLOCK), kj >= ATTN_BLOCK, first


def _head_lanes(a, first, j):
    return jnp.where(first if j == 0 else jnp.logical_not(first), a, jnp.zeros_like(a))


def _load_kv(ref, prev, own):
    return jnp.concatenate([ref[prev, :], ref[own, :]], axis=0).astype(BF16)


def _attn_fwd(qkv, ex=None):
    B = qkv.shape[0]

    def body(coef_ref, q_ref, k_ref, v_ref, o_ref, lse_ref):
        dist, band, own_half, first = _attn_masks()

        def pattern(p, d):
            def unit(u, carry):
                own, prev, _, has_prev, _ = _unit_rows(u, d)
                ok = band & jnp.logical_or(own_half, has_prev)
                q = q_ref[own, :].astype(BF16)
                kcat, vcat = _load_kv(k_ref, prev, own), _load_kv(v_ref, prev, own)
                outs, lses = [], []
                for j in range(2):
                    s = jnp.where(ok, _dot_nt(_head_lanes(q, first, j), kcat) * 0.125 - coef_ref[p, j] * dist, NEG_BIG)
                    m = jnp.max(s, axis=-1, keepdims=True)
                    e = jnp.exp(s - m)
                    l = jnp.sum(e, axis=-1, keepdims=True)
                    outs.append(_dot(e.astype(BF16), vcat) * (1.0 / l))
                    lses.append(m + jnp.log(l))
                o_ref[own, :] = jnp.where(first, outs[0], outs[1])
                lse_ref[own, :] = jnp.where(first, lses[0], lses[1])
                return carry

            lax.fori_loop(0, N_UNITS, unit, 0, unroll=UNIT_UNROLL)

        _per_pattern(pattern)

    shape = jax.ShapeDtypeStruct((B, SEQ_LEN, ATTN_W), F32)
    return _pallas_hosting(
        body, ex, name="attn_fwd", grid=(B, 3),
        in_specs=[pl.BlockSpec(memory_space=pltpu.SMEM), _attn_col(0), _attn_col(3), _attn_col(6)],
        out_specs=[_attn_col(0), _attn_col(0)],
        out_shape=[shape, shape],
        scratch_shapes=[], semantics=("parallel", "parallel"), operands=(_alibi_coef(), qkv, qkv, qkv))


def _attn_bwd(qkv, do, lse, dd):
    B = qkv.shape[0]

    def body(coef_ref, q_ref, k_ref, v_ref, do_ref, lse_ref, dd_ref, dq_ref, dk_ref, dv_ref):
        dist, band, own_half, first = _attn_masks()

        def pattern(p, d):
            def unit(u, carry):
                own, prev, _, has_prev, _ = _unit_rows(u, d)
                ok = band & jnp.logical_or(own_half, has_prev)
                q, do = q_ref[own, :].astype(BF16), do_ref[own, :].astype(BF16)
                kcat, vcat = _load_kv(k_ref, prev, own), _load_kv(v_ref, prev, own)
                lse_a, dd_a = lse_ref[own, :], dd_ref[own, :]
                dqs, dks, dvs = [], [], []
                for j in range(2):
                    col = slice(HEAD_DIM * j, HEAD_DIM * j + 1)
                    s = _dot_nt(_head_lanes(q, first, j), kcat) * 0.125 - coef_ref[p, j] * dist
                    pr = jnp.where(ok, jnp.exp(jnp.where(ok, s, NEG_BIG) - lse_a[:, col]), 0.0)
                    ds = (pr * (_dot_nt(_head_lanes(do, first, j), vcat) + dd_a[:, col])).astype(BF16)
                    dqs.append(_dot(ds, kcat))
                    dks.append(_dot_tn(ds, q))
                    dvs.append(_dot_tn(pr.astype(BF16), do))
                both = lambda pair: jnp.where(jnp.concatenate([first] * (pair[0].shape[0] // ATTN_BLOCK), axis=0), *pair)
                dq_ref[own, :] = both(dqs) * 0.125
                dk, dv = both(dks) * 0.125, both(dvs)
                dk_ref[own, :] = dk[ATTN_BLOCK:]
                dv_ref[own, :] = dv[ATTN_BLOCK:]
                dk_ref[prev, :] += dk[:ATTN_BLOCK]
                dv_ref[prev, :] += dv[:ATTN_BLOCK]
                return carry

            lax.fori_loop(0, N_UNITS, unit, 0, unroll=UNIT_UNROLL)

        _per_pattern(pattern)

    shape = jax.ShapeDtypeStruct((B, SEQ_LEN, ATTN_W), F32)
    return pl.pallas_call(
        body, name="attn_bwd", grid=(B, 3),
        in_specs=[pl.BlockSpec(memory_space=pltpu.SMEM), _attn_col(0), _attn_col(3), _attn_col(6), _attn_col(0), _attn_col(0),
                  _attn_col(0)],
        out_specs=[_attn_col(0)] * 3,
        out_shape=[shape] * 3,
        compiler_params=_params("parallel", "parallel"),
    )(_alibi_coef(), qkv, qkv, qkv, do, lse, dd)


def _for_chunks(n_rows, fn, chunk):
    def step(c, carry):
        fn(pl.multiple_of(c * chunk, chunk))
        return carry

    lax.fori_loop(0, n_rows // chunk, step, 0)


def _shift_down(win, s, rows):
    lead = win.shape[0] - rows
    if s == 0:
        return win[lead:]
    if s % 8 == 0:
        return win[lead - s:lead - s + rows]
    q, r = divmod(s, 8)
    rolled = pltpu.roll(win, r, 0)
    return rolled[lead - 8 * q:lead - 8 * q + rows]


def _shift_up(win, s, rows):
    if s % 8 == 0:
        return win[s:s + rows]
    q, r = divmod(s, 8)
    rolled = pltpu.roll(win, win.shape[0] - r, 0)
    return rolled[8 * q:8 * q + rows]


CONV_PAD = 32


def _ln_silu(c, lg, lb):
    mu = jnp.mean(c, axis=-1, keepdims=True)
    cc = c - mu
    rstd = lax.rsqrt(jnp.mean(cc * cc, axis=-1, keepdims=True) + LN_EPS)
    nrm = cc * rstd
    v = nrm * lg + lb
    sg = _sigmoid(v)
    return v * sg, nrm, rstd, v, sg


def _conv_fwd(ci, w, b, lg, lb):
    B, S, _ = ci.shape
    CH = CONV_FWD_CHUNK

    def body(ci_ref, w_ref, b_ref, lg_ref, lb_ref, y_ref, c_ref, pad_ref):
        pad_ref[0:CONV_PAD, :] = jnp.zeros((CONV_PAD, CONV_W), F32)

        def glu(base):
            blk = ci_ref[pl.ds(base, CH), :]
            pad_ref[pl.ds(CONV_PAD + base, CH), :] = blk[:, 0:CONV_W] * _sigmoid(blk[:, CONV_W:])

        _for_chunks(S, glu, CH)

        def conv(base):
            win = pad_ref[pl.ds(base, CH + CONV_PAD), :]
            acc = jnp.broadcast_to(b_ref[...], (CH, CONV_W))
            for k in range(CONV_TAPS):
                acc = acc + w_ref[k:k + 1, :] * _shift_down(win, CONV_TAPS - 1 - k, CH)
            c_ref[pl.ds(base, CH), :] = acc
            y, _, _, _, _ = _ln_silu(acc, lg_ref[...], lb_ref[...])
            y_ref[pl.ds(base, CH), :] = y.astype(BF16)

        _for_chunks(S, conv, CH)

    vec = pl.BlockSpec((1, CONV_W), lambda i: (0, 0))
    return pl.pallas_call(
        body, name="conv_fwd", grid=(B,),
        in_specs=[pl.BlockSpec((None, S, CONV_IN_W), lambda i: (i, 0, 0)),
                  pl.BlockSpec((CONV_TAPS, CONV_W), lambda i: (0, 0)), vec, vec, vec],
        out_specs=[pl.BlockSpec((None, S, CONV_W), lambda i: (i, 0, 0)),
                   pl.BlockSpec((None, S, CONV_W), lambda i: (i, 0, 0))],
        out_shape=[jax.ShapeDtypeStruct((B, S, CONV_W), BF16), jax.ShapeDtypeStruct((B, S, CONV_W), F32)],
        scratch_shapes=[pltpu.VMEM((S + CONV_PAD, CONV_W), F32)],
        compiler_params=_params("parallel"),
    )(ci, w, b, lg, lb)


def _conv_bwd(ci, cpre, dy, w, lg, lb, ex=None):
    B, S, _ = ci.shape
    CH = CONV_BWD_CHUNK

    def body(ci_ref, c_ref, dy_ref, w_ref, lg_ref, lb_ref, dci_ref, dw_ref, db_ref, dlg_ref, dlb_ref, upad_ref, dcpad_ref,
             dwacc_ref):
        @pl.when(pl.program_id(0) == 0)
        def _():
            dw_ref[...] = jnp.zeros_like(dw_ref)
            db_ref[...] = jnp.zeros_like(db_ref)
            dlg_ref[...] = jnp.zeros_like(dlg_ref)
            dlb_ref[...] = jnp.zeros_like(dlb_ref)

        upad_ref[0:CONV_PAD, :] = jnp.zeros((CONV_PAD, CONV_W), F32)
        dcpad_ref[S:S + CONV_PAD, :] = jnp.zeros((CONV_PAD, CONV_W), F32)
        dwacc_ref[...] = jnp.zeros_like(dwacc_ref)

        def norm_bwd(base):
            blk = ci_ref[pl.ds(base, CH), :]
            upad_ref[pl.ds(CONV_PAD + base, CH), :] = blk[:, 0:CONV_W] * _sigmoid(blk[:, CONV_W:])
            lgv = lg_ref[...]
            _, nrm, rstd, v, sg = _ln_silu(c_ref[pl.ds(base, CH), :], lgv, lb_ref[...])
            dv = dy_ref[pl.ds(base, CH), :] * (sg * (1.0 + v * (1.0 - sg)))
            dlg_ref[...] += jnp.sum(dv * nrm, axis=0, keepdims=True)
            dlb_ref[...] += jnp.sum(dv, axis=0, keepdims=True)
            dn = dv * lgv
            dc = rstd * (dn - jnp.mean(dn, axis=-1, keepdims=True) - nrm * jnp.mean(dn * nrm, axis=-1, keepdims=True))
            dcpad_ref[pl.ds(base, CH), :] = dc
            db_ref[...] += jnp.sum(dc, axis=0, keepdims=True)

        _for_chunks(S, norm_bwd, CH)

        def conv_bwd(base):
            dwin = dcpad_ref[pl.ds(base, CH + CONV_PAD), :]
            uwin = upad_ref[pl.ds(base, CH + CONV_PAD), :]
            dc = dwin[0:CH]
            du = jnp.zeros((CH, CONV_W), F32)
            for k in range(CONV_TAPS):
                du = du + w_ref[k:k + 1, :] * _shift_up(dwin, CONV_TAPS - 1 - k, CH)
                prod = dc * _shift_down(uwin, CONV_TAPS - 1 - k, CH)
                dwacc_ref[8 * k:8 * k + 8, :] += jnp.sum(prod.reshape(CH // 8, 8, CONV_W), axis=0)
            blk = ci_ref[pl.ds(base, CH), :]
            a, sg = blk[:, 0:CONV_W], _sigmoid(blk[:, CONV_W:])
            dci_ref[pl.ds(base, CH), 0:CONV_W] = (du * sg).astype(BF16)
            dci_ref[pl.ds(base, CH), CONV_W:] = (du * a * sg * (1.0 - sg)).astype(BF16)

        _for_chunks(S, conv_bwd, CH)
        for k in range(CONV_TAPS):
            dw_ref[k:k + 1, :] += jnp.sum(dwacc_ref[8 * k:8 * k + 8, :], axis=0, keepdims=True)

    vec = pl.BlockSpec((1, CONV_W), lambda i: (0, 0))
    mat = pl.BlockSpec((CONV_TAPS, CONV_W), lambda i: (0, 0))
    seq = lambda width: pl.BlockSpec((None, S, width), lambda i: (i, 0, 0))
    return _pallas_hosting(
        body, ex, name="conv_bwd", grid=(B,),
        in_specs=[seq(CONV_IN_W), seq(CONV_W), seq(CONV_W), mat, vec, vec],
        out_specs=[seq(CONV_IN_W), mat, vec, vec, vec],
        out_shape=[jax.ShapeDtypeStruct((B, S, CONV_IN_W), BF16), jax.ShapeDtypeStruct((CONV_TAPS, CONV_W), F32),
                   jax.ShapeDtypeStruct((1, CONV_W), F32), jax.ShapeDtypeStruct((1, CONV_W), F32),
                   jax.ShapeDtypeStruct((1, CONV_W), F32)],
        scratch_shapes=[pltpu.VMEM((S + CONV_PAD, CONV_W), F32), pltpu.VMEM((S + CONV_PAD, CONV_W), F32),
                        pltpu.VMEM((8 * CONV_TAPS, CONV_W), F32)],
        semantics=("arbitrary",), operands=(ci, cpre, dy, w, lg, lb))


SCAN_SHIFTS = tuple(1 << e for e in range(11))


def _prev8(ref, base, cols, fill):
    start = pl.multiple_of(jnp.maximum(base - 8, 0), 8)
    return jnp.where(base > 0, ref[pl.ds(start, 8), cols], fill)


def _next8(ref, base, rows, total, cols, fill):
    start = pl.multiple_of(jnp.minimum(base + rows, total - 8), 8)
    return jnp.where(base + rows < total, ref[pl.ds(start, 8), cols], fill)


ALL = slice(None)
LRU_X = slice(LRU_W, LRU_IN_W)
LRU_GATE = slice(0, LRU_W)


def _lru_conv(li_ref, base, rows, cw_ref, cb_ref):
    win = jnp.concatenate([_prev8(li_ref, base, LRU_X, 0.0), li_ref[pl.ds(base, rows), LRU_X]], axis=0)
    u = jnp.broadcast_to(cb_ref[...], (rows, LRU_W))
    for k in range(LRU_TAPS):
        u = u + cw_ref[k:k + 1, :] * _shift_down(win, LRU_TAPS - 1 - k, rows)
    return u, win


def _lru_gates(u, wa_ref, ba_ref, wx_ref, bx_ref, sp):
    ub = u.astype(BF16)
    r = _sigmoid(_dot(ub, wa_ref[...]) + ba_ref[...])
    i = _sigmoid(_dot(ub, wx_ref[...]) + bx_ref[...])
    la = (-LRU_C) * r * sp
    a = jnp.exp(la)
    return ub, r, i, a, _one_minus_exp(2.0 * la, a * a)


def _scan_forward(bufs, S, CH):
    for n, s in enumerate(SCAN_SHIFTS):
        (sa, sb), (da, db) = bufs[n % 2], bufs[(n + 1) % 2]

        def step(base, s=s, sa=sa, sb=sb, da=da, db=db):
            a, b = sa[pl.ds(base, CH), :], sb[pl.ds(base, CH), :]
            if s < 8:
                a_s = _shift_down(jnp.concatenate([_prev8(sa, base, ALL, 1.0), a], axis=0), s, CH)
                b_s = _shift_down(jnp.concatenate([_prev8(sb, base, ALL, 0.0), b], axis=0), s, CH)
            elif s < CH:
                start = pl.multiple_of(jnp.maximum(base - s, 0), 8)
                a_s = jnp.concatenate([jnp.where(base > 0, sa[pl.ds(start, s), :], 1.0), a[0:CH - s]], axis=0)
                b_s = jnp.concatenate([jnp.where(base > 0, sb[pl.ds(start, s), :], 0.0), b[0:CH - s]], axis=0)
            else:
                start = pl.multiple_of(jnp.maximum(base - s, 0), 8)
                a_s = jnp.where(base < s, 1.0, sa[pl.ds(start, CH), :])
                b_s = jnp.where(base < s, 0.0, sb[pl.ds(start, CH), :])
            db[pl.ds(base, CH), :] = a * b_s + b
            da[pl.ds(base, CH), :] = a * a_s

        _for_chunks(S, step, CH)
    return len(SCAN_SHIFTS) % 2


def _scan_backward(bufs, S, CH):
    for n, s in enumerate(SCAN_SHIFTS):
        (sa, sb), (da, db) = bufs[n % 2], bufs[(n + 1) % 2]

        def step(base, s=s, sa=sa, sb=sb, da=da, db=db):
            a, b = sa[pl.ds(base, CH), :], sb[pl.ds(base, CH), :]
            if s < 8:
                a_s = _shift_up(jnp.concatenate([a, _next8(sa, base, CH, S, ALL, 1.0)], axis=0), s, CH)
                b_s = _shift_up(jnp.concatenate([b, _next8(sb, base, CH, S, ALL, 0.0)], axis=0), s, CH)
            elif s < CH:
                start = pl.multiple_of(jnp.minimum(base + CH, S - s), 8)
                more = base + CH < S
                a_s = jnp.concatenate([a[s:CH], jnp.where(more, sa[pl.ds(start, s), :], 1.0)], axis=0)
                b_s = jnp.concatenate([b[s:CH], jnp.where(more, sb[pl.ds(start, s), :], 0.0)], axis=0)
            else:
                start = pl.multiple_of(jnp.minimum(base + s, S - CH), 8)
                a_s = jnp.where(base + s >= S, 1.0, sa[pl.ds(start, CH), :])
                b_s = jnp.where(base + s >= S, 0.0, sb[pl.ds(start, CH), :])
            db[pl.ds(base, CH), :] = a * b_s + b
            da[pl.ds(base, CH), :] = a * a_s

        _for_chunks(S, step, CH)
    return len(SCAN_SHIFTS) % 2


def _lru_fwd(li, cw, cb, wa, ba, wx, bx, lam, ex=None):
    B, S, _ = li.shape
    CH = LRU_FWD_CHUNK

    def body(li_ref, cw_ref, cb_ref, wa_ref, ba_ref, wx_ref, bx_ref, lam_ref, y_ref, h_ref, a0, b0, a1, b1):
        sp = _softplus(-lam_ref[...])

        def gates(base):
            u, _ = _lru_conv(li_ref, base, CH, cw_ref, cb_ref)
            _, _, i, a, em = _lru_gates(u, wa_ref, ba_ref, wx_ref, bx_ref, sp)
            a0[pl.ds(base, CH), :] = a
            b0[pl.ds(base, CH), :] = jnp.sqrt(em) * (i * u)

        _for_chunks(S, gates, CH)
        bufs = ((a0, b0), (a1, b1))
        hb = bufs[_scan_forward(bufs, S, CH)][1]

        def out(base):
            h = hb[pl.ds(base, CH), :]
            h_ref[pl.ds(base, CH), :] = h
            gl, _ = _gelu(li_ref[pl.ds(base, CH), LRU_GATE])
            y_ref[pl.ds(base, CH), :] = (gl * h).astype(BF16)

        _for_chunks(S, out, CH)

    vec = pl.BlockSpec((1, LRU_W), lambda i: (0, 0))
    mat = pl.BlockSpec((LRU_W, LRU_W), lambda i: (0, 0))
    seq = lambda width: pl.BlockSpec((None, S, width), lambda i: (i, 0, 0))
    return _pallas_hosting(
        body, ex, name="lru_fwd", grid=(B,),
        in_specs=[seq(LRU_IN_W), pl.BlockSpec((LRU_TAPS, LRU_W), lambda i: (0, 0)), vec, mat, vec, mat, vec, vec],
        out_specs=[seq(LRU_W), seq(LRU_W)],
        out_shape=[jax.ShapeDtypeStruct((B, S, LRU_W), BF16), jax.ShapeDtypeStruct((B, S, LRU_W), F32)],
        scratch_shapes=[pltpu.VMEM((S, LRU_W), F32)] * 4,
        semantics=("parallel",), operands=(li, cw, cb, wa, ba, wx, bx, lam))


def _lru_bwd(li, hs, dy, cw, cb, wa, ba, wx, bx, lam, ex=None):
    B, S, _ = li.shape
    CH = LRU_BWD_CHUNK

    def body(li_ref, hs_ref, dy_ref, cw_ref, cb_ref, wa_ref, ba_ref, wx_ref, bx_ref, lam_ref,
             dli_ref, dcw_ref, dcb_ref, dwa_ref, dba_ref, dwx_ref, dbx_ref, dlam_ref, a0, b0, a1, b1, u_s, du_s):
        @pl.when(pl.program_id(0) == 0)
        def _():
            for ref in (dcw_ref, dcb_ref, dwa_ref, dba_ref, dwx_ref, dbx_ref, dlam_ref):
                ref[...] = jnp.zeros_like(ref)

        lam_v = lam_ref[...]
        sp = _softplus(-lam_v)
        dsp_dlam = -_sigmoid(-lam_v)

        def gates(base):
            u, _ = _lru_conv(li_ref, base, CH, cw_ref, cb_ref)
            _, _, _, a, _ = _lru_gates(u, wa_ref, ba_ref, wx_ref, bx_ref, sp)
            gl, _ = _gelu(li_ref[pl.ds(base, CH), LRU_GATE])
            u_s[pl.ds(base, CH), :] = u
            a0[pl.ds(base, CH), :] = a
            b0[pl.ds(base, CH), :] = a * (dy_ref[pl.ds(base, CH), :] * gl)

        _for_chunks(S, gates, CH)
        bufs = ((a0, b0), (a1, b1))
        eb = bufs[_scan_backward(bufs, S, CH)][1]

        def grads(base):
            e = eb[pl.ds(base, CH), :]
            e_next = _shift_up(jnp.concatenate([e, _next8(eb, base, CH, S, ALL, 0.0)], axis=0), 1, CH)
            gate = li_ref[pl.ds(base, CH), LRU_GATE]
            gl, th = _gelu(gate)
            dy = dy_ref[pl.ds(base, CH), :]
            g = dy * gl + e_next
            h = hs_ref[pl.ds(base, CH), :]
            h_prev = _shift_down(jnp.concatenate([_prev8(hs_ref, base, ALL, 0.0), h], axis=0), 1, CH)
            dli_ref[pl.ds(base, CH), LRU_GATE] = (dy * h * _gelu_grad(gate, th)).astype(BF16)
            u = u_s[pl.ds(base, CH), :]
            ub, r, i, a, em = _lru_gates(u, wa_ref, ba_ref, wx_ref, bx_ref, sp)
            mult = jnp.sqrt(em)
            da = g * h_prev
            dmult = g * (i * u)
            di = g * mult * u
            dla = da * a - dmult * (a * a) * lax.rsqrt(jnp.maximum(em, 1e-30))
            dlam_ref[...] += dsp_dlam * jnp.sum(dla * ((-LRU_C) * r), axis=0, keepdims=True)
            dpa = (dla * ((-LRU_C) * sp)) * r * (1.0 - r)
            dpx = di * i * (1.0 - i)
            dpab, dpxb = dpa.astype(BF16), dpx.astype(BF16)
            dwa_ref[...] += _dot_tn(ub, dpab)
            dwx_ref[...] += _dot_tn(ub, dpxb)
            dba_ref[...] += jnp.sum(dpa, axis=0, keepdims=True)
            dbx_ref[...] += jnp.sum(dpx, axis=0, keepdims=True)
            du = g * mult * i + _dot_nt(dpab, wa_ref[...]) + _dot_nt(dpxb, wx_ref[...])
            du_s[pl.ds(base, CH), :] = du
            dcb_ref[...] += jnp.sum(du, axis=0, keepdims=True)

        _for_chunks(S, grads, CH)

        def conv_bwd(base):
            du = du_s[pl.ds(base, CH), :]
            dwin = jnp.concatenate([du, _next8(du_s, base, CH, S, ALL, 0.0)], axis=0)
            xwin = jnp.concatenate([_prev8(li_ref, base, LRU_X, 0.0), li_ref[pl.ds(base, CH), LRU_X]], axis=0)
            dx = jnp.zeros((CH, LRU_W), F32)
            for k in range(LRU_TAPS):
                dx = dx + cw_ref[k:k + 1, :] * _shift_up(dwin, LRU_TAPS - 1 - k, CH)
                dcw_ref[k:k + 1, :] += jnp.sum(du * _shift_down(xwin, LRU_TAPS - 1 - k, CH), axis=0, keepdims=True)
            dli_ref[pl.ds(base, CH), LRU_X] = dx.astype(BF16)

        _for_chunks(S, conv_bwd, CH)

    vec = pl.BlockSpec((1, LRU_W), lambda i: (0, 0))
    mat = pl.BlockSpec((LRU_W, LRU_W), lambda i: (0, 0))
    taps = pl.BlockSpec((LRU_TAPS, LRU_W), lambda i: (0, 0))
    seq = lambda width: pl.BlockSpec((None, S, width), lambda i: (i, 0, 0))
    vec_shape = jax.ShapeDtypeStruct((1, LRU_W), F32)
    mat_shape = jax.ShapeDtypeStruct((LRU_W, LRU_W), F32)
    return _pallas_hosting(
        body, ex, name="lru_bwd", grid=(B,),
        in_specs=[seq(LRU_IN_W), seq(LRU_W), seq(LRU_W), taps, vec, mat, vec, mat, vec, vec],
        out_specs=[seq(LRU_IN_W), taps, vec, mat, vec, mat, vec, vec],
        out_shape=[jax.ShapeDtypeStruct((B, S, LRU_IN_W), BF16), jax.ShapeDtypeStruct((LRU_TAPS, LRU_W), F32),
                   vec_shape, mat_shape, vec_shape, mat_shape, vec_shape, vec_shape],
        scratch_shapes=[pltpu.VMEM((S, LRU_W), F32)] * 6,
        semantics=("arbitrary",), operands=(li, hs, dy, cw, cb, wa, ba, wx, bx, lam))


MESH = pl.DeviceIdType.MESH
HBM_SPEC = pl.BlockSpec(memory_space=pltpu.HBM)


def _slot(ref, p):
    return ref.at[p]


def _row_block(rows):
    return lambda ref, p: ref.at[pl.ds(p * rows, rows), :]


def _col_block(cols):
    return lambda ref, p: ref.at[:, pl.ds(p * cols, cols)]


class _Gather:
    def __init__(self, blocks, out_shapes, places):
        self.sources, self.out_shapes, self.places, self.n = list(blocks), list(out_shapes), list(places), len(blocks)

    def scratch(self):
        return [pltpu.SemaphoreType.DMA((self.n, 7)), pltpu.SemaphoreType.DMA((self.n, 7)), pltpu.SemaphoreType.DMA((self.n,))]

    def _plan(self, x_refs, out_refs, send_sems, recv_sems, local_sems):
        n = self.n
        x, y, c = lax.axis_index("x"), lax.axis_index("y"), lax.axis_index("c")
        me, sibling = (x, y, c), (x, y, 1 - c)
        chips = [(1 - x, y), (x, 1 - y), (1 - x, 1 - y)]

        def place(a, dev):
            return self.places[a](out_refs[a], 4 * dev[0] + 2 * dev[1] + dev[2])

        def copy(a, k, blk, to, src=None):
            return pltpu.make_async_remote_copy(
                src_ref=place(a, blk) if src is None else src, dst_ref=place(a, blk),
                send_sem=send_sems.at[a, k], recv_sem=recv_sems.at[a, k], device_id=to, device_id_type=MESH)

        mine = [pltpu.make_async_copy(x_refs[a], place(a, me), local_sems.at[a]) for a in range(n)]
        first = [copy(a, 0, me, sibling, src=x_refs[a]) for a in range(n)]
        first += [copy(a, 1 + j, me, (*chip, c), src=x_refs[a]) for j, chip in enumerate(chips) for a in range(n)]
        return me, sibling, chips, c, copy, mine, first

    def start(self, *refs):
        *_, mine, first = self._plan(*refs)
        for cp in mine + first:
            cp.start()

    def finish(self, *refs):
        me, sibling, chips, c, copy, mine, first = self._plan(*refs)
        passed = []
        for j, chip in enumerate(chips):
            for a in range(self.n):
                copy(a, 1 + j, (*chip, c), me).wait_recv()
                passed.append(copy(a, 4 + j, (*chip, c), sibling))
                passed[-1].start()
        for a in range(self.n):
            copy(a, 0, sibling, me).wait_recv()
        for j, chip in enumerate(chips):
            for a in range(self.n):
                copy(a, 4 + j, (*chip, 1 - c), me).wait_recv()
        for cp in first + passed:
            cp.wait_send()
        for cp in mine:
            cp.wait()


class _GradExchange:
    def __init__(self, sources, takes, piece_shapes):
        self.sources, self.takes, self.n = list(sources), list(takes), len(sources)
        self.out_shapes = [jax.ShapeDtypeStruct((N_DEV,) + tuple(s), BF16) for s in piece_shapes]

    def scratch(self):
        return [pltpu.SemaphoreType.DMA((self.n, 7)), pltpu.SemaphoreType.DMA((self.n, 7)), pltpu.SemaphoreType.DMA((self.n,))]

    def _copies(self, src_refs, out_refs, send_sems, recv_sems, local_sems):
        x, y, c = lax.axis_index("x"), lax.axis_index("y"), lax.axis_index("c")
        me = 4 * x + 2 * y + c
        mine = [pltpu.make_async_copy(self.takes[i](src_refs[i], me), out_refs[i].at[me], local_sems.at[i]) for i in range(self.n)]
        remote = []
        for k in range(1, N_DEV):
            px, py, pc = x ^ ((k >> 2) & 1), y ^ ((k >> 1) & 1), c ^ (k & 1)
            peer = 4 * px + 2 * py + pc
            for i in range(self.n):
                remote.append(pltpu.make_async_remote_copy(
                    src_ref=self.takes[i](src_refs[i], peer), dst_ref=out_refs[i].at[me], send_sem=send_sems.at[i, k - 1],
                    recv_sem=recv_sems.at[i, k - 1], device_id=(px, py, pc), device_id_type=MESH))
        return mine, remote

    def start(self, *refs):
        mine, remote = self._copies(*refs)
        for cp in mine + remote:
            cp.start()

    def finish(self, *refs):
        mine, remote = self._copies(*refs)
        for cp in remote:
            cp.wait_recv()
        for cp in remote:
            cp.wait_send()
        for cp in mine:
            cp.wait()


class _Together:
    def __init__(self, a, b):
        self.a, self.b, self.n = a, b, a.n + b.n
        self.sources, self.out_shapes = a.sources + b.sources, a.out_shapes + b.out_shapes

    def scratch(self):
        return self.a.scratch() + self.b.scratch()

    def _split(self, src_refs, out_refs, *sems):
        k, s = self.a.n, len(self.a.scratch())
        return (src_refs[:k], out_refs[:k], *sems[:s]), (src_refs[k:], out_refs[k:], *sems[s:])

    def start(self, *refs):
        ra, rb = self._split(*refs)
        self.a.start(*ra)
        self.b.start(*rb)

    def finish(self, *refs):
        ra, rb = self._split(*refs)
        self.a.finish(*ra)
        self.b.finish(*rb)


def _run_exchange(ex, name):
    def body(*refs):
        src_refs, out_refs, sems = refs[:ex.n], refs[ex.n:2 * ex.n], refs[2 * ex.n:]
        ex.start(src_refs, out_refs, *sems)
        ex.finish(src_refs, out_refs, *sems)

    return pl.pallas_call(
        body, name=name, out_shape=ex.out_shapes, in_specs=[HBM_SPEC] * ex.n, out_specs=[HBM_SPEC] * ex.n,
        scratch_shapes=ex.scratch(),
    )(*ex.sources)


def _pallas_hosting(body, ex, *, name, grid, in_specs, out_specs, out_shape, scratch_shapes, semantics, operands):
    if ex is None:
        outs = pl.pallas_call(body, name=name, grid=grid, in_specs=in_specs, out_specs=out_specs, out_shape=out_shape,
                              scratch_shapes=scratch_shapes, compiler_params=_params(*semantics))(*operands)
        return outs, None
    n_in, n_out, n_scr, n = len(in_specs), len(out_specs), len(scratch_shapes), ex.n

    def at_step(pick):
        conds = [pl.program_id(k) == pick(size) for k, size in enumerate(grid)]
        return functools.reduce(jnp.logical_and, conds)

    def hosting(*refs):
        ins, ex_ins = refs[:n_in], refs[n_in:n_in + n]
        outs, ex_outs = refs[n_in + n:n_in + n + n_out], refs[n_in + n + n_out:n_in + 2 * n + n_out]
        scratch, sems = refs[n_in + 2 * n + n_out:n_in + 2 * n + n_out + n_scr], refs[n_in + 2 * n + n_out + n_scr:]
        pl.when(at_step(lambda size: 0))(lambda: ex.start(ex_ins, ex_outs, *sems))
        body(*ins, *outs, *scratch)
        pl.when(at_step(lambda size: size - 1))(lambda: ex.finish(ex_ins, ex_outs, *sems))

    res = pl.pallas_call(
        hosting, name=name, grid=grid, in_specs=list(in_specs) + [HBM_SPEC] * n, out_specs=list(out_specs) + [HBM_SPEC] * n,
        out_shape=list(out_shape) + ex.out_shapes, scratch_shapes=list(scratch_shapes) + ex.scratch(),
        compiler_params=_params(*(["arbitrary"] * len(grid))),
    )(*operands, *ex.sources)
    return res[:n_out], res[n_out:]


def _sum_slots(parts, rows_per_step, name):
    _, R, C = parts.shape

    def body(p_ref, o_ref):
        acc = p_ref[0].astype(F32)
        for q in range(1, N_DEV):
            acc = acc + p_ref[q].astype(F32)
        o_ref[...] = acc

    return pl.pallas_call(
        body, name=name, grid=(R // rows_per_step,),
        in_specs=[pl.BlockSpec((N_DEV, rows_per_step, C), lambda i: (0, i, 0))],
        out_specs=pl.BlockSpec((rows_per_step, C), lambda i: (i, 0)),
        out_shape=jax.ShapeDtypeStruct((R, C), F32),
        compiler_params=_params("parallel"),
    )(parts)


def _adamw(w, g, m, v, rows_per_step, name):
    R, C = w.shape
    c1 = 1.0 - ADAM_B1 ** ADAM_STEP
    c2 = 1.0 - ADAM_B2 ** ADAM_STEP

    def body(w_ref, g_ref, m_ref, v_ref, d_ref, nm_ref, nv_ref):
        gv = g_ref[...]
        nm = ADAM_B1 * m_ref[...] + (1.0 - ADAM_B1) * gv
        nv = ADAM_B2 * v_ref[...] + (1.0 - ADAM_B2) * (gv * gv)
        nm_ref[...] = nm
        nv_ref[...] = nv
        d_ref[...] = (-ADAM_LR) * ((nm / c1) / (jnp.sqrt(nv / c2) + ADAM_EPS) + ADAM_WD * w_ref[...])

    spec = pl.BlockSpec((rows_per_step, C), lambda i: (i, 0))
    shape = jax.ShapeDtypeStruct((R, C), F32)
    return pl.pallas_call(
        body, name=name, grid=(R // rows_per_step,),
        in_specs=[spec] * 4, out_specs=[spec] * 3, out_shape=[shape] * 3,
        compiler_params=_params("parallel"),
    )(w, g, m, v)


WEIGHT_ORDER = ("norm1_g", "w_in", "conv_dw_w", "conv_dw_b", "conv_ln_g", "conv_ln_b", "lru_conv_w", "lru_conv_b", "lru_wa",
                "lru_ba", "lru_wx", "lru_bx", "lru_lambda", "w_out", "norm2_g", "w_up", "w_down", "final_g")
BIG = ("w_in", "w_out", "w_up", "w_down")
SMALL_SHARDED = {"conv_dw_w": (DEPTH, CONV_TAPS, CONV_W), "lru_conv_w": (DEPTH, LRU_TAPS, LRU_W)}
SMALL_FULL = {
    "norm1_g": (DEPTH, D_MODEL), "conv_dw_w": (DEPTH, CONV_TAPS, CONV_W), "conv_dw_b": (DEPTH, CONV_W),
    "conv_ln_g": (DEPTH, CONV_W), "conv_ln_b": (DEPTH, CONV_W), "lru_conv_w": (DEPTH, LRU_TAPS, LRU_W),
    "lru_conv_b": (DEPTH, LRU_W), "lru_wa": (DEPTH, LRU_HEADS, HEAD_DIM, HEAD_DIM), "lru_ba": (DEPTH, LRU_W),
    "lru_wx": (DEPTH, LRU_HEADS, HEAD_DIM, HEAD_DIM), "lru_bx": (DEPTH, LRU_W), "lru_lambda": (DEPTH, LRU_W),
    "norm2_g": (DEPTH, D_MODEL), "final_g": (D_MODEL,),
}
SMALL_COLS = 128
SMALL_ROWS = 1000
FILTER_ROWS = 24
W_IN_SHARD = IN_COLS // N_DEV
W_OUT_SHARD = D_MODEL // N_DEV
FF_SHARD = D_FF // N_DEV


def _pack_rows(flat_parts, cols, rows):
    flat = jnp.concatenate(flat_parts)
    return jnp.pad(flat, (0, rows * cols - flat.shape[0])).reshape(rows, cols)


WEIGHT_GATHER = {
    "w_in": ((N_DEV, D_MODEL, W_IN_SHARD), _slot),
    "w_out": ((D_MODEL, D_MODEL), _row_block(W_OUT_SHARD)),
    "w_up": ((D_MODEL, D_FF), _col_block(FF_SHARD)),
    "w_down": ((D_FF, D_MODEL), _row_block(FF_SHARD)),
}
GATHER_HOSTS = {
    "inproj": (("w_out", 0),), "attn_fwd": (("w_up", 0),), "lru_fwd": (("w_down", 0),),
    "outproj": (("w_in", 1),), "up": (("w_up", 1),), "down": (("w_down", 1), ("w_out", 1)),
}


def _weight_gather(local, items, with_filters=False):
    blocks = [local[n][l].astype(BF16) for n, l in items]
    shapes = [jax.ShapeDtypeStruct(WEIGHT_GATHER[n][0], BF16) for n, _ in items]
    places = [WEIGHT_GATHER[n][1] for n, _ in items]
    if with_filters:
        blocks.append(_pack_rows([local[n].reshape(-1) for n in SMALL_SHARDED], SMALL_COLS, FILTER_ROWS))
        shapes.append(jax.ShapeDtypeStruct((N_DEV, FILTER_ROWS, SMALL_COLS), F32))
        places.append(_slot)
    return _Gather(blocks, shapes, places)


def _keep_gathered(full, items, landed):
    for (n, l), arr in zip(items, landed):
        full[n][l] = arr.transpose(1, 0, 2).reshape(D_MODEL, IN_COLS) if n == "w_in" else arr


def _unpack_filters(slots):
    flat, off, out = slots.reshape(N_DEV, -1), 0, {}
    for n, shp in SMALL_SHARDED.items():
        shard = shp[:-1] + (shp[-1] // N_DEV,)
        size = int(np.prod(shard))
        out[n] = jnp.moveaxis(flat[:, off:off + size].reshape((N_DEV,) + shard), 0, -2).reshape(shp)
        off += size
    return out


def _grad_exchange(name, dw):
    if name == "w_in":
        return _GradExchange([dw.reshape(D_MODEL, N_DEV, W_IN_SHARD).transpose(1, 0, 2)], [_slot], [(D_MODEL, W_IN_SHARD)])
    if name == "w_out":
        return _GradExchange([dw], [_row_block(W_OUT_SHARD)], [(W_OUT_SHARD, D_MODEL)])
    return _GradExchange([dw], [_col_block(FF_SHARD)], [(D_MODEL, FF_SHARD)])


def _sum_adamw(parts, w, m, v, rows_per_step, transposed, name):
    _, R, C = parts[0].shape
    tr = rows_per_step
    steps = R // tr
    c1 = 1.0 - ADAM_B1 ** ADAM_STEP
    c2 = 1.0 - ADAM_B2 ** ADAM_STEP

    def slot_sum(p_ref):
        acc = p_ref[0].astype(F32)
        for q in range(1, N_DEV):
            acc = acc + p_ref[q].astype(F32)
        return acc

    def body(p0_ref, p1_ref, w_ref, m_ref, v_ref, g_ref, d_ref, nm_ref, nv_ref):
        gv = jnp.where(pl.program_id(0) == 0, slot_sum(p0_ref), slot_sum(p1_ref))
        if transposed:
            gv = gv.T
        nm = ADAM_B1 * m_ref[...] + (1.0 - ADAM_B1) * gv
        nv = ADAM_B2 * v_ref[...] + (1.0 - ADAM_B2) * (gv * gv)
        g_ref[...] = gv
        nm_ref[...] = nm
        nv_ref[...] = nv
        d_ref[...] = (-ADAM_LR) * ((nm / c1) / (jnp.sqrt(nv / c2) + ADAM_EPS) + ADAM_WD * w_ref[...])

    if transposed:
        spec = pl.BlockSpec((None, C, tr), lambda l, i: (l, 0, i))
    else:
        spec = pl.BlockSpec((None, tr, C), lambda l, i: (l, i, 0))
    shape = jax.ShapeDtypeStruct(w.shape, F32)
    part0 = pl.BlockSpec((N_DEV, tr, C), lambda l, i: (0, jnp.where(l == 0, i, steps - 1), 0))
    part1 = pl.BlockSpec((N_DEV, tr, C), lambda l, i: (0, jnp.where(l == 1, i, 0), 0))
    return pl.pallas_call(
        body, name=name, grid=(DEPTH, steps),
        in_specs=[part0, part1, spec, spec, spec],
        out_specs=[spec] * 4, out_shape=[shape] * 4,
        compiler_params=_params("arbitrary", "arbitrary"),
    )(parts[0], parts[1], w, m, v)


def _block_diag(w):
    out = jnp.zeros((LRU_W, LRU_W), w.dtype)
    for i in range(LRU_HEADS):
        out = lax.dynamic_update_slice(out, w[i], (HEAD_DIM * i, HEAD_DIM * i))
    return out


def _diag_blocks(m):
    return jnp.stack([m[HEAD_DIM * i:HEAD_DIM * (i + 1), HEAD_DIM * i:HEAD_DIM * (i + 1)] for i in range(LRU_HEADS)])


def _pack_small(values, loss_row):
    parts = [values[n].reshape(-1) for n in SMALL_FULL]
    parts.append(loss_row.reshape(-1))
    return _pack_rows(parts, SMALL_COLS, SMALL_ROWS)


def _unpack_small(packed):
    flat = packed.reshape(-1)
    out, off = {}, 0
    for n, shp in SMALL_FULL.items():
        size = int(np.prod(shp))
        out[n] = flat[off:off + size].reshape(shp)
        off += size
    return out, flat[off]


def kernel(x, norm1_g, w_in, conv_dw_w, conv_dw_b, conv_ln_g, conv_ln_b, lru_conv_w, lru_conv_b, lru_wa, lru_ba, lru_wx, lru_bx, lru_lambda, w_out, norm2_g, w_up, w_down, final_g, loss_target, m_norm1_g, m_w_in, m_conv_dw_w, m_conv_dw_b, m_conv_ln_g, m_conv_ln_b, m_lru_conv_w, m_lru_conv_b, m_lru_wa, m_lru_ba, m_lru_wx, m_lru_bx, m_lru_lambda, m_w_out, m_norm2_g, m_w_up, m_w_down, m_final_g, v_norm1_g, v_w_in, v_conv_dw_w, v_conv_dw_b, v_conv_ln_g, v_conv_ln_b, v_lru_conv_w, v_lru_conv_b, v_lru_wa, v_lru_ba, v_lru_wx, v_lru_bx, v_lru_lambda, v_w_out, v_norm2_g, v_w_up, v_w_down, v_final_g):
    local = dict(zip(WEIGHT_ORDER, (norm1_g, w_in, conv_dw_w, conv_dw_b, conv_ln_g, conv_ln_b, lru_conv_w, lru_conv_b, lru_wa,
                                    lru_ba, lru_wx, lru_bx, lru_lambda, w_out, norm2_g, w_up, w_down, final_g)))
    mom1 = dict(zip(WEIGHT_ORDER, (m_norm1_g, m_w_in, m_conv_dw_w, m_conv_dw_b, m_conv_ln_g, m_conv_ln_b, m_lru_conv_w,
                                   m_lru_conv_b, m_lru_wa, m_lru_ba, m_lru_wx, m_lru_bx, m_lru_lambda, m_w_out, m_norm2_g,
                                   m_w_up, m_w_down, m_final_g)))
    mom2 = dict(zip(WEIGHT_ORDER, (v_norm1_g, v_w_in, v_conv_dw_w, v_conv_dw_b, v_conv_ln_g, v_conv_ln_b, v_lru_conv_w,
                                   v_lru_conv_b, v_lru_wa, v_lru_ba, v_lru_wx, v_lru_bx, v_lru_lambda, v_w_out, v_norm2_g,
                                   v_w_up, v_w_down, v_final_g)))
    B, S, _ = x.shape
    T = B * S
    my_slot = 4 * lax.axis_index("x") + 2 * lax.axis_index("y") + lax.axis_index("c")
    row = lambda a: a.reshape(1, -1)

    full = {n: [None] * DEPTH for n in BIG}
    first_items = (("w_in", 0),)
    landed = _run_exchange(_weight_gather(local, first_items, with_filters=True), "gather_first_weights")
    _keep_gathered(full, first_items, landed)
    full.update(_unpack_filters(landed[-1]))

    def hosted(call, layer, fn, *args):
        items = GATHER_HOSTS[call] if layer == 0 else ()
        outs, landed = fn(*args, ex=_weight_gather(local, items) if items else None)
        _keep_gathered(full, items, landed or ())
        return outs

    saved = []
    cur = x.reshape(T, D_MODEL)
    for l in range(DEPTH):
        h, qkv, ci, li = hosted("inproj", l, _inproj, cur, row(norm1_g[l]), full["w_in"][l])
        qkv = qkv.reshape(B, S, QKV_W)
        o, lse = hosted("attn_fwd", l, _attn_fwd, qkv)
        o, lse = o.reshape(T, ATTN_W), lse.reshape(T, ATTN_W)
        ci = ci.reshape(B, S, CONV_IN_W)
        li = li.reshape(B, S, LRU_IN_W)
        conv_p = (full["conv_dw_w"][l], row(conv_dw_b[l]), row(conv_ln_g[l]), row(conv_ln_b[l]))
        lru_p = (full["lru_conv_w"][l], row(lru_conv_b[l]), _block_diag(lru_wa[l]).astype(BF16), row(lru_ba[l]),
                 _block_diag(lru_wx[l]).astype(BF16), row(lru_bx[l]), row(lru_lambda[l]))
        yc, cpre = _conv_fwd(ci, *conv_p)
        yl, hs = hosted("lru_fwd", l, _lru_fwd, li, *lru_p)
        x1, mix = hosted("outproj", l, _outproj, cur, o, lse, yc.reshape(T, CONV_W), yl.reshape(T, LRU_W), full["w_out"][l])
        h2, r = hosted("up", l, _up, x1, row(norm2_g[l]), full["w_up"][l])
        (x2,) = hosted("down", l, _down, x1, r, full["w_down"][l])
        saved.append(dict(x=cur, h=h, qkv=qkv, o=o, lse=lse, ci=ci, li=li, cpre=cpre, hs=hs, x1=x1, mix=mix, h2=h2, r=r,
                          conv_p=conv_p, lru_p=lru_p))
        cur = x2

    dx, loss_part, dgf = _loss_head(cur, loss_target.reshape(T, D_MODEL), row(final_g))

    received = {n: [None] * DEPTH for n in BIG}
    small_grads = {n: [None] * DEPTH for n in SMALL_FULL if n != "final_g"}
    w_in_exchange = None
    for l in reversed(range(DEPTH)):
        sv = saved[l]
        (dpre,), landed = _down_bwd_act(dx, sv["r"], full["w_down"][l], ex=w_in_exchange)
        if landed is not None:
            received["w_in"][l + 1] = landed[0]
        dw_down = _down_bwd_w(sv["r"], dx)
        (dx1, dg2), (received["w_down"][l],) = _up_bwd_act(dpre, full["w_up"][l], sv["x1"], row(norm2_g[l]), dx,
                                                           ex=_grad_exchange("w_down", dw_down))
        dw_up = _up_bwd_w(sv["h2"], dpre)
        do, dd, dyc, dyl, dw_out = _outproj_bwd(dx1, sv["mix"], full["w_out"][l], sv["o"], sv["lse"])
        seq = lambda a: a.reshape(B, S, ATTN_W)
        dq, dk, dv = _attn_bwd(sv["qkv"], seq(do), seq(sv["lse"]), seq(dd))
        (dci, dcw, dcb, dlg, dlb), (received["w_out"][l],) = _conv_bwd(
            sv["ci"], sv["cpre"], dyc.reshape(B, S, CONV_W), sv["conv_p"][0], sv["conv_p"][2], sv["conv_p"][3],
            ex=_grad_exchange("w_out", dw_out))
        (dli, dlcw, dlcb, dwa, dba, dwx, dbx, dlam), (received["w_up"][l],) = _lru_bwd(
            sv["li"], sv["hs"], dyl.reshape(B, S, LRU_W), *sv["lru_p"], ex=_grad_exchange("w_up", dw_up))
        dz = tuple(t.reshape(T, -1) for t in (dq, dk, dv, dci, dli))
        dx, dg1, dw_in = _inproj_bwd(dz, sv["h"], full["w_in"][l], sv["x"], row(norm1_g[l]), dx1)
        w_in_exchange = _grad_exchange("w_in", dw_in)
        for n, g in (("norm1_g", dg1), ("conv_dw_w", dcw), ("conv_dw_b", dcb), ("conv_ln_g", dlg), ("conv_ln_b", dlb),
                     ("lru_conv_w", dlcw), ("lru_conv_b", dlcb), ("lru_wa", _diag_blocks(dwa)), ("lru_ba", dba),
                     ("lru_wx", _diag_blocks(dwx)), ("lru_bx", dbx), ("lru_lambda", dlam), ("norm2_g", dg2)):
            small_grads[n][l] = g.reshape(SMALL_FULL[n][1:])
    grad_x = dx.reshape(B, S, D_MODEL)

    small_local = {n: jnp.stack(g) for n, g in small_grads.items()}
    small_local["final_g"] = dgf.reshape(D_MODEL)
    packet = _pack_small(small_local, loss_part[0:1, :])
    small_gather = _Gather([packet], [jax.ShapeDtypeStruct((N_DEV, SMALL_ROWS, SMALL_COLS), F32)], [_slot])
    received["w_in"][0], gathered = _run_exchange(_Together(w_in_exchange, small_gather), "exchange_last_grads")

    grads, delta, new_m, new_v = {}, {}, {}, {}
    for n, rows_per_step in (("w_in", 256), ("w_out", W_OUT_SHARD), ("w_up", 256), ("w_down", 256)):
        grads[n], delta[n], new_m[n], new_v[n] = _sum_adamw(received[n], local[n], mom1[n], mom2[n], rows_per_step,
                                                            n == "w_down", "sum_adamw_" + n)

    small_sum = _sum_slots(gathered, SMALL_ROWS, "sum_small_grads")
    small_g, loss = _unpack_small(small_sum)

    for n, fullshape in SMALL_SHARDED.items():
        width = fullshape[-1] // N_DEV
        g = lax.dynamic_slice_in_dim(small_g[n], my_slot * width, width, axis=2)
        two_d = (fullshape[0] * fullshape[1], width)
        d, nm, nv = _adamw(local[n].reshape(two_d), g.reshape(two_d), mom1[n].reshape(two_d), mom2[n].reshape(two_d),
                           two_d[0], "adamw_" + n)
        grads[n], delta[n], new_m[n], new_v[n] = (t.reshape(g.shape) for t in (g.reshape(two_d), d, nm, nv))
    replicated = [n for n in SMALL_FULL if n not in SMALL_SHARDED]
    zero_row = jnp.zeros((1, SMALL_COLS), F32)
    packed_state = []
    for src in (local, mom1, mom2):
        vals = {n: (src[n] if n in replicated else jnp.zeros(SMALL_FULL[n], F32)) for n in SMALL_FULL}
        packed_state.append(_pack_small(vals, zero_row))
    d, nm, nv = _adamw(packed_state[0], small_sum, packed_state[1], packed_state[2], SMALL_ROWS, "adamw_small")
    d, nm, nv = _unpack_small(d)[0], _unpack_small(nm)[0], _unpack_small(nv)[0]
    for n in replicated:
        grads[n], delta[n], new_m[n], new_v[n] = small_g[n], d[n], nm[n], nv[n]

    return (loss, grad_x, *[grads[n] for n in WEIGHT_ORDER], *[delta[n] for n in WEIGHT_ORDER],
            *[new_m[n] for n in WEIGHT_ORDER], *[new_v[n] for n in WEIGHT_ORDER])
```

```python
import functools
import math

import numpy as np
import jax
import jax.numpy as jnp
from jax import lax
from jax.experimental import pallas as pl
from jax.experimental.pallas import tpu as pltpu

F32 = jnp.float32
BF16 = jnp.bfloat16

D_MODEL = 1024
SEQ_LEN = 2048
HEAD_DIM = 64
ATTN_W = 384
CONV_W = 256
CONV_TAPS = 31
LRU_W = 384
LRU_TAPS = 4
LRU_HEADS = 6
LRU_C = 8.0
QKV_W = 3 * ATTN_W
CONV_IN_W = 2 * CONV_W
LRU_IN_W = 2 * LRU_W
IN_COLS = QKV_W + CONV_IN_W + LRU_IN_W
D_FF = 4096
DEPTH = 2
N_DEV = 8
RMS_EPS = 1e-6
LN_EPS = 1e-5
ATTN_BLOCK = 128
ATTN_DILATIONS = (1, 4, 16)
N_UNITS = 16
UNIT_UNROLL = 4
NEG_BIG = -1e30

ADAM_LR = 0.001
ADAM_B1 = 0.9
ADAM_B2 = 0.999
ADAM_EPS = 1e-08
ADAM_WD = 0.01
ADAM_STEP = 10

VMEM_LIMIT = 56 * 1024 * 1024
ROW_TILE = 512
CONV_FWD_CHUNK = 128
CONV_BWD_CHUNK = 128
LRU_FWD_CHUNK = 128
LRU_BWD_CHUNK = 128


def _params(*sem):
    return pltpu.CompilerParams(dimension_semantics=sem if sem else None, vmem_limit_bytes=VMEM_LIMIT)


def _resident(shape):
    return pl.BlockSpec(shape, lambda *_: (0,) * len(shape), pipeline_mode=pl.Buffered(1))


def _dot(a, b):
    return jnp.dot(a, b, preferred_element_type=F32)


def _dot_nt(a, b):
    return lax.dot_general(a, b, (((1,), (1,)), ((), ())), preferred_element_type=F32)


def _dot_tn(a, b):
    return lax.dot_general(a, b, (((0,), (0,)), ((), ())), preferred_element_type=F32)


def _rms_fwd(x, g):
    rstd = lax.rsqrt(jnp.mean(x * x, axis=-1, keepdims=True) + RMS_EPS)
    xhat = x * rstd
    return xhat * g, xhat, rstd


def _rms_bwd(dh, xhat, rstd, g):
    dxh = dh * g
    dx = rstd * (dxh - xhat * jnp.mean(dxh * xhat, axis=-1, keepdims=True))
    dg = jnp.sum(dh * xhat, axis=0, keepdims=True)
    return dx, dg


def _sigmoid(x):
    return 0.5 * jnp.tanh(0.5 * x) + 0.5


def _one_minus_exp(x, exp_x):
    small = -x * (1.0 + x * (0.5 + x * (1.0 / 6.0)))
    return jnp.where(x > -0.01, small, 1.0 - exp_x)


def _log1p(z):
    w = 1.0 + z
    return jnp.where(w == 1.0, z, z * jnp.log(w) / jnp.where(w == 1.0, 1.0, w - 1.0))


def _softplus(x):
    return jnp.maximum(x, 0.0) + _log1p(jnp.exp(-jnp.abs(x)))


GELU_K = math.sqrt(2.0 / math.pi)


def _gelu(x):
    t = jnp.tanh(GELU_K * (x + 0.044715 * x * x * x))
    return 0.5 * x * (1.0 + t), t


def _gelu_grad(x, t):
    return 0.5 * (1.0 + t) + 0.5 * x * (1.0 - t * t) * GELU_K * (1.0 + 3.0 * 0.044715 * x * x)


def _inproj(x2d, g, w, ex=None):
    T = x2d.shape[0]
    tm = ROW_TILE

    def body(x_ref, g_ref, w_ref, h_ref, qkv_ref, ci_ref, li_ref):
        h, _, _ = _rms_fwd(x_ref[...], g_ref[...])
        hb = h.astype(BF16)
        h_ref[...] = hb
        qkv_ref[...] = _dot(hb, w_ref[:, 0:QKV_W])
        ci_ref[...] = _dot(hb, w_ref[:, QKV_W:QKV_W + CONV_IN_W])
        li_ref[...] = _dot(hb, w_ref[:, QKV_W + CONV_IN_W:IN_COLS])

    return _pallas_hosting(
        body, ex, name="inproj", grid=(T // tm,),
        in_specs=[pl.BlockSpec((tm, D_MODEL), lambda i: (i, 0)),
                  pl.BlockSpec((1, D_MODEL), lambda i: (0, 0)),
                  _resident((D_MODEL, IN_COLS))],
        out_specs=[pl.BlockSpec((tm, D_MODEL), lambda i: (i, 0)),
                   pl.BlockSpec((tm, QKV_W), lambda i: (i, 0)),
                   pl.BlockSpec((tm, CONV_IN_W), lambda i: (i, 0)),
                   pl.BlockSpec((tm, LRU_IN_W), lambda i: (i, 0))],
        out_shape=[jax.ShapeDtypeStruct((T, D_MODEL), BF16), jax.ShapeDtypeStruct((T, QKV_W), F32),
                   jax.ShapeDtypeStruct((T, CONV_IN_W), F32), jax.ShapeDtypeStruct((T, LRU_IN_W), F32)],
        scratch_shapes=[], semantics=("parallel",), operands=(x2d, g, w))


def _attn_alpha(lse):
    l0, l1, l2 = lse[:, 0:128], lse[:, 128:256], lse[:, 256:384]
    m = jnp.maximum(jnp.maximum(l0, l1), l2)
    e0, e1, e2 = jnp.exp(l0 - m), jnp.exp(l1 - m), jnp.exp(l2 - m)
    inv = 1.0 / (e0 + e1 + e2)
    return e0 * inv, e1 * inv, e2 * inv


def _outproj(x2d, o, lse, yc, yl, w, ex=None):
    T = x2d.shape[0]
    tm = ROW_TILE

    def body(x_ref, o_ref, lse_ref, yc_ref, yl_ref, w_ref, x1_ref, mix_ref):
        al = _attn_alpha(lse_ref[...])
        for p in range(3):
            mix_ref[:, p * 128:(p + 1) * 128] = (o_ref[:, p * 128:(p + 1) * 128] * al[p]).astype(BF16)
        mix_ref[:, ATTN_W:ATTN_W + CONV_W] = yc_ref[...]
        mix_ref[:, ATTN_W + CONV_W:D_MODEL] = yl_ref[...]
        x1_ref[...] = x_ref[...] + _dot(mix_ref[...], w_ref[...])

    return _pallas_hosting(
        body, ex, name="outproj", grid=(T // tm,),
        in_specs=[pl.BlockSpec((tm, D_MODEL), lambda i: (i, 0)),
                  pl.BlockSpec((tm, ATTN_W), lambda i: (i, 0)),
                  pl.BlockSpec((tm, ATTN_W), lambda i: (i, 0)),
                  pl.BlockSpec((tm, CONV_W), lambda i: (i, 0)),
                  pl.BlockSpec((tm, LRU_W), lambda i: (i, 0)),
                  _resident((D_MODEL, D_MODEL))],
        out_specs=[pl.BlockSpec((tm, D_MODEL), lambda i: (i, 0)),
                   pl.BlockSpec((tm, D_MODEL), lambda i: (i, 0))],
        out_shape=[jax.ShapeDtypeStruct((T, D_MODEL), F32), jax.ShapeDtypeStruct((T, D_MODEL), BF16)],
        scratch_shapes=[], semantics=("parallel",), operands=(x2d, o, lse, yc, yl, w))


FF_CHUNK = 1024


def _up(x1, g, w, ex=None):
    T = x1.shape[0]
    tm = ROW_TILE

    def body(x_ref, g_ref, w_ref, h_ref, r_ref):
        h, _, _ = _rms_fwd(x_ref[...], g_ref[...])
        hb = h.astype(BF16)
        h_ref[...] = hb
        for c in range(0, D_FF, FF_CHUNK):
            r_ref[:, c:c + FF_CHUNK] = jnp.maximum(_dot(hb, w_ref[:, c:c + FF_CHUNK]), 0.0).astype(BF16)

    return _pallas_hosting(
        body, ex, name="up", grid=(T // tm,),
        in_specs=[pl.BlockSpec((tm, D_MODEL), lambda i: (i, 0)),
                  pl.BlockSpec((1, D_MODEL), lambda i: (0, 0)),
                  _resident((D_MODEL, D_FF))],
        out_specs=[pl.BlockSpec((tm, D_MODEL), lambda i: (i, 0)),
                   pl.BlockSpec((tm, D_FF), lambda i: (i, 0))],
        out_shape=[jax.ShapeDtypeStruct((T, D_MODEL), BF16), jax.ShapeDtypeStruct((T, D_FF), BF16)],
        scratch_shapes=[], semantics=("parallel",), operands=(x1, g, w))


def _square_bf16(r):
    rf = r.astype(F32)
    return (rf * rf).astype(BF16)


def _down(x1, r, w, ex=None):
    T = x1.shape[0]
    tm = ROW_TILE

    def body(x_ref, r_ref, w_ref, o_ref):
        acc = x_ref[...]
        for c in range(0, D_FF, FF_CHUNK):
            acc = acc + _dot(_square_bf16(r_ref[:, c:c + FF_CHUNK]), w_ref[c:c + FF_CHUNK, :])
        o_ref[...] = acc

    return _pallas_hosting(
        body, ex, name="down", grid=(T // tm,),
        in_specs=[pl.BlockSpec((tm, D_MODEL), lambda i: (i, 0)),
                  pl.BlockSpec((tm, D_FF), lambda i: (i, 0)),
                  _resident((D_FF, D_MODEL))],
        out_specs=[pl.BlockSpec((tm, D_MODEL), lambda i: (i, 0))],
        out_shape=[jax.ShapeDtypeStruct((T, D_MODEL), F32)],
        scratch_shapes=[], semantics=("parallel",), operands=(x1, r, w))


def _loss_head(x2, target, g):
    T = x2.shape[0]
    tm = ROW_TILE

    def body(x_ref, t_ref, g_ref, dx_ref, loss_ref, dg_ref):
        @pl.when(pl.program_id(0) == 0)
        def _():
            loss_ref[...] = jnp.zeros_like(loss_ref)
            dg_ref[...] = jnp.zeros_like(dg_ref)

        gv = g_ref[...]
        y, xhat, rstd = _rms_fwd(x_ref[...], gv)
        err = y - t_ref[...]
        loss_ref[...] += 0.5 * jnp.sum(jnp.mean(err * err, axis=-1, keepdims=True))
        dy = err * (1.0 / D_MODEL)
        dx, dg = _rms_bwd(dy, xhat, rstd, gv)
        dx_ref[...] = dx
        dg_ref[...] += dg

    return pl.pallas_call(
        body, name="loss_head", grid=(T // tm,),
        in_specs=[pl.BlockSpec((tm, D_MODEL), lambda i: (i, 0)),
                  pl.BlockSpec((tm, D_MODEL), lambda i: (i, 0)),
                  pl.BlockSpec((1, D_MODEL), lambda i: (0, 0))],
        out_specs=[pl.BlockSpec((tm, D_MODEL), lambda i: (i, 0)),
                   pl.BlockSpec((8, 128), lambda i: (0, 0)),
                   pl.BlockSpec((1, D_MODEL), lambda i: (0, 0))],
        out_shape=[jax.ShapeDtypeStruct((T, D_MODEL), F32), jax.ShapeDtypeStruct((8, 128), F32),
                   jax.ShapeDtypeStruct((1, D_MODEL), F32)],
        compiler_params=_params("arbitrary"),
    )(x2, target, g)


def _down_bwd_act(dx2, r, w, ex=None):
    T = dx2.shape[0]
    tm = ROW_TILE

    def body(dx_ref, r_ref, w_ref, o_ref):
        dxb = dx_ref[...].astype(BF16)
        for c in range(0, D_FF, FF_CHUNK):
            dff = _dot_nt(dxb, w_ref[c:c + FF_CHUNK, :])
            o_ref[:, c:c + FF_CHUNK] = (dff * (2.0 * r_ref[:, c:c + FF_CHUNK].astype(F32))).astype(BF16)

    return _pallas_hosting(
        body, ex, name="down_bwd_act", grid=(T // tm,),
        in_specs=[pl.BlockSpec((tm, D_MODEL), lambda i: (i, 0)),
                  pl.BlockSpec((tm, D_FF), lambda i: (i, 0)),
                  _resident((D_FF, D_MODEL))],
        out_specs=[pl.BlockSpec((tm, D_FF), lambda i: (i, 0))],
        out_shape=[jax.ShapeDtypeStruct((T, D_FF), BF16)],
        scratch_shapes=[], semantics=("parallel",), operands=(dx2, r, w))


def _down_bwd_w(r, dx2):
    T = dx2.shape[0]
    tk = ROW_TILE
    nk = T // tk

    def body(r_ref, dx_ref, o_ref, acc_ref):
        k = pl.program_id(0)
        dxb = dx_ref[...].astype(BF16)

        @pl.when(k == 0)
        def _():
            acc_ref[...] = jnp.zeros_like(acc_ref)

        for c in range(0, D_FF, FF_CHUNK):
            acc_ref[:, c:c + FF_CHUNK] += _dot_tn(dxb, _square_bf16(r_ref[:, c:c + FF_CHUNK]))

        @pl.when(k == nk - 1)
        def _():
            o_ref[...] = acc_ref[...].astype(BF16)

    return pl.pallas_call(
        body, name="down_bwd_w", grid=(nk,),
        in_specs=[pl.BlockSpec((tk, D_FF), lambda k: (k, 0)),
                  pl.BlockSpec((tk, D_MODEL), lambda k: (k, 0))],
        out_specs=_resident((D_MODEL, D_FF)),
        out_shape=jax.ShapeDtypeStruct((D_MODEL, D_FF), BF16),
        scratch_shapes=[pltpu.VMEM((D_MODEL, D_FF), F32)],
        compiler_params=_params("arbitrary"),
    )(r, dx2)


def _up_bwd_act(dpre, w, x1, g, dx2, ex=None):
    T = dx2.shape[0]
    tm = ROW_TILE

    def body(dp_ref, w_ref, x_ref, g_ref, dx2_ref, dx1_ref, dg_ref):
        dh = _dot_nt(dp_ref[:, 0:FF_CHUNK], w_ref[:, 0:FF_CHUNK])
        for c in range(FF_CHUNK, D_FF, FF_CHUNK):
            dh = dh + _dot_nt(dp_ref[:, c:c + FF_CHUNK], w_ref[:, c:c + FF_CHUNK])
        gv = g_ref[...]
        _, xhat, rstd = _rms_fwd(x_ref[...], gv)
        dx, dg = _rms_bwd(dh, xhat, rstd, gv)
        dx1_ref[...] = dx2_ref[...] + dx

        @pl.when(pl.program_id(0) == 0)
        def _():
            dg_ref[...] = dg

        @pl.when(pl.program_id(0) != 0)
        def _():
            dg_ref[...] += dg

    return _pallas_hosting(
        body, ex, name="up_bwd_act", grid=(T // tm,),
        in_specs=[pl.BlockSpec((tm, D_FF), lambda i: (i, 0)),
                  _resident((D_MODEL, D_FF)),
                  pl.BlockSpec((tm, D_MODEL), lambda i: (i, 0)),
                  pl.BlockSpec((1, D_MODEL), lambda i: (0, 0)),
                  pl.BlockSpec((tm, D_MODEL), lambda i: (i, 0))],
        out_specs=[pl.BlockSpec((tm, D_MODEL), lambda i: (i, 0)),
                   pl.BlockSpec((1, D_MODEL), lambda i: (0, 0))],
        out_shape=[jax.ShapeDtypeStruct((T, D_MODEL), F32), jax.ShapeDtypeStruct((1, D_MODEL), F32)],
        scratch_shapes=[], semantics=("arbitrary",), operands=(dpre, w, x1, g, dx2))


def _up_bwd_w(h2, dpre):
    T = h2.shape[0]
    tk = ROW_TILE
    nk = T // tk

    def body(h_ref, dp_ref, o_ref, acc_ref):
        k = pl.program_id(0)
        hb = h_ref[...]

        @pl.when(k == 0)
        def _():
            acc_ref[...] = jnp.zeros_like(acc_ref)

        for c in range(0, D_FF, FF_CHUNK):
            acc_ref[:, c:c + FF_CHUNK] += _dot_tn(hb, dp_ref[:, c:c + FF_CHUNK])

        @pl.when(k == nk - 1)
        def _():
            o_ref[...] = acc_ref[...].astype(BF16)

    return pl.pallas_call(
        body, name="up_bwd_w", grid=(nk,),
        in_specs=[pl.BlockSpec((tk, D_MODEL), lambda k: (k, 0)),
                  pl.BlockSpec((tk, D_FF), lambda k: (k, 0))],
        out_specs=_resident((D_MODEL, D_FF)),
        out_shape=jax.ShapeDtypeStruct((D_MODEL, D_FF), BF16),
        scratch_shapes=[pltpu.VMEM((D_MODEL, D_FF), F32)],
        compiler_params=_params("arbitrary"),
    )(h2, dpre)


def _outproj_bwd(dx1, mix, w, o, lse):
    T = dx1.shape[0]
    tm = ROW_TILE
    nk = T // tm

    def body(dx_ref, mix_ref, w_ref, o_ref, lse_ref, do_ref, dd_ref, dc_ref, dl_ref, dw_ref, acc_ref):
        i = pl.program_id(0)
        dxb = dx_ref[...].astype(BF16)
        dmix = _dot_nt(dxb, w_ref[...])
        al = _attn_alpha(lse_ref[...])
        first = lax.broadcasted_iota(jnp.int32, (tm, 128), 1) < HEAD_DIM
        tot = jnp.zeros((tm, 128), F32)
        for p in range(3):
            sl = slice(p * 128, (p + 1) * 128)
            dy = dmix[:, sl]
            do_ref[:, sl] = dy * al[p]
            prod = dy * o_ref[:, sl]
            s0 = jnp.sum(jnp.where(first, prod, 0.0), axis=-1, keepdims=True)
            s1 = jnp.sum(jnp.where(first, 0.0, prod), axis=-1, keepdims=True)
            tot = tot + al[p] * jnp.where(first, s0, s1)
        for p in range(3):
            dd_ref[:, p * 128:(p + 1) * 128] = -al[p] * tot
        dc_ref[...] = dmix[:, ATTN_W:ATTN_W + CONV_W]
        dl_ref[...] = dmix[:, ATTN_W + CONV_W:D_MODEL]
        part = _dot_tn(mix_ref[...], dxb)

        @pl.when(i == 0)
        def _():
            acc_ref[...] = part

        @pl.when(i != 0)
        def _():
            acc_ref[...] += part

        @pl.when(i == nk - 1)
        def _():
            dw_ref[...] = acc_ref[...].astype(BF16)

    return pl.pallas_call(
        body, name="outproj_bwd", grid=(nk,),
        in_specs=[pl.BlockSpec((tm, D_MODEL), lambda i: (i, 0)),
                  pl.BlockSpec((tm, D_MODEL), lambda i: (i, 0)),
                  _resident((D_MODEL, D_MODEL)),
                  pl.BlockSpec((tm, ATTN_W), lambda i: (i, 0)),
                  pl.BlockSpec((tm, ATTN_W), lambda i: (i, 0))],
        out_specs=[pl.BlockSpec((tm, ATTN_W), lambda i: (i, 0)),
                   pl.BlockSpec((tm, ATTN_W), lambda i: (i, 0)),
                   pl.BlockSpec((tm, CONV_W), lambda i: (i, 0)),
                   pl.BlockSpec((tm, LRU_W), lambda i: (i, 0)),
                   _resident((D_MODEL, D_MODEL))],
        out_shape=[jax.ShapeDtypeStruct((T, ATTN_W), F32), jax.ShapeDtypeStruct((T, ATTN_W), F32),
                   jax.ShapeDtypeStruct((T, CONV_W), F32), jax.ShapeDtypeStruct((T, LRU_W), F32),
                   jax.ShapeDtypeStruct((D_MODEL, D_MODEL), BF16)],
        scratch_shapes=[pltpu.VMEM((D_MODEL, D_MODEL), F32)],
        compiler_params=_params("arbitrary"),
    )(dx1, mix, w, o, lse)


DZ_COLS = ((0, ATTN_W), (ATTN_W, 2 * ATTN_W), (2 * ATTN_W, QKV_W), (QKV_W, QKV_W + CONV_IN_W), (QKV_W + CONV_IN_W, IN_COLS))


def _inproj_bwd_w(dz_parts, h):
    T = h.shape[0]
    tm = ROW_TILE
    nk = T // tm
    n_parts = len(DZ_COLS)

    def body(*refs):
        dz_refs = refs[:n_parts]
        h_ref, dw_ref, acc_ref = refs[n_parts:]
        i = pl.program_id(0)
        hb = h_ref[...]

        @pl.when(i == 0)
        def _():
            acc_ref[...] = jnp.zeros_like(acc_ref)

        for r, (lo, hi) in zip(dz_refs, DZ_COLS):
            acc_ref[:, lo:hi] += _dot_tn(hb, r[...].astype(BF16))

        @pl.when(i == nk - 1)
        def _():
            dw_ref[...] = acc_ref[...].astype(BF16)

    rows = lambda width: pl.BlockSpec((tm, width), lambda i: (i, 0))
    return pl.pallas_call(
        body, name="inproj_bwd_w", grid=(nk,),
        in_specs=[rows(hi - lo) for lo, hi in DZ_COLS] + [rows(D_MODEL)],
        out_specs=_resident((D_MODEL, IN_COLS)),
        out_shape=jax.ShapeDtypeStruct((D_MODEL, IN_COLS), BF16),
        scratch_shapes=[pltpu.VMEM((D_MODEL, IN_COLS), F32)],
        compiler_params=_params("arbitrary"),
    )(*dz_parts, h)


def _inproj_bwd_act(dz_parts, w, x2d, g, dx1, ex=None):
    T = x2d.shape[0]
    tm = ROW_TILE
    n_parts = len(DZ_COLS)

    def body(*refs):
        dz_refs = refs[:n_parts]
        w_ref, x_ref, g_ref, dx1_ref, dx_ref, dg_ref = refs[n_parts:]
        dh = _dot_nt(dz_refs[0][...].astype(BF16), w_ref[:, DZ_COLS[0][0]:DZ_COLS[0][1]])
        for r, (lo, hi) in zip(dz_refs[1:], DZ_COLS[1:]):
            dh = dh + _dot_nt(r[...].astype(BF16), w_ref[:, lo:hi])
        gv = g_ref[...]
        _, xhat, rstd = _rms_fwd(x_ref[...], gv)
        dx, dg = _rms_bwd(dh, xhat, rstd, gv)
        dx_ref[...] = dx1_ref[...] + dx

        @pl.when(pl.program_id(0) == 0)
        def _():
            dg_ref[...] = dg

        @pl.when(pl.program_id(0) != 0)
        def _():
            dg_ref[...] += dg

    rows = lambda width: pl.BlockSpec((tm, width), lambda i: (i, 0))
    return _pallas_hosting(
        body, ex, name="inproj_bwd_act", grid=(T // tm,),
        in_specs=[rows(hi - lo) for lo, hi in DZ_COLS] + [
            _resident((D_MODEL, IN_COLS)), rows(D_MODEL), pl.BlockSpec((1, D_MODEL), lambda i: (0, 0)), rows(D_MODEL)],
        out_specs=[rows(D_MODEL), pl.BlockSpec((1, D_MODEL), lambda i: (0, 0))],
        out_shape=[jax.ShapeDtypeStruct((T, D_MODEL), F32), jax.ShapeDtypeStruct((1, D_MODEL), F32)],
        scratch_shapes=[], semantics=("arbitrary",), operands=(*dz_parts, w, x2d, g, dx1))


def _alibi_coef():
    slopes = 2.0 ** (-8.0 * np.arange(1, 7) / 6)
    return jnp.asarray((slopes.reshape(3, 2) * np.asarray(ATTN_DILATIONS)[:, None]).astype(np.float32))


def _unit_rows(u, d):
    nb = N_UNITS // d
    r, n = u // nb, u % nb
    span = ATTN_BLOCK * d

    def rows(block):
        start = block * span + r
        return pl.ds(pl.multiple_of(start, ATTN_BLOCK), ATTN_BLOCK) if d == 1 else pl.ds(start, ATTN_BLOCK, stride=d)

    return rows(n), rows(jnp.maximum(n - 1, 0)), rows(jnp.minimum(n + 1, nb - 1)), n > 0, n + 1 < nb


def _per_pattern(fn):
    for p, d in enumerate(ATTN_DILATIONS):
        pl.when(pl.program_id(1) == p)(functools.partial(fn, p, d))


def _attn_col(offset):
    return pl.BlockSpec((None, SEQ_LEN, 128), lambda b, p: (b, 0, p + offset))


def _attn_masks():
    qi = lax.broadcasted_iota(jnp.int32, (ATTN_BLOCK, 2 * ATTN_BLOCK), 0)
    kj = lax.broadcasted_iota(jnp.int32, (ATTN_BLOCK, 2 * ATTN_BLOCK), 1)
    dist = qi + ATTN_BLOCK - kj
    first = lax.broadcasted_iota(jnp.int32, (ATTN_BLOCK, 128), 1) < HEAD_DIM
    return dist.astype(F32), (dist >= 0) & (dist <= ATTN_BLOCK), kj >= ATTN_BLOCK, first


def _head_lanes(a, first, j):
    return jnp.where(first if j == 0 else jnp.logical_not(first), a, jnp.zeros_like(a))


def _load_kv(ref, prev, own):
    return jnp.concatenate([ref[prev, :], ref[own, :]], axis=0).astype(BF16)


def _attn_fwd(qkv, ex=None):
    B = qkv.shape[0]

    def body(coef_ref, q_ref, k_ref, v_ref, o_ref, lse_ref):
        dist, band, own_half, first = _attn_masks()

        def pattern(p, d):
            def unit(u, carry):
                own, prev, _, has_prev, _ = _unit_rows(u, d)
                ok = band & jnp.logical_or(own_half, has_prev)
                q = q_ref[own, :].astype(BF16)
                kcat, vcat = _load_kv(k_ref, prev, own), _load_kv(v_ref, prev, own)
                outs, lses = [], []
                for j in range(2):
                    s = jnp.where(ok, _dot_nt(_head_lanes(q, first, j), kcat) * 0.125 - coef_ref[p, j] * dist, NEG_BIG)
                    m = jnp.max(s, axis=-1, keepdims=True)
                    e = jnp.exp(s - m)
                    l = jnp.sum(e, axis=-1, keepdims=True)
                    outs.append(_dot(e.astype(BF16), vcat) * (1.0 / l))
                    lses.append(m + jnp.log(l))
                o_ref[own, :] = jnp.where(first, outs[0], outs[1])
                lse_ref[own, :] = jnp.where(first, lses[0], lses[1])
                return carry

            lax.fori_loop(0, N_UNITS, unit, 0, unroll=UNIT_UNROLL)

        _per_pattern(pattern)

    shape = jax.ShapeDtypeStruct((B, SEQ_LEN, ATTN_W), F32)
    return _pallas_hosting(
        body, ex, name="attn_fwd", grid=(B, 3),
        in_specs=[pl.BlockSpec(memory_space=pltpu.SMEM), _attn_col(0), _attn_col(3), _attn_col(6)],
        out_specs=[_attn_col(0), _attn_col(0)],
        out_shape=[shape, shape],
        scratch_shapes=[], semantics=("parallel", "parallel"), operands=(_alibi_coef(), qkv, qkv, qkv))


def _attn_bwd(qkv, do, lse, dd, ex=None):
    B = qkv.shape[0]

    def body(coef_ref, q_ref, k_ref, v_ref, do_ref, lse_ref, dd_ref, dq_ref, dk_ref, dv_ref):
        dist, band, own_half, first = _attn_masks()

        def pattern(p, d):
            def unit(u, carry):
                own, prev, _, has_prev, _ = _unit_rows(u, d)
                ok = band & jnp.logical_or(own_half, has_prev)
                q, do = q_ref[own, :].astype(BF16), do_ref[own, :].astype(BF16)
                kcat, vcat = _load_kv(k_ref, prev, own), _load_kv(v_ref, prev, own)
                lse_a, dd_a = lse_ref[own, :], dd_ref[own, :]
                dqs, dks, dvs = [], [], []
                for j in range(2):
                    col = slice(HEAD_DIM * j, HEAD_DIM * j + 1)
                    s = _dot_nt(_head_lanes(q, first, j), kcat) * 0.125 - coef_ref[p, j] * dist
                    pr = jnp.where(ok, jnp.exp(jnp.where(ok, s, NEG_BIG) - lse_a[:, col]), 0.0)
                    ds = (pr * (_dot_nt(_head_lanes(do, first, j), vcat) + dd_a[:, col])).astype(BF16)
                    dqs.append(_dot(ds, kcat))
                    dks.append(_dot_tn(ds, q))
                    dvs.append(_dot_tn(pr.astype(BF16), do))
                both = lambda pair: jnp.where(jnp.concatenate([first] * (pair[0].shape[0] // ATTN_BLOCK), axis=0), *pair)
                dq_ref[own, :] = both(dqs) * 0.125
                dk, dv = both(dks) * 0.125, both(dvs)
                dk_ref[own, :] = dk[ATTN_BLOCK:]
                dv_ref[own, :] = dv[ATTN_BLOCK:]
                dk_ref[prev, :] += dk[:ATTN_BLOCK]
                dv_ref[prev, :] += dv[:ATTN_BLOCK]
                return carry

            lax.fori_loop(0, N_UNITS, unit, 0, unroll=UNIT_UNROLL)

        _per_pattern(pattern)

    shape = jax.ShapeDtypeStruct((B, SEQ_LEN, ATTN_W), F32)
    return _pallas_hosting(
        body, ex, name="attn_bwd", grid=(B, 3),
        in_specs=[pl.BlockSpec(memory_space=pltpu.SMEM), _attn_col(0), _attn_col(3), _attn_col(6), _attn_col(0), _attn_col(0),
                  _attn_col(0)],
        out_specs=[_attn_col(0)] * 3,
        out_shape=[shape] * 3,
        scratch_shapes=[], semantics=("parallel", "parallel"), operands=(_alibi_coef(), qkv, qkv, qkv, do, lse, dd))


def _for_chunks(n_rows, fn, chunk):
    def step(c, carry):
        fn(pl.multiple_of(c * chunk, chunk))
        return carry

    lax.fori_loop(0, n_rows // chunk, step, 0)


def _shift_down(win, s, rows):
    lead = win.shape[0] - rows
    if s == 0:
        return win[lead:]
    if s % 8 == 0:
        return win[lead - s:lead - s + rows]
    q, r = divmod(s, 8)
    rolled = pltpu.roll(win, r, 0)
    return rolled[lead - 8 * q:lead - 8 * q + rows]


def _shift_up(win, s, rows):
    if s % 8 == 0:
        return win[s:s + rows]
    q, r = divmod(s, 8)
    rolled = pltpu.roll(win, win.shape[0] - r, 0)
    return rolled[8 * q:8 * q + rows]


CONV_PAD = 32


def _ln_silu(c, lg, lb):
    mu = jnp.mean(c, axis=-1, keepdims=True)
    cc = c - mu
    rstd = lax.rsqrt(jnp.mean(cc * cc, axis=-1, keepdims=True) + LN_EPS)
    nrm = cc * rstd
    v = nrm * lg + lb
    sg = _sigmoid(v)
    return v * sg, nrm, rstd, v, sg


def _conv_fwd(ci, w, b, lg, lb):
    B, S, _ = ci.shape
    CH = CONV_FWD_CHUNK

    def body(ci_ref, w_ref, b_ref, lg_ref, lb_ref, y_ref, c_ref, pad_ref):
        pad_ref[0:CONV_PAD, :] = jnp.zeros((CONV_PAD, CONV_W), F32)

        def glu(base):
            blk = ci_ref[pl.ds(base, CH), :]
            pad_ref[pl.ds(CONV_PAD + base, CH), :] = blk[:, 0:CONV_W] * _sigmoid(blk[:, CONV_W:])

        _for_chunks(S, glu, CH)

        def conv(base):
            win = pad_ref[pl.ds(base, CH + CONV_PAD), :]
            acc = jnp.broadcast_to(b_ref[...], (CH, CONV_W))
            for k in range(CONV_TAPS):
                acc = acc + w_ref[k:k + 1, :] * _shift_down(win, CONV_TAPS - 1 - k, CH)
            c_ref[pl.ds(base, CH), :] = acc
            y, _, _, _, _ = _ln_silu(acc, lg_ref[...], lb_ref[...])
            y_ref[pl.ds(base, CH), :] = y.astype(BF16)

        _for_chunks(S, conv, CH)

    vec = pl.BlockSpec((1, CONV_W), lambda i: (0, 0))
    return pl.pallas_call(
        body, name="conv_fwd", grid=(B,),
        in_specs=[pl.BlockSpec((None, S, CONV_IN_W), lambda i: (i, 0, 0)),
                  pl.BlockSpec((CONV_TAPS, CONV_W), lambda i: (0, 0)), vec, vec, vec],
        out_specs=[pl.BlockSpec((None, S, CONV_W), lambda i: (i, 0, 0)),
                   pl.BlockSpec((None, S, CONV_W), lambda i: (i, 0, 0))],
        out_shape=[jax.ShapeDtypeStruct((B, S, CONV_W), BF16), jax.ShapeDtypeStruct((B, S, CONV_W), F32)],
        scratch_shapes=[pltpu.VMEM((S + CONV_PAD, CONV_W), F32)],
        compiler_params=_params("parallel"),
    )(ci, w, b, lg, lb)


def _conv_bwd(ci, cpre, dy, w, lg, lb, ex=None):
    B, S, _ = ci.shape
    CH = CONV_BWD_CHUNK

    def body(ci_ref, c_ref, dy_ref, w_ref, lg_ref, lb_ref, dci_ref, dw_ref, db_ref, dlg_ref, dlb_ref, upad_ref, dcpad_ref,
             dwacc_ref):
        @pl.when(pl.program_id(0) == 0)
        def _():
            dw_ref[...] = jnp.zeros_like(dw_ref)
            db_ref[...] = jnp.zeros_like(db_ref)
            dlg_ref[...] = jnp.zeros_like(dlg_ref)
            dlb_ref[...] = jnp.zeros_like(dlb_ref)

        upad_ref[0:CONV_PAD, :] = jnp.zeros((CONV_PAD, CONV_W), F32)
        dcpad_ref[S:S + CONV_PAD, :] = jnp.zeros((CONV_PAD, CONV_W), F32)
        dwacc_ref[...] = jnp.zeros_like(dwacc_ref)

        def norm_bwd(base):
            blk = ci_ref[pl.ds(base, CH), :]
            upad_ref[pl.ds(CONV_PAD + base, CH), :] = blk[:, 0:CONV_W] * _sigmoid(blk[:, CONV_W:])
            lgv = lg_ref[...]
            _, nrm, rstd, v, sg = _ln_silu(c_ref[pl.ds(base, CH), :], lgv, lb_ref[...])
            dv = dy_ref[pl.ds(base, CH), :] * (sg * (1.0 + v * (1.0 - sg)))
            dlg_ref[...] += jnp.sum(dv * nrm, axis=0, keepdims=True)
            dlb_ref[...] += jnp.sum(dv, axis=0, keepdims=True)
            dn = dv * lgv
            dc = rstd * (dn - jnp.mean(dn, axis=-1, keepdims=True) - nrm * jnp.mean(dn * nrm, axis=-1, keepdims=True))
            dcpad_ref[pl.ds(base, CH), :] = dc
            db_ref[...] += jnp.sum(dc, axis=0, keepdims=True)

        _for_chunks(S, norm_bwd, CH)

        def conv_bwd(base):
            dwin = dcpad_ref[pl.ds(base, CH + CONV_PAD), :]
            uwin = upad_ref[pl.ds(base, CH + CONV_PAD), :]
            dc = dwin[0:CH]
            du = jnp.zeros((CH, CONV_W), F32)
            for k in range(CONV_TAPS):
                du = du + w_ref[k:k + 1, :] * _shift_up(dwin, CONV_TAPS - 1 - k, CH)
                prod = dc * _shift_down(uwin, CONV_TAPS - 1 - k, CH)
                dwacc_ref[8 * k:8 * k + 8, :] += jnp.sum(prod.reshape(CH // 8, 8, CONV_W), axis=0)
            blk = ci_ref[pl.ds(base, CH), :]
            a, sg = blk[:, 0:CONV_W], _sigmoid(blk[:, CONV_W:])
            dci_ref[pl.ds(base, CH), 0:CONV_W] = (du * sg).astype(BF16)
            dci_ref[pl.ds(base, CH), CONV_W:] = (du * a * sg * (1.0 - sg)).astype(BF16)

        _for_chunks(S, conv_bwd, CH)
        for k in range(CONV_TAPS):
            dw_ref[k:k + 1, :] += jnp.sum(dwacc_ref[8 * k:8 * k + 8, :], axis=0, keepdims=True)

    vec = pl.BlockSpec((1, CONV_W), lambda i: (0, 0))
    mat = pl.BlockSpec((CONV_TAPS, CONV_W), lambda i: (0, 0))
    seq = lambda width: pl.BlockSpec((None, S, width), lambda i: (i, 0, 0))
    return _pallas_hosting(
        body, ex, name="conv_bwd", grid=(B,),
        in_specs=[seq(CONV_IN_W), seq(CONV_W), seq(CONV_W), mat, vec, vec],
        out_specs=[seq(CONV_IN_W), mat, vec, vec, vec],
        out_shape=[jax.ShapeDtypeStruct((B, S, CONV_IN_W), BF16), jax.ShapeDtypeStruct((CONV_TAPS, CONV_W), F32),
                   jax.ShapeDtypeStruct((1, CONV_W), F32), jax.ShapeDtypeStruct((1, CONV_W), F32),
                   jax.ShapeDtypeStruct((1, CONV_W), F32)],
        scratch_shapes=[pltpu.VMEM((S + CONV_PAD, CONV_W), F32), pltpu.VMEM((S + CONV_PAD, CONV_W), F32),
                        pltpu.VMEM((8 * CONV_TAPS, CONV_W), F32)],
        semantics=("arbitrary",), operands=(ci, cpre, dy, w, lg, lb))


SCAN_SHIFTS = tuple(1 << e for e in range(11))


def _prev8(ref, base, cols, fill):
    start = pl.multiple_of(jnp.maximum(base - 8, 0), 8)
    return jnp.where(base > 0, ref[pl.ds(start, 8), cols], fill)


def _next8(ref, base, rows, total, cols, fill):
    start = pl.multiple_of(jnp.minimum(base + rows, total - 8), 8)
    return jnp.where(base + rows < total, ref[pl.ds(start, 8), cols], fill)


ALL = slice(None)
LRU_X = slice(LRU_W, LRU_IN_W)
LRU_GATE = slice(0, LRU_W)


def _lru_conv(li_ref, base, rows, cw_ref, cb_ref):
    win = jnp.concatenate([_prev8(li_ref, base, LRU_X, 0.0), li_ref[pl.ds(base, rows), LRU_X]], axis=0)
    u = jnp.broadcast_to(cb_ref[...], (rows, LRU_W))
    for k in range(LRU_TAPS):
        u = u + cw_ref[k:k + 1, :] * _shift_down(win, LRU_TAPS - 1 - k, rows)
    return u, win


def _lru_gates(u, wa_ref, ba_ref, wx_ref, bx_ref, sp):
    ub = u.astype(BF16)
    r = _sigmoid(_dot(ub, wa_ref[...]) + ba_ref[...])
    i = _sigmoid(_dot(ub, wx_ref[...]) + bx_ref[...])
    la = (-LRU_C) * r * sp
    a = jnp.exp(la)
    return ub, r, i, a, _one_minus_exp(2.0 * la, a * a)


def _scan_forward(bufs, S, CH):
    for n, s in enumerate(SCAN_SHIFTS):
        (sa, sb), (da, db) = bufs[n % 2], bufs[(n + 1) % 2]

        def step(base, s=s, sa=sa, sb=sb, da=da, db=db):
            a, b = sa[pl.ds(base, CH), :], sb[pl.ds(base, CH), :]
            if s < 8:
                a_s = _shift_down(jnp.concatenate([_prev8(sa, base, ALL, 1.0), a], axis=0), s, CH)
                b_s = _shift_down(jnp.concatenate([_prev8(sb, base, ALL, 0.0), b], axis=0), s, CH)
            elif s < CH:
                start = pl.multiple_of(jnp.maximum(base - s, 0), 8)
                a_s = jnp.concatenate([jnp.where(base > 0, sa[pl.ds(start, s), :], 1.0), a[0:CH - s]], axis=0)
                b_s = jnp.concatenate([jnp.where(base > 0, sb[pl.ds(start, s), :], 0.0), b[0:CH - s]], axis=0)
            else:
                start = pl.multiple_of(jnp.maximum(base - s, 0), 8)
                a_s = jnp.where(base < s, 1.0, sa[pl.ds(start, CH), :])
                b_s = jnp.where(base < s, 0.0, sb[pl.ds(start, CH), :])
            db[pl.ds(base, CH), :] = a * b_s + b
            da[pl.ds(base, CH), :] = a * a_s

        _for_chunks(S, step, CH)
    return len(SCAN_SHIFTS) % 2


def _scan_backward(bufs, S, CH):
    for n, s in enumerate(SCAN_SHIFTS):
        (sa, sb), (da, db) = bufs[n % 2], bufs[(n + 1) % 2]

        def step(base, s=s, sa=sa, sb=sb, da=da, db=db):
            a, b = sa[pl.ds(base, CH), :], sb[pl.ds(base, CH), :]
            if s < 8:
                a_s = _shift_up(jnp.concatenate([a, _next8(sa, base, CH, S, ALL, 1.0)], axis=0), s, CH)
                b_s = _shift_up(jnp.concatenate([b, _next8(sb, base, CH, S, ALL, 0.0)], axis=0), s, CH)
            elif s < CH:
                start = pl.multiple_of(jnp.minimum(base + CH, S - s), 8)
                more = base + CH < S
                a_s = jnp.concatenate([a[s:CH], jnp.where(more, sa[pl.ds(start, s), :], 1.0)], axis=0)
                b_s = jnp.concatenate([b[s:CH], jnp.where(more, sb[pl.ds(start, s), :], 0.0)], axis=0)
            else:
                start = pl.multiple_of(jnp.minimum(base + s, S - CH), 8)
                a_s = jnp.where(base + s >= S, 1.0, sa[pl.ds(start, CH), :])
                b_s = jnp.where(base + s >= S, 0.0, sb[pl.ds(start, CH), :])
            db[pl.ds(base, CH), :] = a * b_s + b
            da[pl.ds(base, CH), :] = a * a_s

        _for_chunks(S, step, CH)
    return len(SCAN_SHIFTS) % 2


def _lru_fwd(li, cw, cb, wa, ba, wx, bx, lam, ex=None):
    B, S, _ = li.shape
    CH = LRU_FWD_CHUNK

    def body(li_ref, cw_ref, cb_ref, wa_ref, ba_ref, wx_ref, bx_ref, lam_ref, y_ref, h_ref, a0, b0, a1, b1):
        sp = _softplus(-lam_ref[...])

        def gates(base):
            u, _ = _lru_conv(li_ref, base, CH, cw_ref, cb_ref)
            _, _, i, a, em = _lru_gates(u, wa_ref, ba_ref, wx_ref, bx_ref, sp)
            a0[pl.ds(base, CH), :] = a
            b0[pl.ds(base, CH), :] = jnp.sqrt(em) * (i * u)

        _for_chunks(S, gates, CH)
        bufs = ((a0, b0), (a1, b1))
        hb = bufs[_scan_forward(bufs, S, CH)][1]

        def out(base):
            h = hb[pl.ds(base, CH), :]
            h_ref[pl.ds(base, CH), :] = h
            gl, _ = _gelu(li_ref[pl.ds(base, CH), LRU_GATE])
            y_ref[pl.ds(base, CH), :] = (gl * h).astype(BF16)

        _for_chunks(S, out, CH)

    vec = pl.BlockSpec((1, LRU_W), lambda i: (0, 0))
    mat = pl.BlockSpec((LRU_W, LRU_W), lambda i: (0, 0))
    seq = lambda width: pl.BlockSpec((None, S, width), lambda i: (i, 0, 0))
    return _pallas_hosting(
        body, ex, name="lru_fwd", grid=(B,),
        in_specs=[seq(LRU_IN_W), pl.BlockSpec((LRU_TAPS, LRU_W), lambda i: (0, 0)), vec, mat, vec, mat, vec, vec],
        out_specs=[seq(LRU_W), seq(LRU_W)],
        out_shape=[jax.ShapeDtypeStruct((B, S, LRU_W), BF16), jax.ShapeDtypeStruct((B, S, LRU_W), F32)],
        scratch_shapes=[pltpu.VMEM((S, LRU_W), F32)] * 4,
        semantics=("parallel",), operands=(li, cw, cb, wa, ba, wx, bx, lam))


def _lru_bwd(li, hs, dy, cw, cb, wa, ba, wx, bx, lam, ex=None):
    B, S, _ = li.shape
    CH = LRU_BWD_CHUNK

    def body(li_ref, hs_ref, dy_ref, cw_ref, cb_ref, wa_ref, ba_ref, wx_ref, bx_ref, lam_ref,
             dli_ref, dcw_ref, dcb_ref, dwa_ref, dba_ref, dwx_ref, dbx_ref, dlam_ref, a0, b0, a1, b1, u_s, du_s):
        @pl.when(pl.program_id(0) == 0)
        def _():
            for ref in (dcw_ref, dcb_ref, dwa_ref, dba_ref, dwx_ref, dbx_ref, dlam_ref):
                ref[...] = jnp.zeros_like(ref)

        lam_v = lam_ref[...]
        sp = _softplus(-lam_v)
        dsp_dlam = -_sigmoid(-lam_v)

        def gates(base):
            u, _ = _lru_conv(li_ref, base, CH, cw_ref, cb_ref)
            _, _, _, a, _ = _lru_gates(u, wa_ref, ba_ref, wx_ref, bx_ref, sp)
            gl, _ = _gelu(li_ref[pl.ds(base, CH), LRU_GATE])
            u_s[pl.ds(base, CH), :] = u
            a0[pl.ds(base, CH), :] = a
            b0[pl.ds(base, CH), :] = a * (dy_ref[pl.ds(base, CH), :] * gl)

        _for_chunks(S, gates, CH)
        bufs = ((a0, b0), (a1, b1))
        eb = bufs[_scan_backward(bufs, S, CH)][1]

        def grads(base):
            e = eb[pl.ds(base, CH), :]
            e_next = _shift_up(jnp.concatenate([e, _next8(eb, base, CH, S, ALL, 0.0)], axis=0), 1, CH)
            gate = li_ref[pl.ds(base, CH), LRU_GATE]
            gl, th = _gelu(gate)
            dy = dy_ref[pl.ds(base, CH), :]
            g = dy * gl + e_next
            h = hs_ref[pl.ds(base, CH), :]
            h_prev = _shift_down(jnp.concatenate([_prev8(hs_ref, base, ALL, 0.0), h], axis=0), 1, CH)
            dli_ref[pl.ds(base, CH), LRU_GATE] = (dy * h * _gelu_grad(gate, th)).astype(BF16)
            u = u_s[pl.ds(base, CH), :]
            ub, r, i, a, em = _lru_gates(u, wa_ref, ba_ref, wx_ref, bx_ref, sp)
            mult = jnp.sqrt(em)
            da = g * h_prev
            dmult = g * (i * u)
            di = g * mult * u
            dla = da * a - dmult * (a * a) * lax.rsqrt(jnp.maximum(em, 1e-30))
            dlam_ref[...] += dsp_dlam * jnp.sum(dla * ((-LRU_C) * r), axis=0, keepdims=True)
            dpa = (dla * ((-LRU_C) * sp)) * r * (1.0 - r)
            dpx = di * i * (1.0 - i)
            dpab, dpxb = dpa.astype(BF16), dpx.astype(BF16)
            dwa_ref[...] += _dot_tn(ub, dpab)
            dwx_ref[...] += _dot_tn(ub, dpxb)
            dba_ref[...] += jnp.sum(dpa, axis=0, keepdims=True)
            dbx_ref[...] += jnp.sum(dpx, axis=0, keepdims=True)
            du = g * mult * i + _dot_nt(dpab, wa_ref[...]) + _dot_nt(dpxb, wx_ref[...])
            du_s[pl.ds(base, CH), :] = du
            dcb_ref[...] += jnp.sum(du, axis=0, keepdims=True)

        _for_chunks(S, grads, CH)

        def conv_bwd(base):
            du = du_s[pl.ds(base, CH), :]
            dwin = jnp.concatenate([du, _next8(du_s, base, CH, S, ALL, 0.0)], axis=0)
            xwin = jnp.concatenate([_prev8(li_ref, base, LRU_X, 0.0), li_ref[pl.ds(base, CH), LRU_X]], axis=0)
            dx = jnp.zeros((CH, LRU_W), F32)
            for k in range(LRU_TAPS):
                dx = dx + cw_ref[k:k + 1, :] * _shift_up(dwin, LRU_TAPS - 1 - k, CH)
                dcw_ref[k:k + 1, :] += jnp.sum(du * _shift_down(xwin, LRU_TAPS - 1 - k, CH), axis=0, keepdims=True)
            dli_ref[pl.ds(base, CH), LRU_X] = dx.astype(BF16)

        _for_chunks(S, conv_bwd, CH)

    vec = pl.BlockSpec((1, LRU_W), lambda i: (0, 0))
    mat = pl.BlockSpec((LRU_W, LRU_W), lambda i: (0, 0))
    taps = pl.BlockSpec((LRU_TAPS, LRU_W), lambda i: (0, 0))
    seq = lambda width: pl.BlockSpec((None, S, width), lambda i: (i, 0, 0))
    vec_shape = jax.ShapeDtypeStruct((1, LRU_W), F32)
    mat_shape = jax.ShapeDtypeStruct((LRU_W, LRU_W), F32)
    return _pallas_hosting(
        body, ex, name="lru_bwd", grid=(B,),
        in_specs=[seq(LRU_IN_W), seq(LRU_W), seq(LRU_W), taps, vec, mat, vec, mat, vec, vec],
        out_specs=[seq(LRU_IN_W), taps, vec, mat, vec, mat, vec, vec],
        out_shape=[jax.ShapeDtypeStruct((B, S, LRU_IN_W), BF16), jax.ShapeDtypeStruct((LRU_TAPS, LRU_W), F32),
                   vec_shape, mat_shape, vec_shape, mat_shape, vec_shape, vec_shape],
        scratch_shapes=[pltpu.VMEM((S, LRU_W), F32)] * 6,
        semantics=("arbitrary",), operands=(li, hs, dy, cw, cb, wa, ba, wx, bx, lam))


MESH = pl.DeviceIdType.MESH
HBM_SPEC = pl.BlockSpec(memory_space=pltpu.HBM)


def _slot(ref, p):
    return ref.at[p]


def _row_block(rows):
    return lambda ref, p: ref.at[pl.ds(p * rows, rows), :]


def _col_block(cols):
    return lambda ref, p: ref.at[:, pl.ds(p * cols, cols)]


class _Gather:
    def __init__(self, blocks, out_shapes, places):
        self.sources, self.out_shapes, self.places, self.n = list(blocks), list(out_shapes), list(places), len(blocks)

    def scratch(self):
        return [pltpu.SemaphoreType.DMA((self.n, 7)), pltpu.SemaphoreType.DMA((self.n, 7)), pltpu.SemaphoreType.DMA((self.n,))]

    def _plan(self, x_refs, out_refs, send_sems, recv_sems, local_sems):
        n = self.n
        x, y, c = lax.axis_index("x"), lax.axis_index("y"), lax.axis_index("c")
        me, sibling = (x, y, c), (x, y, 1 - c)
        chips = [(1 - x, y), (x, 1 - y), (1 - x, 1 - y)]

        def place(a, dev):
            return self.places[a](out_refs[a], 4 * dev[0] + 2 * dev[1] + dev[2])

        def copy(a, k, blk, to, src=None):
            return pltpu.make_async_remote_copy(
                src_ref=place(a, blk) if src is None else src, dst_ref=place(a, blk),
                send_sem=send_sems.at[a, k], recv_sem=recv_sems.at[a, k], device_id=to, device_id_type=MESH)

        mine = [pltpu.make_async_copy(x_refs[a], place(a, me), local_sems.at[a]) for a in range(n)]
        first = [copy(a, 0, me, sibling, src=x_refs[a]) for a in range(n)]
        first += [copy(a, 1 + j, me, (*chip, c), src=x_refs[a]) for j, chip in enumerate(chips) for a in range(n)]
        return me, sibling, chips, c, copy, mine, first

    def start(self, *refs):
        *_, mine, first = self._plan(*refs)
        for cp in mine + first:
            cp.start()

    def finish(self, *refs):
        me, sibling, chips, c, copy, mine, first = self._plan(*refs)
        passed = []
        for j, chip in enumerate(chips):
            for a in range(self.n):
                copy(a, 1 + j, (*chip, c), me).wait_recv()
                passed.append(copy(a, 4 + j, (*chip, c), sibling))
                passed[-1].start()
        for a in range(self.n):
            copy(a, 0, sibling, me).wait_recv()
        for j, chip in enumerate(chips):
            for a in range(self.n):
                copy(a, 4 + j, (*chip, 1 - c), me).wait_recv()
        for cp in first + passed:
            cp.wait_send()
        for cp in mine:
            cp.wait()


class _GradExchange:
    def __init__(self, sources, takes, piece_shapes):
        self.sources, self.takes, self.n = list(sources), list(takes), len(sources)
        self.out_shapes = [jax.ShapeDtypeStruct((N_DEV,) + tuple(s), BF16) for s in piece_shapes]

    def scratch(self):
        return [pltpu.SemaphoreType.DMA((self.n, 7)), pltpu.SemaphoreType.DMA((self.n, 7)), pltpu.SemaphoreType.DMA((self.n,))]

    def _copies(self, src_refs, out_refs, send_sems, recv_sems, local_sems):
        x, y, c = lax.axis_index("x"), lax.axis_index("y"), lax.axis_index("c")
        me = 4 * x + 2 * y + c
        mine = [pltpu.make_async_copy(self.takes[i](src_refs[i], me), out_refs[i].at[me], local_sems.at[i]) for i in range(self.n)]
        remote = []
        for k in range(1, N_DEV):
            px, py, pc = x ^ ((k >> 2) & 1), y ^ ((k >> 1) & 1), c ^ (k & 1)
            peer = 4 * px + 2 * py + pc
            for i in range(self.n):
                remote.append(pltpu.make_async_remote_copy(
                    src_ref=self.takes[i](src_refs[i], peer), dst_ref=out_refs[i].at[me], send_sem=send_sems.at[i, k - 1],
                    recv_sem=recv_sems.at[i, k - 1], device_id=(px, py, pc), device_id_type=MESH))
        return mine, remote

    def start(self, *refs):
        mine, remote = self._copies(*refs)
        for cp in mine + remote:
            cp.start()

    def finish(self, *refs):
        mine, remote = self._copies(*refs)
        for cp in remote:
            cp.wait_recv()
        for cp in remote:
            cp.wait_send()
        for cp in mine:
            cp.wait()


def _run_exchange(ex, name):
    def body(*refs):
        src_refs, out_refs, sems = refs[:ex.n], refs[ex.n:2 * ex.n], refs[2 * ex.n:]
        ex.start(src_refs, out_refs, *sems)
        ex.finish(src_refs, out_refs, *sems)

    return pl.pallas_call(
        body, name=name, out_shape=ex.out_shapes, in_specs=[HBM_SPEC] * ex.n, out_specs=[HBM_SPEC] * ex.n,
        scratch_shapes=ex.scratch(),
    )(*ex.sources)


def _pallas_hosting(body, ex, *, name, grid, in_specs, out_specs, out_shape, scratch_shapes, semantics, operands):
    if ex is None:
        outs = pl.pallas_call(body, name=name, grid=grid, in_specs=in_specs, out_specs=out_specs, out_shape=out_shape,
                              scratch_shapes=scratch_shapes, compiler_params=_params(*semantics))(*operands)
        return outs, None
    n_in, n_out, n_scr, n = len(in_specs), len(out_specs), len(scratch_shapes), ex.n

    def at_step(pick):
        conds = [pl.program_id(k) == pick(size) for k, size in enumerate(grid)]
        return functools.reduce(jnp.logical_and, conds)

    def hosting(*refs):
        ins, ex_ins = refs[:n_in], refs[n_in:n_in + n]
        outs, ex_outs = refs[n_in + n:n_in + n + n_out], refs[n_in + n + n_out:n_in + 2 * n + n_out]
        scratch, sems = refs[n_in + 2 * n + n_out:n_in + 2 * n + n_out + n_scr], refs[n_in + 2 * n + n_out + n_scr:]
        pl.when(at_step(lambda size: 0))(lambda: ex.start(ex_ins, ex_outs, *sems))
        body(*ins, *outs, *scratch)
        pl.when(at_step(lambda size: size - 1))(lambda: ex.finish(ex_ins, ex_outs, *sems))

    res = pl.pallas_call(
        hosting, name=name, grid=grid, in_specs=list(in_specs) + [HBM_SPEC] * n, out_specs=list(out_specs) + [HBM_SPEC] * n,
        out_shape=list(out_shape) + ex.out_shapes, scratch_shapes=list(scratch_shapes) + ex.scratch(),
        compiler_params=_params(*(["arbitrary"] * len(grid))),
    )(*operands, *ex.sources)
    return res[:n_out], res[n_out:]


def _sum_slots(parts, rows_per_step, name):
    _, R, C = parts.shape

    def body(p_ref, o_ref):
        acc = p_ref[0].astype(F32)
        for q in range(1, N_DEV):
            acc = acc + p_ref[q].astype(F32)
        o_ref[...] = acc

    return pl.pallas_call(
        body, name=name, grid=(R // rows_per_step,),
        in_specs=[pl.BlockSpec((N_DEV, rows_per_step, C), lambda i: (0, i, 0))],
        out_specs=pl.BlockSpec((rows_per_step, C), lambda i: (i, 0)),
        out_shape=jax.ShapeDtypeStruct((R, C), F32),
        compiler_params=_params("parallel"),
    )(parts)


def _adamw(w, g, m, v, rows_per_step, name):
    R, C = w.shape
    c1 = 1.0 - ADAM_B1 ** ADAM_STEP
    c2 = 1.0 - ADAM_B2 ** ADAM_STEP

    def body(w_ref, g_ref, m_ref, v_ref, d_ref, nm_ref, nv_ref):
        gv = g_ref[...]
        nm = ADAM_B1 * m_ref[...] + (1.0 - ADAM_B1) * gv
        nv = ADAM_B2 * v_ref[...] + (1.0 - ADAM_B2) * (gv * gv)
        nm_ref[...] = nm
        nv_ref[...] = nv
        d_ref[...] = (-ADAM_LR) * ((nm / c1) / (jnp.sqrt(nv / c2) + ADAM_EPS) + ADAM_WD * w_ref[...])

    spec = pl.BlockSpec((rows_per_step, C), lambda i: (i, 0))
    shape = jax.ShapeDtypeStruct((R, C), F32)
    return pl.pallas_call(
        body, name=name, grid=(R // rows_per_step,),
        in_specs=[spec] * 4, out_specs=[spec] * 3, out_shape=[shape] * 3,
        compiler_params=_params("parallel"),
    )(w, g, m, v)


WEIGHT_ORDER = ("norm1_g", "w_in", "conv_dw_w", "conv_dw_b", "conv_ln_g", "conv_ln_b", "lru_conv_w", "lru_conv_b", "lru_wa",
                "lru_ba", "lru_wx", "lru_bx", "lru_lambda", "w_out", "norm2_g", "w_up", "w_down", "final_g")
BIG = ("w_in", "w_out", "w_up", "w_down")
SMALL_SHARDED = {"conv_dw_w": (DEPTH, CONV_TAPS, CONV_W), "lru_conv_w": (DEPTH, LRU_TAPS, LRU_W)}
SMALL_FULL = {
    "norm1_g": (DEPTH, D_MODEL), "conv_dw_w": (DEPTH, CONV_TAPS, CONV_W), "conv_dw_b": (DEPTH, CONV_W),
    "conv_ln_g": (DEPTH, CONV_W), "conv_ln_b": (DEPTH, CONV_W), "lru_conv_w": (DEPTH, LRU_TAPS, LRU_W),
    "lru_conv_b": (DEPTH, LRU_W), "lru_wa": (DEPTH, LRU_HEADS, HEAD_DIM, HEAD_DIM), "lru_ba": (DEPTH, LRU_W),
    "lru_wx": (DEPTH, LRU_HEADS, HEAD_DIM, HEAD_DIM), "lru_bx": (DEPTH, LRU_W), "lru_lambda": (DEPTH, LRU_W),
    "norm2_g": (DEPTH, D_MODEL), "final_g": (D_MODEL,),
}
SMALL_COLS = 128
SMALL_ROWS = 1000
FILTER_ROWS = 24
W_IN_SHARD = IN_COLS // N_DEV
W_OUT_SHARD = D_MODEL // N_DEV
FF_SHARD = D_FF // N_DEV


def _pack_rows(flat_parts, cols, rows):
    flat = jnp.concatenate(flat_parts)
    return jnp.pad(flat, (0, rows * cols - flat.shape[0])).reshape(rows, cols)


WEIGHT_GATHER = {
    "w_in": ((N_DEV, D_MODEL, W_IN_SHARD), _slot),
    "w_out": ((D_MODEL, D_MODEL), _row_block(W_OUT_SHARD)),
    "w_up": ((D_MODEL, D_FF), _col_block(FF_SHARD)),
    "w_down": ((D_FF, D_MODEL), _row_block(FF_SHARD)),
}
GATHER_HOSTS = {
    "inproj": (("w_out", 0),), "attn_fwd": (("w_up", 0),), "lru_fwd": (("w_down", 0),),
    "outproj": (("w_in", 1),), "up": (("w_up", 1),), "down": (("w_down", 1), ("w_out", 1)),
}


def _weight_gather(local, items, with_filters=False):
    blocks = [local[n][l].astype(BF16) for n, l in items]
    shapes = [jax.ShapeDtypeStruct(WEIGHT_GATHER[n][0], BF16) for n, _ in items]
    places = [WEIGHT_GATHER[n][1] for n, _ in items]
    if with_filters:
        blocks.append(_pack_rows([local[n].reshape(-1) for n in SMALL_SHARDED], SMALL_COLS, FILTER_ROWS))
        shapes.append(jax.ShapeDtypeStruct((N_DEV, FILTER_ROWS, SMALL_COLS), F32))
        places.append(_slot)
    return _Gather(blocks, shapes, places)


def _keep_gathered(full, items, landed):
    for (n, l), arr in zip(items, landed):
        full[n][l] = arr.transpose(1, 0, 2).reshape(D_MODEL, IN_COLS) if n == "w_in" else arr


def _unpack_filters(slots):
    flat, off, out = slots.reshape(N_DEV, -1), 0, {}
    for n, shp in SMALL_SHARDED.items():
        shard = shp[:-1] + (shp[-1] // N_DEV,)
        size = int(np.prod(shard))
        out[n] = jnp.moveaxis(flat[:, off:off + size].reshape((N_DEV,) + shard), 0, -2).reshape(shp)
        off += size
    return out


def _grad_exchange(name, dw):
    if name == "w_in":
        return _GradExchange([dw.reshape(D_MODEL, N_DEV, W_IN_SHARD).transpose(1, 0, 2)], [_slot], [(D_MODEL, W_IN_SHARD)])
    if name == "w_out":
        return _GradExchange([dw], [_row_block(W_OUT_SHARD)], [(W_OUT_SHARD, D_MODEL)])
    return _GradExchange([dw], [_col_block(FF_SHARD)], [(D_MODEL, FF_SHARD)])


def _sum_adamw(parts, w, m, v, rows_per_step, transposed, name):
    _, R, C = parts[0].shape
    tr = rows_per_step
    steps = R // tr
    c1 = 1.0 - ADAM_B1 ** ADAM_STEP
    c2 = 1.0 - ADAM_B2 ** ADAM_STEP

    def slot_sum(p_ref):
        acc = p_ref[0].astype(F32)
        for q in range(1, N_DEV):
            acc = acc + p_ref[q].astype(F32)
        return acc

    def body(p0_ref, p1_ref, w_ref, m_ref, v_ref, g_ref, d_ref, nm_ref, nv_ref):
        gv = jnp.where(pl.program_id(0) == 0, slot_sum(p0_ref), slot_sum(p1_ref))
        if transposed:
            gv = gv.T
        nm = ADAM_B1 * m_ref[...] + (1.0 - ADAM_B1) * gv
        nv = ADAM_B2 * v_ref[...] + (1.0 - ADAM_B2) * (gv * gv)
        g_ref[...] = gv
        nm_ref[...] = nm
        nv_ref[...] = nv
        d_ref[...] = (-ADAM_LR) * ((nm / c1) / (jnp.sqrt(nv / c2) + ADAM_EPS) + ADAM_WD * w_ref[...])

    if transposed:
        spec = pl.BlockSpec((None, C, tr), lambda l, i: (l, 0, i))
    else:
        spec = pl.BlockSpec((None, tr, C), lambda l, i: (l, i, 0))
    shape = jax.ShapeDtypeStruct(w.shape, F32)
    part0 = pl.BlockSpec((N_DEV, tr, C), lambda l, i: (0, jnp.where(l == 0, i, steps - 1), 0))
    part1 = pl.BlockSpec((N_DEV, tr, C), lambda l, i: (0, jnp.where(l == 1, i, 0), 0))
    return pl.pallas_call(
        body, name=name, grid=(DEPTH, steps),
        in_specs=[part0, part1, spec, spec, spec],
        out_specs=[spec] * 4, out_shape=[shape] * 4,
        compiler_params=_params("arbitrary", "arbitrary"),
    )(parts[0], parts[1], w, m, v)


def _block_diag(w):
    eye = jnp.eye(LRU_HEADS, dtype=bool)
    return jnp.where(eye[:, None, :, None], w[:, :, None, :], jnp.zeros((), w.dtype)).reshape(LRU_W, LRU_W)


def _diag_blocks(m):
    eye = jnp.eye(LRU_HEADS, dtype=bool)
    m4 = m.reshape(LRU_HEADS, HEAD_DIM, LRU_HEADS, HEAD_DIM)
    return jnp.sum(jnp.where(eye[:, None, :, None], m4, 0.0), axis=2)


def _pack_small(values, loss_row):
    parts = [values[n].reshape(-1) for n in SMALL_FULL]
    parts.append(loss_row.reshape(-1))
    return _pack_rows(parts, SMALL_COLS, SMALL_ROWS)


def _unpack_small(packed):
    flat = packed.reshape(-1)
    out, off = {}, 0
    for n, shp in SMALL_FULL.items():
        size = int(np.prod(shp))
        out[n] = flat[off:off + size].reshape(shp)
        off += size
    return out, flat[off]


def kernel(x, norm1_g, w_in, conv_dw_w, conv_dw_b, conv_ln_g, conv_ln_b, lru_conv_w, lru_conv_b, lru_wa, lru_ba, lru_wx, lru_bx, lru_lambda, w_out, norm2_g, w_up, w_down, final_g, loss_target, m_norm1_g, m_w_in, m_conv_dw_w, m_conv_dw_b, m_conv_ln_g, m_conv_ln_b, m_lru_conv_w, m_lru_conv_b, m_lru_wa, m_lru_ba, m_lru_wx, m_lru_bx, m_lru_lambda, m_w_out, m_norm2_g, m_w_up, m_w_down, m_final_g, v_norm1_g, v_w_in, v_conv_dw_w, v_conv_dw_b, v_conv_ln_g, v_conv_ln_b, v_lru_conv_w, v_lru_conv_b, v_lru_wa, v_lru_ba, v_lru_wx, v_lru_bx, v_lru_lambda, v_w_out, v_norm2_g, v_w_up, v_w_down, v_final_g):
    local = dict(zip(WEIGHT_ORDER, (norm1_g, w_in, conv_dw_w, conv_dw_b, conv_ln_g, conv_ln_b, lru_conv_w, lru_conv_b, lru_wa,
                                    lru_ba, lru_wx, lru_bx, lru_lambda, w_out, norm2_g, w_up, w_down, final_g)))
    mom1 = dict(zip(WEIGHT_ORDER, (m_norm1_g, m_w_in, m_conv_dw_w, m_conv_dw_b, m_conv_ln_g, m_conv_ln_b, m_lru_conv_w,
                                   m_lru_conv_b, m_lru_wa, m_lru_ba, m_lru_wx, m_lru_bx, m_lru_lambda, m_w_out, m_norm2_g,
                                   m_w_up, m_w_down, m_final_g)))
    mom2 = dict(zip(WEIGHT_ORDER, (v_norm1_g, v_w_in, v_conv_dw_w, v_conv_dw_b, v_conv_ln_g, v_conv_ln_b, v_lru_conv_w,
                                   v_lru_conv_b, v_lru_wa, v_lru_ba, v_lru_wx, v_lru_bx, v_lru_lambda, v_w_out, v_norm2_g,
                                   v_w_up, v_w_down, v_final_g)))
    B, S, _ = x.shape
    T = B * S
    my_slot = 4 * lax.axis_index("x") + 2 * lax.axis_index("y") + lax.axis_index("c")
    row = lambda a: a.reshape(1, -1)

    full = {n: [None] * DEPTH for n in BIG}
    first_items = (("w_in", 0),)
    landed = _run_exchange(_weight_gather(local, first_items, with_filters=True), "gather_first_weights")
    _keep_gathered(full, first_items, landed)
    full.update(_unpack_filters(landed[-1]))

    def hosted(call, layer, fn, *args):
        items = GATHER_HOSTS[call] if layer == 0 else ()
        outs, landed = fn(*args, ex=_weight_gather(local, items) if items else None)
        _keep_gathered(full, items, landed or ())
        return outs

    saved = []
    cur = x.reshape(T, D_MODEL)
    for l in range(DEPTH):
        h, qkv, ci, li = hosted("inproj", l, _inproj, cur, row(norm1_g[l]), full["w_in"][l])
        qkv = qkv.reshape(B, S, QKV_W)
        o, lse = hosted("attn_fwd", l, _attn_fwd, qkv)
        o, lse = o.reshape(T, ATTN_W), lse.reshape(T, ATTN_W)
        ci = ci.reshape(B, S, CONV_IN_W)
        li = li.reshape(B, S, LRU_IN_W)
        conv_p = (full["conv_dw_w"][l], row(conv_dw_b[l]), row(conv_ln_g[l]), row(conv_ln_b[l]))
        lru_p = (full["lru_conv_w"][l], row(lru_conv_b[l]), _block_diag(lru_wa[l]).astype(BF16), row(lru_ba[l]),
                 _block_diag(lru_wx[l]).astype(BF16), row(lru_bx[l]), row(lru_lambda[l]))
        yc, cpre = _conv_fwd(ci, *conv_p)
        yl, hs = hosted("lru_fwd", l, _lru_fwd, li, *lru_p)
        x1, mix = hosted("outproj", l, _outproj, cur, o, lse, yc.reshape(T, CONV_W), yl.reshape(T, LRU_W), full["w_out"][l])
        h2, r = hosted("up", l, _up, x1, row(norm2_g[l]), full["w_up"][l])
        (x2,) = hosted("down", l, _down, x1, r, full["w_down"][l])
        saved.append(dict(x=cur, h=h, qkv=qkv, o=o, lse=lse, ci=ci, li=li, cpre=cpre, hs=hs, x1=x1, mix=mix, h2=h2, r=r,
                          conv_p=conv_p, lru_p=lru_p))
        cur = x2

    dx, loss_part, dgf = _loss_head(cur, loss_target.reshape(T, D_MODEL), row(final_g))

    received = {n: [None] * DEPTH for n in BIG}
    small_grads = {n: [None] * DEPTH for n in SMALL_FULL if n != "final_g"}
    for l in reversed(range(DEPTH)):
        sv = saved[l]
        (dpre,), _ = _down_bwd_act(dx, sv["r"], full["w_down"][l])
        dw_down = _down_bwd_w(sv["r"], dx)
        (dx1, dg2), _ = _up_bwd_act(dpre, full["w_up"][l], sv["x1"], row(norm2_g[l]), dx)
        dw_up = _up_bwd_w(sv["h2"], dpre)
        do, dd, dyc, dyl, dw_out = _outproj_bwd(dx1, sv["mix"], full["w_out"][l], sv["o"], sv["lse"])
        seq = lambda a: a.reshape(B, S, ATTN_W)
        (dq, dk, dv), (received["w_down"][l],) = _attn_bwd(sv["qkv"], seq(do), seq(sv["lse"]), seq(dd),
                                                           ex=_grad_exchange("w_down", dw_down))
        (dci, dcw, dcb, dlg, dlb), (received["w_out"][l],) = _conv_bwd(
            sv["ci"], sv["cpre"], dyc.reshape(B, S, CONV_W), sv["conv_p"][0], sv["conv_p"][2], sv["conv_p"][3],
            ex=_grad_exchange("w_out", dw_out))
        (dli, dlcw, dlcb, dwa, dba, dwx, dbx, dlam), (received["w_up"][l],) = _lru_bwd(
            sv["li"], sv["hs"], dyl.reshape(B, S, LRU_W), *sv["lru_p"], ex=_grad_exchange("w_up", dw_up))
        dz = tuple(t.reshape(T, -1) for t in (dq, dk, dv, dci, dli))
        dw_in = _inproj_bwd_w(dz, sv["h"])
        (dx, dg1), (received["w_in"][l],) = _inproj_bwd_act(dz, full["w_in"][l], sv["x"], row(norm1_g[l]), dx1,
                                                            ex=_grad_exchange("w_in", dw_in))
        for n, g in (("norm1_g", dg1), ("conv_dw_w", dcw), ("conv_dw_b", dcb), ("conv_ln_g", dlg), ("conv_ln_b", dlb),
                     ("lru_conv_w", dlcw), ("lru_conv_b", dlcb), ("lru_wa", _diag_blocks(dwa)), ("lru_ba", dba),
                     ("lru_wx", _diag_blocks(dwx)), ("lru_bx", dbx), ("lru_lambda", dlam), ("norm2_g", dg2)):
            small_grads[n][l] = g.reshape(SMALL_FULL[n][1:])
    grad_x = dx.reshape(B, S, D_MODEL)

    small_local = {n: jnp.stack(g) for n, g in small_grads.items()}
    small_local["final_g"] = dgf.reshape(D_MODEL)
    packet = _pack_small(small_local, loss_part[0:1, :])
    small_gather = _Gather([packet], [jax.ShapeDtypeStruct((N_DEV, SMALL_ROWS, SMALL_COLS), F32)], [_slot])
    (gathered,) = _run_exchange(small_gather, "gather_small_grads")

    grads, delta, new_m, new_v = {}, {}, {}, {}
    for n, rows_per_step in (("w_in", 256), ("w_out", W_OUT_SHARD), ("w_up", 256), ("w_down", 256)):
        grads[n], delta[n], new_m[n], new_v[n] = _sum_adamw(received[n], local[n], mom1[n], mom2[n], rows_per_step,
                                                            n == "w_down", "sum_adamw_" + n)

    small_sum = _sum_slots(gathered, SMALL_ROWS, "sum_small_grads")
    small_g, loss = _unpack_small(small_sum)

    for n, fullshape in SMALL_SHARDED.items():
        width = fullshape[-1] // N_DEV
        g = lax.dynamic_slice_in_dim(small_g[n], my_slot * width, width, axis=2)
        two_d = (fullshape[0] * fullshape[1], width)
        d, nm, nv = _adamw(local[n].reshape(two_d), g.reshape(two_d), mom1[n].reshape(two_d), mom2[n].reshape(two_d),
                           two_d[0], "adamw_" + n)
        grads[n], delta[n], new_m[n], new_v[n] = (t.reshape(g.shape) for t in (g.reshape(two_d), d, nm, nv))
    replicated = [n for n in SMALL_FULL if n not in SMALL_SHARDED]
    zero_row = jnp.zeros((1, SMALL_COLS), F32)
    packed_state = []
    for src in (local, mom1, mom2):
        vals = {n: (src[n] if n in replicated else jnp.zeros(SMALL_FULL[n], F32)) for n in SMALL_FULL}
        packed_state.append(_pack_small(vals, zero_row))
    d, nm, nv = _adamw(packed_state[0], small_sum, packed_state[1], packed_state[2], SMALL_ROWS, "adamw_small")
    d, nm, nv = _unpack_small(d)[0], _unpack_small(nm)[0], _unpack_small(nv)[0]
    for n in replicated:
        grads[n], delta[n], new_m[n], new_v[n] = small_g[n], d[n], nm[n], nv[n]

    return (loss, grad_x, *[grads[n] for n in WEIGHT_ORDER], *[delta[n] for n in WEIGHT_ORDER],
            *[new_m[n] for n in WEIGHT_ORDER], *[new_v[n] for n in WEIGHT_ORDER])
```

```python
import functools
import math

import numpy as np
import jax
import jax.numpy as jnp
from jax import lax
from jax.experimental import pallas as pl
from jax.experimental.pallas import tpu as pltpu

F32 = jnp.float32
BF16 = jnp.bfloat16

D_MODEL = 1024
SEQ_LEN = 2048
HEAD_DIM = 64
ATTN_W = 384
CONV_W = 256
CONV_TAPS = 31
LRU_W = 384
LRU_TAPS = 4
LRU_HEADS = 6
LRU_C = 8.0
QKV_W = 3 * ATTN_W
CONV_IN_W = 2 * CONV_W
LRU_IN_W = 2 * LRU_W
IN_COLS = QKV_W + CONV_IN_W + LRU_IN_W
D_FF = 4096
DEPTH = 2
N_DEV = 8
RMS_EPS = 1e-6
LN_EPS = 1e-5
ATTN_BLOCK = 128
ATTN_DILATIONS = (1, 4, 16)
N_UNITS = 16
UNIT_UNROLL = 4
NEG_BIG = -1e30

ADAM_LR = 0.001
ADAM_B1 = 0.9
ADAM_B2 = 0.999
ADAM_EPS = 1e-08
ADAM_WD = 0.01
ADAM_STEP = 10

VMEM_LIMIT = 56 * 1024 * 1024
ROW_TILE = 512
GRAD_ROW_TILE = 1024
CONV_FWD_CHUNK = 128
CONV_BWD_CHUNK = 128
LRU_FWD_CHUNK = 128
LRU_BWD_CHUNK = 128


def _params(*sem):
    return pltpu.CompilerParams(dimension_semantics=sem if sem else None, vmem_limit_bytes=VMEM_LIMIT)


def _resident(shape):
    return pl.BlockSpec(shape, lambda *_: (0,) * len(shape), pipeline_mode=pl.Buffered(1))


def _dot(a, b):
    return jnp.dot(a, b, preferred_element_type=F32)


def _dot_nt(a, b):
    return lax.dot_general(a, b, (((1,), (1,)), ((), ())), preferred_element_type=F32)


def _dot_tn(a, b):
    return lax.dot_general(a, b, (((0,), (0,)), ((), ())), preferred_element_type=F32)


def _rms_fwd(x, g):
    rstd = lax.rsqrt(jnp.mean(x * x, axis=-1, keepdims=True) + RMS_EPS)
    xhat = x * rstd
    return xhat * g, xhat, rstd


def _rms_bwd(dh, xhat, rstd, g):
    dxh = dh * g
    dx = rstd * (dxh - xhat * jnp.mean(dxh * xhat, axis=-1, keepdims=True))
    dg = jnp.sum(dh * xhat, axis=0, keepdims=True)
    return dx, dg


def _sigmoid(x):
    return 0.5 * jnp.tanh(0.5 * x) + 0.5


def _one_minus_exp(x, exp_x):
    small = -x * (1.0 + x * (0.5 + x * (1.0 / 6.0)))
    return jnp.where(x > -0.01, small, 1.0 - exp_x)


def _log1p(z):
    w = 1.0 + z
    return jnp.where(w == 1.0, z, z * jnp.log(w) / jnp.where(w == 1.0, 1.0, w - 1.0))


def _softplus(x):
    return jnp.maximum(x, 0.0) + _log1p(jnp.exp(-jnp.abs(x)))


GELU_K = math.sqrt(2.0 / math.pi)


def _gelu(x):
    t = jnp.tanh(GELU_K * (x + 0.044715 * x * x * x))
    return 0.5 * x * (1.0 + t), t


def _gelu_grad(x, t):
    return 0.5 * (1.0 + t) + 0.5 * x * (1.0 - t * t) * GELU_K * (1.0 + 3.0 * 0.044715 * x * x)


def _inproj(x2d, g, w, ex=None):
    T = x2d.shape[0]
    tm = ROW_TILE

    def body(x_ref, g_ref, w_ref, h_ref, qkv_ref, ci_ref, li_ref):
        h, _, _ = _rms_fwd(x_ref[...], g_ref[...])
        hb = h.astype(BF16)
        h_ref[...] = hb
        qkv_ref[...] = _dot(hb, w_ref[:, 0:QKV_W])
        ci_ref[...] = _dot(hb, w_ref[:, QKV_W:QKV_W + CONV_IN_W])
        li_ref[...] = _dot(hb, w_ref[:, QKV_W + CONV_IN_W:IN_COLS])

    return _pallas_hosting(
        body, ex, name="inproj", grid=(T // tm,),
        in_specs=[pl.BlockSpec((tm, D_MODEL), lambda i: (i, 0)),
                  pl.BlockSpec((1, D_MODEL), lambda i: (0, 0)),
                  _resident((D_MODEL, IN_COLS))],
        out_specs=[pl.BlockSpec((tm, D_MODEL), lambda i: (i, 0)),
                   pl.BlockSpec((tm, QKV_W), lambda i: (i, 0)),
                   pl.BlockSpec((tm, CONV_IN_W), lambda i: (i, 0)),
                   pl.BlockSpec((tm, LRU_IN_W), lambda i: (i, 0))],
        out_shape=[jax.ShapeDtypeStruct((T, D_MODEL), BF16), jax.ShapeDtypeStruct((T, QKV_W), F32),
                   jax.ShapeDtypeStruct((T, CONV_IN_W), F32), jax.ShapeDtypeStruct((T, LRU_IN_W), F32)],
        scratch_shapes=[], semantics=("parallel",), operands=(x2d, g, w))


def _attn_alpha(lse):
    l0, l1, l2 = lse[:, 0:128], lse[:, 128:256], lse[:, 256:384]
    m = jnp.maximum(jnp.maximum(l0, l1), l2)
    e0, e1, e2 = jnp.exp(l0 - m), jnp.exp(l1 - m), jnp.exp(l2 - m)
    inv = 1.0 / (e0 + e1 + e2)
    return e0 * inv, e1 * inv, e2 * inv


def _outproj(x2d, o, lse, yc, yl, w, ex=None):
    T = x2d.shape[0]
    tm = ROW_TILE

    def body(x_ref, o_ref, lse_ref, yc_ref, yl_ref, w_ref, x1_ref, mix_ref):
        al = _attn_alpha(lse_ref[...])
        for p in range(3):
            mix_ref[:, p * 128:(p + 1) * 128] = (o_ref[:, p * 128:(p + 1) * 128] * al[p]).astype(BF16)
        mix_ref[:, ATTN_W:ATTN_W + CONV_W] = yc_ref[...]
        mix_ref[:, ATTN_W + CONV_W:D_MODEL] = yl_ref[...]
        x1_ref[...] = x_ref[...] + _dot(mix_ref[...], w_ref[...])

    return _pallas_hosting(
        body, ex, name="outproj", grid=(T // tm,),
        in_specs=[pl.BlockSpec((tm, D_MODEL), lambda i: (i, 0)),
                  pl.BlockSpec((tm, ATTN_W), lambda i: (i, 0)),
                  pl.BlockSpec((tm, ATTN_W), lambda i: (i, 0)),
                  pl.BlockSpec((tm, CONV_W), lambda i: (i, 0)),
                  pl.BlockSpec((tm, LRU_W), lambda i: (i, 0)),
                  _resident((D_MODEL, D_MODEL))],
        out_specs=[pl.BlockSpec((tm, D_MODEL), lambda i: (i, 0)),
                   pl.BlockSpec((tm, D_MODEL), lambda i: (i, 0))],
        out_shape=[jax.ShapeDtypeStruct((T, D_MODEL), F32), jax.ShapeDtypeStruct((T, D_MODEL), BF16)],
        scratch_shapes=[], semantics=("parallel",), operands=(x2d, o, lse, yc, yl, w))


FF_CHUNK = 1024


def _up(x1, g, w, ex=None):
    T = x1.shape[0]
    tm = ROW_TILE

    def body(x_ref, g_ref, w_ref, h_ref, r_ref):
        h, _, _ = _rms_fwd(x_ref[...], g_ref[...])
        hb = h.astype(BF16)
        h_ref[...] = hb
        for c in range(0, D_FF, FF_CHUNK):
            r_ref[:, c:c + FF_CHUNK] = jnp.maximum(_dot(hb, w_ref[:, c:c + FF_CHUNK]), 0.0).astype(BF16)

    return _pallas_hosting(
        body, ex, name="up", grid=(T // tm,),
        in_specs=[pl.BlockSpec((tm, D_MODEL), lambda i: (i, 0)),
                  pl.BlockSpec((1, D_MODEL), lambda i: (0, 0)),
                  _resident((D_MODEL, D_FF))],
        out_specs=[pl.BlockSpec((tm, D_MODEL), lambda i: (i, 0)),
                   pl.BlockSpec((tm, D_FF), lambda i: (i, 0))],
        out_shape=[jax.ShapeDtypeStruct((T, D_MODEL), BF16), jax.ShapeDtypeStruct((T, D_FF), BF16)],
        scratch_shapes=[], semantics=("parallel",), operands=(x1, g, w))


def _square_bf16(r):
    rf = r.astype(F32)
    return (rf * rf).astype(BF16)


def _down(x1, r, w, ex=None):
    T = x1.shape[0]
    tm = ROW_TILE

    def body(x_ref, r_ref, w_ref, o_ref):
        acc = x_ref[...]
        for c in range(0, D_FF, FF_CHUNK):
            acc = acc + _dot(_square_bf16(r_ref[:, c:c + FF_CHUNK]), w_ref[c:c + FF_CHUNK, :])
        o_ref[...] = acc

    return _pallas_hosting(
        body, ex, name="down", grid=(T // tm,),
        in_specs=[pl.BlockSpec((tm, D_MODEL), lambda i: (i, 0)),
                  pl.BlockSpec((tm, D_FF), lambda i: (i, 0)),
                  _resident((D_FF, D_MODEL))],
        out_specs=[pl.BlockSpec((tm, D_MODEL), lambda i: (i, 0))],
        out_shape=[jax.ShapeDtypeStruct((T, D_MODEL), F32)],
        scratch_shapes=[], semantics=("parallel",), operands=(x1, r, w))


def _loss_head(x2, target, g):
    T = x2.shape[0]
    tm = ROW_TILE

    def body(x_ref, t_ref, g_ref, dx_ref, loss_ref, dg_ref):
        @pl.when(pl.program_id(0) == 0)
        def _():
            loss_ref[...] = jnp.zeros_like(loss_ref)
            dg_ref[...] = jnp.zeros_like(dg_ref)

        gv = g_ref[...]
        y, xhat, rstd = _rms_fwd(x_ref[...], gv)
        err = y - t_ref[...]
        loss_ref[...] += 0.5 * jnp.sum(jnp.mean(err * err, axis=-1, keepdims=True))
        dy = err * (1.0 / D_MODEL)
        dx, dg = _rms_bwd(dy, xhat, rstd, gv)
        dx_ref[...] = dx
        dg_ref[...] += dg

    return pl.pallas_call(
        body, name="loss_head", grid=(T // tm,),
        in_specs=[pl.BlockSpec((tm, D_MODEL), lambda i: (i, 0)),
                  pl.BlockSpec((tm, D_MODEL), lambda i: (i, 0)),
                  pl.BlockSpec((1, D_MODEL), lambda i: (0, 0))],
        out_specs=[pl.BlockSpec((tm, D_MODEL), lambda i: (i, 0)),
                   pl.BlockSpec((8, 128), lambda i: (0, 0)),
                   pl.BlockSpec((1, D_MODEL), lambda i: (0, 0))],
        out_shape=[jax.ShapeDtypeStruct((T, D_MODEL), F32), jax.ShapeDtypeStruct((8, 128), F32),
                   jax.ShapeDtypeStruct((1, D_MODEL), F32)],
        compiler_params=_params("arbitrary"),
    )(x2, target, g)


def _down_bwd_act(dx2, r, w, ex=None):
    T = dx2.shape[0]
    tm = ROW_TILE

    def body(dx_ref, r_ref, w_ref, o_ref):
        dxb = dx_ref[...].astype(BF16)
        for c in range(0, D_FF, FF_CHUNK):
            dff = _dot_nt(dxb, w_ref[c:c + FF_CHUNK, :])
            o_ref[:, c:c + FF_CHUNK] = (dff * (2.0 * r_ref[:, c:c + FF_CHUNK].astype(F32))).astype(BF16)

    return _pallas_hosting(
        body, ex, name="down_bwd_act", grid=(T // tm,),
        in_specs=[pl.BlockSpec((tm, D_MODEL), lambda i: (i, 0)),
                  pl.BlockSpec((tm, D_FF), lambda i: (i, 0)),
                  _resident((D_FF, D_MODEL))],
        out_specs=[pl.BlockSpec((tm, D_FF), lambda i: (i, 0))],
        out_shape=[jax.ShapeDtypeStruct((T, D_FF), BF16)],
        scratch_shapes=[], semantics=("parallel",), operands=(dx2, r, w))


def _down_bwd_w(r, dx2):
    T = dx2.shape[0]
    tk = GRAD_ROW_TILE
    nk = T // tk

    def body(r_ref, dx_ref, o_ref, acc_ref):
        k = pl.program_id(0)
        dxb = dx_ref[...].astype(BF16)

        @pl.when(k == 0)
        def _():
            acc_ref[...] = jnp.zeros_like(acc_ref)

        for c in range(0, D_FF, FF_CHUNK):
            acc_ref[:, c:c + FF_CHUNK] += _dot_tn(dxb, _square_bf16(r_ref[:, c:c + FF_CHUNK]))

        @pl.when(k == nk - 1)
        def _():
            o_ref[...] = acc_ref[...].astype(BF16)

    return pl.pallas_call(
        body, name="down_bwd_w", grid=(nk,),
        in_specs=[pl.BlockSpec((tk, D_FF), lambda k: (k, 0)),
                  pl.BlockSpec((tk, D_MODEL), lambda k: (k, 0))],
        out_specs=_resident((D_MODEL, D_FF)),
        out_shape=jax.ShapeDtypeStruct((D_MODEL, D_FF), BF16),
        scratch_shapes=[pltpu.VMEM((D_MODEL, D_FF), F32)],
        compiler_params=_params("arbitrary"),
    )(r, dx2)


def _up_bwd_act(dpre, w, x1, g, dx2, ex=None):
    T = dx2.shape[0]
    tm = ROW_TILE

    def body(dp_ref, w_ref, x_ref, g_ref, dx2_ref, dx1_ref, dg_ref):
        dh = _dot_nt(dp_ref[:, 0:FF_CHUNK], w_ref[:, 0:FF_CHUNK])
        for c in range(FF_CHUNK, D_FF, FF_CHUNK):
            dh = dh + _dot_nt(dp_ref[:, c:c + FF_CHUNK], w_ref[:, c:c + FF_CHUNK])
        gv = g_ref[...]
        _, xhat, rstd = _rms_fwd(x_ref[...], gv)
        dx, dg = _rms_bwd(dh, xhat, rstd, gv)
        dx1_ref[...] = dx2_ref[...] + dx

        @pl.when(pl.program_id(0) == 0)
        def _():
            dg_ref[...] = dg

        @pl.when(pl.program_id(0) != 0)
        def _():
            dg_ref[...] += dg

    return _pallas_hosting(
        body, ex, name="up_bwd_act", grid=(T // tm,),
        in_specs=[pl.BlockSpec((tm, D_FF), lambda i: (i, 0)),
                  _resident((D_MODEL, D_FF)),
                  pl.BlockSpec((tm, D_MODEL), lambda i: (i, 0)),
                  pl.BlockSpec((1, D_MODEL), lambda i: (0, 0)),
                  pl.BlockSpec((tm, D_MODEL), lambda i: (i, 0))],
        out_specs=[pl.BlockSpec((tm, D_MODEL), lambda i: (i, 0)),
                   pl.BlockSpec((1, D_MODEL), lambda i: (0, 0))],
        out_shape=[jax.ShapeDtypeStruct((T, D_MODEL), F32), jax.ShapeDtypeStruct((1, D_MODEL), F32)],
        scratch_shapes=[], semantics=("arbitrary",), operands=(dpre, w, x1, g, dx2))


def _up_bwd_w(h2, dpre):
    T = h2.shape[0]
    tk = GRAD_ROW_TILE
    nk = T // tk

    def body(h_ref, dp_ref, o_ref, acc_ref):
        k = pl.program_id(0)
        hb = h_ref[...]

        @pl.when(k == 0)
        def _():
            acc_ref[...] = jnp.zeros_like(acc_ref)

        for c in range(0, D_FF, FF_CHUNK):
            acc_ref[:, c:c + FF_CHUNK] += _dot_tn(hb, dp_ref[:, c:c + FF_CHUNK])

        @pl.when(k == nk - 1)
        def _():
            o_ref[...] = acc_ref[...].astype(BF16)

    return pl.pallas_call(
        body, name="up_bwd_w", grid=(nk,),
        in_specs=[pl.BlockSpec((tk, D_MODEL), lambda k: (k, 0)),
                  pl.BlockSpec((tk, D_FF), lambda k: (k, 0))],
        out_specs=_resident((D_MODEL, D_FF)),
        out_shape=jax.ShapeDtypeStruct((D_MODEL, D_FF), BF16),
        scratch_shapes=[pltpu.VMEM((D_MODEL, D_FF), F32)],
        compiler_params=_params("arbitrary"),
    )(h2, dpre)


def _outproj_bwd(dx1, mix, w, o, lse):
    T = dx1.shape[0]
    tm = ROW_TILE
    nk = T // tm

    def body(dx_ref, mix_ref, w_ref, o_ref, lse_ref, do_ref, dd_ref, dc_ref, dl_ref, dw_ref, acc_ref):
        i = pl.program_id(0)
        dxb = dx_ref[...].astype(BF16)
        dmix = _dot_nt(dxb, w_ref[...])
        al = _attn_alpha(lse_ref[...])
        first = lax.broadcasted_iota(jnp.int32, (tm, 128), 1) < HEAD_DIM
        tot = jnp.zeros((tm, 128), F32)
        for p in range(3):
            sl = slice(p * 128, (p + 1) * 128)
            dy = dmix[:, sl]
            do_ref[:, sl] = dy * al[p]
            prod = dy * o_ref[:, sl]
            s0 = jnp.sum(jnp.where(first, prod, 0.0), axis=-1, keepdims=True)
            s1 = jnp.sum(jnp.where(first, 0.0, prod), axis=-1, keepdims=True)
            tot = tot + al[p] * jnp.where(first, s0, s1)
        for p in range(3):
            dd_ref[:, p * 128:(p + 1) * 128] = -al[p] * tot
        dc_ref[...] = dmix[:, ATTN_W:ATTN_W + CONV_W]
        dl_ref[...] = dmix[:, ATTN_W + CONV_W:D_MODEL]
        part = _dot_tn(mix_ref[...], dxb)

        @pl.when(i == 0)
        def _():
            acc_ref[...] = part

        @pl.when(i != 0)
        def _():
            acc_ref[...] += part

        @pl.when(i == nk - 1)
        def _():
            dw_ref[...] = acc_ref[...].astype(BF16)

    return pl.pallas_call(
        body, name="outproj_bwd", grid=(nk,),
        in_specs=[pl.BlockSpec((tm, D_MODEL), lambda i: (i, 0)),
                  pl.BlockSpec((tm, D_MODEL), lambda i: (i, 0)),
                  _resident((D_MODEL, D_MODEL)),
                  pl.BlockSpec((tm, ATTN_W), lambda i: (i, 0)),
                  pl.BlockSpec((tm, ATTN_W), lambda i: (i, 0))],
        out_specs=[pl.BlockSpec((tm, ATTN_W), lambda i: (i, 0)),
                   pl.BlockSpec((tm, ATTN_W), lambda i: (i, 0)),
                   pl.BlockSpec((tm, CONV_W), lambda i: (i, 0)),
                   pl.BlockSpec((tm, LRU_W), lambda i: (i, 0)),
                   _resident((D_MODEL, D_MODEL))],
        out_shape=[jax.ShapeDtypeStruct((T, ATTN_W), F32), jax.ShapeDtypeStruct((T, ATTN_W), F32),
                   jax.ShapeDtypeStruct((T, CONV_W), F32), jax.ShapeDtypeStruct((T, LRU_W), F32),
                   jax.ShapeDtypeStruct((D_MODEL, D_MODEL), BF16)],
        scratch_shapes=[pltpu.VMEM((D_MODEL, D_MODEL), F32)],
        compiler_params=_params("arbitrary"),
    )(dx1, mix, w, o, lse)


DZ_COLS = ((0, ATTN_W), (ATTN_W, 2 * ATTN_W), (2 * ATTN_W, QKV_W), (QKV_W, QKV_W + CONV_IN_W), (QKV_W + CONV_IN_W, IN_COLS))


def _inproj_bwd_w(dz_parts, h):
    T = h.shape[0]
    tm = GRAD_ROW_TILE
    nk = T // tm
    n_parts = len(DZ_COLS)

    def body(*refs):
        dz_refs = refs[:n_parts]
        h_ref, dw_ref, acc_ref = refs[n_parts:]
        i = pl.program_id(0)
        hb = h_ref[...]

        @pl.when(i == 0)
        def _():
            acc_ref[...] = jnp.zeros_like(acc_ref)

        for r, (lo, hi) in zip(dz_refs, DZ_COLS):
            acc_ref[:, lo:hi] += _dot_tn(hb, r[...].astype(BF16))

        @pl.when(i == nk - 1)
        def _():
            dw_ref[...] = acc_ref[...].astype(BF16)

    rows = lambda width: pl.BlockSpec((tm, width), lambda i: (i, 0))
    return pl.pallas_call(
        body, name="inproj_bwd_w", grid=(nk,),
        in_specs=[rows(hi - lo) for lo, hi in DZ_COLS] + [rows(D_MODEL)],
        out_specs=_resident((D_MODEL, IN_COLS)),
        out_shape=jax.ShapeDtypeStruct((D_MODEL, IN_COLS), BF16),
        scratch_shapes=[pltpu.VMEM((D_MODEL, IN_COLS), F32)],
        compiler_params=_params("arbitrary"),
    )(*dz_parts, h)


def _inproj_bwd_act(dz_parts, w, x2d, g, dx1, ex=None):
    T = x2d.shape[0]
    tm = ROW_TILE
    n_parts = len(DZ_COLS)

    def body(*refs):
        dz_refs = refs[:n_parts]
        w_ref, x_ref, g_ref, dx1_ref, dx_ref, dg_ref = refs[n_parts:]
        dh = _dot_nt(dz_refs[0][...].astype(BF16), w_ref[:, DZ_COLS[0][0]:DZ_COLS[0][1]])
        for r, (lo, hi) in zip(dz_refs[1:], DZ_COLS[1:]):
            dh = dh + _dot_nt(r[...].astype(BF16), w_ref[:, lo:hi])
        gv = g_ref[...]
        _, xhat, rstd = _rms_fwd(x_ref[...], gv)
        dx, dg = _rms_bwd(dh, xhat, rstd, gv)
        dx_ref[...] = dx1_ref[...] + dx

        @pl.when(pl.program_id(0) == 0)
        def _():
            dg_ref[...] = dg

        @pl.when(pl.program_id(0) != 0)
        def _():
            dg_ref[...] += dg

    rows = lambda width: pl.BlockSpec((tm, width), lambda i: (i, 0))
    return _pallas_hosting(
        body, ex, name="inproj_bwd_act", grid=(T // tm,),
        in_specs=[rows(hi - lo) for lo, hi in DZ_COLS] + [
            _resident((D_MODEL, IN_COLS)), rows(D_MODEL), pl.BlockSpec((1, D_MODEL), lambda i: (0, 0)), rows(D_MODEL)],
        out_specs=[rows(D_MODEL), pl.BlockSpec((1, D_MODEL), lambda i: (0, 0))],
        out_shape=[jax.ShapeDtypeStruct((T, D_MODEL), F32), jax.ShapeDtypeStruct((1, D_MODEL), F32)],
        scratch_shapes=[], semantics=("arbitrary",), operands=(*dz_parts, w, x2d, g, dx1))


def _alibi_coef():
    slopes = 2.0 ** (-8.0 * np.arange(1, 7) / 6)
    return jnp.asarray((slopes.reshape(3, 2) * np.asarray(ATTN_DILATIONS)[:, None]).astype(np.float32))


def _unit_rows(u, d):
    nb = N_UNITS // d
    r, n = u // nb, u % nb
    span = ATTN_BLOCK * d

    def rows(block):
        start = block * span + r
        return pl.ds(pl.multiple_of(start, ATTN_BLOCK), ATTN_BLOCK) if d == 1 else pl.ds(start, ATTN_BLOCK, stride=d)

    return rows(n), rows(jnp.maximum(n - 1, 0)), rows(jnp.minimum(n + 1, nb - 1)), n > 0, n + 1 < nb


def _per_pattern(fn):
    for p, d in enumerate(ATTN_DILATIONS):
        pl.when(pl.program_id(1) == p)(functools.partial(fn, p, d))


def _attn_col(offset):
    return pl.BlockSpec((None, SEQ_LEN, 128), lambda b, p: (b, 0, p + offset))


def _attn_masks():
    qi = lax.broadcasted_iota(jnp.int32, (ATTN_BLOCK, 2 * ATTN_BLOCK), 0)
    kj = lax.broadcasted_iota(jnp.int32, (ATTN_BLOCK, 2 * ATTN_BLOCK), 1)
    dist = qi + ATTN_BLOCK - kj
    first = lax.broadcasted_iota(jnp.int32, (ATTN_BLOCK, 128), 1) < HEAD_DIM
    return dist.astype(F32), (dist >= 0) & (dist <= ATTN_BLOCK), kj >= ATTN_BLOCK, first


def _head_lanes(a, first, j):
    return jnp.where(first if j == 0 else jnp.logical_not(first), a, jnp.zeros_like(a))


def _load_kv(ref, prev, own):
    return jnp.concatenate([ref[prev, :], ref[own, :]], axis=0).astype(BF16)


def _attn_fwd(qkv, ex=None):
    B = qkv.shape[0]

    def body(coef_ref, q_ref, k_ref, v_ref, o_ref, lse_ref):
        dist, band, own_half, first = _attn_masks()

        def pattern(p, d):
            def unit(u, carry):
                own, prev, _, has_prev, _ = _unit_rows(u, d)
                ok = band & jnp.logical_or(own_half, has_prev)
                q = q_ref[own, :].astype(BF16)
                kcat, vcat = _load_kv(k_ref, prev, own), _load_kv(v_ref, prev, own)
                outs, lses = [], []
                for j in range(2):
                    s = jnp.where(ok, _dot_nt(_head_lanes(q, first, j), kcat) * 0.125 - coef_ref[p, j] * dist, NEG_BIG)
                    m = jnp.max(s, axis=-1, keepdims=True)
                    e = jnp.exp(s - m)
                    l = jnp.sum(e, axis=-1, keepdims=True)
                    outs.append(_dot(e.astype(BF16), vcat) * (1.0 / l))
                    lses.append(m + jnp.log(l))
                o_ref[own, :] = jnp.where(first, outs[0], outs[1])
                lse_ref[own, :] = jnp.where(first, lses[0], lses[1])
                return carry

            lax.fori_loop(0, N_UNITS, unit, 0, unroll=UNIT_UNROLL)

        _per_pattern(pattern)

    shape = jax.ShapeDtypeStruct((B, SEQ_LEN, ATTN_W), F32)
    return _pallas_hosting(
        body, ex, name="attn_fwd", grid=(B, 3),
        in_specs=[pl.BlockSpec(memory_space=pltpu.SMEM), _attn_col(0), _attn_col(3), _attn_col(6)],
        out_specs=[_attn_col(0), _attn_col(0)],
        out_shape=[shape, shape],
        scratch_shapes=[], semantics=("parallel", "parallel"), operands=(_alibi_coef(), qkv, qkv, qkv))


def _attn_bwd(qkv, do, lse, dd, ex=None):
    B = qkv.shape[0]

    def body(coef_ref, q_ref, k_ref, v_ref, do_ref, lse_ref, dd_ref, dq_ref, dk_ref, dv_ref):
        dist, band, own_half, first = _attn_masks()

        def pattern(p, d):
            def unit(u, carry):
                own, prev, _, has_prev, _ = _unit_rows(u, d)
                ok = band & jnp.logical_or(own_half, has_prev)
                q, do = q_ref[own, :].astype(BF16), do_ref[own, :].astype(BF16)
                kcat, vcat = _load_kv(k_ref, prev, own), _load_kv(v_ref, prev, own)
                lse_a, dd_a = lse_ref[own, :], dd_ref[own, :]
                dqs, dks, dvs = [], [], []
                for j in range(2):
                    col = slice(HEAD_DIM * j, HEAD_DIM * j + 1)
                    s = _dot_nt(_head_lanes(q, first, j), kcat) * 0.125 - coef_ref[p, j] * dist
                    pr = jnp.where(ok, jnp.exp(jnp.where(ok, s, NEG_BIG) - lse_a[:, col]), 0.0)
                    ds = (pr * (_dot_nt(_head_lanes(do, first, j), vcat) + dd_a[:, col])).astype(BF16)
                    dqs.append(_dot(ds, kcat))
                    dks.append(_dot_tn(ds, q))
                    dvs.append(_dot_tn(pr.astype(BF16), do))
                both = lambda pair: jnp.where(jnp.concatenate([first] * (pair[0].shape[0] // ATTN_BLOCK), axis=0), *pair)
                dq_ref[own, :] = both(dqs) * 0.125
                dk, dv = both(dks) * 0.125, both(dvs)
                dk_ref[own, :] = dk[ATTN_BLOCK:]
                dv_ref[own, :] = dv[ATTN_BLOCK:]
                dk_ref[prev, :] += dk[:ATTN_BLOCK]
                dv_ref[prev, :] += dv[:ATTN_BLOCK]
                return carry

            lax.fori_loop(0, N_UNITS, unit, 0, unroll=UNIT_UNROLL)

        _per_pattern(pattern)

    shape = jax.ShapeDtypeStruct((B, SEQ_LEN, ATTN_W), F32)
    return _pallas_hosting(
        body, ex, name="attn_bwd", grid=(B, 3),
        in_specs=[pl.BlockSpec(memory_space=pltpu.SMEM), _attn_col(0), _attn_col(3), _attn_col(6), _attn_col(0), _attn_col(0),
                  _attn_col(0)],
        out_specs=[_attn_col(0)] * 3,
        out_shape=[shape] * 3,
        scratch_shapes=[], semantics=("parallel", "parallel"), operands=(_alibi_coef(), qkv, qkv, qkv, do, lse, dd))


def _for_chunks(n_rows, fn, chunk):
    def step(c, carry):
        fn(pl.multiple_of(c * chunk, chunk))
        return carry

    lax.fori_loop(0, n_rows // chunk, step, 0)


def _shift_down(win, s, rows):
    lead = win.shape[0] - rows
    if s == 0:
        return win[lead:]
    if s % 8 == 0:
        return win[lead - s:lead - s + rows]
    q, r = divmod(s, 8)
    rolled = pltpu.roll(win, r, 0)
    return rolled[lead - 8 * q:lead - 8 * q + rows]


def _tap_shifts(n_taps):
    return [(r, [(n_taps - 1 - (8 * q + r), 8 * q) for q in range((n_taps - 1 - r) // 8 + 1)]) for r in range(min(8, n_taps))]


def _rotated_down(win, r):
    return win if r == 0 else pltpu.roll(win, r, 0)


def _rotated_up(win, r):
    return win if r == 0 else pltpu.roll(win, win.shape[0] - r, 0)


def _shift_up(win, s, rows):
    if s % 8 == 0:
        return win[s:s + rows]
    q, r = divmod(s, 8)
    rolled = pltpu.roll(win, win.shape[0] - r, 0)
    return rolled[8 * q:8 * q + rows]


CONV_PAD = 32


def _ln_silu(c, lg, lb):
    mu = jnp.mean(c, axis=-1, keepdims=True)
    cc = c - mu
    rstd = lax.rsqrt(jnp.mean(cc * cc, axis=-1, keepdims=True) + LN_EPS)
    nrm = cc * rstd
    v = nrm * lg + lb
    sg = _sigmoid(v)
    return v * sg, nrm, rstd, v, sg


def _conv_fwd(ci, w, b, lg, lb):
    B, S, _ = ci.shape
    CH = CONV_FWD_CHUNK

    def body(ci_ref, w_ref, b_ref, lg_ref, lb_ref, y_ref, c_ref, pad_ref):
        pad_ref[0:CONV_PAD, :] = jnp.zeros((CONV_PAD, CONV_W), F32)

        def glu(base):
            blk = ci_ref[pl.ds(base, CH), :]
            pad_ref[pl.ds(CONV_PAD + base, CH), :] = blk[:, 0:CONV_W] * _sigmoid(blk[:, CONV_W:])

        _for_chunks(S, glu, CH)

        def conv(base):
            win = pad_ref[pl.ds(base, CH + CONV_PAD), :]
            acc = jnp.broadcast_to(b_ref[...], (CH, CONV_W))
            for r, taps in _tap_shifts(CONV_TAPS):
                rot = _rotated_down(win, r)
                for k, off in taps:
                    acc = acc + w_ref[k:k + 1, :] * rot[CONV_PAD - off:CONV_PAD - off + CH]
            c_ref[pl.ds(base, CH), :] = acc
            y, _, _, _, _ = _ln_silu(acc, lg_ref[...], lb_ref[...])
            y_ref[pl.ds(base, CH), :] = y.astype(BF16)

        _for_chunks(S, conv, CH)

    vec = pl.BlockSpec((1, CONV_W), lambda i: (0, 0))
    return pl.pallas_call(
        body, name="conv_fwd", grid=(B,),
        in_specs=[pl.BlockSpec((None, S, CONV_IN_W), lambda i: (i, 0, 0)),
                  pl.BlockSpec((CONV_TAPS, CONV_W), lambda i: (0, 0)), vec, vec, vec],
        out_specs=[pl.BlockSpec((None, S, CONV_W), lambda i: (i, 0, 0)),
                   pl.BlockSpec((None, S, CONV_W), lambda i: (i, 0, 0))],
        out_shape=[jax.ShapeDtypeStruct((B, S, CONV_W), BF16), jax.ShapeDtypeStruct((B, S, CONV_W), F32)],
        scratch_shapes=[pltpu.VMEM((S + CONV_PAD, CONV_W), F32)],
        compiler_params=_params("parallel"),
    )(ci, w, b, lg, lb)


def _conv_bwd(ci, cpre, dy, w, lg, lb, ex=None):
    B, S, _ = ci.shape
    CH = CONV_BWD_CHUNK

    def body(ci_ref, c_ref, dy_ref, w_ref, lg_ref, lb_ref, dci_ref, dw_ref, db_ref, dlg_ref, dlb_ref, upad_ref, dcpad_ref,
             dwacc_ref):
        @pl.when(pl.program_id(0) == 0)
        def _():
            dw_ref[...] = jnp.zeros_like(dw_ref)
            db_ref[...] = jnp.zeros_like(db_ref)
            dlg_ref[...] = jnp.zeros_like(dlg_ref)
            dlb_ref[...] = jnp.zeros_like(dlb_ref)

        upad_ref[0:CONV_PAD, :] = jnp.zeros((CONV_PAD, CONV_W), F32)
        dcpad_ref[S:S + CONV_PAD, :] = jnp.zeros((CONV_PAD, CONV_W), F32)
        dwacc_ref[...] = jnp.zeros_like(dwacc_ref)

        def norm_bwd(base):
            blk = ci_ref[pl.ds(base, CH), :]
            upad_ref[pl.ds(CONV_PAD + base, CH), :] = blk[:, 0:CONV_W] * _sigmoid(blk[:, CONV_W:])
            lgv = lg_ref[...]
            _, nrm, rstd, v, sg = _ln_silu(c_ref[pl.ds(base, CH), :], lgv, lb_ref[...])
            dv = dy_ref[pl.ds(base, CH), :] * (sg * (1.0 + v * (1.0 - sg)))
            dlg_ref[...] += jnp.sum(dv * nrm, axis=0, keepdims=True)
            dlb_ref[...] += jnp.sum(dv, axis=0, keepdims=True)
            dn = dv * lgv
            dc = rstd * (dn - jnp.mean(dn, axis=-1, keepdims=True) - nrm * jnp.mean(dn * nrm, axis=-1, keepdims=True))
            dcpad_ref[pl.ds(base, CH), :] = dc
            db_ref[...] += jnp.sum(dc, axis=0, keepdims=True)

        _for_chunks(S, norm_bwd, CH)

        def conv_bwd(base):
            dwin = dcpad_ref[pl.ds(base, CH + CONV_PAD), :]
            uwin = upad_ref[pl.ds(base, CH + CONV_PAD), :]
            dc = dwin[0:CH]
            du = jnp.zeros((CH, CONV_W), F32)
            for r, taps in _tap_shifts(CONV_TAPS):
                d_rot, u_rot = _rotated_up(dwin, r), _rotated_down(uwin, r)
                for k, off in taps:
                    du = du + w_ref[k:k + 1, :] * d_rot[off:off + CH]
                    prod = dc * u_rot[CONV_PAD - off:CONV_PAD - off + CH]
                    dwacc_ref[8 * k:8 * k + 8, :] += jnp.sum(prod.reshape(CH // 8, 8, CONV_W), axis=0)
            blk = ci_ref[pl.ds(base, CH), :]
            a, sg = blk[:, 0:CONV_W], _sigmoid(blk[:, CONV_W:])
            dci_ref[pl.ds(base, CH), 0:CONV_W] = (du * sg).astype(BF16)
            dci_ref[pl.ds(base, CH), CONV_W:] = (du * a * sg * (1.0 - sg)).astype(BF16)

        _for_chunks(S, conv_bwd, CH)
        for k in range(CONV_TAPS):
            dw_ref[k:k + 1, :] += jnp.sum(dwacc_ref[8 * k:8 * k + 8, :], axis=0, keepdims=True)

    vec = pl.BlockSpec((1, CONV_W), lambda i: (0, 0))
    mat = pl.BlockSpec((CONV_TAPS, CONV_W), lambda i: (0, 0))
    seq = lambda width: pl.BlockSpec((None, S, width), lambda i: (i, 0, 0))
    return _pallas_hosting(
        body, ex, name="conv_bwd", grid=(B,),
        in_specs=[seq(CONV_IN_W), seq(CONV_W), seq(CONV_W), mat, vec, vec],
        out_specs=[seq(CONV_IN_W), mat, vec, vec, vec],
        out_shape=[jax.ShapeDtypeStruct((B, S, CONV_IN_W), BF16), jax.ShapeDtypeStruct((CONV_TAPS, CONV_W), F32),
                   jax.ShapeDtypeStruct((1, CONV_W), F32), jax.ShapeDtypeStruct((1, CONV_W), F32),
                   jax.ShapeDtypeStruct((1, CONV_W), F32)],
        scratch_shapes=[pltpu.VMEM((S + CONV_PAD, CONV_W), F32), pltpu.VMEM((S + CONV_PAD, CONV_W), F32),
                        pltpu.VMEM((8 * CONV_TAPS, CONV_W), F32)],
        semantics=("arbitrary",), operands=(ci, cpre, dy, w, lg, lb))


SCAN_SHIFTS = tuple(1 << e for e in range(11))


def _prev8(ref, base, cols, fill):
    start = pl.multiple_of(jnp.maximum(base - 8, 0), 8)
    return jnp.where(base > 0, ref[pl.ds(start, 8), cols], fill)


def _next8(ref, base, rows, total, cols, fill):
    start = pl.multiple_of(jnp.minimum(base + rows, total - 8), 8)
    return jnp.where(base + rows < total, ref[pl.ds(start, 8), cols], fill)


ALL = slice(None)
LRU_X = slice(LRU_W, LRU_IN_W)
LRU_GATE = slice(0, LRU_W)


def _lru_conv(li_ref, base, rows, cw_ref, cb_ref):
    win = jnp.concatenate([_prev8(li_ref, base, LRU_X, 0.0), li_ref[pl.ds(base, rows), LRU_X]], axis=0)
    u = jnp.broadcast_to(cb_ref[...], (rows, LRU_W))
    for k in range(LRU_TAPS):
        u = u + cw_ref[k:k + 1, :] * _shift_down(win, LRU_TAPS - 1 - k, rows)
    return u, win


def _lru_gates(u, wa_ref, ba_ref, wx_ref, bx_ref, sp):
    ub = u.astype(BF16)
    r = _sigmoid(_dot(ub, wa_ref[...]) + ba_ref[...])
    i = _sigmoid(_dot(ub, wx_ref[...]) + bx_ref[...])
    la = (-LRU_C) * r * sp
    a = jnp.exp(la)
    return ub, r, i, a, _one_minus_exp(2.0 * la, a * a)


def _scan_forward(bufs, S, CH):
    for n, s in enumerate(SCAN_SHIFTS):
        (sa, sb), (da, db) = bufs[n % 2], bufs[(n + 1) % 2]

        def step(base, s=s, sa=sa, sb=sb, da=da, db=db):
            a, b = sa[pl.ds(base, CH), :], sb[pl.ds(base, CH), :]
            if s < 8:
                a_s = _shift_down(jnp.concatenate([_prev8(sa, base, ALL, 1.0), a], axis=0), s, CH)
                b_s = _shift_down(jnp.concatenate([_prev8(sb, base, ALL, 0.0), b], axis=0), s, CH)
            elif s < CH:
                start = pl.multiple_of(jnp.maximum(base - s, 0), 8)
                a_s = jnp.concatenate([jnp.where(base > 0, sa[pl.ds(start, s), :], 1.0), a[0:CH - s]], axis=0)
                b_s = jnp.concatenate([jnp.where(base > 0, sb[pl.ds(start, s), :], 0.0), b[0:CH - s]], axis=0)
            else:
                start = pl.multiple_of(jnp.maximum(base - s, 0), 8)
                a_s = jnp.where(base < s, 1.0, sa[pl.ds(start, CH), :])
                b_s = jnp.where(base < s, 0.0, sb[pl.ds(start, CH), :])
            db[pl.ds(base, CH), :] = a * b_s + b
            da[pl.ds(base, CH), :] = a * a_s

        _for_chunks(S, step, CH)
    return len(SCAN_SHIFTS) % 2


def _scan_backward(bufs, S, CH):
    for n, s in enumerate(SCAN_SHIFTS):
        (sa, sb), (da, db) = bufs[n % 2], bufs[(n + 1) % 2]

        def step(base, s=s, sa=sa, sb=sb, da=da, db=db):
            a, b = sa[pl.ds(base, CH), :], sb[pl.ds(base, CH), :]
            if s < 8:
                a_s = _shift_up(jnp.concatenate([a, _next8(sa, base, CH, S, ALL, 1.0)], axis=0), s, CH)
                b_s = _shift_up(jnp.concatenate([b, _next8(sb, base, CH, S, ALL, 0.0)], axis=0), s, CH)
            elif s < CH:
                start = pl.multiple_of(jnp.minimum(base + CH, S - s), 8)
                more = base + CH < S
                a_s = jnp.concatenate([a[s:CH], jnp.where(more, sa[pl.ds(start, s), :], 1.0)], axis=0)
                b_s = jnp.concatenate([b[s:CH], jnp.where(more, sb[pl.ds(start, s), :], 0.0)], axis=0)
            else:
                start = pl.multiple_of(jnp.minimum(base + s, S - CH), 8)
                a_s = jnp.where(base + s >= S, 1.0, sa[pl.ds(start, CH), :])
                b_s = jnp.where(base + s >= S, 0.0, sb[pl.ds(start, CH), :])
            db[pl.ds(base, CH), :] = a * b_s + b
            da[pl.ds(base, CH), :] = a * a_s

        _for_chunks(S, step, CH)
    return len(SCAN_SHIFTS) % 2


def _lru_fwd(li, cw, cb, wa, ba, wx, bx, lam, ex=None):
    B, S, _ = li.shape
    CH = LRU_FWD_CHUNK

    def body(li_ref, cw_ref, cb_ref, wa_ref, ba_ref, wx_ref, bx_ref, lam_ref, y_ref, h_ref, a0, b0, a1, b1):
        sp = _softplus(-lam_ref[...])

        def gates(base):
            u, _ = _lru_conv(li_ref, base, CH, cw_ref, cb_ref)
            _, _, i, a, em = _lru_gates(u, wa_ref, ba_ref, wx_ref, bx_ref, sp)
            a0[pl.ds(base, CH), :] = a
            b0[pl.ds(base, CH), :] = jnp.sqrt(em) * (i * u)

        _for_chunks(S, gates, CH)
        bufs = ((a0, b0), (a1, b1))
        hb = bufs[_scan_forward(bufs, S, CH)][1]

        def out(base):
            h = hb[pl.ds(base, CH), :]
            h_ref[pl.ds(base, CH), :] = h
            gl, _ = _gelu(li_ref[pl.ds(base, CH), LRU_GATE])
            y_ref[pl.ds(base, CH), :] = (gl * h).astype(BF16)

        _for_chunks(S, out, CH)

    vec = pl.BlockSpec((1, LRU_W), lambda i: (0, 0))
    mat = pl.BlockSpec((LRU_W, LRU_W), lambda i: (0, 0))
    seq = lambda width: pl.BlockSpec((None, S, width), lambda i: (i, 0, 0))
    return _pallas_hosting(
        body, ex, name="lru_fwd", grid=(B,),
        in_specs=[seq(LRU_IN_W), pl.BlockSpec((LRU_TAPS, LRU_W), lambda i: (0, 0)), vec, mat, vec, mat, vec, vec],
        out_specs=[seq(LRU_W), seq(LRU_W)],
        out_shape=[jax.ShapeDtypeStruct((B, S, LRU_W), BF16), jax.ShapeDtypeStruct((B, S, LRU_W), F32)],
        scratch_shapes=[pltpu.VMEM((S, LRU_W), F32)] * 4,
        semantics=("parallel",), operands=(li, cw, cb, wa, ba, wx, bx, lam))


def _lru_bwd(li, hs, dy, cw, cb, wa, ba, wx, bx, lam, ex=None):
    B, S, _ = li.shape
    CH = LRU_BWD_CHUNK

    def body(li_ref, hs_ref, dy_ref, cw_ref, cb_ref, wa_ref, ba_ref, wx_ref, bx_ref, lam_ref,
             dli_ref, dcw_ref, dcb_ref, dwa_ref, dba_ref, dwx_ref, dbx_ref, dlam_ref, a0, b0, a1, b1, u_s, du_s):
        @pl.when(pl.program_id(0) == 0)
        def _():
            for ref in (dcw_ref, dcb_ref, dwa_ref, dba_ref, dwx_ref, dbx_ref, dlam_ref):
                ref[...] = jnp.zeros_like(ref)

        lam_v = lam_ref[...]
        sp = _softplus(-lam_v)
        dsp_dlam = -_sigmoid(-lam_v)

        def gates(base):
            u, _ = _lru_conv(li_ref, base, CH, cw_ref, cb_ref)
            _, _, _, a, _ = _lru_gates(u, wa_ref, ba_ref, wx_ref, bx_ref, sp)
            gl, _ = _gelu(li_ref[pl.ds(base, CH), LRU_GATE])
            u_s[pl.ds(base, CH), :] = u
            a0[pl.ds(base, CH), :] = a
            b0[pl.ds(base, CH), :] = a * (dy_ref[pl.ds(base, CH), :] * gl)

        _for_chunks(S, gates, CH)
        bufs = ((a0, b0), (a1, b1))
        eb = bufs[_scan_backward(bufs, S, CH)][1]

        def grads(base):
            e = eb[pl.ds(base, CH), :]
            e_next = _shift_up(jnp.concatenate([e, _next8(eb, base, CH, S, ALL, 0.0)], axis=0), 1, CH)
            gate = li_ref[pl.ds(base, CH), LRU_GATE]
            gl, th = _gelu(gate)
            dy = dy_ref[pl.ds(base, CH), :]
            g = dy * gl + e_next
            h = hs_ref[pl.ds(base, CH), :]
            h_prev = _shift_down(jnp.concatenate([_prev8(hs_ref, base, ALL, 0.0), h], axis=0), 1, CH)
            dli_ref[pl.ds(base, CH), LRU_GATE] = (dy * h * _gelu_grad(gate, th)).astype(BF16)
            u = u_s[pl.ds(base, CH), :]
            ub, r, i, a, em = _lru_gates(u, wa_ref, ba_ref, wx_ref, bx_ref, sp)
            mult = jnp.sqrt(em)
            da = g * h_prev
            dmult = g * (i * u)
            di = g * mult * u
            dla = da * a - dmult * (a * a) * lax.rsqrt(jnp.maximum(em, 1e-30))
            dlam_ref[...] += dsp_dlam * jnp.sum(dla * ((-LRU_C) * r), axis=0, keepdims=True)
            dpa = (dla * ((-LRU_C) * sp)) * r * (1.0 - r)
            dpx = di * i * (1.0 - i)
            dpab, dpxb = dpa.astype(BF16), dpx.astype(BF16)
            dwa_ref[...] += _dot_tn(ub, dpab)
            dwx_ref[...] += _dot_tn(ub, dpxb)
            dba_ref[...] += jnp.sum(dpa, axis=0, keepdims=True)
            dbx_ref[...] += jnp.sum(dpx, axis=0, keepdims=True)
            du = g * mult * i + _dot_nt(dpab, wa_ref[...]) + _dot_nt(dpxb, wx_ref[...])
            du_s[pl.ds(base, CH), :] = du
            dcb_ref[...] += jnp.sum(du, axis=0, keepdims=True)

        _for_chunks(S, grads, CH)

        def conv_bwd(base):
            du = du_s[pl.ds(base, CH), :]
            dwin = jnp.concatenate([du, _next8(du_s, base, CH, S, ALL, 0.0)], axis=0)
            xwin = jnp.concatenate([_prev8(li_ref, base, LRU_X, 0.0), li_ref[pl.ds(base, CH), LRU_X]], axis=0)
            dx = jnp.zeros((CH, LRU_W), F32)
            for k in range(LRU_TAPS):
                dx = dx + cw_ref[k:k + 1, :] * _shift_up(dwin, LRU_TAPS - 1 - k, CH)
                dcw_ref[k:k + 1, :] += jnp.sum(du * _shift_down(xwin, LRU_TAPS - 1 - k, CH), axis=0, keepdims=True)
            dli_ref[pl.ds(base, CH), LRU_X] = dx.astype(BF16)

        _for_chunks(S, conv_bwd, CH)

    vec = pl.BlockSpec((1, LRU_W), lambda i: (0, 0))
    mat = pl.BlockSpec((LRU_W, LRU_W), lambda i: (0, 0))
    taps = pl.BlockSpec((LRU_TAPS, LRU_W), lambda i: (0, 0))
    seq = lambda width: pl.BlockSpec((None, S, width), lambda i: (i, 0, 0))
    vec_shape = jax.ShapeDtypeStruct((1, LRU_W), F32)
    mat_shape = jax.ShapeDtypeStruct((LRU_W, LRU_W), F32)
    return _pallas_hosting(
        body, ex, name="lru_bwd", grid=(B,),
        in_specs=[seq(LRU_IN_W), seq(LRU_W), seq(LRU_W), taps, vec, mat, vec, mat, vec, vec],
        out_specs=[seq(LRU_IN_W), taps, vec, mat, vec, mat, vec, vec],
        out_shape=[jax.ShapeDtypeStruct((B, S, LRU_IN_W), BF16), jax.ShapeDtypeStruct((LRU_TAPS, LRU_W), F32),
                   vec_shape, mat_shape, vec_shape, mat_shape, vec_shape, vec_shape],
        scratch_shapes=[pltpu.VMEM((S, LRU_W), F32)] * 6,
        semantics=("arbitrary",), operands=(li, hs, dy, cw, cb, wa, ba, wx, bx, lam))


MESH = pl.DeviceIdType.MESH
HBM_SPEC = pl.BlockSpec(memory_space=pltpu.HBM)


def _slot(ref, p):
    return ref.at[p]


def _row_block(rows):
    return lambda ref, p: ref.at[pl.ds(p * rows, rows), :]


def _col_block(cols):
    return lambda ref, p: ref.at[:, pl.ds(p * cols, cols)]


class _Gather:
    def __init__(self, blocks, out_shapes, places):
        self.sources, self.out_shapes, self.places, self.n = list(blocks), list(out_shapes), list(places), len(blocks)

    def scratch(self):
        return [pltpu.SemaphoreType.DMA((self.n, 7)), pltpu.SemaphoreType.DMA((self.n, 7)), pltpu.SemaphoreType.DMA((self.n,))]

    def _plan(self, x_refs, out_refs, send_sems, recv_sems, local_sems):
        n = self.n
        x, y, c = lax.axis_index("x"), lax.axis_index("y"), lax.axis_index("c")
        me, sibling = (x, y, c), (x, y, 1 - c)
        chips = [(1 - x, y), (x, 1 - y), (1 - x, 1 - y)]

        def place(a, dev):
            return self.places[a](out_refs[a], 4 * dev[0] + 2 * dev[1] + dev[2])

        def copy(a, k, blk, to, src=None):
            return pltpu.make_async_remote_copy(
                src_ref=place(a, blk) if src is None else src, dst_ref=place(a, blk),
                send_sem=send_sems.at[a, k], recv_sem=recv_sems.at[a, k], device_id=to, device_id_type=MESH)

        mine = [pltpu.make_async_copy(x_refs[a], place(a, me), local_sems.at[a]) for a in range(n)]
        first = [copy(a, 0, me, sibling, src=x_refs[a]) for a in range(n)]
        first += [copy(a, 1 + j, me, (*chip, c), src=x_refs[a]) for j, chip in enumerate(chips) for a in range(n)]
        return me, sibling, chips, c, copy, mine, first

    def start(self, *refs):
        *_, mine, first = self._plan(*refs)
        for cp in mine + first:
            cp.start()

    def forward(self, *refs):
        me, sibling, chips, c, copy, _, _ = self._plan(*refs)
        for j, chip in enumerate(chips):
            for a in range(self.n):
                copy(a, 1 + j, (*chip, c), me).wait_recv()
                copy(a, 4 + j, (*chip, c), sibling).start()

    def finish(self, *refs):
        me, sibling, chips, c, copy, mine, first = self._plan(*refs)
        passed = [copy(a, 4 + j, (*chip, c), sibling) for j, chip in enumerate(chips) for a in range(self.n)]
        for a in range(self.n):
            copy(a, 0, sibling, me).wait_recv()
        for j, chip in enumerate(chips):
            for a in range(self.n):
                copy(a, 4 + j, (*chip, 1 - c), me).wait_recv()
        for cp in first + passed:
            cp.wait_send()
        for cp in mine:
            cp.wait()


class _GradExchange:
    def __init__(self, sources, takes, piece_shapes):
        self.sources, self.takes, self.n = list(sources), list(takes), len(sources)
        self.out_shapes = [jax.ShapeDtypeStruct((N_DEV,) + tuple(s), BF16) for s in piece_shapes]

    def scratch(self):
        return [pltpu.SemaphoreType.DMA((self.n, 7)), pltpu.SemaphoreType.DMA((self.n, 7)), pltpu.SemaphoreType.DMA((self.n,))]

    def _copies(self, src_refs, out_refs, send_sems, recv_sems, local_sems):
        x, y, c = lax.axis_index("x"), lax.axis_index("y"), lax.axis_index("c")
        me = 4 * x + 2 * y + c
        mine = [pltpu.make_async_copy(self.takes[i](src_refs[i], me), out_refs[i].at[me], local_sems.at[i]) for i in range(self.n)]
        remote = []
        for k in range(1, N_DEV):
            px, py, pc = x ^ ((k >> 2) & 1), y ^ ((k >> 1) & 1), c ^ (k & 1)
            peer = 4 * px + 2 * py + pc
            for i in range(self.n):
                remote.append(pltpu.make_async_remote_copy(
                    src_ref=self.takes[i](src_refs[i], peer), dst_ref=out_refs[i].at[me], send_sem=send_sems.at[i, k - 1],
                    recv_sem=recv_sems.at[i, k - 1], device_id=(px, py, pc), device_id_type=MESH))
        return mine, remote

    def start(self, *refs):
        mine, remote = self._copies(*refs)
        for cp in mine + remote:
            cp.start()

    def forward(self, *refs):
        pass

    def finish(self, *refs):
        mine, remote = self._copies(*refs)
        for cp in remote:
            cp.wait_recv()
        for cp in remote:
            cp.wait_send()
        for cp in mine:
            cp.wait()


def _run_exchange(ex, name):
    def body(*refs):
        src_refs, out_refs, sems = refs[:ex.n], refs[ex.n:2 * ex.n], refs[2 * ex.n:]
        ex.start(src_refs, out_refs, *sems)
        ex.forward(src_refs, out_refs, *sems)
        ex.finish(src_refs, out_refs, *sems)

    return pl.pallas_call(
        body, name=name, out_shape=ex.out_shapes, in_specs=[HBM_SPEC] * ex.n, out_specs=[HBM_SPEC] * ex.n,
        scratch_shapes=ex.scratch(),
    )(*ex.sources)


def _pallas_hosting(body, ex, *, name, grid, in_specs, out_specs, out_shape, scratch_shapes, semantics, operands):
    if ex is None:
        outs = pl.pallas_call(body, name=name, grid=grid, in_specs=in_specs, out_specs=out_specs, out_shape=out_shape,
                              scratch_shapes=scratch_shapes, compiler_params=_params(*semantics))(*operands)
        return outs, None
    n_in, n_out, n_scr, n = len(in_specs), len(out_specs), len(scratch_shapes), ex.n

    def at_step(pick):
        conds = [pl.program_id(k) == pick(size) for k, size in enumerate(grid)]
        return functools.reduce(jnp.logical_and, conds)

    def hosting(*refs):
        ins, ex_ins = refs[:n_in], refs[n_in:n_in + n]
        outs, ex_outs = refs[n_in + n:n_in + n + n_out], refs[n_in + n + n_out:n_in + 2 * n + n_out]
        scratch, sems = refs[n_in + 2 * n + n_out:n_in + 2 * n + n_out + n_scr], refs[n_in + 2 * n + n_out + n_scr:]
        pl.when(at_step(lambda size: 0))(lambda: ex.start(ex_ins, ex_outs, *sems))
        pl.when(at_step(lambda size: size - 1))(lambda: ex.forward(ex_ins, ex_outs, *sems))
        body(*ins, *outs, *scratch)
        pl.when(at_step(lambda size: size - 1))(lambda: ex.finish(ex_ins, ex_outs, *sems))

    res = pl.pallas_call(
        hosting, name=name, grid=grid, in_specs=list(in_specs) + [HBM_SPEC] * n, out_specs=list(out_specs) + [HBM_SPEC] * n,
        out_shape=list(out_shape) + ex.out_shapes, scratch_shapes=list(scratch_shapes) + ex.scratch(),
        compiler_params=_params(*(["arbitrary"] * len(grid))),
    )(*operands, *ex.sources)
    return res[:n_out], res[n_out:]


def _sum_slots(parts, rows_per_step, name):
    _, R, C = parts.shape

    def body(p_ref, o_ref):
        acc = p_ref[0].astype(F32)
        for q in range(1, N_DEV):
            acc = acc + p_ref[q].astype(F32)
        o_ref[...] = acc

    return pl.pallas_call(
        body, name=name, grid=(R // rows_per_step,),
        in_specs=[pl.BlockSpec((N_DEV, rows_per_step, C), lambda i: (0, i, 0))],
        out_specs=pl.BlockSpec((rows_per_step, C), lambda i: (i, 0)),
        out_shape=jax.ShapeDtypeStruct((R, C), F32),
        compiler_params=_params("parallel"),
    )(parts)


def _adamw(w, g, m, v, rows_per_step, name):
    R, C = w.shape
    c1 = 1.0 - ADAM_B1 ** ADAM_STEP
    c2 = 1.0 - ADAM_B2 ** ADAM_STEP

    def body(w_ref, g_ref, m_ref, v_ref, d_ref, nm_ref, nv_ref):
        gv = g_ref[...]
        nm = ADAM_B1 * m_ref[...] + (1.0 - ADAM_B1) * gv
        nv = ADAM_B2 * v_ref[...] + (1.0 - ADAM_B2) * (gv * gv)
        nm_ref[...] = nm
        nv_ref[...] = nv
        d_ref[...] = (-ADAM_LR) * ((nm / c1) / (jnp.sqrt(nv / c2) + ADAM_EPS) + ADAM_WD * w_ref[...])

    spec = pl.BlockSpec((rows_per_step, C), lambda i: (i, 0))
    shape = jax.ShapeDtypeStruct((R, C), F32)
    return pl.pallas_call(
        body, name=name, grid=(R // rows_per_step,),
        in_specs=[spec] * 4, out_specs=[spec] * 3, out_shape=[shape] * 3,
        compiler_params=_params("parallel"),
    )(w, g, m, v)


WEIGHT_ORDER = ("norm1_g", "w_in", "conv_dw_w", "conv_dw_b", "conv_ln_g", "conv_ln_b", "lru_conv_w", "lru_conv_b", "lru_wa",
                "lru_ba", "lru_wx", "lru_bx", "lru_lambda", "w_out", "norm2_g", "w_up", "w_down", "final_g")
BIG = ("w_in", "w_out", "w_up", "w_down")
SMALL_SHARDED = {"conv_dw_w": (DEPTH, CONV_TAPS, CONV_W), "lru_conv_w": (DEPTH, LRU_TAPS, LRU_W)}
SMALL_FULL = {
    "norm1_g": (DEPTH, D_MODEL), "conv_dw_w": (DEPTH, CONV_TAPS, CONV_W), "conv_dw_b": (DEPTH, CONV_W),
    "conv_ln_g": (DEPTH, CONV_W), "conv_ln_b": (DEPTH, CONV_W), "lru_conv_w": (DEPTH, LRU_TAPS, LRU_W),
    "lru_conv_b": (DEPTH, LRU_W), "lru_wa": (DEPTH, LRU_HEADS, HEAD_DIM, HEAD_DIM), "lru_ba": (DEPTH, LRU_W),
    "lru_wx": (DEPTH, LRU_HEADS, HEAD_DIM, HEAD_DIM), "lru_bx": (DEPTH, LRU_W), "lru_lambda": (DEPTH, LRU_W),
    "norm2_g": (DEPTH, D_MODEL), "final_g": (D_MODEL,),
}
SMALL_COLS = 128
SMALL_ROWS = 1000
FILTER_ROWS = 24
W_IN_SHARD = IN_COLS // N_DEV
W_OUT_SHARD = D_MODEL // N_DEV
FF_SHARD = D_FF // N_DEV


def _pack_rows(flat_parts, cols, rows):
    flat = jnp.concatenate(flat_parts)
    return jnp.pad(flat, (0, rows * cols - flat.shape[0])).reshape(rows, cols)


WEIGHT_GATHER = {
    "w_in": ((N_DEV, D_MODEL, W_IN_SHARD), _slot),
    "w_out": ((D_MODEL, D_MODEL), _row_block(W_OUT_SHARD)),
    "w_up": ((D_MODEL, D_FF), _col_block(FF_SHARD)),
    "w_down": ((D_FF, D_MODEL), _row_block(FF_SHARD)),
}
GATHER_HOSTS = {
    "inproj": (("w_out", 0),), "attn_fwd": (("w_up", 0),), "lru_fwd": (("w_down", 0),),
    "outproj": (("w_in", 1),), "up": (("w_up", 1),), "down": (("w_down", 1), ("w_out", 1)),
}


def _weight_gather(local, items, with_filters=False):
    blocks = [local[n][l].astype(BF16) for n, l in items]
    shapes = [jax.ShapeDtypeStruct(WEIGHT_GATHER[n][0], BF16) for n, _ in items]
    places = [WEIGHT_GATHER[n][1] for n, _ in items]
    if with_filters:
        blocks.append(_pack_rows([local[n].reshape(-1) for n in SMALL_SHARDED], SMALL_COLS, FILTER_ROWS))
        shapes.append(jax.ShapeDtypeStruct((N_DEV, FILTER_ROWS, SMALL_COLS), F32))
        places.append(_slot)
    return _Gather(blocks, shapes, places)


def _keep_gathered(full, items, landed):
    for (n, l), arr in zip(items, landed):
        full[n][l] = arr.transpose(1, 0, 2).reshape(D_MODEL, IN_COLS) if n == "w_in" else arr


def _unpack_filters(slots):
    flat, off, out = slots.reshape(N_DEV, -1), 0, {}
    for n, shp in SMALL_SHARDED.items():
        shard = shp[:-1] + (shp[-1] // N_DEV,)
        size = int(np.prod(shard))
        out[n] = jnp.moveaxis(flat[:, off:off + size].reshape((N_DEV,) + shard), 0, -2).reshape(shp)
        off += size
    return out


def _grad_exchange(name, dw):
    if name == "w_in":
        return _GradExchange([dw.reshape(D_MODEL, N_DEV, W_IN_SHARD).transpose(1, 0, 2)], [_slot], [(D_MODEL, W_IN_SHARD)])
    if name == "w_out":
        return _GradExchange([dw], [_row_block(W_OUT_SHARD)], [(W_OUT_SHARD, D_MODEL)])
    return _GradExchange([dw], [_col_block(FF_SHARD)], [(D_MODEL, FF_SHARD)])


def _sum_adamw(parts, w, m, v, rows_per_step, transposed, name):
    _, R, C = parts[0].shape
    tr = rows_per_step
    steps = R // tr
    c1 = 1.0 - ADAM_B1 ** ADAM_STEP
    c2 = 1.0 - ADAM_B2 ** ADAM_STEP

    def slot_sum(p_ref):
        acc = p_ref[0].astype(F32)
        for q in range(1, N_DEV):
            acc = acc + p_ref[q].astype(F32)
        return acc

    def body(p0_ref, p1_ref, w_ref, m_ref, v_ref, g_ref, d_ref, nm_ref, nv_ref):
        gv = jnp.where(pl.program_id(0) == 0, slot_sum(p0_ref), slot_sum(p1_ref))
        if transposed:
            gv = gv.T
        nm = ADAM_B1 * m_ref[...] + (1.0 - ADAM_B1) * gv
        nv = ADAM_B2 * v_ref[...] + (1.0 - ADAM_B2) * (gv * gv)
        g_ref[...] = gv
        nm_ref[...] = nm
        nv_ref[...] = nv
        d_ref[...] = (-ADAM_LR) * ((nm / c1) / (jnp.sqrt(nv / c2) + ADAM_EPS) + ADAM_WD * w_ref[...])

    if transposed:
        spec = pl.BlockSpec((None, C, tr), lambda l, i: (l, 0, i))
    else:
        spec = pl.BlockSpec((None, tr, C), lambda l, i: (l, i, 0))
    shape = jax.ShapeDtypeStruct(w.shape, F32)
    part0 = pl.BlockSpec((N_DEV, tr, C), lambda l, i: (0, jnp.where(l == 0, i, steps - 1), 0))
    part1 = pl.BlockSpec((N_DEV, tr, C), lambda l, i: (0, jnp.where(l == 1, i, 0), 0))
    return pl.pallas_call(
        body, name=name, grid=(DEPTH, steps),
        in_specs=[part0, part1, spec, spec, spec],
        out_specs=[spec] * 4, out_shape=[shape] * 4,
        compiler_params=_params("arbitrary", "arbitrary"),
    )(parts[0], parts[1], w, m, v)


def _block_diag(w):
    eye = jnp.eye(LRU_HEADS, dtype=bool)
    return jnp.where(eye[:, None, :, None], w[:, :, None, :], jnp.zeros((), w.dtype)).reshape(LRU_W, LRU_W)


def _diag_blocks(m):
    eye = jnp.eye(LRU_HEADS, dtype=bool)
    m4 = m.reshape(LRU_HEADS, HEAD_DIM, LRU_HEADS, HEAD_DIM)
    return jnp.sum(jnp.where(eye[:, None, :, None], m4, 0.0), axis=2)


def _pack_small(values, loss_row):
    parts = [values[n].reshape(-1) for n in SMALL_FULL]
    parts.append(loss_row.reshape(-1))
    return _pack_rows(parts, SMALL_COLS, SMALL_ROWS)


def _unpack_small(packed):
    flat = packed.reshape(-1)
    out, off = {}, 0
    for n, shp in SMALL_FULL.items():
        size = int(np.prod(shp))
        out[n] = flat[off:off + size].reshape(shp)
        off += size
    return out, flat[off]


def kernel(x, norm1_g, w_in, conv_dw_w, conv_dw_b, conv_ln_g, conv_ln_b, lru_conv_w, lru_conv_b, lru_wa, lru_ba, lru_wx, lru_bx, lru_lambda, w_out, norm2_g, w_up, w_down, final_g, loss_target, m_norm1_g, m_w_in, m_conv_dw_w, m_conv_dw_b, m_conv_ln_g, m_conv_ln_b, m_lru_conv_w, m_lru_conv_b, m_lru_wa, m_lru_ba, m_lru_wx, m_lru_bx, m_lru_lambda, m_w_out, m_norm2_g, m_w_up, m_w_down, m_final_g, v_norm1_g, v_w_in, v_conv_dw_w, v_conv_dw_b, v_conv_ln_g, v_conv_ln_b, v_lru_conv_w, v_lru_conv_b, v_lru_wa, v_lru_ba, v_lru_wx, v_lru_bx, v_lru_lambda, v_w_out, v_norm2_g, v_w_up, v_w_down, v_final_g):
    local = dict(zip(WEIGHT_ORDER, (norm1_g, w_in, conv_dw_w, conv_dw_b, conv_ln_g, conv_ln_b, lru_conv_w, lru_conv_b, lru_wa,
                                    lru_ba, lru_wx, lru_bx, lru_lambda, w_out, norm2_g, w_up, w_down, final_g)))
    mom1 = dict(zip(WEIGHT_ORDER, (m_norm1_g, m_w_in, m_conv_dw_w, m_conv_dw_b, m_conv_ln_g, m_conv_ln_b, m_lru_conv_w,
                                   m_lru_conv_b, m_lru_wa, m_lru_ba, m_lru_wx, m_lru_bx, m_lru_lambda, m_w_out, m_norm2_g,
                                   m_w_up, m_w_down, m_final_g)))
    mom2 = dict(zip(WEIGHT_ORDER, (v_norm1_g, v_w_in, v_conv_dw_w, v_conv_dw_b, v_conv_ln_g, v_conv_ln_b, v_lru_conv_w,
                                   v_lru_conv_b, v_lru_wa, v_lru_ba, v_lru_wx, v_lru_bx, v_lru_lambda, v_w_out, v_norm2_g,
                                   v_w_up, v_w_down, v_final_g)))
    B, S, _ = x.shape
    T = B * S
    my_slot = 4 * lax.axis_index("x") + 2 * lax.axis_index("y") + lax.axis_index("c")
    row = lambda a: a.reshape(1, -1)

    full = {n: [None] * DEPTH for n in BIG}
    first_items = (("w_in", 0),)
    landed = _run_exchange(_weight_gather(local, first_items, with_filters=True), "gather_first_weights")
    _keep_gathered(full, first_items, landed)
    full.update(_unpack_filters(landed[-1]))

    def hosted(call, layer, fn, *args):
        items = GATHER_HOSTS[call] if layer == 0 else ()
        outs, landed = fn(*args, ex=_weight_gather(local, items) if items else None)
        _keep_gathered(full, items, landed or ())
        return outs

    saved = []
    cur = x.reshape(T, D_MODEL)
    for l in range(DEPTH):
        h, qkv, ci, li = hosted("inproj", l, _inproj, cur, row(norm1_g[l]), full["w_in"][l])
        qkv = qkv.reshape(B, S, QKV_W)
        o, lse = hosted("attn_fwd", l, _attn_fwd, qkv)
        o, lse = o.reshape(T, ATTN_W), lse.reshape(T, ATTN_W)
        ci = ci.reshape(B, S, CONV_IN_W)
        li = li.reshape(B, S, LRU_IN_W)
        conv_p = (full["conv_dw_w"][l], row(conv_dw_b[l]), row(conv_ln_g[l]), row(conv_ln_b[l]))
        lru_p = (full["lru_conv_w"][l], row(lru_conv_b[l]), _block_diag(lru_wa[l]).astype(BF16), row(lru_ba[l]),
                 _block_diag(lru_wx[l]).astype(BF16), row(lru_bx[l]), row(lru_lambda[l]))
        yc, cpre = _conv_fwd(ci, *conv_p)
        yl, hs = hosted("lru_fwd", l, _lru_fwd, li, *lru_p)
        x1, mix = hosted("outproj", l, _outproj, cur, o, lse, yc.reshape(T, CONV_W), yl.reshape(T, LRU_W), full["w_out"][l])
        h2, r = hosted("up", l, _up, x1, row(norm2_g[l]), full["w_up"][l])
        (x2,) = hosted("down", l, _down, x1, r, full["w_down"][l])
        saved.append(dict(x=cur, h=h, qkv=qkv, o=o, lse=lse, ci=ci, li=li, cpre=cpre, hs=hs, x1=x1, mix=mix, h2=h2, r=r,
                          conv_p=conv_p, lru_p=lru_p))
        cur = x2

    dx, loss_part, dgf = _loss_head(cur, loss_target.reshape(T, D_MODEL), row(final_g))

    received = {n: [None] * DEPTH for n in BIG}
    small_grads = {n: [None] * DEPTH for n in SMALL_FULL if n != "final_g"}
    for l in reversed(range(DEPTH)):
        sv = saved[l]
        (dpre,), _ = _down_bwd_act(dx, sv["r"], full["w_down"][l])
        dw_down = _down_bwd_w(sv["r"], dx)
        (dx1, dg2), _ = _up_bwd_act(dpre, full["w_up"][l], sv["x1"], row(norm2_g[l]), dx)
        dw_up = _up_bwd_w(sv["h2"], dpre)
        do, dd, dyc, dyl, dw_out = _outproj_bwd(dx1, sv["mix"], full["w_out"][l], sv["o"], sv["lse"])
        seq = lambda a: a.reshape(B, S, ATTN_W)
        (dq, dk, dv), (received["w_down"][l],) = _attn_bwd(sv["qkv"], seq(do), seq(sv["lse"]), seq(dd),
                                                           ex=_grad_exchange("w_down", dw_down))
        (dci, dcw, dcb, dlg, dlb), (received["w_out"][l],) = _conv_bwd(
            sv["ci"], sv["cpre"], dyc.reshape(B, S, CONV_W), sv["conv_p"][0], sv["conv_p"][2], sv["conv_p"][3],
            ex=_grad_exchange("w_out", dw_out))
        (dli, dlcw, dlcb, dwa, dba, dwx, dbx, dlam), (received["w_up"][l],) = _lru_bwd(
            sv["li"], sv["hs"], dyl.reshape(B, S, LRU_W), *sv["lru_p"], ex=_grad_exchange("w_up", dw_up))
        dz = tuple(t.reshape(T, -1) for t in (dq, dk, dv, dci, dli))
        dw_in = _inproj_bwd_w(dz, sv["h"])
        (dx, dg1), (received["w_in"][l],) = _inproj_bwd_act(dz, full["w_in"][l], sv["x"], row(norm1_g[l]), dx1,
                                                            ex=_grad_exchange("w_in", dw_in))
        for n, g in (("norm1_g", dg1), ("conv_dw_w", dcw), ("conv_dw_b", dcb), ("conv_ln_g", dlg), ("conv_ln_b", dlb),
                     ("lru_conv_w", dlcw), ("lru_conv_b", dlcb), ("lru_wa", _diag_blocks(dwa)), ("lru_ba", dba),
                     ("lru_wx", _diag_blocks(dwx)), ("lru_bx", dbx), ("lru_lambda", dlam), ("norm2_g", dg2)):
            small_grads[n][l] = g.reshape(SMALL_FULL[n][1:])
    grad_x = dx.reshape(B, S, D_MODEL)

    small_local = {n: jnp.stack(g) for n, g in small_grads.items()}
    small_local["final_g"] = dgf.reshape(D_MODEL)
    packet = _pack_small(small_local, loss_part[0:1, :])
    small_gather = _Gather([packet], [jax.ShapeDtypeStruct((N_DEV, SMALL_ROWS, SMALL_COLS), F32)], [_slot])
    (gathered,) = _run_exchange(small_gather, "gather_small_grads")

    grads, delta, new_m, new_v = {}, {}, {}, {}
    for n, rows_per_step in (("w_in", 256), ("w_out", W_OUT_SHARD), ("w_up", 256), ("w_down", 256)):
        grads[n], delta[n], new_m[n], new_v[n] = _sum_adamw(received[n], local[n], mom1[n], mom2[n], rows_per_step,
                                                            n == "w_down", "sum_adamw_" + n)

    small_sum = _sum_slots(gathered, SMALL_ROWS, "sum_small_grads")
    small_g, loss = _unpack_small(small_sum)

    for n, fullshape in SMALL_SHARDED.items():
        width = fullshape[-1] // N_DEV
        g = lax.dynamic_slice_in_dim(small_g[n], my_slot * width, width, axis=2)
        two_d = (fullshape[0] * fullshape[1], width)
        d, nm, nv = _adamw(local[n].reshape(two_d), g.reshape(two_d), mom1[n].reshape(two_d), mom2[n].reshape(two_d),
                           two_d[0], "adamw_" + n)
        grads[n], delta[n], new_m[n], new_v[n] = (t.reshape(g.shape) for t in (g.reshape(two_d), d, nm, nv))
    replicated = [n for n in SMALL_FULL if n not in SMALL_SHARDED]
    zero_row = jnp.zeros((1, SMALL_COLS), F32)
    packed_state = []
    for src in (local, mom1, mom2):
        vals = {n: (src[n] if n in replicated else jnp.zeros(SMALL_FULL[n], F32)) for n in SMALL_FULL}
        packed_state.append(_pack_small(vals, zero_row))
    d, nm, nv = _adamw(packed_state[0], small_sum, packed_state[1], packed_state[2], SMALL_ROWS, "adamw_small")
    d, nm, nv = _unpack_small(d)[0], _unpack_small(nm)[0], _unpack_small(nv)[0]
    for n in replicated:
        grads[n], delta[n], new_m[n], new_v[n] = small_g[n], d[n], nm[n], nv[n]

    return (loss, grad_x, *[grads[n] for n in WEIGHT_ORDER], *[delta[n] for n in WEIGHT_ORDER],
            *[new_m[n] for n in WEIGHT_ORDER], *[new_v[n] for n in WEIGHT_ORDER])
```

```python
import functools
import math

import numpy as np
import jax
import jax.numpy as jnp
from jax import lax
from jax.experimental import pallas as pl
from jax.experimental.pallas import tpu as pltpu

F32 = jnp.float32
BF16 = jnp.bfloat16

D_MODEL = 1024
SEQ_LEN = 2048
HEAD_DIM = 64
ATTN_W = 384
CONV_W = 256
CONV_TAPS = 31
LRU_W = 384
LRU_TAPS = 4
LRU_HEADS = 6
LRU_C = 8.0
QKV_W = 3 * ATTN_W
CONV_IN_W = 2 * CONV_W
LRU_IN_W = 2 * LRU_W
IN_COLS = QKV_W + CONV_IN_W + LRU_IN_W
D_FF = 4096
DEPTH = 2
N_DEV = 8
RMS_EPS = 1e-6
LN_EPS = 1e-5
ATTN_BLOCK = 128
ATTN_DILATIONS = (1, 4, 16)
N_UNITS = 16
UNIT_UNROLL = 4
NEG_BIG = -1e30

ADAM_LR = 0.001
ADAM_B1 = 0.9
ADAM_B2 = 0.999
ADAM_EPS = 1e-08
ADAM_WD = 0.01
ADAM_STEP = 10

VMEM_LIMIT = 56 * 1024 * 1024
ROW_TILE = 512
GRAD_ROW_TILE = 1024
CONV_FWD_CHUNK = 128
CONV_BWD_CHUNK = 128
LRU_FWD_CHUNK = 128
LRU_BWD_CHUNK = 128


def _params(*sem):
    return pltpu.CompilerParams(dimension_semantics=sem if sem else None, vmem_limit_bytes=VMEM_LIMIT)


def _resident(shape):
    return pl.BlockSpec(shape, lambda *_: (0,) * len(shape), pipeline_mode=pl.Buffered(1))


def _dot(a, b):
    return jnp.dot(a, b, preferred_element_type=F32)


def _dot_nt(a, b):
    return lax.dot_general(a, b, (((1,), (1,)), ((), ())), preferred_element_type=F32)


def _dot_tn(a, b):
    return lax.dot_general(a, b, (((0,), (0,)), ((), ())), preferred_element_type=F32)


def _rms_fwd(x, g):
    rstd = lax.rsqrt(jnp.mean(x * x, axis=-1, keepdims=True) + RMS_EPS)
    xhat = x * rstd
    return xhat * g, xhat, rstd


def _rms_bwd(dh, xhat, rstd, g):
    dxh = dh * g
    dx = rstd * (dxh - xhat * jnp.mean(dxh * xhat, axis=-1, keepdims=True))
    dg = jnp.sum(dh * xhat, axis=0, keepdims=True)
    return dx, dg


def _sigmoid(x):
    return 0.5 * jnp.tanh(0.5 * x) + 0.5


def _one_minus_exp(x, exp_x):
    small = -x * (1.0 + x * (0.5 + x * (1.0 / 6.0)))
    return jnp.where(x > -0.01, small, 1.0 - exp_x)


def _log1p(z):
    w = 1.0 + z
    return jnp.where(w == 1.0, z, z * jnp.log(w) / jnp.where(w == 1.0, 1.0, w - 1.0))


def _softplus(x):
    return jnp.maximum(x, 0.0) + _log1p(jnp.exp(-jnp.abs(x)))


GELU_K = math.sqrt(2.0 / math.pi)


def _gelu(x):
    t = jnp.tanh(GELU_K * (x + 0.044715 * x * x * x))
    return 0.5 * x * (1.0 + t), t


def _gelu_grad(x, t):
    return 0.5 * (1.0 + t) + 0.5 * x * (1.0 - t * t) * GELU_K * (1.0 + 3.0 * 0.044715 * x * x)


def _inproj(x2d, g, w, ex=None):
    T = x2d.shape[0]
    tm = ROW_TILE

    def body(x_ref, g_ref, w_ref, h_ref, qkv_ref, ci_ref, li_ref):
        h, _, _ = _rms_fwd(x_ref[...], g_ref[...])
        hb = h.astype(BF16)
        h_ref[...] = hb
        qkv_ref[...] = _dot(hb, w_ref[:, 0:QKV_W])
        ci_ref[...] = _dot(hb, w_ref[:, QKV_W:QKV_W + CONV_IN_W])
        li_ref[...] = _dot(hb, w_ref[:, QKV_W + CONV_IN_W:IN_COLS])

    return _pallas_hosting(
        body, ex, name="inproj", grid=(T // tm,),
        in_specs=[pl.BlockSpec((tm, D_MODEL), lambda i: (i, 0)),
                  pl.BlockSpec((1, D_MODEL), lambda i: (0, 0)),
                  _resident((D_MODEL, IN_COLS))],
        out_specs=[pl.BlockSpec((tm, D_MODEL), lambda i: (i, 0)),
                   pl.BlockSpec((tm, QKV_W), lambda i: (i, 0)),
                   pl.BlockSpec((tm, CONV_IN_W), lambda i: (i, 0)),
                   pl.BlockSpec((tm, LRU_IN_W), lambda i: (i, 0))],
        out_shape=[jax.ShapeDtypeStruct((T, D_MODEL), BF16), jax.ShapeDtypeStruct((T, QKV_W), F32),
                   jax.ShapeDtypeStruct((T, CONV_IN_W), F32), jax.ShapeDtypeStruct((T, LRU_IN_W), F32)],
        scratch_shapes=[], semantics=("parallel",), operands=(x2d, g, w))


def _attn_alpha(lse):
    l0, l1, l2 = lse[:, 0:128], lse[:, 128:256], lse[:, 256:384]
    m = jnp.maximum(jnp.maximum(l0, l1), l2)
    e0, e1, e2 = jnp.exp(l0 - m), jnp.exp(l1 - m), jnp.exp(l2 - m)
    inv = 1.0 / (e0 + e1 + e2)
    return e0 * inv, e1 * inv, e2 * inv


def _outproj(x2d, o, lse, yc, yl, w, ex=None):
    T = x2d.shape[0]
    tm = ROW_TILE

    def body(x_ref, o_ref, lse_ref, yc_ref, yl_ref, w_ref, x1_ref, mix_ref):
        al = _attn_alpha(lse_ref[...])
        for p in range(3):
            mix_ref[:, p * 128:(p + 1) * 128] = (o_ref[:, p * 128:(p + 1) * 128] * al[p]).astype(BF16)
        mix_ref[:, ATTN_W:ATTN_W + CONV_W] = yc_ref[...]
        mix_ref[:, ATTN_W + CONV_W:D_MODEL] = yl_ref[...]
        x1_ref[...] = x_ref[...] + _dot(mix_ref[...], w_ref[...])

    return _pallas_hosting(
        body, ex, name="outproj", grid=(T // tm,),
        in_specs=[pl.BlockSpec((tm, D_MODEL), lambda i: (i, 0)),
                  pl.BlockSpec((tm, ATTN_W), lambda i: (i, 0)),
                  pl.BlockSpec((tm, ATTN_W), lambda i: (i, 0)),
                  pl.BlockSpec((tm, CONV_W), lambda i: (i, 0)),
                  pl.BlockSpec((tm, LRU_W), lambda i: (i, 0)),
                  _resident((D_MODEL, D_MODEL))],
        out_specs=[pl.BlockSpec((tm, D_MODEL), lambda i: (i, 0)),
                   pl.BlockSpec((tm, D_MODEL), lambda i: (i, 0))],
        out_shape=[jax.ShapeDtypeStruct((T, D_MODEL), F32), jax.ShapeDtypeStruct((T, D_MODEL), BF16)],
        scratch_shapes=[], semantics=("parallel",), operands=(x2d, o, lse, yc, yl, w))


FF_CHUNK = 1024


def _up(x1, g, w, ex=None):
    T = x1.shape[0]
    tm = ROW_TILE

    def body(x_ref, g_ref, w_ref, h_ref, r_ref):
        h, _, _ = _rms_fwd(x_ref[...], g_ref[...])
        hb = h.astype(BF16)
        h_ref[...] = hb
        for c in range(0, D_FF, FF_CHUNK):
            r_ref[:, c:c + FF_CHUNK] = jnp.maximum(_dot(hb, w_ref[:, c:c + FF_CHUNK]), 0.0).astype(BF16)

    return _pallas_hosting(
        body, ex, name="up", grid=(T // tm,),
        in_specs=[pl.BlockSpec((tm, D_MODEL), lambda i: (i, 0)),
                  pl.BlockSpec((1, D_MODEL), lambda i: (0, 0)),
                  _resident((D_MODEL, D_FF))],
        out_specs=[pl.BlockSpec((tm, D_MODEL), lambda i: (i, 0)),
                   pl.BlockSpec((tm, D_FF), lambda i: (i, 0))],
        out_shape=[jax.ShapeDtypeStruct((T, D_MODEL), BF16), jax.ShapeDtypeStruct((T, D_FF), BF16)],
        scratch_shapes=[], semantics=("parallel",), operands=(x1, g, w))


def _square_bf16(r):
    rf = r.astype(F32)
    return (rf * rf).astype(BF16)


def _down(x1, r, w, ex=None):
    T = x1.shape[0]
    tm = ROW_TILE

    def body(x_ref, r_ref, w_ref, o_ref):
        acc = x_ref[...]
        for c in range(0, D_FF, FF_CHUNK):
            acc = acc + _dot(_square_bf16(r_ref[:, c:c + FF_CHUNK]), w_ref[c:c + FF_CHUNK, :])
        o_ref[...] = acc

    return _pallas_hosting(
        body, ex, name="down", grid=(T // tm,),
        in_specs=[pl.BlockSpec((tm, D_MODEL), lambda i: (i, 0)),
                  pl.BlockSpec((tm, D_FF), lambda i: (i, 0)),
                  _resident((D_FF, D_MODEL))],
        out_specs=[pl.BlockSpec((tm, D_MODEL), lambda i: (i, 0))],
        out_shape=[jax.ShapeDtypeStruct((T, D_MODEL), F32)],
        scratch_shapes=[], semantics=("parallel",), operands=(x1, r, w))


def _loss_head(x2, target, g):
    T = x2.shape[0]
    tm = ROW_TILE

    def body(x_ref, t_ref, g_ref, dx_ref, loss_ref, dg_ref):
        @pl.when(pl.program_id(0) == 0)
        def _():
            loss_ref[...] = jnp.zeros_like(loss_ref)
            dg_ref[...] = jnp.zeros_like(dg_ref)

        gv = g_ref[...]
        y, xhat, rstd = _rms_fwd(x_ref[...], gv)
        err = y - t_ref[...]
        loss_ref[...] += 0.5 * jnp.sum(jnp.mean(err * err, axis=-1, keepdims=True))
        dy = err * (1.0 / D_MODEL)
        dx, dg = _rms_bwd(dy, xhat, rstd, gv)
        dx_ref[...] = dx
        dg_ref[...] += dg

    return pl.pallas_call(
        body, name="loss_head", grid=(T // tm,),
        in_specs=[pl.BlockSpec((tm, D_MODEL), lambda i: (i, 0)),
                  pl.BlockSpec((tm, D_MODEL), lambda i: (i, 0)),
                  pl.BlockSpec((1, D_MODEL), lambda i: (0, 0))],
        out_specs=[pl.BlockSpec((tm, D_MODEL), lambda i: (i, 0)),
                   pl.BlockSpec((8, 128), lambda i: (0, 0)),
                   pl.BlockSpec((1, D_MODEL), lambda i: (0, 0))],
        out_shape=[jax.ShapeDtypeStruct((T, D_MODEL), F32), jax.ShapeDtypeStruct((8, 128), F32),
                   jax.ShapeDtypeStruct((1, D_MODEL), F32)],
        compiler_params=_params("arbitrary"),
    )(x2, target, g)


def _down_bwd_act(dx2, r, w, ex=None):
    T = dx2.shape[0]
    tm = ROW_TILE

    def body(dx_ref, r_ref, w_ref, o_ref):
        dxb = dx_ref[...].astype(BF16)
        for c in range(0, D_FF, FF_CHUNK):
            dff = _dot_nt(dxb, w_ref[c:c + FF_CHUNK, :])
            o_ref[:, c:c + FF_CHUNK] = (dff * (2.0 * r_ref[:, c:c + FF_CHUNK].astype(F32))).astype(BF16)

    return _pallas_hosting(
        body, ex, name="down_bwd_act", grid=(T // tm,),
        in_specs=[pl.BlockSpec((tm, D_MODEL), lambda i: (i, 0)),
                  pl.BlockSpec((tm, D_FF), lambda i: (i, 0)),
                  _resident((D_FF, D_MODEL))],
        out_specs=[pl.BlockSpec((tm, D_FF), lambda i: (i, 0))],
        out_shape=[jax.ShapeDtypeStruct((T, D_FF), BF16)],
        scratch_shapes=[], semantics=("parallel",), operands=(dx2, r, w))


def _down_bwd_w(r, dx2):
    T = dx2.shape[0]
    tk = GRAD_ROW_TILE
    nk = T // tk

    def body(r_ref, dx_ref, o_ref, acc_ref):
        k = pl.program_id(0)
        dxb = dx_ref[...].astype(BF16)

        @pl.when(k == 0)
        def _():
            acc_ref[...] = jnp.zeros_like(acc_ref)

        for c in range(0, D_FF, FF_CHUNK):
            acc_ref[:, c:c + FF_CHUNK] += _dot_tn(dxb, _square_bf16(r_ref[:, c:c + FF_CHUNK]))

        @pl.when(k == nk - 1)
        def _():
            o_ref[...] = acc_ref[...].astype(BF16)

    return pl.pallas_call(
        body, name="down_bwd_w", grid=(nk,),
        in_specs=[pl.BlockSpec((tk, D_FF), lambda k: (k, 0)),
                  pl.BlockSpec((tk, D_MODEL), lambda k: (k, 0))],
        out_specs=_resident((D_MODEL, D_FF)),
        out_shape=jax.ShapeDtypeStruct((D_MODEL, D_FF), BF16),
        scratch_shapes=[pltpu.VMEM((D_MODEL, D_FF), F32)],
        compiler_params=_params("arbitrary"),
    )(r, dx2)


def _up_bwd_act(dpre, w, x1, g, dx2, ex=None):
    T = dx2.shape[0]
    tm = ROW_TILE

    def body(dp_ref, w_ref, x_ref, g_ref, dx2_ref, dx1_ref, dg_ref):
        dh = _dot_nt(dp_ref[:, 0:FF_CHUNK], w_ref[:, 0:FF_CHUNK])
        for c in range(FF_CHUNK, D_FF, FF_CHUNK):
            dh = dh + _dot_nt(dp_ref[:, c:c + FF_CHUNK], w_ref[:, c:c + FF_CHUNK])
        gv = g_ref[...]
        _, xhat, rstd = _rms_fwd(x_ref[...], gv)
        dx, dg = _rms_bwd(dh, xhat, rstd, gv)
        dx1_ref[...] = dx2_ref[...] + dx

        @pl.when(pl.program_id(0) == 0)
        def _():
            dg_ref[...] = dg

        @pl.when(pl.program_id(0) != 0)
        def _():
            dg_ref[...] += dg

    return _pallas_hosting(
        body, ex, name="up_bwd_act", grid=(T // tm,),
        in_specs=[pl.BlockSpec((tm, D_FF), lambda i: (i, 0)),
                  _resident((D_MODEL, D_FF)),
                  pl.BlockSpec((tm, D_MODEL), lambda i: (i, 0)),
                  pl.BlockSpec((1, D_MODEL), lambda i: (0, 0)),
                  pl.BlockSpec((tm, D_MODEL), lambda i: (i, 0))],
        out_specs=[pl.BlockSpec((tm, D_MODEL), lambda i: (i, 0)),
                   pl.BlockSpec((1, D_MODEL), lambda i: (0, 0))],
        out_shape=[jax.ShapeDtypeStruct((T, D_MODEL), F32), jax.ShapeDtypeStruct((1, D_MODEL), F32)],
        scratch_shapes=[], semantics=("arbitrary",), operands=(dpre, w, x1, g, dx2))


def _up_bwd_w(h2, dpre):
    T = h2.shape[0]
    tk = GRAD_ROW_TILE
    nk = T // tk

    def body(h_ref, dp_ref, o_ref, acc_ref):
        k = pl.program_id(0)
        hb = h_ref[...]

        @pl.when(k == 0)
        def _():
            acc_ref[...] = jnp.zeros_like(acc_ref)

        for c in range(0, D_FF, FF_CHUNK):
            acc_ref[:, c:c + FF_CHUNK] += _dot_tn(hb, dp_ref[:, c:c + FF_CHUNK])

        @pl.when(k == nk - 1)
        def _():
            o_ref[...] = acc_ref[...].astype(BF16)

    return pl.pallas_call(
        body, name="up_bwd_w", grid=(nk,),
        in_specs=[pl.BlockSpec((tk, D_MODEL), lambda k: (k, 0)),
                  pl.BlockSpec((tk, D_FF), lambda k: (k, 0))],
        out_specs=_resident((D_MODEL, D_FF)),
        out_shape=jax.ShapeDtypeStruct((D_MODEL, D_FF), BF16),
        scratch_shapes=[pltpu.VMEM((D_MODEL, D_FF), F32)],
        compiler_params=_params("arbitrary"),
    )(h2, dpre)


def _outproj_bwd(dx1, mix, w, o, lse):
    T = dx1.shape[0]
    tm = ROW_TILE
    nk = T // tm

    def body(dx_ref, mix_ref, w_ref, o_ref, lse_ref, do_ref, dd_ref, dc_ref, dl_ref, dw_ref, acc_ref):
        i = pl.program_id(0)
        dxb = dx_ref[...].astype(BF16)
        dmix = _dot_nt(dxb, w_ref[...])
        al = _attn_alpha(lse_ref[...])
        first = lax.broadcasted_iota(jnp.int32, (tm, 128), 1) < HEAD_DIM
        tot = jnp.zeros((tm, 128), F32)
        for p in range(3):
            sl = slice(p * 128, (p + 1) * 128)
            dy = dmix[:, sl]
            do_ref[:, sl] = dy * al[p]
            prod = dy * o_ref[:, sl]
            s0 = jnp.sum(jnp.where(first, prod, 0.0), axis=-1, keepdims=True)
            s1 = jnp.sum(jnp.where(first, 0.0, prod), axis=-1, keepdims=True)
            tot = tot + al[p] * jnp.where(first, s0, s1)
        for p in range(3):
            dd_ref[:, p * 128:(p + 1) * 128] = -al[p] * tot
        dc_ref[...] = dmix[:, ATTN_W:ATTN_W + CONV_W]
        dl_ref[...] = dmix[:, ATTN_W + CONV_W:D_MODEL]
        part = _dot_tn(mix_ref[...], dxb)

        @pl.when(i == 0)
        def _():
            acc_ref[...] = part

        @pl.when(i != 0)
        def _():
            acc_ref[...] += part

        @pl.when(i == nk - 1)
        def _():
            dw_ref[...] = acc_ref[...].astype(BF16)

    return pl.pallas_call(
        body, name="outproj_bwd", grid=(nk,),
        in_specs=[pl.BlockSpec((tm, D_MODEL), lambda i: (i, 0)),
                  pl.BlockSpec((tm, D_MODEL), lambda i: (i, 0)),
                  _resident((D_MODEL, D_MODEL)),
                  pl.BlockSpec((tm, ATTN_W), lambda i: (i, 0)),
                  pl.BlockSpec((tm, ATTN_W), lambda i: (i, 0))],
        out_specs=[pl.BlockSpec((tm, ATTN_W), lambda i: (i, 0)),
                   pl.BlockSpec((tm, ATTN_W), lambda i: (i, 0)),
                   pl.BlockSpec((tm, CONV_W), lambda i: (i, 0)),
                   pl.BlockSpec((tm, LRU_W), lambda i: (i, 0)),
                   _resident((D_MODEL, D_MODEL))],
        out_shape=[jax.ShapeDtypeStruct((T, ATTN_W), F32), jax.ShapeDtypeStruct((T, ATTN_W), F32),
                   jax.ShapeDtypeStruct((T, CONV_W), F32), jax.ShapeDtypeStruct((T, LRU_W), F32),
                   jax.ShapeDtypeStruct((D_MODEL, D_MODEL), BF16)],
        scratch_shapes=[pltpu.VMEM((D_MODEL, D_MODEL), F32)],
        compiler_params=_params("arbitrary"),
    )(dx1, mix, w, o, lse)


DZ_COLS = ((0, ATTN_W), (ATTN_W, 2 * ATTN_W), (2 * ATTN_W, QKV_W), (QKV_W, QKV_W + CONV_IN_W), (QKV_W + CONV_IN_W, IN_COLS))


def _inproj_bwd_w(dz_parts, h):
    T = h.shape[0]
    tm = GRAD_ROW_TILE
    nk = T // tm
    n_parts = len(DZ_COLS)

    def body(*refs):
        dz_refs = refs[:n_parts]
        h_ref, dw_ref, acc_ref = refs[n_parts:]
        i = pl.program_id(0)
        hb = h_ref[...]

        @pl.when(i == 0)
        def _():
            acc_ref[...] = jnp.zeros_like(acc_ref)

        for r, (lo, hi) in zip(dz_refs, DZ_COLS):
            acc_ref[:, lo:hi] += _dot_tn(hb, r[...].astype(BF16))

        @pl.when(i == nk - 1)
        def _():
            dw_ref[...] = acc_ref[...].astype(BF16)

    rows = lambda width: pl.BlockSpec((tm, width), lambda i: (i, 0))
    return pl.pallas_call(
        body, name="inproj_bwd_w", grid=(nk,),
        in_specs=[rows(hi - lo) for lo, hi in DZ_COLS] + [rows(D_MODEL)],
        out_specs=_resident((D_MODEL, IN_COLS)),
        out_shape=jax.ShapeDtypeStruct((D_MODEL, IN_COLS), BF16),
        scratch_shapes=[pltpu.VMEM((D_MODEL, IN_COLS), F32)],
        compiler_params=_params("arbitrary"),
    )(*dz_parts, h)


def _inproj_bwd_act(dz_parts, w, x2d, g, dx1, ex=None):
    T = x2d.shape[0]
    tm = ROW_TILE
    n_parts = len(DZ_COLS)

    def body(*refs):
        dz_refs = refs[:n_parts]
        w_ref, x_ref, g_ref, dx1_ref, dx_ref, dg_ref = refs[n_parts:]
        dh = _dot_nt(dz_refs[0][...].astype(BF16), w_ref[:, DZ_COLS[0][0]:DZ_COLS[0][1]])
        for r, (lo, hi) in zip(dz_refs[1:], DZ_COLS[1:]):
            dh = dh + _dot_nt(r[...].astype(BF16), w_ref[:, lo:hi])
        gv = g_ref[...]
        _, xhat, rstd = _rms_fwd(x_ref[...], gv)
        dx, dg = _rms_bwd(dh, xhat, rstd, gv)
        dx_ref[...] = dx1_ref[...] + dx

        @pl.when(pl.program_id(0) == 0)
        def _():
            dg_ref[...] = dg

        @pl.when(pl.program_id(0) != 0)
        def _():
            dg_ref[...] += dg

    rows = lambda width: pl.BlockSpec((tm, width), lambda i: (i, 0))
    return _pallas_hosting(
        body, ex, name="inproj_bwd_act", grid=(T // tm,),
        in_specs=[rows(hi - lo) for lo, hi in DZ_COLS] + [
            _resident((D_MODEL, IN_COLS)), rows(D_MODEL), pl.BlockSpec((1, D_MODEL), lambda i: (0, 0)), rows(D_MODEL)],
        out_specs=[rows(D_MODEL), pl.BlockSpec((1, D_MODEL), lambda i: (0, 0))],
        out_shape=[jax.ShapeDtypeStruct((T, D_MODEL), F32), jax.ShapeDtypeStruct((1, D_MODEL), F32)],
        scratch_shapes=[], semantics=("arbitrary",), operands=(*dz_parts, w, x2d, g, dx1))


def _alibi_coef():
    slopes = 2.0 ** (-8.0 * np.arange(1, 7) / 6)
    return jnp.asarray((slopes.reshape(3, 2) * np.asarray(ATTN_DILATIONS)[:, None]).astype(np.float32))


def _unit_rows(u, d):
    nb = N_UNITS // d
    r, n = u // nb, u % nb
    span = ATTN_BLOCK * d

    def rows(block):
        start = block * span + r
        return pl.ds(pl.multiple_of(start, ATTN_BLOCK), ATTN_BLOCK) if d == 1 else pl.ds(start, ATTN_BLOCK, stride=d)

    return rows(n), rows(jnp.maximum(n - 1, 0)), rows(jnp.minimum(n + 1, nb - 1)), n > 0, n + 1 < nb


def _per_pattern(fn):
    for p, d in enumerate(ATTN_DILATIONS):
        pl.when(pl.program_id(1) == p)(functools.partial(fn, p, d))


def _attn_col(offset):
    return pl.BlockSpec((None, SEQ_LEN, 128), lambda b, p: (b, 0, p + offset))


def _attn_masks():
    qi = lax.broadcasted_iota(jnp.int32, (ATTN_BLOCK, 2 * ATTN_BLOCK), 0)
    kj = lax.broadcasted_iota(jnp.int32, (ATTN_BLOCK, 2 * ATTN_BLOCK), 1)
    dist = qi + ATTN_BLOCK - kj
    first = lax.broadcasted_iota(jnp.int32, (ATTN_BLOCK, 128), 1) < HEAD_DIM
    return dist.astype(F32), (dist >= 0) & (dist <= ATTN_BLOCK), kj >= ATTN_BLOCK, first


def _head_lanes(a, first, j):
    return jnp.where(first if j == 0 else jnp.logical_not(first), a, jnp.zeros_like(a))


def _load_kv(ref, prev, own):
    return jnp.concatenate([ref[prev, :], ref[own, :]], axis=0).astype(BF16)


def _attn_fwd(qkv, ex=None):
    B = qkv.shape[0]

    def body(coef_ref, q_ref, k_ref, v_ref, o_ref, lse_ref):
        dist, band, own_half, first = _attn_masks()

        def pattern(p, d):
            def unit(u, carry):
                own, prev, _, has_prev, _ = _unit_rows(u, d)
                ok = band & jnp.logical_or(own_half, has_prev)
                q = q_ref[own, :].astype(BF16)
                kcat, vcat = _load_kv(k_ref, prev, own), _load_kv(v_ref, prev, own)
                outs, lses = [], []
                for j in range(2):
                    s = jnp.where(ok, _dot_nt(_head_lanes(q, first, j), kcat) * 0.125 - coef_ref[p, j] * dist, NEG_BIG)
                    m = jnp.max(s, axis=-1, keepdims=True)
                    e = jnp.exp(s - m)
                    l = jnp.sum(e, axis=-1, keepdims=True)
                    outs.append(_dot(e.astype(BF16), vcat) * (1.0 / l))
                    lses.append(m + jnp.log(l))
                o_ref[own, :] = jnp.where(first, outs[0], outs[1])
                lse_ref[own, :] = jnp.where(first, lses[0], lses[1])
                return carry

            lax.fori_loop(0, N_UNITS, unit, 0, unroll=UNIT_UNROLL)

        _per_pattern(pattern)

    shape = jax.ShapeDtypeStruct((B, SEQ_LEN, ATTN_W), F32)
    return _pallas_hosting(
        body, ex, name="attn_fwd", grid=(B, 3),
        in_specs=[pl.BlockSpec(memory_space=pltpu.SMEM), _attn_col(0), _attn_col(3), _attn_col(6)],
        out_specs=[_attn_col(0), _attn_col(0)],
        out_shape=[shape, shape],
        scratch_shapes=[], semantics=("parallel", "parallel"), operands=(_alibi_coef(), qkv, qkv, qkv))


def _attn_bwd(qkv, do, lse, dd, ex=None):
    B = qkv.shape[0]

    def body(coef_ref, q_ref, k_ref, v_ref, do_ref, lse_ref, dd_ref, dq_ref, dk_ref, dv_ref):
        dist, band, own_half, first = _attn_masks()

        def pattern(p, d):
            def unit(u, carry):
                own, prev, _, has_prev, _ = _unit_rows(u, d)
                ok = band & jnp.logical_or(own_half, has_prev)
                q, do = q_ref[own, :].astype(BF16), do_ref[own, :].astype(BF16)
                kcat, vcat = _load_kv(k_ref, prev, own), _load_kv(v_ref, prev, own)
                lse_a, dd_a = lse_ref[own, :], dd_ref[own, :]
                dqs, dks, dvs = [], [], []
                for j in range(2):
                    col = slice(HEAD_DIM * j, HEAD_DIM * j + 1)
                    s = _dot_nt(_head_lanes(q, first, j), kcat) * 0.125 - coef_ref[p, j] * dist
                    pr = jnp.where(ok, jnp.exp(jnp.where(ok, s, NEG_BIG) - lse_a[:, col]), 0.0)
                    ds = (pr * (_dot_nt(_head_lanes(do, first, j), vcat) + dd_a[:, col])).astype(BF16)
                    dqs.append(_dot(ds, kcat))
                    dks.append(_dot_tn(ds, q))
                    dvs.append(_dot_tn(pr.astype(BF16), do))
                both = lambda pair: jnp.where(jnp.concatenate([first] * (pair[0].shape[0] // ATTN_BLOCK), axis=0), *pair)
                dq_ref[own, :] = both(dqs) * 0.125
                dk, dv = both(dks) * 0.125, both(dvs)
                dk_ref[own, :] = dk[ATTN_BLOCK:]
                dv_ref[own, :] = dv[ATTN_BLOCK:]
                dk_ref[prev, :] += dk[:ATTN_BLOCK]
                dv_ref[prev, :] += dv[:ATTN_BLOCK]
                return carry

            lax.fori_loop(0, N_UNITS, unit, 0, unroll=UNIT_UNROLL)

        _per_pattern(pattern)

    shape = jax.ShapeDtypeStruct((B, SEQ_LEN, ATTN_W), F32)
    return _pallas_hosting(
        body, ex, name="attn_bwd", grid=(B, 3),
        in_specs=[pl.BlockSpec(memory_space=pltpu.SMEM), _attn_col(0), _attn_col(3), _attn_col(6), _attn_col(0), _attn_col(0),
                  _attn_col(0)],
        out_specs=[_attn_col(0)] * 3,
        out_shape=[shape] * 3,
        scratch_shapes=[], semantics=("parallel", "parallel"), operands=(_alibi_coef(), qkv, qkv, qkv, do, lse, dd))


def _for_chunks(n_rows, fn, chunk):
    def step(c, carry):
        fn(pl.multiple_of(c * chunk, chunk))
        return carry

    lax.fori_loop(0, n_rows // chunk, step, 0)


def _shift_down(win, s, rows):
    lead = win.shape[0] - rows
    if s == 0:
        return win[lead:]
    if s % 8 == 0:
        return win[lead - s:lead - s + rows]
    q, r = divmod(s, 8)
    rolled = pltpu.roll(win, r, 0)
    return rolled[lead - 8 * q:lead - 8 * q + rows]


def _tap_shifts(n_taps):
    return [(r, [(n_taps - 1 - (8 * q + r), 8 * q) for q in range((n_taps - 1 - r) // 8 + 1)]) for r in range(min(8, n_taps))]


def _rotated_down(win, r):
    return win if r == 0 else pltpu.roll(win, r, 0)


def _rotated_up(win, r):
    return win if r == 0 else pltpu.roll(win, win.shape[0] - r, 0)


def _shift_up(win, s, rows):
    if s % 8 == 0:
        return win[s:s + rows]
    q, r = divmod(s, 8)
    rolled = pltpu.roll(win, win.shape[0] - r, 0)
    return rolled[8 * q:8 * q + rows]


CONV_PAD = 32


def _ln_silu(c, lg, lb):
    mu = jnp.mean(c, axis=-1, keepdims=True)
    cc = c - mu
    rstd = lax.rsqrt(jnp.mean(cc * cc, axis=-1, keepdims=True) + LN_EPS)
    nrm = cc * rstd
    v = nrm * lg + lb
    sg = _sigmoid(v)
    return v * sg, nrm, rstd, v, sg


def _conv_fwd(ci, w, b, lg, lb):
    B, S, _ = ci.shape
    CH = CONV_FWD_CHUNK

    def body(ci_ref, w_ref, b_ref, lg_ref, lb_ref, y_ref, c_ref, pad_ref):
        pad_ref[0:CONV_PAD, :] = jnp.zeros((CONV_PAD, CONV_W), F32)

        def glu(base):
            blk = ci_ref[pl.ds(base, CH), :]
            pad_ref[pl.ds(CONV_PAD + base, CH), :] = blk[:, 0:CONV_W] * _sigmoid(blk[:, CONV_W:])

        _for_chunks(S, glu, CH)

        def conv(base):
            win = pad_ref[pl.ds(base, CH + CONV_PAD), :]
            acc = jnp.broadcast_to(b_ref[...], (CH, CONV_W))
            for r, taps in _tap_shifts(CONV_TAPS):
                rot = _rotated_down(win, r)
                for k, off in taps:
                    acc = acc + w_ref[k:k + 1, :] * rot[CONV_PAD - off:CONV_PAD - off + CH]
            c_ref[pl.ds(base, CH), :] = acc
            y, _, _, _, _ = _ln_silu(acc, lg_ref[...], lb_ref[...])
            y_ref[pl.ds(base, CH), :] = y.astype(BF16)

        _for_chunks(S, conv, CH)

    vec = pl.BlockSpec((1, CONV_W), lambda i: (0, 0))
    return pl.pallas_call(
        body, name="conv_fwd", grid=(B,),
        in_specs=[pl.BlockSpec((None, S, CONV_IN_W), lambda i: (i, 0, 0)),
                  pl.BlockSpec((CONV_TAPS, CONV_W), lambda i: (0, 0)), vec, vec, vec],
        out_specs=[pl.BlockSpec((None, S, CONV_W), lambda i: (i, 0, 0)),
                   pl.BlockSpec((None, S, CONV_W), lambda i: (i, 0, 0))],
        out_shape=[jax.ShapeDtypeStruct((B, S, CONV_W), BF16), jax.ShapeDtypeStruct((B, S, CONV_W), F32)],
        scratch_shapes=[pltpu.VMEM((S + CONV_PAD, CONV_W), F32)],
        compiler_params=_params("parallel"),
    )(ci, w, b, lg, lb)


def _conv_bwd(ci, cpre, dy, w, lg, lb, ex=None):
    B, S, _ = ci.shape
    CH = CONV_BWD_CHUNK

    def body(ci_ref, c_ref, dy_ref, w_ref, lg_ref, lb_ref, dci_ref, dw_ref, db_ref, dlg_ref, dlb_ref, upad_ref, dcpad_ref,
             dwacc_ref):
        @pl.when(pl.program_id(0) == 0)
        def _():
            dw_ref[...] = jnp.zeros_like(dw_ref)
            db_ref[...] = jnp.zeros_like(db_ref)
            dlg_ref[...] = jnp.zeros_like(dlg_ref)
            dlb_ref[...] = jnp.zeros_like(dlb_ref)

        upad_ref[0:CONV_PAD, :] = jnp.zeros((CONV_PAD, CONV_W), F32)
        dcpad_ref[S:S + CONV_PAD, :] = jnp.zeros((CONV_PAD, CONV_W), F32)
        dwacc_ref[...] = jnp.zeros_like(dwacc_ref)

        def norm_bwd(base):
            blk = ci_ref[pl.ds(base, CH), :]
            upad_ref[pl.ds(CONV_PAD + base, CH), :] = blk[:, 0:CONV_W] * _sigmoid(blk[:, CONV_W:])
            lgv = lg_ref[...]
            _, nrm, rstd, v, sg = _ln_silu(c_ref[pl.ds(base, CH), :], lgv, lb_ref[...])
            dv = dy_ref[pl.ds(base, CH), :] * (sg * (1.0 + v * (1.0 - sg)))
            dlg_ref[...] += jnp.sum(dv * nrm, axis=0, keepdims=True)
            dlb_ref[...] += jnp.sum(dv, axis=0, keepdims=True)
            dn = dv * lgv
            dc = rstd * (dn - jnp.mean(dn, axis=-1, keepdims=True) - nrm * jnp.mean(dn * nrm, axis=-1, keepdims=True))
            dcpad_ref[pl.ds(base, CH), :] = dc
            db_ref[...] += jnp.sum(dc, axis=0, keepdims=True)

        _for_chunks(S, norm_bwd, CH)

        def conv_bwd(base):
            dwin = dcpad_ref[pl.ds(base, CH + CONV_PAD), :]
            uwin = upad_ref[pl.ds(base, CH + CONV_PAD), :]
            dc = dwin[0:CH]
            du = jnp.zeros((CH, CONV_W), F32)
            for r, taps in _tap_shifts(CONV_TAPS):
                d_rot, u_rot = _rotated_up(dwin, r), _rotated_down(uwin, r)
                for k, off in taps:
                    du = du + w_ref[k:k + 1, :] * d_rot[off:off + CH]
                    prod = dc * u_rot[CONV_PAD - off:CONV_PAD - off + CH]
                    dwacc_ref[8 * k:8 * k + 8, :] += jnp.sum(prod.reshape(CH // 8, 8, CONV_W), axis=0)
            blk = ci_ref[pl.ds(base, CH), :]
            a, sg = blk[:, 0:CONV_W], _sigmoid(blk[:, CONV_W:])
            dci_ref[pl.ds(base, CH), 0:CONV_W] = (du * sg).astype(BF16)
            dci_ref[pl.ds(base, CH), CONV_W:] = (du * a * sg * (1.0 - sg)).astype(BF16)

        _for_chunks(S, conv_bwd, CH)
        for k in range(CONV_TAPS):
            dw_ref[k:k + 1, :] += jnp.sum(dwacc_ref[8 * k:8 * k + 8, :], axis=0, keepdims=True)

    vec = pl.BlockSpec((1, CONV_W), lambda i: (0, 0))
    mat = pl.BlockSpec((CONV_TAPS, CONV_W), lambda i: (0, 0))
    seq = lambda width: pl.BlockSpec((None, S, width), lambda i: (i, 0, 0))
    return _pallas_hosting(
        body, ex, name="conv_bwd", grid=(B,),
        in_specs=[seq(CONV_IN_W), seq(CONV_W), seq(CONV_W), mat, vec, vec],
        out_specs=[seq(CONV_IN_W), mat, vec, vec, vec],
        out_shape=[jax.ShapeDtypeStruct((B, S, CONV_IN_W), BF16), jax.ShapeDtypeStruct((CONV_TAPS, CONV_W), F32),
                   jax.ShapeDtypeStruct((1, CONV_W), F32), jax.ShapeDtypeStruct((1, CONV_W), F32),
                   jax.ShapeDtypeStruct((1, CONV_W), F32)],
        scratch_shapes=[pltpu.VMEM((S + CONV_PAD, CONV_W), F32), pltpu.VMEM((S + CONV_PAD, CONV_W), F32),
                        pltpu.VMEM((8 * CONV_TAPS, CONV_W), F32)],
        semantics=("arbitrary",), operands=(ci, cpre, dy, w, lg, lb))


SCAN_SHIFTS = tuple(1 << e for e in range(11))


def _prev8(ref, base, cols, fill):
    start = pl.multiple_of(jnp.maximum(base - 8, 0), 8)
    return jnp.where(base > 0, ref[pl.ds(start, 8), cols], fill)


def _next8(ref, base, rows, total, cols, fill):
    start = pl.multiple_of(jnp.minimum(base + rows, total - 8), 8)
    return jnp.where(base + rows < total, ref[pl.ds(start, 8), cols], fill)


ALL = slice(None)
LRU_X = slice(LRU_W, LRU_IN_W)
LRU_GATE = slice(0, LRU_W)


def _lru_conv(li_ref, base, rows, cw_ref, cb_ref):
    win = jnp.concatenate([_prev8(li_ref, base, LRU_X, 0.0), li_ref[pl.ds(base, rows), LRU_X]], axis=0)
    u = jnp.broadcast_to(cb_ref[...], (rows, LRU_W))
    for k in range(LRU_TAPS):
        u = u + cw_ref[k:k + 1, :] * _shift_down(win, LRU_TAPS - 1 - k, rows)
    return u, win


def _lru_gates(u, wa_ref, ba_ref, wx_ref, bx_ref, sp):
    ub = u.astype(BF16)
    r = _sigmoid(_dot(ub, wa_ref[...]) + ba_ref[...])
    i = _sigmoid(_dot(ub, wx_ref[...]) + bx_ref[...])
    la = (-LRU_C) * r * sp
    a = jnp.exp(la)
    return ub, r, i, a, _one_minus_exp(2.0 * la, a * a)


def _scan_forward(bufs, S, CH):
    for n, s in enumerate(SCAN_SHIFTS):
        (sa, sb), (da, db) = bufs[n % 2], bufs[(n + 1) % 2]

        def step(base, s=s, sa=sa, sb=sb, da=da, db=db):
            a, b = sa[pl.ds(base, CH), :], sb[pl.ds(base, CH), :]
            if s < 8:
                a_s = _shift_down(jnp.concatenate([_prev8(sa, base, ALL, 1.0), a], axis=0), s, CH)
                b_s = _shift_down(jnp.concatenate([_prev8(sb, base, ALL, 0.0), b], axis=0), s, CH)
            elif s < CH:
                start = pl.multiple_of(jnp.maximum(base - s, 0), 8)
                a_s = jnp.concatenate([jnp.where(base > 0, sa[pl.ds(start, s), :], 1.0), a[0:CH - s]], axis=0)
                b_s = jnp.concatenate([jnp.where(base > 0, sb[pl.ds(start, s), :], 0.0), b[0:CH - s]], axis=0)
            else:
                start = pl.multiple_of(jnp.maximum(base - s, 0), 8)
                a_s = jnp.where(base < s, 1.0, sa[pl.ds(start, CH), :])
                b_s = jnp.where(base < s, 0.0, sb[pl.ds(start, CH), :])
            db[pl.ds(base, CH), :] = a * b_s + b
            da[pl.ds(base, CH), :] = a * a_s

        _for_chunks(S, step, CH)
    return len(SCAN_SHIFTS) % 2


def _scan_backward(bufs, S, CH):
    for n, s in enumerate(SCAN_SHIFTS):
        (sa, sb), (da, db) = bufs[n % 2], bufs[(n + 1) % 2]

        def step(base, s=s, sa=sa, sb=sb, da=da, db=db):
            a, b = sa[pl.ds(base, CH), :], sb[pl.ds(base, CH), :]
            if s < 8:
                a_s = _shift_up(jnp.concatenate([a, _next8(sa, base, CH, S, ALL, 1.0)], axis=0), s, CH)
                b_s = _shift_up(jnp.concatenate([b, _next8(sb, base, CH, S, ALL, 0.0)], axis=0), s, CH)
            elif s < CH:
                start = pl.multiple_of(jnp.minimum(base + CH, S - s), 8)
                more = base + CH < S
                a_s = jnp.concatenate([a[s:CH], jnp.where(more, sa[pl.ds(start, s), :], 1.0)], axis=0)
                b_s = jnp.concatenate([b[s:CH], jnp.where(more, sb[pl.ds(start, s), :], 0.0)], axis=0)
            else:
                start = pl.multiple_of(jnp.minimum(base + s, S - CH), 8)
                a_s = jnp.where(base + s >= S, 1.0, sa[pl.ds(start, CH), :])
                b_s = jnp.where(base + s >= S, 0.0, sb[pl.ds(start, CH), :])
            db[pl.ds(base, CH), :] = a * b_s + b
            da[pl.ds(base, CH), :] = a * a_s

        _for_chunks(S, step, CH)
    return len(SCAN_SHIFTS) % 2


def _lru_fwd(li, cw, cb, wa, ba, wx, bx, lam, ex=None):
    B, S, _ = li.shape
    CH = LRU_FWD_CHUNK

    def body(li_ref, cw_ref, cb_ref, wa_ref, ba_ref, wx_ref, bx_ref, lam_ref, y_ref, h_ref, a0, b0, a1, b1):
        sp = _softplus(-lam_ref[...])

        def gates(base):
            u, _ = _lru_conv(li_ref, base, CH, cw_ref, cb_ref)
            _, _, i, a, em = _lru_gates(u, wa_ref, ba_ref, wx_ref, bx_ref, sp)
            a0[pl.ds(base, CH), :] = a
            b0[pl.ds(base, CH), :] = jnp.sqrt(em) * (i * u)

        _for_chunks(S, gates, CH)
        bufs = ((a0, b0), (a1, b1))
        hb = bufs[_scan_forward(bufs, S, CH)][1]

        def out(base):
            h = hb[pl.ds(base, CH), :]
            h_ref[pl.ds(base, CH), :] = h
            gl, _ = _gelu(li_ref[pl.ds(base, CH), LRU_GATE])
            y_ref[pl.ds(base, CH), :] = (gl * h).astype(BF16)

        _for_chunks(S, out, CH)

    vec = pl.BlockSpec((1, LRU_W), lambda i: (0, 0))
    mat = pl.BlockSpec((LRU_W, LRU_W), lambda i: (0, 0))
    seq = lambda width: pl.BlockSpec((None, S, width), lambda i: (i, 0, 0))
    return _pallas_hosting(
        body, ex, name="lru_fwd", grid=(B,),
        in_specs=[seq(LRU_IN_W), pl.BlockSpec((LRU_TAPS, LRU_W), lambda i: (0, 0)), vec, mat, vec, mat, vec, vec],
        out_specs=[seq(LRU_W), seq(LRU_W)],
        out_shape=[jax.ShapeDtypeStruct((B, S, LRU_W), BF16), jax.ShapeDtypeStruct((B, S, LRU_W), F32)],
        scratch_shapes=[pltpu.VMEM((S, LRU_W), F32)] * 4,
        semantics=("parallel",), operands=(li, cw, cb, wa, ba, wx, bx, lam))


def _lru_bwd(li, hs, dy, cw, cb, wa, ba, wx, bx, lam, ex=None):
    B, S, _ = li.shape
    CH = LRU_BWD_CHUNK

    def body(li_ref, hs_ref, dy_ref, cw_ref, cb_ref, wa_ref, ba_ref, wx_ref, bx_ref, lam_ref,
             dli_ref, dcw_ref, dcb_ref, dwa_ref, dba_ref, dwx_ref, dbx_ref, dlam_ref, a0, b0, a1, b1, u_s, du_s):
        @pl.when(pl.program_id(0) == 0)
        def _():
            for ref in (dcw_ref, dcb_ref, dwa_ref, dba_ref, dwx_ref, dbx_ref, dlam_ref):
                ref[...] = jnp.zeros_like(ref)

        lam_v = lam_ref[...]
        sp = _softplus(-lam_v)
        dsp_dlam = -_sigmoid(-lam_v)

        def gates(base):
            u, _ = _lru_conv(li_ref, base, CH, cw_ref, cb_ref)
            _, _, _, a, _ = _lru_gates(u, wa_ref, ba_ref, wx_ref, bx_ref, sp)
            gl, _ = _gelu(li_ref[pl.ds(base, CH), LRU_GATE])
            u_s[pl.ds(base, CH), :] = u
            a0[pl.ds(base, CH), :] = a
            b0[pl.ds(base, CH), :] = a * (dy_ref[pl.ds(base, CH), :] * gl)

        _for_chunks(S, gates, CH)
        bufs = ((a0, b0), (a1, b1))
        eb = bufs[_scan_backward(bufs, S, CH)][1]

        def grads(base):
            e = eb[pl.ds(base, CH), :]
            e_next = _shift_up(jnp.concatenate([e, _next8(eb, base, CH, S, ALL, 0.0)], axis=0), 1, CH)
            gate = li_ref[pl.ds(base, CH), LRU_GATE]
            gl, th = _gelu(gate)
            dy = dy_ref[pl.ds(base, CH), :]
            g = dy * gl + e_next
            h = hs_ref[pl.ds(base, CH), :]
            h_prev = _shift_down(jnp.concatenate([_prev8(hs_ref, base, ALL, 0.0), h], axis=0), 1, CH)
            dli_ref[pl.ds(base, CH), LRU_GATE] = (dy * h * _gelu_grad(gate, th)).astype(BF16)
            u = u_s[pl.ds(base, CH), :]
            ub, r, i, a, em = _lru_gates(u, wa_ref, ba_ref, wx_ref, bx_ref, sp)
            mult = jnp.sqrt(em)
            da = g * h_prev
            dmult = g * (i * u)
            di = g * mult * u
            dla = da * a - dmult * (a * a) * lax.rsqrt(jnp.maximum(em, 1e-30))
            dlam_ref[...] += dsp_dlam * jnp.sum(dla * ((-LRU_C) * r), axis=0, keepdims=True)
            dpa = (dla * ((-LRU_C) * sp)) * r * (1.0 - r)
            dpx = di * i * (1.0 - i)
            dpab, dpxb = dpa.astype(BF16), dpx.astype(BF16)
            dwa_ref[...] += _dot_tn(ub, dpab)
            dwx_ref[...] += _dot_tn(ub, dpxb)
            dba_ref[...] += jnp.sum(dpa, axis=0, keepdims=True)
            dbx_ref[...] += jnp.sum(dpx, axis=0, keepdims=True)
            du = g * mult * i + _dot_nt(dpab, wa_ref[...]) + _dot_nt(dpxb, wx_ref[...])
            du_s[pl.ds(base, CH), :] = du
            dcb_ref[...] += jnp.sum(du, axis=0, keepdims=True)

        _for_chunks(S, grads, CH)

        def conv_bwd(base):
            du = du_s[pl.ds(base, CH), :]
            dwin = jnp.concatenate([du, _next8(du_s, base, CH, S, ALL, 0.0)], axis=0)
            xwin = jnp.concatenate([_prev8(li_ref, base, LRU_X, 0.0), li_ref[pl.ds(base, CH), LRU_X]], axis=0)
            dx = jnp.zeros((CH, LRU_W), F32)
            for k in range(LRU_TAPS):
                dx = dx + cw_ref[k:k + 1, :] * _shift_up(dwin, LRU_TAPS - 1 - k, CH)
                dcw_ref[k:k + 1, :] += jnp.sum(du * _shift_down(xwin, LRU_TAPS - 1 - k, CH), axis=0, keepdims=True)
            dli_ref[pl.ds(base, CH), LRU_X] = dx.astype(BF16)

        _for_chunks(S, conv_bwd, CH)

    vec = pl.BlockSpec((1, LRU_W), lambda i: (0, 0))
    mat = pl.BlockSpec((LRU_W, LRU_W), lambda i: (0, 0))
    taps = pl.BlockSpec((LRU_TAPS, LRU_W), lambda i: (0, 0))
    seq = lambda width: pl.BlockSpec((None, S, width), lambda i: (i, 0, 0))
    vec_shape = jax.ShapeDtypeStruct((1, LRU_W), F32)
    mat_shape = jax.ShapeDtypeStruct((LRU_W, LRU_W), F32)
    return _pallas_hosting(
        body, ex, name="lru_bwd", grid=(B,),
        in_specs=[seq(LRU_IN_W), seq(LRU_W), seq(LRU_W), taps, vec, mat, vec, mat, vec, vec],
        out_specs=[seq(LRU_IN_W), taps, vec, mat, vec, mat, vec, vec],
        out_shape=[jax.ShapeDtypeStruct((B, S, LRU_IN_W), BF16), jax.ShapeDtypeStruct((LRU_TAPS, LRU_W), F32),
                   vec_shape, mat_shape, vec_shape, mat_shape, vec_shape, vec_shape],
        scratch_shapes=[pltpu.VMEM((S, LRU_W), F32)] * 6,
        semantics=("arbitrary",), operands=(li, hs, dy, cw, cb, wa, ba, wx, bx, lam))


MESH = pl.DeviceIdType.MESH
HBM_SPEC = pl.BlockSpec(memory_space=pltpu.HBM)


def _slot(ref, p):
    return ref.at[p]


def _row_block(rows):
    return lambda ref, p: ref.at[pl.ds(p * rows, rows), :]


def _col_block(cols):
    return lambda ref, p: ref.at[:, pl.ds(p * cols, cols)]


class _Gather:
    def __init__(self, blocks, out_shapes, places):
        self.sources, self.out_shapes, self.places, self.n = list(blocks), list(out_shapes), list(places), len(blocks)

    def scratch(self):
        return [pltpu.SemaphoreType.DMA((self.n, 7)), pltpu.SemaphoreType.DMA((self.n, 7)), pltpu.SemaphoreType.DMA((self.n,))]

    def _plan(self, x_refs, out_refs, send_sems, recv_sems, local_sems):
        n = self.n
        x, y, c = lax.axis_index("x"), lax.axis_index("y"), lax.axis_index("c")
        me, sibling = (x, y, c), (x, y, 1 - c)
        chips = [(1 - x, y), (x, 1 - y), (1 - x, 1 - y)]

        def place(a, dev):
            return self.places[a](out_refs[a], 4 * dev[0] + 2 * dev[1] + dev[2])

        def copy(a, k, blk, to, src=None):
            return pltpu.make_async_remote_copy(
                src_ref=place(a, blk) if src is None else src, dst_ref=place(a, blk),
                send_sem=send_sems.at[a, k], recv_sem=recv_sems.at[a, k], device_id=to, device_id_type=MESH)

        mine = [pltpu.make_async_copy(x_refs[a], place(a, me), local_sems.at[a]) for a in range(n)]
        first = [copy(a, 0, me, sibling, src=x_refs[a]) for a in range(n)]
        first += [copy(a, 1 + j, me, (*chip, c), src=x_refs[a]) for j, chip in enumerate(chips) for a in range(n)]
        return me, sibling, chips, c, copy, mine, first

    def start(self, *refs):
        *_, mine, first = self._plan(*refs)
        for cp in mine + first:
            cp.start()

    def forward(self, *refs):
        me, sibling, chips, c, copy, _, _ = self._plan(*refs)
        for j, chip in enumerate(chips):
            for a in range(self.n):
                copy(a, 1 + j, (*chip, c), me).wait_recv()
                copy(a, 4 + j, (*chip, c), sibling).start()

    def finish(self, *refs):
        me, sibling, chips, c, copy, mine, first = self._plan(*refs)
        passed = [copy(a, 4 + j, (*chip, c), sibling) for j, chip in enumerate(chips) for a in range(self.n)]
        for a in range(self.n):
            copy(a, 0, sibling, me).wait_recv()
        for j, chip in enumerate(chips):
            for a in range(self.n):
                copy(a, 4 + j, (*chip, 1 - c), me).wait_recv()
        for cp in first + passed:
            cp.wait_send()
        for cp in mine:
            cp.wait()


class _GradExchange:
    def __init__(self, sources, takes, piece_shapes):
        self.sources, self.takes, self.n = list(sources), list(takes), len(sources)
        self.out_shapes = [jax.ShapeDtypeStruct((N_DEV,) + tuple(s), BF16) for s in piece_shapes]

    def scratch(self):
        return [pltpu.SemaphoreType.DMA((self.n, 7)), pltpu.SemaphoreType.DMA((self.n, 7)), pltpu.SemaphoreType.DMA((self.n,))]

    def _copies(self, src_refs, out_refs, send_sems, recv_sems, local_sems):
        x, y, c = lax.axis_index("x"), lax.axis_index("y"), lax.axis_index("c")
        me = 4 * x + 2 * y + c
        mine = [pltpu.make_async_copy(self.takes[i](src_refs[i], me), out_refs[i].at[me], local_sems.at[i]) for i in range(self.n)]
        remote = []
        for k in range(1, N_DEV):
            px, py, pc = x ^ ((k >> 2) & 1), y ^ ((k >> 1) & 1), c ^ (k & 1)
            peer = 4 * px + 2 * py + pc
            for i in range(self.n):
                remote.append(pltpu.make_async_remote_copy(
                    src_ref=self.takes[i](src_refs[i], peer), dst_ref=out_refs[i].at[me], send_sem=send_sems.at[i, k - 1],
                    recv_sem=recv_sems.at[i, k - 1], device_id=(px, py, pc), device_id_type=MESH))
        return mine, remote

    def start(self, *refs):
        mine, remote = self._copies(*refs)
        for cp in mine + remote:
            cp.start()

    def forward(self, *refs):
        pass

    def finish(self, *refs):
        mine, remote = self._copies(*refs)
        for cp in remote:
            cp.wait_recv()
        for cp in remote:
            cp.wait_send()
        for cp in mine:
            cp.wait()


def _run_exchange(ex, name):
    def body(*refs):
        src_refs, out_refs, sems = refs[:ex.n], refs[ex.n:2 * ex.n], refs[2 * ex.n:]
        ex.start(src_refs, out_refs, *sems)
        ex.forward(src_refs, out_refs, *sems)
        ex.finish(src_refs, out_refs, *sems)

    return pl.pallas_call(
        body, name=name, out_shape=ex.out_shapes, in_specs=[HBM_SPEC] * ex.n, out_specs=[HBM_SPEC] * ex.n,
        scratch_shapes=ex.scratch(),
    )(*ex.sources)


def _pallas_hosting(body, ex, *, name, grid, in_specs, out_specs, out_shape, scratch_shapes, semantics, operands):
    if ex is None:
        outs = pl.pallas_call(body, name=name, grid=grid, in_specs=in_specs, out_specs=out_specs, out_shape=out_shape,
                              scratch_shapes=scratch_shapes, compiler_params=_params(*semantics))(*operands)
        return outs, None
    n_in, n_out, n_scr, n = len(in_specs), len(out_specs), len(scratch_shapes), ex.n

    def at_step(pick):
        conds = [pl.program_id(k) == pick(size) for k, size in enumerate(grid)]
        return functools.reduce(jnp.logical_and, conds)

    def hosting(*refs):
        ins, ex_ins = refs[:n_in], refs[n_in:n_in + n]
        outs, ex_outs = refs[n_in + n:n_in + n + n_out], refs[n_in + n + n_out:n_in + 2 * n + n_out]
        scratch, sems = refs[n_in + 2 * n + n_out:n_in + 2 * n + n_out + n_scr], refs[n_in + 2 * n + n_out + n_scr:]
        pl.when(at_step(lambda size: 0))(lambda: ex.start(ex_ins, ex_outs, *sems))
        pl.when(at_step(lambda size: size - 1))(lambda: ex.forward(ex_ins, ex_outs, *sems))
        body(*ins, *outs, *scratch)
        pl.when(at_step(lambda size: size - 1))(lambda: ex.finish(ex_ins, ex_outs, *sems))

    res = pl.pallas_call(
        hosting, name=name, grid=grid, in_specs=list(in_specs) + [HBM_SPEC] * n, out_specs=list(out_specs) + [HBM_SPEC] * n,
        out_shape=list(out_shape) + ex.out_shapes, scratch_shapes=list(scratch_shapes) + ex.scratch(),
        compiler_params=_params(*(["arbitrary"] * len(grid))),
    )(*operands, *ex.sources)
    return res[:n_out], res[n_out:]


def _adamw(w, g, m, v, rows_per_step, name):
    R, C = w.shape
    c1 = 1.0 - ADAM_B1 ** ADAM_STEP
    c2 = 1.0 - ADAM_B2 ** ADAM_STEP

    def body(w_ref, g_ref, m_ref, v_ref, d_ref, nm_ref, nv_ref):
        gv = g_ref[...]
        nm = ADAM_B1 * m_ref[...] + (1.0 - ADAM_B1) * gv
        nv = ADAM_B2 * v_ref[...] + (1.0 - ADAM_B2) * (gv * gv)
        nm_ref[...] = nm
        nv_ref[...] = nv
        d_ref[...] = (-ADAM_LR) * ((nm / c1) / (jnp.sqrt(nv / c2) + ADAM_EPS) + ADAM_WD * w_ref[...])

    spec = pl.BlockSpec((rows_per_step, C), lambda i: (i, 0))
    shape = jax.ShapeDtypeStruct((R, C), F32)
    return pl.pallas_call(
        body, name=name, grid=(R // rows_per_step,),
        in_specs=[spec] * 4, out_specs=[spec] * 3, out_shape=[shape] * 3,
        compiler_params=_params("parallel"),
    )(w, g, m, v)


def _adam_update(w, g, m, v):
    nm = ADAM_B1 * m + (1.0 - ADAM_B1) * g
    nv = ADAM_B2 * v + (1.0 - ADAM_B2) * (g * g)
    c1 = 1.0 - ADAM_B1 ** ADAM_STEP
    c2 = 1.0 - ADAM_B2 ** ADAM_STEP
    return (-ADAM_LR) * ((nm / c1) / (jnp.sqrt(nv / c2) + ADAM_EPS) + ADAM_WD * w), nm, nv


def _small_sum_adamw(gathered, loss_slots, state):
    names, updated = list(gathered), list(state)
    n_g, n_u = len(names), len(updated)

    def slot_sum(ref):
        acc = ref[0]
        for q in range(1, N_DEV):
            acc = acc + ref[q]
        return acc

    def body(*refs):
        p_refs, loss_in = refs[:n_g], refs[n_g]
        st = refs[n_g + 1:n_g + 1 + 3 * n_u]
        outs = refs[n_g + 1 + 3 * n_u:]
        g_refs, loss_out, upd = outs[:n_g], outs[n_g], outs[n_g + 1:]
        loss_out[...] = slot_sum(loss_in)
        for i, n in enumerate(names):
            g = slot_sum(p_refs[i])
            g_refs[i][...] = g
            if n in state:
                j = updated.index(n)
                d, nm, nv = _adam_update(st[3 * j][...], g, st[3 * j + 1][...], st[3 * j + 2][...])
                upd[3 * j][...], upd[3 * j + 1][...], upd[3 * j + 2][...] = d, nm, nv

    g_shapes = [jax.ShapeDtypeStruct(gathered[n].shape[1:], F32) for n in names]
    u_shapes = [jax.ShapeDtypeStruct(state[n][0].shape, F32) for n in updated for _ in range(3)]
    outs = pl.pallas_call(
        body, name="small_sum_adamw", out_shape=g_shapes + [jax.ShapeDtypeStruct((8, 128), F32)] + u_shapes,
        compiler_params=_params(),
    )(*[gathered[n] for n in names], loss_slots, *[a for n in updated for a in state[n]])
    g = dict(zip(names, outs[:n_g]))
    upd = {n: tuple(outs[n_g + 1 + 3 * j:n_g + 4 + 3 * j]) for j, n in enumerate(updated)}
    return g, outs[n_g], upd


WEIGHT_ORDER = ("norm1_g", "w_in", "conv_dw_w", "conv_dw_b", "conv_ln_g", "conv_ln_b", "lru_conv_w", "lru_conv_b", "lru_wa",
                "lru_ba", "lru_wx", "lru_bx", "lru_lambda", "w_out", "norm2_g", "w_up", "w_down", "final_g")
BIG = ("w_in", "w_out", "w_up", "w_down")
SMALL_SHARDED = {"conv_dw_w": (DEPTH, CONV_TAPS, CONV_W), "lru_conv_w": (DEPTH, LRU_TAPS, LRU_W)}
SMALL_FULL = {
    "norm1_g": (DEPTH, D_MODEL), "conv_dw_w": (DEPTH, CONV_TAPS, CONV_W), "conv_dw_b": (DEPTH, CONV_W),
    "conv_ln_g": (DEPTH, CONV_W), "conv_ln_b": (DEPTH, CONV_W), "lru_conv_w": (DEPTH, LRU_TAPS, LRU_W),
    "lru_conv_b": (DEPTH, LRU_W), "lru_wa": (DEPTH, LRU_HEADS, HEAD_DIM, HEAD_DIM), "lru_ba": (DEPTH, LRU_W),
    "lru_wx": (DEPTH, LRU_HEADS, HEAD_DIM, HEAD_DIM), "lru_bx": (DEPTH, LRU_W), "lru_lambda": (DEPTH, LRU_W),
    "norm2_g": (DEPTH, D_MODEL), "final_g": (D_MODEL,),
}
SMALL_COLS = 128
FILTER_ROWS = 24
W_IN_SHARD = IN_COLS // N_DEV
W_OUT_SHARD = D_MODEL // N_DEV
FF_SHARD = D_FF // N_DEV


def _pack_rows(flat_parts, cols, rows):
    flat = jnp.concatenate(flat_parts)
    return jnp.pad(flat, (0, rows * cols - flat.shape[0])).reshape(rows, cols)


WEIGHT_GATHER = {
    "w_in": ((N_DEV, D_MODEL, W_IN_SHARD), _slot),
    "w_out": ((D_MODEL, D_MODEL), _row_block(W_OUT_SHARD)),
    "w_up": ((D_MODEL, D_FF), _col_block(FF_SHARD)),
    "w_down": ((D_FF, D_MODEL), _row_block(FF_SHARD)),
}
GATHER_HOSTS = {
    "inproj": (("w_out", 0),), "attn_fwd": (("w_up", 0),), "lru_fwd": (("w_down", 0),),
    "outproj": (("w_in", 1),), "up": (("w_up", 1),), "down": (("w_down", 1), ("w_out", 1)),
}


def _weight_gather(local, items, with_filters=False):
    blocks = [local[n][l].astype(BF16) for n, l in items]
    shapes = [jax.ShapeDtypeStruct(WEIGHT_GATHER[n][0], BF16) for n, _ in items]
    places = [WEIGHT_GATHER[n][1] for n, _ in items]
    if with_filters:
        blocks.append(_pack_rows([local[n].reshape(-1) for n in SMALL_SHARDED], SMALL_COLS, FILTER_ROWS))
        shapes.append(jax.ShapeDtypeStruct((N_DEV, FILTER_ROWS, SMALL_COLS), F32))
        places.append(_slot)
    return _Gather(blocks, shapes, places)


def _keep_gathered(full, items, landed):
    for (n, l), arr in zip(items, landed):
        full[n][l] = arr.transpose(1, 0, 2).reshape(D_MODEL, IN_COLS) if n == "w_in" else arr


def _unpack_filters(slots):
    flat, off, out = slots.reshape(N_DEV, -1), 0, {}
    for n, shp in SMALL_SHARDED.items():
        shard = shp[:-1] + (shp[-1] // N_DEV,)
        size = int(np.prod(shard))
        out[n] = jnp.moveaxis(flat[:, off:off + size].reshape((N_DEV,) + shard), 0, -2).reshape(shp)
        off += size
    return out


def _grad_exchange(name, dw):
    if name == "w_in":
        return _GradExchange([dw.reshape(D_MODEL, N_DEV, W_IN_SHARD).transpose(1, 0, 2)], [_slot], [(D_MODEL, W_IN_SHARD)])
    if name == "w_out":
        return _GradExchange([dw], [_row_block(W_OUT_SHARD)], [(W_OUT_SHARD, D_MODEL)])
    return _GradExchange([dw], [_col_block(FF_SHARD)], [(D_MODEL, FF_SHARD)])


def _sum_adamw(parts, w, m, v, rows_per_step, transposed, name):
    _, R, C = parts[0].shape
    tr = rows_per_step
    steps = R // tr
    c1 = 1.0 - ADAM_B1 ** ADAM_STEP
    c2 = 1.0 - ADAM_B2 ** ADAM_STEP

    def slot_sum(p_ref):
        acc = p_ref[0].astype(F32)
        for q in range(1, N_DEV):
            acc = acc + p_ref[q].astype(F32)
        return acc

    def body(p0_ref, p1_ref, w_ref, m_ref, v_ref, g_ref, d_ref, nm_ref, nv_ref):
        gv = jnp.where(pl.program_id(0) == 0, slot_sum(p0_ref), slot_sum(p1_ref))
        if transposed:
            gv = gv.T
        nm = ADAM_B1 * m_ref[...] + (1.0 - ADAM_B1) * gv
        nv = ADAM_B2 * v_ref[...] + (1.0 - ADAM_B2) * (gv * gv)
        g_ref[...] = gv
        nm_ref[...] = nm
        nv_ref[...] = nv
        d_ref[...] = (-ADAM_LR) * ((nm / c1) / (jnp.sqrt(nv / c2) + ADAM_EPS) + ADAM_WD * w_ref[...])

    if transposed:
        spec = pl.BlockSpec((None, C, tr), lambda l, i: (l, 0, i))
    else:
        spec = pl.BlockSpec((None, tr, C), lambda l, i: (l, i, 0))
    shape = jax.ShapeDtypeStruct(w.shape, F32)
    part0 = pl.BlockSpec((N_DEV, tr, C), lambda l, i: (0, jnp.where(l == 0, i, steps - 1), 0))
    part1 = pl.BlockSpec((N_DEV, tr, C), lambda l, i: (0, jnp.where(l == 1, i, 0), 0))
    return pl.pallas_call(
        body, name=name, grid=(DEPTH, steps),
        in_specs=[part0, part1, spec, spec, spec],
        out_specs=[spec] * 4, out_shape=[shape] * 4,
        compiler_params=_params("arbitrary", "arbitrary"),
    )(parts[0], parts[1], w, m, v)


def _block_diag(w):
    eye = jnp.eye(LRU_HEADS, dtype=bool)
    return jnp.where(eye[:, None, :, None], w[:, :, None, :], jnp.zeros((), w.dtype)).reshape(LRU_W, LRU_W)


def _diag_blocks(m):
    eye = jnp.eye(LRU_HEADS, dtype=bool)
    m4 = m.reshape(LRU_HEADS, HEAD_DIM, LRU_HEADS, HEAD_DIM)
    return jnp.sum(jnp.where(eye[:, None, :, None], m4, 0.0), axis=2)


def kernel(x, norm1_g, w_in, conv_dw_w, conv_dw_b, conv_ln_g, conv_ln_b, lru_conv_w, lru_conv_b, lru_wa, lru_ba, lru_wx, lru_bx, lru_lambda, w_out, norm2_g, w_up, w_down, final_g, loss_target, m_norm1_g, m_w_in, m_conv_dw_w, m_conv_dw_b, m_conv_ln_g, m_conv_ln_b, m_lru_conv_w, m_lru_conv_b, m_lru_wa, m_lru_ba, m_lru_wx, m_lru_bx, m_lru_lambda, m_w_out, m_norm2_g, m_w_up, m_w_down, m_final_g, v_norm1_g, v_w_in, v_conv_dw_w, v_conv_dw_b, v_conv_ln_g, v_conv_ln_b, v_lru_conv_w, v_lru_conv_b, v_lru_wa, v_lru_ba, v_lru_wx, v_lru_bx, v_lru_lambda, v_w_out, v_norm2_g, v_w_up, v_w_down, v_final_g):
    local = dict(zip(WEIGHT_ORDER, (norm1_g, w_in, conv_dw_w, conv_dw_b, conv_ln_g, conv_ln_b, lru_conv_w, lru_conv_b, lru_wa,
                                    lru_ba, lru_wx, lru_bx, lru_lambda, w_out, norm2_g, w_up, w_down, final_g)))
    mom1 = dict(zip(WEIGHT_ORDER, (m_norm1_g, m_w_in, m_conv_dw_w, m_conv_dw_b, m_conv_ln_g, m_conv_ln_b, m_lru_conv_w,
                                   m_lru_conv_b, m_lru_wa, m_lru_ba, m_lru_wx, m_lru_bx, m_lru_lambda, m_w_out, m_norm2_g,
                                   m_w_up, m_w_down, m_final_g)))
    mom2 = dict(zip(WEIGHT_ORDER, (v_norm1_g, v_w_in, v_conv_dw_w, v_conv_dw_b, v_conv_ln_g, v_conv_ln_b, v_lru_conv_w,
                                   v_lru_conv_b, v_lru_wa, v_lru_ba, v_lru_wx, v_lru_bx, v_lru_lambda, v_w_out, v_norm2_g,
                                   v_w_up, v_w_down, v_final_g)))
    B, S, _ = x.shape
    T = B * S
    my_slot = 4 * lax.axis_index("x") + 2 * lax.axis_index("y") + lax.axis_index("c")
    row = lambda a: a.reshape(1, -1)

    full = {n: [None] * DEPTH for n in BIG}
    first_items = (("w_in", 0),)
    landed = _run_exchange(_weight_gather(local, first_items, with_filters=True), "gather_first_weights")
    _keep_gathered(full, first_items, landed)
    full.update(_unpack_filters(landed[-1]))

    def hosted(call, layer, fn, *args):
        items = GATHER_HOSTS[call] if layer == 0 else ()
        outs, landed = fn(*args, ex=_weight_gather(local, items) if items else None)
        _keep_gathered(full, items, landed or ())
        return outs

    saved = []
    cur = x.reshape(T, D_MODEL)
    for l in range(DEPTH):
        h, qkv, ci, li = hosted("inproj", l, _inproj, cur, row(norm1_g[l]), full["w_in"][l])
        qkv = qkv.reshape(B, S, QKV_W)
        o, lse = hosted("attn_fwd", l, _attn_fwd, qkv)
        o, lse = o.reshape(T, ATTN_W), lse.reshape(T, ATTN_W)
        ci = ci.reshape(B, S, CONV_IN_W)
        li = li.reshape(B, S, LRU_IN_W)
        conv_p = (full["conv_dw_w"][l], row(conv_dw_b[l]), row(conv_ln_g[l]), row(conv_ln_b[l]))
        lru_p = (full["lru_conv_w"][l], row(lru_conv_b[l]), _block_diag(lru_wa[l]).astype(BF16), row(lru_ba[l]),
                 _block_diag(lru_wx[l]).astype(BF16), row(lru_bx[l]), row(lru_lambda[l]))
        yc, cpre = _conv_fwd(ci, *conv_p)
        yl, hs = hosted("lru_fwd", l, _lru_fwd, li, *lru_p)
        x1, mix = hosted("outproj", l, _outproj, cur, o, lse, yc.reshape(T, CONV_W), yl.reshape(T, LRU_W), full["w_out"][l])
        h2, r = hosted("up", l, _up, x1, row(norm2_g[l]), full["w_up"][l])
        (x2,) = hosted("down", l, _down, x1, r, full["w_down"][l])
        saved.append(dict(x=cur, h=h, qkv=qkv, o=o, lse=lse, ci=ci, li=li, cpre=cpre, hs=hs, x1=x1, mix=mix, h2=h2, r=r,
                          conv_p=conv_p, lru_p=lru_p))
        cur = x2

    dx, loss_part, dgf = _loss_head(cur, loss_target.reshape(T, D_MODEL), row(final_g))

    received = {n: [None] * DEPTH for n in BIG}
    small_grads = {n: [None] * DEPTH for n in SMALL_FULL if n != "final_g"}
    for l in reversed(range(DEPTH)):
        sv = saved[l]
        (dpre,), _ = _down_bwd_act(dx, sv["r"], full["w_down"][l])
        dw_down = _down_bwd_w(sv["r"], dx)
        (dx1, dg2), _ = _up_bwd_act(dpre, full["w_up"][l], sv["x1"], row(norm2_g[l]), dx)
        dw_up = _up_bwd_w(sv["h2"], dpre)
        do, dd, dyc, dyl, dw_out = _outproj_bwd(dx1, sv["mix"], full["w_out"][l], sv["o"], sv["lse"])
        seq = lambda a: a.reshape(B, S, ATTN_W)
        (dq, dk, dv), (received["w_down"][l],) = _attn_bwd(sv["qkv"], seq(do), seq(sv["lse"]), seq(dd),
                                                           ex=_grad_exchange("w_down", dw_down))
        (dci, dcw, dcb, dlg, dlb), (received["w_out"][l],) = _conv_bwd(
            sv["ci"], sv["cpre"], dyc.reshape(B, S, CONV_W), sv["conv_p"][0], sv["conv_p"][2], sv["conv_p"][3],
            ex=_grad_exchange("w_out", dw_out))
        (dli, dlcw, dlcb, dwa, dba, dwx, dbx, dlam), (received["w_up"][l],) = _lru_bwd(
            sv["li"], sv["hs"], dyl.reshape(B, S, LRU_W), *sv["lru_p"], ex=_grad_exchange("w_up", dw_up))
        dz = tuple(t.reshape(T, -1) for t in (dq, dk, dv, dci, dli))
        dw_in = _inproj_bwd_w(dz, sv["h"])
        (dx, dg1), (received["w_in"][l],) = _inproj_bwd_act(dz, full["w_in"][l], sv["x"], row(norm1_g[l]), dx1,
                                                            ex=_grad_exchange("w_in", dw_in))
        for n, g in (("norm1_g", dg1), ("conv_dw_w", dcw), ("conv_dw_b", dcb), ("conv_ln_g", dlg), ("conv_ln_b", dlb),
                     ("lru_conv_w", dlcw), ("lru_conv_b", dlcb), ("lru_wa", _diag_blocks(dwa)), ("lru_ba", dba),
                     ("lru_wx", _diag_blocks(dwx)), ("lru_bx", dbx), ("lru_lambda", dlam), ("norm2_g", dg2)):
            small_grads[n][l] = g.reshape(SMALL_FULL[n][1:])
    grad_x = dx.reshape(B, S, D_MODEL)

    two_d = lambda n: (int(np.prod(SMALL_FULL[n][:-1])), SMALL_FULL[n][-1])
    small_local = {n: jnp.stack(g).reshape(two_d(n)) for n, g in small_grads.items()}
    small_local["final_g"] = dgf
    small_names = list(SMALL_FULL)
    landed = _run_exchange(
        _Gather([small_local[n] for n in small_names] + [loss_part],
                [jax.ShapeDtypeStruct((N_DEV,) + two_d(n), F32) for n in small_names] + [jax.ShapeDtypeStruct((N_DEV, 8, 128), F32)],
                [_slot] * (len(small_names) + 1)),
        "gather_small_grads")
    gathered = dict(zip(small_names, landed[:-1]))

    grads, delta, new_m, new_v = {}, {}, {}, {}
    for n, rows_per_step in (("w_in", 256), ("w_out", W_OUT_SHARD), ("w_up", 256), ("w_down", 256)):
        grads[n], delta[n], new_m[n], new_v[n] = _sum_adamw(received[n], local[n], mom1[n], mom2[n], rows_per_step,
                                                            n == "w_down", "sum_adamw_" + n)

    replicated = [n for n in SMALL_FULL if n not in SMALL_SHARDED]
    state = {n: tuple(src[n].reshape(two_d(n)) for src in (local, mom1, mom2)) for n in replicated}
    small_g, loss_sum, updated = _small_sum_adamw(gathered, landed[-1], state)
    loss = loss_sum[0, 0]
    for n in replicated:
        grads[n] = small_g[n].reshape(SMALL_FULL[n])
        delta[n], new_m[n], new_v[n] = (t.reshape(SMALL_FULL[n]) for t in updated[n])
    for n, fullshape in SMALL_SHARDED.items():
        width = fullshape[-1] // N_DEV
        g = lax.dynamic_slice_in_dim(small_g[n], my_slot * width, width, axis=1)
        d, nm, nv = _adamw(local[n].reshape(g.shape), g, mom1[n].reshape(g.shape), mom2[n].reshape(g.shape), g.shape[0],
                           "adamw_" + n)
        grads[n], delta[n], new_m[n], new_v[n] = (t.reshape(local[n].shape) for t in (g, d, nm, nv))

    return (loss, grad_x, *[grads[n] for n in WEIGHT_ORDER], *[delta[n] for n in WEIGHT_ORDER],
            *[new_m[n] for n in WEIGHT_ORDER], *[new_v[n] for n in WEIGHT_ORDER])
```

```python
import functools
import math

import numpy as np
import jax
import jax.numpy as jnp
from jax import lax
from jax.experimental import pallas as pl
from jax.experimental.pallas import tpu as pltpu

F32 = jnp.float32
BF16 = jnp.bfloat16

D_MODEL = 1024
SEQ_LEN = 2048
HEAD_DIM = 64
ATTN_W = 384
CONV_W = 256
CONV_TAPS = 31
LRU_W = 384
LRU_TAPS = 4
LRU_HEADS = 6
LRU_C = 8.0
QKV_W = 3 * ATTN_W
CONV_IN_W = 2 * CONV_W
LRU_IN_W = 2 * LRU_W
IN_COLS = QKV_W + CONV_IN_W + LRU_IN_W
D_FF = 4096
DEPTH = 2
N_DEV = 8
RMS_EPS = 1e-6
LN_EPS = 1e-5
ATTN_BLOCK = 128
ATTN_DILATIONS = (1, 4, 16)
N_UNITS = 16
UNIT_UNROLL = 8
NEG_BIG = -1e30

ADAM_LR = 0.001
ADAM_B1 = 0.9
ADAM_B2 = 0.999
ADAM_EPS = 1e-08
ADAM_WD = 0.01
ADAM_STEP = 10

VMEM_LIMIT = 56 * 1024 * 1024
ROW_TILE = 512
GRAD_ROW_TILE = 1024
CONV_FWD_CHUNK = 128
CONV_BWD_CHUNK = 128
LRU_FWD_CHUNK = 128
LRU_BWD_CHUNK = 128


def _params(*sem):
    return pltpu.CompilerParams(dimension_semantics=sem if sem else None, vmem_limit_bytes=VMEM_LIMIT)


def _resident(shape):
    return pl.BlockSpec(shape, lambda *_: (0,) * len(shape), pipeline_mode=pl.Buffered(1))


def _dot(a, b):
    return jnp.dot(a, b, preferred_element_type=F32)


def _dot_nt(a, b):
    return lax.dot_general(a, b, (((1,), (1,)), ((), ())), preferred_element_type=F32)


def _dot_tn(a, b):
    return lax.dot_general(a, b, (((0,), (0,)), ((), ())), preferred_element_type=F32)


def _rms_fwd(x, g):
    rstd = lax.rsqrt(jnp.mean(x * x, axis=-1, keepdims=True) + RMS_EPS)
    xhat = x * rstd
    return xhat * g, xhat, rstd


def _rms_bwd(dh, xhat, rstd, g):
    dxh = dh * g
    dx = rstd * (dxh - xhat * jnp.mean(dxh * xhat, axis=-1, keepdims=True))
    dg = jnp.sum(dh * xhat, axis=0, keepdims=True)
    return dx, dg


def _sigmoid(x):
    return 0.5 * jnp.tanh(0.5 * x) + 0.5


def _one_minus_exp(x, exp_x):
    small = -x * (1.0 + x * (0.5 + x * (1.0 / 6.0)))
    return jnp.where(x > -0.01, small, 1.0 - exp_x)


def _log1p(z):
    w = 1.0 + z
    return jnp.where(w == 1.0, z, z * jnp.log(w) / jnp.where(w == 1.0, 1.0, w - 1.0))


def _softplus(x):
    return jnp.maximum(x, 0.0) + _log1p(jnp.exp(-jnp.abs(x)))


GELU_K = math.sqrt(2.0 / math.pi)


def _gelu(x):
    t = jnp.tanh(GELU_K * (x + 0.044715 * x * x * x))
    return 0.5 * x * (1.0 + t), t


def _gelu_grad(x, t):
    return 0.5 * (1.0 + t) + 0.5 * x * (1.0 - t * t) * GELU_K * (1.0 + 3.0 * 0.044715 * x * x)


def _inproj(x2d, g, w, ex=None):
    T = x2d.shape[0]
    tm = ROW_TILE

    def body(x_ref, g_ref, w_ref, h_ref, qkv_ref, ci_ref, li_ref):
        h, _, _ = _rms_fwd(x_ref[...], g_ref[...])
        hb = h.astype(BF16)
        h_ref[...] = hb
        qkv_ref[...] = _dot(hb, w_ref[:, 0:QKV_W])
        ci_ref[...] = _dot(hb, w_ref[:, QKV_W:QKV_W + CONV_IN_W])
        li_ref[...] = _dot(hb, w_ref[:, QKV_W + CONV_IN_W:IN_COLS])

    return _pallas_hosting(
        body, ex, name="inproj", grid=(T // tm,),
        in_specs=[pl.BlockSpec((tm, D_MODEL), lambda i: (i, 0)),
                  pl.BlockSpec((1, D_MODEL), lambda i: (0, 0)),
                  _resident((D_MODEL, IN_COLS))],
        out_specs=[pl.BlockSpec((tm, D_MODEL), lambda i: (i, 0)),
                   pl.BlockSpec((tm, QKV_W), lambda i: (i, 0)),
                   pl.BlockSpec((tm, CONV_IN_W), lambda i: (i, 0)),
                   pl.BlockSpec((tm, LRU_IN_W), lambda i: (i, 0))],
        out_shape=[jax.ShapeDtypeStruct((T, D_MODEL), BF16), jax.ShapeDtypeStruct((T, QKV_W), F32),
                   jax.ShapeDtypeStruct((T, CONV_IN_W), F32), jax.ShapeDtypeStruct((T, LRU_IN_W), F32)],
        scratch_shapes=[], semantics=("parallel",), operands=(x2d, g, w))


def _attn_alpha(lse):
    l0, l1, l2 = lse[:, 0:128], lse[:, 128:256], lse[:, 256:384]
    m = jnp.maximum(jnp.maximum(l0, l1), l2)
    e0, e1, e2 = jnp.exp(l0 - m), jnp.exp(l1 - m), jnp.exp(l2 - m)
    inv = 1.0 / (e0 + e1 + e2)
    return e0 * inv, e1 * inv, e2 * inv


def _outproj(x2d, o, lse, yc, yl, w, ex=None):
    T = x2d.shape[0]
    tm = ROW_TILE

    def body(x_ref, o_ref, lse_ref, yc_ref, yl_ref, w_ref, x1_ref, mix_ref):
        al = _attn_alpha(lse_ref[...])
        for p in range(3):
            mix_ref[:, p * 128:(p + 1) * 128] = (o_ref[:, p * 128:(p + 1) * 128] * al[p]).astype(BF16)
        mix_ref[:, ATTN_W:ATTN_W + CONV_W] = yc_ref[...]
        mix_ref[:, ATTN_W + CONV_W:D_MODEL] = yl_ref[...]
        x1_ref[...] = x_ref[...] + _dot(mix_ref[...], w_ref[...])

    return _pallas_hosting(
        body, ex, name="outproj", grid=(T // tm,),
        in_specs=[pl.BlockSpec((tm, D_MODEL), lambda i: (i, 0)),
                  pl.BlockSpec((tm, ATTN_W), lambda i: (i, 0)),
                  pl.BlockSpec((tm, ATTN_W), lambda i: (i, 0)),
                  pl.BlockSpec((tm, CONV_W), lambda i: (i, 0)),
                  pl.BlockSpec((tm, LRU_W), lambda i: (i, 0)),
                  _resident((D_MODEL, D_MODEL))],
        out_specs=[pl.BlockSpec((tm, D_MODEL), lambda i: (i, 0)),
                   pl.BlockSpec((tm, D_MODEL), lambda i: (i, 0))],
        out_shape=[jax.ShapeDtypeStruct((T, D_MODEL), F32), jax.ShapeDtypeStruct((T, D_MODEL), BF16)],
        scratch_shapes=[], semantics=("parallel",), operands=(x2d, o, lse, yc, yl, w))


FF_CHUNK = 1024


def _up(x1, g, w, ex=None):
    T = x1.shape[0]
    tm = ROW_TILE

    def body(x_ref, g_ref, w_ref, h_ref, r_ref):
        h, _, _ = _rms_fwd(x_ref[...], g_ref[...])
        hb = h.astype(BF16)
        h_ref[...] = hb
        for c in range(0, D_FF, FF_CHUNK):
            r_ref[:, c:c + FF_CHUNK] = jnp.maximum(_dot(hb, w_ref[:, c:c + FF_CHUNK]), 0.0).astype(BF16)

    return _pallas_hosting(
        body, ex, name="up", grid=(T // tm,),
        in_specs=[pl.BlockSpec((tm, D_MODEL), lambda i: (i, 0)),
                  pl.BlockSpec((1, D_MODEL), lambda i: (0, 0)),
                  _resident((D_MODEL, D_FF))],
        out_specs=[pl.BlockSpec((tm, D_MODEL), lambda i: (i, 0)),
                   pl.BlockSpec((tm, D_FF), lambda i: (i, 0))],
        out_shape=[jax.ShapeDtypeStruct((T, D_MODEL), BF16), jax.ShapeDtypeStruct((T, D_FF), BF16)],
        scratch_shapes=[], semantics=("parallel",), operands=(x1, g, w))


def _square_bf16(r):
    rf = r.astype(F32)
    return (rf * rf).astype(BF16)


def _down(x1, r, w, ex=None):
    T = x1.shape[0]
    tm = ROW_TILE

    def body(x_ref, r_ref, w_ref, o_ref):
        acc = x_ref[...]
        for c in range(0, D_FF, FF_CHUNK):
            acc = acc + _dot(_square_bf16(r_ref[:, c:c + FF_CHUNK]), w_ref[c:c + FF_CHUNK, :])
        o_ref[...] = acc

    return _pallas_hosting(
        body, ex, name="down", grid=(T // tm,),
        in_specs=[pl.BlockSpec((tm, D_MODEL), lambda i: (i, 0)),
                  pl.BlockSpec((tm, D_FF), lambda i: (i, 0)),
                  _resident((D_FF, D_MODEL))],
        out_specs=[pl.BlockSpec((tm, D_MODEL), lambda i: (i, 0))],
        out_shape=[jax.ShapeDtypeStruct((T, D_MODEL), F32)],
        scratch_shapes=[], semantics=("parallel",), operands=(x1, r, w))


def _loss_head(x2, target, g):
    T = x2.shape[0]
    tm = ROW_TILE

    def body(x_ref, t_ref, g_ref, dx_ref, loss_ref, dg_ref):
        @pl.when(pl.program_id(0) == 0)
        def _():
            loss_ref[...] = jnp.zeros_like(loss_ref)
            dg_ref[...] = jnp.zeros_like(dg_ref)

        gv = g_ref[...]
        y, xhat, rstd = _rms_fwd(x_ref[...], gv)
        err = y - t_ref[...]
        loss_ref[...] += 0.5 * jnp.sum(jnp.mean(err * err, axis=-1, keepdims=True))
        dy = err * (1.0 / D_MODEL)
        dx, dg = _rms_bwd(dy, xhat, rstd, gv)
        dx_ref[...] = dx
        dg_ref[...] += dg

    return pl.pallas_call(
        body, name="loss_head", grid=(T // tm,),
        in_specs=[pl.BlockSpec((tm, D_MODEL), lambda i: (i, 0)),
                  pl.BlockSpec((tm, D_MODEL), lambda i: (i, 0)),
                  pl.BlockSpec((1, D_MODEL), lambda i: (0, 0))],
        out_specs=[pl.BlockSpec((tm, D_MODEL), lambda i: (i, 0)),
                   pl.BlockSpec((8, 128), lambda i: (0, 0)),
                   pl.BlockSpec((1, D_MODEL), lambda i: (0, 0))],
        out_shape=[jax.ShapeDtypeStruct((T, D_MODEL), F32), jax.ShapeDtypeStruct((8, 128), F32),
                   jax.ShapeDtypeStruct((1, D_MODEL), F32)],
        compiler_params=_params("arbitrary"),
    )(x2, target, g)


def _down_bwd_act(dx2, r, w, ex=None):
    T = dx2.shape[0]
    tm = ROW_TILE

    def body(dx_ref, r_ref, w_ref, o_ref):
        dxb = dx_ref[...].astype(BF16)
        for c in range(0, D_FF, FF_CHUNK):
            dff = _dot_nt(dxb, w_ref[c:c + FF_CHUNK, :])
            o_ref[:, c:c + FF_CHUNK] = (dff * (2.0 * r_ref[:, c:c + FF_CHUNK].astype(F32))).astype(BF16)

    return _pallas_hosting(
        body, ex, name="down_bwd_act", grid=(T // tm,),
        in_specs=[pl.BlockSpec((tm, D_MODEL), lambda i: (i, 0)),
                  pl.BlockSpec((tm, D_FF), lambda i: (i, 0)),
                  _resident((D_FF, D_MODEL))],
        out_specs=[pl.BlockSpec((tm, D_FF), lambda i: (i, 0))],
        out_shape=[jax.ShapeDtypeStruct((T, D_FF), BF16)],
        scratch_shapes=[], semantics=("parallel",), operands=(dx2, r, w))


def _down_bwd_w(r, dx2):
    T = dx2.shape[0]
    tk = GRAD_ROW_TILE
    nk = T // tk

    def body(r_ref, dx_ref, o_ref, acc_ref):
        k = pl.program_id(0)
        dxb = dx_ref[...].astype(BF16)

        @pl.when(k == 0)
        def _():
            acc_ref[...] = jnp.zeros_like(acc_ref)

        for c in range(0, D_FF, FF_CHUNK):
            acc_ref[:, c:c + FF_CHUNK] += _dot_tn(dxb, _square_bf16(r_ref[:, c:c + FF_CHUNK]))

        @pl.when(k == nk - 1)
        def _():
            o_ref[...] = acc_ref[...].astype(BF16)

    return pl.pallas_call(
        body, name="down_bwd_w", grid=(nk,),
        in_specs=[pl.BlockSpec((tk, D_FF), lambda k: (k, 0)),
                  pl.BlockSpec((tk, D_MODEL), lambda k: (k, 0))],
        out_specs=_resident((D_MODEL, D_FF)),
        out_shape=jax.ShapeDtypeStruct((D_MODEL, D_FF), BF16),
        scratch_shapes=[pltpu.VMEM((D_MODEL, D_FF), F32)],
        compiler_params=_params("arbitrary"),
    )(r, dx2)


def _up_bwd_act(dpre, w, x1, g, dx2, ex=None):
    T = dx2.shape[0]
    tm = ROW_TILE

    def body(dp_ref, w_ref, x_ref, g_ref, dx2_ref, dx1_ref, dg_ref):
        dh = _dot_nt(dp_ref[:, 0:FF_CHUNK], w_ref[:, 0:FF_CHUNK])
        for c in range(FF_CHUNK, D_FF, FF_CHUNK):
            dh = dh + _dot_nt(dp_ref[:, c:c + FF_CHUNK], w_ref[:, c:c + FF_CHUNK])
        gv = g_ref[...]
        _, xhat, rstd = _rms_fwd(x_ref[...], gv)
        dx, dg = _rms_bwd(dh, xhat, rstd, gv)
        dx1_ref[...] = dx2_ref[...] + dx

        @pl.when(pl.program_id(0) == 0)
        def _():
            dg_ref[...] = dg

        @pl.when(pl.program_id(0) != 0)
        def _():
            dg_ref[...] += dg

    return _pallas_hosting(
        body, ex, name="up_bwd_act", grid=(T // tm,),
        in_specs=[pl.BlockSpec((tm, D_FF), lambda i: (i, 0)),
                  _resident((D_MODEL, D_FF)),
                  pl.BlockSpec((tm, D_MODEL), lambda i: (i, 0)),
                  pl.BlockSpec((1, D_MODEL), lambda i: (0, 0)),
                  pl.BlockSpec((tm, D_MODEL), lambda i: (i, 0))],
        out_specs=[pl.BlockSpec((tm, D_MODEL), lambda i: (i, 0)),
                   pl.BlockSpec((1, D_MODEL), lambda i: (0, 0))],
        out_shape=[jax.ShapeDtypeStruct((T, D_MODEL), F32), jax.ShapeDtypeStruct((1, D_MODEL), F32)],
        scratch_shapes=[], semantics=("arbitrary",), operands=(dpre, w, x1, g, dx2))


def _up_bwd_w(h2, dpre):
    T = h2.shape[0]
    tk = GRAD_ROW_TILE
    nk = T // tk

    def body(h_ref, dp_ref, o_ref, acc_ref):
        k = pl.program_id(0)
        hb = h_ref[...]

        @pl.when(k == 0)
        def _():
            acc_ref[...] = jnp.zeros_like(acc_ref)

        for c in range(0, D_FF, FF_CHUNK):
            acc_ref[:, c:c + FF_CHUNK] += _dot_tn(hb, dp_ref[:, c:c + FF_CHUNK])

        @pl.when(k == nk - 1)
        def _():
            o_ref[...] = acc_ref[...].astype(BF16)

    return pl.pallas_call(
        body, name="up_bwd_w", grid=(nk,),
        in_specs=[pl.BlockSpec((tk, D_MODEL), lambda k: (k, 0)),
                  pl.BlockSpec((tk, D_FF), lambda k: (k, 0))],
        out_specs=_resident((D_MODEL, D_FF)),
        out_shape=jax.ShapeDtypeStruct((D_MODEL, D_FF), BF16),
        scratch_shapes=[pltpu.VMEM((D_MODEL, D_FF), F32)],
        compiler_params=_params("arbitrary"),
    )(h2, dpre)


def _outproj_bwd(dx1, mix, w, o, lse):
    T = dx1.shape[0]
    tm = ROW_TILE
    nk = T // tm

    def body(dx_ref, mix_ref, w_ref, o_ref, lse_ref, do_ref, dd_ref, dc_ref, dl_ref, dw_ref, acc_ref):
        i = pl.program_id(0)
        dxb = dx_ref[...].astype(BF16)
        dmix = _dot_nt(dxb, w_ref[...])
        al = _attn_alpha(lse_ref[...])
        first = lax.broadcasted_iota(jnp.int32, (tm, 128), 1) < HEAD_DIM
        tot = jnp.zeros((tm, 128), F32)
        for p in range(3):
            sl = slice(p * 128, (p + 1) * 128)
            dy = dmix[:, sl]
            do_ref[:, sl] = dy * al[p]
            prod = dy * o_ref[:, sl]
            s0 = jnp.sum(jnp.where(first, prod, 0.0), axis=-1, keepdims=True)
            s1 = jnp.sum(jnp.where(first, 0.0, prod), axis=-1, keepdims=True)
            tot = tot + al[p] * jnp.where(first, s0, s1)
        for p in range(3):
            dd_ref[:, p * 128:(p + 1) * 128] = -al[p] * tot
        dc_ref[...] = dmix[:, ATTN_W:ATTN_W + CONV_W]
        dl_ref[...] = dmix[:, ATTN_W + CONV_W:D_MODEL]
        part = _dot_tn(mix_ref[...], dxb)

        @pl.when(i == 0)
        def _():
            acc_ref[...] = part

        @pl.when(i != 0)
        def _():
            acc_ref[...] += part

        @pl.when(i == nk - 1)
        def _():
            dw_ref[...] = acc_ref[...].astype(BF16)

    return pl.pallas_call(
        body, name="outproj_bwd", grid=(nk,),
        in_specs=[pl.BlockSpec((tm, D_MODEL), lambda i: (i, 0)),
                  pl.BlockSpec((tm, D_MODEL), lambda i: (i, 0)),
                  _resident((D_MODEL, D_MODEL)),
                  pl.BlockSpec((tm, ATTN_W), lambda i: (i, 0)),
                  pl.BlockSpec((tm, ATTN_W), lambda i: (i, 0))],
        out_specs=[pl.BlockSpec((tm, ATTN_W), lambda i: (i, 0)),
                   pl.BlockSpec((tm, ATTN_W), lambda i: (i, 0)),
                   pl.BlockSpec((tm, CONV_W), lambda i: (i, 0)),
                   pl.BlockSpec((tm, LRU_W), lambda i: (i, 0)),
                   _resident((D_MODEL, D_MODEL))],
        out_shape=[jax.ShapeDtypeStruct((T, ATTN_W), F32), jax.ShapeDtypeStruct((T, ATTN_W), F32),
                   jax.ShapeDtypeStruct((T, CONV_W), F32), jax.ShapeDtypeStruct((T, LRU_W), F32),
                   jax.ShapeDtypeStruct((D_MODEL, D_MODEL), BF16)],
        scratch_shapes=[pltpu.VMEM((D_MODEL, D_MODEL), F32)],
        compiler_params=_params("arbitrary"),
    )(dx1, mix, w, o, lse)


DZ_COLS = ((0, ATTN_W), (ATTN_W, 2 * ATTN_W), (2 * ATTN_W, QKV_W), (QKV_W, QKV_W + CONV_IN_W), (QKV_W + CONV_IN_W, IN_COLS))


def _inproj_bwd_w(dz_parts, h):
    T = h.shape[0]
    tm = GRAD_ROW_TILE
    nk = T // tm
    n_parts = len(DZ_COLS)

    def body(*refs):
        dz_refs = refs[:n_parts]
        h_ref, dw_ref, acc_ref = refs[n_parts:]
        i = pl.program_id(0)
        hb = h_ref[...]

        @pl.when(i == 0)
        def _():
            acc_ref[...] = jnp.zeros_like(acc_ref)

        for r, (lo, hi) in zip(dz_refs, DZ_COLS):
            acc_ref[:, lo:hi] += _dot_tn(hb, r[...].astype(BF16))

        @pl.when(i == nk - 1)
        def _():
            dw_ref[...] = acc_ref[...].astype(BF16)

    rows = lambda width: pl.BlockSpec((tm, width), lambda i: (i, 0))
    return pl.pallas_call(
        body, name="inproj_bwd_w", grid=(nk,),
        in_specs=[rows(hi - lo) for lo, hi in DZ_COLS] + [rows(D_MODEL)],
        out_specs=_resident((D_MODEL, IN_COLS)),
        out_shape=jax.ShapeDtypeStruct((D_MODEL, IN_COLS), BF16),
        scratch_shapes=[pltpu.VMEM((D_MODEL, IN_COLS), F32)],
        compiler_params=_params("arbitrary"),
    )(*dz_parts, h)


def _inproj_bwd_act(dz_parts, w, x2d, g, dx1, ex=None):
    T = x2d.shape[0]
    tm = ROW_TILE
    n_parts = len(DZ_COLS)

    def body(*refs):
        dz_refs = refs[:n_parts]
        w_ref, x_ref, g_ref, dx1_ref, dx_ref, dg_ref = refs[n_parts:]
        dh = _dot_nt(dz_refs[0][...].astype(BF16), w_ref[:, DZ_COLS[0][0]:DZ_COLS[0][1]])
        for r, (lo, hi) in zip(dz_refs[1:], DZ_COLS[1:]):
            dh = dh + _dot_nt(r[...].astype(BF16), w_ref[:, lo:hi])
        gv = g_ref[...]
        _, xhat, rstd = _rms_fwd(x_ref[...], gv)
        dx, dg = _rms_bwd(dh, xhat, rstd, gv)
        dx_ref[...] = dx1_ref[...] + dx

        @pl.when(pl.program_id(0) == 0)
        def _():
            dg_ref[...] = dg

        @pl.when(pl.program_id(0) != 0)
        def _():
            dg_ref[...] += dg

    rows = lambda width: pl.BlockSpec((tm, width), lambda i: (i, 0))
    return _pallas_hosting(
        body, ex, name="inproj_bwd_act", grid=(T // tm,),
        in_specs=[rows(hi - lo) for lo, hi in DZ_COLS] + [
            _resident((D_MODEL, IN_COLS)), rows(D_MODEL), pl.BlockSpec((1, D_MODEL), lambda i: (0, 0)), rows(D_MODEL)],
        out_specs=[rows(D_MODEL), pl.BlockSpec((1, D_MODEL), lambda i: (0, 0))],
        out_shape=[jax.ShapeDtypeStruct((T, D_MODEL), F32), jax.ShapeDtypeStruct((1, D_MODEL), F32)],
        scratch_shapes=[], semantics=("arbitrary",), operands=(*dz_parts, w, x2d, g, dx1))


def _alibi_coef():
    slopes = 2.0 ** (-8.0 * np.arange(1, 7) / 6)
    return jnp.asarray((slopes.reshape(3, 2) * np.asarray(ATTN_DILATIONS)[:, None]).astype(np.float32))


def _unit_rows(u, d):
    nb = N_UNITS // d
    r, n = u // nb, u % nb
    span = ATTN_BLOCK * d

    def rows(block):
        start = block * span + r
        return pl.ds(pl.multiple_of(start, ATTN_BLOCK), ATTN_BLOCK) if d == 1 else pl.ds(start, ATTN_BLOCK, stride=d)

    return rows(n), rows(jnp.maximum(n - 1, 0)), rows(jnp.minimum(n + 1, nb - 1)), n > 0, n + 1 < nb


def _per_pattern(fn):
    for p, d in enumerate(ATTN_DILATIONS):
        pl.when(pl.program_id(1) == p)(functools.partial(fn, p, d))


def _attn_col(offset):
    return pl.BlockSpec((None, SEQ_LEN, 128), lambda b, p: (b, 0, p + offset))


def _attn_masks():
    qi = lax.broadcasted_iota(jnp.int32, (ATTN_BLOCK, 2 * ATTN_BLOCK), 0)
    kj = lax.broadcasted_iota(jnp.int32, (ATTN_BLOCK, 2 * ATTN_BLOCK), 1)
    dist = qi + ATTN_BLOCK - kj
    first = lax.broadcasted_iota(jnp.int32, (ATTN_BLOCK, 128), 1) < HEAD_DIM
    return dist.astype(F32), (dist >= 0) & (dist <= ATTN_BLOCK), kj >= ATTN_BLOCK, first


def _head_lanes(a, first, j):
    return jnp.where(first if j == 0 else jnp.logical_not(first), a, jnp.zeros_like(a))


def _load_kv(ref, prev, own):
    return jnp.concatenate([ref[prev, :], ref[own, :]], axis=0).astype(BF16)


def _attn_fwd(qkv, ex=None):
    B = qkv.shape[0]

    def body(coef_ref, q_ref, k_ref, v_ref, o_ref, lse_ref):
        dist, band, own_half, first = _attn_masks()

        def pattern(p, d):
            def unit(u, carry):
                own, prev, _, has_prev, _ = _unit_rows(u, d)
                ok = band & jnp.logical_or(own_half, has_prev)
                q = q_ref[own, :].astype(BF16)
                kcat, vcat = _load_kv(k_ref, prev, own), _load_kv(v_ref, prev, own)
                outs, lses = [], []
                for j in range(2):
                    s = jnp.where(ok, _dot_nt(_head_lanes(q, first, j), kcat) * 0.125 - coef_ref[p, j] * dist, NEG_BIG)
                    m = jnp.max(s, axis=-1, keepdims=True)
                    e = jnp.exp(s - m)
                    l = jnp.sum(e, axis=-1, keepdims=True)
                    outs.append(_dot(e.astype(BF16), vcat) * (1.0 / l))
                    lses.append(m + jnp.log(l))
                o_ref[own, :] = jnp.where(first, outs[0], outs[1])
                lse_ref[own, :] = jnp.where(first, lses[0], lses[1])
                return carry

            lax.fori_loop(0, N_UNITS, unit, 0, unroll=UNIT_UNROLL)

        _per_pattern(pattern)

    shape = jax.ShapeDtypeStruct((B, SEQ_LEN, ATTN_W), F32)
    return _pallas_hosting(
        body, ex, name="attn_fwd", grid=(B, 3),
        in_specs=[pl.BlockSpec(memory_space=pltpu.SMEM), _attn_col(0), _attn_col(3), _attn_col(6)],
        out_specs=[_attn_col(0), _attn_col(0)],
        out_shape=[shape, shape],
        scratch_shapes=[], semantics=("parallel", "parallel"), operands=(_alibi_coef(), qkv, qkv, qkv))


def _attn_bwd(qkv, do, lse, dd, ex=None):
    B = qkv.shape[0]

    def body(coef_ref, q_ref, k_ref, v_ref, do_ref, lse_ref, dd_ref, dq_ref, dk_ref, dv_ref):
        dist, band, own_half, first = _attn_masks()

        def pattern(p, d):
            def unit(u, carry):
                own, prev, _, has_prev, _ = _unit_rows(u, d)
                ok = band & jnp.logical_or(own_half, has_prev)
                q, do = q_ref[own, :].astype(BF16), do_ref[own, :].astype(BF16)
                kcat, vcat = _load_kv(k_ref, prev, own), _load_kv(v_ref, prev, own)
                lse_a, dd_a = lse_ref[own, :], dd_ref[own, :]
                dqs, dks, dvs = [], [], []
                for j in range(2):
                    col = slice(HEAD_DIM * j, HEAD_DIM * j + 1)
                    s = _dot_nt(_head_lanes(q, first, j), kcat) * 0.125 - coef_ref[p, j] * dist
                    pr = jnp.where(ok, jnp.exp(jnp.where(ok, s, NEG_BIG) - lse_a[:, col]), 0.0)
                    ds = (pr * (_dot_nt(_head_lanes(do, first, j), vcat) + dd_a[:, col])).astype(BF16)
                    dqs.append(_dot(ds, kcat))
                    dks.append(_dot_tn(ds, q))
                    dvs.append(_dot_tn(pr.astype(BF16), do))
                both = lambda pair: jnp.where(jnp.concatenate([first] * (pair[0].shape[0] // ATTN_BLOCK), axis=0), *pair)
                dq_ref[own, :] = both(dqs) * 0.125
                dk, dv = both(dks) * 0.125, both(dvs)
                dk_ref[own, :] = dk[ATTN_BLOCK:]
                dv_ref[own, :] = dv[ATTN_BLOCK:]
                dk_ref[prev, :] += dk[:ATTN_BLOCK]
                dv_ref[prev, :] += dv[:ATTN_BLOCK]
                return carry

            lax.fori_loop(0, N_UNITS, unit, 0, unroll=UNIT_UNROLL)

        _per_pattern(pattern)

    shape = jax.ShapeDtypeStruct((B, SEQ_LEN, ATTN_W), F32)
    return _pallas_hosting(
        body, ex, name="attn_bwd", grid=(B, 3),
        in_specs=[pl.BlockSpec(memory_space=pltpu.SMEM), _attn_col(0), _attn_col(3), _attn_col(6), _attn_col(0), _attn_col(0),
                  _attn_col(0)],
        out_specs=[_attn_col(0)] * 3,
        out_shape=[shape] * 3,
        scratch_shapes=[], semantics=("parallel", "parallel"), operands=(_alibi_coef(), qkv, qkv, qkv, do, lse, dd))


def _for_chunks(n_rows, fn, chunk):
    def step(c, carry):
        fn(pl.multiple_of(c * chunk, chunk))
        return carry

    lax.fori_loop(0, n_rows // chunk, step, 0)


def _shift_down(win, s, rows):
    lead = win.shape[0] - rows
    if s == 0:
        return win[lead:]
    if s % 8 == 0:
        return win[lead - s:lead - s + rows]
    q, r = divmod(s, 8)
    rolled = pltpu.roll(win, r, 0)
    return rolled[lead - 8 * q:lead - 8 * q + rows]


def _tap_shifts(n_taps):
    return [(r, [(n_taps - 1 - (8 * q + r), 8 * q) for q in range((n_taps - 1 - r) // 8 + 1)]) for r in range(min(8, n_taps))]


def _rotated_down(win, r):
    return win if r == 0 else pltpu.roll(win, r, 0)


def _rotated_up(win, r):
    return win if r == 0 else pltpu.roll(win, win.shape[0] - r, 0)


def _shift_up(win, s, rows):
    if s % 8 == 0:
        return win[s:s + rows]
    q, r = divmod(s, 8)
    rolled = pltpu.roll(win, win.shape[0] - r, 0)
    return rolled[8 * q:8 * q + rows]


CONV_PAD = 32


def _ln_silu(c, lg, lb):
    mu = jnp.mean(c, axis=-1, keepdims=True)
    cc = c - mu
    rstd = lax.rsqrt(jnp.mean(cc * cc, axis=-1, keepdims=True) + LN_EPS)
    nrm = cc * rstd
    v = nrm * lg + lb
    sg = _sigmoid(v)
    return v * sg, nrm, rstd, v, sg


def _conv_fwd(ci, w, b, lg, lb):
    B, S, _ = ci.shape
    CH = CONV_FWD_CHUNK

    def body(ci_ref, w_ref, b_ref, lg_ref, lb_ref, y_ref, c_ref, pad_ref):
        pad_ref[0:CONV_PAD, :] = jnp.zeros((CONV_PAD, CONV_W), F32)

        def glu(base):
            blk = ci_ref[pl.ds(base, CH), :]
            pad_ref[pl.ds(CONV_PAD + base, CH), :] = blk[:, 0:CONV_W] * _sigmoid(blk[:, CONV_W:])

        _for_chunks(S, glu, CH)

        def conv(base):
            win = pad_ref[pl.ds(base, CH + CONV_PAD), :]
            acc = jnp.broadcast_to(b_ref[...], (CH, CONV_W))
            for r, taps in _tap_shifts(CONV_TAPS):
                rot = _rotated_down(win, r)
                for k, off in taps:
                    acc = acc + w_ref[k:k + 1, :] * rot[CONV_PAD - off:CONV_PAD - off + CH]
            c_ref[pl.ds(base, CH), :] = acc
            y, _, _, _, _ = _ln_silu(acc, lg_ref[...], lb_ref[...])
            y_ref[pl.ds(base, CH), :] = y.astype(BF16)

        _for_chunks(S, conv, CH)

    vec = pl.BlockSpec((1, CONV_W), lambda i: (0, 0))
    return pl.pallas_call(
        body, name="conv_fwd", grid=(B,),
        in_specs=[pl.BlockSpec((None, S, CONV_IN_W), lambda i: (i, 0, 0)),
                  pl.BlockSpec((CONV_TAPS, CONV_W), lambda i: (0, 0)), vec, vec, vec],
        out_specs=[pl.BlockSpec((None, S, CONV_W), lambda i: (i, 0, 0)),
                   pl.BlockSpec((None, S, CONV_W), lambda i: (i, 0, 0))],
        out_shape=[jax.ShapeDtypeStruct((B, S, CONV_W), BF16), jax.ShapeDtypeStruct((B, S, CONV_W), F32)],
        scratch_shapes=[pltpu.VMEM((S + CONV_PAD, CONV_W), F32)],
        compiler_params=_params("parallel"),
    )(ci, w, b, lg, lb)


def _conv_bwd(ci, cpre, dy, w, lg, lb, ex=None):
    B, S, _ = ci.shape
    CH = CONV_BWD_CHUNK

    def body(ci_ref, c_ref, dy_ref, w_ref, lg_ref, lb_ref, dci_ref, dw_ref, db_ref, dlg_ref, dlb_ref, upad_ref, dcpad_ref,
             dwacc_ref):
        @pl.when(pl.program_id(0) == 0)
        def _():
            dw_ref[...] = jnp.zeros_like(dw_ref)
            db_ref[...] = jnp.zeros_like(db_ref)
            dlg_ref[...] = jnp.zeros_like(dlg_ref)
            dlb_ref[...] = jnp.zeros_like(dlb_ref)

        upad_ref[0:CONV_PAD, :] = jnp.zeros((CONV_PAD, CONV_W), F32)
        dcpad_ref[S:S + CONV_PAD, :] = jnp.zeros((CONV_PAD, CONV_W), F32)
        dwacc_ref[...] = jnp.zeros_like(dwacc_ref)

        def norm_bwd(base):
            blk = ci_ref[pl.ds(base, CH), :]
            upad_ref[pl.ds(CONV_PAD + base, CH), :] = blk[:, 0:CONV_W] * _sigmoid(blk[:, CONV_W:])
            lgv = lg_ref[...]
            _, nrm, rstd, v, sg = _ln_silu(c_ref[pl.ds(base, CH), :], lgv, lb_ref[...])
            dv = dy_ref[pl.ds(base, CH), :] * (sg * (1.0 + v * (1.0 - sg)))
            dlg_ref[...] += jnp.sum(dv * nrm, axis=0, keepdims=True)
            dlb_ref[...] += jnp.sum(dv, axis=0, keepdims=True)
            dn = dv * lgv
            dc = rstd * (dn - jnp.mean(dn, axis=-1, keepdims=True) - nrm * jnp.mean(dn * nrm, axis=-1, keepdims=True))
            dcpad_ref[pl.ds(base, CH), :] = dc
            db_ref[...] += jnp.sum(dc, axis=0, keepdims=True)

        _for_chunks(S, norm_bwd, CH)

        def conv_bwd(base):
            dwin = dcpad_ref[pl.ds(base, CH + CONV_PAD), :]
            uwin = upad_ref[pl.ds(base, CH + CONV_PAD), :]
            dc = dwin[0:CH]
            du = jnp.zeros((CH, CONV_W), F32)
            for r, taps in _tap_shifts(CONV_TAPS):
                d_rot, u_rot = _rotated_up(dwin, r), _rotated_down(uwin, r)
                for k, off in taps:
                    du = du + w_ref[k:k + 1, :] * d_rot[off:off + CH]
                    prod = dc * u_rot[CONV_PAD - off:CONV_PAD - off + CH]
                    dwacc_ref[8 * k:8 * k + 8, :] += jnp.sum(prod.reshape(CH // 8, 8, CONV_W), axis=0)
            blk = ci_ref[pl.ds(base, CH), :]
            a, sg = blk[:, 0:CONV_W], _sigmoid(blk[:, CONV_W:])
            dci_ref[pl.ds(base, CH), 0:CONV_W] = (du * sg).astype(BF16)
            dci_ref[pl.ds(base, CH), CONV_W:] = (du * a * sg * (1.0 - sg)).astype(BF16)

        _for_chunks(S, conv_bwd, CH)
        for k in range(CONV_TAPS):
            dw_ref[k:k + 1, :] += jnp.sum(dwacc_ref[8 * k:8 * k + 8, :], axis=0, keepdims=True)

    vec = pl.BlockSpec((1, CONV_W), lambda i: (0, 0))
    mat = pl.BlockSpec((CONV_TAPS, CONV_W), lambda i: (0, 0))
    seq = lambda width: pl.BlockSpec((None, S, width), lambda i: (i, 0, 0))
    return _pallas_hosting(
        body, ex, name="conv_bwd", grid=(B,),
        in_specs=[seq(CONV_IN_W), seq(CONV_W), seq(CONV_W), mat, vec, vec],
        out_specs=[seq(CONV_IN_W), mat, vec, vec, vec],
        out_shape=[jax.ShapeDtypeStruct((B, S, CONV_IN_W), BF16), jax.ShapeDtypeStruct((CONV_TAPS, CONV_W), F32),
                   jax.ShapeDtypeStruct((1, CONV_W), F32), jax.ShapeDtypeStruct((1, CONV_W), F32),
                   jax.ShapeDtypeStruct((1, CONV_W), F32)],
        scratch_shapes=[pltpu.VMEM((S + CONV_PAD, CONV_W), F32), pltpu.VMEM((S + CONV_PAD, CONV_W), F32),
                        pltpu.VMEM((8 * CONV_TAPS, CONV_W), F32)],
        semantics=("arbitrary",), operands=(ci, cpre, dy, w, lg, lb))


SCAN_SHIFTS = tuple(1 << e for e in range(11))


def _prev8(ref, base, cols, fill):
    start = pl.multiple_of(jnp.maximum(base - 8, 0), 8)
    return jnp.where(base > 0, ref[pl.ds(start, 8), cols], fill)


def _next8(ref, base, rows, total, cols, fill):
    start = pl.multiple_of(jnp.minimum(base + rows, total - 8), 8)
    return jnp.where(base + rows < total, ref[pl.ds(start, 8), cols], fill)


ALL = slice(None)
LRU_X = slice(LRU_W, LRU_IN_W)
LRU_GATE = slice(0, LRU_W)


def _lru_conv(li_ref, base, rows, cw_ref, cb_ref):
    win = jnp.concatenate([_prev8(li_ref, base, LRU_X, 0.0), li_ref[pl.ds(base, rows), LRU_X]], axis=0)
    u = jnp.broadcast_to(cb_ref[...], (rows, LRU_W))
    for k in range(LRU_TAPS):
        u = u + cw_ref[k:k + 1, :] * _shift_down(win, LRU_TAPS - 1 - k, rows)
    return u, win


def _lru_gates(u, wa_ref, ba_ref, wx_ref, bx_ref, sp):
    ub = u.astype(BF16)
    r = _sigmoid(_dot(ub, wa_ref[...]) + ba_ref[...])
    i = _sigmoid(_dot(ub, wx_ref[...]) + bx_ref[...])
    la = (-LRU_C) * r * sp
    a = jnp.exp(la)
    return ub, r, i, a, _one_minus_exp(2.0 * la, a * a)


def _scan_forward(bufs, S, CH):
    for n, s in enumerate(SCAN_SHIFTS):
        (sa, sb), (da, db) = bufs[n % 2], bufs[(n + 1) % 2]

        def step(base, s=s, sa=sa, sb=sb, da=da, db=db):
            a, b = sa[pl.ds(base, CH), :], sb[pl.ds(base, CH), :]
            if s < 8:
                a_s = _shift_down(jnp.concatenate([_prev8(sa, base, ALL, 1.0), a], axis=0), s, CH)
                b_s = _shift_down(jnp.concatenate([_prev8(sb, base, ALL, 0.0), b], axis=0), s, CH)
            elif s < CH:
                start = pl.multiple_of(jnp.maximum(base - s, 0), 8)
                a_s = jnp.concatenate([jnp.where(base > 0, sa[pl.ds(start, s), :], 1.0), a[0:CH - s]], axis=0)
                b_s = jnp.concatenate([jnp.where(base > 0, sb[pl.ds(start, s), :], 0.0), b[0:CH - s]], axis=0)
            else:
                start = pl.multiple_of(jnp.maximum(base - s, 0), 8)
                a_s = jnp.where(base < s, 1.0, sa[pl.ds(start, CH), :])
                b_s = jnp.where(base < s, 0.0, sb[pl.ds(start, CH), :])
            db[pl.ds(base, CH), :] = a * b_s + b
            da[pl.ds(base, CH), :] = a * a_s

        _for_chunks(S, step, CH)
    return len(SCAN_SHIFTS) % 2


def _scan_backward(bufs, S, CH):
    for n, s in enumerate(SCAN_SHIFTS):
        (sa, sb), (da, db) = bufs[n % 2], bufs[(n + 1) % 2]

        def step(base, s=s, sa=sa, sb=sb, da=da, db=db):
            a, b = sa[pl.ds(base, CH), :], sb[pl.ds(base, CH), :]
            if s < 8:
                a_s = _shift_up(jnp.concatenate([a, _next8(sa, base, CH, S, ALL, 1.0)], axis=0), s, CH)
                b_s = _shift_up(jnp.concatenate([b, _next8(sb, base, CH, S, ALL, 0.0)], axis=0), s, CH)
            elif s < CH:
                start = pl.multiple_of(jnp.minimum(base + CH, S - s), 8)
                more = base + CH < S
                a_s = jnp.concatenate([a[s:CH], jnp.where(more, sa[pl.ds(start, s), :], 1.0)], axis=0)
                b_s = jnp.concatenate([b[s:CH], jnp.where(more, sb[pl.ds(start, s), :], 0.0)], axis=0)
            else:
                start = pl.multiple_of(jnp.minimum(base + s, S - CH), 8)
                a_s = jnp.where(base + s >= S, 1.0, sa[pl.ds(start, CH), :])
                b_s = jnp.where(base + s >= S, 0.0, sb[pl.ds(start, CH), :])
            db[pl.ds(base, CH), :] = a * b_s + b
            da[pl.ds(base, CH), :] = a * a_s

        _for_chunks(S, step, CH)
    return len(SCAN_SHIFTS) % 2


def _lru_fwd(li, cw, cb, wa, ba, wx, bx, lam, ex=None):
    B, S, _ = li.shape
    CH = LRU_FWD_CHUNK

    def body(li_ref, cw_ref, cb_ref, wa_ref, ba_ref, wx_ref, bx_ref, lam_ref, y_ref, h_ref, a0, b0, a1, b1):
        sp = _softplus(-lam_ref[...])

        def gates(base):
            u, _ = _lru_conv(li_ref, base, CH, cw_ref, cb_ref)
            _, _, i, a, em = _lru_gates(u, wa_ref, ba_ref, wx_ref, bx_ref, sp)
            a0[pl.ds(base, CH), :] = a
            b0[pl.ds(base, CH), :] = jnp.sqrt(em) * (i * u)

        _for_chunks(S, gates, CH)
        bufs = ((a0, b0), (a1, b1))
        hb = bufs[_scan_forward(bufs, S, CH)][1]

        def out(base):
            h = hb[pl.ds(base, CH), :]
            h_ref[pl.ds(base, CH), :] = h
            gl, _ = _gelu(li_ref[pl.ds(base, CH), LRU_GATE])
            y_ref[pl.ds(base, CH), :] = (gl * h).astype(BF16)

        _for_chunks(S, out, CH)

    vec = pl.BlockSpec((1, LRU_W), lambda i: (0, 0))
    mat = pl.BlockSpec((LRU_W, LRU_W), lambda i: (0, 0))
    seq = lambda width: pl.BlockSpec((None, S, width), lambda i: (i, 0, 0))
    return _pallas_hosting(
        body, ex, name="lru_fwd", grid=(B,),
        in_specs=[seq(LRU_IN_W), pl.BlockSpec((LRU_TAPS, LRU_W), lambda i: (0, 0)), vec, mat, vec, mat, vec, vec],
        out_specs=[seq(LRU_W), seq(LRU_W)],
        out_shape=[jax.ShapeDtypeStruct((B, S, LRU_W), BF16), jax.ShapeDtypeStruct((B, S, LRU_W), F32)],
        scratch_shapes=[pltpu.VMEM((S, LRU_W), F32)] * 4,
        semantics=("parallel",), operands=(li, cw, cb, wa, ba, wx, bx, lam))


def _lru_bwd(li, hs, dy, cw, cb, wa, ba, wx, bx, lam, ex=None):
    B, S, _ = li.shape
    CH = LRU_BWD_CHUNK

    def body(li_ref, hs_ref, dy_ref, cw_ref, cb_ref, wa_ref, ba_ref, wx_ref, bx_ref, lam_ref,
             dli_ref, dcw_ref, dcb_ref, dwa_ref, dba_ref, dwx_ref, dbx_ref, dlam_ref, a0, b0, a1, b1, u_s, du_s):
        @pl.when(pl.program_id(0) == 0)
        def _():
            for ref in (dcw_ref, dcb_ref, dwa_ref, dba_ref, dwx_ref, dbx_ref, dlam_ref):
                ref[...] = jnp.zeros_like(ref)

        lam_v = lam_ref[...]
        sp = _softplus(-lam_v)
        dsp_dlam = -_sigmoid(-lam_v)

        def gates(base):
            u, _ = _lru_conv(li_ref, base, CH, cw_ref, cb_ref)
            _, _, _, a, _ = _lru_gates(u, wa_ref, ba_ref, wx_ref, bx_ref, sp)
            gl, _ = _gelu(li_ref[pl.ds(base, CH), LRU_GATE])
            u_s[pl.ds(base, CH), :] = u
            a0[pl.ds(base, CH), :] = a
            b0[pl.ds(base, CH), :] = a * (dy_ref[pl.ds(base, CH), :] * gl)

        _for_chunks(S, gates, CH)
        bufs = ((a0, b0), (a1, b1))
        eb = bufs[_scan_backward(bufs, S, CH)][1]

        def grads(base):
            e = eb[pl.ds(base, CH), :]
            e_next = _shift_up(jnp.concatenate([e, _next8(eb, base, CH, S, ALL, 0.0)], axis=0), 1, CH)
            gate = li_ref[pl.ds(base, CH), LRU_GATE]
            gl, th = _gelu(gate)
            dy = dy_ref[pl.ds(base, CH), :]
            g = dy * gl + e_next
            h = hs_ref[pl.ds(base, CH), :]
            h_prev = _shift_down(jnp.concatenate([_prev8(hs_ref, base, ALL, 0.0), h], axis=0), 1, CH)
            dli_ref[pl.ds(base, CH), LRU_GATE] = (dy * h * _gelu_grad(gate, th)).astype(BF16)
            u = u_s[pl.ds(base, CH), :]
            ub, r, i, a, em = _lru_gates(u, wa_ref, ba_ref, wx_ref, bx_ref, sp)
            mult = jnp.sqrt(em)
            da = g * h_prev
            dmult = g * (i * u)
            di = g * mult * u
            dla = da * a - dmult * (a * a) * lax.rsqrt(jnp.maximum(em, 1e-30))
            dlam_ref[...] += dsp_dlam * jnp.sum(dla * ((-LRU_C) * r), axis=0, keepdims=True)
            dpa = (dla * ((-LRU_C) * sp)) * r * (1.0 - r)
            dpx = di * i * (1.0 - i)
            dpab, dpxb = dpa.astype(BF16), dpx.astype(BF16)
            dwa_ref[...] += _dot_tn(ub, dpab)
            dwx_ref[...] += _dot_tn(ub, dpxb)
            dba_ref[...] += jnp.sum(dpa, axis=0, keepdims=True)
            dbx_ref[...] += jnp.sum(dpx, axis=0, keepdims=True)
            du = g * mult * i + _dot_nt(dpab, wa_ref[...]) + _dot_nt(dpxb, wx_ref[...])
            du_s[pl.ds(base, CH), :] = du
            dcb_ref[...] += jnp.sum(du, axis=0, keepdims=True)

        _for_chunks(S, grads, CH)

        def conv_bwd(base):
            du = du_s[pl.ds(base, CH), :]
            dwin = jnp.concatenate([du, _next8(du_s, base, CH, S, ALL, 0.0)], axis=0)
            xwin = jnp.concatenate([_prev8(li_ref, base, LRU_X, 0.0), li_ref[pl.ds(base, CH), LRU_X]], axis=0)
            dx = jnp.zeros((CH, LRU_W), F32)
            for k in range(LRU_TAPS):
                dx = dx + cw_ref[k:k + 1, :] * _shift_up(dwin, LRU_TAPS - 1 - k, CH)
                dcw_ref[k:k + 1, :] += jnp.sum(du * _shift_down(xwin, LRU_TAPS - 1 - k, CH), axis=0, keepdims=True)
            dli_ref[pl.ds(base, CH), LRU_X] = dx.astype(BF16)

        _for_chunks(S, conv_bwd, CH)

    vec = pl.BlockSpec((1, LRU_W), lambda i: (0, 0))
    mat = pl.BlockSpec((LRU_W, LRU_W), lambda i: (0, 0))
    taps = pl.BlockSpec((LRU_TAPS, LRU_W), lambda i: (0, 0))
    seq = lambda width: pl.BlockSpec((None, S, width), lambda i: (i, 0, 0))
    vec_shape = jax.ShapeDtypeStruct((1, LRU_W), F32)
    mat_shape = jax.ShapeDtypeStruct((LRU_W, LRU_W), F32)
    return _pallas_hosting(
        body, ex, name="lru_bwd", grid=(B,),
        in_specs=[seq(LRU_IN_W), seq(LRU_W), seq(LRU_W), taps, vec, mat, vec, mat, vec, vec],
        out_specs=[seq(LRU_IN_W), taps, vec, mat, vec, mat, vec, vec],
        out_shape=[jax.ShapeDtypeStruct((B, S, LRU_IN_W), BF16), jax.ShapeDtypeStruct((LRU_TAPS, LRU_W), F32),
                   vec_shape, mat_shape, vec_shape, mat_shape, vec_shape, vec_shape],
        scratch_shapes=[pltpu.VMEM((S, LRU_W), F32)] * 6,
        semantics=("arbitrary",), operands=(li, hs, dy, cw, cb, wa, ba, wx, bx, lam))


MESH = pl.DeviceIdType.MESH
HBM_SPEC = pl.BlockSpec(memory_space=pltpu.HBM)


def _slot(ref, p):
    return ref.at[p]


def _row_block(rows):
    return lambda ref, p: ref.at[pl.ds(p * rows, rows), :]


def _col_block(cols):
    return lambda ref, p: ref.at[:, pl.ds(p * cols, cols)]


class _Gather:
    def __init__(self, blocks, out_shapes, places):
        self.sources, self.out_shapes, self.places, self.n = list(blocks), list(out_shapes), list(places), len(blocks)

    def scratch(self):
        return [pltpu.SemaphoreType.DMA((self.n, 7)), pltpu.SemaphoreType.DMA((self.n, 7)), pltpu.SemaphoreType.DMA((self.n,))]

    def _plan(self, x_refs, out_refs, send_sems, recv_sems, local_sems):
        n = self.n
        x, y, c = lax.axis_index("x"), lax.axis_index("y"), lax.axis_index("c")
        me, sibling = (x, y, c), (x, y, 1 - c)
        chips = [(1 - x, y), (x, 1 - y), (1 - x, 1 - y)]

        def place(a, dev):
            return self.places[a](out_refs[a], 4 * dev[0] + 2 * dev[1] + dev[2])

        def copy(a, k, blk, to, src=None):
            return pltpu.make_async_remote_copy(
                src_ref=place(a, blk) if src is None else src, dst_ref=place(a, blk),
                send_sem=send_sems.at[a, k], recv_sem=recv_sems.at[a, k], device_id=to, device_id_type=MESH)

        mine = [pltpu.make_async_copy(x_refs[a], place(a, me), local_sems.at[a]) for a in range(n)]
        first = [copy(a, 0, me, sibling, src=x_refs[a]) for a in range(n)]
        first += [copy(a, 1 + j, me, (*chip, c), src=x_refs[a]) for j, chip in enumerate(chips) for a in range(n)]
        return me, sibling, chips, c, copy, mine, first

    def start(self, *refs):
        *_, mine, first = self._plan(*refs)
        for cp in mine + first:
            cp.start()

    def forward(self, *refs):
        me, sibling, chips, c, copy, _, _ = self._plan(*refs)
        for j, chip in enumerate(chips):
            for a in range(self.n):
                copy(a, 1 + j, (*chip, c), me).wait_recv()
                copy(a, 4 + j, (*chip, c), sibling).start()

    def finish(self, *refs):
        me, sibling, chips, c, copy, mine, first = self._plan(*refs)
        passed = [copy(a, 4 + j, (*chip, c), sibling) for j, chip in enumerate(chips) for a in range(self.n)]
        for a in range(self.n):
            copy(a, 0, sibling, me).wait_recv()
        for j, chip in enumerate(chips):
            for a in range(self.n):
                copy(a, 4 + j, (*chip, 1 - c), me).wait_recv()
        for cp in first + passed:
            cp.wait_send()
        for cp in mine:
            cp.wait()


class _DirectGather(_Gather):
    def _copies(self, x_refs, out_refs, send_sems, recv_sems, local_sems):
        x, y, c = lax.axis_index("x"), lax.axis_index("y"), lax.axis_index("c")
        me = 4 * x + 2 * y + c
        mine = [pltpu.make_async_copy(x_refs[a], self.places[a](out_refs[a], me), local_sems.at[a]) for a in range(self.n)]
        remote = []
        for k in range(1, N_DEV):
            peer = (x ^ ((k >> 2) & 1), y ^ ((k >> 1) & 1), c ^ (k & 1))
            for a in range(self.n):
                remote.append(pltpu.make_async_remote_copy(
                    src_ref=x_refs[a], dst_ref=self.places[a](out_refs[a], me), send_sem=send_sems.at[a, k - 1],
                    recv_sem=recv_sems.at[a, k - 1], device_id=peer, device_id_type=MESH))
        return mine, remote

    def start(self, *refs):
        mine, remote = self._copies(*refs)
        for cp in mine + remote:
            cp.start()

    def forward(self, *refs):
        pass

    def finish(self, *refs):
        mine, remote = self._copies(*refs)
        for cp in remote:
            cp.wait_recv()
        for cp in remote:
            cp.wait_send()
        for cp in mine:
            cp.wait()


class _GradExchange:
    def __init__(self, sources, takes, piece_shapes):
        self.sources, self.takes, self.n = list(sources), list(takes), len(sources)
        self.out_shapes = [jax.ShapeDtypeStruct((N_DEV,) + tuple(s), BF16) for s in piece_shapes]

    def scratch(self):
        return [pltpu.SemaphoreType.DMA((self.n, 7)), pltpu.SemaphoreType.DMA((self.n, 7)), pltpu.SemaphoreType.DMA((self.n,))]

    def _copies(self, src_refs, out_refs, send_sems, recv_sems, local_sems):
        x, y, c = lax.axis_index("x"), lax.axis_index("y"), lax.axis_index("c")
        me = 4 * x + 2 * y + c
        mine = [pltpu.make_async_copy(self.takes[i](src_refs[i], me), out_refs[i].at[me], local_sems.at[i]) for i in range(self.n)]
        remote = []
        for k in range(1, N_DEV):
            px, py, pc = x ^ ((k >> 2) & 1), y ^ ((k >> 1) & 1), c ^ (k & 1)
            peer = 4 * px + 2 * py + pc
            for i in range(self.n):
                remote.append(pltpu.make_async_remote_copy(
                    src_ref=self.takes[i](src_refs[i], peer), dst_ref=out_refs[i].at[me], send_sem=send_sems.at[i, k - 1],
                    recv_sem=recv_sems.at[i, k - 1], device_id=(px, py, pc), device_id_type=MESH))
        return mine, remote

    def start(self, *refs):
        mine, remote = self._copies(*refs)
        for cp in mine + remote:
            cp.start()

    def forward(self, *refs):
        pass

    def finish(self, *refs):
        mine, remote = self._copies(*refs)
        for cp in remote:
            cp.wait_recv()
        for cp in remote:
            cp.wait_send()
        for cp in mine:
            cp.wait()


def _run_exchange(ex, name):
    def body(*refs):
        src_refs, out_refs, sems = refs[:ex.n], refs[ex.n:2 * ex.n], refs[2 * ex.n:]
        ex.start(src_refs, out_refs, *sems)
        ex.forward(src_refs, out_refs, *sems)
        ex.finish(src_refs, out_refs, *sems)

    return pl.pallas_call(
        body, name=name, out_shape=ex.out_shapes, in_specs=[HBM_SPEC] * ex.n, out_specs=[HBM_SPEC] * ex.n,
        scratch_shapes=ex.scratch(),
    )(*ex.sources)


def _pallas_hosting(body, ex, *, name, grid, in_specs, out_specs, out_shape, scratch_shapes, semantics, operands):
    if ex is None:
        outs = pl.pallas_call(body, name=name, grid=grid, in_specs=in_specs, out_specs=out_specs, out_shape=out_shape,
                              scratch_shapes=scratch_shapes, compiler_params=_params(*semantics))(*operands)
        return outs, None
    n_in, n_out, n_scr, n = len(in_specs), len(out_specs), len(scratch_shapes), ex.n

    def at_step(pick):
        conds = [pl.program_id(k) == pick(size) for k, size in enumerate(grid)]
        return functools.reduce(jnp.logical_and, conds)

    def hosting(*refs):
        ins, ex_ins = refs[:n_in], refs[n_in:n_in + n]
        outs, ex_outs = refs[n_in + n:n_in + n + n_out], refs[n_in + n + n_out:n_in + 2 * n + n_out]
        scratch, sems = refs[n_in + 2 * n + n_out:n_in + 2 * n + n_out + n_scr], refs[n_in + 2 * n + n_out + n_scr:]
        pl.when(at_step(lambda size: 0))(lambda: ex.start(ex_ins, ex_outs, *sems))
        pl.when(at_step(lambda size: size - 1))(lambda: ex.forward(ex_ins, ex_outs, *sems))
        body(*ins, *outs, *scratch)
        pl.when(at_step(lambda size: size - 1))(lambda: ex.finish(ex_ins, ex_outs, *sems))

    res = pl.pallas_call(
        hosting, name=name, grid=grid, in_specs=list(in_specs) + [HBM_SPEC] * n, out_specs=list(out_specs) + [HBM_SPEC] * n,
        out_shape=list(out_shape) + ex.out_shapes, scratch_shapes=list(scratch_shapes) + ex.scratch(),
        compiler_params=_params(*(["arbitrary"] * len(grid))),
    )(*operands, *ex.sources)
    return res[:n_out], res[n_out:]


def _adamw(w, g, m, v, rows_per_step, name):
    R, C = w.shape
    c1 = 1.0 - ADAM_B1 ** ADAM_STEP
    c2 = 1.0 - ADAM_B2 ** ADAM_STEP

    def body(w_ref, g_ref, m_ref, v_ref, d_ref, nm_ref, nv_ref):
        gv = g_ref[...]
        nm = ADAM_B1 * m_ref[...] + (1.0 - ADAM_B1) * gv
        nv = ADAM_B2 * v_ref[...] + (1.0 - ADAM_B2) * (gv * gv)
        nm_ref[...] = nm
        nv_ref[...] = nv
        d_ref[...] = (-ADAM_LR) * ((nm / c1) / (jnp.sqrt(nv / c2) + ADAM_EPS) + ADAM_WD * w_ref[...])

    spec = pl.BlockSpec((rows_per_step, C), lambda i: (i, 0))
    shape = jax.ShapeDtypeStruct((R, C), F32)
    return pl.pallas_call(
        body, name=name, grid=(R // rows_per_step,),
        in_specs=[spec] * 4, out_specs=[spec] * 3, out_shape=[shape] * 3,
        compiler_params=_params("parallel"),
    )(w, g, m, v)


def _adam_update(w, g, m, v):
    nm = ADAM_B1 * m + (1.0 - ADAM_B1) * g
    nv = ADAM_B2 * v + (1.0 - ADAM_B2) * (g * g)
    c1 = 1.0 - ADAM_B1 ** ADAM_STEP
    c2 = 1.0 - ADAM_B2 ** ADAM_STEP
    return (-ADAM_LR) * ((nm / c1) / (jnp.sqrt(nv / c2) + ADAM_EPS) + ADAM_WD * w), nm, nv


def _small_sum_adamw(gathered, loss_slots, state):
    names, updated = list(gathered), list(state)
    n_g, n_u = len(names), len(updated)

    def slot_sum(ref):
        acc = ref[0]
        for q in range(1, N_DEV):
            acc = acc + ref[q]
        return acc

    def body(*refs):
        p_refs, loss_in = refs[:n_g], refs[n_g]
        st = refs[n_g + 1:n_g + 1 + 3 * n_u]
        outs = refs[n_g + 1 + 3 * n_u:]
        g_refs, loss_out, upd = outs[:n_g], outs[n_g], outs[n_g + 1:]
        loss_out[...] = slot_sum(loss_in)
        for i, n in enumerate(names):
            g = slot_sum(p_refs[i])
            g_refs[i][...] = g
            if n in state:
                j = updated.index(n)
                d, nm, nv = _adam_update(st[3 * j][...], g, st[3 * j + 1][...], st[3 * j + 2][...])
                upd[3 * j][...], upd[3 * j + 1][...], upd[3 * j + 2][...] = d, nm, nv

    g_shapes = [jax.ShapeDtypeStruct(gathered[n].shape[1:], F32) for n in names]
    u_shapes = [jax.ShapeDtypeStruct(state[n][0].shape, F32) for n in updated for _ in range(3)]
    outs = pl.pallas_call(
        body, name="small_sum_adamw", out_shape=g_shapes + [jax.ShapeDtypeStruct((8, 128), F32)] + u_shapes,
        compiler_params=_params(),
    )(*[gathered[n] for n in names], loss_slots, *[a for n in updated for a in state[n]])
    g = dict(zip(names, outs[:n_g]))
    upd = {n: tuple(outs[n_g + 1 + 3 * j:n_g + 4 + 3 * j]) for j, n in enumerate(updated)}
    return g, outs[n_g], upd


WEIGHT_ORDER = ("norm1_g", "w_in", "conv_dw_w", "conv_dw_b", "conv_ln_g", "conv_ln_b", "lru_conv_w", "lru_conv_b", "lru_wa",
                "lru_ba", "lru_wx", "lru_bx", "lru_lambda", "w_out", "norm2_g", "w_up", "w_down", "final_g")
BIG = ("w_in", "w_out", "w_up", "w_down")
SMALL_SHARDED = {"conv_dw_w": (DEPTH, CONV_TAPS, CONV_W), "lru_conv_w": (DEPTH, LRU_TAPS, LRU_W)}
SMALL_FULL = {
    "norm1_g": (DEPTH, D_MODEL), "conv_dw_w": (DEPTH, CONV_TAPS, CONV_W), "conv_dw_b": (DEPTH, CONV_W),
    "conv_ln_g": (DEPTH, CONV_W), "conv_ln_b": (DEPTH, CONV_W), "lru_conv_w": (DEPTH, LRU_TAPS, LRU_W),
    "lru_conv_b": (DEPTH, LRU_W), "lru_wa": (DEPTH, LRU_HEADS, HEAD_DIM, HEAD_DIM), "lru_ba": (DEPTH, LRU_W),
    "lru_wx": (DEPTH, LRU_HEADS, HEAD_DIM, HEAD_DIM), "lru_bx": (DEPTH, LRU_W), "lru_lambda": (DEPTH, LRU_W),
    "norm2_g": (DEPTH, D_MODEL), "final_g": (D_MODEL,),
}
SMALL_COLS = 128
FILTER_ROWS = 24
W_IN_SHARD = IN_COLS // N_DEV
W_OUT_SHARD = D_MODEL // N_DEV
FF_SHARD = D_FF // N_DEV


def _pack_rows(flat_parts, cols, rows):
    flat = jnp.concatenate(flat_parts)
    return jnp.pad(flat, (0, rows * cols - flat.shape[0])).reshape(rows, cols)


WEIGHT_GATHER = {
    "w_in": ((N_DEV, D_MODEL, W_IN_SHARD), _slot),
    "w_out": ((D_MODEL, D_MODEL), _row_block(W_OUT_SHARD)),
    "w_up": ((D_MODEL, D_FF), _col_block(FF_SHARD)),
    "w_down": ((D_FF, D_MODEL), _row_block(FF_SHARD)),
}
GATHER_HOSTS = {
    "inproj": (("w_out", 0),), "attn_fwd": (("w_up", 0),), "lru_fwd": (("w_down", 0),),
    "outproj": (("w_in", 1),), "up": (("w_up", 1),), "down": (("w_down", 1), ("w_out", 1)),
}


def _weight_gather(local, items, with_filters=False):
    blocks = [local[n][l].astype(BF16) for n, l in items]
    shapes = [jax.ShapeDtypeStruct(WEIGHT_GATHER[n][0], BF16) for n, _ in items]
    places = [WEIGHT_GATHER[n][1] for n, _ in items]
    if with_filters:
        blocks.append(_pack_rows([local[n].reshape(-1) for n in SMALL_SHARDED], SMALL_COLS, FILTER_ROWS))
        shapes.append(jax.ShapeDtypeStruct((N_DEV, FILTER_ROWS, SMALL_COLS), F32))
        places.append(_slot)
    return _Gather(blocks, shapes, places)


def _keep_gathered(full, items, landed):
    for (n, l), arr in zip(items, landed):
        full[n][l] = arr.transpose(1, 0, 2).reshape(D_MODEL, IN_COLS) if n == "w_in" else arr


def _unpack_filters(slots):
    flat, off, out = slots.reshape(N_DEV, -1), 0, {}
    for n, shp in SMALL_SHARDED.items():
        shard = shp[:-1] + (shp[-1] // N_DEV,)
        size = int(np.prod(shard))
        out[n] = jnp.moveaxis(flat[:, off:off + size].reshape((N_DEV,) + shard), 0, -2).reshape(shp)
        off += size
    return out


def _grad_exchange(name, dw):
    if name == "w_in":
        return _GradExchange([dw.reshape(D_MODEL, N_DEV, W_IN_SHARD).transpose(1, 0, 2)], [_slot], [(D_MODEL, W_IN_SHARD)])
    if name == "w_out":
        return _GradExchange([dw], [_row_block(W_OUT_SHARD)], [(W_OUT_SHARD, D_MODEL)])
    return _GradExchange([dw], [_col_block(FF_SHARD)], [(D_MODEL, FF_SHARD)])


def _sum_adamw(parts, w, m, v, rows_per_step, transposed, name):
    _, R, C = parts[0].shape
    tr = rows_per_step
    steps = R // tr
    c1 = 1.0 - ADAM_B1 ** ADAM_STEP
    c2 = 1.0 - ADAM_B2 ** ADAM_STEP

    def slot_sum(p_ref):
        acc = p_ref[0].astype(F32)
        for q in range(1, N_DEV):
            acc = acc + p_ref[q].astype(F32)
        return acc

    def body(p0_ref, p1_ref, w_ref, m_ref, v_ref, g_ref, d_ref, nm_ref, nv_ref):
        gv = jnp.where(pl.program_id(0) == 0, slot_sum(p0_ref), slot_sum(p1_ref))
        if transposed:
            gv = gv.T
        nm = ADAM_B1 * m_ref[...] + (1.0 - ADAM_B1) * gv
        nv = ADAM_B2 * v_ref[...] + (1.0 - ADAM_B2) * (gv * gv)
        g_ref[...] = gv
        nm_ref[...] = nm
        nv_ref[...] = nv
        d_ref[...] = (-ADAM_LR) * ((nm / c1) / (jnp.sqrt(nv / c2) + ADAM_EPS) + ADAM_WD * w_ref[...])

    if transposed:
        spec = pl.BlockSpec((None, C, tr), lambda l, i: (l, 0, i))
    else:
        spec = pl.BlockSpec((None, tr, C), lambda l, i: (l, i, 0))
    shape = jax.ShapeDtypeStruct(w.shape, F32)
    part0 = pl.BlockSpec((N_DEV, tr, C), lambda l, i: (0, jnp.where(l == 0, i, steps - 1), 0))
    part1 = pl.BlockSpec((N_DEV, tr, C), lambda l, i: (0, jnp.where(l == 1, i, 0), 0))
    return pl.pallas_call(
        body, name=name, grid=(DEPTH, steps),
        in_specs=[part0, part1, spec, spec, spec],
        out_specs=[spec] * 4, out_shape=[shape] * 4,
        compiler_params=_params("arbitrary", "arbitrary"),
    )(parts[0], parts[1], w, m, v)


def _block_diag(w):
    eye = jnp.eye(LRU_HEADS, dtype=bool)
    return jnp.where(eye[:, None, :, None], w[:, :, None, :], jnp.zeros((), w.dtype)).reshape(LRU_W, LRU_W)


def _diag_blocks(m):
    eye = jnp.eye(LRU_HEADS, dtype=bool)
    m4 = m.reshape(LRU_HEADS, HEAD_DIM, LRU_HEADS, HEAD_DIM)
    return jnp.sum(jnp.where(eye[:, None, :, None], m4, 0.0), axis=2)


def kernel(x, norm1_g, w_in, conv_dw_w, conv_dw_b, conv_ln_g, conv_ln_b, lru_conv_w, lru_conv_b, lru_wa, lru_ba, lru_wx, lru_bx, lru_lambda, w_out, norm2_g, w_up, w_down, final_g, loss_target, m_norm1_g, m_w_in, m_conv_dw_w, m_conv_dw_b, m_conv_ln_g, m_conv_ln_b, m_lru_conv_w, m_lru_conv_b, m_lru_wa, m_lru_ba, m_lru_wx, m_lru_bx, m_lru_lambda, m_w_out, m_norm2_g, m_w_up, m_w_down, m_final_g, v_norm1_g, v_w_in, v_conv_dw_w, v_conv_dw_b, v_conv_ln_g, v_conv_ln_b, v_lru_conv_w, v_lru_conv_b, v_lru_wa, v_lru_ba, v_lru_wx, v_lru_bx, v_lru_lambda, v_w_out, v_norm2_g, v_w_up, v_w_down, v_final_g):
    local = dict(zip(WEIGHT_ORDER, (norm1_g, w_in, conv_dw_w, conv_dw_b, conv_ln_g, conv_ln_b, lru_conv_w, lru_conv_b, lru_wa,
                                    lru_ba, lru_wx, lru_bx, lru_lambda, w_out, norm2_g, w_up, w_down, final_g)))
    mom1 = dict(zip(WEIGHT_ORDER, (m_norm1_g, m_w_in, m_conv_dw_w, m_conv_dw_b, m_conv_ln_g, m_conv_ln_b, m_lru_conv_w,
                                   m_lru_conv_b, m_lru_wa, m_lru_ba, m_lru_wx, m_lru_bx, m_lru_lambda, m_w_out, m_norm2_g,
                                   m_w_up, m_w_down, m_final_g)))
    mom2 = dict(zip(WEIGHT_ORDER, (v_norm1_g, v_w_in, v_conv_dw_w, v_conv_dw_b, v_conv_ln_g, v_conv_ln_b, v_lru_conv_w,
                                   v_lru_conv_b, v_lru_wa, v_lru_ba, v_lru_wx, v_lru_bx, v_lru_lambda, v_w_out, v_norm2_g,
                                   v_w_up, v_w_down, v_final_g)))
    B, S, _ = x.shape
    T = B * S
    my_slot = 4 * lax.axis_index("x") + 2 * lax.axis_index("y") + lax.axis_index("c")
    row = lambda a: a.reshape(1, -1)

    full = {n: [None] * DEPTH for n in BIG}
    first_items = (("w_in", 0),)
    landed = _run_exchange(_weight_gather(local, first_items, with_filters=True), "gather_first_weights")
    _keep_gathered(full, first_items, landed)
    full.update(_unpack_filters(landed[-1]))

    def hosted(call, layer, fn, *args):
        items = GATHER_HOSTS[call] if layer == 0 else ()
        outs, landed = fn(*args, ex=_weight_gather(local, items) if items else None)
        _keep_gathered(full, items, landed or ())
        return outs

    saved = []
    cur = x.reshape(T, D_MODEL)
    for l in range(DEPTH):
        h, qkv, ci, li = hosted("inproj", l, _inproj, cur, row(norm1_g[l]), full["w_in"][l])
        qkv = qkv.reshape(B, S, QKV_W)
        o, lse = hosted("attn_fwd", l, _attn_fwd, qkv)
        o, lse = o.reshape(T, ATTN_W), lse.reshape(T, ATTN_W)
        ci = ci.reshape(B, S, CONV_IN_W)
        li = li.reshape(B, S, LRU_IN_W)
        conv_p = (full["conv_dw_w"][l], row(conv_dw_b[l]), row(conv_ln_g[l]), row(conv_ln_b[l]))
        lru_p = (full["lru_conv_w"][l], row(lru_conv_b[l]), _block_diag(lru_wa[l]).astype(BF16), row(lru_ba[l]),
                 _block_diag(lru_wx[l]).astype(BF16), row(lru_bx[l]), row(lru_lambda[l]))
        yc, cpre = _conv_fwd(ci, *conv_p)
        yl, hs = hosted("lru_fwd", l, _lru_fwd, li, *lru_p)
        x1, mix = hosted("outproj", l, _outproj, cur, o, lse, yc.reshape(T, CONV_W), yl.reshape(T, LRU_W), full["w_out"][l])
        h2, r = hosted("up", l, _up, x1, row(norm2_g[l]), full["w_up"][l])
        (x2,) = hosted("down", l, _down, x1, r, full["w_down"][l])
        saved.append(dict(x=cur, h=h, qkv=qkv, o=o, lse=lse, ci=ci, li=li, cpre=cpre, hs=hs, x1=x1, mix=mix, h2=h2, r=r,
                          conv_p=conv_p, lru_p=lru_p))
        cur = x2

    dx, loss_part, dgf = _loss_head(cur, loss_target.reshape(T, D_MODEL), row(final_g))

    received = {n: [None] * DEPTH for n in BIG}
    small_grads = {n: [None] * DEPTH for n in SMALL_FULL if n != "final_g"}
    for l in reversed(range(DEPTH)):
        sv = saved[l]
        (dpre,), _ = _down_bwd_act(dx, sv["r"], full["w_down"][l])
        dw_down = _down_bwd_w(sv["r"], dx)
        (dx1, dg2), _ = _up_bwd_act(dpre, full["w_up"][l], sv["x1"], row(norm2_g[l]), dx)
        dw_up = _up_bwd_w(sv["h2"], dpre)
        do, dd, dyc, dyl, dw_out = _outproj_bwd(dx1, sv["mix"], full["w_out"][l], sv["o"], sv["lse"])
        seq = lambda a: a.reshape(B, S, ATTN_W)
        (dq, dk, dv), (received["w_down"][l],) = _attn_bwd(sv["qkv"], seq(do), seq(sv["lse"]), seq(dd),
                                                           ex=_grad_exchange("w_down", dw_down))
        (dci, dcw, dcb, dlg, dlb), (received["w_out"][l],) = _conv_bwd(
            sv["ci"], sv["cpre"], dyc.reshape(B, S, CONV_W), sv["conv_p"][0], sv["conv_p"][2], sv["conv_p"][3],
            ex=_grad_exchange("w_out", dw_out))
        (dli, dlcw, dlcb, dwa, dba, dwx, dbx, dlam), (received["w_up"][l],) = _lru_bwd(
            sv["li"], sv["hs"], dyl.reshape(B, S, LRU_W), *sv["lru_p"], ex=_grad_exchange("w_up", dw_up))
        dz = tuple(t.reshape(T, -1) for t in (dq, dk, dv, dci, dli))
        dw_in = _inproj_bwd_w(dz, sv["h"])
        (dx, dg1), (received["w_in"][l],) = _inproj_bwd_act(dz, full["w_in"][l], sv["x"], row(norm1_g[l]), dx1,
                                                            ex=_grad_exchange("w_in", dw_in))
        for n, g in (("norm1_g", dg1), ("conv_dw_w", dcw), ("conv_dw_b", dcb), ("conv_ln_g", dlg), ("conv_ln_b", dlb),
                     ("lru_conv_w", dlcw), ("lru_conv_b", dlcb), ("lru_wa", _diag_blocks(dwa)), ("lru_ba", dba),
                     ("lru_wx", _diag_blocks(dwx)), ("lru_bx", dbx), ("lru_lambda", dlam), ("norm2_g", dg2)):
            small_grads[n][l] = g.reshape(SMALL_FULL[n][1:])
    grad_x = dx.reshape(B, S, D_MODEL)

    two_d = lambda n: (int(np.prod(SMALL_FULL[n][:-1])), SMALL_FULL[n][-1])
    small_local = {n: jnp.stack(g).reshape(two_d(n)) for n, g in small_grads.items()}
    small_local["final_g"] = dgf
    small_names = list(SMALL_FULL)
    landed = _run_exchange(
        _DirectGather([small_local[n] for n in small_names] + [loss_part],
                [jax.ShapeDtypeStruct((N_DEV,) + two_d(n), F32) for n in small_names] + [jax.ShapeDtypeStruct((N_DEV, 8, 128), F32)],
                [_slot] * (len(small_names) + 1)),
        "gather_small_grads")
    gathered = dict(zip(small_names, landed[:-1]))

    grads, delta, new_m, new_v = {}, {}, {}, {}
    for n, rows_per_step in (("w_in", 256), ("w_out", W_OUT_SHARD), ("w_up", 256), ("w_down", 256)):
        grads[n], delta[n], new_m[n], new_v[n] = _sum_adamw(received[n], local[n], mom1[n], mom2[n], rows_per_step,
                                                            n == "w_down", "sum_adamw_" + n)

    replicated = [n for n in SMALL_FULL if n not in SMALL_SHARDED]
    state = {n: tuple(src[n].reshape(two_d(n)) for src in (local, mom1, mom2)) for n in replicated}
    small_g, loss_sum, updated = _small_sum_adamw(gathered, landed[-1], state)
    loss = loss_sum[0, 0]
    for n in replicated:
        grads[n] = small_g[n].reshape(SMALL_FULL[n])
        delta[n], new_m[n], new_v[n] = (t.reshape(SMALL_FULL[n]) for t in updated[n])
    for n, fullshape in SMALL_SHARDED.items():
        width = fullshape[-1] // N_DEV
        g = lax.dynamic_slice_in_dim(small_g[n], my_slot * width, width, axis=1)
        d, nm, nv = _adamw(local[n].reshape(g.shape), g, mom1[n].reshape(g.shape), mom2[n].reshape(g.shape), g.shape[0],
                           "adamw_" + n)
        grads[n], delta[n], new_m[n], new_v[n] = (t.reshape(local[n].shape) for t in (g, d, nm, nv))

    return (loss, grad_x, *[grads[n] for n in WEIGHT_ORDER], *[delta[n] for n in WEIGHT_ORDER],
            *[new_m[n] for n in WEIGHT_ORDER], *[new_v[n] for n in WEIGHT_ORDER])
```

```python
import functools
import math

import numpy as np
import jax
import jax.numpy as jnp
from jax import lax
from jax.experimental import pallas as pl
from jax.experimental.pallas import tpu as pltpu

F32 = jnp.float32
BF16 = jnp.bfloat16

D_MODEL = 1024
SEQ_LEN = 2048
HEAD_DIM = 64
ATTN_W = 384
CONV_W = 256
CONV_TAPS = 31
LRU_W = 384
LRU_TAPS = 4
LRU_HEADS = 6
LRU_C = 8.0
QKV_W = 3 * ATTN_W
CONV_IN_W = 2 * CONV_W
LRU_IN_W = 2 * LRU_W
IN_COLS = QKV_W + CONV_IN_W + LRU_IN_W
D_FF = 4096
DEPTH = 2
N_DEV = 8
RMS_EPS = 1e-6
LN_EPS = 1e-5
ATTN_BLOCK = 128
ATTN_DILATIONS = (1, 4, 16)
N_UNITS = 16
UNIT_UNROLL = 8
NEG_BIG = -1e30

ADAM_LR = 0.001
ADAM_B1 = 0.9
ADAM_B2 = 0.999
ADAM_EPS = 1e-08
ADAM_WD = 0.01
ADAM_STEP = 10

VMEM_LIMIT = 56 * 1024 * 1024
ROW_TILE = 512
GRAD_ROW_TILE = 1024
CONV_FWD_CHUNK = 128
CONV_BWD_CHUNK = 128
LRU_FWD_CHUNK = 128
LRU_BWD_CHUNK = 128


def _params(*sem):
    return pltpu.CompilerParams(dimension_semantics=sem if sem else None, vmem_limit_bytes=VMEM_LIMIT)


def _resident(shape):
    return pl.BlockSpec(shape, lambda *_: (0,) * len(shape), pipeline_mode=pl.Buffered(1))


def _dot(a, b):
    return jnp.dot(a, b, preferred_element_type=F32)


def _dot_nt(a, b):
    return lax.dot_general(a, b, (((1,), (1,)), ((), ())), preferred_element_type=F32)


def _dot_tn(a, b):
    return lax.dot_general(a, b, (((0,), (0,)), ((), ())), preferred_element_type=F32)


def _rms_fwd(x, g):
    rstd = lax.rsqrt(jnp.mean(x * x, axis=-1, keepdims=True) + RMS_EPS)
    xhat = x * rstd
    return xhat * g, xhat, rstd


def _rms_bwd(dh, xhat, rstd, g):
    dxh = dh * g
    dx = rstd * (dxh - xhat * jnp.mean(dxh * xhat, axis=-1, keepdims=True))
    dg = jnp.sum(dh * xhat, axis=0, keepdims=True)
    return dx, dg


def _sigmoid(x):
    return 0.5 * jnp.tanh(0.5 * x) + 0.5


def _one_minus_exp(x, exp_x):
    small = -x * (1.0 + x * (0.5 + x * (1.0 / 6.0)))
    return jnp.where(x > -0.01, small, 1.0 - exp_x)


def _log1p(z):
    w = 1.0 + z
    return jnp.where(w == 1.0, z, z * jnp.log(w) / jnp.where(w == 1.0, 1.0, w - 1.0))


def _softplus(x):
    return jnp.maximum(x, 0.0) + _log1p(jnp.exp(-jnp.abs(x)))


GELU_K = math.sqrt(2.0 / math.pi)


def _gelu(x):
    t = jnp.tanh(GELU_K * (x + 0.044715 * x * x * x))
    return 0.5 * x * (1.0 + t), t


def _gelu_grad(x, t):
    return 0.5 * (1.0 + t) + 0.5 * x * (1.0 - t * t) * GELU_K * (1.0 + 3.0 * 0.044715 * x * x)


def _inproj(x2d, g, w, ex=None):
    T = x2d.shape[0]
    tm = ROW_TILE

    def body(x_ref, g_ref, w_ref, h_ref, qkv_ref, ci_ref, li_ref):
        h, _, _ = _rms_fwd(x_ref[...], g_ref[...])
        hb = h.astype(BF16)
        h_ref[...] = hb
        qkv_ref[...] = _dot(hb, w_ref[:, 0:QKV_W])
        ci_ref[...] = _dot(hb, w_ref[:, QKV_W:QKV_W + CONV_IN_W])
        li_ref[...] = _dot(hb, w_ref[:, QKV_W + CONV_IN_W:IN_COLS])

    return _pallas_hosting(
        body, ex, name="inproj", grid=(T // tm,),
        in_specs=[pl.BlockSpec((tm, D_MODEL), lambda i: (i, 0)),
                  pl.BlockSpec((1, D_MODEL), lambda i: (0, 0)),
                  _resident((D_MODEL, IN_COLS))],
        out_specs=[pl.BlockSpec((tm, D_MODEL), lambda i: (i, 0)),
                   pl.BlockSpec((tm, QKV_W), lambda i: (i, 0)),
                   pl.BlockSpec((tm, CONV_IN_W), lambda i: (i, 0)),
                   pl.BlockSpec((tm, LRU_IN_W), lambda i: (i, 0))],
        out_shape=[jax.ShapeDtypeStruct((T, D_MODEL), BF16), jax.ShapeDtypeStruct((T, QKV_W), F32),
                   jax.ShapeDtypeStruct((T, CONV_IN_W), F32), jax.ShapeDtypeStruct((T, LRU_IN_W), F32)],
        scratch_shapes=[], semantics=("parallel",), operands=(x2d, g, w))


def _attn_alpha(lse):
    l0, l1, l2 = lse[:, 0:128], lse[:, 128:256], lse[:, 256:384]
    m = jnp.maximum(jnp.maximum(l0, l1), l2)
    e0, e1, e2 = jnp.exp(l0 - m), jnp.exp(l1 - m), jnp.exp(l2 - m)
    inv = 1.0 / (e0 + e1 + e2)
    return e0 * inv, e1 * inv, e2 * inv


def _outproj(x2d, o, lse, yc, yl, w, ex=None):
    T = x2d.shape[0]
    tm = ROW_TILE

    def body(x_ref, o_ref, lse_ref, yc_ref, yl_ref, w_ref, x1_ref, mix_ref):
        al = _attn_alpha(lse_ref[...])
        for p in range(3):
            mix_ref[:, p * 128:(p + 1) * 128] = (o_ref[:, p * 128:(p + 1) * 128] * al[p]).astype(BF16)
        mix_ref[:, ATTN_W:ATTN_W + CONV_W] = yc_ref[...]
        mix_ref[:, ATTN_W + CONV_W:D_MODEL] = yl_ref[...]
        x1_ref[...] = x_ref[...] + _dot(mix_ref[...], w_ref[...])

    return _pallas_hosting(
        body, ex, name="outproj", grid=(T // tm,),
        in_specs=[pl.BlockSpec((tm, D_MODEL), lambda i: (i, 0)),
                  pl.BlockSpec((tm, ATTN_W), lambda i: (i, 0)),
                  pl.BlockSpec((tm, ATTN_W), lambda i: (i, 0)),
                  pl.BlockSpec((tm, CONV_W), lambda i: (i, 0)),
                  pl.BlockSpec((tm, LRU_W), lambda i: (i, 0)),
                  _resident((D_MODEL, D_MODEL))],
        out_specs=[pl.BlockSpec((tm, D_MODEL), lambda i: (i, 0)),
                   pl.BlockSpec((tm, D_MODEL), lambda i: (i, 0))],
        out_shape=[jax.ShapeDtypeStruct((T, D_MODEL), F32), jax.ShapeDtypeStruct((T, D_MODEL), BF16)],
        scratch_shapes=[], semantics=("parallel",), operands=(x2d, o, lse, yc, yl, w))


FF_CHUNK = 1024


def _up(x1, g, w, ex=None):
    T = x1.shape[0]
    tm = ROW_TILE

    def body(x_ref, g_ref, w_ref, h_ref, r_ref):
        h, _, _ = _rms_fwd(x_ref[...], g_ref[...])
        hb = h.astype(BF16)
        h_ref[...] = hb
        for c in range(0, D_FF, FF_CHUNK):
            r_ref[:, c:c + FF_CHUNK] = jnp.maximum(_dot(hb, w_ref[:, c:c + FF_CHUNK]), 0.0).astype(BF16)

    return _pallas_hosting(
        body, ex, name="up", grid=(T // tm,),
        in_specs=[pl.BlockSpec((tm, D_MODEL), lambda i: (i, 0)),
                  pl.BlockSpec((1, D_MODEL), lambda i: (0, 0)),
                  _resident((D_MODEL, D_FF))],
        out_specs=[pl.BlockSpec((tm, D_MODEL), lambda i: (i, 0)),
                   pl.BlockSpec((tm, D_FF), lambda i: (i, 0))],
        out_shape=[jax.ShapeDtypeStruct((T, D_MODEL), BF16), jax.ShapeDtypeStruct((T, D_FF), BF16)],
        scratch_shapes=[], semantics=("parallel",), operands=(x1, g, w))


def _square_bf16(r):
    rf = r.astype(F32)
    return (rf * rf).astype(BF16)


def _down(x1, r, w, ex=None):
    T = x1.shape[0]
    tm = ROW_TILE

    def body(x_ref, r_ref, w_ref, o_ref):
        acc = x_ref[...]
        for c in range(0, D_FF, FF_CHUNK):
            acc = acc + _dot(_square_bf16(r_ref[:, c:c + FF_CHUNK]), w_ref[c:c + FF_CHUNK, :])
        o_ref[...] = acc

    return _pallas_hosting(
        body, ex, name="down", grid=(T // tm,),
        in_specs=[pl.BlockSpec((tm, D_MODEL), lambda i: (i, 0)),
                  pl.BlockSpec((tm, D_FF), lambda i: (i, 0)),
                  _resident((D_FF, D_MODEL))],
        out_specs=[pl.BlockSpec((tm, D_MODEL), lambda i: (i, 0))],
        out_shape=[jax.ShapeDtypeStruct((T, D_MODEL), F32)],
        scratch_shapes=[], semantics=("parallel",), operands=(x1, r, w))


def _loss_head(x2, target, g):
    T = x2.shape[0]
    tm = ROW_TILE

    def body(x_ref, t_ref, g_ref, dx_ref, loss_ref, dg_ref):
        @pl.when(pl.program_id(0) == 0)
        def _():
            loss_ref[...] = jnp.zeros_like(loss_ref)
            dg_ref[...] = jnp.zeros_like(dg_ref)

        gv = g_ref[...]
        y, xhat, rstd = _rms_fwd(x_ref[...], gv)
        err = y - t_ref[...]
        loss_ref[...] += 0.5 * jnp.sum(jnp.mean(err * err, axis=-1, keepdims=True))
        dy = err * (1.0 / D_MODEL)
        dx, dg = _rms_bwd(dy, xhat, rstd, gv)
        dx_ref[...] = dx
        dg_ref[...] += dg

    return pl.pallas_call(
        body, name="loss_head", grid=(T // tm,),
        in_specs=[pl.BlockSpec((tm, D_MODEL), lambda i: (i, 0)),
                  pl.BlockSpec((tm, D_MODEL), lambda i: (i, 0)),
                  pl.BlockSpec((1, D_MODEL), lambda i: (0, 0))],
        out_specs=[pl.BlockSpec((tm, D_MODEL), lambda i: (i, 0)),
                   pl.BlockSpec((8, 128), lambda i: (0, 0)),
                   pl.BlockSpec((1, D_MODEL), lambda i: (0, 0))],
        out_shape=[jax.ShapeDtypeStruct((T, D_MODEL), F32), jax.ShapeDtypeStruct((8, 128), F32),
                   jax.ShapeDtypeStruct((1, D_MODEL), F32)],
        compiler_params=_params("arbitrary"),
    )(x2, target, g)


def _down_bwd_act(dx2, r, w, ex=None):
    T = dx2.shape[0]
    tm = ROW_TILE

    def body(dx_ref, r_ref, w_ref, o_ref):
        dxb = dx_ref[...].astype(BF16)
        for c in range(0, D_FF, FF_CHUNK):
            dff = _dot_nt(dxb, w_ref[c:c + FF_CHUNK, :])
            o_ref[:, c:c + FF_CHUNK] = (dff * (2.0 * r_ref[:, c:c + FF_CHUNK].astype(F32))).astype(BF16)

    return _pallas_hosting(
        body, ex, name="down_bwd_act", grid=(T // tm,),
        in_specs=[pl.BlockSpec((tm, D_MODEL), lambda i: (i, 0)),
                  pl.BlockSpec((tm, D_FF), lambda i: (i, 0)),
                  _resident((D_FF, D_MODEL))],
        out_specs=[pl.BlockSpec((tm, D_FF), lambda i: (i, 0))],
        out_shape=[jax.ShapeDtypeStruct((T, D_FF), BF16)],
        scratch_shapes=[], semantics=("parallel",), operands=(dx2, r, w))


def _down_bwd_w(r, dx2):
    T = dx2.shape[0]
    tk = GRAD_ROW_TILE
    nk = T // tk

    def body(r_ref, dx_ref, o_ref, acc_ref):
        k = pl.program_id(0)
        dxb = dx_ref[...].astype(BF16)

        @pl.when(k == 0)
        def _():
            acc_ref[...] = jnp.zeros_like(acc_ref)

        for c in range(0, D_FF, FF_CHUNK):
            acc_ref[:, c:c + FF_CHUNK] += _dot_tn(dxb, _square_bf16(r_ref[:, c:c + FF_CHUNK]))

        @pl.when(k == nk - 1)
        def _():
            o_ref[...] = acc_ref[...].astype(BF16)

    return pl.pallas_call(
        body, name="down_bwd_w", grid=(nk,),
        in_specs=[pl.BlockSpec((tk, D_FF), lambda k: (k, 0)),
                  pl.BlockSpec((tk, D_MODEL), lambda k: (k, 0))],
        out_specs=_resident((D_MODEL, D_FF)),
        out_shape=jax.ShapeDtypeStruct((D_MODEL, D_FF), BF16),
        scratch_shapes=[pltpu.VMEM((D_MODEL, D_FF), F32)],
        compiler_params=_params("arbitrary"),
    )(r, dx2)


def _up_bwd_act(dpre, w, x1, g, dx2, ex=None):
    T = dx2.shape[0]
    tm = ROW_TILE

    def body(dp_ref, w_ref, x_ref, g_ref, dx2_ref, dx1_ref, dg_ref):
        dh = _dot_nt(dp_ref[:, 0:FF_CHUNK], w_ref[:, 0:FF_CHUNK])
        for c in range(FF_CHUNK, D_FF, FF_CHUNK):
            dh = dh + _dot_nt(dp_ref[:, c:c + FF_CHUNK], w_ref[:, c:c + FF_CHUNK])
        gv = g_ref[...]
        _, xhat, rstd = _rms_fwd(x_ref[...], gv)
        dx, dg = _rms_bwd(dh, xhat, rstd, gv)
        dx1_ref[...] = dx2_ref[...] + dx

        @pl.when(pl.program_id(0) == 0)
        def _():
            dg_ref[...] = dg

        @pl.when(pl.program_id(0) != 0)
        def _():
            dg_ref[...] += dg

    return _pallas_hosting(
        body, ex, name="up_bwd_act", grid=(T // tm,),
        in_specs=[pl.BlockSpec((tm, D_FF), lambda i: (i, 0)),
                  _resident((D_MODEL, D_FF)),
                  pl.BlockSpec((tm, D_MODEL), lambda i: (i, 0)),
                  pl.BlockSpec((1, D_MODEL), lambda i: (0, 0)),
                  pl.BlockSpec((tm, D_MODEL), lambda i: (i, 0))],
        out_specs=[pl.BlockSpec((tm, D_MODEL), lambda i: (i, 0)),
                   pl.BlockSpec((1, D_MODEL), lambda i: (0, 0))],
        out_shape=[jax.ShapeDtypeStruct((T, D_MODEL), F32), jax.ShapeDtypeStruct((1, D_MODEL), F32)],
        scratch_shapes=[], semantics=("arbitrary",), operands=(dpre, w, x1, g, dx2))


def _up_bwd_w(h2, dpre):
    T = h2.shape[0]
    tk = GRAD_ROW_TILE
    nk = T // tk

    def body(h_ref, dp_ref, o_ref, acc_ref):
        k = pl.program_id(0)
        hb = h_ref[...]

        @pl.when(k == 0)
        def _():
            acc_ref[...] = jnp.zeros_like(acc_ref)

        for c in range(0, D_FF, FF_CHUNK):
            acc_ref[:, c:c + FF_CHUNK] += _dot_tn(hb, dp_ref[:, c:c + FF_CHUNK])

        @pl.when(k == nk - 1)
        def _():
            o_ref[...] = acc_ref[...].astype(BF16)

    return pl.pallas_call(
        body, name="up_bwd_w", grid=(nk,),
        in_specs=[pl.BlockSpec((tk, D_MODEL), lambda k: (k, 0)),
                  pl.BlockSpec((tk, D_FF), lambda k: (k, 0))],
        out_specs=_resident((D_MODEL, D_FF)),
        out_shape=jax.ShapeDtypeStruct((D_MODEL, D_FF), BF16),
        scratch_shapes=[pltpu.VMEM((D_MODEL, D_FF), F32)],
        compiler_params=_params("arbitrary"),
    )(h2, dpre)


def _outproj_bwd(dx1, mix, w, o, lse):
    T = dx1.shape[0]
    tm = ROW_TILE
    nk = T // tm

    def body(dx_ref, mix_ref, w_ref, o_ref, lse_ref, do_ref, dd_ref, dc_ref, dl_ref, dw_ref, acc_ref):
        i = pl.program_id(0)
        dxb = dx_ref[...].astype(BF16)
        dmix = _dot_nt(dxb, w_ref[...])
        al = _attn_alpha(lse_ref[...])
        first = lax.broadcasted_iota(jnp.int32, (tm, 128), 1) < HEAD_DIM
        tot = jnp.zeros((tm, 128), F32)
        for p in range(3):
            sl = slice(p * 128, (p + 1) * 128)
            dy = dmix[:, sl]
            do_ref[:, sl] = dy * al[p]
            prod = dy * o_ref[:, sl]
            s0 = jnp.sum(jnp.where(first, prod, 0.0), axis=-1, keepdims=True)
            s1 = jnp.sum(jnp.where(first, 0.0, prod), axis=-1, keepdims=True)
            tot = tot + al[p] * jnp.where(first, s0, s1)
        for p in range(3):
            dd_ref[:, p * 128:(p + 1) * 128] = -al[p] * tot
        dc_ref[...] = dmix[:, ATTN_W:ATTN_W + CONV_W]
        dl_ref[...] = dmix[:, ATTN_W + CONV_W:D_MODEL]
        part = _dot_tn(mix_ref[...], dxb)

        @pl.when(i == 0)
        def _():
            acc_ref[...] = part

        @pl.when(i != 0)
        def _():
            acc_ref[...] += part

        @pl.when(i == nk - 1)
        def _():
            dw_ref[...] = acc_ref[...].astype(BF16)

    return pl.pallas_call(
        body, name="outproj_bwd", grid=(nk,),
        in_specs=[pl.BlockSpec((tm, D_MODEL), lambda i: (i, 0)),
                  pl.BlockSpec((tm, D_MODEL), lambda i: (i, 0)),
                  _resident((D_MODEL, D_MODEL)),
                  pl.BlockSpec((tm, ATTN_W), lambda i: (i, 0)),
                  pl.BlockSpec((tm, ATTN_W), lambda i: (i, 0))],
        out_specs=[pl.BlockSpec((tm, ATTN_W), lambda i: (i, 0)),
                   pl.BlockSpec((tm, ATTN_W), lambda i: (i, 0)),
                   pl.BlockSpec((tm, CONV_W), lambda i: (i, 0)),
                   pl.BlockSpec((tm, LRU_W), lambda i: (i, 0)),
                   _resident((D_MODEL, D_MODEL))],
        out_shape=[jax.ShapeDtypeStruct((T, ATTN_W), F32), jax.ShapeDtypeStruct((T, ATTN_W), F32),
                   jax.ShapeDtypeStruct((T, CONV_W), F32), jax.ShapeDtypeStruct((T, LRU_W), F32),
                   jax.ShapeDtypeStruct((D_MODEL, D_MODEL), BF16)],
        scratch_shapes=[pltpu.VMEM((D_MODEL, D_MODEL), F32)],
        compiler_params=_params("arbitrary"),
    )(dx1, mix, w, o, lse)


DZ_COLS = ((0, ATTN_W), (ATTN_W, 2 * ATTN_W), (2 * ATTN_W, QKV_W), (QKV_W, QKV_W + CONV_IN_W), (QKV_W + CONV_IN_W, IN_COLS))


def _inproj_bwd_w(dz_parts, h):
    T = h.shape[0]
    tm = GRAD_ROW_TILE
    nk = T // tm
    n_parts = len(DZ_COLS)

    def body(*refs):
        dz_refs = refs[:n_parts]
        h_ref, dw_ref, acc_ref = refs[n_parts:]
        i = pl.program_id(0)
        hb = h_ref[...]

        @pl.when(i == 0)
        def _():
            acc_ref[...] = jnp.zeros_like(acc_ref)

        for r, (lo, hi) in zip(dz_refs, DZ_COLS):
            acc_ref[:, lo:hi] += _dot_tn(hb, r[...].astype(BF16))

        @pl.when(i == nk - 1)
        def _():
            dw_ref[...] = acc_ref[...].astype(BF16)

    rows = lambda width: pl.BlockSpec((tm, width), lambda i: (i, 0))
    return pl.pallas_call(
        body, name="inproj_bwd_w", grid=(nk,),
        in_specs=[rows(hi - lo) for lo, hi in DZ_COLS] + [rows(D_MODEL)],
        out_specs=_resident((D_MODEL, IN_COLS)),
        out_shape=jax.ShapeDtypeStruct((D_MODEL, IN_COLS), BF16),
        scratch_shapes=[pltpu.VMEM((D_MODEL, IN_COLS), F32)],
        compiler_params=_params("arbitrary"),
    )(*dz_parts, h)


def _inproj_bwd_act(dz_parts, w, x2d, g, dx1, ex=None):
    T = x2d.shape[0]
    tm = ROW_TILE
    n_parts = len(DZ_COLS)

    def body(*refs):
        dz_refs = refs[:n_parts]
        w_ref, x_ref, g_ref, dx1_ref, dx_ref, dg_ref = refs[n_parts:]
        dh = _dot_nt(dz_refs[0][...].astype(BF16), w_ref[:, DZ_COLS[0][0]:DZ_COLS[0][1]])
        for r, (lo, hi) in zip(dz_refs[1:], DZ_COLS[1:]):
            dh = dh + _dot_nt(r[...].astype(BF16), w_ref[:, lo:hi])
        gv = g_ref[...]
        _, xhat, rstd = _rms_fwd(x_ref[...], gv)
        dx, dg = _rms_bwd(dh, xhat, rstd, gv)
        dx_ref[...] = dx1_ref[...] + dx

        @pl.when(pl.program_id(0) == 0)
        def _():
            dg_ref[...] = dg

        @pl.when(pl.program_id(0) != 0)
        def _():
            dg_ref[...] += dg

    rows = lambda width: pl.BlockSpec((tm, width), lambda i: (i, 0))
    return _pallas_hosting(
        body, ex, name="inproj_bwd_act", grid=(T // tm,),
        in_specs=[rows(hi - lo) for lo, hi in DZ_COLS] + [
            _resident((D_MODEL, IN_COLS)), rows(D_MODEL), pl.BlockSpec((1, D_MODEL), lambda i: (0, 0)), rows(D_MODEL)],
        out_specs=[rows(D_MODEL), pl.BlockSpec((1, D_MODEL), lambda i: (0, 0))],
        out_shape=[jax.ShapeDtypeStruct((T, D_MODEL), F32), jax.ShapeDtypeStruct((1, D_MODEL), F32)],
        scratch_shapes=[], semantics=("arbitrary",), operands=(*dz_parts, w, x2d, g, dx1))


def _alibi_coef():
    slopes = 2.0 ** (-8.0 * np.arange(1, 7) / 6)
    return jnp.asarray((slopes.reshape(3, 2) * np.asarray(ATTN_DILATIONS)[:, None]).astype(np.float32))


def _unit_rows(u, d):
    nb = N_UNITS // d
    r, n = u // nb, u % nb
    span = ATTN_BLOCK * d

    def rows(block):
        start = block * span + r
        return pl.ds(pl.multiple_of(start, ATTN_BLOCK), ATTN_BLOCK) if d == 1 else pl.ds(start, ATTN_BLOCK, stride=d)

    return rows(n), rows(jnp.maximum(n - 1, 0)), rows(jnp.minimum(n + 1, nb - 1)), n > 0, n + 1 < nb


def _per_pattern(fn):
    for p, d in enumerate(ATTN_DILATIONS):
        pl.when(pl.program_id(1) == p)(functools.partial(fn, p, d))


def _attn_col(offset):
    return pl.BlockSpec((None, SEQ_LEN, 128), lambda b, p: (b, 0, p + offset))


def _attn_masks():
    qi = lax.broadcasted_iota(jnp.int32, (ATTN_BLOCK, 2 * ATTN_BLOCK), 0)
    kj = lax.broadcasted_iota(jnp.int32, (ATTN_BLOCK, 2 * ATTN_BLOCK), 1)
    dist = qi + ATTN_BLOCK - kj
    first = lax.broadcasted_iota(jnp.int32, (ATTN_BLOCK, 128), 1) < HEAD_DIM
    return dist.astype(F32), (dist >= 0) & (dist <= ATTN_BLOCK), kj >= ATTN_BLOCK, first


def _head_lanes(a, first, j):
    return jnp.where(first if j == 0 else jnp.logical_not(first), a, jnp.zeros_like(a))


def _load_kv(ref, prev, own):
    return jnp.concatenate([ref[prev, :], ref[own, :]], axis=0).astype(BF16)


def _attn_fwd(qkv, ex=None):
    B = qkv.shape[0]

    def body(coef_ref, q_ref, k_ref, v_ref, o_ref, lse_ref):
        dist, band, own_half, first = _attn_masks()

        def pattern(p, d):
            def unit(u, carry):
                own, prev, _, has_prev, _ = _unit_rows(u, d)
                ok = band & jnp.logical_or(own_half, has_prev)
                q = q_ref[own, :].astype(BF16)
                kcat, vcat = _load_kv(k_ref, prev, own), _load_kv(v_ref, prev, own)
                outs, lses = [], []
                for j in range(2):
                    s = jnp.where(ok, _dot_nt(_head_lanes(q, first, j), kcat) * 0.125 - coef_ref[p, j] * dist, NEG_BIG)
                    m = jnp.max(s, axis=-1, keepdims=True)
                    e = jnp.exp(s - m)
                    l = jnp.sum(e, axis=-1, keepdims=True)
                    outs.append(_dot(e.astype(BF16), vcat) * (1.0 / l))
                    lses.append(m + jnp.log(l))
                o_ref[own, :] = jnp.where(first, outs[0], outs[1])
                lse_ref[own, :] = jnp.where(first, lses[0], lses[1])
                return carry

            lax.fori_loop(0, N_UNITS, unit, 0, unroll=UNIT_UNROLL)

        _per_pattern(pattern)

    shape = jax.ShapeDtypeStruct((B, SEQ_LEN, ATTN_W), F32)
    return _pallas_hosting(
        body, ex, name="attn_fwd", grid=(B, 3),
        in_specs=[pl.BlockSpec(memory_space=pltpu.SMEM), _attn_col(0), _attn_col(3), _attn_col(6)],
        out_specs=[_attn_col(0), _attn_col(0)],
        out_shape=[shape, shape],
        scratch_shapes=[], semantics=("parallel", "parallel"), operands=(_alibi_coef(), qkv, qkv, qkv))


def _attn_bwd(qkv, do, lse, dd, ex=None):
    B = qkv.shape[0]

    def body(coef_ref, q_ref, k_ref, v_ref, do_ref, lse_ref, dd_ref, dq_ref, dk_ref, dv_ref):
        dist, band, own_half, first = _attn_masks()

        def pattern(p, d):
            def unit(u, carry):
                own, prev, _, has_prev, _ = _unit_rows(u, d)
                ok = band & jnp.logical_or(own_half, has_prev)
                q, do = q_ref[own, :].astype(BF16), do_ref[own, :].astype(BF16)
                kcat, vcat = _load_kv(k_ref, prev, own), _load_kv(v_ref, prev, own)
                lse_a, dd_a = lse_ref[own, :], dd_ref[own, :]
                dqs, dks, dvs = [], [], []
                for j in range(2):
                    col = slice(HEAD_DIM * j, HEAD_DIM * j + 1)
                    s = _dot_nt(_head_lanes(q, first, j), kcat) * 0.125 - coef_ref[p, j] * dist
                    pr = jnp.where(ok, jnp.exp(jnp.where(ok, s, NEG_BIG) - lse_a[:, col]), 0.0)
                    ds = (pr * (_dot_nt(_head_lanes(do, first, j), vcat) + dd_a[:, col])).astype(BF16)
                    dqs.append(_dot(ds, kcat))
                    dks.append(_dot_tn(ds, q))
                    dvs.append(_dot_tn(pr.astype(BF16), do))
                both = lambda pair: jnp.where(jnp.concatenate([first] * (pair[0].shape[0] // ATTN_BLOCK), axis=0), *pair)
                dq_ref[own, :] = both(dqs) * 0.125
                dk, dv = both(dks) * 0.125, both(dvs)
                dk_ref[own, :] = dk[ATTN_BLOCK:]
                dv_ref[own, :] = dv[ATTN_BLOCK:]
                dk_ref[prev, :] += dk[:ATTN_BLOCK]
                dv_ref[prev, :] += dv[:ATTN_BLOCK]
                return carry

            lax.fori_loop(0, N_UNITS, unit, 0, unroll=UNIT_UNROLL)

        _per_pattern(pattern)

    shape = jax.ShapeDtypeStruct((B, SEQ_LEN, ATTN_W), F32)
    return _pallas_hosting(
        body, ex, name="attn_bwd", grid=(B, 3),
        in_specs=[pl.BlockSpec(memory_space=pltpu.SMEM), _attn_col(0), _attn_col(3), _attn_col(6), _attn_col(0), _attn_col(0),
                  _attn_col(0)],
        out_specs=[_attn_col(0)] * 3,
        out_shape=[shape] * 3,
        scratch_shapes=[], semantics=("parallel", "parallel"), operands=(_alibi_coef(), qkv, qkv, qkv, do, lse, dd))


def _for_chunks(n_rows, fn, chunk):
    def step(c, carry):
        fn(pl.multiple_of(c * chunk, chunk))
        return carry

    lax.fori_loop(0, n_rows // chunk, step, 0)


def _shift_down(win, s, rows):
    lead = win.shape[0] - rows
    if s == 0:
        return win[lead:]
    if s % 8 == 0:
        return win[lead - s:lead - s + rows]
    q, r = divmod(s, 8)
    rolled = pltpu.roll(win, r, 0)
    return rolled[lead - 8 * q:lead - 8 * q + rows]


def _tap_shifts(n_taps):
    return [(r, [(n_taps - 1 - (8 * q + r), 8 * q) for q in range((n_taps - 1 - r) // 8 + 1)]) for r in range(min(8, n_taps))]


def _rotated_down(win, r):
    return win if r == 0 else pltpu.roll(win, r, 0)


def _rotated_up(win, r):
    return win if r == 0 else pltpu.roll(win, win.shape[0] - r, 0)


def _shift_up(win, s, rows):
    if s % 8 == 0:
        return win[s:s + rows]
    q, r = divmod(s, 8)
    rolled = pltpu.roll(win, win.shape[0] - r, 0)
    return rolled[8 * q:8 * q + rows]


CONV_PAD = 32


def _ln_silu(c, lg, lb):
    mu = jnp.mean(c, axis=-1, keepdims=True)
    cc = c - mu
    rstd = lax.rsqrt(jnp.mean(cc * cc, axis=-1, keepdims=True) + LN_EPS)
    nrm = cc * rstd
    v = nrm * lg + lb
    sg = _sigmoid(v)
    return v * sg, nrm, rstd, v, sg


def _conv_fwd(ci, w, b, lg, lb):
    B, S, _ = ci.shape
    CH = CONV_FWD_CHUNK

    def body(ci_ref, w_ref, b_ref, lg_ref, lb_ref, y_ref, c_ref, pad_ref):
        pad_ref[0:CONV_PAD, :] = jnp.zeros((CONV_PAD, CONV_W), F32)

        def glu(base):
            blk = ci_ref[pl.ds(base, CH), :]
            pad_ref[pl.ds(CONV_PAD + base, CH), :] = blk[:, 0:CONV_W] * _sigmoid(blk[:, CONV_W:])

        _for_chunks(S, glu, CH)

        def conv(base):
            win = pad_ref[pl.ds(base, CH + CONV_PAD), :]
            acc = jnp.broadcast_to(b_ref[...], (CH, CONV_W))
            for r, taps in _tap_shifts(CONV_TAPS):
                rot = _rotated_down(win, r)
                for k, off in taps:
                    acc = acc + w_ref[k:k + 1, :] * rot[CONV_PAD - off:CONV_PAD - off + CH]
            c_ref[pl.ds(base, CH), :] = acc
            y, _, _, _, _ = _ln_silu(acc, lg_ref[...], lb_ref[...])
            y_ref[pl.ds(base, CH), :] = y.astype(BF16)

        _for_chunks(S, conv, CH)

    vec = pl.BlockSpec((1, CONV_W), lambda i: (0, 0))
    return pl.pallas_call(
        body, name="conv_fwd", grid=(B,),
        in_specs=[pl.BlockSpec((None, S, CONV_IN_W), lambda i: (i, 0, 0)),
                  pl.BlockSpec((CONV_TAPS, CONV_W), lambda i: (0, 0)), vec, vec, vec],
        out_specs=[pl.BlockSpec((None, S, CONV_W), lambda i: (i, 0, 0)),
                   pl.BlockSpec((None, S, CONV_W), lambda i: (i, 0, 0))],
        out_shape=[jax.ShapeDtypeStruct((B, S, CONV_W), BF16), jax.ShapeDtypeStruct((B, S, CONV_W), F32)],
        scratch_shapes=[pltpu.VMEM((S + CONV_PAD, CONV_W), F32)],
        compiler_params=_params("parallel"),
    )(ci, w, b, lg, lb)


def _conv_bwd(ci, cpre, dy, w, lg, lb, ex=None):
    B, S, _ = ci.shape
    CH = CONV_BWD_CHUNK

    def body(ci_ref, c_ref, dy_ref, w_ref, lg_ref, lb_ref, dci_ref, dw_ref, db_ref, dlg_ref, dlb_ref, upad_ref, dcpad_ref,
             dwacc_ref):
        @pl.when(pl.program_id(0) == 0)
        def _():
            dw_ref[...] = jnp.zeros_like(dw_ref)
            db_ref[...] = jnp.zeros_like(db_ref)
            dlg_ref[...] = jnp.zeros_like(dlg_ref)
            dlb_ref[...] = jnp.zeros_like(dlb_ref)

        upad_ref[0:CONV_PAD, :] = jnp.zeros((CONV_PAD, CONV_W), F32)
        dcpad_ref[S:S + CONV_PAD, :] = jnp.zeros((CONV_PAD, CONV_W), F32)
        dwacc_ref[...] = jnp.zeros_like(dwacc_ref)

        def norm_bwd(base):
            blk = ci_ref[pl.ds(base, CH), :]
            upad_ref[pl.ds(CONV_PAD + base, CH), :] = blk[:, 0:CONV_W] * _sigmoid(blk[:, CONV_W:])
            lgv = lg_ref[...]
            _, nrm, rstd, v, sg = _ln_silu(c_ref[pl.ds(base, CH), :], lgv, lb_ref[...])
            dv = dy_ref[pl.ds(base, CH), :] * (sg * (1.0 + v * (1.0 - sg)))
            dlg_ref[...] += jnp.sum(dv * nrm, axis=0, keepdims=True)
            dlb_ref[...] += jnp.sum(dv, axis=0, keepdims=True)
            dn = dv * lgv
            dc = rstd * (dn - jnp.mean(dn, axis=-1, keepdims=True) - nrm * jnp.mean(dn * nrm, axis=-1, keepdims=True))
            dcpad_ref[pl.ds(base, CH), :] = dc
            db_ref[...] += jnp.sum(dc, axis=0, keepdims=True)

        _for_chunks(S, norm_bwd, CH)

        def conv_bwd(base):
            dwin = dcpad_ref[pl.ds(base, CH + CONV_PAD), :]
            uwin = upad_ref[pl.ds(base, CH + CONV_PAD), :]
            dc = dwin[0:CH]
            du = jnp.zeros((CH, CONV_W), F32)
            for r, taps in _tap_shifts(CONV_TAPS):
                d_rot, u_rot = _rotated_up(dwin, r), _rotated_down(uwin, r)
                for k, off in taps:
                    du = du + w_ref[k:k + 1, :] * d_rot[off:off + CH]
                    prod = dc * u_rot[CONV_PAD - off:CONV_PAD - off + CH]
                    dwacc_ref[8 * k:8 * k + 8, :] += jnp.sum(prod.reshape(CH // 8, 8, CONV_W), axis=0)
            blk = ci_ref[pl.ds(base, CH), :]
            a, sg = blk[:, 0:CONV_W], _sigmoid(blk[:, CONV_W:])
            dci_ref[pl.ds(base, CH), 0:CONV_W] = (du * sg).astype(BF16)
            dci_ref[pl.ds(base, CH), CONV_W:] = (du * a * sg * (1.0 - sg)).astype(BF16)

        _for_chunks(S, conv_bwd, CH)
        for k in range(CONV_TAPS):
            dw_ref[k:k + 1, :] += jnp.sum(dwacc_ref[8 * k:8 * k + 8, :], axis=0, keepdims=True)

    vec = pl.BlockSpec((1, CONV_W), lambda i: (0, 0))
    mat = pl.BlockSpec((CONV_TAPS, CONV_W), lambda i: (0, 0))
    seq = lambda width: pl.BlockSpec((None, S, width), lambda i: (i, 0, 0))
    return _pallas_hosting(
        body, ex, name="conv_bwd", grid=(B,),
        in_specs=[seq(CONV_IN_W), seq(CONV_W), seq(CONV_W), mat, vec, vec],
        out_specs=[seq(CONV_IN_W), mat, vec, vec, vec],
        out_shape=[jax.ShapeDtypeStruct((B, S, CONV_IN_W), BF16), jax.ShapeDtypeStruct((CONV_TAPS, CONV_W), F32),
                   jax.ShapeDtypeStruct((1, CONV_W), F32), jax.ShapeDtypeStruct((1, CONV_W), F32),
                   jax.ShapeDtypeStruct((1, CONV_W), F32)],
        scratch_shapes=[pltpu.VMEM((S + CONV_PAD, CONV_W), F32), pltpu.VMEM((S + CONV_PAD, CONV_W), F32),
                        pltpu.VMEM((8 * CONV_TAPS, CONV_W), F32)],
        semantics=("arbitrary",), operands=(ci, cpre, dy, w, lg, lb))


SCAN_SHIFTS = tuple(1 << e for e in range(11))


def _prev8(ref, base, cols, fill):
    start = pl.multiple_of(jnp.maximum(base - 8, 0), 8)
    return jnp.where(base > 0, ref[pl.ds(start, 8), cols], fill)


def _next8(ref, base, rows, total, cols, fill):
    start = pl.multiple_of(jnp.minimum(base + rows, total - 8), 8)
    return jnp.where(base + rows < total, ref[pl.ds(start, 8), cols], fill)


ALL = slice(None)
LRU_X = slice(LRU_W, LRU_IN_W)
LRU_GATE = slice(0, LRU_W)


def _lru_conv(li_ref, base, rows, cw_ref, cb_ref):
    win = jnp.concatenate([_prev8(li_ref, base, LRU_X, 0.0), li_ref[pl.ds(base, rows), LRU_X]], axis=0)
    u = jnp.broadcast_to(cb_ref[...], (rows, LRU_W))
    for k in range(LRU_TAPS):
        u = u + cw_ref[k:k + 1, :] * _shift_down(win, LRU_TAPS - 1 - k, rows)
    return u, win


def _lru_gates(u, wa_ref, ba_ref, wx_ref, bx_ref, sp):
    ub = u.astype(BF16)
    r = _sigmoid(_dot(ub, wa_ref[...]) + ba_ref[...])
    i = _sigmoid(_dot(ub, wx_ref[...]) + bx_ref[...])
    la = (-LRU_C) * r * sp
    a = jnp.exp(la)
    return ub, r, i, a, _one_minus_exp(2.0 * la, a * a)


def _scan_forward(bufs, S, CH):
    for n, s in enumerate(SCAN_SHIFTS):
        (sa, sb), (da, db) = bufs[n % 2], bufs[(n + 1) % 2]

        def step(base, s=s, sa=sa, sb=sb, da=da, db=db):
            a, b = sa[pl.ds(base, CH), :], sb[pl.ds(base, CH), :]
            if s < 8:
                a_s = _shift_down(jnp.concatenate([_prev8(sa, base, ALL, 1.0), a], axis=0), s, CH)
                b_s = _shift_down(jnp.concatenate([_prev8(sb, base, ALL, 0.0), b], axis=0), s, CH)
            elif s < CH:
                start = pl.multiple_of(jnp.maximum(base - s, 0), 8)
                a_s = jnp.concatenate([jnp.where(base > 0, sa[pl.ds(start, s), :], 1.0), a[0:CH - s]], axis=0)
                b_s = jnp.concatenate([jnp.where(base > 0, sb[pl.ds(start, s), :], 0.0), b[0:CH - s]], axis=0)
            else:
                start = pl.multiple_of(jnp.maximum(base - s, 0), 8)
                a_s = jnp.where(base < s, 1.0, sa[pl.ds(start, CH), :])
                b_s = jnp.where(base < s, 0.0, sb[pl.ds(start, CH), :])
            db[pl.ds(base, CH), :] = a * b_s + b
            da[pl.ds(base, CH), :] = a * a_s

        _for_chunks(S, step, CH)
    return len(SCAN_SHIFTS) % 2


def _scan_backward(bufs, S, CH):
    for n, s in enumerate(SCAN_SHIFTS):
        (sa, sb), (da, db) = bufs[n % 2], bufs[(n + 1) % 2]

        def step(base, s=s, sa=sa, sb=sb, da=da, db=db):
            a, b = sa[pl.ds(base, CH), :], sb[pl.ds(base, CH), :]
            if s < 8:
                a_s = _shift_up(jnp.concatenate([a, _next8(sa, base, CH, S, ALL, 1.0)], axis=0), s, CH)
                b_s = _shift_up(jnp.concatenate([b, _next8(sb, base, CH, S, ALL, 0.0)], axis=0), s, CH)
            elif s < CH:
                start = pl.multiple_of(jnp.minimum(base + CH, S - s), 8)
                more = base + CH < S
                a_s = jnp.concatenate([a[s:CH], jnp.where(more, sa[pl.ds(start, s), :], 1.0)], axis=0)
                b_s = jnp.concatenate([b[s:CH], jnp.where(more, sb[pl.ds(start, s), :], 0.0)], axis=0)
            else:
                start = pl.multiple_of(jnp.minimum(base + s, S - CH), 8)
                a_s = jnp.where(base + s >= S, 1.0, sa[pl.ds(start, CH), :])
                b_s = jnp.where(base + s >= S, 0.0, sb[pl.ds(start, CH), :])
            db[pl.ds(base, CH), :] = a * b_s + b
            da[pl.ds(base, CH), :] = a * a_s

        _for_chunks(S, step, CH)
    return len(SCAN_SHIFTS) % 2


def _lru_fwd(li, cw, cb, wa, ba, wx, bx, lam, ex=None):
    B, S, _ = li.shape
    CH = LRU_FWD_CHUNK

    def body(li_ref, cw_ref, cb_ref, wa_ref, ba_ref, wx_ref, bx_ref, lam_ref, y_ref, h_ref, a0, b0, a1, b1):
        sp = _softplus(-lam_ref[...])

        def gates(base):
            u, _ = _lru_conv(li_ref, base, CH, cw_ref, cb_ref)
            _, _, i, a, em = _lru_gates(u, wa_ref, ba_ref, wx_ref, bx_ref, sp)
            a0[pl.ds(base, CH), :] = a
            b0[pl.ds(base, CH), :] = jnp.sqrt(em) * (i * u)

        _for_chunks(S, gates, CH)
        bufs = ((a0, b0), (a1, b1))
        hb = bufs[_scan_forward(bufs, S, CH)][1]

        def out(base):
            h = hb[pl.ds(base, CH), :]
            h_ref[pl.ds(base, CH), :] = h
            gl, _ = _gelu(li_ref[pl.ds(base, CH), LRU_GATE])
            y_ref[pl.ds(base, CH), :] = (gl * h).astype(BF16)

        _for_chunks(S, out, CH)

    vec = pl.BlockSpec((1, LRU_W), lambda i: (0, 0))
    mat = pl.BlockSpec((LRU_W, LRU_W), lambda i: (0, 0))
    seq = lambda width: pl.BlockSpec((None, S, width), lambda i: (i, 0, 0))
    return _pallas_hosting(
        body, ex, name="lru_fwd", grid=(B,),
        in_specs=[seq(LRU_IN_W), pl.BlockSpec((LRU_TAPS, LRU_W), lambda i: (0, 0)), vec, mat, vec, mat, vec, vec],
        out_specs=[seq(LRU_W), seq(LRU_W)],
        out_shape=[jax.ShapeDtypeStruct((B, S, LRU_W), BF16), jax.ShapeDtypeStruct((B, S, LRU_W), F32)],
        scratch_shapes=[pltpu.VMEM((S, LRU_W), F32)] * 4,
        semantics=("parallel",), operands=(li, cw, cb, wa, ba, wx, bx, lam))


def _lru_bwd(li, hs, dy, cw, cb, wa, ba, wx, bx, lam, ex=None):
    B, S, _ = li.shape
    CH = LRU_BWD_CHUNK

    def body(li_ref, hs_ref, dy_ref, cw_ref, cb_ref, wa_ref, ba_ref, wx_ref, bx_ref, lam_ref,
             dli_ref, dcw_ref, dcb_ref, dwa_ref, dba_ref, dwx_ref, dbx_ref, dlam_ref, a0, b0, a1, b1, u_s, du_s):
        @pl.when(pl.program_id(0) == 0)
        def _():
            for ref in (dcw_ref, dcb_ref, dwa_ref, dba_ref, dwx_ref, dbx_ref, dlam_ref):
                ref[...] = jnp.zeros_like(ref)

        lam_v = lam_ref[...]
        sp = _softplus(-lam_v)
        dsp_dlam = -_sigmoid(-lam_v)

        def gates(base):
            u, _ = _lru_conv(li_ref, base, CH, cw_ref, cb_ref)
            _, _, _, a, _ = _lru_gates(u, wa_ref, ba_ref, wx_ref, bx_ref, sp)
            gl, _ = _gelu(li_ref[pl.ds(base, CH), LRU_GATE])
            u_s[pl.ds(base, CH), :] = u
            a0[pl.ds(base, CH), :] = a
            b0[pl.ds(base, CH), :] = a * (dy_ref[pl.ds(base, CH), :] * gl)

        _for_chunks(S, gates, CH)
        bufs = ((a0, b0), (a1, b1))
        eb = bufs[_scan_backward(bufs, S, CH)][1]

        def grads(base):
            e = eb[pl.ds(base, CH), :]
            e_next = _shift_up(jnp.concatenate([e, _next8(eb, base, CH, S, ALL, 0.0)], axis=0), 1, CH)
            gate = li_ref[pl.ds(base, CH), LRU_GATE]
            gl, th = _gelu(gate)
            dy = dy_ref[pl.ds(base, CH), :]
            g = dy * gl + e_next
            h = hs_ref[pl.ds(base, CH), :]
            h_prev = _shift_down(jnp.concatenate([_prev8(hs_ref, base, ALL, 0.0), h], axis=0), 1, CH)
            dli_ref[pl.ds(base, CH), LRU_GATE] = (dy * h * _gelu_grad(gate, th)).astype(BF16)
            u = u_s[pl.ds(base, CH), :]
            ub, r, i, a, em = _lru_gates(u, wa_ref, ba_ref, wx_ref, bx_ref, sp)
            mult = jnp.sqrt(em)
            da = g * h_prev
            dmult = g * (i * u)
            di = g * mult * u
            dla = da * a - dmult * (a * a) * lax.rsqrt(jnp.maximum(em, 1e-30))
            dlam_ref[...] += dsp_dlam * jnp.sum(dla * ((-LRU_C) * r), axis=0, keepdims=True)
            dpa = (dla * ((-LRU_C) * sp)) * r * (1.0 - r)
            dpx = di * i * (1.0 - i)
            dpab, dpxb = dpa.astype(BF16), dpx.astype(BF16)
            dwa_ref[...] += _dot_tn(ub, dpab)
            dwx_ref[...] += _dot_tn(ub, dpxb)
            dba_ref[...] += jnp.sum(dpa, axis=0, keepdims=True)
            dbx_ref[...] += jnp.sum(dpx, axis=0, keepdims=True)
            du = g * mult * i + _dot_nt(dpab, wa_ref[...]) + _dot_nt(dpxb, wx_ref[...])
            du_s[pl.ds(base, CH), :] = du
            dcb_ref[...] += jnp.sum(du, axis=0, keepdims=True)

        _for_chunks(S, grads, CH)

        def conv_bwd(base):
            du = du_s[pl.ds(base, CH), :]
            dwin = jnp.concatenate([du, _next8(du_s, base, CH, S, ALL, 0.0)], axis=0)
            xwin = jnp.concatenate([_prev8(li_ref, base, LRU_X, 0.0), li_ref[pl.ds(base, CH), LRU_X]], axis=0)
            dx = jnp.zeros((CH, LRU_W), F32)
            for k in range(LRU_TAPS):
                dx = dx + cw_ref[k:k + 1, :] * _shift_up(dwin, LRU_TAPS - 1 - k, CH)
                dcw_ref[k:k + 1, :] += jnp.sum(du * _shift_down(xwin, LRU_TAPS - 1 - k, CH), axis=0, keepdims=True)
            dli_ref[pl.ds(base, CH), LRU_X] = dx.astype(BF16)

        _for_chunks(S, conv_bwd, CH)

    vec = pl.BlockSpec((1, LRU_W), lambda i: (0, 0))
    mat = pl.BlockSpec((LRU_W, LRU_W), lambda i: (0, 0))
    taps = pl.BlockSpec((LRU_TAPS, LRU_W), lambda i: (0, 0))
    seq = lambda width: pl.BlockSpec((None, S, width), lambda i: (i, 0, 0))
    vec_shape = jax.ShapeDtypeStruct((1, LRU_W), F32)
    mat_shape = jax.ShapeDtypeStruct((LRU_W, LRU_W), F32)
    return _pallas_hosting(
        body, ex, name="lru_bwd", grid=(B,),
        in_specs=[seq(LRU_IN_W), seq(LRU_W), seq(LRU_W), taps, vec, mat, vec, mat, vec, vec],
        out_specs=[seq(LRU_IN_W), taps, vec, mat, vec, mat, vec, vec],
        out_shape=[jax.ShapeDtypeStruct((B, S, LRU_IN_W), BF16), jax.ShapeDtypeStruct((LRU_TAPS, LRU_W), F32),
                   vec_shape, mat_shape, vec_shape, mat_shape, vec_shape, vec_shape],
        scratch_shapes=[pltpu.VMEM((S, LRU_W), F32)] * 6,
        semantics=("arbitrary",), operands=(li, hs, dy, cw, cb, wa, ba, wx, bx, lam))


MESH = pl.DeviceIdType.MESH
HBM_SPEC = pl.BlockSpec(memory_space=pltpu.HBM)


def _slot(ref, p):
    return ref.at[p]


def _row_block(rows):
    return lambda ref, p: ref.at[pl.ds(p * rows, rows), :]


def _col_block(cols):
    return lambda ref, p: ref.at[:, pl.ds(p * cols, cols)]


class _Gather:
    def __init__(self, blocks, out_shapes, places):
        self.sources, self.out_shapes, self.places, self.n = list(blocks), list(out_shapes), list(places), len(blocks)

    def scratch(self):
        return [pltpu.SemaphoreType.DMA((self.n, 7)), pltpu.SemaphoreType.DMA((self.n, 7)), pltpu.SemaphoreType.DMA((self.n,))]

    def _plan(self, x_refs, out_refs, send_sems, recv_sems, local_sems):
        n = self.n
        x, y, c = lax.axis_index("x"), lax.axis_index("y"), lax.axis_index("c")
        me, sibling = (x, y, c), (x, y, 1 - c)
        chips = [(1 - x, y), (x, 1 - y), (1 - x, 1 - y)]

        def place(a, dev):
            return self.places[a](out_refs[a], 4 * dev[0] + 2 * dev[1] + dev[2])

        def copy(a, k, blk, to, src=None):
            return pltpu.make_async_remote_copy(
                src_ref=place(a, blk) if src is None else src, dst_ref=place(a, blk),
                send_sem=send_sems.at[a, k], recv_sem=recv_sems.at[a, k], device_id=to, device_id_type=MESH)

        mine = [pltpu.make_async_copy(x_refs[a], place(a, me), local_sems.at[a]) for a in range(n)]
        first = [copy(a, 0, me, sibling, src=x_refs[a]) for a in range(n)]
        first += [copy(a, 1 + j, me, (*chip, c), src=x_refs[a]) for j, chip in enumerate(chips) for a in range(n)]
        return me, sibling, chips, c, copy, mine, first

    def start(self, *refs):
        *_, mine, first = self._plan(*refs)
        for cp in mine + first:
            cp.start()

    def forward(self, *refs):
        me, sibling, chips, c, copy, _, _ = self._plan(*refs)
        for j, chip in enumerate(chips):
            for a in range(self.n):
                copy(a, 1 + j, (*chip, c), me).wait_recv()
                copy(a, 4 + j, (*chip, c), sibling).start()

    def finish(self, *refs):
        me, sibling, chips, c, copy, mine, first = self._plan(*refs)
        passed = [copy(a, 4 + j, (*chip, c), sibling) for j, chip in enumerate(chips) for a in range(self.n)]
        for a in range(self.n):
            copy(a, 0, sibling, me).wait_recv()
        for j, chip in enumerate(chips):
            for a in range(self.n):
                copy(a, 4 + j, (*chip, 1 - c), me).wait_recv()
        for cp in first + passed:
            cp.wait_send()
        for cp in mine:
            cp.wait()


class _GradExchange:
    def __init__(self, sources, takes, piece_shapes):
        self.sources, self.takes, self.n = list(sources), list(takes), len(sources)
        self.out_shapes = [jax.ShapeDtypeStruct((N_DEV,) + tuple(s), BF16) for s in piece_shapes]

    def scratch(self):
        return [pltpu.SemaphoreType.DMA((self.n, 7)), pltpu.SemaphoreType.DMA((self.n, 7)), pltpu.SemaphoreType.DMA((self.n,))]

    def _copies(self, src_refs, out_refs, send_sems, recv_sems, local_sems):
        x, y, c = lax.axis_index("x"), lax.axis_index("y"), lax.axis_index("c")
        me = 4 * x + 2 * y + c
        mine = [pltpu.make_async_copy(self.takes[i](src_refs[i], me), out_refs[i].at[me], local_sems.at[i]) for i in range(self.n)]
        remote = []
        for k in range(1, N_DEV):
            px, py, pc = x ^ ((k >> 2) & 1), y ^ ((k >> 1) & 1), c ^ (k & 1)
            peer = 4 * px + 2 * py + pc
            for i in range(self.n):
                remote.append(pltpu.make_async_remote_copy(
                    src_ref=self.takes[i](src_refs[i], peer), dst_ref=out_refs[i].at[me], send_sem=send_sems.at[i, k - 1],
                    recv_sem=recv_sems.at[i, k - 1], device_id=(px, py, pc), device_id_type=MESH))
        return mine, remote

    def start(self, *refs):
        mine, remote = self._copies(*refs)
        for cp in mine + remote:
            cp.start()

    def forward(self, *refs):
        pass

    def finish(self, *refs):
        mine, remote = self._copies(*refs)
        for cp in remote:
            cp.wait_recv()
        for cp in remote:
            cp.wait_send()
        for cp in mine:
            cp.wait()


def _run_exchange(ex, name):
    def body(*refs):
        src_refs, out_refs, sems = refs[:ex.n], refs[ex.n:2 * ex.n], refs[2 * ex.n:]
        ex.start(src_refs, out_refs, *sems)
        ex.forward(src_refs, out_refs, *sems)
        ex.finish(src_refs, out_refs, *sems)

    return pl.pallas_call(
        body, name=name, out_shape=ex.out_shapes, in_specs=[HBM_SPEC] * ex.n, out_specs=[HBM_SPEC] * ex.n,
        scratch_shapes=ex.scratch(),
    )(*ex.sources)


def _pallas_hosting(body, ex, *, name, grid, in_specs, out_specs, out_shape, scratch_shapes, semantics, operands):
    if ex is None:
        outs = pl.pallas_call(body, name=name, grid=grid, in_specs=in_specs, out_specs=out_specs, out_shape=out_shape,
                              scratch_shapes=scratch_shapes, compiler_params=_params(*semantics))(*operands)
        return outs, None
    n_in, n_out, n_scr, n = len(in_specs), len(out_specs), len(scratch_shapes), ex.n

    def at_step(pick):
        conds = [pl.program_id(k) == pick(size) for k, size in enumerate(grid)]
        return functools.reduce(jnp.logical_and, conds)

    def hosting(*refs):
        ins, ex_ins = refs[:n_in], refs[n_in:n_in + n]
        outs, ex_outs = refs[n_in + n:n_in + n + n_out], refs[n_in + n + n_out:n_in + 2 * n + n_out]
        scratch, sems = refs[n_in + 2 * n + n_out:n_in + 2 * n + n_out + n_scr], refs[n_in + 2 * n + n_out + n_scr:]
        pl.when(at_step(lambda size: 0))(lambda: ex.start(ex_ins, ex_outs, *sems))
        pl.when(at_step(lambda size: size - 1))(lambda: ex.forward(ex_ins, ex_outs, *sems))
        body(*ins, *outs, *scratch)
        pl.when(at_step(lambda size: size - 1))(lambda: ex.finish(ex_ins, ex_outs, *sems))

    res = pl.pallas_call(
        hosting, name=name, grid=grid, in_specs=list(in_specs) + [HBM_SPEC] * n, out_specs=list(out_specs) + [HBM_SPEC] * n,
        out_shape=list(out_shape) + ex.out_shapes, scratch_shapes=list(scratch_shapes) + ex.scratch(),
        compiler_params=_params(*(["arbitrary"] * len(grid))),
    )(*operands, *ex.sources)
    return res[:n_out], res[n_out:]


def _adamw(w, g, m, v, rows_per_step, name):
    R, C = w.shape
    c1 = 1.0 - ADAM_B1 ** ADAM_STEP
    c2 = 1.0 - ADAM_B2 ** ADAM_STEP

    def body(w_ref, g_ref, m_ref, v_ref, d_ref, nm_ref, nv_ref):
        gv = g_ref[...]
        nm = ADAM_B1 * m_ref[...] + (1.0 - ADAM_B1) * gv
        nv = ADAM_B2 * v_ref[...] + (1.0 - ADAM_B2) * (gv * gv)
        nm_ref[...] = nm
        nv_ref[...] = nv
        d_ref[...] = (-ADAM_LR) * ((nm / c1) / (jnp.sqrt(nv / c2) + ADAM_EPS) + ADAM_WD * w_ref[...])

    spec = pl.BlockSpec((rows_per_step, C), lambda i: (i, 0))
    shape = jax.ShapeDtypeStruct((R, C), F32)
    return pl.pallas_call(
        body, name=name, grid=(R // rows_per_step,),
        in_specs=[spec] * 4, out_specs=[spec] * 3, out_shape=[shape] * 3,
        compiler_params=_params("parallel"),
    )(w, g, m, v)


def _adam_update(w, g, m, v):
    nm = ADAM_B1 * m + (1.0 - ADAM_B1) * g
    nv = ADAM_B2 * v + (1.0 - ADAM_B2) * (g * g)
    c1 = 1.0 - ADAM_B1 ** ADAM_STEP
    c2 = 1.0 - ADAM_B2 ** ADAM_STEP
    return (-ADAM_LR) * ((nm / c1) / (jnp.sqrt(nv / c2) + ADAM_EPS) + ADAM_WD * w), nm, nv


def _small_sum_adamw(gathered, vectors, vector_widths, loss_slots, state):
    names, updated = list(vector_widths) + list(gathered), list(state)
    n_in, n_g, n_u = len(gathered) + 1, len(names), len(updated)

    def slot_sum(ref):
        acc = ref[0]
        for q in range(1, N_DEV):
            acc = acc + ref[q]
        return acc

    def body(*refs):
        p_refs, loss_in = refs[:n_in], refs[n_in]
        st = refs[n_in + 1:n_in + 1 + 3 * n_u]
        outs = refs[n_in + 1 + 3 * n_u:]
        g_refs, loss_out, upd = outs[:n_g], outs[n_g], outs[n_g + 1:]
        loss_out[...] = slot_sum(loss_in)
        side_by_side, off, sums = slot_sum(p_refs[0]), 0, {}
        for n, width in vector_widths.items():
            sums[n] = side_by_side[:, off:off + width]
            off += width
        for i, n in enumerate(gathered):
            sums[n] = slot_sum(p_refs[1 + i])
        for i, n in enumerate(names):
            g_refs[i][...] = sums[n]
            if n in state:
                j = updated.index(n)
                d, nm, nv = _adam_update(st[3 * j][...], sums[n], st[3 * j + 1][...], st[3 * j + 2][...])
                upd[3 * j][...], upd[3 * j + 1][...], upd[3 * j + 2][...] = d, nm, nv

    g_shapes = [jax.ShapeDtypeStruct((DEPTH, w), F32) for w in vector_widths.values()]
    g_shapes += [jax.ShapeDtypeStruct(gathered[n].shape[1:], F32) for n in gathered]
    u_shapes = [jax.ShapeDtypeStruct(state[n][0].shape, F32) for n in updated for _ in range(3)]
    outs = pl.pallas_call(
        body, name="small_sum_adamw", out_shape=g_shapes + [jax.ShapeDtypeStruct((8, 128), F32)] + u_shapes,
        compiler_params=_params(),
    )(vectors, *gathered.values(), loss_slots, *[a for n in updated for a in state[n]])
    g = dict(zip(names, outs[:n_g]))
    upd = {n: tuple(outs[n_g + 1 + 3 * j:n_g + 4 + 3 * j]) for j, n in enumerate(updated)}
    return g, outs[n_g], upd


WEIGHT_ORDER = ("norm1_g", "w_in", "conv_dw_w", "conv_dw_b", "conv_ln_g", "conv_ln_b", "lru_conv_w", "lru_conv_b", "lru_wa",
                "lru_ba", "lru_wx", "lru_bx", "lru_lambda", "w_out", "norm2_g", "w_up", "w_down", "final_g")
BIG = ("w_in", "w_out", "w_up", "w_down")
SMALL_SHARDED = {"conv_dw_w": (DEPTH, CONV_TAPS, CONV_W), "lru_conv_w": (DEPTH, LRU_TAPS, LRU_W)}
SMALL_FULL = {
    "norm1_g": (DEPTH, D_MODEL), "conv_dw_w": (DEPTH, CONV_TAPS, CONV_W), "conv_dw_b": (DEPTH, CONV_W),
    "conv_ln_g": (DEPTH, CONV_W), "conv_ln_b": (DEPTH, CONV_W), "lru_conv_w": (DEPTH, LRU_TAPS, LRU_W),
    "lru_conv_b": (DEPTH, LRU_W), "lru_wa": (DEPTH, LRU_HEADS, HEAD_DIM, HEAD_DIM), "lru_ba": (DEPTH, LRU_W),
    "lru_wx": (DEPTH, LRU_HEADS, HEAD_DIM, HEAD_DIM), "lru_bx": (DEPTH, LRU_W), "lru_lambda": (DEPTH, LRU_W),
    "norm2_g": (DEPTH, D_MODEL), "final_g": (D_MODEL,),
}
SMALL_COLS = 128
FILTER_ROWS = 24
W_IN_SHARD = IN_COLS // N_DEV
W_OUT_SHARD = D_MODEL // N_DEV
FF_SHARD = D_FF // N_DEV


def _pack_rows(flat_parts, cols, rows):
    flat = jnp.concatenate(flat_parts)
    return jnp.pad(flat, (0, rows * cols - flat.shape[0])).reshape(rows, cols)


WEIGHT_GATHER = {
    "w_in": ((N_DEV, D_MODEL, W_IN_SHARD), _slot),
    "w_out": ((D_MODEL, D_MODEL), _row_block(W_OUT_SHARD)),
    "w_up": ((D_MODEL, D_FF), _col_block(FF_SHARD)),
    "w_down": ((D_FF, D_MODEL), _row_block(FF_SHARD)),
}
GATHER_HOSTS = {
    "inproj": (("w_out", 0),), "attn_fwd": (("w_up", 0),), "lru_fwd": (("w_down", 0),),
    "outproj": (("w_in", 1),), "up": (("w_up", 1),), "down": (("w_down", 1), ("w_out", 1)),
}


def _weight_gather(local, items, with_filters=False):
    blocks = [local[n][l].astype(BF16) for n, l in items]
    shapes = [jax.ShapeDtypeStruct(WEIGHT_GATHER[n][0], BF16) for n, _ in items]
    places = [WEIGHT_GATHER[n][1] for n, _ in items]
    if with_filters:
        blocks.append(_pack_rows([local[n].reshape(-1) for n in SMALL_SHARDED], SMALL_COLS, FILTER_ROWS))
        shapes.append(jax.ShapeDtypeStruct((N_DEV, FILTER_ROWS, SMALL_COLS), F32))
        places.append(_slot)
    return _Gather(blocks, shapes, places)


def _keep_gathered(full, items, landed):
    for (n, l), arr in zip(items, landed):
        full[n][l] = arr.transpose(1, 0, 2).reshape(D_MODEL, IN_COLS) if n == "w_in" else arr


def _unpack_filters(slots):
    flat, off, out = slots.reshape(N_DEV, -1), 0, {}
    for n, shp in SMALL_SHARDED.items():
        shard = shp[:-1] + (shp[-1] // N_DEV,)
        size = int(np.prod(shard))
        out[n] = jnp.moveaxis(flat[:, off:off + size].reshape((N_DEV,) + shard), 0, -2).reshape(shp)
        off += size
    return out


def _grad_exchange(name, dw):
    if name == "w_in":
        return _GradExchange([dw.reshape(D_MODEL, N_DEV, W_IN_SHARD).transpose(1, 0, 2)], [_slot], [(D_MODEL, W_IN_SHARD)])
    if name == "w_out":
        return _GradExchange([dw], [_row_block(W_OUT_SHARD)], [(W_OUT_SHARD, D_MODEL)])
    return _GradExchange([dw], [_col_block(FF_SHARD)], [(D_MODEL, FF_SHARD)])


def _sum_adamw(parts, w, m, v, rows_per_step, transposed, name):
    _, R, C = parts[0].shape
    tr = rows_per_step
    steps = R // tr
    c1 = 1.0 - ADAM_B1 ** ADAM_STEP
    c2 = 1.0 - ADAM_B2 ** ADAM_STEP

    def slot_sum(p_ref):
        acc = p_ref[0].astype(F32)
        for q in range(1, N_DEV):
            acc = acc + p_ref[q].astype(F32)
        return acc

    def body(p0_ref, p1_ref, w_ref, m_ref, v_ref, g_ref, d_ref, nm_ref, nv_ref):
        gv = jnp.where(pl.program_id(0) == 0, slot_sum(p0_ref), slot_sum(p1_ref))
        if transposed:
            gv = gv.T
        nm = ADAM_B1 * m_ref[...] + (1.0 - ADAM_B1) * gv
        nv = ADAM_B2 * v_ref[...] + (1.0 - ADAM_B2) * (gv * gv)
        g_ref[...] = gv
        nm_ref[...] = nm
        nv_ref[...] = nv
        d_ref[...] = (-ADAM_LR) * ((nm / c1) / (jnp.sqrt(nv / c2) + ADAM_EPS) + ADAM_WD * w_ref[...])

    if transposed:
        spec = pl.BlockSpec((None, C, tr), lambda l, i: (l, 0, i))
    else:
        spec = pl.BlockSpec((None, tr, C), lambda l, i: (l, i, 0))
    shape = jax.ShapeDtypeStruct(w.shape, F32)
    part0 = pl.BlockSpec((N_DEV, tr, C), lambda l, i: (0, jnp.where(l == 0, i, steps - 1), 0))
    part1 = pl.BlockSpec((N_DEV, tr, C), lambda l, i: (0, jnp.where(l == 1, i, 0), 0))
    return pl.pallas_call(
        body, name=name, grid=(DEPTH, steps),
        in_specs=[part0, part1, spec, spec, spec],
        out_specs=[spec] * 4, out_shape=[shape] * 4,
        compiler_params=_params("arbitrary", "arbitrary"),
    )(parts[0], parts[1], w, m, v)


def _block_diag(w):
    eye = jnp.eye(LRU_HEADS, dtype=bool)
    return jnp.where(eye[:, None, :, None], w[:, :, None, :], jnp.zeros((), w.dtype)).reshape(LRU_W, LRU_W)


def _diag_blocks(m):
    eye = jnp.eye(LRU_HEADS, dtype=bool)
    m4 = m.reshape(LRU_HEADS, HEAD_DIM, LRU_HEADS, HEAD_DIM)
    return jnp.sum(jnp.where(eye[:, None, :, None], m4, 0.0), axis=2)


def kernel(x, norm1_g, w_in, conv_dw_w, conv_dw_b, conv_ln_g, conv_ln_b, lru_conv_w, lru_conv_b, lru_wa, lru_ba, lru_wx, lru_bx, lru_lambda, w_out, norm2_g, w_up, w_down, final_g, loss_target, m_norm1_g, m_w_in, m_conv_dw_w, m_conv_dw_b, m_conv_ln_g, m_conv_ln_b, m_lru_conv_w, m_lru_conv_b, m_lru_wa, m_lru_ba, m_lru_wx, m_lru_bx, m_lru_lambda, m_w_out, m_norm2_g, m_w_up, m_w_down, m_final_g, v_norm1_g, v_w_in, v_conv_dw_w, v_conv_dw_b, v_conv_ln_g, v_conv_ln_b, v_lru_conv_w, v_lru_conv_b, v_lru_wa, v_lru_ba, v_lru_wx, v_lru_bx, v_lru_lambda, v_w_out, v_norm2_g, v_w_up, v_w_down, v_final_g):
    local = dict(zip(WEIGHT_ORDER, (norm1_g, w_in, conv_dw_w, conv_dw_b, conv_ln_g, conv_ln_b, lru_conv_w, lru_conv_b, lru_wa,
                                    lru_ba, lru_wx, lru_bx, lru_lambda, w_out, norm2_g, w_up, w_down, final_g)))
    mom1 = dict(zip(WEIGHT_ORDER, (m_norm1_g, m_w_in, m_conv_dw_w, m_conv_dw_b, m_conv_ln_g, m_conv_ln_b, m_lru_conv_w,
                                   m_lru_conv_b, m_lru_wa, m_lru_ba, m_lru_wx, m_lru_bx, m_lru_lambda, m_w_out, m_norm2_g,
                                   m_w_up, m_w_down, m_final_g)))
    mom2 = dict(zip(WEIGHT_ORDER, (v_norm1_g, v_w_in, v_conv_dw_w, v_conv_dw_b, v_conv_ln_g, v_conv_ln_b, v_lru_conv_w,
                                   v_lru_conv_b, v_lru_wa, v_lru_ba, v_lru_wx, v_lru_bx, v_lru_lambda, v_w_out, v_norm2_g,
                                   v_w_up, v_w_down, v_final_g)))
    B, S, _ = x.shape
    T = B * S
    my_slot = 4 * lax.axis_index("x") + 2 * lax.axis_index("y") + lax.axis_index("c")
    row = lambda a: a.reshape(1, -1)

    full = {n: [None] * DEPTH for n in BIG}
    first_items = (("w_in", 0),)
    landed = _run_exchange(_weight_gather(local, first_items, with_filters=True), "gather_first_weights")
    _keep_gathered(full, first_items, landed)
    full.update(_unpack_filters(landed[-1]))

    def hosted(call, layer, fn, *args):
        items = GATHER_HOSTS[call] if layer == 0 else ()
        outs, landed = fn(*args, ex=_weight_gather(local, items) if items else None)
        _keep_gathered(full, items, landed or ())
        return outs

    saved = []
    cur = x.reshape(T, D_MODEL)
    for l in range(DEPTH):
        h, qkv, ci, li = hosted("inproj", l, _inproj, cur, row(norm1_g[l]), full["w_in"][l])
        qkv = qkv.reshape(B, S, QKV_W)
        o, lse = hosted("attn_fwd", l, _attn_fwd, qkv)
        o, lse = o.reshape(T, ATTN_W), lse.reshape(T, ATTN_W)
        ci = ci.reshape(B, S, CONV_IN_W)
        li = li.reshape(B, S, LRU_IN_W)
        conv_p = (full["conv_dw_w"][l], row(conv_dw_b[l]), row(conv_ln_g[l]), row(conv_ln_b[l]))
        lru_p = (full["lru_conv_w"][l], row(lru_conv_b[l]), _block_diag(lru_wa[l]).astype(BF16), row(lru_ba[l]),
                 _block_diag(lru_wx[l]).astype(BF16), row(lru_bx[l]), row(lru_lambda[l]))
        yc, cpre = _conv_fwd(ci, *conv_p)
        yl, hs = hosted("lru_fwd", l, _lru_fwd, li, *lru_p)
        x1, mix = hosted("outproj", l, _outproj, cur, o, lse, yc.reshape(T, CONV_W), yl.reshape(T, LRU_W), full["w_out"][l])
        h2, r = hosted("up", l, _up, x1, row(norm2_g[l]), full["w_up"][l])
        (x2,) = hosted("down", l, _down, x1, r, full["w_down"][l])
        saved.append(dict(x=cur, h=h, qkv=qkv, o=o, lse=lse, ci=ci, li=li, cpre=cpre, hs=hs, x1=x1, mix=mix, h2=h2, r=r,
                          conv_p=conv_p, lru_p=lru_p))
        cur = x2

    dx, loss_part, dgf = _loss_head(cur, loss_target.reshape(T, D_MODEL), row(final_g))

    received = {n: [None] * DEPTH for n in BIG}
    small_grads = {n: [None] * DEPTH for n in SMALL_FULL if n != "final_g"}
    for l in reversed(range(DEPTH)):
        sv = saved[l]
        (dpre,), _ = _down_bwd_act(dx, sv["r"], full["w_down"][l])
        dw_down = _down_bwd_w(sv["r"], dx)
        (dx1, dg2), _ = _up_bwd_act(dpre, full["w_up"][l], sv["x1"], row(norm2_g[l]), dx)
        dw_up = _up_bwd_w(sv["h2"], dpre)
        do, dd, dyc, dyl, dw_out = _outproj_bwd(dx1, sv["mix"], full["w_out"][l], sv["o"], sv["lse"])
        seq = lambda a: a.reshape(B, S, ATTN_W)
        (dq, dk, dv), (received["w_down"][l],) = _attn_bwd(sv["qkv"], seq(do), seq(sv["lse"]), seq(dd),
                                                           ex=_grad_exchange("w_down", dw_down))
        (dci, dcw, dcb, dlg, dlb), (received["w_out"][l],) = _conv_bwd(
            sv["ci"], sv["cpre"], dyc.reshape(B, S, CONV_W), sv["conv_p"][0], sv["conv_p"][2], sv["conv_p"][3],
            ex=_grad_exchange("w_out", dw_out))
        (dli, dlcw, dlcb, dwa, dba, dwx, dbx, dlam), (received["w_up"][l],) = _lru_bwd(
            sv["li"], sv["hs"], dyl.reshape(B, S, LRU_W), *sv["lru_p"], ex=_grad_exchange("w_up", dw_up))
        dz = tuple(t.reshape(T, -1) for t in (dq, dk, dv, dci, dli))
        dw_in = _inproj_bwd_w(dz, sv["h"])
        (dx, dg1), (received["w_in"][l],) = _inproj_bwd_act(dz, full["w_in"][l], sv["x"], row(norm1_g[l]), dx1,
                                                            ex=_grad_exchange("w_in", dw_in))
        for n, g in (("norm1_g", dg1), ("conv_dw_w", dcw), ("conv_dw_b", dcb), ("conv_ln_g", dlg), ("conv_ln_b", dlb),
                     ("lru_conv_w", dlcw), ("lru_conv_b", dlcb), ("lru_wa", _diag_blocks(dwa)), ("lru_ba", dba),
                     ("lru_wx", _diag_blocks(dwx)), ("lru_bx", dbx), ("lru_lambda", dlam), ("norm2_g", dg2)):
            small_grads[n][l] = g.reshape(SMALL_FULL[n][1:])
    grad_x = dx.reshape(B, S, D_MODEL)

    two_d = lambda n: (int(np.prod(SMALL_FULL[n][:-1])), SMALL_FULL[n][-1])
    small_local = {n: jnp.stack(g).reshape(two_d(n)) for n, g in small_grads.items()}
    small_local["final_g"] = dgf
    vector_widths = {n: shp[1] for n, shp in SMALL_FULL.items() if len(shp) == 2}
    other_names = [n for n in SMALL_FULL if n not in vector_widths]
    vectors = jnp.concatenate([small_local[n] for n in vector_widths], axis=1)
    blocks = [vectors] + [small_local[n] for n in other_names] + [loss_part]
    landed = _run_exchange(
        _Gather(blocks, [jax.ShapeDtypeStruct((N_DEV,) + b.shape, F32) for b in blocks], [_slot] * len(blocks)),
        "gather_small_grads")
    gathered = dict(zip(other_names, landed[1:-1]))

    grads, delta, new_m, new_v = {}, {}, {}, {}
    for n, rows_per_step in (("w_in", 256), ("w_out", W_OUT_SHARD), ("w_up", 256), ("w_down", 256)):
        grads[n], delta[n], new_m[n], new_v[n] = _sum_adamw(received[n], local[n], mom1[n], mom2[n], rows_per_step,
                                                            n == "w_down", "sum_adamw_" + n)

    replicated = [n for n in SMALL_FULL if n not in SMALL_SHARDED]
    state = {n: tuple(src[n].reshape(two_d(n)) for src in (local, mom1, mom2)) for n in replicated}
    small_g, loss_sum, updated = _small_sum_adamw(gathered, landed[0], vector_widths, landed[-1], state)
    loss = loss_sum[0, 0]
    for n in replicated:
        grads[n] = small_g[n].reshape(SMALL_FULL[n])
        delta[n], new_m[n], new_v[n] = (t.reshape(SMALL_FULL[n]) for t in updated[n])
    for n, fullshape in SMALL_SHARDED.items():
        width = fullshape[-1] // N_DEV
        g = lax.dynamic_slice_in_dim(small_g[n], my_slot * width, width, axis=1)
        d, nm, nv = _adamw(local[n].reshape(g.shape), g, mom1[n].reshape(g.shape), mom2[n].reshape(g.shape), g.shape[0],
                           "adamw_" + n)
        grads[n], delta[n], new_m[n], new_v[n] = (t.reshape(local[n].shape) for t in (g, d, nm, nv))

    return (loss, grad_x, *[grads[n] for n in WEIGHT_ORDER], *[delta[n] for n in WEIGHT_ORDER],
            *[new_m[n] for n in WEIGHT_ORDER], *[new_v[n] for n in WEIGHT_ORDER])
```

```python
import functools
import math

import numpy as np
import jax
import jax.numpy as jnp
from jax import lax
from jax.experimental import pallas as pl
from jax.experimental.pallas import tpu as pltpu

F32 = jnp.float32
BF16 = jnp.bfloat16

D_MODEL = 1024
SEQ_LEN = 2048
HEAD_DIM = 64
ATTN_W = 384
CONV_W = 256
CONV_TAPS = 31
LRU_W = 384
LRU_TAPS = 4
LRU_HEADS = 6
LRU_C = 8.0
QKV_W = 3 * ATTN_W
CONV_IN_W = 2 * CONV_W
LRU_IN_W = 2 * LRU_W
IN_COLS = QKV_W + CONV_IN_W + LRU_IN_W
D_FF = 4096
DEPTH = 2
N_DEV = 8
RMS_EPS = 1e-6
LN_EPS = 1e-5
ATTN_BLOCK = 128
ATTN_DILATIONS = (1, 4, 16)
N_UNITS = 16
UNIT_UNROLL = 8
NEG_BIG = -1e30

ADAM_LR = 0.001
ADAM_B1 = 0.9
ADAM_B2 = 0.999
ADAM_EPS = 1e-08
ADAM_WD = 0.01
ADAM_STEP = 10

VMEM_LIMIT = 56 * 1024 * 1024
ROW_TILE = 512
GRAD_ROW_TILE = 1024
CONV_FWD_CHUNK = 128
CONV_BWD_CHUNK = 128
LRU_FWD_CHUNK = 128
LRU_BWD_CHUNK = 128


def _params(*sem):
    return pltpu.CompilerParams(dimension_semantics=sem if sem else None, vmem_limit_bytes=VMEM_LIMIT)


def _resident(shape):
    return pl.BlockSpec(shape, lambda *_: (0,) * len(shape), pipeline_mode=pl.Buffered(1))


def _dot(a, b):
    return jnp.dot(a, b, preferred_element_type=F32)


def _dot_nt(a, b):
    return lax.dot_general(a, b, (((1,), (1,)), ((), ())), preferred_element_type=F32)


def _dot_tn(a, b):
    return lax.dot_general(a, b, (((0,), (0,)), ((), ())), preferred_element_type=F32)


def _rms_fwd(x, g):
    rstd = lax.rsqrt(jnp.mean(x * x, axis=-1, keepdims=True) + RMS_EPS)
    xhat = x * rstd
    return xhat * g, xhat, rstd


def _rms_bwd(dh, xhat, rstd, g):
    dxh = dh * g
    dx = rstd * (dxh - xhat * jnp.mean(dxh * xhat, axis=-1, keepdims=True))
    dg = jnp.sum(dh * xhat, axis=0, keepdims=True)
    return dx, dg


def _sigmoid(x):
    return 0.5 * jnp.tanh(0.5 * x) + 0.5


def _one_minus_exp(x, exp_x):
    small = -x * (1.0 + x * (0.5 + x * (1.0 / 6.0)))
    return jnp.where(x > -0.01, small, 1.0 - exp_x)


def _log1p(z):
    w = 1.0 + z
    return jnp.where(w == 1.0, z, z * jnp.log(w) / jnp.where(w == 1.0, 1.0, w - 1.0))


def _softplus(x):
    return jnp.maximum(x, 0.0) + _log1p(jnp.exp(-jnp.abs(x)))


GELU_K = math.sqrt(2.0 / math.pi)


def _gelu(x):
    t = jnp.tanh(GELU_K * (x + 0.044715 * x * x * x))
    return 0.5 * x * (1.0 + t), t


def _gelu_grad(x, t):
    return 0.5 * (1.0 + t) + 0.5 * x * (1.0 - t * t) * GELU_K * (1.0 + 3.0 * 0.044715 * x * x)


def _inproj(x2d, g, w, ex=None):
    T = x2d.shape[0]
    tm = ROW_TILE

    def body(x_ref, g_ref, w_ref, h_ref, qkv_ref, ci_ref, li_ref):
        h, _, _ = _rms_fwd(x_ref[...], g_ref[...])
        hb = h.astype(BF16)
        h_ref[...] = hb
        qkv_ref[...] = _dot(hb, w_ref[:, 0:QKV_W])
        ci_ref[...] = _dot(hb, w_ref[:, QKV_W:QKV_W + CONV_IN_W])
        li_ref[...] = _dot(hb, w_ref[:, QKV_W + CONV_IN_W:IN_COLS])

    return _pallas_hosting(
        body, ex, name="inproj", grid=(T // tm,),
        in_specs=[pl.BlockSpec((tm, D_MODEL), lambda i: (i, 0)),
                  pl.BlockSpec((1, D_MODEL), lambda i: (0, 0)),
                  _resident((D_MODEL, IN_COLS))],
        out_specs=[pl.BlockSpec((tm, D_MODEL), lambda i: (i, 0)),
                   pl.BlockSpec((tm, QKV_W), lambda i: (i, 0)),
                   pl.BlockSpec((tm, CONV_IN_W), lambda i: (i, 0)),
                   pl.BlockSpec((tm, LRU_IN_W), lambda i: (i, 0))],
        out_shape=[jax.ShapeDtypeStruct((T, D_MODEL), BF16), jax.ShapeDtypeStruct((T, QKV_W), F32),
                   jax.ShapeDtypeStruct((T, CONV_IN_W), F32), jax.ShapeDtypeStruct((T, LRU_IN_W), F32)],
        scratch_shapes=[], semantics=("parallel",), operands=(x2d, g, w))


def _attn_alpha(lse):
    l0, l1, l2 = lse[:, 0:128], lse[:, 128:256], lse[:, 256:384]
    m = jnp.maximum(jnp.maximum(l0, l1), l2)
    e0, e1, e2 = jnp.exp(l0 - m), jnp.exp(l1 - m), jnp.exp(l2 - m)
    inv = 1.0 / (e0 + e1 + e2)
    return e0 * inv, e1 * inv, e2 * inv


def _outproj(x2d, o, lse, yc, yl, w, ex=None):
    T = x2d.shape[0]
    tm = ROW_TILE

    def body(x_ref, o_ref, lse_ref, yc_ref, yl_ref, w_ref, x1_ref, mix_ref):
        al = _attn_alpha(lse_ref[...])
        for p in range(3):
            mix_ref[:, p * 128:(p + 1) * 128] = (o_ref[:, p * 128:(p + 1) * 128] * al[p]).astype(BF16)
        mix_ref[:, ATTN_W:ATTN_W + CONV_W] = yc_ref[...]
        mix_ref[:, ATTN_W + CONV_W:D_MODEL] = yl_ref[...]
        x1_ref[...] = x_ref[...] + _dot(mix_ref[...], w_ref[...])

    return _pallas_hosting(
        body, ex, name="outproj", grid=(T // tm,),
        in_specs=[pl.BlockSpec((tm, D_MODEL), lambda i: (i, 0)),
                  pl.BlockSpec((tm, ATTN_W), lambda i: (i, 0)),
                  pl.BlockSpec((tm, ATTN_W), lambda i: (i, 0)),
                  pl.BlockSpec((tm, CONV_W), lambda i: (i, 0)),
                  pl.BlockSpec((tm, LRU_W), lambda i: (i, 0)),
                  _resident((D_MODEL, D_MODEL))],
        out_specs=[pl.BlockSpec((tm, D_MODEL), lambda i: (i, 0)),
                   pl.BlockSpec((tm, D_MODEL), lambda i: (i, 0))],
        out_shape=[jax.ShapeDtypeStruct((T, D_MODEL), F32), jax.ShapeDtypeStruct((T, D_MODEL), BF16)],
        scratch_shapes=[], semantics=("parallel",), operands=(x2d, o, lse, yc, yl, w))


FF_CHUNK = 1024


def _up(x1, g, w, ex=None):
    T = x1.shape[0]
    tm = ROW_TILE

    def body(x_ref, g_ref, w_ref, h_ref, r_ref):
        h, _, _ = _rms_fwd(x_ref[...], g_ref[...])
        hb = h.astype(BF16)
        h_ref[...] = hb
        for c in range(0, D_FF, FF_CHUNK):
            r_ref[:, c:c + FF_CHUNK] = jnp.maximum(_dot(hb, w_ref[:, c:c + FF_CHUNK]), 0.0).astype(BF16)

    return _pallas_hosting(
        body, ex, name="up", grid=(T // tm,),
        in_specs=[pl.BlockSpec((tm, D_MODEL), lambda i: (i, 0)),
                  pl.BlockSpec((1, D_MODEL), lambda i: (0, 0)),
                  _resident((D_MODEL, D_FF))],
        out_specs=[pl.BlockSpec((tm, D_MODEL), lambda i: (i, 0)),
                   pl.BlockSpec((tm, D_FF), lambda i: (i, 0))],
        out_shape=[jax.ShapeDtypeStruct((T, D_MODEL), BF16), jax.ShapeDtypeStruct((T, D_FF), BF16)],
        scratch_shapes=[], semantics=("parallel",), operands=(x1, g, w))


def _square_bf16(r):
    rf = r.astype(F32)
    return (rf * rf).astype(BF16)


def _down(x1, r, w, ex=None):
    T = x1.shape[0]
    tm = ROW_TILE

    def body(x_ref, r_ref, w_ref, o_ref):
        acc = x_ref[...]
        for c in range(0, D_FF, FF_CHUNK):
            acc = acc + _dot(_square_bf16(r_ref[:, c:c + FF_CHUNK]), w_ref[c:c + FF_CHUNK, :])
        o_ref[...] = acc

    return _pallas_hosting(
        body, ex, name="down", grid=(T // tm,),
        in_specs=[pl.BlockSpec((tm, D_MODEL), lambda i: (i, 0)),
                  pl.BlockSpec((tm, D_FF), lambda i: (i, 0)),
                  _resident((D_FF, D_MODEL))],
        out_specs=[pl.BlockSpec((tm, D_MODEL), lambda i: (i, 0))],
        out_shape=[jax.ShapeDtypeStruct((T, D_MODEL), F32)],
        scratch_shapes=[], semantics=("parallel",), operands=(x1, r, w))


def _loss_head(x2, target, g):
    T = x2.shape[0]
    tm = ROW_TILE

    def body(x_ref, t_ref, g_ref, dx_ref, loss_ref, dg_ref):
        @pl.when(pl.program_id(0) == 0)
        def _():
            loss_ref[...] = jnp.zeros_like(loss_ref)
            dg_ref[...] = jnp.zeros_like(dg_ref)

        gv = g_ref[...]
        y, xhat, rstd = _rms_fwd(x_ref[...], gv)
        err = y - t_ref[...]
        loss_ref[...] += 0.5 * jnp.sum(jnp.mean(err * err, axis=-1, keepdims=True))
        dy = err * (1.0 / D_MODEL)
        dx, dg = _rms_bwd(dy, xhat, rstd, gv)
        dx_ref[...] = dx
        dg_ref[...] += dg

    return pl.pallas_call(
        body, name="loss_head", grid=(T // tm,),
        in_specs=[pl.BlockSpec((tm, D_MODEL), lambda i: (i, 0)),
                  pl.BlockSpec((tm, D_MODEL), lambda i: (i, 0)),
                  pl.BlockSpec((1, D_MODEL), lambda i: (0, 0))],
        out_specs=[pl.BlockSpec((tm, D_MODEL), lambda i: (i, 0)),
                   pl.BlockSpec((8, 128), lambda i: (0, 0)),
                   pl.BlockSpec((1, D_MODEL), lambda i: (0, 0))],
        out_shape=[jax.ShapeDtypeStruct((T, D_MODEL), F32), jax.ShapeDtypeStruct((8, 128), F32),
                   jax.ShapeDtypeStruct((1, D_MODEL), F32)],
        compiler_params=_params("arbitrary"),
    )(x2, target, g)


def _down_bwd_act(dx2, r, w, ex=None):
    T = dx2.shape[0]
    tm = ROW_TILE

    def body(dx_ref, r_ref, w_ref, o_ref):
        dxb = dx_ref[...].astype(BF16)
        for c in range(0, D_FF, FF_CHUNK):
            dff = _dot_nt(dxb, w_ref[c:c + FF_CHUNK, :])
            o_ref[:, c:c + FF_CHUNK] = (dff * (2.0 * r_ref[:, c:c + FF_CHUNK].astype(F32))).astype(BF16)

    return _pallas_hosting(
        body, ex, name="down_bwd_act", grid=(T // tm,),
        in_specs=[pl.BlockSpec((tm, D_MODEL), lambda i: (i, 0)),
                  pl.BlockSpec((tm, D_FF), lambda i: (i, 0)),
                  _resident((D_FF, D_MODEL))],
        out_specs=[pl.BlockSpec((tm, D_FF), lambda i: (i, 0))],
        out_shape=[jax.ShapeDtypeStruct((T, D_FF), BF16)],
        scratch_shapes=[], semantics=("parallel",), operands=(dx2, r, w))


def _down_bwd_w(r, dx2):
    T = dx2.shape[0]
    tk = GRAD_ROW_TILE
    nk = T // tk

    def body(r_ref, dx_ref, o_ref, acc_ref):
        k = pl.program_id(0)
        dxb = dx_ref[...].astype(BF16)

        @pl.when(k == 0)
        def _():
            acc_ref[...] = jnp.zeros_like(acc_ref)

        for c in range(0, D_FF, FF_CHUNK):
            acc_ref[:, c:c + FF_CHUNK] += _dot_tn(dxb, _square_bf16(r_ref[:, c:c + FF_CHUNK]))

        @pl.when(k == nk - 1)
        def _():
            o_ref[...] = acc_ref[...].astype(BF16)

    return pl.pallas_call(
        body, name="down_bwd_w", grid=(nk,),
        in_specs=[pl.BlockSpec((tk, D_FF), lambda k: (k, 0)),
                  pl.BlockSpec((tk, D_MODEL), lambda k: (k, 0))],
        out_specs=_resident((D_MODEL, D_FF)),
        out_shape=jax.ShapeDtypeStruct((D_MODEL, D_FF), BF16),
        scratch_shapes=[pltpu.VMEM((D_MODEL, D_FF), F32)],
        compiler_params=_params("arbitrary"),
    )(r, dx2)


def _up_bwd_act(dpre, w, x1, g, dx2, ex=None):
    T = dx2.shape[0]
    tm = ROW_TILE

    def body(dp_ref, w_ref, x_ref, g_ref, dx2_ref, dx1_ref, dg_ref):
        dh = _dot_nt(dp_ref[:, 0:FF_CHUNK], w_ref[:, 0:FF_CHUNK])
        for c in range(FF_CHUNK, D_FF, FF_CHUNK):
            dh = dh + _dot_nt(dp_ref[:, c:c + FF_CHUNK], w_ref[:, c:c + FF_CHUNK])
        gv = g_ref[...]
        _, xhat, rstd = _rms_fwd(x_ref[...], gv)
        dx, dg = _rms_bwd(dh, xhat, rstd, gv)
        dx1_ref[...] = dx2_ref[...] + dx

        @pl.when(pl.program_id(0) == 0)
        def _():
            dg_ref[...] = dg

        @pl.when(pl.program_id(0) != 0)
        def _():
            dg_ref[...] += dg

    return _pallas_hosting(
        body, ex, name="up_bwd_act", grid=(T // tm,),
        in_specs=[pl.BlockSpec((tm, D_FF), lambda i: (i, 0)),
                  _resident((D_MODEL, D_FF)),
                  pl.BlockSpec((tm, D_MODEL), lambda i: (i, 0)),
                  pl.BlockSpec((1, D_MODEL), lambda i: (0, 0)),
                  pl.BlockSpec((tm, D_MODEL), lambda i: (i, 0))],
        out_specs=[pl.BlockSpec((tm, D_MODEL), lambda i: (i, 0)),
                   pl.BlockSpec((1, D_MODEL), lambda i: (0, 0))],
        out_shape=[jax.ShapeDtypeStruct((T, D_MODEL), F32), jax.ShapeDtypeStruct((1, D_MODEL), F32)],
        scratch_shapes=[], semantics=("arbitrary",), operands=(dpre, w, x1, g, dx2))


def _up_bwd_w(h2, dpre):
    T = h2.shape[0]
    tk = GRAD_ROW_TILE
    nk = T // tk

    def body(h_ref, dp_ref, o_ref, acc_ref):
        k = pl.program_id(0)
        hb = h_ref[...]

        @pl.when(k == 0)
        def _():
            acc_ref[...] = jnp.zeros_like(acc_ref)

        for c in range(0, D_FF, FF_CHUNK):
            acc_ref[:, c:c + FF_CHUNK] += _dot_tn(hb, dp_ref[:, c:c + FF_CHUNK])

        @pl.when(k == nk - 1)
        def _():
            o_ref[...] = acc_ref[...].astype(BF16)

    return pl.pallas_call(
        body, name="up_bwd_w", grid=(nk,),
        in_specs=[pl.BlockSpec((tk, D_MODEL), lambda k: (k, 0)),
                  pl.BlockSpec((tk, D_FF), lambda k: (k, 0))],
        out_specs=_resident((D_MODEL, D_FF)),
        out_shape=jax.ShapeDtypeStruct((D_MODEL, D_FF), BF16),
        scratch_shapes=[pltpu.VMEM((D_MODEL, D_FF), F32)],
        compiler_params=_params("arbitrary"),
    )(h2, dpre)


def _outproj_bwd(dx1, mix, w, o, lse):
    T = dx1.shape[0]
    tm = ROW_TILE
    nk = T // tm

    def body(dx_ref, mix_ref, w_ref, o_ref, lse_ref, do_ref, dd_ref, dc_ref, dl_ref, dw_ref, acc_ref):
        i = pl.program_id(0)
        dxb = dx_ref[...].astype(BF16)
        dmix = _dot_nt(dxb, w_ref[...])
        al = _attn_alpha(lse_ref[...])
        first = lax.broadcasted_iota(jnp.int32, (tm, 128), 1) < HEAD_DIM
        tot = jnp.zeros((tm, 128), F32)
        for p in range(3):
            sl = slice(p * 128, (p + 1) * 128)
            dy = dmix[:, sl]
            do_ref[:, sl] = dy * al[p]
            prod = dy * o_ref[:, sl]
            s0 = jnp.sum(jnp.where(first, prod, 0.0), axis=-1, keepdims=True)
            s1 = jnp.sum(jnp.where(first, 0.0, prod), axis=-1, keepdims=True)
            tot = tot + al[p] * jnp.where(first, s0, s1)
        for p in range(3):
            dd_ref[:, p * 128:(p + 1) * 128] = -al[p] * tot
        dc_ref[...] = dmix[:, ATTN_W:ATTN_W + CONV_W]
        dl_ref[...] = dmix[:, ATTN_W + CONV_W:D_MODEL]
        part = _dot_tn(mix_ref[...], dxb)

        @pl.when(i == 0)
        def _():
            acc_ref[...] = part

        @pl.when(i != 0)
        def _():
            acc_ref[...] += part

        @pl.when(i == nk - 1)
        def _():
            dw_ref[...] = acc_ref[...].astype(BF16)

    return pl.pallas_call(
        body, name="outproj_bwd", grid=(nk,),
        in_specs=[pl.BlockSpec((tm, D_MODEL), lambda i: (i, 0)),
                  pl.BlockSpec((tm, D_MODEL), lambda i: (i, 0)),
                  _resident((D_MODEL, D_MODEL)),
                  pl.BlockSpec((tm, ATTN_W), lambda i: (i, 0)),
                  pl.BlockSpec((tm, ATTN_W), lambda i: (i, 0))],
        out_specs=[pl.BlockSpec((tm, ATTN_W), lambda i: (i, 0)),
                   pl.BlockSpec((tm, ATTN_W), lambda i: (i, 0)),
                   pl.BlockSpec((tm, CONV_W), lambda i: (i, 0)),
                   pl.BlockSpec((tm, LRU_W), lambda i: (i, 0)),
                   _resident((D_MODEL, D_MODEL))],
        out_shape=[jax.ShapeDtypeStruct((T, ATTN_W), F32), jax.ShapeDtypeStruct((T, ATTN_W), F32),
                   jax.ShapeDtypeStruct((T, CONV_W), F32), jax.ShapeDtypeStruct((T, LRU_W), F32),
                   jax.ShapeDtypeStruct((D_MODEL, D_MODEL), BF16)],
        scratch_shapes=[pltpu.VMEM((D_MODEL, D_MODEL), F32)],
        compiler_params=_params("arbitrary"),
    )(dx1, mix, w, o, lse)


DZ_COLS = ((0, ATTN_W), (ATTN_W, 2 * ATTN_W), (2 * ATTN_W, QKV_W), (QKV_W, QKV_W + CONV_IN_W), (QKV_W + CONV_IN_W, IN_COLS))


def _inproj_bwd_w(dz_parts, h):
    T = h.shape[0]
    tm = GRAD_ROW_TILE
    nk = T // tm
    n_parts = len(DZ_COLS)

    def body(*refs):
        dz_refs = refs[:n_parts]
        h_ref, dw_ref, acc_ref = refs[n_parts:]
        i = pl.program_id(0)
        hb = h_ref[...]

        @pl.when(i == 0)
        def _():
            acc_ref[...] = jnp.zeros_like(acc_ref)

        for r, (lo, hi) in zip(dz_refs, DZ_COLS):
            acc_ref[:, lo:hi] += _dot_tn(hb, r[...].astype(BF16))

        @pl.when(i == nk - 1)
        def _():
            dw_ref[...] = acc_ref[...].astype(BF16)

    rows = lambda width: pl.BlockSpec((tm, width), lambda i: (i, 0))
    return pl.pallas_call(
        body, name="inproj_bwd_w", grid=(nk,),
        in_specs=[rows(hi - lo) for lo, hi in DZ_COLS] + [rows(D_MODEL)],
        out_specs=_resident((D_MODEL, IN_COLS)),
        out_shape=jax.ShapeDtypeStruct((D_MODEL, IN_COLS), BF16),
        scratch_shapes=[pltpu.VMEM((D_MODEL, IN_COLS), F32)],
        compiler_params=_params("arbitrary"),
    )(*dz_parts, h)


def _inproj_bwd_act(dz_parts, w, x2d, g, dx1, ex=None):
    T = x2d.shape[0]
    tm = ROW_TILE
    n_parts = len(DZ_COLS)

    def body(*refs):
        dz_refs = refs[:n_parts]
        w_ref, x_ref, g_ref, dx1_ref, dx_ref, dg_ref = refs[n_parts:]
        dh = _dot_nt(dz_refs[0][...].astype(BF16), w_ref[:, DZ_COLS[0][0]:DZ_COLS[0][1]])
        for r, (lo, hi) in zip(dz_refs[1:], DZ_COLS[1:]):
            dh = dh + _dot_nt(r[...].astype(BF16), w_ref[:, lo:hi])
        gv = g_ref[...]
        _, xhat, rstd = _rms_fwd(x_ref[...], gv)
        dx, dg = _rms_bwd(dh, xhat, rstd, gv)
        dx_ref[...] = dx1_ref[...] + dx

        @pl.when(pl.program_id(0) == 0)
        def _():
            dg_ref[...] = dg

        @pl.when(pl.program_id(0) != 0)
        def _():
            dg_ref[...] += dg

    rows = lambda width: pl.BlockSpec((tm, width), lambda i: (i, 0))
    return _pallas_hosting(
        body, ex, name="inproj_bwd_act", grid=(T // tm,),
        in_specs=[rows(hi - lo) for lo, hi in DZ_COLS] + [
            _resident((D_MODEL, IN_COLS)), rows(D_MODEL), pl.BlockSpec((1, D_MODEL), lambda i: (0, 0)), rows(D_MODEL)],
        out_specs=[rows(D_MODEL), pl.BlockSpec((1, D_MODEL), lambda i: (0, 0))],
        out_shape=[jax.ShapeDtypeStruct((T, D_MODEL), F32), jax.ShapeDtypeStruct((1, D_MODEL), F32)],
        scratch_shapes=[], semantics=("arbitrary",), operands=(*dz_parts, w, x2d, g, dx1))


def _alibi_coef():
    slopes = 2.0 ** (-8.0 * np.arange(1, 7) / 6)
    return jnp.asarray((slopes.reshape(3, 2) * np.asarray(ATTN_DILATIONS)[:, None]).astype(np.float32))


def _unit_rows(u, d):
    nb = N_UNITS // d
    r, n = u // nb, u % nb
    span = ATTN_BLOCK * d

    def rows(block):
        start = block * span + r
        return pl.ds(pl.multiple_of(start, ATTN_BLOCK), ATTN_BLOCK) if d == 1 else pl.ds(start, ATTN_BLOCK, stride=d)

    return rows(n), rows(jnp.maximum(n - 1, 0)), rows(jnp.minimum(n + 1, nb - 1)), n > 0, n + 1 < nb


def _per_pattern(fn):
    for p, d in enumerate(ATTN_DILATIONS):
        pl.when(pl.program_id(1) == p)(functools.partial(fn, p, d))


def _attn_col(offset):
    return pl.BlockSpec((None, SEQ_LEN, 128), lambda b, p: (b, 0, p + offset))


def _attn_masks():
    qi = lax.broadcasted_iota(jnp.int32, (ATTN_BLOCK, 2 * ATTN_BLOCK), 0)
    kj = lax.broadcasted_iota(jnp.int32, (ATTN_BLOCK, 2 * ATTN_BLOCK), 1)
    dist = qi + ATTN_BLOCK - kj
    first = lax.broadcasted_iota(jnp.int32, (ATTN_BLOCK, 128), 1) < HEAD_DIM
    return dist.astype(F32), (dist >= 0) & (dist <= ATTN_BLOCK), kj >= ATTN_BLOCK, first


def _head_lanes(a, first, j):
    return jnp.where(first if j == 0 else jnp.logical_not(first), a, jnp.zeros_like(a))


def _load_kv(ref, prev, own):
    return jnp.concatenate([ref[prev, :], ref[own, :]], axis=0).astype(BF16)


def _attn_fwd(qkv, ex=None):
    B = qkv.shape[0]

    def body(coef_ref, q_ref, k_ref, v_ref, o_ref, lse_ref):
        dist, band, own_half, first = _attn_masks()

        def pattern(p, d):
            def unit(u, carry):
                own, prev, _, has_prev, _ = _unit_rows(u, d)
                ok = band & jnp.logical_or(own_half, has_prev)
                q = q_ref[own, :].astype(BF16)
                kcat, vcat = _load_kv(k_ref, prev, own), _load_kv(v_ref, prev, own)
                outs, lses = [], []
                for j in range(2):
                    s = jnp.where(ok, _dot_nt(_head_lanes(q, first, j), kcat) * 0.125 - coef_ref[p, j] * dist, NEG_BIG)
                    m = jnp.max(s, axis=-1, keepdims=True)
                    e = jnp.exp(s - m)
                    l = jnp.sum(e, axis=-1, keepdims=True)
                    outs.append(_dot(e.astype(BF16), vcat) * (1.0 / l))
                    lses.append(m + jnp.log(l))
                o_ref[own, :] = jnp.where(first, outs[0], outs[1])
                lse_ref[own, :] = jnp.where(first, lses[0], lses[1])
                return carry

            lax.fori_loop(0, N_UNITS, unit, 0, unroll=UNIT_UNROLL)

        _per_pattern(pattern)

    shape = jax.ShapeDtypeStruct((B, SEQ_LEN, ATTN_W), F32)
    return _pallas_hosting(
        body, ex, name="attn_fwd", grid=(B, 3),
        in_specs=[pl.BlockSpec(memory_space=pltpu.SMEM), _attn_col(0), _attn_col(3), _attn_col(6)],
        out_specs=[_attn_col(0), _attn_col(0)],
        out_shape=[shape, shape],
        scratch_shapes=[], semantics=("parallel", "parallel"), operands=(_alibi_coef(), qkv, qkv, qkv))


def _attn_bwd(qkv, do, lse, dd, ex=None):
    B = qkv.shape[0]

    def body(coef_ref, q_ref, k_ref, v_ref, do_ref, lse_ref, dd_ref, dq_ref, dk_ref, dv_ref):
        dist, band, own_half, first = _attn_masks()

        def pattern(p, d):
            def unit(u, carry):
                own, prev, _, has_prev, _ = _unit_rows(u, d)
                ok = band & jnp.logical_or(own_half, has_prev)
                q, do = q_ref[own, :].astype(BF16), do_ref[own, :].astype(BF16)
                kcat, vcat = _load_kv(k_ref, prev, own), _load_kv(v_ref, prev, own)
                lse_a, dd_a = lse_ref[own, :], dd_ref[own, :]
                dqs, dks, dvs = [], [], []
                for j in range(2):
                    col = slice(HEAD_DIM * j, HEAD_DIM * j + 1)
                    s = _dot_nt(_head_lanes(q, first, j), kcat) * 0.125 - coef_ref[p, j] * dist
                    pr = jnp.where(ok, jnp.exp(jnp.where(ok, s, NEG_BIG) - lse_a[:, col]), 0.0)
                    ds = (pr * (_dot_nt(_head_lanes(do, first, j), vcat) + dd_a[:, col])).astype(BF16)
                    dqs.append(_dot(ds, kcat))
                    dks.append(_dot_tn(ds, q))
                    dvs.append(_dot_tn(pr.astype(BF16), do))
                both = lambda pair: jnp.where(jnp.concatenate([first] * (pair[0].shape[0] // ATTN_BLOCK), axis=0), *pair)
                dq_ref[own, :] = both(dqs) * 0.125
                dk, dv = both(dks) * 0.125, both(dvs)
                dk_ref[own, :] = dk[ATTN_BLOCK:]
                dv_ref[own, :] = dv[ATTN_BLOCK:]
                dk_ref[prev, :] += dk[:ATTN_BLOCK]
                dv_ref[prev, :] += dv[:ATTN_BLOCK]
                return carry

            lax.fori_loop(0, N_UNITS, unit, 0, unroll=UNIT_UNROLL)

        _per_pattern(pattern)

    shape = jax.ShapeDtypeStruct((B, SEQ_LEN, ATTN_W), F32)
    return _pallas_hosting(
        body, ex, name="attn_bwd", grid=(B, 3),
        in_specs=[pl.BlockSpec(memory_space=pltpu.SMEM), _attn_col(0), _attn_col(3), _attn_col(6), _attn_col(0), _attn_col(0),
                  _attn_col(0)],
        out_specs=[_attn_col(0)] * 3,
        out_shape=[shape] * 3,
        scratch_shapes=[], semantics=("parallel", "parallel"), operands=(_alibi_coef(), qkv, qkv, qkv, do, lse, dd))


def _for_chunks(n_rows, fn, chunk):
    def step(c, carry):
        fn(pl.multiple_of(c * chunk, chunk))
        return carry

    lax.fori_loop(0, n_rows // chunk, step, 0)


def _shift_down(win, s, rows):
    lead = win.shape[0] - rows
    if s == 0:
        return win[lead:]
    if s % 8 == 0:
        return win[lead - s:lead - s + rows]
    q, r = divmod(s, 8)
    rolled = pltpu.roll(win, r, 0)
    return rolled[lead - 8 * q:lead - 8 * q + rows]


def _tap_shifts(n_taps):
    return [(r, [(n_taps - 1 - (8 * q + r), 8 * q) for q in range((n_taps - 1 - r) // 8 + 1)]) for r in range(min(8, n_taps))]


def _rotated_down(win, r):
    return win if r == 0 else pltpu.roll(win, r, 0)


def _rotated_up(win, r):
    return win if r == 0 else pltpu.roll(win, win.shape[0] - r, 0)


def _shift_up(win, s, rows):
    if s % 8 == 0:
        return win[s:s + rows]
    q, r = divmod(s, 8)
    rolled = pltpu.roll(win, win.shape[0] - r, 0)
    return rolled[8 * q:8 * q + rows]


CONV_PAD = 32


def _ln_silu(c, lg, lb):
    mu = jnp.mean(c, axis=-1, keepdims=True)
    cc = c - mu
    rstd = lax.rsqrt(jnp.mean(cc * cc, axis=-1, keepdims=True) + LN_EPS)
    nrm = cc * rstd
    v = nrm * lg + lb
    sg = _sigmoid(v)
    return v * sg, nrm, rstd, v, sg


def _conv_fwd(ci, w, b, lg, lb):
    B, S, _ = ci.shape
    CH = CONV_FWD_CHUNK

    def body(ci_ref, w_ref, b_ref, lg_ref, lb_ref, y_ref, c_ref, pad_ref):
        pad_ref[0:CONV_PAD, :] = jnp.zeros((CONV_PAD, CONV_W), F32)

        def glu(base):
            blk = ci_ref[pl.ds(base, CH), :]
            pad_ref[pl.ds(CONV_PAD + base, CH), :] = blk[:, 0:CONV_W] * _sigmoid(blk[:, CONV_W:])

        _for_chunks(S, glu, CH)

        def conv(base):
            win = pad_ref[pl.ds(base, CH + CONV_PAD), :]
            acc = jnp.broadcast_to(b_ref[...], (CH, CONV_W))
            for r, taps in _tap_shifts(CONV_TAPS):
                rot = _rotated_down(win, r)
                for k, off in taps:
                    acc = acc + w_ref[k:k + 1, :] * rot[CONV_PAD - off:CONV_PAD - off + CH]
            c_ref[pl.ds(base, CH), :] = acc
            y, _, _, _, _ = _ln_silu(acc, lg_ref[...], lb_ref[...])
            y_ref[pl.ds(base, CH), :] = y.astype(BF16)

        _for_chunks(S, conv, CH)

    vec = pl.BlockSpec((1, CONV_W), lambda i: (0, 0))
    return pl.pallas_call(
        body, name="conv_fwd", grid=(B,),
        in_specs=[pl.BlockSpec((None, S, CONV_IN_W), lambda i: (i, 0, 0)),
                  pl.BlockSpec((CONV_TAPS, CONV_W), lambda i: (0, 0)), vec, vec, vec],
        out_specs=[pl.BlockSpec((None, S, CONV_W), lambda i: (i, 0, 0)),
                   pl.BlockSpec((None, S, CONV_W), lambda i: (i, 0, 0))],
        out_shape=[jax.ShapeDtypeStruct((B, S, CONV_W), BF16), jax.ShapeDtypeStruct((B, S, CONV_W), F32)],
        scratch_shapes=[pltpu.VMEM((S + CONV_PAD, CONV_W), F32)],
        compiler_params=_params("parallel"),
    )(ci, w, b, lg, lb)


def _conv_bwd(ci, cpre, dy, w, lg, lb, ex=None):
    B, S, _ = ci.shape
    CH = CONV_BWD_CHUNK

    def body(ci_ref, c_ref, dy_ref, w_ref, lg_ref, lb_ref, dci_ref, dw_ref, db_ref, dlg_ref, dlb_ref, upad_ref, dcpad_ref,
             dwacc_ref):
        @pl.when(pl.program_id(0) == 0)
        def _():
            dw_ref[...] = jnp.zeros_like(dw_ref)
            db_ref[...] = jnp.zeros_like(db_ref)
            dlg_ref[...] = jnp.zeros_like(dlg_ref)
            dlb_ref[...] = jnp.zeros_like(dlb_ref)

        upad_ref[0:CONV_PAD, :] = jnp.zeros((CONV_PAD, CONV_W), F32)
        dcpad_ref[S:S + CONV_PAD, :] = jnp.zeros((CONV_PAD, CONV_W), F32)
        dwacc_ref[...] = jnp.zeros_like(dwacc_ref)

        def norm_bwd(base):
            blk = ci_ref[pl.ds(base, CH), :]
            upad_ref[pl.ds(CONV_PAD + base, CH), :] = blk[:, 0:CONV_W] * _sigmoid(blk[:, CONV_W:])
            lgv = lg_ref[...]
            _, nrm, rstd, v, sg = _ln_silu(c_ref[pl.ds(base, CH), :], lgv, lb_ref[...])
            dv = dy_ref[pl.ds(base, CH), :] * (sg * (1.0 + v * (1.0 - sg)))
            dlg_ref[...] += jnp.sum(dv * nrm, axis=0, keepdims=True)
            dlb_ref[...] += jnp.sum(dv, axis=0, keepdims=True)
            dn = dv * lgv
            dc = rstd * (dn - jnp.mean(dn, axis=-1, keepdims=True) - nrm * jnp.mean(dn * nrm, axis=-1, keepdims=True))
            dcpad_ref[pl.ds(base, CH), :] = dc
            db_ref[...] += jnp.sum(dc, axis=0, keepdims=True)

        _for_chunks(S, norm_bwd, CH)

        def conv_bwd(base):
            dwin = dcpad_ref[pl.ds(base, CH + CONV_PAD), :]
            uwin = upad_ref[pl.ds(base, CH + CONV_PAD), :]
            dc = dwin[0:CH]
            du = jnp.zeros((CH, CONV_W), F32)
            for r, taps in _tap_shifts(CONV_TAPS):
                d_rot, u_rot = _rotated_up(dwin, r), _rotated_down(uwin, r)
                for k, off in taps:
                    du = du + w_ref[k:k + 1, :] * d_rot[off:off + CH]
                    prod = dc * u_rot[CONV_PAD - off:CONV_PAD - off + CH]
                    dwacc_ref[8 * k:8 * k + 8, :] += jnp.sum(prod.reshape(CH // 8, 8, CONV_W), axis=0)
            blk = ci_ref[pl.ds(base, CH), :]
            a, sg = blk[:, 0:CONV_W], _sigmoid(blk[:, CONV_W:])
            dci_ref[pl.ds(base, CH), 0:CONV_W] = (du * sg).astype(BF16)
            dci_ref[pl.ds(base, CH), CONV_W:] = (du * a * sg * (1.0 - sg)).astype(BF16)

        _for_chunks(S, conv_bwd, CH)
        for k in range(CONV_TAPS):
            dw_ref[k:k + 1, :] += jnp.sum(dwacc_ref[8 * k:8 * k + 8, :], axis=0, keepdims=True)

    vec = pl.BlockSpec((1, CONV_W), lambda i: (0, 0))
    mat = pl.BlockSpec((CONV_TAPS, CONV_W), lambda i: (0, 0))
    seq = lambda width: pl.BlockSpec((None, S, width), lambda i: (i, 0, 0))
    return _pallas_hosting(
        body, ex, name="conv_bwd", grid=(B,),
        in_specs=[seq(CONV_IN_W), seq(CONV_W), seq(CONV_W), mat, vec, vec],
        out_specs=[seq(CONV_IN_W), mat, vec, vec, vec],
        out_shape=[jax.ShapeDtypeStruct((B, S, CONV_IN_W), BF16), jax.ShapeDtypeStruct((CONV_TAPS, CONV_W), F32),
                   jax.ShapeDtypeStruct((1, CONV_W), F32), jax.ShapeDtypeStruct((1, CONV_W), F32),
                   jax.ShapeDtypeStruct((1, CONV_W), F32)],
        scratch_shapes=[pltpu.VMEM((S + CONV_PAD, CONV_W), F32), pltpu.VMEM((S + CONV_PAD, CONV_W), F32),
                        pltpu.VMEM((8 * CONV_TAPS, CONV_W), F32)],
        semantics=("arbitrary",), operands=(ci, cpre, dy, w, lg, lb))


SCAN_SHIFTS = tuple(1 << e for e in range(11))


def _prev8(ref, base, cols, fill):
    start = pl.multiple_of(jnp.maximum(base - 8, 0), 8)
    return jnp.where(base > 0, ref[pl.ds(start, 8), cols], fill)


def _next8(ref, base, rows, total, cols, fill):
    start = pl.multiple_of(jnp.minimum(base + rows, total - 8), 8)
    return jnp.where(base + rows < total, ref[pl.ds(start, 8), cols], fill)


ALL = slice(None)
LRU_X = slice(LRU_W, LRU_IN_W)
LRU_GATE = slice(0, LRU_W)


def _lru_conv(li_ref, base, rows, cw_ref, cb_ref):
    win = jnp.concatenate([_prev8(li_ref, base, LRU_X, 0.0), li_ref[pl.ds(base, rows), LRU_X]], axis=0)
    u = jnp.broadcast_to(cb_ref[...], (rows, LRU_W))
    for k in range(LRU_TAPS):
        u = u + cw_ref[k:k + 1, :] * _shift_down(win, LRU_TAPS - 1 - k, rows)
    return u, win


def _lru_gates(u, wa_ref, ba_ref, wx_ref, bx_ref, sp):
    ub = u.astype(BF16)
    r = _sigmoid(_dot(ub, wa_ref[...]) + ba_ref[...])
    i = _sigmoid(_dot(ub, wx_ref[...]) + bx_ref[...])
    la = (-LRU_C) * r * sp
    a = jnp.exp(la)
    return ub, r, i, a, _one_minus_exp(2.0 * la, a * a)


def _scan_forward(bufs, S, CH):
    for n, s in enumerate(SCAN_SHIFTS):
        (sa, sb), (da, db) = bufs[n % 2], bufs[(n + 1) % 2]

        def step(base, s=s, sa=sa, sb=sb, da=da, db=db):
            a, b = sa[pl.ds(base, CH), :], sb[pl.ds(base, CH), :]
            if s < 8:
                a_s = _shift_down(jnp.concatenate([_prev8(sa, base, ALL, 1.0), a], axis=0), s, CH)
                b_s = _shift_down(jnp.concatenate([_prev8(sb, base, ALL, 0.0), b], axis=0), s, CH)
            elif s < CH:
                start = pl.multiple_of(jnp.maximum(base - s, 0), 8)
                a_s = jnp.concatenate([jnp.where(base > 0, sa[pl.ds(start, s), :], 1.0), a[0:CH - s]], axis=0)
                b_s = jnp.concatenate([jnp.where(base > 0, sb[pl.ds(start, s), :], 0.0), b[0:CH - s]], axis=0)
            else:
                start = pl.multiple_of(jnp.maximum(base - s, 0), 8)
                a_s = jnp.where(base < s, 1.0, sa[pl.ds(start, CH), :])
                b_s = jnp.where(base < s, 0.0, sb[pl.ds(start, CH), :])
            db[pl.ds(base, CH), :] = a * b_s + b
            da[pl.ds(base, CH), :] = a * a_s

        _for_chunks(S, step, CH)
    return len(SCAN_SHIFTS) % 2


def _scan_backward(bufs, S, CH):
    for n, s in enumerate(SCAN_SHIFTS):
        (sa, sb), (da, db) = bufs[n % 2], bufs[(n + 1) % 2]

        def step(base, s=s, sa=sa, sb=sb, da=da, db=db):
            a, b = sa[pl.ds(base, CH), :], sb[pl.ds(base, CH), :]
            if s < 8:
                a_s = _shift_up(jnp.concatenate([a, _next8(sa, base, CH, S, ALL, 1.0)], axis=0), s, CH)
                b_s = _shift_up(jnp.concatenate([b, _next8(sb, base, CH, S, ALL, 0.0)], axis=0), s, CH)
            elif s < CH:
                start = pl.multiple_of(jnp.minimum(base + CH, S - s), 8)
                more = base + CH < S
                a_s = jnp.concatenate([a[s:CH], jnp.where(more, sa[pl.ds(start, s), :], 1.0)], axis=0)
                b_s = jnp.concatenate([b[s:CH], jnp.where(more, sb[pl.ds(start, s), :], 0.0)], axis=0)
            else:
                start = pl.multiple_of(jnp.minimum(base + s, S - CH), 8)
                a_s = jnp.where(base + s >= S, 1.0, sa[pl.ds(start, CH), :])
                b_s = jnp.where(base + s >= S, 0.0, sb[pl.ds(start, CH), :])
            db[pl.ds(base, CH), :] = a * b_s + b
            da[pl.ds(base, CH), :] = a * a_s

        _for_chunks(S, step, CH)
    return len(SCAN_SHIFTS) % 2


def _lru_fwd(li, cw, cb, wa, ba, wx, bx, lam, ex=None):
    B, S, _ = li.shape
    CH = LRU_FWD_CHUNK

    def body(li_ref, cw_ref, cb_ref, wa_ref, ba_ref, wx_ref, bx_ref, lam_ref, y_ref, h_ref, a0, b0, a1, b1):
        sp = _softplus(-lam_ref[...])

        def gates(base):
            u, _ = _lru_conv(li_ref, base, CH, cw_ref, cb_ref)
            _, _, i, a, em = _lru_gates(u, wa_ref, ba_ref, wx_ref, bx_ref, sp)
            a0[pl.ds(base, CH), :] = a
            b0[pl.ds(base, CH), :] = jnp.sqrt(em) * (i * u)

        _for_chunks(S, gates, CH)
        bufs = ((a0, b0), (a1, b1))
        hb = bufs[_scan_forward(bufs, S, CH)][1]

        def out(base):
            h = hb[pl.ds(base, CH), :]
            h_ref[pl.ds(base, CH), :] = h
            gl, _ = _gelu(li_ref[pl.ds(base, CH), LRU_GATE])
            y_ref[pl.ds(base, CH), :] = (gl * h).astype(BF16)

        _for_chunks(S, out, CH)

    vec = pl.BlockSpec((1, LRU_W), lambda i: (0, 0))
    mat = pl.BlockSpec((LRU_W, LRU_W), lambda i: (0, 0))
    seq = lambda width: pl.BlockSpec((None, S, width), lambda i: (i, 0, 0))
    return _pallas_hosting(
        body, ex, name="lru_fwd", grid=(B,),
        in_specs=[seq(LRU_IN_W), pl.BlockSpec((LRU_TAPS, LRU_W), lambda i: (0, 0)), vec, mat, vec, mat, vec, vec],
        out_specs=[seq(LRU_W), seq(LRU_W)],
        out_shape=[jax.ShapeDtypeStruct((B, S, LRU_W), BF16), jax.ShapeDtypeStruct((B, S, LRU_W), F32)],
        scratch_shapes=[pltpu.VMEM((S, LRU_W), F32)] * 4,
        semantics=("parallel",), operands=(li, cw, cb, wa, ba, wx, bx, lam))


def _lru_bwd(li, hs, dy, cw, cb, wa, ba, wx, bx, lam, ex=None):
    B, S, _ = li.shape
    CH = LRU_BWD_CHUNK

    def body(li_ref, hs_ref, dy_ref, cw_ref, cb_ref, wa_ref, ba_ref, wx_ref, bx_ref, lam_ref,
             dli_ref, dcw_ref, dcb_ref, dwa_ref, dba_ref, dwx_ref, dbx_ref, dlam_ref, a0, b0, a1, b1, u_s, du_s):
        @pl.when(pl.program_id(0) == 0)
        def _():
            for ref in (dcw_ref, dcb_ref, dwa_ref, dba_ref, dwx_ref, dbx_ref, dlam_ref):
                ref[...] = jnp.zeros_like(ref)

        lam_v = lam_ref[...]
        sp = _softplus(-lam_v)
        dsp_dlam = -_sigmoid(-lam_v)

        def gates(base):
            u, _ = _lru_conv(li_ref, base, CH, cw_ref, cb_ref)
            _, _, _, a, _ = _lru_gates(u, wa_ref, ba_ref, wx_ref, bx_ref, sp)
            gl, _ = _gelu(li_ref[pl.ds(base, CH), LRU_GATE])
            u_s[pl.ds(base, CH), :] = u
            a0[pl.ds(base, CH), :] = a
            b0[pl.ds(base, CH), :] = a * (dy_ref[pl.ds(base, CH), :] * gl)

        _for_chunks(S, gates, CH)
        bufs = ((a0, b0), (a1, b1))
        eb = bufs[_scan_backward(bufs, S, CH)][1]

        def grads(base):
            e = eb[pl.ds(base, CH), :]
            e_next = _shift_up(jnp.concatenate([e, _next8(eb, base, CH, S, ALL, 0.0)], axis=0), 1, CH)
            gate = li_ref[pl.ds(base, CH), LRU_GATE]
            gl, th = _gelu(gate)
            dy = dy_ref[pl.ds(base, CH), :]
            g = dy * gl + e_next
            h = hs_ref[pl.ds(base, CH), :]
            h_prev = _shift_down(jnp.concatenate([_prev8(hs_ref, base, ALL, 0.0), h], axis=0), 1, CH)
            dli_ref[pl.ds(base, CH), LRU_GATE] = (dy * h * _gelu_grad(gate, th)).astype(BF16)
            u = u_s[pl.ds(base, CH), :]
            ub, r, i, a, em = _lru_gates(u, wa_ref, ba_ref, wx_ref, bx_ref, sp)
            mult = jnp.sqrt(em)
            da = g * h_prev
            dmult = g * (i * u)
            di = g * mult * u
            dla = da * a - dmult * (a * a) * lax.rsqrt(jnp.maximum(em, 1e-30))
            dlam_ref[...] += dsp_dlam * jnp.sum(dla * ((-LRU_C) * r), axis=0, keepdims=True)
            dpa = (dla * ((-LRU_C) * sp)) * r * (1.0 - r)
            dpx = di * i * (1.0 - i)
            dpab, dpxb = dpa.astype(BF16), dpx.astype(BF16)
            dwa_ref[...] += _dot_tn(ub, dpab)
            dwx_ref[...] += _dot_tn(ub, dpxb)
            dba_ref[...] += jnp.sum(dpa, axis=0, keepdims=True)
            dbx_ref[...] += jnp.sum(dpx, axis=0, keepdims=True)
            du = g * mult * i + _dot_nt(dpab, wa_ref[...]) + _dot_nt(dpxb, wx_ref[...])
            du_s[pl.ds(base, CH), :] = du
            dcb_ref[...] += jnp.sum(du, axis=0, keepdims=True)

        _for_chunks(S, grads, CH)

        def conv_bwd(base):
            du = du_s[pl.ds(base, CH), :]
            dwin = jnp.concatenate([du, _next8(du_s, base, CH, S, ALL, 0.0)], axis=0)
            xwin = jnp.concatenate([_prev8(li_ref, base, LRU_X, 0.0), li_ref[pl.ds(base, CH), LRU_X]], axis=0)
            dx = jnp.zeros((CH, LRU_W), F32)
            for k in range(LRU_TAPS):
                dx = dx + cw_ref[k:k + 1, :] * _shift_up(dwin, LRU_TAPS - 1 - k, CH)
                dcw_ref[k:k + 1, :] += jnp.sum(du * _shift_down(xwin, LRU_TAPS - 1 - k, CH), axis=0, keepdims=True)
            dli_ref[pl.ds(base, CH), LRU_X] = dx.astype(BF16)

        _for_chunks(S, conv_bwd, CH)

    vec = pl.BlockSpec((1, LRU_W), lambda i: (0, 0))
    mat = pl.BlockSpec((LRU_W, LRU_W), lambda i: (0, 0))
    taps = pl.BlockSpec((LRU_TAPS, LRU_W), lambda i: (0, 0))
    seq = lambda width: pl.BlockSpec((None, S, width), lambda i: (i, 0, 0))
    vec_shape = jax.ShapeDtypeStruct((1, LRU_W), F32)
    mat_shape = jax.ShapeDtypeStruct((LRU_W, LRU_W), F32)
    return _pallas_hosting(
        body, ex, name="lru_bwd", grid=(B,),
        in_specs=[seq(LRU_IN_W), seq(LRU_W), seq(LRU_W), taps, vec, mat, vec, mat, vec, vec],
        out_specs=[seq(LRU_IN_W), taps, vec, mat, vec, mat, vec, vec],
        out_shape=[jax.ShapeDtypeStruct((B, S, LRU_IN_W), BF16), jax.ShapeDtypeStruct((LRU_TAPS, LRU_W), F32),
                   vec_shape, mat_shape, vec_shape, mat_shape, vec_shape, vec_shape],
        scratch_shapes=[pltpu.VMEM((S, LRU_W), F32)] * 6,
        semantics=("arbitrary",), operands=(li, hs, dy, cw, cb, wa, ba, wx, bx, lam))


MESH = pl.DeviceIdType.MESH
HBM_SPEC = pl.BlockSpec(memory_space=pltpu.HBM)


def _slot(ref, p):
    return ref.at[p]


def _row_block(rows):
    return lambda ref, p: ref.at[pl.ds(p * rows, rows), :]


def _col_block(cols):
    return lambda ref, p: ref.at[:, pl.ds(p * cols, cols)]


class _Gather:
    def __init__(self, blocks, out_shapes, places):
        self.sources, self.out_shapes, self.places, self.n = list(blocks), list(out_shapes), list(places), len(blocks)

    def scratch(self):
        return [pltpu.SemaphoreType.DMA((self.n, 7)), pltpu.SemaphoreType.DMA((self.n, 7)), pltpu.SemaphoreType.DMA((self.n,))]

    def _plan(self, x_refs, out_refs, send_sems, recv_sems, local_sems):
        n = self.n
        x, y, c = lax.axis_index("x"), lax.axis_index("y"), lax.axis_index("c")
        me, sibling = (x, y, c), (x, y, 1 - c)
        chips = [(1 - x, y), (x, 1 - y), (1 - x, 1 - y)]

        def place(a, dev):
            return self.places[a](out_refs[a], 4 * dev[0] + 2 * dev[1] + dev[2])

        def copy(a, k, blk, to, src=None):
            return pltpu.make_async_remote_copy(
                src_ref=place(a, blk) if src is None else src, dst_ref=place(a, blk),
                send_sem=send_sems.at[a, k], recv_sem=recv_sems.at[a, k], device_id=to, device_id_type=MESH)

        mine = [pltpu.make_async_copy(x_refs[a], place(a, me), local_sems.at[a]) for a in range(n)]
        first = [copy(a, 0, me, sibling, src=x_refs[a]) for a in range(n)]
        first += [copy(a, 1 + j, me, (*chip, c), src=x_refs[a]) for j, chip in enumerate(chips) for a in range(n)]
        return me, sibling, chips, c, copy, mine, first

    def start(self, *refs):
        *_, mine, first = self._plan(*refs)
        for cp in mine + first:
            cp.start()

    def forward(self, *refs):
        me, sibling, chips, c, copy, _, _ = self._plan(*refs)
        for j, chip in enumerate(chips):
            for a in range(self.n):
                copy(a, 1 + j, (*chip, c), me).wait_recv()
                copy(a, 4 + j, (*chip, c), sibling).start()

    def finish(self, *refs):
        me, sibling, chips, c, copy, mine, first = self._plan(*refs)
        passed = [copy(a, 4 + j, (*chip, c), sibling) for j, chip in enumerate(chips) for a in range(self.n)]
        for a in range(self.n):
            copy(a, 0, sibling, me).wait_recv()
        for j, chip in enumerate(chips):
            for a in range(self.n):
                copy(a, 4 + j, (*chip, 1 - c), me).wait_recv()
        for cp in first + passed:
            cp.wait_send()
        for cp in mine:
            cp.wait()


class _GradExchange:
    def __init__(self, sources, takes, piece_shapes):
        self.sources, self.takes, self.n = list(sources), list(takes), len(sources)
        self.out_shapes = [jax.ShapeDtypeStruct((N_DEV,) + tuple(s), BF16) for s in piece_shapes]

    def scratch(self):
        return [pltpu.SemaphoreType.DMA((self.n, 7)), pltpu.SemaphoreType.DMA((self.n, 7)), pltpu.SemaphoreType.DMA((self.n,))]

    def _copies(self, src_refs, out_refs, send_sems, recv_sems, local_sems):
        x, y, c = lax.axis_index("x"), lax.axis_index("y"), lax.axis_index("c")
        me = 4 * x + 2 * y + c
        mine = [pltpu.make_async_copy(self.takes[i](src_refs[i], me), out_refs[i].at[me], local_sems.at[i]) for i in range(self.n)]
        remote = []
        for k in range(1, N_DEV):
            px, py, pc = x ^ ((k >> 2) & 1), y ^ ((k >> 1) & 1), c ^ (k & 1)
            peer = 4 * px + 2 * py + pc
            for i in range(self.n):
                remote.append(pltpu.make_async_remote_copy(
                    src_ref=self.takes[i](src_refs[i], peer), dst_ref=out_refs[i].at[me], send_sem=send_sems.at[i, k - 1],
                    recv_sem=recv_sems.at[i, k - 1], device_id=(px, py, pc), device_id_type=MESH))
        return mine, remote

    def start(self, *refs):
        mine, remote = self._copies(*refs)
        for cp in mine + remote:
            cp.start()

    def forward(self, *refs):
        pass

    def finish(self, *refs):
        mine, remote = self._copies(*refs)
        for cp in remote:
            cp.wait_recv()
        for cp in remote:
            cp.wait_send()
        for cp in mine:
            cp.wait()


def _run_exchange(ex, name):
    def body(*refs):
        src_refs, out_refs, sems = refs[:ex.n], refs[ex.n:2 * ex.n], refs[2 * ex.n:]
        ex.start(src_refs, out_refs, *sems)
        ex.forward(src_refs, out_refs, *sems)
        ex.finish(src_refs, out_refs, *sems)

    return pl.pallas_call(
        body, name=name, out_shape=ex.out_shapes, in_specs=[HBM_SPEC] * ex.n, out_specs=[HBM_SPEC] * ex.n,
        scratch_shapes=ex.scratch(),
    )(*ex.sources)


def _pallas_hosting(body, ex, *, name, grid, in_specs, out_specs, out_shape, scratch_shapes, semantics, operands):
    if ex is None:
        outs = pl.pallas_call(body, name=name, grid=grid, in_specs=in_specs, out_specs=out_specs, out_shape=out_shape,
                              scratch_shapes=scratch_shapes, compiler_params=_params(*semantics))(*operands)
        return outs, None
    n_in, n_out, n_scr, n = len(in_specs), len(out_specs), len(scratch_shapes), ex.n

    def at_step(pick):
        conds = [pl.program_id(k) == pick(size) for k, size in enumerate(grid)]
        return functools.reduce(jnp.logical_and, conds)

    def hosting(*refs):
        ins, ex_ins = refs[:n_in], refs[n_in:n_in + n]
        outs, ex_outs = refs[n_in + n:n_in + n + n_out], refs[n_in + n + n_out:n_in + 2 * n + n_out]
        scratch, sems = refs[n_in + 2 * n + n_out:n_in + 2 * n + n_out + n_scr], refs[n_in + 2 * n + n_out + n_scr:]
        pl.when(at_step(lambda size: 0))(lambda: ex.start(ex_ins, ex_outs, *sems))
        pl.when(at_step(lambda size: size - 1))(lambda: ex.forward(ex_ins, ex_outs, *sems))
        body(*ins, *outs, *scratch)
        pl.when(at_step(lambda size: size - 1))(lambda: ex.finish(ex_ins, ex_outs, *sems))

    res = pl.pallas_call(
        hosting, name=name, grid=grid, in_specs=list(in_specs) + [HBM_SPEC] * n, out_specs=list(out_specs) + [HBM_SPEC] * n,
        out_shape=list(out_shape) + ex.out_shapes, scratch_shapes=list(scratch_shapes) + ex.scratch(),
        compiler_params=_params(*(["arbitrary"] * len(grid))),
    )(*operands, *ex.sources)
    return res[:n_out], res[n_out:]


def _adamw(w, g, m, v, rows_per_step, name):
    R, C = w.shape
    c1 = 1.0 - ADAM_B1 ** ADAM_STEP
    c2 = 1.0 - ADAM_B2 ** ADAM_STEP

    def body(w_ref, g_ref, m_ref, v_ref, d_ref, nm_ref, nv_ref):
        gv = g_ref[...]
        nm = ADAM_B1 * m_ref[...] + (1.0 - ADAM_B1) * gv
        nv = ADAM_B2 * v_ref[...] + (1.0 - ADAM_B2) * (gv * gv)
        nm_ref[...] = nm
        nv_ref[...] = nv
        d_ref[...] = (-ADAM_LR) * ((nm / c1) / (jnp.sqrt(nv / c2) + ADAM_EPS) + ADAM_WD * w_ref[...])

    spec = pl.BlockSpec((rows_per_step, C), lambda i: (i, 0))
    shape = jax.ShapeDtypeStruct((R, C), F32)
    return pl.pallas_call(
        body, name=name, grid=(R // rows_per_step,),
        in_specs=[spec] * 4, out_specs=[spec] * 3, out_shape=[shape] * 3,
        compiler_params=_params("parallel"),
    )(w, g, m, v)


def _adam_update(w, g, m, v):
    nm = ADAM_B1 * m + (1.0 - ADAM_B1) * g
    nv = ADAM_B2 * v + (1.0 - ADAM_B2) * (g * g)
    c1 = 1.0 - ADAM_B1 ** ADAM_STEP
    c2 = 1.0 - ADAM_B2 ** ADAM_STEP
    return (-ADAM_LR) * ((nm / c1) / (jnp.sqrt(nv / c2) + ADAM_EPS) + ADAM_WD * w), nm, nv


def _small_sum_adamw(gathered, vectors, vector_widths, loss_slots, state):
    names, updated = list(vector_widths) + list(gathered), list(state)
    n_in, n_g, n_u = len(gathered) + 1, len(names), len(updated)

    def slot_sum(ref):
        acc = ref[0]
        for q in range(1, N_DEV):
            acc = acc + ref[q]
        return acc

    def body(*refs):
        p_refs, loss_in = refs[:n_in], refs[n_in]
        st = refs[n_in + 1:n_in + 1 + 3 * n_u]
        outs = refs[n_in + 1 + 3 * n_u:]
        g_refs, loss_out, upd = outs[:n_g], outs[n_g], outs[n_g + 1:]
        loss_out[...] = slot_sum(loss_in)
        side_by_side, off, sums = slot_sum(p_refs[0]), 0, {}
        for n, width in vector_widths.items():
            sums[n] = side_by_side[:, off:off + width]
            off += width
        for i, n in enumerate(gathered):
            sums[n] = slot_sum(p_refs[1 + i])
        for i, n in enumerate(names):
            g_refs[i][...] = sums[n]
            if n in state:
                j = updated.index(n)
                d, nm, nv = _adam_update(st[3 * j][...], sums[n], st[3 * j + 1][...], st[3 * j + 2][...])
                upd[3 * j][...], upd[3 * j + 1][...], upd[3 * j + 2][...] = d, nm, nv

    g_shapes = [jax.ShapeDtypeStruct((DEPTH, w), F32) for w in vector_widths.values()]
    g_shapes += [jax.ShapeDtypeStruct(gathered[n].shape[1:], F32) for n in gathered]
    u_shapes = [jax.ShapeDtypeStruct(state[n][0].shape, F32) for n in updated for _ in range(3)]
    outs = pl.pallas_call(
        body, name="small_sum_adamw", out_shape=g_shapes + [jax.ShapeDtypeStruct((8, 128), F32)] + u_shapes,
        compiler_params=_params(),
    )(vectors, *gathered.values(), loss_slots, *[a for n in updated for a in state[n]])
    g = dict(zip(names, outs[:n_g]))
    upd = {n: tuple(outs[n_g + 1 + 3 * j:n_g + 4 + 3 * j]) for j, n in enumerate(updated)}
    return g, outs[n_g], upd


WEIGHT_ORDER = ("norm1_g", "w_in", "conv_dw_w", "conv_dw_b", "conv_ln_g", "conv_ln_b", "lru_conv_w", "lru_conv_b", "lru_wa",
                "lru_ba", "lru_wx", "lru_bx", "lru_lambda", "w_out", "norm2_g", "w_up", "w_down", "final_g")
BIG = ("w_in", "w_out", "w_up", "w_down")
SMALL_SHARDED = {"conv_dw_w": (DEPTH, CONV_TAPS, CONV_W), "lru_conv_w": (DEPTH, LRU_TAPS, LRU_W)}
SMALL_FULL = {
    "norm1_g": (DEPTH, D_MODEL), "conv_dw_w": (DEPTH, CONV_TAPS, CONV_W), "conv_dw_b": (DEPTH, CONV_W),
    "conv_ln_g": (DEPTH, CONV_W), "conv_ln_b": (DEPTH, CONV_W), "lru_conv_w": (DEPTH, LRU_TAPS, LRU_W),
    "lru_conv_b": (DEPTH, LRU_W), "lru_wa": (DEPTH, LRU_HEADS, HEAD_DIM, HEAD_DIM), "lru_ba": (DEPTH, LRU_W),
    "lru_wx": (DEPTH, LRU_HEADS, HEAD_DIM, HEAD_DIM), "lru_bx": (DEPTH, LRU_W), "lru_lambda": (DEPTH, LRU_W),
    "norm2_g": (DEPTH, D_MODEL), "final_g": (D_MODEL,),
}
SMALL_COLS = 128
FILTER_ROWS = 24
W_IN_SHARD = IN_COLS // N_DEV
W_OUT_SHARD = D_MODEL // N_DEV
FF_SHARD = D_FF // N_DEV


def _pack_rows(flat_parts, cols, rows):
    flat = jnp.concatenate(flat_parts)
    return jnp.pad(flat, (0, rows * cols - flat.shape[0])).reshape(rows, cols)


WEIGHT_GATHER = {
    "w_in": ((N_DEV, D_MODEL, W_IN_SHARD), _slot),
    "w_out": ((D_MODEL, D_MODEL), _row_block(W_OUT_SHARD)),
    "w_up": ((D_MODEL, D_FF), _col_block(FF_SHARD)),
    "w_down": ((D_FF, D_MODEL), _row_block(FF_SHARD)),
}
GATHER_HOSTS = {
    "inproj": (("w_up", 0),), "attn_fwd": (("w_out", 0), ("w_in", 1)), "lru_fwd": (("w_down", 0),),
    "outproj": (), "up": (("w_up", 1), ("w_out", 1)), "down": (("w_down", 1),),
}


def _weight_gather(local, items, with_filters=False):
    blocks = [local[n][l].astype(BF16) for n, l in items]
    shapes = [jax.ShapeDtypeStruct(WEIGHT_GATHER[n][0], BF16) for n, _ in items]
    places = [WEIGHT_GATHER[n][1] for n, _ in items]
    if with_filters:
        blocks.append(_pack_rows([local[n].reshape(-1) for n in SMALL_SHARDED], SMALL_COLS, FILTER_ROWS))
        shapes.append(jax.ShapeDtypeStruct((N_DEV, FILTER_ROWS, SMALL_COLS), F32))
        places.append(_slot)
    return _Gather(blocks, shapes, places)


def _keep_gathered(full, items, landed):
    for (n, l), arr in zip(items, landed):
        full[n][l] = arr.transpose(1, 0, 2).reshape(D_MODEL, IN_COLS) if n == "w_in" else arr


def _unpack_filters(slots):
    flat, off, out = slots.reshape(N_DEV, -1), 0, {}
    for n, shp in SMALL_SHARDED.items():
        shard = shp[:-1] + (shp[-1] // N_DEV,)
        size = int(np.prod(shard))
        out[n] = jnp.moveaxis(flat[:, off:off + size].reshape((N_DEV,) + shard), 0, -2).reshape(shp)
        off += size
    return out


def _grad_exchange(name, dw):
    if name == "w_in":
        return _GradExchange([dw.reshape(D_MODEL, N_DEV, W_IN_SHARD).transpose(1, 0, 2)], [_slot], [(D_MODEL, W_IN_SHARD)])
    if name == "w_out":
        return _GradExchange([dw], [_row_block(W_OUT_SHARD)], [(W_OUT_SHARD, D_MODEL)])
    return _GradExchange([dw], [_col_block(FF_SHARD)], [(D_MODEL, FF_SHARD)])


def _sum_adamw(parts, w, m, v, rows_per_step, transposed, name):
    _, R, C = parts[0].shape
    tr = rows_per_step
    steps = R // tr
    c1 = 1.0 - ADAM_B1 ** ADAM_STEP
    c2 = 1.0 - ADAM_B2 ** ADAM_STEP

    def slot_sum(p_ref):
        acc = p_ref[0].astype(F32)
        for q in range(1, N_DEV):
            acc = acc + p_ref[q].astype(F32)
        return acc

    def body(p0_ref, p1_ref, w_ref, m_ref, v_ref, g_ref, d_ref, nm_ref, nv_ref):
        gv = jnp.where(pl.program_id(0) == 0, slot_sum(p0_ref), slot_sum(p1_ref))
        if transposed:
            gv = gv.T
        nm = ADAM_B1 * m_ref[...] + (1.0 - ADAM_B1) * gv
        nv = ADAM_B2 * v_ref[...] + (1.0 - ADAM_B2) * (gv * gv)
        g_ref[...] = gv
        nm_ref[...] = nm
        nv_ref[...] = nv
        d_ref[...] = (-ADAM_LR) * ((nm / c1) / (jnp.sqrt(nv / c2) + ADAM_EPS) + ADAM_WD * w_ref[...])

    if transposed:
        spec = pl.BlockSpec((None, C, tr), lambda l, i: (l, 0, i))
    else:
        spec = pl.BlockSpec((None, tr, C), lambda l, i: (l, i, 0))
    shape = jax.ShapeDtypeStruct(w.shape, F32)
    part0 = pl.BlockSpec((N_DEV, tr, C), lambda l, i: (0, jnp.where(l == 0, i, steps - 1), 0))
    part1 = pl.BlockSpec((N_DEV, tr, C), lambda l, i: (0, jnp.where(l == 1, i, 0), 0))
    return pl.pallas_call(
        body, name=name, grid=(DEPTH, steps),
        in_specs=[part0, part1, spec, spec, spec],
        out_specs=[spec] * 4, out_shape=[shape] * 4,
        compiler_params=_params("arbitrary", "arbitrary"),
    )(parts[0], parts[1], w, m, v)


def _block_diag(w):
    eye = jnp.eye(LRU_HEADS, dtype=bool)
    return jnp.where(eye[:, None, :, None], w[:, :, None, :], jnp.zeros((), w.dtype)).reshape(LRU_W, LRU_W)


def _diag_blocks(m):
    eye = jnp.eye(LRU_HEADS, dtype=bool)
    m4 = m.reshape(LRU_HEADS, HEAD_DIM, LRU_HEADS, HEAD_DIM)
    return jnp.sum(jnp.where(eye[:, None, :, None], m4, 0.0), axis=2)


def kernel(x, norm1_g, w_in, conv_dw_w, conv_dw_b, conv_ln_g, conv_ln_b, lru_conv_w, lru_conv_b, lru_wa, lru_ba, lru_wx, lru_bx, lru_lambda, w_out, norm2_g, w_up, w_down, final_g, loss_target, m_norm1_g, m_w_in, m_conv_dw_w, m_conv_dw_b, m_conv_ln_g, m_conv_ln_b, m_lru_conv_w, m_lru_conv_b, m_lru_wa, m_lru_ba, m_lru_wx, m_lru_bx, m_lru_lambda, m_w_out, m_norm2_g, m_w_up, m_w_down, m_final_g, v_norm1_g, v_w_in, v_conv_dw_w, v_conv_dw_b, v_conv_ln_g, v_conv_ln_b, v_lru_conv_w, v_lru_conv_b, v_lru_wa, v_lru_ba, v_lru_wx, v_lru_bx, v_lru_lambda, v_w_out, v_norm2_g, v_w_up, v_w_down, v_final_g):
    local = dict(zip(WEIGHT_ORDER, (norm1_g, w_in, conv_dw_w, conv_dw_b, conv_ln_g, conv_ln_b, lru_conv_w, lru_conv_b, lru_wa,
                                    lru_ba, lru_wx, lru_bx, lru_lambda, w_out, norm2_g, w_up, w_down, final_g)))
    mom1 = dict(zip(WEIGHT_ORDER, (m_norm1_g, m_w_in, m_conv_dw_w, m_conv_dw_b, m_conv_ln_g, m_conv_ln_b, m_lru_conv_w,
                                   m_lru_conv_b, m_lru_wa, m_lru_ba, m_lru_wx, m_lru_bx, m_lru_lambda, m_w_out, m_norm2_g,
                                   m_w_up, m_w_down, m_final_g)))
    mom2 = dict(zip(WEIGHT_ORDER, (v_norm1_g, v_w_in, v_conv_dw_w, v_conv_dw_b, v_conv_ln_g, v_conv_ln_b, v_lru_conv_w,
                                   v_lru_conv_b, v_lru_wa, v_lru_ba, v_lru_wx, v_lru_bx, v_lru_lambda, v_w_out, v_norm2_g,
                                   v_w_up, v_w_down, v_final_g)))
    B, S, _ = x.shape
    T = B * S
    my_slot = 4 * lax.axis_index("x") + 2 * lax.axis_index("y") + lax.axis_index("c")
    row = lambda a: a.reshape(1, -1)

    full = {n: [None] * DEPTH for n in BIG}
    first_items = (("w_in", 0),)
    landed = _run_exchange(_weight_gather(local, first_items, with_filters=True), "gather_first_weights")
    _keep_gathered(full, first_items, landed)
    full.update(_unpack_filters(landed[-1]))

    def hosted(call, layer, fn, *args):
        items = GATHER_HOSTS[call] if layer == 0 else ()
        outs, landed = fn(*args, ex=_weight_gather(local, items) if items else None)
        _keep_gathered(full, items, landed or ())
        return outs

    saved = []
    cur = x.reshape(T, D_MODEL)
    for l in range(DEPTH):
        h, qkv, ci, li = hosted("inproj", l, _inproj, cur, row(norm1_g[l]), full["w_in"][l])
        qkv = qkv.reshape(B, S, QKV_W)
        o, lse = hosted("attn_fwd", l, _attn_fwd, qkv)
        o, lse = o.reshape(T, ATTN_W), lse.reshape(T, ATTN_W)
        ci = ci.reshape(B, S, CONV_IN_W)
        li = li.reshape(B, S, LRU_IN_W)
        conv_p = (full["conv_dw_w"][l], row(conv_dw_b[l]), row(conv_ln_g[l]), row(conv_ln_b[l]))
        lru_p = (full["lru_conv_w"][l], row(lru_conv_b[l]), _block_diag(lru_wa[l]).astype(BF16), row(lru_ba[l]),
                 _block_diag(lru_wx[l]).astype(BF16), row(lru_bx[l]), row(lru_lambda[l]))
        yc, cpre = _conv_fwd(ci, *conv_p)
        yl, hs = hosted("lru_fwd", l, _lru_fwd, li, *lru_p)
        x1, mix = hosted("outproj", l, _outproj, cur, o, lse, yc.reshape(T, CONV_W), yl.reshape(T, LRU_W), full["w_out"][l])
        h2, r = hosted("up", l, _up, x1, row(norm2_g[l]), full["w_up"][l])
        (x2,) = hosted("down", l, _down, x1, r, full["w_down"][l])
        saved.append(dict(x=cur, h=h, qkv=qkv, o=o, lse=lse, ci=ci, li=li, cpre=cpre, hs=hs, x1=x1, mix=mix, h2=h2, r=r,
                          conv_p=conv_p, lru_p=lru_p))
        cur = x2

    dx, loss_part, dgf = _loss_head(cur, loss_target.reshape(T, D_MODEL), row(final_g))

    received = {n: [None] * DEPTH for n in BIG}
    small_grads = {n: [None] * DEPTH for n in SMALL_FULL if n != "final_g"}
    for l in reversed(range(DEPTH)):
        sv = saved[l]
        (dpre,), _ = _down_bwd_act(dx, sv["r"], full["w_down"][l])
        dw_down = _down_bwd_w(sv["r"], dx)
        (dx1, dg2), _ = _up_bwd_act(dpre, full["w_up"][l], sv["x1"], row(norm2_g[l]), dx)
        dw_up = _up_bwd_w(sv["h2"], dpre)
        do, dd, dyc, dyl, dw_out = _outproj_bwd(dx1, sv["mix"], full["w_out"][l], sv["o"], sv["lse"])
        seq = lambda a: a.reshape(B, S, ATTN_W)
        (dq, dk, dv), (received["w_down"][l],) = _attn_bwd(sv["qkv"], seq(do), seq(sv["lse"]), seq(dd),
                                                           ex=_grad_exchange("w_down", dw_down))
        (dci, dcw, dcb, dlg, dlb), (received["w_out"][l],) = _conv_bwd(
            sv["ci"], sv["cpre"], dyc.reshape(B, S, CONV_W), sv["conv_p"][0], sv["conv_p"][2], sv["conv_p"][3],
            ex=_grad_exchange("w_out", dw_out))
        (dli, dlcw, dlcb, dwa, dba, dwx, dbx, dlam), (received["w_up"][l],) = _lru_bwd(
            sv["li"], sv["hs"], dyl.reshape(B, S, LRU_W), *sv["lru_p"], ex=_grad_exchange("w_up", dw_up))
        dz = tuple(t.reshape(T, -1) for t in (dq, dk, dv, dci, dli))
        dw_in = _inproj_bwd_w(dz, sv["h"])
        (dx, dg1), (received["w_in"][l],) = _inproj_bwd_act(dz, full["w_in"][l], sv["x"], row(norm1_g[l]), dx1,
                                                            ex=_grad_exchange("w_in", dw_in))
        for n, g in (("norm1_g", dg1), ("conv_dw_w", dcw), ("conv_dw_b", dcb), ("conv_ln_g", dlg), ("conv_ln_b", dlb),
                     ("lru_conv_w", dlcw), ("lru_conv_b", dlcb), ("lru_wa", _diag_blocks(dwa)), ("lru_ba", dba),
                     ("lru_wx", _diag_blocks(dwx)), ("lru_bx", dbx), ("lru_lambda", dlam), ("norm2_g", dg2)):
            small_grads[n][l] = g.reshape(SMALL_FULL[n][1:])
    grad_x = dx.reshape(B, S, D_MODEL)

    two_d = lambda n: (int(np.prod(SMALL_FULL[n][:-1])), SMALL_FULL[n][-1])
    small_local = {n: jnp.stack(g).reshape(two_d(n)) for n, g in small_grads.items()}
    small_local["final_g"] = dgf
    vector_widths = {n: shp[1] for n, shp in SMALL_FULL.items() if len(shp) == 2}
    other_names = [n for n in SMALL_FULL if n not in vector_widths]
    vectors = jnp.concatenate([small_local[n] for n in vector_widths], axis=1)
    blocks = [vectors] + [small_local[n] for n in other_names] + [loss_part]
    landed = _run_exchange(
        _Gather(blocks, [jax.ShapeDtypeStruct((N_DEV,) + b.shape, F32) for b in blocks], [_slot] * len(blocks)),
        "gather_small_grads")
    gathered = dict(zip(other_names, landed[1:-1]))

    grads, delta, new_m, new_v = {}, {}, {}, {}
    for n, rows_per_step in (("w_in", 256), ("w_out", W_OUT_SHARD), ("w_up", 256), ("w_down", 256)):
        grads[n], delta[n], new_m[n], new_v[n] = _sum_adamw(received[n], local[n], mom1[n], mom2[n], rows_per_step,
                                                            n == "w_down", "sum_adamw_" + n)

    replicated = [n for n in SMALL_FULL if n not in SMALL_SHARDED]
    state = {n: tuple(src[n].reshape(two_d(n)) for src in (local, mom1, mom2)) for n in replicated}
    small_g, loss_sum, updated = _small_sum_adamw(gathered, landed[0], vector_widths, landed[-1], state)
    loss = loss_sum[0, 0]
    for n in replicated:
        grads[n] = small_g[n].reshape(SMALL_FULL[n])
        delta[n], new_m[n], new_v[n] = (t.reshape(SMALL_FULL[n]) for t in updated[n])
    for n, fullshape in SMALL_SHARDED.items():
        width = fullshape[-1] // N_DEV
        g = lax.dynamic_slice_in_dim(small_g[n], my_slot * width, width, axis=1)
        d, nm, nv = _adamw(local[n].reshape(g.shape), g, mom1[n].reshape(g.shape), mom2[n].reshape(g.shape), g.shape[0],
                           "adamw_" + n)
        grads[n], delta[n], new_m[n], new_v[n] = (t.reshape(local[n].shape) for t in (g, d, nm, nv))

    return (loss, grad_x, *[grads[n] for n in WEIGHT_ORDER], *[delta[n] for n in WEIGHT_ORDER],
            *[new_m[n] for n in WEIGHT_ORDER], *[new_v[n] for n in WEIGHT_ORDER])
```

```python
import functools
import math

import numpy as np
import jax
import jax.numpy as jnp
from jax import lax
from jax.experimental import pallas as pl
from jax.experimental.pallas import tpu as pltpu

F32 = jnp.float32
BF16 = jnp.bfloat16

D_MODEL = 1024
SEQ_LEN = 2048
HEAD_DIM = 64
ATTN_W = 384
CONV_W = 256
CONV_TAPS = 31
LRU_W = 384
LRU_TAPS = 4
LRU_HEADS = 6
LRU_C = 8.0
QKV_W = 3 * ATTN_W
CONV_IN_W = 2 * CONV_W
LRU_IN_W = 2 * LRU_W
IN_COLS = QKV_W + CONV_IN_W + LRU_IN_W
D_FF = 4096
DEPTH = 2
N_DEV = 8
RMS_EPS = 1e-6
LN_EPS = 1e-5
ATTN_BLOCK = 128
ATTN_DILATIONS = (1, 4, 16)
N_UNITS = 16
UNIT_UNROLL = 8
NEG_BIG = -1e30

ADAM_LR = 0.001
ADAM_B1 = 0.9
ADAM_B2 = 0.999
ADAM_EPS = 1e-08
ADAM_WD = 0.01
ADAM_STEP = 10

VMEM_LIMIT = 56 * 1024 * 1024
ROW_TILE = 512
GRAD_ROW_TILE = 1024
CONV_FWD_CHUNK = 128
CONV_BWD_CHUNK = 128
LRU_FWD_CHUNK = 128
LRU_BWD_CHUNK = 128


def _params(*sem):
    return pltpu.CompilerParams(dimension_semantics=sem if sem else None, vmem_limit_bytes=VMEM_LIMIT)


def _resident(shape):
    return pl.BlockSpec(shape, lambda *_: (0,) * len(shape), pipeline_mode=pl.Buffered(1))


def _dot(a, b):
    return jnp.dot(a, b, preferred_element_type=F32)


def _dot_nt(a, b):
    return lax.dot_general(a, b, (((1,), (1,)), ((), ())), preferred_element_type=F32)


def _dot_tn(a, b):
    return lax.dot_general(a, b, (((0,), (0,)), ((), ())), preferred_element_type=F32)


def _rms_fwd(x, g):
    rstd = lax.rsqrt(jnp.mean(x * x, axis=-1, keepdims=True) + RMS_EPS)
    xhat = x * rstd
    return xhat * g, xhat, rstd


def _rms_bwd(dh, xhat, rstd, g):
    dxh = dh * g
    dx = rstd * (dxh - xhat * jnp.mean(dxh * xhat, axis=-1, keepdims=True))
    dg = jnp.sum(dh * xhat, axis=0, keepdims=True)
    return dx, dg


def _sigmoid(x):
    return 0.5 * jnp.tanh(0.5 * x) + 0.5


def _one_minus_exp(x, exp_x):
    small = -x * (1.0 + x * (0.5 + x * (1.0 / 6.0)))
    return jnp.where(x > -0.01, small, 1.0 - exp_x)


def _log1p(z):
    w = 1.0 + z
    return jnp.where(w == 1.0, z, z * jnp.log(w) / jnp.where(w == 1.0, 1.0, w - 1.0))


def _softplus(x):
    return jnp.maximum(x, 0.0) + _log1p(jnp.exp(-jnp.abs(x)))


GELU_K = math.sqrt(2.0 / math.pi)


def _gelu(x):
    t = jnp.tanh(GELU_K * (x + 0.044715 * x * x * x))
    return 0.5 * x * (1.0 + t), t


def _gelu_grad(x, t):
    return 0.5 * (1.0 + t) + 0.5 * x * (1.0 - t * t) * GELU_K * (1.0 + 3.0 * 0.044715 * x * x)


def _inproj(x2d, g, w, ex=None):
    T = x2d.shape[0]
    tm = ROW_TILE

    def body(x_ref, g_ref, w_ref, h_ref, qkv_ref, ci_ref, li_ref):
        h, _, _ = _rms_fwd(x_ref[...], g_ref[...])
        hb = h.astype(BF16)
        h_ref[...] = hb
        qkv_ref[...] = _dot(hb, w_ref[:, 0:QKV_W])
        ci_ref[...] = _dot(hb, w_ref[:, QKV_W:QKV_W + CONV_IN_W])
        li_ref[...] = _dot(hb, w_ref[:, QKV_W + CONV_IN_W:IN_COLS])

    return _pallas_hosting(
        body, ex, name="inproj", grid=(T // tm,),
        in_specs=[pl.BlockSpec((tm, D_MODEL), lambda i: (i, 0)),
                  pl.BlockSpec((1, D_MODEL), lambda i: (0, 0)),
                  _resident((D_MODEL, IN_COLS))],
        out_specs=[pl.BlockSpec((tm, D_MODEL), lambda i: (i, 0)),
                   pl.BlockSpec((tm, QKV_W), lambda i: (i, 0)),
                   pl.BlockSpec((tm, CONV_IN_W), lambda i: (i, 0)),
                   pl.BlockSpec((tm, LRU_IN_W), lambda i: (i, 0))],
        out_shape=[jax.ShapeDtypeStruct((T, D_MODEL), BF16), jax.ShapeDtypeStruct((T, QKV_W), F32),
                   jax.ShapeDtypeStruct((T, CONV_IN_W), F32), jax.ShapeDtypeStruct((T, LRU_IN_W), F32)],
        scratch_shapes=[], semantics=("parallel",), operands=(x2d, g, w))


def _attn_alpha(lse):
    l0, l1, l2 = lse[:, 0:128], lse[:, 128:256], lse[:, 256:384]
    m = jnp.maximum(jnp.maximum(l0, l1), l2)
    e0, e1, e2 = jnp.exp(l0 - m), jnp.exp(l1 - m), jnp.exp(l2 - m)
    inv = 1.0 / (e0 + e1 + e2)
    return e0 * inv, e1 * inv, e2 * inv


def _outproj(x2d, o, lse, yc, yl, w, ex=None):
    T = x2d.shape[0]
    tm = ROW_TILE

    def body(x_ref, o_ref, lse_ref, yc_ref, yl_ref, w_ref, x1_ref, mix_ref):
        al = _attn_alpha(lse_ref[...])
        for p in range(3):
            mix_ref[:, p * 128:(p + 1) * 128] = (o_ref[:, p * 128:(p + 1) * 128] * al[p]).astype(BF16)
        mix_ref[:, ATTN_W:ATTN_W + CONV_W] = yc_ref[...]
        mix_ref[:, ATTN_W + CONV_W:D_MODEL] = yl_ref[...]
        x1_ref[...] = x_ref[...] + _dot(mix_ref[...], w_ref[...])

    return _pallas_hosting(
        body, ex, name="outproj", grid=(T // tm,),
        in_specs=[pl.BlockSpec((tm, D_MODEL), lambda i: (i, 0)),
                  pl.BlockSpec((tm, ATTN_W), lambda i: (i, 0)),
                  pl.BlockSpec((tm, ATTN_W), lambda i: (i, 0)),
                  pl.BlockSpec((tm, CONV_W), lambda i: (i, 0)),
                  pl.BlockSpec((tm, LRU_W), lambda i: (i, 0)),
                  _resident((D_MODEL, D_MODEL))],
        out_specs=[pl.BlockSpec((tm, D_MODEL), lambda i: (i, 0)),
                   pl.BlockSpec((tm, D_MODEL), lambda i: (i, 0))],
        out_shape=[jax.ShapeDtypeStruct((T, D_MODEL), F32), jax.ShapeDtypeStruct((T, D_MODEL), BF16)],
        scratch_shapes=[], semantics=("parallel",), operands=(x2d, o, lse, yc, yl, w))


FF_CHUNK = 1024


def _up(x1, g, w, ex=None):
    T = x1.shape[0]
    tm = ROW_TILE

    def body(x_ref, g_ref, w_ref, h_ref, r_ref):
        h, _, _ = _rms_fwd(x_ref[...], g_ref[...])
        hb = h.astype(BF16)
        h_ref[...] = hb
        for c in range(0, D_FF, FF_CHUNK):
            r_ref[:, c:c + FF_CHUNK] = jnp.maximum(_dot(hb, w_ref[:, c:c + FF_CHUNK]), 0.0).astype(BF16)

    return _pallas_hosting(
        body, ex, name="up", grid=(T // tm,),
        in_specs=[pl.BlockSpec((tm, D_MODEL), lambda i: (i, 0)),
                  pl.BlockSpec((1, D_MODEL), lambda i: (0, 0)),
                  _resident((D_MODEL, D_FF))],
        out_specs=[pl.BlockSpec((tm, D_MODEL), lambda i: (i, 0)),
                   pl.BlockSpec((tm, D_FF), lambda i: (i, 0))],
        out_shape=[jax.ShapeDtypeStruct((T, D_MODEL), BF16), jax.ShapeDtypeStruct((T, D_FF), BF16)],
        scratch_shapes=[], semantics=("parallel",), operands=(x1, g, w))


def _square_bf16(r):
    rf = r.astype(F32)
    return (rf * rf).astype(BF16)


def _down(x1, r, w, ex=None):
    T = x1.shape[0]
    tm = ROW_TILE

    def body(x_ref, r_ref, w_ref, o_ref):
        acc = x_ref[...]
        for c in range(0, D_FF, FF_CHUNK):
            acc = acc + _dot(_square_bf16(r_ref[:, c:c + FF_CHUNK]), w_ref[c:c + FF_CHUNK, :])
        o_ref[...] = acc

    return _pallas_hosting(
        body, ex, name="down", grid=(T // tm,),
        in_specs=[pl.BlockSpec((tm, D_MODEL), lambda i: (i, 0)),
                  pl.BlockSpec((tm, D_FF), lambda i: (i, 0)),
                  _resident((D_FF, D_MODEL))],
        out_specs=[pl.BlockSpec((tm, D_MODEL), lambda i: (i, 0))],
        out_shape=[jax.ShapeDtypeStruct((T, D_MODEL), F32)],
        scratch_shapes=[], semantics=("parallel",), operands=(x1, r, w))


def _down_loss(x1, r, w, target, g):
    T = x1.shape[0]
    tm = ROW_TILE

    def body(x_ref, r_ref, w_ref, t_ref, g_ref, dx_ref, loss_ref, dg_ref):
        @pl.when(pl.program_id(0) == 0)
        def _():
            loss_ref[...] = jnp.zeros_like(loss_ref)
            dg_ref[...] = jnp.zeros_like(dg_ref)

        x2 = x_ref[...]
        for c in range(0, D_FF, FF_CHUNK):
            x2 = x2 + _dot(_square_bf16(r_ref[:, c:c + FF_CHUNK]), w_ref[c:c + FF_CHUNK, :])
        gv = g_ref[...]
        y, xhat, rstd = _rms_fwd(x2, gv)
        err = y - t_ref[...]
        loss_ref[...] += 0.5 * jnp.sum(jnp.mean(err * err, axis=-1, keepdims=True))
        dy = err * (1.0 / D_MODEL)
        dx, dg = _rms_bwd(dy, xhat, rstd, gv)
        dx_ref[...] = dx
        dg_ref[...] += dg

    rows = pl.BlockSpec((tm, D_MODEL), lambda i: (i, 0))
    return pl.pallas_call(
        body, name="down_loss", grid=(T // tm,),
        in_specs=[rows, pl.BlockSpec((tm, D_FF), lambda i: (i, 0)), _resident((D_FF, D_MODEL)), rows,
                  pl.BlockSpec((1, D_MODEL), lambda i: (0, 0))],
        out_specs=[rows, pl.BlockSpec((8, 128), lambda i: (0, 0)), pl.BlockSpec((1, D_MODEL), lambda i: (0, 0))],
        out_shape=[jax.ShapeDtypeStruct((T, D_MODEL), F32), jax.ShapeDtypeStruct((8, 128), F32),
                   jax.ShapeDtypeStruct((1, D_MODEL), F32)],
        compiler_params=_params("arbitrary"),
    )(x1, r, w, target, g)


def _down_bwd_act(dx2, r, w, ex=None):
    T = dx2.shape[0]
    tm = ROW_TILE

    def body(dx_ref, r_ref, w_ref, o_ref):
        dxb = dx_ref[...].astype(BF16)
        for c in range(0, D_FF, FF_CHUNK):
            dff = _dot_nt(dxb, w_ref[c:c + FF_CHUNK, :])
            o_ref[:, c:c + FF_CHUNK] = (dff * (2.0 * r_ref[:, c:c + FF_CHUNK].astype(F32))).astype(BF16)

    return _pallas_hosting(
        body, ex, name="down_bwd_act", grid=(T // tm,),
        in_specs=[pl.BlockSpec((tm, D_MODEL), lambda i: (i, 0)),
                  pl.BlockSpec((tm, D_FF), lambda i: (i, 0)),
                  _resident((D_FF, D_MODEL))],
        out_specs=[pl.BlockSpec((tm, D_FF), lambda i: (i, 0))],
        out_shape=[jax.ShapeDtypeStruct((T, D_FF), BF16)],
        scratch_shapes=[], semantics=("parallel",), operands=(dx2, r, w))


def _down_bwd_w(r, dx2):
    T = dx2.shape[0]
    tk = GRAD_ROW_TILE
    nk = T // tk

    def body(r_ref, dx_ref, o_ref, acc_ref):
        k = pl.program_id(0)
        dxb = dx_ref[...].astype(BF16)

        @pl.when(k == 0)
        def _():
            acc_ref[...] = jnp.zeros_like(acc_ref)

        for c in range(0, D_FF, FF_CHUNK):
            acc_ref[:, c:c + FF_CHUNK] += _dot_tn(dxb, _square_bf16(r_ref[:, c:c + FF_CHUNK]))

        @pl.when(k == nk - 1)
        def _():
            o_ref[...] = acc_ref[...].astype(BF16)

    return pl.pallas_call(
        body, name="down_bwd_w", grid=(nk,),
        in_specs=[pl.BlockSpec((tk, D_FF), lambda k: (k, 0)),
                  pl.BlockSpec((tk, D_MODEL), lambda k: (k, 0))],
        out_specs=_resident((D_MODEL, D_FF)),
        out_shape=jax.ShapeDtypeStruct((D_MODEL, D_FF), BF16),
        scratch_shapes=[pltpu.VMEM((D_MODEL, D_FF), F32)],
        compiler_params=_params("arbitrary"),
    )(r, dx2)


def _up_bwd_act(dpre, w, x1, g, dx2, ex=None):
    T = dx2.shape[0]
    tm = ROW_TILE

    def body(dp_ref, w_ref, x_ref, g_ref, dx2_ref, dx1_ref, dg_ref):
        dh = _dot_nt(dp_ref[:, 0:FF_CHUNK], w_ref[:, 0:FF_CHUNK])
        for c in range(FF_CHUNK, D_FF, FF_CHUNK):
            dh = dh + _dot_nt(dp_ref[:, c:c + FF_CHUNK], w_ref[:, c:c + FF_CHUNK])
        gv = g_ref[...]
        _, xhat, rstd = _rms_fwd(x_ref[...], gv)
        dx, dg = _rms_bwd(dh, xhat, rstd, gv)
        dx1_ref[...] = dx2_ref[...] + dx

        @pl.when(pl.program_id(0) == 0)
        def _():
            dg_ref[...] = dg

        @pl.when(pl.program_id(0) != 0)
        def _():
            dg_ref[...] += dg

    return _pallas_hosting(
        body, ex, name="up_bwd_act", grid=(T // tm,),
        in_specs=[pl.BlockSpec((tm, D_FF), lambda i: (i, 0)),
                  _resident((D_MODEL, D_FF)),
                  pl.BlockSpec((tm, D_MODEL), lambda i: (i, 0)),
                  pl.BlockSpec((1, D_MODEL), lambda i: (0, 0)),
                  pl.BlockSpec((tm, D_MODEL), lambda i: (i, 0))],
        out_specs=[pl.BlockSpec((tm, D_MODEL), lambda i: (i, 0)),
                   pl.BlockSpec((1, D_MODEL), lambda i: (0, 0))],
        out_shape=[jax.ShapeDtypeStruct((T, D_MODEL), F32), jax.ShapeDtypeStruct((1, D_MODEL), F32)],
        scratch_shapes=[], semantics=("arbitrary",), operands=(dpre, w, x1, g, dx2))


def _up_bwd_w(h2, dpre):
    T = h2.shape[0]
    tk = GRAD_ROW_TILE
    nk = T // tk

    def body(h_ref, dp_ref, o_ref, acc_ref):
        k = pl.program_id(0)
        hb = h_ref[...]

        @pl.when(k == 0)
        def _():
            acc_ref[...] = jnp.zeros_like(acc_ref)

        for c in range(0, D_FF, FF_CHUNK):
            acc_ref[:, c:c + FF_CHUNK] += _dot_tn(hb, dp_ref[:, c:c + FF_CHUNK])

        @pl.when(k == nk - 1)
        def _():
            o_ref[...] = acc_ref[...].astype(BF16)

    return pl.pallas_call(
        body, name="up_bwd_w", grid=(nk,),
        in_specs=[pl.BlockSpec((tk, D_MODEL), lambda k: (k, 0)),
                  pl.BlockSpec((tk, D_FF), lambda k: (k, 0))],
        out_specs=_resident((D_MODEL, D_FF)),
        out_shape=jax.ShapeDtypeStruct((D_MODEL, D_FF), BF16),
        scratch_shapes=[pltpu.VMEM((D_MODEL, D_FF), F32)],
        compiler_params=_params("arbitrary"),
    )(h2, dpre)


def _outproj_bwd(dx1, mix, w, o, lse):
    T = dx1.shape[0]
    tm = ROW_TILE
    nk = T // tm

    def body(dx_ref, mix_ref, w_ref, o_ref, lse_ref, do_ref, dd_ref, dc_ref, dl_ref, dw_ref, acc_ref):
        i = pl.program_id(0)
        dxb = dx_ref[...].astype(BF16)
        dmix = _dot_nt(dxb, w_ref[...])
        al = _attn_alpha(lse_ref[...])
        first = lax.broadcasted_iota(jnp.int32, (tm, 128), 1) < HEAD_DIM
        tot = jnp.zeros((tm, 128), F32)
        for p in range(3):
            sl = slice(p * 128, (p + 1) * 128)
            dy = dmix[:, sl]
            do_ref[:, sl] = dy * al[p]
            prod = dy * o_ref[:, sl]
            s0 = jnp.sum(jnp.where(first, prod, 0.0), axis=-1, keepdims=True)
            s1 = jnp.sum(jnp.where(first, 0.0, prod), axis=-1, keepdims=True)
            tot = tot + al[p] * jnp.where(first, s0, s1)
        for p in range(3):
            dd_ref[:, p * 128:(p + 1) * 128] = -al[p] * tot
        dc_ref[...] = dmix[:, ATTN_W:ATTN_W + CONV_W]
        dl_ref[...] = dmix[:, ATTN_W + CONV_W:D_MODEL]
        part = _dot_tn(mix_ref[...], dxb)

        @pl.when(i == 0)
        def _():
            acc_ref[...] = part

        @pl.when(i != 0)
        def _():
            acc_ref[...] += part

        @pl.when(i == nk - 1)
        def _():
            dw_ref[...] = acc_ref[...].astype(BF16)

    return pl.pallas_call(
        body, name="outproj_bwd", grid=(nk,),
        in_specs=[pl.BlockSpec((tm, D_MODEL), lambda i: (i, 0)),
                  pl.BlockSpec((tm, D_MODEL), lambda i: (i, 0)),
                  _resident((D_MODEL, D_MODEL)),
                  pl.BlockSpec((tm, ATTN_W), lambda i: (i, 0)),
                  pl.BlockSpec((tm, ATTN_W), lambda i: (i, 0))],
        out_specs=[pl.BlockSpec((tm, ATTN_W), lambda i: (i, 0)),
                   pl.BlockSpec((tm, ATTN_W), lambda i: (i, 0)),
                   pl.BlockSpec((tm, CONV_W), lambda i: (i, 0)),
                   pl.BlockSpec((tm, LRU_W), lambda i: (i, 0)),
                   _resident((D_MODEL, D_MODEL))],
        out_shape=[jax.ShapeDtypeStruct((T, ATTN_W), F32), jax.ShapeDtypeStruct((T, ATTN_W), F32),
                   jax.ShapeDtypeStruct((T, CONV_W), F32), jax.ShapeDtypeStruct((T, LRU_W), F32),
                   jax.ShapeDtypeStruct((D_MODEL, D_MODEL), BF16)],
        scratch_shapes=[pltpu.VMEM((D_MODEL, D_MODEL), F32)],
        compiler_params=_params("arbitrary"),
    )(dx1, mix, w, o, lse)


DZ_COLS = ((0, ATTN_W), (ATTN_W, 2 * ATTN_W), (2 * ATTN_W, QKV_W), (QKV_W, QKV_W + CONV_IN_W), (QKV_W + CONV_IN_W, IN_COLS))


def _inproj_bwd_w(dz_parts, h):
    T = h.shape[0]
    tm = GRAD_ROW_TILE
    nk = T // tm
    n_parts = len(DZ_COLS)

    def body(*refs):
        dz_refs = refs[:n_parts]
        h_ref, dw_ref, acc_ref = refs[n_parts:]
        i = pl.program_id(0)
        hb = h_ref[...]

        @pl.when(i == 0)
        def _():
            acc_ref[...] = jnp.zeros_like(acc_ref)

        for r, (lo, hi) in zip(dz_refs, DZ_COLS):
            acc_ref[:, lo:hi] += _dot_tn(hb, r[...].astype(BF16))

        @pl.when(i == nk - 1)
        def _():
            dw_ref[...] = acc_ref[...].astype(BF16)

    rows = lambda width: pl.BlockSpec((tm, width), lambda i: (i, 0))
    return pl.pallas_call(
        body, name="inproj_bwd_w", grid=(nk,),
        in_specs=[rows(hi - lo) for lo, hi in DZ_COLS] + [rows(D_MODEL)],
        out_specs=_resident((D_MODEL, IN_COLS)),
        out_shape=jax.ShapeDtypeStruct((D_MODEL, IN_COLS), BF16),
        scratch_shapes=[pltpu.VMEM((D_MODEL, IN_COLS), F32)],
        compiler_params=_params("arbitrary"),
    )(*dz_parts, h)


def _inproj_bwd_act(dz_parts, w, x2d, g, dx1, ex=None):
    T = x2d.shape[0]
    tm = ROW_TILE
    n_parts = len(DZ_COLS)

    def body(*refs):
        dz_refs = refs[:n_parts]
        w_ref, x_ref, g_ref, dx1_ref, dx_ref, dg_ref = refs[n_parts:]
        dh = _dot_nt(dz_refs[0][...].astype(BF16), w_ref[:, DZ_COLS[0][0]:DZ_COLS[0][1]])
        for r, (lo, hi) in zip(dz_refs[1:], DZ_COLS[1:]):
            dh = dh + _dot_nt(r[...].astype(BF16), w_ref[:, lo:hi])
        gv = g_ref[...]
        _, xhat, rstd = _rms_fwd(x_ref[...], gv)
        dx, dg = _rms_bwd(dh, xhat, rstd, gv)
        dx_ref[...] = dx1_ref[...] + dx

        @pl.when(pl.program_id(0) == 0)
        def _():
            dg_ref[...] = dg

        @pl.when(pl.program_id(0) != 0)
        def _():
            dg_ref[...] += dg

    rows = lambda width: pl.BlockSpec((tm, width), lambda i: (i, 0))
    return _pallas_hosting(
        body, ex, name="inproj_bwd_act", grid=(T // tm,),
        in_specs=[rows(hi - lo) for lo, hi in DZ_COLS] + [
            _resident((D_MODEL, IN_COLS)), rows(D_MODEL), pl.BlockSpec((1, D_MODEL), lambda i: (0, 0)), rows(D_MODEL)],
        out_specs=[rows(D_MODEL), pl.BlockSpec((1, D_MODEL), lambda i: (0, 0))],
        out_shape=[jax.ShapeDtypeStruct((T, D_MODEL), F32), jax.ShapeDtypeStruct((1, D_MODEL), F32)],
        scratch_shapes=[], semantics=("arbitrary",), operands=(*dz_parts, w, x2d, g, dx1))


def _alibi_coef():
    slopes = 2.0 ** (-8.0 * np.arange(1, 7) / 6)
    return jnp.asarray((slopes.reshape(3, 2) * np.asarray(ATTN_DILATIONS)[:, None]).astype(np.float32))


def _unit_rows(u, d):
    nb = N_UNITS // d
    r, n = u // nb, u % nb
    span = ATTN_BLOCK * d

    def rows(block):
        start = block * span + r
        return pl.ds(pl.multiple_of(start, ATTN_BLOCK), ATTN_BLOCK) if d == 1 else pl.ds(start, ATTN_BLOCK, stride=d)

    return rows(n), rows(jnp.maximum(n - 1, 0)), rows(jnp.minimum(n + 1, nb - 1)), n > 0, n + 1 < nb


def _per_pattern(fn):
    for p, d in enumerate(ATTN_DILATIONS):
        pl.when(pl.program_id(1) == p)(functools.partial(fn, p, d))


def _attn_col(offset):
    return pl.BlockSpec((None, SEQ_LEN, 128), lambda b, p: (b, 0, p + offset))


def _attn_masks():
    qi = lax.broadcasted_iota(jnp.int32, (ATTN_BLOCK, 2 * ATTN_BLOCK), 0)
    kj = lax.broadcasted_iota(jnp.int32, (ATTN_BLOCK, 2 * ATTN_BLOCK), 1)
    dist = qi + ATTN_BLOCK - kj
    first = lax.broadcasted_iota(jnp.int32, (ATTN_BLOCK, 128), 1) < HEAD_DIM
    return dist.astype(F32), (dist >= 0) & (dist <= ATTN_BLOCK), kj >= ATTN_BLOCK, first


def _head_lanes(a, first, j):
    return jnp.where(first if j == 0 else jnp.logical_not(first), a, jnp.zeros_like(a))


def _load_kv(ref, prev, own):
    return jnp.concatenate([ref[prev, :], ref[own, :]], axis=0).astype(BF16)


def _attn_fwd(qkv, ex=None):
    B = qkv.shape[0]

    def body(coef_ref, q_ref, k_ref, v_ref, o_ref, lse_ref):
        dist, band, own_half, first = _attn_masks()

        def pattern(p, d):
            def unit(u, carry):
                own, prev, _, has_prev, _ = _unit_rows(u, d)
                ok = band & jnp.logical_or(own_half, has_prev)
                q = q_ref[own, :].astype(BF16)
                kcat, vcat = _load_kv(k_ref, prev, own), _load_kv(v_ref, prev, own)
                outs, lses = [], []
                for j in range(2):
                    s = jnp.where(ok, _dot_nt(_head_lanes(q, first, j), kcat) * 0.125 - coef_ref[p, j] * dist, NEG_BIG)
                    m = jnp.max(s, axis=-1, keepdims=True)
                    e = jnp.exp(s - m)
                    l = jnp.sum(e, axis=-1, keepdims=True)
                    outs.append(_dot(e.astype(BF16), vcat) * (1.0 / l))
                    lses.append(m + jnp.log(l))
                o_ref[own, :] = jnp.where(first, outs[0], outs[1])
                lse_ref[own, :] = jnp.where(first, lses[0], lses[1])
                return carry

            lax.fori_loop(0, N_UNITS, unit, 0, unroll=UNIT_UNROLL)

        _per_pattern(pattern)

    shape = jax.ShapeDtypeStruct((B, SEQ_LEN, ATTN_W), F32)
    return _pallas_hosting(
        body, ex, name="attn_fwd", grid=(B, 3),
        in_specs=[pl.BlockSpec(memory_space=pltpu.SMEM), _attn_col(0), _attn_col(3), _attn_col(6)],
        out_specs=[_attn_col(0), _attn_col(0)],
        out_shape=[shape, shape],
        scratch_shapes=[], semantics=("parallel", "parallel"), operands=(_alibi_coef(), qkv, qkv, qkv))


def _attn_bwd(qkv, do, lse, dd, ex=None):
    B = qkv.shape[0]

    def body(coef_ref, q_ref, k_ref, v_ref, do_ref, lse_ref, dd_ref, dq_ref, dk_ref, dv_ref):
        dist, band, own_half, first = _attn_masks()

        def pattern(p, d):
            def unit(u, carry):
                own, prev, _, has_prev, _ = _unit_rows(u, d)
                ok = band & jnp.logical_or(own_half, has_prev)
                q, do = q_ref[own, :].astype(BF16), do_ref[own, :].astype(BF16)
                kcat, vcat = _load_kv(k_ref, prev, own), _load_kv(v_ref, prev, own)
                lse_a, dd_a = lse_ref[own, :], dd_ref[own, :]
                dqs, dks, dvs = [], [], []
                for j in range(2):
                    col = slice(HEAD_DIM * j, HEAD_DIM * j + 1)
                    s = _dot_nt(_head_lanes(q, first, j), kcat) * 0.125 - coef_ref[p, j] * dist
                    pr = jnp.where(ok, jnp.exp(jnp.where(ok, s, NEG_BIG) - lse_a[:, col]), 0.0)
                    ds = (pr * (_dot_nt(_head_lanes(do, first, j), vcat) + dd_a[:, col])).astype(BF16)
                    dqs.append(_dot(ds, kcat))
                    dks.append(_dot_tn(ds, q))
                    dvs.append(_dot_tn(pr.astype(BF16), do))
                both = lambda pair: jnp.where(jnp.concatenate([first] * (pair[0].shape[0] // ATTN_BLOCK), axis=0), *pair)
                dq_ref[own, :] = both(dqs) * 0.125
                dk, dv = both(dks) * 0.125, both(dvs)
                dk_ref[own, :] = dk[ATTN_BLOCK:]
                dv_ref[own, :] = dv[ATTN_BLOCK:]
                dk_ref[prev, :] += dk[:ATTN_BLOCK]
                dv_ref[prev, :] += dv[:ATTN_BLOCK]
                return carry

            lax.fori_loop(0, N_UNITS, unit, 0, unroll=UNIT_UNROLL)

        _per_pattern(pattern)

    shape = jax.ShapeDtypeStruct((B, SEQ_LEN, ATTN_W), F32)
    return _pallas_hosting(
        body, ex, name="attn_bwd", grid=(B, 3),
        in_specs=[pl.BlockSpec(memory_space=pltpu.SMEM), _attn_col(0), _attn_col(3), _attn_col(6), _attn_col(0), _attn_col(0),
                  _attn_col(0)],
        out_specs=[_attn_col(0)] * 3,
        out_shape=[shape] * 3,
        scratch_shapes=[], semantics=("parallel", "parallel"), operands=(_alibi_coef(), qkv, qkv, qkv, do, lse, dd))


def _for_chunks(n_rows, fn, chunk):
    def step(c, carry):
        fn(pl.multiple_of(c * chunk, chunk))
        return carry

    lax.fori_loop(0, n_rows // chunk, step, 0)


def _shift_down(win, s, rows):
    lead = win.shape[0] - rows
    if s == 0:
        return win[lead:]
    if s % 8 == 0:
        return win[lead - s:lead - s + rows]
    q, r = divmod(s, 8)
    rolled = pltpu.roll(win, r, 0)
    return rolled[lead - 8 * q:lead - 8 * q + rows]


def _tap_shifts(n_taps):
    return [(r, [(n_taps - 1 - (8 * q + r), 8 * q) for q in range((n_taps - 1 - r) // 8 + 1)]) for r in range(min(8, n_taps))]


def _rotated_down(win, r):
    return win if r == 0 else pltpu.roll(win, r, 0)


def _rotated_up(win, r):
    return win if r == 0 else pltpu.roll(win, win.shape[0] - r, 0)


def _shift_up(win, s, rows):
    if s % 8 == 0:
        return win[s:s + rows]
    q, r = divmod(s, 8)
    rolled = pltpu.roll(win, win.shape[0] - r, 0)
    return rolled[8 * q:8 * q + rows]


CONV_PAD = 32


def _ln_silu(c, lg, lb):
    mu = jnp.mean(c, axis=-1, keepdims=True)
    cc = c - mu
    rstd = lax.rsqrt(jnp.mean(cc * cc, axis=-1, keepdims=True) + LN_EPS)
    nrm = cc * rstd
    v = nrm * lg + lb
    sg = _sigmoid(v)
    return v * sg, nrm, rstd, v, sg


def _conv_fwd(ci, w, b, lg, lb):
    B, S, _ = ci.shape
    CH = CONV_FWD_CHUNK

    def body(ci_ref, w_ref, b_ref, lg_ref, lb_ref, y_ref, c_ref, pad_ref):
        pad_ref[0:CONV_PAD, :] = jnp.zeros((CONV_PAD, CONV_W), F32)

        def glu(base):
            blk = ci_ref[pl.ds(base, CH), :]
            pad_ref[pl.ds(CONV_PAD + base, CH), :] = blk[:, 0:CONV_W] * _sigmoid(blk[:, CONV_W:])

        _for_chunks(S, glu, CH)

        def conv(base):
            win = pad_ref[pl.ds(base, CH + CONV_PAD), :]
            acc = jnp.broadcast_to(b_ref[...], (CH, CONV_W))
            for r, taps in _tap_shifts(CONV_TAPS):
                rot = _rotated_down(win, r)
                for k, off in taps:
                    acc = acc + w_ref[k:k + 1, :] * rot[CONV_PAD - off:CONV_PAD - off + CH]
            c_ref[pl.ds(base, CH), :] = acc
            y, _, _, _, _ = _ln_silu(acc, lg_ref[...], lb_ref[...])
            y_ref[pl.ds(base, CH), :] = y.astype(BF16)

        _for_chunks(S, conv, CH)

    vec = pl.BlockSpec((1, CONV_W), lambda i: (0, 0))
    return pl.pallas_call(
        body, name="conv_fwd", grid=(B,),
        in_specs=[pl.BlockSpec((None, S, CONV_IN_W), lambda i: (i, 0, 0)),
                  pl.BlockSpec((CONV_TAPS, CONV_W), lambda i: (0, 0)), vec, vec, vec],
        out_specs=[pl.BlockSpec((None, S, CONV_W), lambda i: (i, 0, 0)),
                   pl.BlockSpec((None, S, CONV_W), lambda i: (i, 0, 0))],
        out_shape=[jax.ShapeDtypeStruct((B, S, CONV_W), BF16), jax.ShapeDtypeStruct((B, S, CONV_W), F32)],
        scratch_shapes=[pltpu.VMEM((S + CONV_PAD, CONV_W), F32)],
        compiler_params=_params("parallel"),
    )(ci, w, b, lg, lb)


def _conv_bwd(ci, cpre, dy, w, lg, lb, ex=None):
    B, S, _ = ci.shape
    CH = CONV_BWD_CHUNK

    def body(ci_ref, c_ref, dy_ref, w_ref, lg_ref, lb_ref, dci_ref, dw_ref, db_ref, dlg_ref, dlb_ref, upad_ref, dcpad_ref,
             dwacc_ref):
        @pl.when(pl.program_id(0) == 0)
        def _():
            dw_ref[...] = jnp.zeros_like(dw_ref)
            db_ref[...] = jnp.zeros_like(db_ref)
            dlg_ref[...] = jnp.zeros_like(dlg_ref)
            dlb_ref[...] = jnp.zeros_like(dlb_ref)

        upad_ref[0:CONV_PAD, :] = jnp.zeros((CONV_PAD, CONV_W), F32)
        dcpad_ref[S:S + CONV_PAD, :] = jnp.zeros((CONV_PAD, CONV_W), F32)
        dwacc_ref[...] = jnp.zeros_like(dwacc_ref)

        def norm_bwd(base):
            blk = ci_ref[pl.ds(base, CH), :]
            upad_ref[pl.ds(CONV_PAD + base, CH), :] = blk[:, 0:CONV_W] * _sigmoid(blk[:, CONV_W:])
            lgv = lg_ref[...]
            _, nrm, rstd, v, sg = _ln_silu(c_ref[pl.ds(base, CH), :], lgv, lb_ref[...])
            dv = dy_ref[pl.ds(base, CH), :] * (sg * (1.0 + v * (1.0 - sg)))
            dlg_ref[...] += jnp.sum(dv * nrm, axis=0, keepdims=True)
            dlb_ref[...] += jnp.sum(dv, axis=0, keepdims=True)
            dn = dv * lgv
            dc = rstd * (dn - jnp.mean(dn, axis=-1, keepdims=True) - nrm * jnp.mean(dn * nrm, axis=-1, keepdims=True))
            dcpad_ref[pl.ds(base, CH), :] = dc
            db_ref[...] += jnp.sum(dc, axis=0, keepdims=True)

        _for_chunks(S, norm_bwd, CH)

        def conv_bwd(base):
            dwin = dcpad_ref[pl.ds(base, CH + CONV_PAD), :]
            uwin = upad_ref[pl.ds(base, CH + CONV_PAD), :]
            dc = dwin[0:CH]
            du = jnp.zeros((CH, CONV_W), F32)
            for r, taps in _tap_shifts(CONV_TAPS):
                d_rot, u_rot = _rotated_up(dwin, r), _rotated_down(uwin, r)
                for k, off in taps:
                    du = du + w_ref[k:k + 1, :] * d_rot[off:off + CH]
                    prod = dc * u_rot[CONV_PAD - off:CONV_PAD - off + CH]
                    dwacc_ref[8 * k:8 * k + 8, :] += jnp.sum(prod.reshape(CH // 8, 8, CONV_W), axis=0)
            blk = ci_ref[pl.ds(base, CH), :]
            a, sg = blk[:, 0:CONV_W], _sigmoid(blk[:, CONV_W:])
            dci_ref[pl.ds(base, CH), 0:CONV_W] = (du * sg).astype(BF16)
            dci_ref[pl.ds(base, CH), CONV_W:] = (du * a * sg * (1.0 - sg)).astype(BF16)

        _for_chunks(S, conv_bwd, CH)
        for k in range(CONV_TAPS):
            dw_ref[k:k + 1, :] += jnp.sum(dwacc_ref[8 * k:8 * k + 8, :], axis=0, keepdims=True)

    vec = pl.BlockSpec((1, CONV_W), lambda i: (0, 0))
    mat = pl.BlockSpec((CONV_TAPS, CONV_W), lambda i: (0, 0))
    seq = lambda width: pl.BlockSpec((None, S, width), lambda i: (i, 0, 0))
    return _pallas_hosting(
        body, ex, name="conv_bwd", grid=(B,),
        in_specs=[seq(CONV_IN_W), seq(CONV_W), seq(CONV_W), mat, vec, vec],
        out_specs=[seq(CONV_IN_W), mat, vec, vec, vec],
        out_shape=[jax.ShapeDtypeStruct((B, S, CONV_IN_W), BF16), jax.ShapeDtypeStruct((CONV_TAPS, CONV_W), F32),
                   jax.ShapeDtypeStruct((1, CONV_W), F32), jax.ShapeDtypeStruct((1, CONV_W), F32),
                   jax.ShapeDtypeStruct((1, CONV_W), F32)],
        scratch_shapes=[pltpu.VMEM((S + CONV_PAD, CONV_W), F32), pltpu.VMEM((S + CONV_PAD, CONV_W), F32),
                        pltpu.VMEM((8 * CONV_TAPS, CONV_W), F32)],
        semantics=("arbitrary",), operands=(ci, cpre, dy, w, lg, lb))


SCAN_SHIFTS = tuple(1 << e for e in range(11))


def _prev8(ref, base, cols, fill):
    start = pl.multiple_of(jnp.maximum(base - 8, 0), 8)
    return jnp.where(base > 0, ref[pl.ds(start, 8), cols], fill)


def _next8(ref, base, rows, total, cols, fill):
    start = pl.multiple_of(jnp.minimum(base + rows, total - 8), 8)
    return jnp.where(base + rows < total, ref[pl.ds(start, 8), cols], fill)


ALL = slice(None)
LRU_X = slice(LRU_W, LRU_IN_W)
LRU_GATE = slice(0, LRU_W)


def _lru_conv(li_ref, base, rows, cw_ref, cb_ref):
    win = jnp.concatenate([_prev8(li_ref, base, LRU_X, 0.0), li_ref[pl.ds(base, rows), LRU_X]], axis=0)
    u = jnp.broadcast_to(cb_ref[...], (rows, LRU_W))
    for k in range(LRU_TAPS):
        u = u + cw_ref[k:k + 1, :] * _shift_down(win, LRU_TAPS - 1 - k, rows)
    return u, win


def _lru_gates(u, wa_ref, ba_ref, wx_ref, bx_ref, sp):
    ub = u.astype(BF16)
    r = _sigmoid(_dot(ub, wa_ref[...]) + ba_ref[...])
    i = _sigmoid(_dot(ub, wx_ref[...]) + bx_ref[...])
    la = (-LRU_C) * r * sp
    a = jnp.exp(la)
    return ub, r, i, a, _one_minus_exp(2.0 * la, a * a)


def _scan_forward(bufs, S, CH):
    for n, s in enumerate(SCAN_SHIFTS):
        (sa, sb), (da, db) = bufs[n % 2], bufs[(n + 1) % 2]

        def step(base, s=s, sa=sa, sb=sb, da=da, db=db):
            a, b = sa[pl.ds(base, CH), :], sb[pl.ds(base, CH), :]
            if s < 8:
                a_s = _shift_down(jnp.concatenate([_prev8(sa, base, ALL, 1.0), a], axis=0), s, CH)
                b_s = _shift_down(jnp.concatenate([_prev8(sb, base, ALL, 0.0), b], axis=0), s, CH)
            elif s < CH:
                start = pl.multiple_of(jnp.maximum(base - s, 0), 8)
                a_s = jnp.concatenate([jnp.where(base > 0, sa[pl.ds(start, s), :], 1.0), a[0:CH - s]], axis=0)
                b_s = jnp.concatenate([jnp.where(base > 0, sb[pl.ds(start, s), :], 0.0), b[0:CH - s]], axis=0)
            else:
                start = pl.multiple_of(jnp.maximum(base - s, 0), 8)
                a_s = jnp.where(base < s, 1.0, sa[pl.ds(start, CH), :])
                b_s = jnp.where(base < s, 0.0, sb[pl.ds(start, CH), :])
            db[pl.ds(base, CH), :] = a * b_s + b
            da[pl.ds(base, CH), :] = a * a_s

        _for_chunks(S, step, CH)
    return len(SCAN_SHIFTS) % 2


def _scan_backward(bufs, S, CH):
    for n, s in enumerate(SCAN_SHIFTS):
        (sa, sb), (da, db) = bufs[n % 2], bufs[(n + 1) % 2]

        def step(base, s=s, sa=sa, sb=sb, da=da, db=db):
            a, b = sa[pl.ds(base, CH), :], sb[pl.ds(base, CH), :]
            if s < 8:
                a_s = _shift_up(jnp.concatenate([a, _next8(sa, base, CH, S, ALL, 1.0)], axis=0), s, CH)
                b_s = _shift_up(jnp.concatenate([b, _next8(sb, base, CH, S, ALL, 0.0)], axis=0), s, CH)
            elif s < CH:
                start = pl.multiple_of(jnp.minimum(base + CH, S - s), 8)
                more = base + CH < S
                a_s = jnp.concatenate([a[s:CH], jnp.where(more, sa[pl.ds(start, s), :], 1.0)], axis=0)
                b_s = jnp.concatenate([b[s:CH], jnp.where(more, sb[pl.ds(start, s), :], 0.0)], axis=0)
            else:
                start = pl.multiple_of(jnp.minimum(base + s, S - CH), 8)
                a_s = jnp.where(base + s >= S, 1.0, sa[pl.ds(start, CH), :])
                b_s = jnp.where(base + s >= S, 0.0, sb[pl.ds(start, CH), :])
            db[pl.ds(base, CH), :] = a * b_s + b
            da[pl.ds(base, CH), :] = a * a_s

        _for_chunks(S, step, CH)
    return len(SCAN_SHIFTS) % 2


def _lru_fwd(li, cw, cb, wa, ba, wx, bx, lam, ex=None):
    B, S, _ = li.shape
    CH = LRU_FWD_CHUNK

    def body(li_ref, cw_ref, cb_ref, wa_ref, ba_ref, wx_ref, bx_ref, lam_ref, y_ref, h_ref, a0, b0, a1, b1):
        sp = _softplus(-lam_ref[...])

        def gates(base):
            u, _ = _lru_conv(li_ref, base, CH, cw_ref, cb_ref)
            _, _, i, a, em = _lru_gates(u, wa_ref, ba_ref, wx_ref, bx_ref, sp)
            a0[pl.ds(base, CH), :] = a
            b0[pl.ds(base, CH), :] = jnp.sqrt(em) * (i * u)

        _for_chunks(S, gates, CH)
        bufs = ((a0, b0), (a1, b1))
        hb = bufs[_scan_forward(bufs, S, CH)][1]

        def out(base):
            h = hb[pl.ds(base, CH), :]
            h_ref[pl.ds(base, CH), :] = h
            gl, _ = _gelu(li_ref[pl.ds(base, CH), LRU_GATE])
            y_ref[pl.ds(base, CH), :] = (gl * h).astype(BF16)

        _for_chunks(S, out, CH)

    vec = pl.BlockSpec((1, LRU_W), lambda i: (0, 0))
    mat = pl.BlockSpec((LRU_W, LRU_W), lambda i: (0, 0))
    seq = lambda width: pl.BlockSpec((None, S, width), lambda i: (i, 0, 0))
    return _pallas_hosting(
        body, ex, name="lru_fwd", grid=(B,),
        in_specs=[seq(LRU_IN_W), pl.BlockSpec((LRU_TAPS, LRU_W), lambda i: (0, 0)), vec, mat, vec, mat, vec, vec],
        out_specs=[seq(LRU_W), seq(LRU_W)],
        out_shape=[jax.ShapeDtypeStruct((B, S, LRU_W), BF16), jax.ShapeDtypeStruct((B, S, LRU_W), F32)],
        scratch_shapes=[pltpu.VMEM((S, LRU_W), F32)] * 4,
        semantics=("parallel",), operands=(li, cw, cb, wa, ba, wx, bx, lam))


def _lru_bwd(li, hs, dy, cw, cb, wa, ba, wx, bx, lam, ex=None):
    B, S, _ = li.shape
    CH = LRU_BWD_CHUNK

    def body(li_ref, hs_ref, dy_ref, cw_ref, cb_ref, wa_ref, ba_ref, wx_ref, bx_ref, lam_ref,
             dli_ref, dcw_ref, dcb_ref, dwa_ref, dba_ref, dwx_ref, dbx_ref, dlam_ref, a0, b0, a1, b1, u_s, du_s):
        @pl.when(pl.program_id(0) == 0)
        def _():
            for ref in (dcw_ref, dcb_ref, dwa_ref, dba_ref, dwx_ref, dbx_ref, dlam_ref):
                ref[...] = jnp.zeros_like(ref)

        lam_v = lam_ref[...]
        sp = _softplus(-lam_v)
        dsp_dlam = -_sigmoid(-lam_v)

        def gates(base):
            u, _ = _lru_conv(li_ref, base, CH, cw_ref, cb_ref)
            _, _, _, a, _ = _lru_gates(u, wa_ref, ba_ref, wx_ref, bx_ref, sp)
            gl, _ = _gelu(li_ref[pl.ds(base, CH), LRU_GATE])
            u_s[pl.ds(base, CH), :] = u
            a0[pl.ds(base, CH), :] = a
            b0[pl.ds(base, CH), :] = a * (dy_ref[pl.ds(base, CH), :] * gl)

        _for_chunks(S, gates, CH)
        bufs = ((a0, b0), (a1, b1))
        eb = bufs[_scan_backward(bufs, S, CH)][1]

        def grads(base):
            e = eb[pl.ds(base, CH), :]
            e_next = _shift_up(jnp.concatenate([e, _next8(eb, base, CH, S, ALL, 0.0)], axis=0), 1, CH)
            gate = li_ref[pl.ds(base, CH), LRU_GATE]
            gl, th = _gelu(gate)
            dy = dy_ref[pl.ds(base, CH), :]
            g = dy * gl + e_next
            h = hs_ref[pl.ds(base, CH), :]
            h_prev = _shift_down(jnp.concatenate([_prev8(hs_ref, base, ALL, 0.0), h], axis=0), 1, CH)
            dli_ref[pl.ds(base, CH), LRU_GATE] = (dy * h * _gelu_grad(gate, th)).astype(BF16)
            u = u_s[pl.ds(base, CH), :]
            ub, r, i, a, em = _lru_gates(u, wa_ref, ba_ref, wx_ref, bx_ref, sp)
            mult = jnp.sqrt(em)
            da = g * h_prev
            dmult = g * (i * u)
            di = g * mult * u
            dla = da * a - dmult * (a * a) * lax.rsqrt(jnp.maximum(em, 1e-30))
            dlam_ref[...] += dsp_dlam * jnp.sum(dla * ((-LRU_C) * r), axis=0, keepdims=True)
            dpa = (dla * ((-LRU_C) * sp)) * r * (1.0 - r)
            dpx = di * i * (1.0 - i)
            dpab, dpxb = dpa.astype(BF16), dpx.astype(BF16)
            dwa_ref[...] += _dot_tn(ub, dpab)
            dwx_ref[...] += _dot_tn(ub, dpxb)
            dba_ref[...] += jnp.sum(dpa, axis=0, keepdims=True)
            dbx_ref[...] += jnp.sum(dpx, axis=0, keepdims=True)
            du = g * mult * i + _dot_nt(dpab, wa_ref[...]) + _dot_nt(dpxb, wx_ref[...])
            du_s[pl.ds(base, CH), :] = du
            dcb_ref[...] += jnp.sum(du, axis=0, keepdims=True)

        _for_chunks(S, grads, CH)

        def conv_bwd(base):
            du = du_s[pl.ds(base, CH), :]
            dwin = jnp.concatenate([du, _next8(du_s, base, CH, S, ALL, 0.0)], axis=0)
            xwin = jnp.concatenate([_prev8(li_ref, base, LRU_X, 0.0), li_ref[pl.ds(base, CH), LRU_X]], axis=0)
            dx = jnp.zeros((CH, LRU_W), F32)
            for k in range(LRU_TAPS):
                dx = dx + cw_ref[k:k + 1, :] * _shift_up(dwin, LRU_TAPS - 1 - k, CH)
                dcw_ref[k:k + 1, :] += jnp.sum(du * _shift_down(xwin, LRU_TAPS - 1 - k, CH), axis=0, keepdims=True)
            dli_ref[pl.ds(base, CH), LRU_X] = dx.astype(BF16)

        _for_chunks(S, conv_bwd, CH)

    vec = pl.BlockSpec((1, LRU_W), lambda i: (0, 0))
    mat = pl.BlockSpec((LRU_W, LRU_W), lambda i: (0, 0))
    taps = pl.BlockSpec((LRU_TAPS, LRU_W), lambda i: (0, 0))
    seq = lambda width: pl.BlockSpec((None, S, width), lambda i: (i, 0, 0))
    vec_shape = jax.ShapeDtypeStruct((1, LRU_W), F32)
    mat_shape = jax.ShapeDtypeStruct((LRU_W, LRU_W), F32)
    return _pallas_hosting(
        body, ex, name="lru_bwd", grid=(B,),
        in_specs=[seq(LRU_IN_W), seq(LRU_W), seq(LRU_W), taps, vec, mat, vec, mat, vec, vec],
        out_specs=[seq(LRU_IN_W), taps, vec, mat, vec, mat, vec, vec],
        out_shape=[jax.ShapeDtypeStruct((B, S, LRU_IN_W), BF16), jax.ShapeDtypeStruct((LRU_TAPS, LRU_W), F32),
                   vec_shape, mat_shape, vec_shape, mat_shape, vec_shape, vec_shape],
        scratch_shapes=[pltpu.VMEM((S, LRU_W), F32)] * 6,
        semantics=("arbitrary",), operands=(li, hs, dy, cw, cb, wa, ba, wx, bx, lam))


MESH = pl.DeviceIdType.MESH
HBM_SPEC = pl.BlockSpec(memory_space=pltpu.HBM)


def _slot(ref, p):
    return ref.at[p]


def _row_block(rows):
    return lambda ref, p: ref.at[pl.ds(p * rows, rows), :]


def _col_block(cols):
    return lambda ref, p: ref.at[:, pl.ds(p * cols, cols)]


class _Gather:
    def __init__(self, blocks, out_shapes, places):
        self.sources, self.out_shapes, self.places, self.n = list(blocks), list(out_shapes), list(places), len(blocks)

    def scratch(self):
        return [pltpu.SemaphoreType.DMA((self.n, 7)), pltpu.SemaphoreType.DMA((self.n, 7)), pltpu.SemaphoreType.DMA((self.n,))]

    def _plan(self, x_refs, out_refs, send_sems, recv_sems, local_sems):
        n = self.n
        x, y, c = lax.axis_index("x"), lax.axis_index("y"), lax.axis_index("c")
        me, sibling = (x, y, c), (x, y, 1 - c)
        chips = [(1 - x, y), (x, 1 - y), (1 - x, 1 - y)]

        def place(a, dev):
            return self.places[a](out_refs[a], 4 * dev[0] + 2 * dev[1] + dev[2])

        def copy(a, k, blk, to, src=None):
            return pltpu.make_async_remote_copy(
                src_ref=place(a, blk) if src is None else src, dst_ref=place(a, blk),
                send_sem=send_sems.at[a, k], recv_sem=recv_sems.at[a, k], device_id=to, device_id_type=MESH)

        mine = [pltpu.make_async_copy(x_refs[a], place(a, me), local_sems.at[a]) for a in range(n)]
        first = [copy(a, 0, me, sibling, src=x_refs[a]) for a in range(n)]
        first += [copy(a, 1 + j, me, (*chip, c), src=x_refs[a]) for j, chip in enumerate(chips) for a in range(n)]
        return me, sibling, chips, c, copy, mine, first

    def start(self, *refs):
        *_, mine, first = self._plan(*refs)
        for cp in mine + first:
            cp.start()

    def forward(self, *refs):
        me, sibling, chips, c, copy, _, _ = self._plan(*refs)
        for j, chip in enumerate(chips):
            for a in range(self.n):
                copy(a, 1 + j, (*chip, c), me).wait_recv()
                copy(a, 4 + j, (*chip, c), sibling).start()

    def finish(self, *refs):
        me, sibling, chips, c, copy, mine, first = self._plan(*refs)
        passed = [copy(a, 4 + j, (*chip, c), sibling) for j, chip in enumerate(chips) for a in range(self.n)]
        for a in range(self.n):
            copy(a, 0, sibling, me).wait_recv()
        for j, chip in enumerate(chips):
            for a in range(self.n):
                copy(a, 4 + j, (*chip, 1 - c), me).wait_recv()
        for cp in first + passed:
            cp.wait_send()
        for cp in mine:
            cp.wait()


class _GradExchange:
    def __init__(self, sources, takes, piece_shapes):
        self.sources, self.takes, self.n = list(sources), list(takes), len(sources)
        self.out_shapes = [jax.ShapeDtypeStruct((N_DEV,) + tuple(s), BF16) for s in piece_shapes]

    def scratch(self):
        return [pltpu.SemaphoreType.DMA((self.n, 7)), pltpu.SemaphoreType.DMA((self.n, 7)), pltpu.SemaphoreType.DMA((self.n,))]

    def _copies(self, src_refs, out_refs, send_sems, recv_sems, local_sems):
        x, y, c = lax.axis_index("x"), lax.axis_index("y"), lax.axis_index("c")
        me = 4 * x + 2 * y + c
        mine = [pltpu.make_async_copy(self.takes[i](src_refs[i], me), out_refs[i].at[me], local_sems.at[i]) for i in range(self.n)]
        remote = []
        for k in range(1, N_DEV):
            px, py, pc = x ^ ((k >> 2) & 1), y ^ ((k >> 1) & 1), c ^ (k & 1)
            peer = 4 * px + 2 * py + pc
            for i in range(self.n):
                remote.append(pltpu.make_async_remote_copy(
                    src_ref=self.takes[i](src_refs[i], peer), dst_ref=out_refs[i].at[me], send_sem=send_sems.at[i, k - 1],
                    recv_sem=recv_sems.at[i, k - 1], device_id=(px, py, pc), device_id_type=MESH))
        return mine, remote

    def start(self, *refs):
        mine, remote = self._copies(*refs)
        for cp in mine + remote:
            cp.start()

    def forward(self, *refs):
        pass

    def finish(self, *refs):
        mine, remote = self._copies(*refs)
        for cp in remote:
            cp.wait_recv()
        for cp in remote:
            cp.wait_send()
        for cp in mine:
            cp.wait()


def _run_exchange(ex, name):
    def body(*refs):
        src_refs, out_refs, sems = refs[:ex.n], refs[ex.n:2 * ex.n], refs[2 * ex.n:]
        ex.start(src_refs, out_refs, *sems)
        ex.forward(src_refs, out_refs, *sems)
        ex.finish(src_refs, out_refs, *sems)

    return pl.pallas_call(
        body, name=name, out_shape=ex.out_shapes, in_specs=[HBM_SPEC] * ex.n, out_specs=[HBM_SPEC] * ex.n,
        scratch_shapes=ex.scratch(),
    )(*ex.sources)


def _pallas_hosting(body, ex, *, name, grid, in_specs, out_specs, out_shape, scratch_shapes, semantics, operands):
    if ex is None:
        outs = pl.pallas_call(body, name=name, grid=grid, in_specs=in_specs, out_specs=out_specs, out_shape=out_shape,
                              scratch_shapes=scratch_shapes, compiler_params=_params(*semantics))(*operands)
        return outs, None
    n_in, n_out, n_scr, n = len(in_specs), len(out_specs), len(scratch_shapes), ex.n

    def at_step(pick):
        conds = [pl.program_id(k) == pick(size) for k, size in enumerate(grid)]
        return functools.reduce(jnp.logical_and, conds)

    def hosting(*refs):
        ins, ex_ins = refs[:n_in], refs[n_in:n_in + n]
        outs, ex_outs = refs[n_in + n:n_in + n + n_out], refs[n_in + n + n_out:n_in + 2 * n + n_out]
        scratch, sems = refs[n_in + 2 * n + n_out:n_in + 2 * n + n_out + n_scr], refs[n_in + 2 * n + n_out + n_scr:]
        pl.when(at_step(lambda size: 0))(lambda: ex.start(ex_ins, ex_outs, *sems))
        pl.when(at_step(lambda size: size - 1))(lambda: ex.forward(ex_ins, ex_outs, *sems))
        body(*ins, *outs, *scratch)
        pl.when(at_step(lambda size: size - 1))(lambda: ex.finish(ex_ins, ex_outs, *sems))

    res = pl.pallas_call(
        hosting, name=name, grid=grid, in_specs=list(in_specs) + [HBM_SPEC] * n, out_specs=list(out_specs) + [HBM_SPEC] * n,
        out_shape=list(out_shape) + ex.out_shapes, scratch_shapes=list(scratch_shapes) + ex.scratch(),
        compiler_params=_params(*(["arbitrary"] * len(grid))),
    )(*operands, *ex.sources)
    return res[:n_out], res[n_out:]


def _adamw(w, g, m, v, rows_per_step, name):
    R, C = w.shape

    def body(w_ref, g_ref, m_ref, v_ref, d_ref, nm_ref, nv_ref):
        d_ref[...], nm_ref[...], nv_ref[...] = _adam_update(w_ref[...], g_ref[...], m_ref[...], v_ref[...])

    spec = pl.BlockSpec((rows_per_step, C), lambda i: (i, 0))
    shape = jax.ShapeDtypeStruct((R, C), F32)
    return pl.pallas_call(
        body, name=name, grid=(R // rows_per_step,),
        in_specs=[spec] * 4, out_specs=[spec] * 3, out_shape=[shape] * 3,
        compiler_params=_params("parallel"),
    )(w, g, m, v)


def _adam_update(w, g, m, v):
    nm = ADAM_B1 * m + (1.0 - ADAM_B1) * g
    nv = ADAM_B2 * v + (1.0 - ADAM_B2) * (g * g)
    c1 = 1.0 - ADAM_B1 ** ADAM_STEP
    c2 = 1.0 - ADAM_B2 ** ADAM_STEP
    return (-ADAM_LR) * ((nm / c1) / (jnp.sqrt(nv / c2) + ADAM_EPS) + ADAM_WD * w), nm, nv


def _small_sum_adamw(gathered, vectors, vector_widths, loss_slots, state):
    names, updated = list(vector_widths) + list(gathered), list(state)
    n_in, n_g, n_u = len(gathered) + 1, len(names), len(updated)

    def slot_sum(ref):
        acc = ref[0]
        for q in range(1, N_DEV):
            acc = acc + ref[q]
        return acc

    def body(*refs):
        p_refs, loss_in = refs[:n_in], refs[n_in]
        st = refs[n_in + 1:n_in + 1 + 3 * n_u]
        outs = refs[n_in + 1 + 3 * n_u:]
        g_refs, loss_out, upd = outs[:n_g], outs[n_g], outs[n_g + 1:]
        loss_out[...] = slot_sum(loss_in)
        side_by_side, off, sums = slot_sum(p_refs[0]), 0, {}
        for n, width in vector_widths.items():
            sums[n] = side_by_side[:, off:off + width]
            off += width
        for i, n in enumerate(gathered):
            sums[n] = slot_sum(p_refs[1 + i])
        for i, n in enumerate(names):
            g_refs[i][...] = sums[n]
            if n in state:
                j = updated.index(n)
                d, nm, nv = _adam_update(st[3 * j][...], sums[n], st[3 * j + 1][...], st[3 * j + 2][...])
                upd[3 * j][...], upd[3 * j + 1][...], upd[3 * j + 2][...] = d, nm, nv

    g_shapes = [jax.ShapeDtypeStruct((DEPTH, w), F32) for w in vector_widths.values()]
    g_shapes += [jax.ShapeDtypeStruct(gathered[n].shape[1:], F32) for n in gathered]
    u_shapes = [jax.ShapeDtypeStruct(state[n][0].shape, F32) for n in updated for _ in range(3)]
    outs = pl.pallas_call(
        body, name="small_sum_adamw", out_shape=g_shapes + [jax.ShapeDtypeStruct((8, 128), F32)] + u_shapes,
        compiler_params=_params(),
    )(vectors, *gathered.values(), loss_slots, *[a for n in updated for a in state[n]])
    g = dict(zip(names, outs[:n_g]))
    upd = {n: tuple(outs[n_g + 1 + 3 * j:n_g + 4 + 3 * j]) for j, n in enumerate(updated)}
    return g, outs[n_g], upd


WEIGHT_ORDER = ("norm1_g", "w_in", "conv_dw_w", "conv_dw_b", "conv_ln_g", "conv_ln_b", "lru_conv_w", "lru_conv_b", "lru_wa",
                "lru_ba", "lru_wx", "lru_bx", "lru_lambda", "w_out", "norm2_g", "w_up", "w_down", "final_g")
BIG = ("w_in", "w_out", "w_up", "w_down")
SMALL_SHARDED = {"conv_dw_w": (DEPTH, CONV_TAPS, CONV_W), "lru_conv_w": (DEPTH, LRU_TAPS, LRU_W)}
SMALL_FULL = {
    "norm1_g": (DEPTH, D_MODEL), "conv_dw_w": (DEPTH, CONV_TAPS, CONV_W), "conv_dw_b": (DEPTH, CONV_W),
    "conv_ln_g": (DEPTH, CONV_W), "conv_ln_b": (DEPTH, CONV_W), "lru_conv_w": (DEPTH, LRU_TAPS, LRU_W),
    "lru_conv_b": (DEPTH, LRU_W), "lru_wa": (DEPTH, LRU_HEADS, HEAD_DIM, HEAD_DIM), "lru_ba": (DEPTH, LRU_W),
    "lru_wx": (DEPTH, LRU_HEADS, HEAD_DIM, HEAD_DIM), "lru_bx": (DEPTH, LRU_W), "lru_lambda": (DEPTH, LRU_W),
    "norm2_g": (DEPTH, D_MODEL), "final_g": (D_MODEL,),
}
SMALL_COLS = 128
FILTER_ROWS = 24
W_IN_SHARD = IN_COLS // N_DEV
W_OUT_SHARD = D_MODEL // N_DEV
FF_SHARD = D_FF // N_DEV


def _pack_rows(flat_parts, cols, rows):
    flat = jnp.concatenate(flat_parts)
    return jnp.pad(flat, (0, rows * cols - flat.shape[0])).reshape(rows, cols)


WEIGHT_GATHER = {
    "w_in": ((N_DEV, D_MODEL, W_IN_SHARD), _slot),
    "w_out": ((D_MODEL, D_MODEL), _row_block(W_OUT_SHARD)),
    "w_up": ((D_MODEL, D_FF), _col_block(FF_SHARD)),
    "w_down": ((D_FF, D_MODEL), _row_block(FF_SHARD)),
}
GATHER_HOSTS = {
    "inproj": (("w_up", 0),), "attn_fwd": (("w_out", 0), ("w_in", 1)), "lru_fwd": (("w_down", 0),),
    "outproj": (), "up": (("w_up", 1), ("w_out", 1)), "down": (("w_down", 1),),
}


def _weight_gather(local, items, with_filters=False):
    blocks = [local[n][l].astype(BF16) for n, l in items]
    shapes = [jax.ShapeDtypeStruct(WEIGHT_GATHER[n][0], BF16) for n, _ in items]
    places = [WEIGHT_GATHER[n][1] for n, _ in items]
    if with_filters:
        blocks.append(_pack_rows([local[n].reshape(-1) for n in SMALL_SHARDED], SMALL_COLS, FILTER_ROWS))
        shapes.append(jax.ShapeDtypeStruct((N_DEV, FILTER_ROWS, SMALL_COLS), F32))
        places.append(_slot)
    return _Gather(blocks, shapes, places)


def _keep_gathered(full, items, landed):
    for (n, l), arr in zip(items, landed):
        full[n][l] = arr.transpose(1, 0, 2).reshape(D_MODEL, IN_COLS) if n == "w_in" else arr


def _unpack_filters(slots):
    flat, off, out = slots.reshape(N_DEV, -1), 0, {}
    for n, shp in SMALL_SHARDED.items():
        shard = shp[:-1] + (shp[-1] // N_DEV,)
        size = int(np.prod(shard))
        out[n] = jnp.moveaxis(flat[:, off:off + size].reshape((N_DEV,) + shard), 0, -2).reshape(shp)
        off += size
    return out


def _grad_exchange(name, dw):
    if name == "w_in":
        return _GradExchange([dw.reshape(D_MODEL, N_DEV, W_IN_SHARD).transpose(1, 0, 2)], [_slot], [(D_MODEL, W_IN_SHARD)])
    if name == "w_out":
        return _GradExchange([dw], [_row_block(W_OUT_SHARD)], [(W_OUT_SHARD, D_MODEL)])
    return _GradExchange([dw], [_col_block(FF_SHARD)], [(D_MODEL, FF_SHARD)])


def _sum_adamw(parts, w, m, v, rows_per_step, transposed, name):
    _, R, C = parts[0].shape
    tr = rows_per_step
    steps = R // tr

    def slot_sum(p_ref):
        acc = p_ref[0].astype(F32)
        for q in range(1, N_DEV):
            acc = acc + p_ref[q].astype(F32)
        return acc

    def body(p0_ref, p1_ref, w_ref, m_ref, v_ref, g_ref, d_ref, nm_ref, nv_ref):
        gv = jnp.where(pl.program_id(0) == 0, slot_sum(p0_ref), slot_sum(p1_ref))
        if transposed:
            gv = gv.T
        g_ref[...] = gv
        d_ref[...], nm_ref[...], nv_ref[...] = _adam_update(w_ref[...], gv, m_ref[...], v_ref[...])

    if transposed:
        spec = pl.BlockSpec((None, C, tr), lambda l, i: (l, 0, i))
    else:
        spec = pl.BlockSpec((None, tr, C), lambda l, i: (l, i, 0))
    shape = jax.ShapeDtypeStruct(w.shape, F32)
    part0 = pl.BlockSpec((N_DEV, tr, C), lambda l, i: (0, jnp.where(l == 0, i, steps - 1), 0))
    part1 = pl.BlockSpec((N_DEV, tr, C), lambda l, i: (0, jnp.where(l == 1, i, 0), 0))
    return pl.pallas_call(
        body, name=name, grid=(DEPTH, steps),
        in_specs=[part0, part1, spec, spec, spec],
        out_specs=[spec] * 4, out_shape=[shape] * 4,
        compiler_params=_params("arbitrary", "arbitrary"),
    )(parts[0], parts[1], w, m, v)


def _block_diag(w):
    eye = jnp.eye(LRU_HEADS, dtype=bool)
    return jnp.where(eye[:, None, :, None], w[:, :, None, :], jnp.zeros((), w.dtype)).reshape(LRU_W, LRU_W)


def _diag_blocks(m):
    eye = jnp.eye(LRU_HEADS, dtype=bool)
    m4 = m.reshape(LRU_HEADS, HEAD_DIM, LRU_HEADS, HEAD_DIM)
    return jnp.sum(jnp.where(eye[:, None, :, None], m4, 0.0), axis=2)


def kernel(x, norm1_g, w_in, conv_dw_w, conv_dw_b, conv_ln_g, conv_ln_b, lru_conv_w, lru_conv_b, lru_wa, lru_ba, lru_wx, lru_bx, lru_lambda, w_out, norm2_g, w_up, w_down, final_g, loss_target, m_norm1_g, m_w_in, m_conv_dw_w, m_conv_dw_b, m_conv_ln_g, m_conv_ln_b, m_lru_conv_w, m_lru_conv_b, m_lru_wa, m_lru_ba, m_lru_wx, m_lru_bx, m_lru_lambda, m_w_out, m_norm2_g, m_w_up, m_w_down, m_final_g, v_norm1_g, v_w_in, v_conv_dw_w, v_conv_dw_b, v_conv_ln_g, v_conv_ln_b, v_lru_conv_w, v_lru_conv_b, v_lru_wa, v_lru_ba, v_lru_wx, v_lru_bx, v_lru_lambda, v_w_out, v_norm2_g, v_w_up, v_w_down, v_final_g):
    local = dict(zip(WEIGHT_ORDER, (norm1_g, w_in, conv_dw_w, conv_dw_b, conv_ln_g, conv_ln_b, lru_conv_w, lru_conv_b, lru_wa,
                                    lru_ba, lru_wx, lru_bx, lru_lambda, w_out, norm2_g, w_up, w_down, final_g)))
    mom1 = dict(zip(WEIGHT_ORDER, (m_norm1_g, m_w_in, m_conv_dw_w, m_conv_dw_b, m_conv_ln_g, m_conv_ln_b, m_lru_conv_w,
                                   m_lru_conv_b, m_lru_wa, m_lru_ba, m_lru_wx, m_lru_bx, m_lru_lambda, m_w_out, m_norm2_g,
                                   m_w_up, m_w_down, m_final_g)))
    mom2 = dict(zip(WEIGHT_ORDER, (v_norm1_g, v_w_in, v_conv_dw_w, v_conv_dw_b, v_conv_ln_g, v_conv_ln_b, v_lru_conv_w,
                                   v_lru_conv_b, v_lru_wa, v_lru_ba, v_lru_wx, v_lru_bx, v_lru_lambda, v_w_out, v_norm2_g,
                                   v_w_up, v_w_down, v_final_g)))
    B, S, _ = x.shape
    T = B * S
    my_slot = 4 * lax.axis_index("x") + 2 * lax.axis_index("y") + lax.axis_index("c")
    row = lambda a: a.reshape(1, -1)

    full = {n: [None] * DEPTH for n in BIG}
    first_items = (("w_in", 0),)
    landed = _run_exchange(_weight_gather(local, first_items, with_filters=True), "gather_first_weights")
    _keep_gathered(full, first_items, landed)
    full.update(_unpack_filters(landed[-1]))

    def hosted(call, layer, fn, *args):
        items = GATHER_HOSTS[call] if layer == 0 else ()
        outs, landed = fn(*args, ex=_weight_gather(local, items) if items else None)
        _keep_gathered(full, items, landed or ())
        return outs

    saved = []
    cur = x.reshape(T, D_MODEL)
    for l in range(DEPTH):
        h, qkv, ci, li = hosted("inproj", l, _inproj, cur, row(norm1_g[l]), full["w_in"][l])
        qkv = qkv.reshape(B, S, QKV_W)
        o, lse = hosted("attn_fwd", l, _attn_fwd, qkv)
        o, lse = o.reshape(T, ATTN_W), lse.reshape(T, ATTN_W)
        ci = ci.reshape(B, S, CONV_IN_W)
        li = li.reshape(B, S, LRU_IN_W)
        conv_p = (full["conv_dw_w"][l], row(conv_dw_b[l]), row(conv_ln_g[l]), row(conv_ln_b[l]))
        lru_p = (full["lru_conv_w"][l], row(lru_conv_b[l]), _block_diag(lru_wa[l]).astype(BF16), row(lru_ba[l]),
                 _block_diag(lru_wx[l]).astype(BF16), row(lru_bx[l]), row(lru_lambda[l]))
        yc, cpre = _conv_fwd(ci, *conv_p)
        yl, hs = hosted("lru_fwd", l, _lru_fwd, li, *lru_p)
        x1, mix = hosted("outproj", l, _outproj, cur, o, lse, yc.reshape(T, CONV_W), yl.reshape(T, LRU_W), full["w_out"][l])
        h2, r = hosted("up", l, _up, x1, row(norm2_g[l]), full["w_up"][l])
        if l < DEPTH - 1:
            (x2,) = hosted("down", l, _down, x1, r, full["w_down"][l])
        else:
            x2, loss_part, dgf = _down_loss(x1, r, full["w_down"][l], loss_target.reshape(T, D_MODEL), row(final_g))
        saved.append(dict(x=cur, h=h, qkv=qkv, o=o, lse=lse, ci=ci, li=li, cpre=cpre, hs=hs, x1=x1, mix=mix, h2=h2, r=r,
                          conv_p=conv_p, lru_p=lru_p))
        cur = x2

    dx = cur

    received = {n: [None] * DEPTH for n in BIG}
    small_grads = {n: [None] * DEPTH for n in SMALL_FULL if n != "final_g"}
    for l in reversed(range(DEPTH)):
        sv = saved[l]
        (dpre,), _ = _down_bwd_act(dx, sv["r"], full["w_down"][l])
        dw_down = _down_bwd_w(sv["r"], dx)
        (dx1, dg2), _ = _up_bwd_act(dpre, full["w_up"][l], sv["x1"], row(norm2_g[l]), dx)
        dw_up = _up_bwd_w(sv["h2"], dpre)
        do, dd, dyc, dyl, dw_out = _outproj_bwd(dx1, sv["mix"], full["w_out"][l], sv["o"], sv["lse"])
        seq = lambda a: a.reshape(B, S, ATTN_W)
        (dq, dk, dv), (received["w_down"][l],) = _attn_bwd(sv["qkv"], seq(do), seq(sv["lse"]), seq(dd),
                                                           ex=_grad_exchange("w_down", dw_down))
        (dci, dcw, dcb, dlg, dlb), (received["w_out"][l],) = _conv_bwd(
            sv["ci"], sv["cpre"], dyc.reshape(B, S, CONV_W), sv["conv_p"][0], sv["conv_p"][2], sv["conv_p"][3],
            ex=_grad_exchange("w_out", dw_out))
        (dli, dlcw, dlcb, dwa, dba, dwx, dbx, dlam), (received["w_up"][l],) = _lru_bwd(
            sv["li"], sv["hs"], dyl.reshape(B, S, LRU_W), *sv["lru_p"], ex=_grad_exchange("w_up", dw_up))
        dz = tuple(t.reshape(T, -1) for t in (dq, dk, dv, dci, dli))
        dw_in = _inproj_bwd_w(dz, sv["h"])
        (dx, dg1), (received["w_in"][l],) = _inproj_bwd_act(dz, full["w_in"][l], sv["x"], row(norm1_g[l]), dx1,
                                                            ex=_grad_exchange("w_in", dw_in))
        for n, g in (("norm1_g", dg1), ("conv_dw_w", dcw), ("conv_dw_b", dcb), ("conv_ln_g", dlg), ("conv_ln_b", dlb),
                     ("lru_conv_w", dlcw), ("lru_conv_b", dlcb), ("lru_wa", _diag_blocks(dwa)), ("lru_ba", dba),
                     ("lru_wx", _diag_blocks(dwx)), ("lru_bx", dbx), ("lru_lambda", dlam), ("norm2_g", dg2)):
            small_grads[n][l] = g.reshape(SMALL_FULL[n][1:])
    grad_x = dx.reshape(B, S, D_MODEL)

    two_d = lambda n: (int(np.prod(SMALL_FULL[n][:-1])), SMALL_FULL[n][-1])
    small_local = {n: jnp.stack(g).reshape(two_d(n)) for n, g in small_grads.items()}
    small_local["final_g"] = dgf
    vector_widths = {n: shp[1] for n, shp in SMALL_FULL.items() if len(shp) == 2}
    other_names = [n for n in SMALL_FULL if n not in vector_widths]
    vectors = jnp.concatenate([small_local[n] for n in vector_widths], axis=1)
    blocks = [vectors] + [small_local[n] for n in other_names] + [loss_part]
    landed = _run_exchange(
        _Gather(blocks, [jax.ShapeDtypeStruct((N_DEV,) + b.shape, F32) for b in blocks], [_slot] * len(blocks)),
        "gather_small_grads")
    gathered = dict(zip(other_names, landed[1:-1]))

    grads, delta, new_m, new_v = {}, {}, {}, {}
    for n, rows_per_step in (("w_in", 256), ("w_out", W_OUT_SHARD), ("w_up", 256), ("w_down", 256)):
        grads[n], delta[n], new_m[n], new_v[n] = _sum_adamw(received[n], local[n], mom1[n], mom2[n], rows_per_step,
                                                            n == "w_down", "sum_adamw_" + n)

    replicated = [n for n in SMALL_FULL if n not in SMALL_SHARDED]
    state = {n: tuple(src[n].reshape(two_d(n)) for src in (local, mom1, mom2)) for n in replicated}
    small_g, loss_sum, updated = _small_sum_adamw(gathered, landed[0], vector_widths, landed[-1], state)
    loss = loss_sum[0, 0]
    for n in replicated:
        grads[n] = small_g[n].reshape(SMALL_FULL[n])
        delta[n], new_m[n], new_v[n] = (t.reshape(SMALL_FULL[n]) for t in updated[n])
    for n, fullshape in SMALL_SHARDED.items():
        width = fullshape[-1] // N_DEV
        g = lax.dynamic_slice_in_dim(small_g[n], my_slot * width, width, axis=1)
        d, nm, nv = _adamw(local[n].reshape(g.shape), g, mom1[n].reshape(g.shape), mom2[n].reshape(g.shape), g.shape[0],
                           "adamw_" + n)
        grads[n], delta[n], new_m[n], new_v[n] = (t.reshape(local[n].shape) for t in (g, d, nm, nv))

    return (loss, grad_x, *[grads[n] for n in WEIGHT_ORDER], *[delta[n] for n in WEIGHT_ORDER],
            *[new_m[n] for n in WEIGHT_ORDER], *[new_v[n] for n in WEIGHT_ORDER])
```

```python
import functools
import math

import numpy as np
import jax
import jax.numpy as jnp
from jax import lax
from jax.experimental import pallas as pl
from jax.experimental.pallas import tpu as pltpu

F32 = jnp.float32
BF16 = jnp.bfloat16

D_MODEL = 1024
SEQ_LEN = 2048
HEAD_DIM = 64
ATTN_W = 384
CONV_W = 256
CONV_TAPS = 31
LRU_W = 384
LRU_TAPS = 4
LRU_HEADS = 6
LRU_C = 8.0
QKV_W = 3 * ATTN_W
CONV_IN_W = 2 * CONV_W
LRU_IN_W = 2 * LRU_W
IN_COLS = QKV_W + CONV_IN_W + LRU_IN_W
D_FF = 4096
DEPTH = 2
N_DEV = 8
RMS_EPS = 1e-6
LN_EPS = 1e-5
ATTN_BLOCK = 128
ATTN_DILATIONS = (1, 4, 16)
N_UNITS = 16
UNIT_UNROLL = 8
NEG_BIG = -1e30

ADAM_LR = 0.001
ADAM_B1 = 0.9
ADAM_B2 = 0.999
ADAM_EPS = 1e-08
ADAM_WD = 0.01
ADAM_STEP = 10

VMEM_LIMIT = 56 * 1024 * 1024
ROW_TILE = 512
GRAD_ROW_TILE = 1024
CONV_FWD_CHUNK = 128
CONV_BWD_CHUNK = 128
LRU_FWD_CHUNK = 128
LRU_BWD_CHUNK = 128


def _params(*sem):
    return pltpu.CompilerParams(dimension_semantics=sem if sem else None, vmem_limit_bytes=VMEM_LIMIT)


def _resident(shape):
    return pl.BlockSpec(shape, lambda *_: (0,) * len(shape), pipeline_mode=pl.Buffered(1))


def _dot(a, b):
    return jnp.dot(a, b, preferred_element_type=F32)


def _dot_nt(a, b):
    return lax.dot_general(a, b, (((1,), (1,)), ((), ())), preferred_element_type=F32)


def _dot_tn(a, b):
    return lax.dot_general(a, b, (((0,), (0,)), ((), ())), preferred_element_type=F32)


def _rms_fwd(x, g):
    rstd = lax.rsqrt(jnp.mean(x * x, axis=-1, keepdims=True) + RMS_EPS)
    xhat = x * rstd
    return xhat * g, xhat, rstd


def _rms_bwd(dh, xhat, rstd, g):
    dxh = dh * g
    dx = rstd * (dxh - xhat * jnp.mean(dxh * xhat, axis=-1, keepdims=True))
    dg = jnp.sum(dh * xhat, axis=0, keepdims=True)
    return dx, dg


def _sigmoid(x):
    return 0.5 * jnp.tanh(0.5 * x) + 0.5


def _one_minus_exp(x, exp_x):
    small = -x * (1.0 + x * (0.5 + x * (1.0 / 6.0)))
    return jnp.where(x > -0.01, small, 1.0 - exp_x)


def _log1p(z):
    w = 1.0 + z
    return jnp.where(w == 1.0, z, z * jnp.log(w) / jnp.where(w == 1.0, 1.0, w - 1.0))


def _softplus(x):
    return jnp.maximum(x, 0.0) + _log1p(jnp.exp(-jnp.abs(x)))


GELU_K = math.sqrt(2.0 / math.pi)


def _gelu(x):
    t = jnp.tanh(GELU_K * (x + 0.044715 * x * x * x))
    return 0.5 * x * (1.0 + t), t


def _gelu_grad(x, t):
    return 0.5 * (1.0 + t) + 0.5 * x * (1.0 - t * t) * GELU_K * (1.0 + 3.0 * 0.044715 * x * x)


def _inproj(x2d, g, w, ex=None):
    T = x2d.shape[0]
    tm = ROW_TILE

    def body(x_ref, g_ref, w_ref, h_ref, qkv_ref, ci_ref, li_ref):
        h, _, _ = _rms_fwd(x_ref[...], g_ref[...])
        hb = h.astype(BF16)
        h_ref[...] = hb
        qkv_ref[...] = _dot(hb, w_ref[:, 0:QKV_W])
        ci_ref[...] = _dot(hb, w_ref[:, QKV_W:QKV_W + CONV_IN_W])
        li_ref[...] = _dot(hb, w_ref[:, QKV_W + CONV_IN_W:IN_COLS])

    return _pallas_hosting(
        body, ex, name="inproj", grid=(T // tm,),
        in_specs=[pl.BlockSpec((tm, D_MODEL), lambda i: (i, 0)),
                  pl.BlockSpec((1, D_MODEL), lambda i: (0, 0)),
                  _resident((D_MODEL, IN_COLS))],
        out_specs=[pl.BlockSpec((tm, D_MODEL), lambda i: (i, 0)),
                   pl.BlockSpec((tm, QKV_W), lambda i: (i, 0)),
                   pl.BlockSpec((tm, CONV_IN_W), lambda i: (i, 0)),
                   pl.BlockSpec((tm, LRU_IN_W), lambda i: (i, 0))],
        out_shape=[jax.ShapeDtypeStruct((T, D_MODEL), BF16), jax.ShapeDtypeStruct((T, QKV_W), F32),
                   jax.ShapeDtypeStruct((T, CONV_IN_W), F32), jax.ShapeDtypeStruct((T, LRU_IN_W), F32)],
        scratch_shapes=[], semantics=("parallel",), operands=(x2d, g, w))


def _attn_alpha(lse):
    l0, l1, l2 = lse[:, 0:128], lse[:, 128:256], lse[:, 256:384]
    m = jnp.maximum(jnp.maximum(l0, l1), l2)
    e0, e1, e2 = jnp.exp(l0 - m), jnp.exp(l1 - m), jnp.exp(l2 - m)
    inv = 1.0 / (e0 + e1 + e2)
    return e0 * inv, e1 * inv, e2 * inv


def _outproj(x2d, o, lse, yc, yl, w, ex=None):
    T = x2d.shape[0]
    tm = ROW_TILE

    def body(x_ref, o_ref, lse_ref, yc_ref, yl_ref, w_ref, x1_ref, mix_ref):
        al = _attn_alpha(lse_ref[...])
        for p in range(3):
            mix_ref[:, p * 128:(p + 1) * 128] = (o_ref[:, p * 128:(p + 1) * 128] * al[p]).astype(BF16)
        mix_ref[:, ATTN_W:ATTN_W + CONV_W] = yc_ref[...]
        mix_ref[:, ATTN_W + CONV_W:D_MODEL] = yl_ref[...]
        x1_ref[...] = x_ref[...] + _dot(mix_ref[...], w_ref[...])

    return _pallas_hosting(
        body, ex, name="outproj", grid=(T // tm,),
        in_specs=[pl.BlockSpec((tm, D_MODEL), lambda i: (i, 0)),
                  pl.BlockSpec((tm, ATTN_W), lambda i: (i, 0)),
                  pl.BlockSpec((tm, ATTN_W), lambda i: (i, 0)),
                  pl.BlockSpec((tm, CONV_W), lambda i: (i, 0)),
                  pl.BlockSpec((tm, LRU_W), lambda i: (i, 0)),
                  _resident((D_MODEL, D_MODEL))],
        out_specs=[pl.BlockSpec((tm, D_MODEL), lambda i: (i, 0)),
                   pl.BlockSpec((tm, D_MODEL), lambda i: (i, 0))],
        out_shape=[jax.ShapeDtypeStruct((T, D_MODEL), F32), jax.ShapeDtypeStruct((T, D_MODEL), BF16)],
        scratch_shapes=[], semantics=("parallel",), operands=(x2d, o, lse, yc, yl, w))


FF_CHUNK = 1024


def _up(x1, g, w, ex=None):
    T = x1.shape[0]
    tm = ROW_TILE

    def body(x_ref, g_ref, w_ref, h_ref, r_ref):
        h, _, _ = _rms_fwd(x_ref[...], g_ref[...])
        hb = h.astype(BF16)
        h_ref[...] = hb
        for c in range(0, D_FF, FF_CHUNK):
            r_ref[:, c:c + FF_CHUNK] = jnp.maximum(_dot(hb, w_ref[:, c:c + FF_CHUNK]), 0.0).astype(BF16)

    return _pallas_hosting(
        body, ex, name="up", grid=(T // tm,),
        in_specs=[pl.BlockSpec((tm, D_MODEL), lambda i: (i, 0)),
                  pl.BlockSpec((1, D_MODEL), lambda i: (0, 0)),
                  _resident((D_MODEL, D_FF))],
        out_specs=[pl.BlockSpec((tm, D_MODEL), lambda i: (i, 0)),
                   pl.BlockSpec((tm, D_FF), lambda i: (i, 0))],
        out_shape=[jax.ShapeDtypeStruct((T, D_MODEL), BF16), jax.ShapeDtypeStruct((T, D_FF), BF16)],
        scratch_shapes=[], semantics=("parallel",), operands=(x1, g, w))


def _square_bf16(r):
    rf = r.astype(F32)
    return (rf * rf).astype(BF16)


def _down(x1, r, w, ex=None):
    T = x1.shape[0]
    tm = ROW_TILE

    def body(x_ref, r_ref, w_ref, o_ref):
        acc = x_ref[...]
        for c in range(0, D_FF, FF_CHUNK):
            acc = acc + _dot(_square_bf16(r_ref[:, c:c + FF_CHUNK]), w_ref[c:c + FF_CHUNK, :])
        o_ref[...] = acc

    return _pallas_hosting(
        body, ex, name="down", grid=(T // tm,),
        in_specs=[pl.BlockSpec((tm, D_MODEL), lambda i: (i, 0)),
                  pl.BlockSpec((tm, D_FF), lambda i: (i, 0)),
                  _resident((D_FF, D_MODEL))],
        out_specs=[pl.BlockSpec((tm, D_MODEL), lambda i: (i, 0))],
        out_shape=[jax.ShapeDtypeStruct((T, D_MODEL), F32)],
        scratch_shapes=[], semantics=("parallel",), operands=(x1, r, w))


def _down_loss(x1, r, w, target, g):
    T = x1.shape[0]
    tm = ROW_TILE

    def body(x_ref, r_ref, w_ref, t_ref, g_ref, dx_ref, loss_ref, dg_ref):
        @pl.when(pl.program_id(0) == 0)
        def _():
            loss_ref[...] = jnp.zeros_like(loss_ref)
            dg_ref[...] = jnp.zeros_like(dg_ref)

        x2 = x_ref[...]
        for c in range(0, D_FF, FF_CHUNK):
            x2 = x2 + _dot(_square_bf16(r_ref[:, c:c + FF_CHUNK]), w_ref[c:c + FF_CHUNK, :])
        gv = g_ref[...]
        y, xhat, rstd = _rms_fwd(x2, gv)
        err = y - t_ref[...]
        loss_ref[...] += 0.5 * jnp.sum(jnp.mean(err * err, axis=-1, keepdims=True))
        dy = err * (1.0 / D_MODEL)
        dx, dg = _rms_bwd(dy, xhat, rstd, gv)
        dx_ref[...] = dx
        dg_ref[...] += dg

    rows = pl.BlockSpec((tm, D_MODEL), lambda i: (i, 0))
    return pl.pallas_call(
        body, name="down_loss", grid=(T // tm,),
        in_specs=[rows, pl.BlockSpec((tm, D_FF), lambda i: (i, 0)), _resident((D_FF, D_MODEL)), rows,
                  pl.BlockSpec((1, D_MODEL), lambda i: (0, 0))],
        out_specs=[rows, pl.BlockSpec((8, 128), lambda i: (0, 0)), pl.BlockSpec((1, D_MODEL), lambda i: (0, 0))],
        out_shape=[jax.ShapeDtypeStruct((T, D_MODEL), F32), jax.ShapeDtypeStruct((8, 128), F32),
                   jax.ShapeDtypeStruct((1, D_MODEL), F32)],
        compiler_params=_params("arbitrary"),
    )(x1, r, w, target, g)


def _down_bwd_act(dx2, r, w, ex=None):
    T = dx2.shape[0]
    tm = ROW_TILE

    def body(dx_ref, r_ref, w_ref, o_ref):
        dxb = dx_ref[...].astype(BF16)
        for c in range(0, D_FF, FF_CHUNK):
            dff = _dot_nt(dxb, w_ref[c:c + FF_CHUNK, :])
            o_ref[:, c:c + FF_CHUNK] = (dff * (2.0 * r_ref[:, c:c + FF_CHUNK].astype(F32))).astype(BF16)

    return _pallas_hosting(
        body, ex, name="down_bwd_act", grid=(T // tm,),
        in_specs=[pl.BlockSpec((tm, D_MODEL), lambda i: (i, 0)),
                  pl.BlockSpec((tm, D_FF), lambda i: (i, 0)),
                  _resident((D_FF, D_MODEL))],
        out_specs=[pl.BlockSpec((tm, D_FF), lambda i: (i, 0))],
        out_shape=[jax.ShapeDtypeStruct((T, D_FF), BF16)],
        scratch_shapes=[], semantics=("parallel",), operands=(dx2, r, w))


def _down_bwd_w(r, dx2):
    T = dx2.shape[0]
    tk = GRAD_ROW_TILE
    nk = T // tk

    def body(r_ref, dx_ref, o_ref, acc_ref):
        k = pl.program_id(0)
        dxb = dx_ref[...].astype(BF16)

        @pl.when(k == 0)
        def _():
            acc_ref[...] = jnp.zeros_like(acc_ref)

        for c in range(0, D_FF, FF_CHUNK):
            acc_ref[:, c:c + FF_CHUNK] += _dot_tn(dxb, _square_bf16(r_ref[:, c:c + FF_CHUNK]))

        @pl.when(k == nk - 1)
        def _():
            o_ref[...] = acc_ref[...].astype(BF16)

    return pl.pallas_call(
        body, name="down_bwd_w", grid=(nk,),
        in_specs=[pl.BlockSpec((tk, D_FF), lambda k: (k, 0)),
                  pl.BlockSpec((tk, D_MODEL), lambda k: (k, 0))],
        out_specs=_resident((D_MODEL, D_FF)),
        out_shape=jax.ShapeDtypeStruct((D_MODEL, D_FF), BF16),
        scratch_shapes=[pltpu.VMEM((D_MODEL, D_FF), F32)],
        compiler_params=_params("arbitrary"),
    )(r, dx2)


def _up_bwd_act(dpre, w, x1, g, dx2, ex=None):
    T = dx2.shape[0]
    tm = ROW_TILE

    def body(dp_ref, w_ref, x_ref, g_ref, dx2_ref, dx1_ref, dg_ref):
        dh = _dot_nt(dp_ref[:, 0:FF_CHUNK], w_ref[:, 0:FF_CHUNK])
        for c in range(FF_CHUNK, D_FF, FF_CHUNK):
            dh = dh + _dot_nt(dp_ref[:, c:c + FF_CHUNK], w_ref[:, c:c + FF_CHUNK])
        gv = g_ref[...]
        _, xhat, rstd = _rms_fwd(x_ref[...], gv)
        dx, dg = _rms_bwd(dh, xhat, rstd, gv)
        dx1_ref[...] = dx2_ref[...] + dx

        @pl.when(pl.program_id(0) == 0)
        def _():
            dg_ref[...] = dg

        @pl.when(pl.program_id(0) != 0)
        def _():
            dg_ref[...] += dg

    return _pallas_hosting(
        body, ex, name="up_bwd_act", grid=(T // tm,),
        in_specs=[pl.BlockSpec((tm, D_FF), lambda i: (i, 0)),
                  _resident((D_MODEL, D_FF)),
                  pl.BlockSpec((tm, D_MODEL), lambda i: (i, 0)),
                  pl.BlockSpec((1, D_MODEL), lambda i: (0, 0)),
                  pl.BlockSpec((tm, D_MODEL), lambda i: (i, 0))],
        out_specs=[pl.BlockSpec((tm, D_MODEL), lambda i: (i, 0)),
                   pl.BlockSpec((1, D_MODEL), lambda i: (0, 0))],
        out_shape=[jax.ShapeDtypeStruct((T, D_MODEL), F32), jax.ShapeDtypeStruct((1, D_MODEL), F32)],
        scratch_shapes=[], semantics=("arbitrary",), operands=(dpre, w, x1, g, dx2))


def _up_bwd_w(h2, dpre):
    T = h2.shape[0]
    tk = GRAD_ROW_TILE
    nk = T // tk

    def body(h_ref, dp_ref, o_ref, acc_ref):
        k = pl.program_id(0)
        hb = h_ref[...]

        @pl.when(k == 0)
        def _():
            acc_ref[...] = jnp.zeros_like(acc_ref)

        for c in range(0, D_FF, FF_CHUNK):
            acc_ref[:, c:c + FF_CHUNK] += _dot_tn(hb, dp_ref[:, c:c + FF_CHUNK])

        @pl.when(k == nk - 1)
        def _():
            o_ref[...] = acc_ref[...].astype(BF16)

    return pl.pallas_call(
        body, name="up_bwd_w", grid=(nk,),
        in_specs=[pl.BlockSpec((tk, D_MODEL), lambda k: (k, 0)),
                  pl.BlockSpec((tk, D_FF), lambda k: (k, 0))],
        out_specs=_resident((D_MODEL, D_FF)),
        out_shape=jax.ShapeDtypeStruct((D_MODEL, D_FF), BF16),
        scratch_shapes=[pltpu.VMEM((D_MODEL, D_FF), F32)],
        compiler_params=_params("arbitrary"),
    )(h2, dpre)


def _outproj_bwd(dx1, mix, w, o, lse):
    T = dx1.shape[0]
    tm = GRAD_ROW_TILE
    nk = T // tm

    def body(dx_ref, mix_ref, w_ref, o_ref, lse_ref, do_ref, dd_ref, dc_ref, dl_ref, dw_ref, acc_ref):
        i = pl.program_id(0)
        dxb = dx_ref[...].astype(BF16)
        dmix = _dot_nt(dxb, w_ref[...])
        al = _attn_alpha(lse_ref[...])
        first = lax.broadcasted_iota(jnp.int32, (tm, 128), 1) < HEAD_DIM
        tot = jnp.zeros((tm, 128), F32)
        for p in range(3):
            sl = slice(p * 128, (p + 1) * 128)
            dy = dmix[:, sl]
            do_ref[:, sl] = dy * al[p]
            prod = dy * o_ref[:, sl]
            s0 = jnp.sum(jnp.where(first, prod, 0.0), axis=-1, keepdims=True)
            s1 = jnp.sum(jnp.where(first, 0.0, prod), axis=-1, keepdims=True)
            tot = tot + al[p] * jnp.where(first, s0, s1)
        for p in range(3):
            dd_ref[:, p * 128:(p + 1) * 128] = -al[p] * tot
        dc_ref[...] = dmix[:, ATTN_W:ATTN_W + CONV_W]
        dl_ref[...] = dmix[:, ATTN_W + CONV_W:D_MODEL]
        part = _dot_tn(mix_ref[...], dxb)

        @pl.when(i == 0)
        def _():
            acc_ref[...] = part

        @pl.when(i != 0)
        def _():
            acc_ref[...] += part

        @pl.when(i == nk - 1)
        def _():
            dw_ref[...] = acc_ref[...].astype(BF16)

    return pl.pallas_call(
        body, name="outproj_bwd", grid=(nk,),
        in_specs=[pl.BlockSpec((tm, D_MODEL), lambda i: (i, 0)),
                  pl.BlockSpec((tm, D_MODEL), lambda i: (i, 0)),
                  _resident((D_MODEL, D_MODEL)),
                  pl.BlockSpec((tm, ATTN_W), lambda i: (i, 0)),
                  pl.BlockSpec((tm, ATTN_W), lambda i: (i, 0))],
        out_specs=[pl.BlockSpec((tm, ATTN_W), lambda i: (i, 0)),
                   pl.BlockSpec((tm, ATTN_W), lambda i: (i, 0)),
                   pl.BlockSpec((tm, CONV_W), lambda i: (i, 0)),
                   pl.BlockSpec((tm, LRU_W), lambda i: (i, 0)),
                   _resident((D_MODEL, D_MODEL))],
        out_shape=[jax.ShapeDtypeStruct((T, ATTN_W), F32), jax.ShapeDtypeStruct((T, ATTN_W), F32),
                   jax.ShapeDtypeStruct((T, CONV_W), F32), jax.ShapeDtypeStruct((T, LRU_W), F32),
                   jax.ShapeDtypeStruct((D_MODEL, D_MODEL), BF16)],
        scratch_shapes=[pltpu.VMEM((D_MODEL, D_MODEL), F32)],
        compiler_params=_params("arbitrary"),
    )(dx1, mix, w, o, lse)


DZ_COLS = ((0, ATTN_W), (ATTN_W, 2 * ATTN_W), (2 * ATTN_W, QKV_W), (QKV_W, QKV_W + CONV_IN_W), (QKV_W + CONV_IN_W, IN_COLS))


def _inproj_bwd_w(dz_parts, h):
    T = h.shape[0]
    tm = GRAD_ROW_TILE
    nk = T // tm
    n_parts = len(DZ_COLS)

    def body(*refs):
        dz_refs = refs[:n_parts]
        h_ref, dw_ref, acc_ref = refs[n_parts:]
        i = pl.program_id(0)
        hb = h_ref[...]

        @pl.when(i == 0)
        def _():
            acc_ref[...] = jnp.zeros_like(acc_ref)

        for r, (lo, hi) in zip(dz_refs, DZ_COLS):
            acc_ref[:, lo:hi] += _dot_tn(hb, r[...].astype(BF16))

        @pl.when(i == nk - 1)
        def _():
            dw_ref[...] = acc_ref[...].astype(BF16)

    rows = lambda width: pl.BlockSpec((tm, width), lambda i: (i, 0))
    return pl.pallas_call(
        body, name="inproj_bwd_w", grid=(nk,),
        in_specs=[rows(hi - lo) for lo, hi in DZ_COLS] + [rows(D_MODEL)],
        out_specs=_resident((D_MODEL, IN_COLS)),
        out_shape=jax.ShapeDtypeStruct((D_MODEL, IN_COLS), BF16),
        scratch_shapes=[pltpu.VMEM((D_MODEL, IN_COLS), F32)],
        compiler_params=_params("arbitrary"),
    )(*dz_parts, h)


def _inproj_bwd_act(dz_parts, w, x2d, g, dx1, ex=None):
    T = x2d.shape[0]
    tm = ROW_TILE
    n_parts = len(DZ_COLS)

    def body(*refs):
        dz_refs = refs[:n_parts]
        w_ref, x_ref, g_ref, dx1_ref, dx_ref, dg_ref = refs[n_parts:]
        dh = _dot_nt(dz_refs[0][...].astype(BF16), w_ref[:, DZ_COLS[0][0]:DZ_COLS[0][1]])
        for r, (lo, hi) in zip(dz_refs[1:], DZ_COLS[1:]):
            dh = dh + _dot_nt(r[...].astype(BF16), w_ref[:, lo:hi])
        gv = g_ref[...]
        _, xhat, rstd = _rms_fwd(x_ref[...], gv)
        dx, dg = _rms_bwd(dh, xhat, rstd, gv)
        dx_ref[...] = dx1_ref[...] + dx

        @pl.when(pl.program_id(0) == 0)
        def _():
            dg_ref[...] = dg

        @pl.when(pl.program_id(0) != 0)
        def _():
            dg_ref[...] += dg

    rows = lambda width: pl.BlockSpec((tm, width), lambda i: (i, 0))
    return _pallas_hosting(
        body, ex, name="inproj_bwd_act", grid=(T // tm,),
        in_specs=[rows(hi - lo) for lo, hi in DZ_COLS] + [
            _resident((D_MODEL, IN_COLS)), rows(D_MODEL), pl.BlockSpec((1, D_MODEL), lambda i: (0, 0)), rows(D_MODEL)],
        out_specs=[rows(D_MODEL), pl.BlockSpec((1, D_MODEL), lambda i: (0, 0))],
        out_shape=[jax.ShapeDtypeStruct((T, D_MODEL), F32), jax.ShapeDtypeStruct((1, D_MODEL), F32)],
        scratch_shapes=[], semantics=("arbitrary",), operands=(*dz_parts, w, x2d, g, dx1))


def _alibi_coef():
    slopes = 2.0 ** (-8.0 * np.arange(1, 7) / 6)
    return jnp.asarray((slopes.reshape(3, 2) * np.asarray(ATTN_DILATIONS)[:, None]).astype(np.float32))


def _unit_rows(u, d):
    nb = N_UNITS // d
    r, n = u // nb, u % nb
    span = ATTN_BLOCK * d

    def rows(block):
        start = block * span + r
        return pl.ds(pl.multiple_of(start, ATTN_BLOCK), ATTN_BLOCK) if d == 1 else pl.ds(start, ATTN_BLOCK, stride=d)

    return rows(n), rows(jnp.maximum(n - 1, 0)), rows(jnp.minimum(n + 1, nb - 1)), n > 0, n + 1 < nb


def _per_pattern(fn):
    for p, d in enumerate(ATTN_DILATIONS):
        pl.when(pl.program_id(1) == p)(functools.partial(fn, p, d))


def _attn_col(offset):
    return pl.BlockSpec((None, SEQ_LEN, 128), lambda b, p: (b, 0, p + offset))


def _attn_masks():
    qi = lax.broadcasted_iota(jnp.int32, (ATTN_BLOCK, 2 * ATTN_BLOCK), 0)
    kj = lax.broadcasted_iota(jnp.int32, (ATTN_BLOCK, 2 * ATTN_BLOCK), 1)
    dist = qi + ATTN_BLOCK - kj
    first = lax.broadcasted_iota(jnp.int32, (ATTN_BLOCK, 128), 1) < HEAD_DIM
    return dist.astype(F32), (dist >= 0) & (dist <= ATTN_BLOCK), kj >= ATTN_BLOCK, first


def _head_lanes(a, first, j):
    return jnp.where(first if j == 0 else jnp.logical_not(first), a, jnp.zeros_like(a))


def _load_kv(ref, prev, own):
    return jnp.concatenate([ref[prev, :], ref[own, :]], axis=0).astype(BF16)


def _attn_fwd(qkv, ex=None):
    B = qkv.shape[0]

    def body(coef_ref, q_ref, k_ref, v_ref, o_ref, lse_ref):
        dist, band, own_half, first = _attn_masks()

        def pattern(p, d):
            def unit(u, carry):
                own, prev, _, has_prev, _ = _unit_rows(u, d)
                ok = band & jnp.logical_or(own_half, has_prev)
                q = q_ref[own, :].astype(BF16)
                kcat, vcat = _load_kv(k_ref, prev, own), _load_kv(v_ref, prev, own)
                outs, lses = [], []
                for j in range(2):
                    s = jnp.where(ok, _dot_nt(_head_lanes(q, first, j), kcat) * 0.125 - coef_ref[p, j] * dist, NEG_BIG)
                    m = jnp.max(s, axis=-1, keepdims=True)
                    e = jnp.exp(s - m)
                    l = jnp.sum(e, axis=-1, keepdims=True)
                    outs.append(_dot(e.astype(BF16), vcat) * (1.0 / l))
                    lses.append(m + jnp.log(l))
                o_ref[own, :] = jnp.where(first, outs[0], outs[1])
                lse_ref[own, :] = jnp.where(first, lses[0], lses[1])
                return carry

            lax.fori_loop(0, N_UNITS, unit, 0, unroll=UNIT_UNROLL)

        _per_pattern(pattern)

    shape = jax.ShapeDtypeStruct((B, SEQ_LEN, ATTN_W), F32)
    return _pallas_hosting(
        body, ex, name="attn_fwd", grid=(B, 3),
        in_specs=[pl.BlockSpec(memory_space=pltpu.SMEM), _attn_col(0), _attn_col(3), _attn_col(6)],
        out_specs=[_attn_col(0), _attn_col(0)],
        out_shape=[shape, shape],
        scratch_shapes=[], semantics=("parallel", "parallel"), operands=(_alibi_coef(), qkv, qkv, qkv))


def _attn_bwd(qkv, do, lse, dd, ex=None):
    B = qkv.shape[0]

    def body(coef_ref, q_ref, k_ref, v_ref, do_ref, lse_ref, dd_ref, dq_ref, dk_ref, dv_ref):
        dist, band, own_half, first = _attn_masks()

        def pattern(p, d):
            def unit(u, carry):
                own, prev, _, has_prev, _ = _unit_rows(u, d)
                ok = band & jnp.logical_or(own_half, has_prev)
                q, do = q_ref[own, :].astype(BF16), do_ref[own, :].astype(BF16)
                kcat, vcat = _load_kv(k_ref, prev, own), _load_kv(v_ref, prev, own)
                lse_a, dd_a = lse_ref[own, :], dd_ref[own, :]
                dqs, dks, dvs = [], [], []
                for j in range(2):
                    col = slice(HEAD_DIM * j, HEAD_DIM * j + 1)
                    s = _dot_nt(_head_lanes(q, first, j), kcat) * 0.125 - coef_ref[p, j] * dist
                    pr = jnp.where(ok, jnp.exp(jnp.where(ok, s, NEG_BIG) - lse_a[:, col]), 0.0)
                    ds = (pr * (_dot_nt(_head_lanes(do, first, j), vcat) + dd_a[:, col])).astype(BF16)
                    dqs.append(_dot(ds, kcat))
                    dks.append(_dot_tn(ds, q))
                    dvs.append(_dot_tn(pr.astype(BF16), do))
                both = lambda pair: jnp.where(jnp.concatenate([first] * (pair[0].shape[0] // ATTN_BLOCK), axis=0), *pair)
                dq_ref[own, :] = both(dqs) * 0.125
                dk, dv = both(dks) * 0.125, both(dvs)
                dk_ref[own, :] = dk[ATTN_BLOCK:]
                dv_ref[own, :] = dv[ATTN_BLOCK:]
                dk_ref[prev, :] += dk[:ATTN_BLOCK]
                dv_ref[prev, :] += dv[:ATTN_BLOCK]
                return carry

            lax.fori_loop(0, N_UNITS, unit, 0, unroll=UNIT_UNROLL)

        _per_pattern(pattern)

    shape = jax.ShapeDtypeStruct((B, SEQ_LEN, ATTN_W), F32)
    return _pallas_hosting(
        body, ex, name="attn_bwd", grid=(B, 3),
        in_specs=[pl.BlockSpec(memory_space=pltpu.SMEM), _attn_col(0), _attn_col(3), _attn_col(6), _attn_col(0), _attn_col(0),
                  _attn_col(0)],
        out_specs=[_attn_col(0)] * 3,
        out_shape=[shape] * 3,
        scratch_shapes=[], semantics=("parallel", "parallel"), operands=(_alibi_coef(), qkv, qkv, qkv, do, lse, dd))


def _for_chunks(n_rows, fn, chunk):
    def step(c, carry):
        fn(pl.multiple_of(c * chunk, chunk))
        return carry

    lax.fori_loop(0, n_rows // chunk, step, 0)


def _shift_down(win, s, rows):
    lead = win.shape[0] - rows
    if s == 0:
        return win[lead:]
    if s % 8 == 0:
        return win[lead - s:lead - s + rows]
    q, r = divmod(s, 8)
    rolled = pltpu.roll(win, r, 0)
    return rolled[lead - 8 * q:lead - 8 * q + rows]


def _tap_shifts(n_taps):
    return [(r, [(n_taps - 1 - (8 * q + r), 8 * q) for q in range((n_taps - 1 - r) // 8 + 1)]) for r in range(min(8, n_taps))]


def _rotated_down(win, r):
    return win if r == 0 else pltpu.roll(win, r, 0)


def _rotated_up(win, r):
    return win if r == 0 else pltpu.roll(win, win.shape[0] - r, 0)


def _shift_up(win, s, rows):
    if s % 8 == 0:
        return win[s:s + rows]
    q, r = divmod(s, 8)
    rolled = pltpu.roll(win, win.shape[0] - r, 0)
    return rolled[8 * q:8 * q + rows]


CONV_PAD = 32


def _ln_silu(c, lg, lb):
    mu = jnp.mean(c, axis=-1, keepdims=True)
    cc = c - mu
    rstd = lax.rsqrt(jnp.mean(cc * cc, axis=-1, keepdims=True) + LN_EPS)
    nrm = cc * rstd
    v = nrm * lg + lb
    sg = _sigmoid(v)
    return v * sg, nrm, rstd, v, sg


def _conv_fwd(ci, w, b, lg, lb):
    B, S, _ = ci.shape
    CH = CONV_FWD_CHUNK

    def body(ci_ref, w_ref, b_ref, lg_ref, lb_ref, y_ref, c_ref, pad_ref):
        pad_ref[0:CONV_PAD, :] = jnp.zeros((CONV_PAD, CONV_W), F32)

        def glu(base):
            blk = ci_ref[pl.ds(base, CH), :]
            pad_ref[pl.ds(CONV_PAD + base, CH), :] = blk[:, 0:CONV_W] * _sigmoid(blk[:, CONV_W:])

        _for_chunks(S, glu, CH)

        def conv(base):
            win = pad_ref[pl.ds(base, CH + CONV_PAD), :]
            acc = jnp.broadcast_to(b_ref[...], (CH, CONV_W))
            for r, taps in _tap_shifts(CONV_TAPS):
                rot = _rotated_down(win, r)
                for k, off in taps:
                    acc = acc + w_ref[k:k + 1, :] * rot[CONV_PAD - off:CONV_PAD - off + CH]
            c_ref[pl.ds(base, CH), :] = acc
            y, _, _, _, _ = _ln_silu(acc, lg_ref[...], lb_ref[...])
            y_ref[pl.ds(base, CH), :] = y.astype(BF16)

        _for_chunks(S, conv, CH)

    vec = pl.BlockSpec((1, CONV_W), lambda i: (0, 0))
    return pl.pallas_call(
        body, name="conv_fwd", grid=(B,),
        in_specs=[pl.BlockSpec((None, S, CONV_IN_W), lambda i: (i, 0, 0)),
                  pl.BlockSpec((CONV_TAPS, CONV_W), lambda i: (0, 0)), vec, vec, vec],
        out_specs=[pl.BlockSpec((None, S, CONV_W), lambda i: (i, 0, 0)),
                   pl.BlockSpec((None, S, CONV_W), lambda i: (i, 0, 0))],
        out_shape=[jax.ShapeDtypeStruct((B, S, CONV_W), BF16), jax.ShapeDtypeStruct((B, S, CONV_W), F32)],
        scratch_shapes=[pltpu.VMEM((S + CONV_PAD, CONV_W), F32)],
        compiler_params=_params("parallel"),
    )(ci, w, b, lg, lb)


def _conv_bwd(ci, cpre, dy, w, lg, lb, ex=None):
    B, S, _ = ci.shape
    CH = CONV_BWD_CHUNK

    def body(ci_ref, c_ref, dy_ref, w_ref, lg_ref, lb_ref, dci_ref, dw_ref, db_ref, dlg_ref, dlb_ref, upad_ref, dcpad_ref,
             dwacc_ref):
        @pl.when(pl.program_id(0) == 0)
        def _():
            dw_ref[...] = jnp.zeros_like(dw_ref)
            db_ref[...] = jnp.zeros_like(db_ref)
            dlg_ref[...] = jnp.zeros_like(dlg_ref)
            dlb_ref[...] = jnp.zeros_like(dlb_ref)

        upad_ref[0:CONV_PAD, :] = jnp.zeros((CONV_PAD, CONV_W), F32)
        dcpad_ref[S:S + CONV_PAD, :] = jnp.zeros((CONV_PAD, CONV_W), F32)
        dwacc_ref[...] = jnp.zeros_like(dwacc_ref)

        def norm_bwd(base):
            blk = ci_ref[pl.ds(base, CH), :]
            upad_ref[pl.ds(CONV_PAD + base, CH), :] = blk[:, 0:CONV_W] * _sigmoid(blk[:, CONV_W:])
            lgv = lg_ref[...]
            _, nrm, rstd, v, sg = _ln_silu(c_ref[pl.ds(base, CH), :], lgv, lb_ref[...])
            dv = dy_ref[pl.ds(base, CH), :] * (sg * (1.0 + v * (1.0 - sg)))
            dlg_ref[...] += jnp.sum(dv * nrm, axis=0, keepdims=True)
            dlb_ref[...] += jnp.sum(dv, axis=0, keepdims=True)
            dn = dv * lgv
            dc = rstd * (dn - jnp.mean(dn, axis=-1, keepdims=True) - nrm * jnp.mean(dn * nrm, axis=-1, keepdims=True))
            dcpad_ref[pl.ds(base, CH), :] = dc
            db_ref[...] += jnp.sum(dc, axis=0, keepdims=True)

        _for_chunks(S, norm_bwd, CH)

        def conv_bwd(base):
            dwin = dcpad_ref[pl.ds(base, CH + CONV_PAD), :]
            uwin = upad_ref[pl.ds(base, CH + CONV_PAD), :]
            dc = dwin[0:CH]
            du = jnp.zeros((CH, CONV_W), F32)
            for r, taps in _tap_shifts(CONV_TAPS):
                d_rot, u_rot = _rotated_up(dwin, r), _rotated_down(uwin, r)
                for k, off in taps:
                    du = du + w_ref[k:k + 1, :] * d_rot[off:off + CH]
                    prod = dc * u_rot[CONV_PAD - off:CONV_PAD - off + CH]
                    dwacc_ref[8 * k:8 * k + 8, :] += jnp.sum(prod.reshape(CH // 8, 8, CONV_W), axis=0)
            blk = ci_ref[pl.ds(base, CH), :]
            a, sg = blk[:, 0:CONV_W], _sigmoid(blk[:, CONV_W:])
            dci_ref[pl.ds(base, CH), 0:CONV_W] = (du * sg).astype(BF16)
            dci_ref[pl.ds(base, CH), CONV_W:] = (du * a * sg * (1.0 - sg)).astype(BF16)

        _for_chunks(S, conv_bwd, CH)
        for k in range(CONV_TAPS):
            dw_ref[k:k + 1, :] += jnp.sum(dwacc_ref[8 * k:8 * k + 8, :], axis=0, keepdims=True)

    vec = pl.BlockSpec((1, CONV_W), lambda i: (0, 0))
    mat = pl.BlockSpec((CONV_TAPS, CONV_W), lambda i: (0, 0))
    seq = lambda width: pl.BlockSpec((None, S, width), lambda i: (i, 0, 0))
    return _pallas_hosting(
        body, ex, name="conv_bwd", grid=(B,),
        in_specs=[seq(CONV_IN_W), seq(CONV_W), seq(CONV_W), mat, vec, vec],
        out_specs=[seq(CONV_IN_W), mat, vec, vec, vec],
        out_shape=[jax.ShapeDtypeStruct((B, S, CONV_IN_W), BF16), jax.ShapeDtypeStruct((CONV_TAPS, CONV_W), F32),
                   jax.ShapeDtypeStruct((1, CONV_W), F32), jax.ShapeDtypeStruct((1, CONV_W), F32),
                   jax.ShapeDtypeStruct((1, CONV_W), F32)],
        scratch_shapes=[pltpu.VMEM((S + CONV_PAD, CONV_W), F32), pltpu.VMEM((S + CONV_PAD, CONV_W), F32),
                        pltpu.VMEM((8 * CONV_TAPS, CONV_W), F32)],
        semantics=("arbitrary",), operands=(ci, cpre, dy, w, lg, lb))


SCAN_SHIFTS = tuple(1 << e for e in range(11))


def _prev8(ref, base, cols, fill):
    start = pl.multiple_of(jnp.maximum(base - 8, 0), 8)
    return jnp.where(base > 0, ref[pl.ds(start, 8), cols], fill)


def _next8(ref, base, rows, total, cols, fill):
    start = pl.multiple_of(jnp.minimum(base + rows, total - 8), 8)
    return jnp.where(base + rows < total, ref[pl.ds(start, 8), cols], fill)


ALL = slice(None)
LRU_X = slice(LRU_W, LRU_IN_W)
LRU_GATE = slice(0, LRU_W)


def _lru_conv(li_ref, base, rows, cw_ref, cb_ref):
    win = jnp.concatenate([_prev8(li_ref, base, LRU_X, 0.0), li_ref[pl.ds(base, rows), LRU_X]], axis=0)
    u = jnp.broadcast_to(cb_ref[...], (rows, LRU_W))
    for k in range(LRU_TAPS):
        u = u + cw_ref[k:k + 1, :] * _shift_down(win, LRU_TAPS - 1 - k, rows)
    return u, win


def _lru_gates(u, wa_ref, ba_ref, wx_ref, bx_ref, sp):
    ub = u.astype(BF16)
    r = _sigmoid(_dot(ub, wa_ref[...]) + ba_ref[...])
    i = _sigmoid(_dot(ub, wx_ref[...]) + bx_ref[...])
    la = (-LRU_C) * r * sp
    a = jnp.exp(la)
    return ub, r, i, a, _one_minus_exp(2.0 * la, a * a)


def _scan_forward(bufs, S, CH):
    for n, s in enumerate(SCAN_SHIFTS):
        (sa, sb), (da, db) = bufs[n % 2], bufs[(n + 1) % 2]

        def step(base, s=s, sa=sa, sb=sb, da=da, db=db):
            a, b = sa[pl.ds(base, CH), :], sb[pl.ds(base, CH), :]
            if s < 8:
                a_s = _shift_down(jnp.concatenate([_prev8(sa, base, ALL, 1.0), a], axis=0), s, CH)
                b_s = _shift_down(jnp.concatenate([_prev8(sb, base, ALL, 0.0), b], axis=0), s, CH)
            elif s < CH:
                start = pl.multiple_of(jnp.maximum(base - s, 0), 8)
                a_s = jnp.concatenate([jnp.where(base > 0, sa[pl.ds(start, s), :], 1.0), a[0:CH - s]], axis=0)
                b_s = jnp.concatenate([jnp.where(base > 0, sb[pl.ds(start, s), :], 0.0), b[0:CH - s]], axis=0)
            else:
                start = pl.multiple_of(jnp.maximum(base - s, 0), 8)
                a_s = jnp.where(base < s, 1.0, sa[pl.ds(start, CH), :])
                b_s = jnp.where(base < s, 0.0, sb[pl.ds(start, CH), :])
            db[pl.ds(base, CH), :] = a * b_s + b
            da[pl.ds(base, CH), :] = a * a_s

        _for_chunks(S, step, CH)
    return len(SCAN_SHIFTS) % 2


def _scan_backward(bufs, S, CH):
    for n, s in enumerate(SCAN_SHIFTS):
        (sa, sb), (da, db) = bufs[n % 2], bufs[(n + 1) % 2]

        def step(base, s=s, sa=sa, sb=sb, da=da, db=db):
            a, b = sa[pl.ds(base, CH), :], sb[pl.ds(base, CH), :]
            if s < 8:
                a_s = _shift_up(jnp.concatenate([a, _next8(sa, base, CH, S, ALL, 1.0)], axis=0), s, CH)
                b_s = _shift_up(jnp.concatenate([b, _next8(sb, base, CH, S, ALL, 0.0)], axis=0), s, CH)
            elif s < CH:
                start = pl.multiple_of(jnp.minimum(base + CH, S - s), 8)
                more = base + CH < S
                a_s = jnp.concatenate([a[s:CH], jnp.where(more, sa[pl.ds(start, s), :], 1.0)], axis=0)
                b_s = jnp.concatenate([b[s:CH], jnp.where(more, sb[pl.ds(start, s), :], 0.0)], axis=0)
            else:
                start = pl.multiple_of(jnp.minimum(base + s, S - CH), 8)
                a_s = jnp.where(base + s >= S, 1.0, sa[pl.ds(start, CH), :])
                b_s = jnp.where(base + s >= S, 0.0, sb[pl.ds(start, CH), :])
            db[pl.ds(base, CH), :] = a * b_s + b
            da[pl.ds(base, CH), :] = a * a_s

        _for_chunks(S, step, CH)
    return len(SCAN_SHIFTS) % 2


def _lru_fwd(li, cw, cb, wa, ba, wx, bx, lam, ex=None):
    B, S, _ = li.shape
    CH = LRU_FWD_CHUNK

    def body(li_ref, cw_ref, cb_ref, wa_ref, ba_ref, wx_ref, bx_ref, lam_ref, y_ref, h_ref, a0, b0, a1, b1):
        sp = _softplus(-lam_ref[...])

        def gates(base):
            u, _ = _lru_conv(li_ref, base, CH, cw_ref, cb_ref)
            _, _, i, a, em = _lru_gates(u, wa_ref, ba_ref, wx_ref, bx_ref, sp)
            a0[pl.ds(base, CH), :] = a
            b0[pl.ds(base, CH), :] = jnp.sqrt(em) * (i * u)

        _for_chunks(S, gates, CH)
        bufs = ((a0, b0), (a1, b1))
        hb = bufs[_scan_forward(bufs, S, CH)][1]

        def out(base):
            h = hb[pl.ds(base, CH), :]
            h_ref[pl.ds(base, CH), :] = h
            gl, _ = _gelu(li_ref[pl.ds(base, CH), LRU_GATE])
            y_ref[pl.ds(base, CH), :] = (gl * h).astype(BF16)

        _for_chunks(S, out, CH)

    vec = pl.BlockSpec((1, LRU_W), lambda i: (0, 0))
    mat = pl.BlockSpec((LRU_W, LRU_W), lambda i: (0, 0))
    seq = lambda width: pl.BlockSpec((None, S, width), lambda i: (i, 0, 0))
    return _pallas_hosting(
        body, ex, name="lru_fwd", grid=(B,),
        in_specs=[seq(LRU_IN_W), pl.BlockSpec((LRU_TAPS, LRU_W), lambda i: (0, 0)), vec, mat, vec, mat, vec, vec],
        out_specs=[seq(LRU_W), seq(LRU_W)],
        out_shape=[jax.ShapeDtypeStruct((B, S, LRU_W), BF16), jax.ShapeDtypeStruct((B, S, LRU_W), F32)],
        scratch_shapes=[pltpu.VMEM((S, LRU_W), F32)] * 4,
        semantics=("parallel",), operands=(li, cw, cb, wa, ba, wx, bx, lam))


def _lru_bwd(li, hs, dy, cw, cb, wa, ba, wx, bx, lam, ex=None):
    B, S, _ = li.shape
    CH = LRU_BWD_CHUNK

    def body(li_ref, hs_ref, dy_ref, cw_ref, cb_ref, wa_ref, ba_ref, wx_ref, bx_ref, lam_ref,
             dli_ref, dcw_ref, dcb_ref, dwa_ref, dba_ref, dwx_ref, dbx_ref, dlam_ref, a0, b0, a1, b1, u_s, du_s):
        @pl.when(pl.program_id(0) == 0)
        def _():
            for ref in (dcw_ref, dcb_ref, dwa_ref, dba_ref, dwx_ref, dbx_ref, dlam_ref):
                ref[...] = jnp.zeros_like(ref)

        lam_v = lam_ref[...]
        sp = _softplus(-lam_v)
        dsp_dlam = -_sigmoid(-lam_v)

        def gates(base):
            u, _ = _lru_conv(li_ref, base, CH, cw_ref, cb_ref)
            _, _, _, a, _ = _lru_gates(u, wa_ref, ba_ref, wx_ref, bx_ref, sp)
            gl, _ = _gelu(li_ref[pl.ds(base, CH), LRU_GATE])
            u_s[pl.ds(base, CH), :] = u
            a0[pl.ds(base, CH), :] = a
            b0[pl.ds(base, CH), :] = a * (dy_ref[pl.ds(base, CH), :] * gl)

        _for_chunks(S, gates, CH)
        bufs = ((a0, b0), (a1, b1))
        eb = bufs[_scan_backward(bufs, S, CH)][1]

        def grads(base):
            e = eb[pl.ds(base, CH), :]
            e_next = _shift_up(jnp.concatenate([e, _next8(eb, base, CH, S, ALL, 0.0)], axis=0), 1, CH)
            gate = li_ref[pl.ds(base, CH), LRU_GATE]
            gl, th = _gelu(gate)
            dy = dy_ref[pl.ds(base, CH), :]
            g = dy * gl + e_next
            h = hs_ref[pl.ds(base, CH), :]
            h_prev = _shift_down(jnp.concatenate([_prev8(hs_ref, base, ALL, 0.0), h], axis=0), 1, CH)
            dli_ref[pl.ds(base, CH), LRU_GATE] = (dy * h * _gelu_grad(gate, th)).astype(BF16)
            u = u_s[pl.ds(base, CH), :]
            ub, r, i, a, em = _lru_gates(u, wa_ref, ba_ref, wx_ref, bx_ref, sp)
            mult = jnp.sqrt(em)
            da = g * h_prev
            dmult = g * (i * u)
            di = g * mult * u
            dla = da * a - dmult * (a * a) * lax.rsqrt(jnp.maximum(em, 1e-30))
            dlam_ref[...] += dsp_dlam * jnp.sum(dla * ((-LRU_C) * r), axis=0, keepdims=True)
            dpa = (dla * ((-LRU_C) * sp)) * r * (1.0 - r)
            dpx = di * i * (1.0 - i)
            dpab, dpxb = dpa.astype(BF16), dpx.astype(BF16)
            dwa_ref[...] += _dot_tn(ub, dpab)
            dwx_ref[...] += _dot_tn(ub, dpxb)
            dba_ref[...] += jnp.sum(dpa, axis=0, keepdims=True)
            dbx_ref[...] += jnp.sum(dpx, axis=0, keepdims=True)
            du = g * mult * i + _dot_nt(dpab, wa_ref[...]) + _dot_nt(dpxb, wx_ref[...])
            du_s[pl.ds(base, CH), :] = du
            dcb_ref[...] += jnp.sum(du, axis=0, keepdims=True)

        _for_chunks(S, grads, CH)

        def conv_bwd(base):
            du = du_s[pl.ds(base, CH), :]
            dwin = jnp.concatenate([du, _next8(du_s, base, CH, S, ALL, 0.0)], axis=0)
            xwin = jnp.concatenate([_prev8(li_ref, base, LRU_X, 0.0), li_ref[pl.ds(base, CH), LRU_X]], axis=0)
            dx = jnp.zeros((CH, LRU_W), F32)
            for k in range(LRU_TAPS):
                dx = dx + cw_ref[k:k + 1, :] * _shift_up(dwin, LRU_TAPS - 1 - k, CH)
                dcw_ref[k:k + 1, :] += jnp.sum(du * _shift_down(xwin, LRU_TAPS - 1 - k, CH), axis=0, keepdims=True)
            dli_ref[pl.ds(base, CH), LRU_X] = dx.astype(BF16)

        _for_chunks(S, conv_bwd, CH)

    vec = pl.BlockSpec((1, LRU_W), lambda i: (0, 0))
    mat = pl.BlockSpec((LRU_W, LRU_W), lambda i: (0, 0))
    taps = pl.BlockSpec((LRU_TAPS, LRU_W), lambda i: (0, 0))
    seq = lambda width: pl.BlockSpec((None, S, width), lambda i: (i, 0, 0))
    vec_shape = jax.ShapeDtypeStruct((1, LRU_W), F32)
    mat_shape = jax.ShapeDtypeStruct((LRU_W, LRU_W), F32)
    return _pallas_hosting(
        body, ex, name="lru_bwd", grid=(B,),
        in_specs=[seq(LRU_IN_W), seq(LRU_W), seq(LRU_W), taps, vec, mat, vec, mat, vec, vec],
        out_specs=[seq(LRU_IN_W), taps, vec, mat, vec, mat, vec, vec],
        out_shape=[jax.ShapeDtypeStruct((B, S, LRU_IN_W), BF16), jax.ShapeDtypeStruct((LRU_TAPS, LRU_W), F32),
                   vec_shape, mat_shape, vec_shape, mat_shape, vec_shape, vec_shape],
        scratch_shapes=[pltpu.VMEM((S, LRU_W), F32)] * 6,
        semantics=("arbitrary",), operands=(li, hs, dy, cw, cb, wa, ba, wx, bx, lam))


MESH = pl.DeviceIdType.MESH
HBM_SPEC = pl.BlockSpec(memory_space=pltpu.HBM)


def _slot(ref, p):
    return ref.at[p]


def _row_block(rows):
    return lambda ref, p: ref.at[pl.ds(p * rows, rows), :]


def _col_block(cols):
    return lambda ref, p: ref.at[:, pl.ds(p * cols, cols)]


class _Gather:
    def __init__(self, blocks, out_shapes, places):
        self.sources, self.out_shapes, self.places, self.n = list(blocks), list(out_shapes), list(places), len(blocks)

    def scratch(self):
        return [pltpu.SemaphoreType.DMA((self.n, 7)), pltpu.SemaphoreType.DMA((self.n, 7)), pltpu.SemaphoreType.DMA((self.n,))]

    def _plan(self, x_refs, out_refs, send_sems, recv_sems, local_sems):
        n = self.n
        x, y, c = lax.axis_index("x"), lax.axis_index("y"), lax.axis_index("c")
        me, sibling = (x, y, c), (x, y, 1 - c)
        chips = [(1 - x, y), (x, 1 - y), (1 - x, 1 - y)]

        def place(a, dev):
            return self.places[a](out_refs[a], 4 * dev[0] + 2 * dev[1] + dev[2])

        def copy(a, k, blk, to, src=None):
            return pltpu.make_async_remote_copy(
                src_ref=place(a, blk) if src is None else src, dst_ref=place(a, blk),
                send_sem=send_sems.at[a, k], recv_sem=recv_sems.at[a, k], device_id=to, device_id_type=MESH)

        mine = [pltpu.make_async_copy(x_refs[a], place(a, me), local_sems.at[a]) for a in range(n)]
        first = [copy(a, 0, me, sibling, src=x_refs[a]) for a in range(n)]
        first += [copy(a, 1 + j, me, (*chip, c), src=x_refs[a]) for j, chip in enumerate(chips) for a in range(n)]
        return me, sibling, chips, c, copy, mine, first

    def start(self, *refs):
        *_, mine, first = self._plan(*refs)
        for cp in mine + first:
            cp.start()

    def forward(self, *refs):
        me, sibling, chips, c, copy, _, _ = self._plan(*refs)
        for j, chip in enumerate(chips):
            for a in range(self.n):
                copy(a, 1 + j, (*chip, c), me).wait_recv()
                copy(a, 4 + j, (*chip, c), sibling).start()

    def finish(self, *refs):
        me, sibling, chips, c, copy, mine, first = self._plan(*refs)
        passed = [copy(a, 4 + j, (*chip, c), sibling) for j, chip in enumerate(chips) for a in range(self.n)]
        for a in range(self.n):
            copy(a, 0, sibling, me).wait_recv()
        for j, chip in enumerate(chips):
            for a in range(self.n):
                copy(a, 4 + j, (*chip, 1 - c), me).wait_recv()
        for cp in first + passed:
            cp.wait_send()
        for cp in mine:
            cp.wait()


class _GradExchange:
    def __init__(self, sources, takes, piece_shapes):
        self.sources, self.takes, self.n = list(sources), list(takes), len(sources)
        self.out_shapes = [jax.ShapeDtypeStruct((N_DEV,) + tuple(s), BF16) for s in piece_shapes]

    def scratch(self):
        return [pltpu.SemaphoreType.DMA((self.n, 7)), pltpu.SemaphoreType.DMA((self.n, 7)), pltpu.SemaphoreType.DMA((self.n,))]

    def _copies(self, src_refs, out_refs, send_sems, recv_sems, local_sems):
        x, y, c = lax.axis_index("x"), lax.axis_index("y"), lax.axis_index("c")
        me = 4 * x + 2 * y + c
        mine = [pltpu.make_async_copy(self.takes[i](src_refs[i], me), out_refs[i].at[me], local_sems.at[i]) for i in range(self.n)]
        remote = []
        for k in range(1, N_DEV):
            px, py, pc = x ^ ((k >> 2) & 1), y ^ ((k >> 1) & 1), c ^ (k & 1)
            peer = 4 * px + 2 * py + pc
            for i in range(self.n):
                remote.append(pltpu.make_async_remote_copy(
                    src_ref=self.takes[i](src_refs[i], peer), dst_ref=out_refs[i].at[me], send_sem=send_sems.at[i, k - 1],
                    recv_sem=recv_sems.at[i, k - 1], device_id=(px, py, pc), device_id_type=MESH))
        return mine, remote

    def start(self, *refs):
        mine, remote = self._copies(*refs)
        for cp in mine + remote:
            cp.start()

    def forward(self, *refs):
        pass

    def finish(self, *refs):
        mine, remote = self._copies(*refs)
        for cp in remote:
            cp.wait_recv()
        for cp in remote:
            cp.wait_send()
        for cp in mine:
            cp.wait()


def _run_exchange(ex, name):
    def body(*refs):
        src_refs, out_refs, sems = refs[:ex.n], refs[ex.n:2 * ex.n], refs[2 * ex.n:]
        ex.start(src_refs, out_refs, *sems)
        ex.forward(src_refs, out_refs, *sems)
        ex.finish(src_refs, out_refs, *sems)

    return pl.pallas_call(
        body, name=name, out_shape=ex.out_shapes, in_specs=[HBM_SPEC] * ex.n, out_specs=[HBM_SPEC] * ex.n,
        scratch_shapes=ex.scratch(),
    )(*ex.sources)


def _pallas_hosting(body, ex, *, name, grid, in_specs, out_specs, out_shape, scratch_shapes, semantics, operands):
    if ex is None:
        outs = pl.pallas_call(body, name=name, grid=grid, in_specs=in_specs, out_specs=out_specs, out_shape=out_shape,
                              scratch_shapes=scratch_shapes, compiler_params=_params(*semantics))(*operands)
        return outs, None
    n_in, n_out, n_scr, n = len(in_specs), len(out_specs), len(scratch_shapes), ex.n

    def at_step(pick):
        conds = [pl.program_id(k) == pick(size) for k, size in enumerate(grid)]
        return functools.reduce(jnp.logical_and, conds)

    def hosting(*refs):
        ins, ex_ins = refs[:n_in], refs[n_in:n_in + n]
        outs, ex_outs = refs[n_in + n:n_in + n + n_out], refs[n_in + n + n_out:n_in + 2 * n + n_out]
        scratch, sems = refs[n_in + 2 * n + n_out:n_in + 2 * n + n_out + n_scr], refs[n_in + 2 * n + n_out + n_scr:]
        pl.when(at_step(lambda size: 0))(lambda: ex.start(ex_ins, ex_outs, *sems))
        pl.when(at_step(lambda size: size - 1))(lambda: ex.forward(ex_ins, ex_outs, *sems))
        body(*ins, *outs, *scratch)
        pl.when(at_step(lambda size: size - 1))(lambda: ex.finish(ex_ins, ex_outs, *sems))

    res = pl.pallas_call(
        hosting, name=name, grid=grid, in_specs=list(in_specs) + [HBM_SPEC] * n, out_specs=list(out_specs) + [HBM_SPEC] * n,
        out_shape=list(out_shape) + ex.out_shapes, scratch_shapes=list(scratch_shapes) + ex.scratch(),
        compiler_params=_params(*(["arbitrary"] * len(grid))),
    )(*operands, *ex.sources)
    return res[:n_out], res[n_out:]


def _adamw(w, g, m, v, rows_per_step, name):
    R, C = w.shape

    def body(w_ref, g_ref, m_ref, v_ref, d_ref, nm_ref, nv_ref):
        d_ref[...], nm_ref[...], nv_ref[...] = _adam_update(w_ref[...], g_ref[...], m_ref[...], v_ref[...])

    spec = pl.BlockSpec((rows_per_step, C), lambda i: (i, 0))
    shape = jax.ShapeDtypeStruct((R, C), F32)
    return pl.pallas_call(
        body, name=name, grid=(R // rows_per_step,),
        in_specs=[spec] * 4, out_specs=[spec] * 3, out_shape=[shape] * 3,
        compiler_params=_params("parallel"),
    )(w, g, m, v)


def _adam_update(w, g, m, v):
    nm = ADAM_B1 * m + (1.0 - ADAM_B1) * g
    nv = ADAM_B2 * v + (1.0 - ADAM_B2) * (g * g)
    c1 = 1.0 - ADAM_B1 ** ADAM_STEP
    c2 = 1.0 - ADAM_B2 ** ADAM_STEP
    return (-ADAM_LR) * ((nm / c1) / (jnp.sqrt(nv / c2) + ADAM_EPS) + ADAM_WD * w), nm, nv


def _small_sum_adamw(gathered, vectors, vector_widths, loss_slots, state):
    names, updated = list(vector_widths) + list(gathered), list(state)
    n_in, n_g, n_u = len(gathered) + 1, len(names), len(updated)

    def slot_sum(ref):
        acc = ref[0]
        for q in range(1, N_DEV):
            acc = acc + ref[q]
        return acc

    def body(*refs):
        p_refs, loss_in = refs[:n_in], refs[n_in]
        st = refs[n_in + 1:n_in + 1 + 3 * n_u]
        outs = refs[n_in + 1 + 3 * n_u:]
        g_refs, loss_out, upd = outs[:n_g], outs[n_g], outs[n_g + 1:]
        loss_out[...] = slot_sum(loss_in)
        side_by_side, off, sums = slot_sum(p_refs[0]), 0, {}
        for n, width in vector_widths.items():
            sums[n] = side_by_side[:, off:off + width]
            off += width
        for i, n in enumerate(gathered):
            sums[n] = slot_sum(p_refs[1 + i])
        for i, n in enumerate(names):
            g_refs[i][...] = sums[n]
            if n in state:
                j = updated.index(n)
                d, nm, nv = _adam_update(st[3 * j][...], sums[n], st[3 * j + 1][...], st[3 * j + 2][...])
                upd[3 * j][...], upd[3 * j + 1][...], upd[3 * j + 2][...] = d, nm, nv

    g_shapes = [jax.ShapeDtypeStruct((DEPTH, w), F32) for w in vector_widths.values()]
    g_shapes += [jax.ShapeDtypeStruct(gathered[n].shape[1:], F32) for n in gathered]
    u_shapes = [jax.ShapeDtypeStruct(state[n][0].shape, F32) for n in updated for _ in range(3)]
    outs = pl.pallas_call(
        body, name="small_sum_adamw", out_shape=g_shapes + [jax.ShapeDtypeStruct((8, 128), F32)] + u_shapes,
        compiler_params=_params(),
    )(vectors, *gathered.values(), loss_slots, *[a for n in updated for a in state[n]])
    g = dict(zip(names, outs[:n_g]))
    upd = {n: tuple(outs[n_g + 1 + 3 * j:n_g + 4 + 3 * j]) for j, n in enumerate(updated)}
    return g, outs[n_g], upd


WEIGHT_ORDER = ("norm1_g", "w_in", "conv_dw_w", "conv_dw_b", "conv_ln_g", "conv_ln_b", "lru_conv_w", "lru_conv_b", "lru_wa",
                "lru_ba", "lru_wx", "lru_bx", "lru_lambda", "w_out", "norm2_g", "w_up", "w_down", "final_g")
BIG = ("w_in", "w_out", "w_up", "w_down")
SMALL_SHARDED = {"conv_dw_w": (DEPTH, CONV_TAPS, CONV_W), "lru_conv_w": (DEPTH, LRU_TAPS, LRU_W)}
SMALL_FULL = {
    "norm1_g": (DEPTH, D_MODEL), "conv_dw_w": (DEPTH, CONV_TAPS, CONV_W), "conv_dw_b": (DEPTH, CONV_W),
    "conv_ln_g": (DEPTH, CONV_W), "conv_ln_b": (DEPTH, CONV_W), "lru_conv_w": (DEPTH, LRU_TAPS, LRU_W),
    "lru_conv_b": (DEPTH, LRU_W), "lru_wa": (DEPTH, LRU_HEADS, HEAD_DIM, HEAD_DIM), "lru_ba": (DEPTH, LRU_W),
    "lru_wx": (DEPTH, LRU_HEADS, HEAD_DIM, HEAD_DIM), "lru_bx": (DEPTH, LRU_W), "lru_lambda": (DEPTH, LRU_W),
    "norm2_g": (DEPTH, D_MODEL), "final_g": (D_MODEL,),
}
SMALL_COLS = 128
FILTER_ROWS = 24
W_IN_SHARD = IN_COLS // N_DEV
W_OUT_SHARD = D_MODEL // N_DEV
FF_SHARD = D_FF // N_DEV


def _pack_rows(flat_parts, cols, rows):
    flat = jnp.concatenate(flat_parts)
    return jnp.pad(flat, (0, rows * cols - flat.shape[0])).reshape(rows, cols)


WEIGHT_GATHER = {
    "w_in": ((N_DEV, D_MODEL, W_IN_SHARD), _slot),
    "w_out": ((D_MODEL, D_MODEL), _row_block(W_OUT_SHARD)),
    "w_up": ((D_MODEL, D_FF), _col_block(FF_SHARD)),
    "w_down": ((D_FF, D_MODEL), _row_block(FF_SHARD)),
}
GATHER_HOSTS = {
    "inproj": (("w_up", 0),), "attn_fwd": (("w_out", 0), ("w_in", 1)), "lru_fwd": (("w_down", 0),),
    "outproj": (), "up": (("w_up", 1), ("w_out", 1)), "down": (("w_down", 1),),
}


def _weight_gather(local, items, with_filters=False):
    blocks = [local[n][l].astype(BF16) for n, l in items]
    shapes = [jax.ShapeDtypeStruct(WEIGHT_GATHER[n][0], BF16) for n, _ in items]
    places = [WEIGHT_GATHER[n][1] for n, _ in items]
    if with_filters:
        blocks.append(_pack_rows([local[n].reshape(-1) for n in SMALL_SHARDED], SMALL_COLS, FILTER_ROWS))
        shapes.append(jax.ShapeDtypeStruct((N_DEV, FILTER_ROWS, SMALL_COLS), F32))
        places.append(_slot)
    return _Gather(blocks, shapes, places)


def _keep_gathered(full, items, landed):
    for (n, l), arr in zip(items, landed):
        full[n][l] = arr.transpose(1, 0, 2).reshape(D_MODEL, IN_COLS) if n == "w_in" else arr


def _unpack_filters(slots):
    flat, off, out = slots.reshape(N_DEV, -1), 0, {}
    for n, shp in SMALL_SHARDED.items():
        shard = shp[:-1] + (shp[-1] // N_DEV,)
        size = int(np.prod(shard))
        out[n] = jnp.moveaxis(flat[:, off:off + size].reshape((N_DEV,) + shard), 0, -2).reshape(shp)
        off += size
    return out


def _grad_exchange(name, dw):
    if name == "w_in":
        return _GradExchange([dw.reshape(D_MODEL, N_DEV, W_IN_SHARD).transpose(1, 0, 2)], [_slot], [(D_MODEL, W_IN_SHARD)])
    if name == "w_out":
        return _GradExchange([dw], [_row_block(W_OUT_SHARD)], [(W_OUT_SHARD, D_MODEL)])
    return _GradExchange([dw], [_col_block(FF_SHARD)], [(D_MODEL, FF_SHARD)])


def _sum_adamw(parts, w, m, v, rows_per_step, transposed, name):
    _, R, C = parts[0].shape
    tr = rows_per_step
    steps = R // tr

    def slot_sum(p_ref):
        acc = p_ref[0].astype(F32)
        for q in range(1, N_DEV):
            acc = acc + p_ref[q].astype(F32)
        return acc

    def body(p0_ref, p1_ref, w_ref, m_ref, v_ref, g_ref, d_ref, nm_ref, nv_ref):
        gv = jnp.where(pl.program_id(0) == 0, slot_sum(p0_ref), slot_sum(p1_ref))
        if transposed:
            gv = gv.T
        g_ref[...] = gv
        d_ref[...], nm_ref[...], nv_ref[...] = _adam_update(w_ref[...], gv, m_ref[...], v_ref[...])

    if transposed:
        spec = pl.BlockSpec((None, C, tr), lambda l, i: (l, 0, i))
    else:
        spec = pl.BlockSpec((None, tr, C), lambda l, i: (l, i, 0))
    shape = jax.ShapeDtypeStruct(w.shape, F32)
    part0 = pl.BlockSpec((N_DEV, tr, C), lambda l, i: (0, jnp.where(l == 0, i, steps - 1), 0))
    part1 = pl.BlockSpec((N_DEV, tr, C), lambda l, i: (0, jnp.where(l == 1, i, 0), 0))
    return pl.pallas_call(
        body, name=name, grid=(DEPTH, steps),
        in_specs=[part0, part1, spec, spec, spec],
        out_specs=[spec] * 4, out_shape=[shape] * 4,
        compiler_params=_params("arbitrary", "arbitrary"),
    )(parts[0], parts[1], w, m, v)


def _block_diag(w):
    eye = jnp.eye(LRU_HEADS, dtype=bool)
    return jnp.where(eye[:, None, :, None], w[:, :, None, :], jnp.zeros((), w.dtype)).reshape(LRU_W, LRU_W)


def _diag_blocks(m):
    eye = jnp.eye(LRU_HEADS, dtype=bool)
    m4 = m.reshape(LRU_HEADS, HEAD_DIM, LRU_HEADS, HEAD_DIM)
    return jnp.sum(jnp.where(eye[:, None, :, None], m4, 0.0), axis=2)


def kernel(x, norm1_g, w_in, conv_dw_w, conv_dw_b, conv_ln_g, conv_ln_b, lru_conv_w, lru_conv_b, lru_wa, lru_ba, lru_wx, lru_bx, lru_lambda, w_out, norm2_g, w_up, w_down, final_g, loss_target, m_norm1_g, m_w_in, m_conv_dw_w, m_conv_dw_b, m_conv_ln_g, m_conv_ln_b, m_lru_conv_w, m_lru_conv_b, m_lru_wa, m_lru_ba, m_lru_wx, m_lru_bx, m_lru_lambda, m_w_out, m_norm2_g, m_w_up, m_w_down, m_final_g, v_norm1_g, v_w_in, v_conv_dw_w, v_conv_dw_b, v_conv_ln_g, v_conv_ln_b, v_lru_conv_w, v_lru_conv_b, v_lru_wa, v_lru_ba, v_lru_wx, v_lru_bx, v_lru_lambda, v_w_out, v_norm2_g, v_w_up, v_w_down, v_final_g):
    local = dict(zip(WEIGHT_ORDER, (norm1_g, w_in, conv_dw_w, conv_dw_b, conv_ln_g, conv_ln_b, lru_conv_w, lru_conv_b, lru_wa,
                                    lru_ba, lru_wx, lru_bx, lru_lambda, w_out, norm2_g, w_up, w_down, final_g)))
    mom1 = dict(zip(WEIGHT_ORDER, (m_norm1_g, m_w_in, m_conv_dw_w, m_conv_dw_b, m_conv_ln_g, m_conv_ln_b, m_lru_conv_w,
                                   m_lru_conv_b, m_lru_wa, m_lru_ba, m_lru_wx, m_lru_bx, m_lru_lambda, m_w_out, m_norm2_g,
                                   m_w_up, m_w_down, m_final_g)))
    mom2 = dict(zip(WEIGHT_ORDER, (v_norm1_g, v_w_in, v_conv_dw_w, v_conv_dw_b, v_conv_ln_g, v_conv_ln_b, v_lru_conv_w,
                                   v_lru_conv_b, v_lru_wa, v_lru_ba, v_lru_wx, v_lru_bx, v_lru_lambda, v_w_out, v_norm2_g,
                                   v_w_up, v_w_down, v_final_g)))
    B, S, _ = x.shape
    T = B * S
    my_slot = 4 * lax.axis_index("x") + 2 * lax.axis_index("y") + lax.axis_index("c")
    row = lambda a: a.reshape(1, -1)

    full = {n: [None] * DEPTH for n in BIG}
    first_items = (("w_in", 0),)
    landed = _run_exchange(_weight_gather(local, first_items, with_filters=True), "gather_first_weights")
    _keep_gathered(full, first_items, landed)
    full.update(_unpack_filters(landed[-1]))

    def hosted(call, layer, fn, *args):
        items = GATHER_HOSTS[call] if layer == 0 else ()
        outs, landed = fn(*args, ex=_weight_gather(local, items) if items else None)
        _keep_gathered(full, items, landed or ())
        return outs

    saved = []
    cur = x.reshape(T, D_MODEL)
    for l in range(DEPTH):
        h, qkv, ci, li = hosted("inproj", l, _inproj, cur, row(norm1_g[l]), full["w_in"][l])
        qkv = qkv.reshape(B, S, QKV_W)
        o, lse = hosted("attn_fwd", l, _attn_fwd, qkv)
        o, lse = o.reshape(T, ATTN_W), lse.reshape(T, ATTN_W)
        ci = ci.reshape(B, S, CONV_IN_W)
        li = li.reshape(B, S, LRU_IN_W)
        conv_p = (full["conv_dw_w"][l], row(conv_dw_b[l]), row(conv_ln_g[l]), row(conv_ln_b[l]))
        lru_p = (full["lru_conv_w"][l], row(lru_conv_b[l]), _block_diag(lru_wa[l]).astype(BF16), row(lru_ba[l]),
                 _block_diag(lru_wx[l]).astype(BF16), row(lru_bx[l]), row(lru_lambda[l]))
        yc, cpre = _conv_fwd(ci, *conv_p)
        yl, hs = hosted("lru_fwd", l, _lru_fwd, li, *lru_p)
        x1, mix = hosted("outproj", l, _outproj, cur, o, lse, yc.reshape(T, CONV_W), yl.reshape(T, LRU_W), full["w_out"][l])
        h2, r = hosted("up", l, _up, x1, row(norm2_g[l]), full["w_up"][l])
        if l < DEPTH - 1:
            (x2,) = hosted("down", l, _down, x1, r, full["w_down"][l])
        else:
            x2, loss_part, dgf = _down_loss(x1, r, full["w_down"][l], loss_target.reshape(T, D_MODEL), row(final_g))
        saved.append(dict(x=cur, h=h, qkv=qkv, o=o, lse=lse, ci=ci, li=li, cpre=cpre, hs=hs, x1=x1, mix=mix, h2=h2, r=r,
                          conv_p=conv_p, lru_p=lru_p))
        cur = x2

    dx = cur

    received = {n: [None] * DEPTH for n in BIG}
    small_grads = {n: [None] * DEPTH for n in SMALL_FULL if n != "final_g"}
    for l in reversed(range(DEPTH)):
        sv = saved[l]
        (dpre,), _ = _down_bwd_act(dx, sv["r"], full["w_down"][l])
        dw_down = _down_bwd_w(sv["r"], dx)
        (dx1, dg2), _ = _up_bwd_act(dpre, full["w_up"][l], sv["x1"], row(norm2_g[l]), dx)
        dw_up = _up_bwd_w(sv["h2"], dpre)
        do, dd, dyc, dyl, dw_out = _outproj_bwd(dx1, sv["mix"], full["w_out"][l], sv["o"], sv["lse"])
        seq = lambda a: a.reshape(B, S, ATTN_W)
        (dq, dk, dv), (received["w_down"][l],) = _attn_bwd(sv["qkv"], seq(do), seq(sv["lse"]), seq(dd),
                                                           ex=_grad_exchange("w_down", dw_down))
        (dci, dcw, dcb, dlg, dlb), (received["w_out"][l],) = _conv_bwd(
            sv["ci"], sv["cpre"], dyc.reshape(B, S, CONV_W), sv["conv_p"][0], sv["conv_p"][2], sv["conv_p"][3],
            ex=_grad_exchange("w_out", dw_out))
        (dli, dlcw, dlcb, dwa, dba, dwx, dbx, dlam), (received["w_up"][l],) = _lru_bwd(
            sv["li"], sv["hs"], dyl.reshape(B, S, LRU_W), *sv["lru_p"], ex=_grad_exchange("w_up", dw_up))
        dz = tuple(t.reshape(T, -1) for t in (dq, dk, dv, dci, dli))
        dw_in = _inproj_bwd_w(dz, sv["h"])
        (dx, dg1), (received["w_in"][l],) = _inproj_bwd_act(dz, full["w_in"][l], sv["x"], row(norm1_g[l]), dx1,
                                                            ex=_grad_exchange("w_in", dw_in))
        for n, g in (("norm1_g", dg1), ("conv_dw_w", dcw), ("conv_dw_b", dcb), ("conv_ln_g", dlg), ("conv_ln_b", dlb),
                     ("lru_conv_w", dlcw), ("lru_conv_b", dlcb), ("lru_wa", _diag_blocks(dwa)), ("lru_ba", dba),
                     ("lru_wx", _diag_blocks(dwx)), ("lru_bx", dbx), ("lru_lambda", dlam), ("norm2_g", dg2)):
            small_grads[n][l] = g.reshape(SMALL_FULL[n][1:])
    grad_x = dx.reshape(B, S, D_MODEL)

    two_d = lambda n: (int(np.prod(SMALL_FULL[n][:-1])), SMALL_FULL[n][-1])
    small_local = {n: jnp.stack(g).reshape(two_d(n)) for n, g in small_grads.items()}
    small_local["final_g"] = dgf
    vector_widths = {n: shp[1] for n, shp in SMALL_FULL.items() if len(shp) == 2}
    other_names = [n for n in SMALL_FULL if n not in vector_widths]
    vectors = jnp.concatenate([small_local[n] for n in vector_widths], axis=1)
    blocks = [vectors] + [small_local[n] for n in other_names] + [loss_part]
    landed = _run_exchange(
        _Gather(blocks, [jax.ShapeDtypeStruct((N_DEV,) + b.shape, F32) for b in blocks], [_slot] * len(blocks)),
        "gather_small_grads")
    gathered = dict(zip(other_names, landed[1:-1]))

    grads, delta, new_m, new_v = {}, {}, {}, {}
    for n, rows_per_step in (("w_in", 256), ("w_out", W_OUT_SHARD), ("w_up", 256), ("w_down", 256)):
        grads[n], delta[n], new_m[n], new_v[n] = _sum_adamw(received[n], local[n], mom1[n], mom2[n], rows_per_step,
                                                            n == "w_down", "sum_adamw_" + n)

    replicated = [n for n in SMALL_FULL if n not in SMALL_SHARDED]
    state = {n: tuple(src[n].reshape(two_d(n)) for src in (local, mom1, mom2)) for n in replicated}
    small_g, loss_sum, updated = _small_sum_adamw(gathered, landed[0], vector_widths, landed[-1], state)
    loss = loss_sum[0, 0]
    for n in replicated:
        grads[n] = small_g[n].reshape(SMALL_FULL[n])
        delta[n], new_m[n], new_v[n] = (t.reshape(SMALL_FULL[n]) for t in updated[n])
    for n, fullshape in SMALL_SHARDED.items():
        width = fullshape[-1] // N_DEV
        g = lax.dynamic_slice_in_dim(small_g[n], my_slot * width, width, axis=1)
        d, nm, nv = _adamw(local[n].reshape(g.shape), g, mom1[n].reshape(g.shape), mom2[n].reshape(g.shape), g.shape[0],
                           "adamw_" + n)
        grads[n], delta[n], new_m[n], new_v[n] = (t.reshape(local[n].shape) for t in (g, d, nm, nv))

    return (loss, grad_x, *[grads[n] for n in WEIGHT_ORDER], *[delta[n] for n in WEIGHT_ORDER],
            *[new_m[n] for n in WEIGHT_ORDER], *[new_v[n] for n in WEIGHT_ORDER])
```

```python
import functools
import math

import numpy as np
import jax
import jax.numpy as jnp
from jax import lax
from jax.experimental import pallas as pl
from jax.experimental.pallas import tpu as pltpu

F32 = jnp.float32
BF16 = jnp.bfloat16

D_MODEL = 1024
SEQ_LEN = 2048
HEAD_DIM = 64
ATTN_W = 384
CONV_W = 256
CONV_TAPS = 31
LRU_W = 384
LRU_TAPS = 4
LRU_HEADS = 6
LRU_C = 8.0
QKV_W = 3 * ATTN_W
CONV_IN_W = 2 * CONV_W
LRU_IN_W = 2 * LRU_W
IN_COLS = QKV_W + CONV_IN_W + LRU_IN_W
D_FF = 4096
DEPTH = 2
N_DEV = 8
RMS_EPS = 1e-6
LN_EPS = 1e-5
ATTN_BLOCK = 128
ATTN_DILATIONS = (1, 4, 16)
N_UNITS = 16
UNIT_UNROLL = 8
NEG_BIG = -1e30

ADAM_LR = 0.001
ADAM_B1 = 0.9
ADAM_B2 = 0.999
ADAM_EPS = 1e-08
ADAM_WD = 0.01
ADAM_STEP = 10

VMEM_LIMIT = 56 * 1024 * 1024
ROW_TILE = 512
GRAD_ROW_TILE = 1024
CONV_FWD_CHUNK = 128
CONV_BWD_CHUNK = 128
LRU_FWD_CHUNK = 128
LRU_BWD_CHUNK = 128


def _params(*sem):
    return pltpu.CompilerParams(dimension_semantics=sem if sem else None, vmem_limit_bytes=VMEM_LIMIT)


def _resident(shape):
    return pl.BlockSpec(shape, lambda *_: (0,) * len(shape), pipeline_mode=pl.Buffered(1))


def _dot(a, b):
    return jnp.dot(a, b, preferred_element_type=F32)


def _dot_nt(a, b):
    return lax.dot_general(a, b, (((1,), (1,)), ((), ())), preferred_element_type=F32)


def _dot_tn(a, b):
    return lax.dot_general(a, b, (((0,), (0,)), ((), ())), preferred_element_type=F32)


def _rms_fwd(x, g):
    rstd = lax.rsqrt(jnp.mean(x * x, axis=-1, keepdims=True) + RMS_EPS)
    xhat = x * rstd
    return xhat * g, xhat, rstd


def _rms_bwd(dh, xhat, rstd, g):
    dxh = dh * g
    dx = rstd * (dxh - xhat * jnp.mean(dxh * xhat, axis=-1, keepdims=True))
    dg = jnp.sum(dh * xhat, axis=0, keepdims=True)
    return dx, dg


def _sigmoid(x):
    return 0.5 * jnp.tanh(0.5 * x) + 0.5


def _one_minus_exp(x, exp_x):
    small = -x * (1.0 + x * (0.5 + x * (1.0 / 6.0)))
    return jnp.where(x > -0.01, small, 1.0 - exp_x)


def _log1p(z):
    w = 1.0 + z
    return jnp.where(w == 1.0, z, z * jnp.log(w) / jnp.where(w == 1.0, 1.0, w - 1.0))


def _softplus(x):
    return jnp.maximum(x, 0.0) + _log1p(jnp.exp(-jnp.abs(x)))


GELU_K = math.sqrt(2.0 / math.pi)


def _gelu(x):
    t = jnp.tanh(GELU_K * (x + 0.044715 * x * x * x))
    return 0.5 * x * (1.0 + t), t


def _gelu_grad(x, t):
    return 0.5 * (1.0 + t) + 0.5 * x * (1.0 - t * t) * GELU_K * (1.0 + 3.0 * 0.044715 * x * x)


def _inproj(x2d, g, w, ex=None):
    T = x2d.shape[0]
    tm = ROW_TILE

    def body(x_ref, g_ref, w_ref, h_ref, qkv_ref, ci_ref, li_ref):
        h, _, _ = _rms_fwd(x_ref[...], g_ref[...])
        hb = h.astype(BF16)
        h_ref[...] = hb
        qkv_ref[...] = _dot(hb, w_ref[:, 0:QKV_W])
        ci_ref[...] = _dot(hb, w_ref[:, QKV_W:QKV_W + CONV_IN_W])
        li_ref[...] = _dot(hb, w_ref[:, QKV_W + CONV_IN_W:IN_COLS])

    return _pallas_hosting(
        body, ex, name="inproj", grid=(T // tm,),
        in_specs=[pl.BlockSpec((tm, D_MODEL), lambda i: (i, 0)),
                  pl.BlockSpec((1, D_MODEL), lambda i: (0, 0)),
                  _resident((D_MODEL, IN_COLS))],
        out_specs=[pl.BlockSpec((tm, D_MODEL), lambda i: (i, 0)),
                   pl.BlockSpec((tm, QKV_W), lambda i: (i, 0)),
                   pl.BlockSpec((tm, CONV_IN_W), lambda i: (i, 0)),
                   pl.BlockSpec((tm, LRU_IN_W), lambda i: (i, 0))],
        out_shape=[jax.ShapeDtypeStruct((T, D_MODEL), BF16), jax.ShapeDtypeStruct((T, QKV_W), F32),
                   jax.ShapeDtypeStruct((T, CONV_IN_W), F32), jax.ShapeDtypeStruct((T, LRU_IN_W), F32)],
        scratch_shapes=[], semantics=("parallel",), operands=(x2d, g, w))


def _attn_alpha(lse):
    l0, l1, l2 = lse[:, 0:128], lse[:, 128:256], lse[:, 256:384]
    m = jnp.maximum(jnp.maximum(l0, l1), l2)
    e0, e1, e2 = jnp.exp(l0 - m), jnp.exp(l1 - m), jnp.exp(l2 - m)
    inv = 1.0 / (e0 + e1 + e2)
    return e0 * inv, e1 * inv, e2 * inv


def _outproj(x2d, o, lse, yc, yl, w, ex=None):
    T = x2d.shape[0]
    tm = ROW_TILE

    def body(x_ref, o_ref, lse_ref, yc_ref, yl_ref, w_ref, x1_ref, mix_ref):
        al = _attn_alpha(lse_ref[...])
        for p in range(3):
            mix_ref[:, p * 128:(p + 1) * 128] = (o_ref[:, p * 128:(p + 1) * 128] * al[p]).astype(BF16)
        mix_ref[:, ATTN_W:ATTN_W + CONV_W] = yc_ref[...]
        mix_ref[:, ATTN_W + CONV_W:D_MODEL] = yl_ref[...]
        x1_ref[...] = x_ref[...] + _dot(mix_ref[...], w_ref[...])

    return _pallas_hosting(
        body, ex, name="outproj", grid=(T // tm,),
        in_specs=[pl.BlockSpec((tm, D_MODEL), lambda i: (i, 0)),
                  pl.BlockSpec((tm, ATTN_W), lambda i: (i, 0)),
                  pl.BlockSpec((tm, ATTN_W), lambda i: (i, 0)),
                  pl.BlockSpec((tm, CONV_W), lambda i: (i, 0)),
                  pl.BlockSpec((tm, LRU_W), lambda i: (i, 0)),
                  _resident((D_MODEL, D_MODEL))],
        out_specs=[pl.BlockSpec((tm, D_MODEL), lambda i: (i, 0)),
                   pl.BlockSpec((tm, D_MODEL), lambda i: (i, 0))],
        out_shape=[jax.ShapeDtypeStruct((T, D_MODEL), F32), jax.ShapeDtypeStruct((T, D_MODEL), BF16)],
        scratch_shapes=[], semantics=("parallel",), operands=(x2d, o, lse, yc, yl, w))


FF_CHUNK = 1024


def _up(x1, g, w, ex=None):
    T = x1.shape[0]
    tm = ROW_TILE

    def body(x_ref, g_ref, w_ref, h_ref, r_ref):
        h, _, _ = _rms_fwd(x_ref[...], g_ref[...])
        hb = h.astype(BF16)
        h_ref[...] = hb
        for c in range(0, D_FF, FF_CHUNK):
            r_ref[:, c:c + FF_CHUNK] = jnp.maximum(_dot(hb, w_ref[:, c:c + FF_CHUNK]), 0.0).astype(BF16)

    return _pallas_hosting(
        body, ex, name="up", grid=(T // tm,),
        in_specs=[pl.BlockSpec((tm, D_MODEL), lambda i: (i, 0)),
                  pl.BlockSpec((1, D_MODEL), lambda i: (0, 0)),
                  _resident((D_MODEL, D_FF))],
        out_specs=[pl.BlockSpec((tm, D_MODEL), lambda i: (i, 0)),
                   pl.BlockSpec((tm, D_FF), lambda i: (i, 0))],
        out_shape=[jax.ShapeDtypeStruct((T, D_MODEL), BF16), jax.ShapeDtypeStruct((T, D_FF), BF16)],
        scratch_shapes=[], semantics=("parallel",), operands=(x1, g, w))


def _square_bf16(r):
    rf = r.astype(F32)
    return (rf * rf).astype(BF16)


def _down(x1, r, w, ex=None):
    T = x1.shape[0]
    tm = ROW_TILE

    def body(x_ref, r_ref, w_ref, o_ref):
        acc = x_ref[...]
        for c in range(0, D_FF, FF_CHUNK):
            acc = acc + _dot(_square_bf16(r_ref[:, c:c + FF_CHUNK]), w_ref[c:c + FF_CHUNK, :])
        o_ref[...] = acc

    return _pallas_hosting(
        body, ex, name="down", grid=(T // tm,),
        in_specs=[pl.BlockSpec((tm, D_MODEL), lambda i: (i, 0)),
                  pl.BlockSpec((tm, D_FF), lambda i: (i, 0)),
                  _resident((D_FF, D_MODEL))],
        out_specs=[pl.BlockSpec((tm, D_MODEL), lambda i: (i, 0))],
        out_shape=[jax.ShapeDtypeStruct((T, D_MODEL), F32)],
        scratch_shapes=[], semantics=("parallel",), operands=(x1, r, w))


def _down_loss(x1, r, w, target, g):
    T = x1.shape[0]
    tm = ROW_TILE

    def body(x_ref, r_ref, w_ref, t_ref, g_ref, dx_ref, loss_ref, dg_ref):
        @pl.when(pl.program_id(0) == 0)
        def _():
            loss_ref[...] = jnp.zeros_like(loss_ref)
            dg_ref[...] = jnp.zeros_like(dg_ref)

        x2 = x_ref[...]
        for c in range(0, D_FF, FF_CHUNK):
            x2 = x2 + _dot(_square_bf16(r_ref[:, c:c + FF_CHUNK]), w_ref[c:c + FF_CHUNK, :])
        gv = g_ref[...]
        y, xhat, rstd = _rms_fwd(x2, gv)
        err = y - t_ref[...]
        loss_ref[...] += 0.5 * jnp.sum(jnp.mean(err * err, axis=-1, keepdims=True))
        dy = err * (1.0 / D_MODEL)
        dx, dg = _rms_bwd(dy, xhat, rstd, gv)
        dx_ref[...] = dx
        dg_ref[...] += dg

    rows = pl.BlockSpec((tm, D_MODEL), lambda i: (i, 0))
    return pl.pallas_call(
        body, name="down_loss", grid=(T // tm,),
        in_specs=[rows, pl.BlockSpec((tm, D_FF), lambda i: (i, 0)), _resident((D_FF, D_MODEL)), rows,
                  pl.BlockSpec((1, D_MODEL), lambda i: (0, 0))],
        out_specs=[rows, pl.BlockSpec((8, 128), lambda i: (0, 0)), pl.BlockSpec((1, D_MODEL), lambda i: (0, 0))],
        out_shape=[jax.ShapeDtypeStruct((T, D_MODEL), F32), jax.ShapeDtypeStruct((8, 128), F32),
                   jax.ShapeDtypeStruct((1, D_MODEL), F32)],
        compiler_params=_params("arbitrary"),
    )(x1, r, w, target, g)


def _down_bwd_act(dx2, r, w, ex=None):
    T = dx2.shape[0]
    tm = ROW_TILE

    def body(dx_ref, r_ref, w_ref, o_ref):
        dxb = dx_ref[...].astype(BF16)
        for c in range(0, D_FF, FF_CHUNK):
            dff = _dot_nt(dxb, w_ref[c:c + FF_CHUNK, :])
            o_ref[:, c:c + FF_CHUNK] = (dff * (2.0 * r_ref[:, c:c + FF_CHUNK].astype(F32))).astype(BF16)

    return _pallas_hosting(
        body, ex, name="down_bwd_act", grid=(T // tm,),
        in_specs=[pl.BlockSpec((tm, D_MODEL), lambda i: (i, 0)),
                  pl.BlockSpec((tm, D_FF), lambda i: (i, 0)),
                  _resident((D_FF, D_MODEL))],
        out_specs=[pl.BlockSpec((tm, D_FF), lambda i: (i, 0))],
        out_shape=[jax.ShapeDtypeStruct((T, D_FF), BF16)],
        scratch_shapes=[], semantics=("parallel",), operands=(dx2, r, w))


def _down_bwd_w(r, dx2):
    T = dx2.shape[0]
    tk = GRAD_ROW_TILE
    nk = T // tk

    def body(r_ref, dx_ref, o_ref, acc_ref):
        k = pl.program_id(0)
        dxb = dx_ref[...].astype(BF16)

        @pl.when(k == 0)
        def _():
            acc_ref[...] = jnp.zeros_like(acc_ref)

        for c in range(0, D_FF, FF_CHUNK):
            acc_ref[:, c:c + FF_CHUNK] += _dot_tn(dxb, _square_bf16(r_ref[:, c:c + FF_CHUNK]))

        @pl.when(k == nk - 1)
        def _():
            o_ref[...] = acc_ref[...].astype(BF16)

    return pl.pallas_call(
        body, name="down_bwd_w", grid=(nk,),
        in_specs=[pl.BlockSpec((tk, D_FF), lambda k: (k, 0)),
                  pl.BlockSpec((tk, D_MODEL), lambda k: (k, 0))],
        out_specs=_resident((D_MODEL, D_FF)),
        out_shape=jax.ShapeDtypeStruct((D_MODEL, D_FF), BF16),
        scratch_shapes=[pltpu.VMEM((D_MODEL, D_FF), F32)],
        compiler_params=_params("arbitrary"),
    )(r, dx2)


def _up_bwd_act(dpre, w, x1, g, dx2, ex=None):
    T = dx2.shape[0]
    tm = ROW_TILE

    def body(dp_ref, w_ref, x_ref, g_ref, dx2_ref, dx1_ref, dg_ref):
        dh = _dot_nt(dp_ref[:, 0:FF_CHUNK], w_ref[:, 0:FF_CHUNK])
        for c in range(FF_CHUNK, D_FF, FF_CHUNK):
            dh = dh + _dot_nt(dp_ref[:, c:c + FF_CHUNK], w_ref[:, c:c + FF_CHUNK])
        gv = g_ref[...]
        _, xhat, rstd = _rms_fwd(x_ref[...], gv)
        dx, dg = _rms_bwd(dh, xhat, rstd, gv)
        dx1_ref[...] = dx2_ref[...] + dx

        @pl.when(pl.program_id(0) == 0)
        def _():
            dg_ref[...] = dg

        @pl.when(pl.program_id(0) != 0)
        def _():
            dg_ref[...] += dg

    return _pallas_hosting(
        body, ex, name="up_bwd_act", grid=(T // tm,),
        in_specs=[pl.BlockSpec((tm, D_FF), lambda i: (i, 0)),
                  _resident((D_MODEL, D_FF)),
                  pl.BlockSpec((tm, D_MODEL), lambda i: (i, 0)),
                  pl.BlockSpec((1, D_MODEL), lambda i: (0, 0)),
                  pl.BlockSpec((tm, D_MODEL), lambda i: (i, 0))],
        out_specs=[pl.BlockSpec((tm, D_MODEL), lambda i: (i, 0)),
                   pl.BlockSpec((1, D_MODEL), lambda i: (0, 0))],
        out_shape=[jax.ShapeDtypeStruct((T, D_MODEL), F32), jax.ShapeDtypeStruct((1, D_MODEL), F32)],
        scratch_shapes=[], semantics=("arbitrary",), operands=(dpre, w, x1, g, dx2))


def _up_bwd_w(h2, dpre):
    T = h2.shape[0]
    tk = GRAD_ROW_TILE
    nk = T // tk

    def body(h_ref, dp_ref, o_ref, acc_ref):
        k = pl.program_id(0)
        hb = h_ref[...]

        @pl.when(k == 0)
        def _():
            acc_ref[...] = jnp.zeros_like(acc_ref)

        for c in range(0, D_FF, FF_CHUNK):
            acc_ref[:, c:c + FF_CHUNK] += _dot_tn(hb, dp_ref[:, c:c + FF_CHUNK])

        @pl.when(k == nk - 1)
        def _():
            o_ref[...] = acc_ref[...].astype(BF16)

    return pl.pallas_call(
        body, name="up_bwd_w", grid=(nk,),
        in_specs=[pl.BlockSpec((tk, D_MODEL), lambda k: (k, 0)),
                  pl.BlockSpec((tk, D_FF), lambda k: (k, 0))],
        out_specs=_resident((D_MODEL, D_FF)),
        out_shape=jax.ShapeDtypeStruct((D_MODEL, D_FF), BF16),
        scratch_shapes=[pltpu.VMEM((D_MODEL, D_FF), F32)],
        compiler_params=_params("arbitrary"),
    )(h2, dpre)


def _outproj_bwd(dx1, mix, w, o, lse):
    T = dx1.shape[0]
    tm = GRAD_ROW_TILE
    nk = T // tm

    def body(dx_ref, mix_ref, w_ref, o_ref, lse_ref, do_ref, dd_ref, dc_ref, dl_ref, dw_ref, acc_ref):
        i = pl.program_id(0)
        dxb = dx_ref[...].astype(BF16)
        dmix = _dot_nt(dxb, w_ref[...])
        al = _attn_alpha(lse_ref[...])
        first = lax.broadcasted_iota(jnp.int32, (tm, 128), 1) < HEAD_DIM
        tot = jnp.zeros((tm, 128), F32)
        for p in range(3):
            sl = slice(p * 128, (p + 1) * 128)
            dy = dmix[:, sl]
            do_ref[:, sl] = dy * al[p]
            prod = dy * o_ref[:, sl]
            s0 = jnp.sum(jnp.where(first, prod, 0.0), axis=-1, keepdims=True)
            s1 = jnp.sum(jnp.where(first, 0.0, prod), axis=-1, keepdims=True)
            tot = tot + al[p] * jnp.where(first, s0, s1)
        for p in range(3):
            dd_ref[:, p * 128:(p + 1) * 128] = -al[p] * tot
        dc_ref[...] = dmix[:, ATTN_W:ATTN_W + CONV_W]
        dl_ref[...] = dmix[:, ATTN_W + CONV_W:D_MODEL]
        part = _dot_tn(mix_ref[...], dxb)

        @pl.when(i == 0)
        def _():
            acc_ref[...] = part

        @pl.when(i != 0)
        def _():
            acc_ref[...] += part

        @pl.when(i == nk - 1)
        def _():
            dw_ref[...] = acc_ref[...].astype(BF16)

    return pl.pallas_call(
        body, name="outproj_bwd", grid=(nk,),
        in_specs=[pl.BlockSpec((tm, D_MODEL), lambda i: (i, 0)),
                  pl.BlockSpec((tm, D_MODEL), lambda i: (i, 0)),
                  _resident((D_MODEL, D_MODEL)),
                  pl.BlockSpec((tm, ATTN_W), lambda i: (i, 0)),
                  pl.BlockSpec((tm, ATTN_W), lambda i: (i, 0))],
        out_specs=[pl.BlockSpec((tm, ATTN_W), lambda i: (i, 0)),
                   pl.BlockSpec((tm, ATTN_W), lambda i: (i, 0)),
                   pl.BlockSpec((tm, CONV_W), lambda i: (i, 0)),
                   pl.BlockSpec((tm, LRU_W), lambda i: (i, 0)),
                   _resident((D_MODEL, D_MODEL))],
        out_shape=[jax.ShapeDtypeStruct((T, ATTN_W), F32), jax.ShapeDtypeStruct((T, ATTN_W), F32),
                   jax.ShapeDtypeStruct((T, CONV_W), F32), jax.ShapeDtypeStruct((T, LRU_W), F32),
                   jax.ShapeDtypeStruct((D_MODEL, D_MODEL), BF16)],
        scratch_shapes=[pltpu.VMEM((D_MODEL, D_MODEL), F32)],
        compiler_params=_params("arbitrary"),
    )(dx1, mix, w, o, lse)


DZ_COLS = ((0, ATTN_W), (ATTN_W, 2 * ATTN_W), (2 * ATTN_W, QKV_W), (QKV_W, QKV_W + CONV_IN_W), (QKV_W + CONV_IN_W, IN_COLS))


def _inproj_bwd_w(dz_parts, h):
    T = h.shape[0]
    tm = GRAD_ROW_TILE
    nk = T // tm
    n_parts = len(DZ_COLS)

    def body(*refs):
        dz_refs = refs[:n_parts]
        h_ref, dw_ref, acc_ref = refs[n_parts:]
        i = pl.program_id(0)
        hb = h_ref[...]

        @pl.when(i == 0)
        def _():
            acc_ref[...] = jnp.zeros_like(acc_ref)

        for r, (lo, hi) in zip(dz_refs, DZ_COLS):
            acc_ref[:, lo:hi] += _dot_tn(hb, r[...].astype(BF16))

        @pl.when(i == nk - 1)
        def _():
            dw_ref[...] = acc_ref[...].astype(BF16)

    rows = lambda width: pl.BlockSpec((tm, width), lambda i: (i, 0))
    return pl.pallas_call(
        body, name="inproj_bwd_w", grid=(nk,),
        in_specs=[rows(hi - lo) for lo, hi in DZ_COLS] + [rows(D_MODEL)],
        out_specs=_resident((D_MODEL, IN_COLS)),
        out_shape=jax.ShapeDtypeStruct((D_MODEL, IN_COLS), BF16),
        scratch_shapes=[pltpu.VMEM((D_MODEL, IN_COLS), F32)],
        compiler_params=_params("arbitrary"),
    )(*dz_parts, h)


def _inproj_bwd_act(dz_parts, w, x2d, g, dx1, ex=None):
    T = x2d.shape[0]
    tm = ROW_TILE
    n_parts = len(DZ_COLS)

    def body(*refs):
        dz_refs = refs[:n_parts]
        w_ref, x_ref, g_ref, dx1_ref, dx_ref, dg_ref = refs[n_parts:]
        dh = _dot_nt(dz_refs[0][...].astype(BF16), w_ref[:, DZ_COLS[0][0]:DZ_COLS[0][1]])
        for r, (lo, hi) in zip(dz_refs[1:], DZ_COLS[1:]):
            dh = dh + _dot_nt(r[...].astype(BF16), w_ref[:, lo:hi])
        gv = g_ref[...]
        _, xhat, rstd = _rms_fwd(x_ref[...], gv)
        dx, dg = _rms_bwd(dh, xhat, rstd, gv)
        dx_ref[...] = dx1_ref[...] + dx

        @pl.when(pl.program_id(0) == 0)
        def _():
            dg_ref[...] = dg

        @pl.when(pl.program_id(0) != 0)
        def _():
            dg_ref[...] += dg

    rows = lambda width: pl.BlockSpec((tm, width), lambda i: (i, 0))
    return _pallas_hosting(
        body, ex, name="inproj_bwd_act", grid=(T // tm,),
        in_specs=[rows(hi - lo) for lo, hi in DZ_COLS] + [
            _resident((D_MODEL, IN_COLS)), rows(D_MODEL), pl.BlockSpec((1, D_MODEL), lambda i: (0, 0)), rows(D_MODEL)],
        out_specs=[rows(D_MODEL), pl.BlockSpec((1, D_MODEL), lambda i: (0, 0))],
        out_shape=[jax.ShapeDtypeStruct((T, D_MODEL), F32), jax.ShapeDtypeStruct((1, D_MODEL), F32)],
        scratch_shapes=[], semantics=("arbitrary",), operands=(*dz_parts, w, x2d, g, dx1))


def _alibi_coef():
    slopes = 2.0 ** (-8.0 * np.arange(1, 7) / 6)
    return jnp.asarray((slopes.reshape(3, 2) * np.asarray(ATTN_DILATIONS)[:, None]).astype(np.float32))


def _unit_rows(u, d):
    nb = N_UNITS // d
    r, n = u // nb, u % nb
    span = ATTN_BLOCK * d

    def rows(block):
        start = block * span + r
        return pl.ds(pl.multiple_of(start, ATTN_BLOCK), ATTN_BLOCK) if d == 1 else pl.ds(start, ATTN_BLOCK, stride=d)

    return rows(n), rows(jnp.maximum(n - 1, 0)), rows(jnp.minimum(n + 1, nb - 1)), n > 0, n + 1 < nb


def _per_pattern(fn):
    for p, d in enumerate(ATTN_DILATIONS):
        pl.when(pl.program_id(1) == p)(functools.partial(fn, p, d))


def _attn_col(offset):
    return pl.BlockSpec((None, SEQ_LEN, 128), lambda b, p: (b, 0, p + offset))


def _attn_masks():
    qi = lax.broadcasted_iota(jnp.int32, (ATTN_BLOCK, 2 * ATTN_BLOCK), 0)
    kj = lax.broadcasted_iota(jnp.int32, (ATTN_BLOCK, 2 * ATTN_BLOCK), 1)
    dist = qi + ATTN_BLOCK - kj
    first = lax.broadcasted_iota(jnp.int32, (ATTN_BLOCK, 128), 1) < HEAD_DIM
    return dist.astype(F32), (dist >= 0) & (dist <= ATTN_BLOCK), kj >= ATTN_BLOCK, first


def _head_lanes(a, first, j):
    return jnp.where(first if j == 0 else jnp.logical_not(first), a, jnp.zeros_like(a))


def _load_kv(ref, prev, own):
    return jnp.concatenate([ref[prev, :], ref[own, :]], axis=0).astype(BF16)


def _attn_fwd(qkv, ex=None):
    B = qkv.shape[0]

    def body(coef_ref, q_ref, k_ref, v_ref, o_ref, lse_ref):
        dist, band, own_half, first = _attn_masks()

        def pattern(p, d):
            def unit(u, carry):
                own, prev, _, has_prev, _ = _unit_rows(u, d)
                ok = band & jnp.logical_or(own_half, has_prev)
                q = q_ref[own, :].astype(BF16)
                kcat, vcat = _load_kv(k_ref, prev, own), _load_kv(v_ref, prev, own)
                outs, lses = [], []
                for j in range(2):
                    s = jnp.where(ok, _dot_nt(_head_lanes(q, first, j), kcat) * 0.125 - coef_ref[p, j] * dist, NEG_BIG)
                    m = jnp.max(s, axis=-1, keepdims=True)
                    e = jnp.exp(s - m)
                    l = jnp.sum(e, axis=-1, keepdims=True)
                    outs.append(_dot(e.astype(BF16), vcat) * (1.0 / l))
                    lses.append(m + jnp.log(l))
                o_ref[own, :] = jnp.where(first, outs[0], outs[1])
                lse_ref[own, :] = jnp.where(first, lses[0], lses[1])
                return carry

            lax.fori_loop(0, N_UNITS, unit, 0, unroll=UNIT_UNROLL)

        _per_pattern(pattern)

    shape = jax.ShapeDtypeStruct((B, SEQ_LEN, ATTN_W), F32)
    return _pallas_hosting(
        body, ex, name="attn_fwd", grid=(B, 3),
        in_specs=[pl.BlockSpec(memory_space=pltpu.SMEM), _attn_col(0), _attn_col(3), _attn_col(6)],
        out_specs=[_attn_col(0), _attn_col(0)],
        out_shape=[shape, shape],
        scratch_shapes=[], semantics=("parallel", "parallel"), operands=(_alibi_coef(), qkv, qkv, qkv))


def _attn_bwd(qkv, do, lse, dd, ex=None):
    B = qkv.shape[0]

    def body(coef_ref, q_ref, k_ref, v_ref, do_ref, lse_ref, dd_ref, dq_ref, dk_ref, dv_ref):
        dist, band, own_half, first = _attn_masks()

        def pattern(p, d):
            def unit(u, carry):
                own, prev, _, has_prev, _ = _unit_rows(u, d)
                ok = band & jnp.logical_or(own_half, has_prev)
                q, do = q_ref[own, :].astype(BF16), do_ref[own, :].astype(BF16)
                kcat, vcat = _load_kv(k_ref, prev, own), _load_kv(v_ref, prev, own)
                lse_a, dd_a = lse_ref[own, :], dd_ref[own, :]
                dqs, dks, dvs = [], [], []
                for j in range(2):
                    col = slice(HEAD_DIM * j, HEAD_DIM * j + 1)
                    s = _dot_nt(_head_lanes(q, first, j), kcat) * 0.125 - coef_ref[p, j] * dist
                    pr = jnp.where(ok, jnp.exp(jnp.where(ok, s, NEG_BIG) - lse_a[:, col]), 0.0)
                    ds = (pr * (_dot_nt(_head_lanes(do, first, j), vcat) + dd_a[:, col])).astype(BF16)
                    dqs.append(_dot(ds, kcat))
                    dks.append(_dot_tn(ds, q))
                    dvs.append(_dot_tn(pr.astype(BF16), do))
                both = lambda pair: jnp.where(jnp.concatenate([first] * (pair[0].shape[0] // ATTN_BLOCK), axis=0), *pair)
                dq_ref[own, :] = both(dqs) * 0.125
                dk, dv = both(dks) * 0.125, both(dvs)
                dk_ref[own, :] = dk[ATTN_BLOCK:]
                dv_ref[own, :] = dv[ATTN_BLOCK:]
                dk_ref[prev, :] += dk[:ATTN_BLOCK]
                dv_ref[prev, :] += dv[:ATTN_BLOCK]
                return carry

            lax.fori_loop(0, N_UNITS, unit, 0, unroll=UNIT_UNROLL)

        _per_pattern(pattern)

    shape = jax.ShapeDtypeStruct((B, SEQ_LEN, ATTN_W), F32)
    return _pallas_hosting(
        body, ex, name="attn_bwd", grid=(B, 3),
        in_specs=[pl.BlockSpec(memory_space=pltpu.SMEM), _attn_col(0), _attn_col(3), _attn_col(6), _attn_col(0), _attn_col(0),
                  _attn_col(0)],
        out_specs=[_attn_col(0)] * 3,
        out_shape=[shape] * 3,
        scratch_shapes=[], semantics=("parallel", "parallel"), operands=(_alibi_coef(), qkv, qkv, qkv, do, lse, dd))


def _for_chunks(n_rows, fn, chunk):
    def step(c, carry):
        fn(pl.multiple_of(c * chunk, chunk))
        return carry

    lax.fori_loop(0, n_rows // chunk, step, 0)


def _shift_down(win, s, rows):
    lead = win.shape[0] - rows
    if s == 0:
        return win[lead:]
    if s % 8 == 0:
        return win[lead - s:lead - s + rows]
    q, r = divmod(s, 8)
    rolled = pltpu.roll(win, r, 0)
    return rolled[lead - 8 * q:lead - 8 * q + rows]


def _tap_shifts(n_taps):
    return [(r, [(n_taps - 1 - (8 * q + r), 8 * q) for q in range((n_taps - 1 - r) // 8 + 1)]) for r in range(min(8, n_taps))]


def _rotated_down(win, r):
    return win if r == 0 else pltpu.roll(win, r, 0)


def _rotated_up(win, r):
    return win if r == 0 else pltpu.roll(win, win.shape[0] - r, 0)


def _shift_up(win, s, rows):
    if s % 8 == 0:
        return win[s:s + rows]
    q, r = divmod(s, 8)
    rolled = pltpu.roll(win, win.shape[0] - r, 0)
    return rolled[8 * q:8 * q + rows]


CONV_PAD = 32
CONV_HALVES = (slice(0, 128), slice(128, CONV_W))


def _ln_silu(c, lg, lb):
    mu = jnp.mean(c, axis=-1, keepdims=True)
    cc = c - mu
    rstd = lax.rsqrt(jnp.mean(cc * cc, axis=-1, keepdims=True) + LN_EPS)
    nrm = cc * rstd
    v = nrm * lg + lb
    sg = _sigmoid(v)
    return v * sg, nrm, rstd, v, sg


def _conv_fwd(ci, w, b, lg, lb):
    B, S, _ = ci.shape
    CH = CONV_FWD_CHUNK

    def body(ci_ref, w_ref, b_ref, lg_ref, lb_ref, y_ref, c_ref, pad_ref):
        pad_ref[0:CONV_PAD, :] = jnp.zeros((CONV_PAD, CONV_W), F32)

        def glu(base):
            blk = ci_ref[pl.ds(base, CH), :]
            pad_ref[pl.ds(CONV_PAD + base, CH), :] = blk[:, 0:CONV_W] * _sigmoid(blk[:, CONV_W:])

        _for_chunks(S, glu, CH)

        def conv(base):
            for half in CONV_HALVES:
                win = pad_ref[pl.ds(base, CH + CONV_PAD), half]
                acc = jnp.broadcast_to(b_ref[:, half], (CH, 128))
                for r, taps in _tap_shifts(CONV_TAPS):
                    rot = _rotated_down(win, r)
                    for k, off in taps:
                        acc = acc + w_ref[k:k + 1, half] * rot[CONV_PAD - off:CONV_PAD - off + CH]
                c_ref[pl.ds(base, CH), half] = acc
            y, _, _, _, _ = _ln_silu(c_ref[pl.ds(base, CH), :], lg_ref[...], lb_ref[...])
            y_ref[pl.ds(base, CH), :] = y.astype(BF16)

        _for_chunks(S, conv, CH)

    vec = pl.BlockSpec((1, CONV_W), lambda i: (0, 0))
    return pl.pallas_call(
        body, name="conv_fwd", grid=(B,),
        in_specs=[pl.BlockSpec((None, S, CONV_IN_W), lambda i: (i, 0, 0)),
                  pl.BlockSpec((CONV_TAPS, CONV_W), lambda i: (0, 0)), vec, vec, vec],
        out_specs=[pl.BlockSpec((None, S, CONV_W), lambda i: (i, 0, 0)),
                   pl.BlockSpec((None, S, CONV_W), lambda i: (i, 0, 0))],
        out_shape=[jax.ShapeDtypeStruct((B, S, CONV_W), BF16), jax.ShapeDtypeStruct((B, S, CONV_W), F32)],
        scratch_shapes=[pltpu.VMEM((S + CONV_PAD, CONV_W), F32)],
        compiler_params=_params("parallel"),
    )(ci, w, b, lg, lb)


def _conv_bwd(ci, cpre, dy, w, lg, lb, ex=None):
    B, S, _ = ci.shape
    CH = CONV_BWD_CHUNK

    def body(ci_ref, c_ref, dy_ref, w_ref, lg_ref, lb_ref, dci_ref, dw_ref, db_ref, dlg_ref, dlb_ref, upad_ref, dcpad_ref,
             dwacc_ref):
        @pl.when(pl.program_id(0) == 0)
        def _():
            dw_ref[...] = jnp.zeros_like(dw_ref)
            db_ref[...] = jnp.zeros_like(db_ref)
            dlg_ref[...] = jnp.zeros_like(dlg_ref)
            dlb_ref[...] = jnp.zeros_like(dlb_ref)

        upad_ref[0:CONV_PAD, :] = jnp.zeros((CONV_PAD, CONV_W), F32)
        dcpad_ref[S:S + CONV_PAD, :] = jnp.zeros((CONV_PAD, CONV_W), F32)
        dwacc_ref[...] = jnp.zeros_like(dwacc_ref)

        def norm_bwd(base):
            blk = ci_ref[pl.ds(base, CH), :]
            upad_ref[pl.ds(CONV_PAD + base, CH), :] = blk[:, 0:CONV_W] * _sigmoid(blk[:, CONV_W:])
            lgv = lg_ref[...]
            _, nrm, rstd, v, sg = _ln_silu(c_ref[pl.ds(base, CH), :], lgv, lb_ref[...])
            dv = dy_ref[pl.ds(base, CH), :] * (sg * (1.0 + v * (1.0 - sg)))
            dlg_ref[...] += jnp.sum(dv * nrm, axis=0, keepdims=True)
            dlb_ref[...] += jnp.sum(dv, axis=0, keepdims=True)
            dn = dv * lgv
            dc = rstd * (dn - jnp.mean(dn, axis=-1, keepdims=True) - nrm * jnp.mean(dn * nrm, axis=-1, keepdims=True))
            dcpad_ref[pl.ds(base, CH), :] = dc
            db_ref[...] += jnp.sum(dc, axis=0, keepdims=True)

        _for_chunks(S, norm_bwd, CH)

        def conv_bwd(base):
            for half in CONV_HALVES:
                gate_half = slice(CONV_W + half.start, CONV_W + half.stop)
                dwin = dcpad_ref[pl.ds(base, CH + CONV_PAD), half]
                uwin = upad_ref[pl.ds(base, CH + CONV_PAD), half]
                dc = dwin[0:CH]
                du = jnp.zeros((CH, 128), F32)
                for r, taps in _tap_shifts(CONV_TAPS):
                    d_rot, u_rot = _rotated_up(dwin, r), _rotated_down(uwin, r)
                    for k, off in taps:
                        du = du + w_ref[k:k + 1, half] * d_rot[off:off + CH]
                        prod = dc * u_rot[CONV_PAD - off:CONV_PAD - off + CH]
                        dwacc_ref[8 * k:8 * k + 8, half] += jnp.sum(prod.reshape(CH // 8, 8, 128), axis=0)
                a, sg = ci_ref[pl.ds(base, CH), half], _sigmoid(ci_ref[pl.ds(base, CH), gate_half])
                dci_ref[pl.ds(base, CH), half] = (du * sg).astype(BF16)
                dci_ref[pl.ds(base, CH), gate_half] = (du * a * sg * (1.0 - sg)).astype(BF16)

        _for_chunks(S, conv_bwd, CH)
        for k in range(CONV_TAPS):
            dw_ref[k:k + 1, :] += jnp.sum(dwacc_ref[8 * k:8 * k + 8, :], axis=0, keepdims=True)

    vec = pl.BlockSpec((1, CONV_W), lambda i: (0, 0))
    mat = pl.BlockSpec((CONV_TAPS, CONV_W), lambda i: (0, 0))
    seq = lambda width: pl.BlockSpec((None, S, width), lambda i: (i, 0, 0))
    return _pallas_hosting(
        body, ex, name="conv_bwd", grid=(B,),
        in_specs=[seq(CONV_IN_W), seq(CONV_W), seq(CONV_W), mat, vec, vec],
        out_specs=[seq(CONV_IN_W), mat, vec, vec, vec],
        out_shape=[jax.ShapeDtypeStruct((B, S, CONV_IN_W), BF16), jax.ShapeDtypeStruct((CONV_TAPS, CONV_W), F32),
                   jax.ShapeDtypeStruct((1, CONV_W), F32), jax.ShapeDtypeStruct((1, CONV_W), F32),
                   jax.ShapeDtypeStruct((1, CONV_W), F32)],
        scratch_shapes=[pltpu.VMEM((S + CONV_PAD, CONV_W), F32), pltpu.VMEM((S + CONV_PAD, CONV_W), F32),
                        pltpu.VMEM((8 * CONV_TAPS, CONV_W), F32)],
        semantics=("arbitrary",), operands=(ci, cpre, dy, w, lg, lb))


SCAN_SHIFTS = tuple(1 << e for e in range(11))


def _prev8(ref, base, cols, fill):
    start = pl.multiple_of(jnp.maximum(base - 8, 0), 8)
    return jnp.where(base > 0, ref[pl.ds(start, 8), cols], fill)


def _next8(ref, base, rows, total, cols, fill):
    start = pl.multiple_of(jnp.minimum(base + rows, total - 8), 8)
    return jnp.where(base + rows < total, ref[pl.ds(start, 8), cols], fill)


ALL = slice(None)
LRU_X = slice(LRU_W, LRU_IN_W)
LRU_GATE = slice(0, LRU_W)


def _lru_conv(li_ref, base, rows, cw_ref, cb_ref):
    win = jnp.concatenate([_prev8(li_ref, base, LRU_X, 0.0), li_ref[pl.ds(base, rows), LRU_X]], axis=0)
    u = jnp.broadcast_to(cb_ref[...], (rows, LRU_W))
    for k in range(LRU_TAPS):
        u = u + cw_ref[k:k + 1, :] * _shift_down(win, LRU_TAPS - 1 - k, rows)
    return u, win


def _lru_gates(u, wa_ref, ba_ref, wx_ref, bx_ref, sp):
    ub = u.astype(BF16)
    r = _sigmoid(_dot(ub, wa_ref[...]) + ba_ref[...])
    i = _sigmoid(_dot(ub, wx_ref[...]) + bx_ref[...])
    la = (-LRU_C) * r * sp
    a = jnp.exp(la)
    return ub, r, i, a, _one_minus_exp(2.0 * la, a * a)


def _scan_forward(bufs, S, CH):
    for n, s in enumerate(SCAN_SHIFTS):
        (sa, sb), (da, db) = bufs[n % 2], bufs[(n + 1) % 2]

        def step(base, s=s, sa=sa, sb=sb, da=da, db=db):
            a, b = sa[pl.ds(base, CH), :], sb[pl.ds(base, CH), :]
            if s < 8:
                a_s = _shift_down(jnp.concatenate([_prev8(sa, base, ALL, 1.0), a], axis=0), s, CH)
                b_s = _shift_down(jnp.concatenate([_prev8(sb, base, ALL, 0.0), b], axis=0), s, CH)
            elif s < CH:
                start = pl.multiple_of(jnp.maximum(base - s, 0), 8)
                a_s = jnp.concatenate([jnp.where(base > 0, sa[pl.ds(start, s), :], 1.0), a[0:CH - s]], axis=0)
                b_s = jnp.concatenate([jnp.where(base > 0, sb[pl.ds(start, s), :], 0.0), b[0:CH - s]], axis=0)
            else:
                start = pl.multiple_of(jnp.maximum(base - s, 0), 8)
                a_s = jnp.where(base < s, 1.0, sa[pl.ds(start, CH), :])
                b_s = jnp.where(base < s, 0.0, sb[pl.ds(start, CH), :])
            db[pl.ds(base, CH), :] = a * b_s + b
            da[pl.ds(base, CH), :] = a * a_s

        _for_chunks(S, step, CH)
    return len(SCAN_SHIFTS) % 2


def _scan_backward(bufs, S, CH):
    for n, s in enumerate(SCAN_SHIFTS):
        (sa, sb), (da, db) = bufs[n % 2], bufs[(n + 1) % 2]

        def step(base, s=s, sa=sa, sb=sb, da=da, db=db):
            a, b = sa[pl.ds(base, CH), :], sb[pl.ds(base, CH), :]
            if s < 8:
                a_s = _shift_up(jnp.concatenate([a, _next8(sa, base, CH, S, ALL, 1.0)], axis=0), s, CH)
                b_s = _shift_up(jnp.concatenate([b, _next8(sb, base, CH, S, ALL, 0.0)], axis=0), s, CH)
            elif s < CH:
                start = pl.multiple_of(jnp.minimum(base + CH, S - s), 8)
                more = base + CH < S
                a_s = jnp.concatenate([a[s:CH], jnp.where(more, sa[pl.ds(start, s), :], 1.0)], axis=0)
                b_s = jnp.concatenate([b[s:CH], jnp.where(more, sb[pl.ds(start, s), :], 0.0)], axis=0)
            else:
                start = pl.multiple_of(jnp.minimum(base + s, S - CH), 8)
                a_s = jnp.where(base + s >= S, 1.0, sa[pl.ds(start, CH), :])
                b_s = jnp.where(base + s >= S, 0.0, sb[pl.ds(start, CH), :])
            db[pl.ds(base, CH), :] = a * b_s + b
            da[pl.ds(base, CH), :] = a * a_s

        _for_chunks(S, step, CH)
    return len(SCAN_SHIFTS) % 2


def _lru_fwd(li, cw, cb, wa, ba, wx, bx, lam, ex=None):
    B, S, _ = li.shape
    CH = LRU_FWD_CHUNK

    def body(li_ref, cw_ref, cb_ref, wa_ref, ba_ref, wx_ref, bx_ref, lam_ref, y_ref, h_ref, a0, b0, a1, b1):
        sp = _softplus(-lam_ref[...])

        def gates(base):
            u, _ = _lru_conv(li_ref, base, CH, cw_ref, cb_ref)
            _, _, i, a, em = _lru_gates(u, wa_ref, ba_ref, wx_ref, bx_ref, sp)
            a0[pl.ds(base, CH), :] = a
            b0[pl.ds(base, CH), :] = jnp.sqrt(em) * (i * u)

        _for_chunks(S, gates, CH)
        bufs = ((a0, b0), (a1, b1))
        hb = bufs[_scan_forward(bufs, S, CH)][1]

        def out(base):
            h = hb[pl.ds(base, CH), :]
            h_ref[pl.ds(base, CH), :] = h
            gl, _ = _gelu(li_ref[pl.ds(base, CH), LRU_GATE])
            y_ref[pl.ds(base, CH), :] = (gl * h).astype(BF16)

        _for_chunks(S, out, CH)

    vec = pl.BlockSpec((1, LRU_W), lambda i: (0, 0))
    mat = pl.BlockSpec((LRU_W, LRU_W), lambda i: (0, 0))
    seq = lambda width: pl.BlockSpec((None, S, width), lambda i: (i, 0, 0))
    return _pallas_hosting(
        body, ex, name="lru_fwd", grid=(B,),
        in_specs=[seq(LRU_IN_W), pl.BlockSpec((LRU_TAPS, LRU_W), lambda i: (0, 0)), vec, mat, vec, mat, vec, vec],
        out_specs=[seq(LRU_W), seq(LRU_W)],
        out_shape=[jax.ShapeDtypeStruct((B, S, LRU_W), BF16), jax.ShapeDtypeStruct((B, S, LRU_W), F32)],
        scratch_shapes=[pltpu.VMEM((S, LRU_W), F32)] * 4,
        semantics=("parallel",), operands=(li, cw, cb, wa, ba, wx, bx, lam))


def _lru_bwd(li, hs, dy, cw, cb, wa, ba, wx, bx, lam, ex=None):
    B, S, _ = li.shape
    CH = LRU_BWD_CHUNK

    def body(li_ref, hs_ref, dy_ref, cw_ref, cb_ref, wa_ref, ba_ref, wx_ref, bx_ref, lam_ref,
             dli_ref, dcw_ref, dcb_ref, dwa_ref, dba_ref, dwx_ref, dbx_ref, dlam_ref, a0, b0, a1, b1, u_s, du_s):
        @pl.when(pl.program_id(0) == 0)
        def _():
            for ref in (dcw_ref, dcb_ref, dwa_ref, dba_ref, dwx_ref, dbx_ref, dlam_ref):
                ref[...] = jnp.zeros_like(ref)

        lam_v = lam_ref[...]
        sp = _softplus(-lam_v)
        dsp_dlam = -_sigmoid(-lam_v)

        def gates(base):
            u, _ = _lru_conv(li_ref, base, CH, cw_ref, cb_ref)
            _, _, _, a, _ = _lru_gates(u, wa_ref, ba_ref, wx_ref, bx_ref, sp)
            gl, _ = _gelu(li_ref[pl.ds(base, CH), LRU_GATE])
            u_s[pl.ds(base, CH), :] = u
            a0[pl.ds(base, CH), :] = a
            b0[pl.ds(base, CH), :] = a * (dy_ref[pl.ds(base, CH), :] * gl)

        _for_chunks(S, gates, CH)
        bufs = ((a0, b0), (a1, b1))
        eb = bufs[_scan_backward(bufs, S, CH)][1]

        def grads(base):
            e = eb[pl.ds(base, CH), :]
            e_next = _shift_up(jnp.concatenate([e, _next8(eb, base, CH, S, ALL, 0.0)], axis=0), 1, CH)
            gate = li_ref[pl.ds(base, CH), LRU_GATE]
            gl, th = _gelu(gate)
            dy = dy_ref[pl.ds(base, CH), :]
            g = dy * gl + e_next
            h = hs_ref[pl.ds(base, CH), :]
            h_prev = _shift_down(jnp.concatenate([_prev8(hs_ref, base, ALL, 0.0), h], axis=0), 1, CH)
            dli_ref[pl.ds(base, CH), LRU_GATE] = (dy * h * _gelu_grad(gate, th)).astype(BF16)
            u = u_s[pl.ds(base, CH), :]
            ub, r, i, a, em = _lru_gates(u, wa_ref, ba_ref, wx_ref, bx_ref, sp)
            mult = jnp.sqrt(em)
            da = g * h_prev
            dmult = g * (i * u)
            di = g * mult * u
            dla = da * a - dmult * (a * a) * lax.rsqrt(jnp.maximum(em, 1e-30))
            dlam_ref[...] += dsp_dlam * jnp.sum(dla * ((-LRU_C) * r), axis=0, keepdims=True)
            dpa = (dla * ((-LRU_C) * sp)) * r * (1.0 - r)
            dpx = di * i * (1.0 - i)
            dpab, dpxb = dpa.astype(BF16), dpx.astype(BF16)
            dwa_ref[...] += _dot_tn(ub, dpab)
            dwx_ref[...] += _dot_tn(ub, dpxb)
            dba_ref[...] += jnp.sum(dpa, axis=0, keepdims=True)
            dbx_ref[...] += jnp.sum(dpx, axis=0, keepdims=True)
            du = g * mult * i + _dot_nt(dpab, wa_ref[...]) + _dot_nt(dpxb, wx_ref[...])
            du_s[pl.ds(base, CH), :] = du
            dcb_ref[...] += jnp.sum(du, axis=0, keepdims=True)

        _for_chunks(S, grads, CH)

        def conv_bwd(base):
            du = du_s[pl.ds(base, CH), :]
            dwin = jnp.concatenate([du, _next8(du_s, base, CH, S, ALL, 0.0)], axis=0)
            xwin = jnp.concatenate([_prev8(li_ref, base, LRU_X, 0.0), li_ref[pl.ds(base, CH), LRU_X]], axis=0)
            dx = jnp.zeros((CH, LRU_W), F32)
            for k in range(LRU_TAPS):
                dx = dx + cw_ref[k:k + 1, :] * _shift_up(dwin, LRU_TAPS - 1 - k, CH)
                dcw_ref[k:k + 1, :] += jnp.sum(du * _shift_down(xwin, LRU_TAPS - 1 - k, CH), axis=0, keepdims=True)
            dli_ref[pl.ds(base, CH), LRU_X] = dx.astype(BF16)

        _for_chunks(S, conv_bwd, CH)

    vec = pl.BlockSpec((1, LRU_W), lambda i: (0, 0))
    mat = pl.BlockSpec((LRU_W, LRU_W), lambda i: (0, 0))
    taps = pl.BlockSpec((LRU_TAPS, LRU_W), lambda i: (0, 0))
    seq = lambda width: pl.BlockSpec((None, S, width), lambda i: (i, 0, 0))
    vec_shape = jax.ShapeDtypeStruct((1, LRU_W), F32)
    mat_shape = jax.ShapeDtypeStruct((LRU_W, LRU_W), F32)
    return _pallas_hosting(
        body, ex, name="lru_bwd", grid=(B,),
        in_specs=[seq(LRU_IN_W), seq(LRU_W), seq(LRU_W), taps, vec, mat, vec, mat, vec, vec],
        out_specs=[seq(LRU_IN_W), taps, vec, mat, vec, mat, vec, vec],
        out_shape=[jax.ShapeDtypeStruct((B, S, LRU_IN_W), BF16), jax.ShapeDtypeStruct((LRU_TAPS, LRU_W), F32),
                   vec_shape, mat_shape, vec_shape, mat_shape, vec_shape, vec_shape],
        scratch_shapes=[pltpu.VMEM((S, LRU_W), F32)] * 6,
        semantics=("arbitrary",), operands=(li, hs, dy, cw, cb, wa, ba, wx, bx, lam))


MESH = pl.DeviceIdType.MESH
HBM_SPEC = pl.BlockSpec(memory_space=pltpu.HBM)


def _slot(ref, p):
    return ref.at[p]


def _row_block(rows):
    return lambda ref, p: ref.at[pl.ds(p * rows, rows), :]


def _col_block(cols):
    return lambda ref, p: ref.at[:, pl.ds(p * cols, cols)]


class _Gather:
    def __init__(self, blocks, out_shapes, places):
        self.sources, self.out_shapes, self.places, self.n = list(blocks), list(out_shapes), list(places), len(blocks)

    def scratch(self):
        return [pltpu.SemaphoreType.DMA((self.n, 7)), pltpu.SemaphoreType.DMA((self.n, 7)), pltpu.SemaphoreType.DMA((self.n,))]

    def _plan(self, x_refs, out_refs, send_sems, recv_sems, local_sems):
        n = self.n
        x, y, c = lax.axis_index("x"), lax.axis_index("y"), lax.axis_index("c")
        me, sibling = (x, y, c), (x, y, 1 - c)
        chips = [(1 - x, y), (x, 1 - y), (1 - x, 1 - y)]

        def place(a, dev):
            return self.places[a](out_refs[a], 4 * dev[0] + 2 * dev[1] + dev[2])

        def copy(a, k, blk, to, src=None):
            return pltpu.make_async_remote_copy(
                src_ref=place(a, blk) if src is None else src, dst_ref=place(a, blk),
                send_sem=send_sems.at[a, k], recv_sem=recv_sems.at[a, k], device_id=to, device_id_type=MESH)

        mine = [pltpu.make_async_copy(x_refs[a], place(a, me), local_sems.at[a]) for a in range(n)]
        first = [copy(a, 0, me, sibling, src=x_refs[a]) for a in range(n)]
        first += [copy(a, 1 + j, me, (*chip, c), src=x_refs[a]) for j, chip in enumerate(chips) for a in range(n)]
        return me, sibling, chips, c, copy, mine, first

    def start(self, *refs):
        *_, mine, first = self._plan(*refs)
        for cp in mine + first:
            cp.start()

    def forward(self, *refs):
        me, sibling, chips, c, copy, _, _ = self._plan(*refs)
        for j, chip in enumerate(chips):
            for a in range(self.n):
                copy(a, 1 + j, (*chip, c), me).wait_recv()
                copy(a, 4 + j, (*chip, c), sibling).start()

    def finish(self, *refs):
        me, sibling, chips, c, copy, mine, first = self._plan(*refs)
        passed = [copy(a, 4 + j, (*chip, c), sibling) for j, chip in enumerate(chips) for a in range(self.n)]
        for a in range(self.n):
            copy(a, 0, sibling, me).wait_recv()
        for j, chip in enumerate(chips):
            for a in range(self.n):
                copy(a, 4 + j, (*chip, 1 - c), me).wait_recv()
        for cp in first + passed:
            cp.wait_send()
        for cp in mine:
            cp.wait()


class _GradExchange:
    def __init__(self, sources, takes, piece_shapes):
        self.sources, self.takes, self.n = list(sources), list(takes), len(sources)
        self.out_shapes = [jax.ShapeDtypeStruct((N_DEV,) + tuple(s), BF16) for s in piece_shapes]

    def scratch(self):
        return [pltpu.SemaphoreType.DMA((self.n, 7)), pltpu.SemaphoreType.DMA((self.n, 7)), pltpu.SemaphoreType.DMA((self.n,))]

    def _copies(self, src_refs, out_refs, send_sems, recv_sems, local_sems):
        x, y, c = lax.axis_index("x"), lax.axis_index("y"), lax.axis_index("c")
        me = 4 * x + 2 * y + c
        mine = [pltpu.make_async_copy(self.takes[i](src_refs[i], me), out_refs[i].at[me], local_sems.at[i]) for i in range(self.n)]
        remote = []
        for k in range(1, N_DEV):
            px, py, pc = x ^ ((k >> 2) & 1), y ^ ((k >> 1) & 1), c ^ (k & 1)
            peer = 4 * px + 2 * py + pc
            for i in range(self.n):
                remote.append(pltpu.make_async_remote_copy(
                    src_ref=self.takes[i](src_refs[i], peer), dst_ref=out_refs[i].at[me], send_sem=send_sems.at[i, k - 1],
                    recv_sem=recv_sems.at[i, k - 1], device_id=(px, py, pc), device_id_type=MESH))
        return mine, remote

    def start(self, *refs):
        mine, remote = self._copies(*refs)
        for cp in mine + remote:
            cp.start()

    def forward(self, *refs):
        pass

    def finish(self, *refs):
        mine, remote = self._copies(*refs)
        for cp in remote:
            cp.wait_recv()
        for cp in remote:
            cp.wait_send()
        for cp in mine:
            cp.wait()


def _run_exchange(ex, name):
    def body(*refs):
        src_refs, out_refs, sems = refs[:ex.n], refs[ex.n:2 * ex.n], refs[2 * ex.n:]
        ex.start(src_refs, out_refs, *sems)
        ex.forward(src_refs, out_refs, *sems)
        ex.finish(src_refs, out_refs, *sems)

    return pl.pallas_call(
        body, name=name, out_shape=ex.out_shapes, in_specs=[HBM_SPEC] * ex.n, out_specs=[HBM_SPEC] * ex.n,
        scratch_shapes=ex.scratch(),
    )(*ex.sources)


def _pallas_hosting(body, ex, *, name, grid, in_specs, out_specs, out_shape, scratch_shapes, semantics, operands):
    if ex is None:
        outs = pl.pallas_call(body, name=name, grid=grid, in_specs=in_specs, out_specs=out_specs, out_shape=out_shape,
                              scratch_shapes=scratch_shapes, compiler_params=_params(*semantics))(*operands)
        return outs, None
    n_in, n_out, n_scr, n = len(in_specs), len(out_specs), len(scratch_shapes), ex.n

    def at_step(pick):
        conds = [pl.program_id(k) == pick(size) for k, size in enumerate(grid)]
        return functools.reduce(jnp.logical_and, conds)

    def hosting(*refs):
        ins, ex_ins = refs[:n_in], refs[n_in:n_in + n]
        outs, ex_outs = refs[n_in + n:n_in + n + n_out], refs[n_in + n + n_out:n_in + 2 * n + n_out]
        scratch, sems = refs[n_in + 2 * n + n_out:n_in + 2 * n + n_out + n_scr], refs[n_in + 2 * n + n_out + n_scr:]
        pl.when(at_step(lambda size: 0))(lambda: ex.start(ex_ins, ex_outs, *sems))
        pl.when(at_step(lambda size: size - 1))(lambda: ex.forward(ex_ins, ex_outs, *sems))
        body(*ins, *outs, *scratch)
        pl.when(at_step(lambda size: size - 1))(lambda: ex.finish(ex_ins, ex_outs, *sems))

    res = pl.pallas_call(
        hosting, name=name, grid=grid, in_specs=list(in_specs) + [HBM_SPEC] * n, out_specs=list(out_specs) + [HBM_SPEC] * n,
        out_shape=list(out_shape) + ex.out_shapes, scratch_shapes=list(scratch_shapes) + ex.scratch(),
        compiler_params=_params(*(["arbitrary"] * len(grid))),
    )(*operands, *ex.sources)
    return res[:n_out], res[n_out:]


def _adamw(w, g, m, v, rows_per_step, name):
    R, C = w.shape

    def body(w_ref, g_ref, m_ref, v_ref, d_ref, nm_ref, nv_ref):
        d_ref[...], nm_ref[...], nv_ref[...] = _adam_update(w_ref[...], g_ref[...], m_ref[...], v_ref[...])

    spec = pl.BlockSpec((rows_per_step, C), lambda i: (i, 0))
    shape = jax.ShapeDtypeStruct((R, C), F32)
    return pl.pallas_call(
        body, name=name, grid=(R // rows_per_step,),
        in_specs=[spec] * 4, out_specs=[spec] * 3, out_shape=[shape] * 3,
        compiler_params=_params("parallel"),
    )(w, g, m, v)


def _adam_update(w, g, m, v):
    nm = ADAM_B1 * m + (1.0 - ADAM_B1) * g
    nv = ADAM_B2 * v + (1.0 - ADAM_B2) * (g * g)
    c1 = 1.0 - ADAM_B1 ** ADAM_STEP
    c2 = 1.0 - ADAM_B2 ** ADAM_STEP
    return (-ADAM_LR) * ((nm / c1) / (jnp.sqrt(nv / c2) + ADAM_EPS) + ADAM_WD * w), nm, nv


def _small_sum_adamw(gathered, vectors, vector_widths, loss_slots, state):
    names, updated = list(vector_widths) + list(gathered), list(state)
    n_in, n_g, n_u = len(gathered) + 1, len(names), len(updated)

    def slot_sum(ref):
        acc = ref[0]
        for q in range(1, N_DEV):
            acc = acc + ref[q]
        return acc

    def body(*refs):
        p_refs, loss_in = refs[:n_in], refs[n_in]
        st = refs[n_in + 1:n_in + 1 + 3 * n_u]
        outs = refs[n_in + 1 + 3 * n_u:]
        g_refs, loss_out, upd = outs[:n_g], outs[n_g], outs[n_g + 1:]
        loss_out[...] = slot_sum(loss_in)
        side_by_side, off, sums = slot_sum(p_refs[0]), 0, {}
        for n, width in vector_widths.items():
            sums[n] = side_by_side[:, off:off + width]
            off += width
        for i, n in enumerate(gathered):
            sums[n] = slot_sum(p_refs[1 + i])
        for i, n in enumerate(names):
            g_refs[i][...] = sums[n]
            if n in state:
                j = updated.index(n)
                d, nm, nv = _adam_update(st[3 * j][...], sums[n], st[3 * j + 1][...], st[3 * j + 2][...])
                upd[3 * j][...], upd[3 * j + 1][...], upd[3 * j + 2][...] = d, nm, nv

    g_shapes = [jax.ShapeDtypeStruct((DEPTH, w), F32) for w in vector_widths.values()]
    g_shapes += [jax.ShapeDtypeStruct(gathered[n].shape[1:], F32) for n in gathered]
    u_shapes = [jax.ShapeDtypeStruct(state[n][0].shape, F32) for n in updated for _ in range(3)]
    outs = pl.pallas_call(
        body, name="small_sum_adamw", out_shape=g_shapes + [jax.ShapeDtypeStruct((8, 128), F32)] + u_shapes,
        compiler_params=_params(),
    )(vectors, *gathered.values(), loss_slots, *[a for n in updated for a in state[n]])
    g = dict(zip(names, outs[:n_g]))
    upd = {n: tuple(outs[n_g + 1 + 3 * j:n_g + 4 + 3 * j]) for j, n in enumerate(updated)}
    return g, outs[n_g], upd


WEIGHT_ORDER = ("norm1_g", "w_in", "conv_dw_w", "conv_dw_b", "conv_ln_g", "conv_ln_b", "lru_conv_w", "lru_conv_b", "lru_wa",
                "lru_ba", "lru_wx", "lru_bx", "lru_lambda", "w_out", "norm2_g", "w_up", "w_down", "final_g")
BIG = ("w_in", "w_out", "w_up", "w_down")
SMALL_SHARDED = {"conv_dw_w": (DEPTH, CONV_TAPS, CONV_W), "lru_conv_w": (DEPTH, LRU_TAPS, LRU_W)}
SMALL_FULL = {
    "norm1_g": (DEPTH, D_MODEL), "conv_dw_w": (DEPTH, CONV_TAPS, CONV_W), "conv_dw_b": (DEPTH, CONV_W),
    "conv_ln_g": (DEPTH, CONV_W), "conv_ln_b": (DEPTH, CONV_W), "lru_conv_w": (DEPTH, LRU_TAPS, LRU_W),
    "lru_conv_b": (DEPTH, LRU_W), "lru_wa": (DEPTH, LRU_HEADS, HEAD_DIM, HEAD_DIM), "lru_ba": (DEPTH, LRU_W),
    "lru_wx": (DEPTH, LRU_HEADS, HEAD_DIM, HEAD_DIM), "lru_bx": (DEPTH, LRU_W), "lru_lambda": (DEPTH, LRU_W),
    "norm2_g": (DEPTH, D_MODEL), "final_g": (D_MODEL,),
}
SMALL_COLS = 128
FILTER_ROWS = 24
W_IN_SHARD = IN_COLS // N_DEV
W_OUT_SHARD = D_MODEL // N_DEV
FF_SHARD = D_FF // N_DEV


def _pack_rows(flat_parts, cols, rows):
    flat = jnp.concatenate(flat_parts)
    return jnp.pad(flat, (0, rows * cols - flat.shape[0])).reshape(rows, cols)


WEIGHT_GATHER = {
    "w_in": ((N_DEV, D_MODEL, W_IN_SHARD), _slot),
    "w_out": ((D_MODEL, D_MODEL), _row_block(W_OUT_SHARD)),
    "w_up": ((D_MODEL, D_FF), _col_block(FF_SHARD)),
    "w_down": ((D_FF, D_MODEL), _row_block(FF_SHARD)),
}
GATHER_HOSTS = {
    "inproj": (("w_up", 0),), "attn_fwd": (("w_out", 0), ("w_in", 1)), "lru_fwd": (("w_down", 0),),
    "outproj": (), "up": (("w_up", 1), ("w_out", 1)), "down": (("w_down", 1),),
}


def _weight_gather(local, items, with_filters=False):
    blocks = [local[n][l].astype(BF16) for n, l in items]
    shapes = [jax.ShapeDtypeStruct(WEIGHT_GATHER[n][0], BF16) for n, _ in items]
    places = [WEIGHT_GATHER[n][1] for n, _ in items]
    if with_filters:
        blocks.append(_pack_rows([local[n].reshape(-1) for n in SMALL_SHARDED], SMALL_COLS, FILTER_ROWS))
        shapes.append(jax.ShapeDtypeStruct((N_DEV, FILTER_ROWS, SMALL_COLS), F32))
        places.append(_slot)
    return _Gather(blocks, shapes, places)


def _keep_gathered(full, items, landed):
    for (n, l), arr in zip(items, landed):
        full[n][l] = arr.transpose(1, 0, 2).reshape(D_MODEL, IN_COLS) if n == "w_in" else arr


def _unpack_filters(slots):
    flat, off, out = slots.reshape(N_DEV, -1), 0, {}
    for n, shp in SMALL_SHARDED.items():
        shard = shp[:-1] + (shp[-1] // N_DEV,)
        size = int(np.prod(shard))
        out[n] = jnp.moveaxis(flat[:, off:off + size].reshape((N_DEV,) + shard), 0, -2).reshape(shp)
        off += size
    return out


def _grad_exchange(name, dw):
    if name == "w_in":
        return _GradExchange([dw.reshape(D_MODEL, N_DEV, W_IN_SHARD).transpose(1, 0, 2)], [_slot], [(D_MODEL, W_IN_SHARD)])
    if name == "w_out":
        return _GradExchange([dw], [_row_block(W_OUT_SHARD)], [(W_OUT_SHARD, D_MODEL)])
    return _GradExchange([dw], [_col_block(FF_SHARD)], [(D_MODEL, FF_SHARD)])


def _sum_adamw(parts, w, m, v, rows_per_step, transposed, name):
    _, R, C = parts[0].shape
    tr = rows_per_step
    steps = R // tr

    def slot_sum(p_ref):
        acc = p_ref[0].astype(F32)
        for q in range(1, N_DEV):
            acc = acc + p_ref[q].astype(F32)
        return acc

    def body(p0_ref, p1_ref, w_ref, m_ref, v_ref, g_ref, d_ref, nm_ref, nv_ref):
        gv = jnp.where(pl.program_id(0) == 0, slot_sum(p0_ref), slot_sum(p1_ref))
        if transposed:
            gv = gv.T
        g_ref[...] = gv
        d_ref[...], nm_ref[...], nv_ref[...] = _adam_update(w_ref[...], gv, m_ref[...], v_ref[...])

    if transposed:
        spec = pl.BlockSpec((None, C, tr), lambda l, i: (l, 0, i))
    else:
        spec = pl.BlockSpec((None, tr, C), lambda l, i: (l, i, 0))
    shape = jax.ShapeDtypeStruct(w.shape, F32)
    part0 = pl.BlockSpec((N_DEV, tr, C), lambda l, i: (0, jnp.where(l == 0, i, steps - 1), 0))
    part1 = pl.BlockSpec((N_DEV, tr, C), lambda l, i: (0, jnp.where(l == 1, i, 0), 0))
    return pl.pallas_call(
        body, name=name, grid=(DEPTH, steps),
        in_specs=[part0, part1, spec, spec, spec],
        out_specs=[spec] * 4, out_shape=[shape] * 4,
        compiler_params=_params("arbitrary", "arbitrary"),
    )(parts[0], parts[1], w, m, v)


def _block_diag(w):
    eye = jnp.eye(LRU_HEADS, dtype=bool)
    return jnp.where(eye[:, None, :, None], w[:, :, None, :], jnp.zeros((), w.dtype)).reshape(LRU_W, LRU_W)


def _diag_blocks(m):
    eye = jnp.eye(LRU_HEADS, dtype=bool)
    m4 = m.reshape(LRU_HEADS, HEAD_DIM, LRU_HEADS, HEAD_DIM)
    return jnp.sum(jnp.where(eye[:, None, :, None], m4, 0.0), axis=2)


def kernel(x, norm1_g, w_in, conv_dw_w, conv_dw_b, conv_ln_g, conv_ln_b, lru_conv_w, lru_conv_b, lru_wa, lru_ba, lru_wx, lru_bx, lru_lambda, w_out, norm2_g, w_up, w_down, final_g, loss_target, m_norm1_g, m_w_in, m_conv_dw_w, m_conv_dw_b, m_conv_ln_g, m_conv_ln_b, m_lru_conv_w, m_lru_conv_b, m_lru_wa, m_lru_ba, m_lru_wx, m_lru_bx, m_lru_lambda, m_w_out, m_norm2_g, m_w_up, m_w_down, m_final_g, v_norm1_g, v_w_in, v_conv_dw_w, v_conv_dw_b, v_conv_ln_g, v_conv_ln_b, v_lru_conv_w, v_lru_conv_b, v_lru_wa, v_lru_ba, v_lru_wx, v_lru_bx, v_lru_lambda, v_w_out, v_norm2_g, v_w_up, v_w_down, v_final_g):
    local = dict(zip(WEIGHT_ORDER, (norm1_g, w_in, conv_dw_w, conv_dw_b, conv_ln_g, conv_ln_b, lru_conv_w, lru_conv_b, lru_wa,
                                    lru_ba, lru_wx, lru_bx, lru_lambda, w_out, norm2_g, w_up, w_down, final_g)))
    mom1 = dict(zip(WEIGHT_ORDER, (m_norm1_g, m_w_in, m_conv_dw_w, m_conv_dw_b, m_conv_ln_g, m_conv_ln_b, m_lru_conv_w,
                                   m_lru_conv_b, m_lru_wa, m_lru_ba, m_lru_wx, m_lru_bx, m_lru_lambda, m_w_out, m_norm2_g,
                                   m_w_up, m_w_down, m_final_g)))
    mom2 = dict(zip(WEIGHT_ORDER, (v_norm1_g, v_w_in, v_conv_dw_w, v_conv_dw_b, v_conv_ln_g, v_conv_ln_b, v_lru_conv_w,
                                   v_lru_conv_b, v_lru_wa, v_lru_ba, v_lru_wx, v_lru_bx, v_lru_lambda, v_w_out, v_norm2_g,
                                   v_w_up, v_w_down, v_final_g)))
    B, S, _ = x.shape
    T = B * S
    my_slot = 4 * lax.axis_index("x") + 2 * lax.axis_index("y") + lax.axis_index("c")
    row = lambda a: a.reshape(1, -1)

    full = {n: [None] * DEPTH for n in BIG}
    first_items = (("w_in", 0),)
    landed = _run_exchange(_weight_gather(local, first_items, with_filters=True), "gather_first_weights")
    _keep_gathered(full, first_items, landed)
    full.update(_unpack_filters(landed[-1]))

    def hosted(call, layer, fn, *args):
        items = GATHER_HOSTS[call] if layer == 0 else ()
        outs, landed = fn(*args, ex=_weight_gather(local, items) if items else None)
        _keep_gathered(full, items, landed or ())
        return outs

    saved = []
    cur = x.reshape(T, D_MODEL)
    for l in range(DEPTH):
        h, qkv, ci, li = hosted("inproj", l, _inproj, cur, row(norm1_g[l]), full["w_in"][l])
        qkv = qkv.reshape(B, S, QKV_W)
        o, lse = hosted("attn_fwd", l, _attn_fwd, qkv)
        o, lse = o.reshape(T, ATTN_W), lse.reshape(T, ATTN_W)
        ci = ci.reshape(B, S, CONV_IN_W)
        li = li.reshape(B, S, LRU_IN_W)
        conv_p = (full["conv_dw_w"][l], row(conv_dw_b[l]), row(conv_ln_g[l]), row(conv_ln_b[l]))
        lru_p = (full["lru_conv_w"][l], row(lru_conv_b[l]), _block_diag(lru_wa[l]).astype(BF16), row(lru_ba[l]),
                 _block_diag(lru_wx[l]).astype(BF16), row(lru_bx[l]), row(lru_lambda[l]))
        yc, cpre = _conv_fwd(ci, *conv_p)
        yl, hs = hosted("lru_fwd", l, _lru_fwd, li, *lru_p)
        x1, mix = hosted("outproj", l, _outproj, cur, o, lse, yc.reshape(T, CONV_W), yl.reshape(T, LRU_W), full["w_out"][l])
        h2, r = hosted("up", l, _up, x1, row(norm2_g[l]), full["w_up"][l])
        if l < DEPTH - 1:
            (x2,) = hosted("down", l, _down, x1, r, full["w_down"][l])
        else:
            x2, loss_part, dgf = _down_loss(x1, r, full["w_down"][l], loss_target.reshape(T, D_MODEL), row(final_g))
        saved.append(dict(x=cur, h=h, qkv=qkv, o=o, lse=lse, ci=ci, li=li, cpre=cpre, hs=hs, x1=x1, mix=mix, h2=h2, r=r,
                          conv_p=conv_p, lru_p=lru_p))
        cur = x2

    dx = cur

    received = {n: [None] * DEPTH for n in BIG}
    small_grads = {n: [None] * DEPTH for n in SMALL_FULL if n != "final_g"}
    for l in reversed(range(DEPTH)):
        sv = saved[l]
        (dpre,), _ = _down_bwd_act(dx, sv["r"], full["w_down"][l])
        dw_down = _down_bwd_w(sv["r"], dx)
        (dx1, dg2), _ = _up_bwd_act(dpre, full["w_up"][l], sv["x1"], row(norm2_g[l]), dx)
        dw_up = _up_bwd_w(sv["h2"], dpre)
        do, dd, dyc, dyl, dw_out = _outproj_bwd(dx1, sv["mix"], full["w_out"][l], sv["o"], sv["lse"])
        seq = lambda a: a.reshape(B, S, ATTN_W)
        (dq, dk, dv), (received["w_down"][l],) = _attn_bwd(sv["qkv"], seq(do), seq(sv["lse"]), seq(dd),
                                                           ex=_grad_exchange("w_down", dw_down))
        (dci, dcw, dcb, dlg, dlb), (received["w_out"][l],) = _conv_bwd(
            sv["ci"], sv["cpre"], dyc.reshape(B, S, CONV_W), sv["conv_p"][0], sv["conv_p"][2], sv["conv_p"][3],
            ex=_grad_exchange("w_out", dw_out))
        (dli, dlcw, dlcb, dwa, dba, dwx, dbx, dlam), (received["w_up"][l],) = _lru_bwd(
            sv["li"], sv["hs"], dyl.reshape(B, S, LRU_W), *sv["lru_p"], ex=_grad_exchange("w_up", dw_up))
        dz = tuple(t.reshape(T, -1) for t in (dq, dk, dv, dci, dli))
        dw_in = _inproj_bwd_w(dz, sv["h"])
        (dx, dg1), (received["w_in"][l],) = _inproj_bwd_act(dz, full["w_in"][l], sv["x"], row(norm1_g[l]), dx1,
                                                            ex=_grad_exchange("w_in", dw_in))
        for n, g in (("norm1_g", dg1), ("conv_dw_w", dcw), ("conv_dw_b", dcb), ("conv_ln_g", dlg), ("conv_ln_b", dlb),
                     ("lru_conv_w", dlcw), ("lru_conv_b", dlcb), ("lru_wa", _diag_blocks(dwa)), ("lru_ba", dba),
                     ("lru_wx", _diag_blocks(dwx)), ("lru_bx", dbx), ("lru_lambda", dlam), ("norm2_g", dg2)):
            small_grads[n][l] = g.reshape(SMALL_FULL[n][1:])
    grad_x = dx.reshape(B, S, D_MODEL)

    two_d = lambda n: (int(np.prod(SMALL_FULL[n][:-1])), SMALL_FULL[n][-1])
    small_local = {n: jnp.stack(g).reshape(two_d(n)) for n, g in small_grads.items()}
    small_local["final_g"] = dgf
    vector_widths = {n: shp[1] for n, shp in SMALL_FULL.items() if len(shp) == 2}
    other_names = [n for n in SMALL_FULL if n not in vector_widths]
    vectors = jnp.concatenate([small_local[n] for n in vector_widths], axis=1)
    blocks = [vectors] + [small_local[n] for n in other_names] + [loss_part]
    landed = _run_exchange(
        _Gather(blocks, [jax.ShapeDtypeStruct((N_DEV,) + b.shape, F32) for b in blocks], [_slot] * len(blocks)),
        "gather_small_grads")
    gathered = dict(zip(other_names, landed[1:-1]))

    grads, delta, new_m, new_v = {}, {}, {}, {}
    for n, rows_per_step in (("w_in", 256), ("w_out", W_OUT_SHARD), ("w_up", 256), ("w_down", 256)):
        grads[n], delta[n], new_m[n], new_v[n] = _sum_adamw(received[n], local[n], mom1[n], mom2[n], rows_per_step,
                                                            n == "w_down", "sum_adamw_" + n)

    replicated = [n for n in SMALL_FULL if n not in SMALL_SHARDED]
    state = {n: tuple(src[n].reshape(two_d(n)) for src in (local, mom1, mom2)) for n in replicated}
    small_g, loss_sum, updated = _small_sum_adamw(gathered, landed[0], vector_widths, landed[-1], state)
    loss = loss_sum[0, 0]
    for n in replicated:
        grads[n] = small_g[n].reshape(SMALL_FULL[n])
        delta[n], new_m[n], new_v[n] = (t.reshape(SMALL_FULL[n]) for t in updated[n])
    for n, fullshape in SMALL_SHARDED.items():
        width = fullshape[-1] // N_DEV
        g = lax.dynamic_slice_in_dim(small_g[n], my_slot * width, width, axis=1)
        d, nm, nv = _adamw(local[n].reshape(g.shape), g, mom1[n].reshape(g.shape), mom2[n].reshape(g.shape), g.shape[0],
                           "adamw_" + n)
        grads[n], delta[n], new_m[n], new_v[n] = (t.reshape(local[n].shape) for t in (g, d, nm, nv))

    return (loss, grad_x, *[grads[n] for n in WEIGHT_ORDER], *[delta[n] for n in WEIGHT_ORDER],
            *[new_m[n] for n in WEIGHT_ORDER], *[new_v[n] for n in WEIGHT_ORDER])
```
